```python
import math
import jax, jax.numpy as jnp
from jax import lax
import numpy as np

D_MODEL = 1024
BATCH = 8
SEQ = 4096
DEPTH = 2

DN_HEADS = 8
DN_HEAD_DIM = 128
DN_WIDTH = DN_HEADS * DN_HEAD_DIM
DN_CONV = 4
DN_CHUNK = 64
DA_HEADS = 12
DA_HEAD_DIM = 64
DA_WIDTH = DA_HEADS * DA_HEAD_DIM
DA_PATTERNS = ((128, 1), (512, 4), (2048, 16))
DA_BLOCK = 128
ALIBI_MAX_EXP = 8.0
D_FF = 2816
MACARON_WEIGHT = 0.5
NORM_EPS = 1e-6
N_ADA = 9
IN_SPLITS = (3 * DN_WIDTH, DN_WIDTH, DN_HEADS, DN_HEADS,
             DA_WIDTH, DA_WIDTH, DA_WIDTH, D_MODEL, D_MODEL)
IN_COLS = 3 * DN_WIDTH + DN_WIDTH + 2 * DN_HEADS + 3 * DA_WIDTH + 2 * D_MODEL

kernel_name = "hybrid_deltanet_dilated_attn_macaron_adaln"


def rmsnorm(x, g):
    xf = x.astype(jnp.float32)
    y = xf * lax.rsqrt(jnp.mean(xf * xf, axis=-1, keepdims=True) + NORM_EPS)
    return (y * g.astype(jnp.float32)).astype(x.dtype)


def l2norm(x):
    xf = x.astype(jnp.float32)
    return xf * lax.rsqrt(jnp.sum(xf * xf, axis=-1, keepdims=True) + NORM_EPS)


def modulate(x, shift, scale):
    return x * (1.0 + scale[:, None, :]) + shift[:, None, :]


def swiglu(x, w_gate, w_up, w_down):
    return (jax.nn.silu(x @ w_gate) * (x @ w_up)) @ w_down


def causal_depthwise_conv(x, w):
    K = w.shape[0]
    S = x.shape[1]
    xp = jnp.pad(x, ((0, 0), (K - 1, 0), (0, 0)))
    y = xp[:, 0:S] * w[0]
    for j in range(1, K):
        y = y + xp[:, j:j + S] * w[j]
    return y


def gated_delta_rule(q, k, v, g, beta):
    B, S, H, dk = q.shape
    dv = v.shape[-1]
    C = DN_CHUNK
    N = S // C
    f32 = jnp.float32

    def chunks(t):
        t = t.astype(f32).reshape((B, N, C, H) + t.shape[3:])
        return t.transpose((1, 0, 3, 2) + tuple(range(4, t.ndim)))

    qc, kc, vc = chunks(q), chunks(k), chunks(v)
    gc = jnp.cumsum(chunks(g), axis=-1)
    bc = chunks(beta)
    kb = kc * bc[..., None]
    vb = vc * bc[..., None]
    incl = jnp.tril(jnp.ones((C, C), dtype=bool))
    strict = jnp.tril(jnp.ones((C, C), dtype=bool), -1)
    decay = jnp.exp(jnp.where(incl, gc[..., :, None] - gc[..., None, :], -jnp.inf))
    m = jnp.where(strict, jnp.einsum('nbhid,nbhjd->nbhij', kb, kc) * decay, 0.0)
    a = m + jnp.eye(C, dtype=f32)
    u_c = lax.linalg.triangular_solve(a, vb, left_side=True, lower=True, unit_diagonal=True)
    w_c = lax.linalg.triangular_solve(a, kb * jnp.exp(gc)[..., None], left_side=True,
                                      lower=True, unit_diagonal=True)
    qk = jnp.einsum('nbhid,nbhjd->nbhij', qc, kc) * decay

    def step(state, xs):
        q_i, k_i, u_i, w_i, g_i, qk_i = xs
        v_new = u_i - jnp.einsum('bhcd,bhde->bhce', w_i, state)
        o_i = (jnp.einsum('bhcd,bhde->bhce', q_i * jnp.exp(g_i)[..., None], state)
               + jnp.einsum('bhij,bhje->bhie', qk_i, v_new))
        g_last = g_i[..., -1]
        state = (state * jnp.exp(g_last)[..., None, None]
                 + jnp.einsum('bhcd,bhce->bhde',
                              k_i * jnp.exp(g_last[..., None] - g_i)[..., None], v_new))
        return state, o_i

    state0 = jnp.zeros((B, H, dk, dv), f32)
    _, o = lax.scan(step, state0, (qc, kc, u_c, w_c, gc, qk))
    return o.transpose(1, 0, 3, 2, 4).reshape(B, S, H, dv)


def dilated_window_branch(q, k, v, slopes, window, dilation):
    B, S, H, dh = q.shape
    r = dilation
    n = S // r
    span = window // r
    nb = -(-n // DA_BLOCK)
    n_pad = nb * DA_BLOCK
    z = B * r

    def to_sub(t):
        t = t.reshape(B, n, r, H, dh).transpose(0, 2, 1, 3, 4).reshape(z, n, H, dh)
        return jnp.pad(t, ((0, 0), (0, n_pad - n), (0, 0), (0, 0)))

    def band(t):
        tp = jnp.pad(t, ((0, 0), (DA_BLOCK, 0), (0, 0), (0, 0)))
        prev = tp[:, :n_pad].reshape(z, nb, DA_BLOCK, H, dh)
        cur = t.reshape(z, nb, DA_BLOCK, H, dh)
        return jnp.concatenate([prev, cur], axis=2)

    qs, ks, vs = to_sub(q), to_sub(k), to_sub(v)
    qb = qs.reshape(z, nb, DA_BLOCK, H, dh)
    kb, vb = band(ks), band(vs)
    s = jnp.einsum('znqhd,znkhd->znhqk', qb, kb).astype(jnp.float32) * (dh ** -0.5)
    qi = jnp.arange(DA_BLOCK)[:, None]
    ki = jnp.arange(2 * DA_BLOCK)[None, :]
    dist = qi + DA_BLOCK - ki
    key_pos = jnp.arange(nb)[:, None, None] * DA_BLOCK + ki[None] - DA_BLOCK
    valid = (dist[None] >= 0) & (dist[None] <= span) & (key_pos >= 0)
    bias = -slopes[:, None, None] * (dist * r).astype(jnp.float32)[None]
    s = jnp.where(valid[None, :, None], s + bias[None, None], -jnp.inf)
    mx = jnp.max(s, axis=-1, keepdims=True)
    p = jnp.exp(s - mx)
    l = jnp.sum(p, axis=-1, keepdims=True)
    o = jnp.einsum('znhqk,znkhd->znqhd', (p / l).astype(v.dtype), vb).astype(jnp.float32)
    lse = (mx + jnp.log(l))[..., 0]
    o = o.reshape(z, n_pad, H, dh)[:, :n].reshape(B, r, n, H, dh)
    o = o.transpose(0, 2, 1, 3, 4).reshape(B, S, H, dh)
    lse = lse.transpose(0, 1, 3, 2).reshape(z, n_pad, H)[:, :n].reshape(B, r, n, H)
    lse = lse.transpose(0, 2, 1, 3).reshape(B, S, H)
    return o, lse


def hybrid_mixer(u, w_in, conv_w, a_log, dt_bias, dn_norm, w_a, w_b, w_o):
    B, S, _ = u.shape
    proj = u @ w_in
    idx = np.cumsum(IN_SPLITS)[:-1].tolist()
    dn_qkv, dn_z, dn_b, dn_a, da_q, da_k, da_v, gate_a, gate_b = jnp.split(proj, idx, axis=-1)

    qkv = jax.nn.silu(causal_depthwise_conv(dn_qkv, conv_w))
    q, k, v = jnp.split(qkv, 3, axis=-1)
    q = l2norm(q.reshape(B, S, DN_HEADS, DN_HEAD_DIM)) * (DN_HEAD_DIM ** -0.5)
    k = l2norm(k.reshape(B, S, DN_HEADS, DN_HEAD_DIM))
    v = v.reshape(B, S, DN_HEADS, DN_HEAD_DIM)
    beta = jax.nn.sigmoid(dn_b.astype(jnp.float32))
    g = -jnp.exp(a_log.astype(jnp.float32)) * jax.nn.softplus(
        dn_a.astype(jnp.float32) + dt_bias.astype(jnp.float32))
    o_a = gated_delta_rule(q, k, v, g, beta).astype(u.dtype)
    o_a = rmsnorm(o_a, dn_norm) * jax.nn.silu(dn_z.reshape(B, S, DN_HEADS, DN_HEAD_DIM))
    y_a = o_a.reshape(B, S, DN_WIDTH) @ w_a

    qd = da_q.reshape(B, S, DA_HEADS, DA_HEAD_DIM)
    kd = da_k.reshape(B, S, DA_HEADS, DA_HEAD_DIM)
    vd = da_v.reshape(B, S, DA_HEADS, DA_HEAD_DIM)
    slopes = 2.0 ** (-ALIBI_MAX_EXP * jnp.arange(1, DA_HEADS + 1, dtype=jnp.float32) / DA_HEADS)
    outs, lses = [], []
    for window, dilation in DA_PATTERNS:
        o_p, lse_p = dilated_window_branch(qd, kd, vd, slopes, window, dilation)
        outs.append(o_p)
        lses.append(lse_p)
    wts = jax.nn.softmax(jnp.stack(lses, axis=0), axis=0)
    o_b = jnp.sum(wts[..., None] * jnp.stack(outs, axis=0), axis=0).astype(u.dtype)
    y_b = o_b.reshape(B, S, DA_WIDTH) @ w_b

    merged = jax.nn.sigmoid(gate_a) * y_a + jax.nn.sigmoid(gate_b) * y_b
    return merged @ w_o


def _fwd_setup_inputs(seed: int = 0) -> dict:
    key = jax.random.key(seed)
    ks = jax.random.split(key, 24)
    f32 = jnp.float32
    D = D_MODEL

    def nrm(k, shape, scale):
        return jax.random.normal(k, shape, f32) * scale

    x = nrm(ks[0], (BATCH, SEQ, D), 1.0)
    c = nrm(ks[1], (BATCH, D), 1.0)
    ada_w = nrm(ks[2], (DEPTH, D, N_ADA * D), 0.5 * D ** -0.5)
    ada_b = nrm(ks[3], (DEPTH, N_ADA * D), 0.02)
    ln_ffn1 = 1.0 + nrm(ks[4], (DEPTH, D), 0.02)
    ln_mix = 1.0 + nrm(ks[5], (DEPTH, D), 0.02)
    ln_ffn2 = 1.0 + nrm(ks[6], (DEPTH, D), 0.02)
    ffn1_wg = nrm(ks[7], (DEPTH, D, D_FF), D ** -0.5)
    ffn1_wu = nrm(ks[8], (DEPTH, D, D_FF), D ** -0.5)
    ffn1_wd = nrm(ks[9], (DEPTH, D_FF, D), D_FF ** -0.5)
    w_in = nrm(ks[10], (DEPTH, D, IN_COLS), D ** -0.5)
    conv_w = nrm(ks[11], (DEPTH, DN_CONV, 3 * DN_WIDTH), DN_CONV ** -0.5)
    a_log = jnp.log(jax.random.uniform(ks[12], (DEPTH, DN_HEADS), f32, 1.0, 16.0))
    dt = jnp.exp(jax.random.uniform(ks[13], (DEPTH, DN_HEADS), f32,
                                    math.log(1e-3), math.log(1e-1)))
    dt_bias = dt + jnp.log(-jnp.expm1(-dt))
    dn_norm = 1.0 + nrm(ks[14], (DEPTH, DN_HEAD_DIM), 0.02)
    w_a = nrm(ks[15], (DEPTH, DN_WIDTH, D), DN_WIDTH ** -0.5)
    w_b = nrm(ks[16], (DEPTH, DA_WIDTH, D), DA_WIDTH ** -0.5)
    w_o = nrm(ks[17], (DEPTH, D, D), D ** -0.5)
    ffn2_wg = nrm(ks[18], (DEPTH, D, D_FF), D ** -0.5)
    ffn2_wu = nrm(ks[19], (DEPTH, D, D_FF), D ** -0.5)
    ffn2_wd = nrm(ks[20], (DEPTH, D_FF, D), D_FF ** -0.5)
    final_norm = 1.0 + nrm(ks[21], (D,), 0.02)
    return {"x": x, "c": c, "ada_w": ada_w, "ada_b": ada_b,
            "ln_ffn1": ln_ffn1, "ln_mix": ln_mix, "ln_ffn2": ln_ffn2,
            "ffn1_wg": ffn1_wg, "ffn1_wu": ffn1_wu, "ffn1_wd": ffn1_wd,
            "w_in": w_in, "conv_w": conv_w, "a_log": a_log, "dt_bias": dt_bias,
            "dn_norm": dn_norm, "w_a": w_a, "w_b": w_b, "w_o": w_o,
            "ffn2_wg": ffn2_wg, "ffn2_wu": ffn2_wu, "ffn2_wd": ffn2_wd,
            "final_norm": final_norm}


def _fwd_reference(x, c, ada_w, ada_b, ln_ffn1, ln_mix, ln_ffn2, ffn1_wg, ffn1_wu, ffn1_wd,
              w_in, conv_w, a_log, dt_bias, dn_norm, w_a, w_b, w_o,
              ffn2_wg, ffn2_wu, ffn2_wd, final_norm):
    h = x
    c_act = jax.nn.silu(c)
    for l in range(DEPTH):
        mod = c_act @ ada_w[l] + ada_b[l]
        (sh1, sc1, gt1, sh2, sc2, gt2, sh3, sc3, gt3) = jnp.split(mod, N_ADA, axis=-1)
        f = swiglu(modulate(rmsnorm(h, ln_ffn1[l]), sh1, sc1), ffn1_wg[l], ffn1_wu[l], ffn1_wd[l])
        h = h + MACARON_WEIGHT * gt1[:, None, :] * f
        u = modulate(rmsnorm(h, ln_mix[l]), sh2, sc2)
        m = hybrid_mixer(u, w_in[l], conv_w[l], a_log[l], dt_bias[l], dn_norm[l],
                         w_a[l], w_b[l], w_o[l])
        h = h + gt2[:, None, :] * m
        f = swiglu(modulate(rmsnorm(h, ln_ffn2[l]), sh3, sc3), ffn2_wg[l], ffn2_wu[l], ffn2_wd[l])
        h = h + MACARON_WEIGHT * gt3[:, None, :] * f
    return rmsnorm(h, final_norm)


import jax as _jax
import jax.numpy as _jnp

TWIN_FORMAT = 'train_step'
FWD_PARAMS = ['x', 'c', 'ada_w', 'ada_b', 'ln_ffn1', 'ln_mix', 'ln_ffn2', 'ffn1_wg', 'ffn1_wu', 'ffn1_wd', 'w_in', 'conv_w', 'a_log', 'dt_bias', 'dn_norm', 'w_a', 'w_b', 'w_o', 'ffn2_wg', 'ffn2_wu', 'ffn2_wd', 'final_norm']
TWIN_WEIGHTS = ['ada_w', 'ada_b', 'ln_ffn1', 'ln_mix', 'ln_ffn2', 'ffn1_wg', 'ffn1_wu', 'ffn1_wd', 'w_in', 'conv_w', 'a_log', 'dt_bias', 'dn_norm', 'w_a', 'w_b', 'w_o', 'ffn2_wg', 'ffn2_wu', 'ffn2_wd', 'final_norm']
TWIN_DIFF_INPUT = 'x'
TWIN_INPUTS = ['x', 'c', 'ada_w', 'ada_b', 'ln_ffn1', 'ln_mix', 'ln_ffn2', 'ffn1_wg', 'ffn1_wu', 'ffn1_wd', 'w_in', 'conv_w', 'a_log', 'dt_bias', 'dn_norm', 'w_a', 'w_b', 'w_o', 'ffn2_wg', 'ffn2_wu', 'ffn2_wd', 'final_norm', 'loss_target', 'm_ada_w', 'm_ada_b', 'm_ln_ffn1', 'm_ln_mix', 'm_ln_ffn2', 'm_ffn1_wg', 'm_ffn1_wu', 'm_ffn1_wd', 'm_w_in', 'm_conv_w', 'm_a_log', 'm_dt_bias', 'm_dn_norm', 'm_w_a', 'm_w_b', 'm_w_o', 'm_ffn2_wg', 'm_ffn2_wu', 'm_ffn2_wd', 'm_final_norm', 'v_ada_w', 'v_ada_b', 'v_ln_ffn1', 'v_ln_mix', 'v_ln_ffn2', 'v_ffn1_wg', 'v_ffn1_wu', 'v_ffn1_wd', 'v_w_in', 'v_conv_w', 'v_a_log', 'v_dt_bias', 'v_dn_norm', 'v_w_a', 'v_w_b', 'v_w_o', 'v_ffn2_wg', 'v_ffn2_wu', 'v_ffn2_wd', 'v_final_norm']
TWIN_OUTPUTS = ['loss', 'grad_x', 'grad_ada_w', 'grad_ada_b', 'grad_ln_ffn1', 'grad_ln_mix', 'grad_ln_ffn2', 'grad_ffn1_wg', 'grad_ffn1_wu', 'grad_ffn1_wd', 'grad_w_in', 'grad_conv_w', 'grad_a_log', 'grad_dt_bias', 'grad_dn_norm', 'grad_w_a', 'grad_w_b', 'grad_w_o', 'grad_ffn2_wg', 'grad_ffn2_wu', 'grad_ffn2_wd', 'grad_final_norm', 'delta_ada_w', 'delta_ada_b', 'delta_ln_ffn1', 'delta_ln_mix', 'delta_ln_ffn2', 'delta_ffn1_wg', 'delta_ffn1_wu', 'delta_ffn1_wd', 'delta_w_in', 'delta_conv_w', 'delta_a_log', 'delta_dt_bias', 'delta_dn_norm', 'delta_w_a', 'delta_w_b', 'delta_w_o', 'delta_ffn2_wg', 'delta_ffn2_wu', 'delta_ffn2_wd', 'delta_final_norm', 'new_m_ada_w', 'new_m_ada_b', 'new_m_ln_ffn1', 'new_m_ln_mix', 'new_m_ln_ffn2', 'new_m_ffn1_wg', 'new_m_ffn1_wu', 'new_m_ffn1_wd', 'new_m_w_in', 'new_m_conv_w', 'new_m_a_log', 'new_m_dt_bias', 'new_m_dn_norm', 'new_m_w_a', 'new_m_w_b', 'new_m_w_o', 'new_m_ffn2_wg', 'new_m_ffn2_wu', 'new_m_ffn2_wd', 'new_m_final_norm', 'new_v_ada_w', 'new_v_ada_b', 'new_v_ln_ffn1', 'new_v_ln_mix', 'new_v_ln_ffn2', 'new_v_ffn1_wg', 'new_v_ffn1_wu', 'new_v_ffn1_wd', 'new_v_w_in', 'new_v_conv_w', 'new_v_a_log', 'new_v_dt_bias', 'new_v_dn_norm', 'new_v_w_a', 'new_v_w_b', 'new_v_w_o', 'new_v_ffn2_wg', 'new_v_ffn2_wu', 'new_v_ffn2_wd', 'new_v_final_norm']
TWIN_LEAF_KINDS = {'loss': 'loss', 'grad_x': 'grad_x', 'grad_ada_w': 'grad_w', 'grad_ada_b': 'grad_w', 'grad_ln_ffn1': 'grad_w', 'grad_ln_mix': 'grad_w', 'grad_ln_ffn2': 'grad_w', 'grad_ffn1_wg': 'grad_w', 'grad_ffn1_wu': 'grad_w', 'grad_ffn1_wd': 'grad_w', 'grad_w_in': 'grad_w', 'grad_conv_w': 'grad_w', 'grad_a_log': 'grad_w', 'grad_dt_bias': 'grad_w', 'grad_dn_norm': 'grad_w', 'grad_w_a': 'grad_w', 'grad_w_b': 'grad_w', 'grad_w_o': 'grad_w', 'grad_ffn2_wg': 'grad_w', 'grad_ffn2_wu': 'grad_w', 'grad_ffn2_wd': 'grad_w', 'grad_final_norm': 'grad_w', 'delta_ada_w': 'delta_w', 'delta_ada_b': 'delta_w', 'delta_ln_ffn1': 'delta_w', 'delta_ln_mix': 'delta_w', 'delta_ln_ffn2': 'delta_w', 'delta_ffn1_wg': 'delta_w', 'delta_ffn1_wu': 'delta_w', 'delta_ffn1_wd': 'delta_w', 'delta_w_in': 'delta_w', 'delta_conv_w': 'delta_w', 'delta_a_log': 'delta_w', 'delta_dt_bias': 'delta_w', 'delta_dn_norm': 'delta_w', 'delta_w_a': 'delta_w', 'delta_w_b': 'delta_w', 'delta_w_o': 'delta_w', 'delta_ffn2_wg': 'delta_w', 'delta_ffn2_wu': 'delta_w', 'delta_ffn2_wd': 'delta_w', 'delta_final_norm': 'delta_w', 'new_m_ada_w': 'new_m', 'new_m_ada_b': 'new_m', 'new_m_ln_ffn1': 'new_m', 'new_m_ln_mix': 'new_m', 'new_m_ln_ffn2': 'new_m', 'new_m_ffn1_wg': 'new_m', 'new_m_ffn1_wu': 'new_m', 'new_m_ffn1_wd': 'new_m', 'new_m_w_in': 'new_m', 'new_m_conv_w': 'new_m', 'new_m_a_log': 'new_m', 'new_m_dt_bias': 'new_m', 'new_m_dn_norm': 'new_m', 'new_m_w_a': 'new_m', 'new_m_w_b': 'new_m', 'new_m_w_o': 'new_m', 'new_m_ffn2_wg': 'new_m', 'new_m_ffn2_wu': 'new_m', 'new_m_ffn2_wd': 'new_m', 'new_m_final_norm': 'new_m', 'new_v_ada_w': 'new_v', 'new_v_ada_b': 'new_v', 'new_v_ln_ffn1': 'new_v', 'new_v_ln_mix': 'new_v', 'new_v_ln_ffn2': 'new_v', 'new_v_ffn1_wg': 'new_v', 'new_v_ffn1_wu': 'new_v', 'new_v_ffn1_wd': 'new_v', 'new_v_w_in': 'new_v', 'new_v_conv_w': 'new_v', 'new_v_a_log': 'new_v', 'new_v_dt_bias': 'new_v', 'new_v_dn_norm': 'new_v', 'new_v_w_a': 'new_v', 'new_v_w_b': 'new_v', 'new_v_w_o': 'new_v', 'new_v_ffn2_wg': 'new_v', 'new_v_ffn2_wu': 'new_v', 'new_v_ffn2_wd': 'new_v', 'new_v_final_norm': 'new_v'}


def _forward(args):
    return _fwd_reference(*[args[k] for k in FWD_PARAMS])


def _output_shape():
    out = _jax.eval_shape(lambda: _forward(_fwd_setup_inputs(0)))
    return out.shape, out.dtype

N_MICROBATCH = 1
ADAM_LR = 0.001
ADAM_B1 = 0.9
ADAM_B2 = 0.999
ADAM_EPS = 1e-08
ADAM_WD = 0.01
ADAM_STEP = 10
PER_EXAMPLE_BATCH_AXIS = {'x': 0, 'c': 0, 'loss_target': 0}
SHARED_INPUTS = []
_WEIGHT_DTYPES = {'ada_w': _jnp.float32, 'ada_b': _jnp.float32, 'ln_ffn1': _jnp.float32, 'ln_mix': _jnp.float32, 'ln_ffn2': _jnp.float32, 'ffn1_wg': _jnp.float32, 'ffn1_wu': _jnp.float32, 'ffn1_wd': _jnp.float32, 'w_in': _jnp.float32, 'conv_w': _jnp.float32, 'a_log': _jnp.float32, 'dt_bias': _jnp.float32, 'dn_norm': _jnp.float32, 'w_a': _jnp.float32, 'w_b': _jnp.float32, 'w_o': _jnp.float32, 'ffn2_wg': _jnp.float32, 'ffn2_wu': _jnp.float32, 'ffn2_wd': _jnp.float32, 'final_norm': _jnp.float32}
MOMENT_SCALE = {'ada_w': 2.989504e-02, 'ada_b': 4.879101e-02, 'ln_ffn1': 2.765389e-02, 'ln_mix': 3.753031e-02, 'ln_ffn2': 2.720454e-02, 'ffn1_wg': 1.241446e-02, 'ffn1_wu': 1.203650e-02, 'ffn1_wd': 1.989783e-02, 'w_in': 1.364238e-02, 'conv_w': 1.460069e-02, 'a_log': 2.009078e-01, 'dt_bias': 1.840814e-01, 'dn_norm': 5.319262e-02, 'w_a': 1.901775e-02, 'w_b': 1.591119e-02, 'w_o': 2.468532e-02, 'ffn2_wg': 1.214272e-02, 'ffn2_wu': 1.173901e-02, 'ffn2_wd': 1.946038e-02, 'final_norm': 3.200042e+01}


def _to_microbatches(a, axis):
    t = _jnp.moveaxis(a, axis, 0)
    t = t.reshape((N_MICROBATCH, t.shape[0] // N_MICROBATCH) + t.shape[1:])
    return _jnp.moveaxis(t, 1, axis + 1)


def setup_inputs(seed: int = 0) -> dict:
    inp = _fwd_setup_inputs(seed)
    key = _jax.random.fold_in(_jax.random.key(seed), 7919)
    shape, _ = _output_shape()
    out = dict(inp)
    out["loss_target"] = _jax.random.normal(_jax.random.fold_in(key, 0), shape, _jnp.float32)
    for i, name in enumerate(TWIN_WEIGHTS):
        w = inp[name].astype(_jnp.float32)
        if MOMENT_SCALE is None:
            s = _jnp.sqrt(_jnp.mean(_jnp.square(w)) + 1e-30)
        else:
            s = MOMENT_SCALE[name]
        km, kv = _jax.random.split(_jax.random.fold_in(key, i + 1))
        out[name] = w
        out["m_" + name] = s * _jax.random.normal(km, w.shape, _jnp.float32)
        out["v_" + name] = (s * s) * _jax.random.uniform(kv, w.shape, _jnp.float32, 0.5, 1.5)
    if N_MICROBATCH > 1:
        for name, axis in PER_EXAMPLE_BATCH_AXIS.items():
            out[name] = _to_microbatches(out[name], axis)
    return {'x': out['x'], 'c': out['c'], 'ada_w': out['ada_w'], 'ada_b': out['ada_b'], 'ln_ffn1': out['ln_ffn1'], 'ln_mix': out['ln_mix'], 'ln_ffn2': out['ln_ffn2'], 'ffn1_wg': out['ffn1_wg'], 'ffn1_wu': out['ffn1_wu'], 'ffn1_wd': out['ffn1_wd'], 'w_in': out['w_in'], 'conv_w': out['conv_w'], 'a_log': out['a_log'], 'dt_bias': out['dt_bias'], 'dn_norm': out['dn_norm'], 'w_a': out['w_a'], 'w_b': out['w_b'], 'w_o': out['w_o'], 'ffn2_wg': out['ffn2_wg'], 'ffn2_wu': out['ffn2_wu'], 'ffn2_wd': out['ffn2_wd'], 'final_norm': out['final_norm'], 'loss_target': out['loss_target'], 'm_ada_w': out['m_ada_w'], 'm_ada_b': out['m_ada_b'], 'm_ln_ffn1': out['m_ln_ffn1'], 'm_ln_mix': out['m_ln_mix'], 'm_ln_ffn2': out['m_ln_ffn2'], 'm_ffn1_wg': out['m_ffn1_wg'], 'm_ffn1_wu': out['m_ffn1_wu'], 'm_ffn1_wd': out['m_ffn1_wd'], 'm_w_in': out['m_w_in'], 'm_conv_w': out['m_conv_w'], 'm_a_log': out['m_a_log'], 'm_dt_bias': out['m_dt_bias'], 'm_dn_norm': out['m_dn_norm'], 'm_w_a': out['m_w_a'], 'm_w_b': out['m_w_b'], 'm_w_o': out['m_w_o'], 'm_ffn2_wg': out['m_ffn2_wg'], 'm_ffn2_wu': out['m_ffn2_wu'], 'm_ffn2_wd': out['m_ffn2_wd'], 'm_final_norm': out['m_final_norm'], 'v_ada_w': out['v_ada_w'], 'v_ada_b': out['v_ada_b'], 'v_ln_ffn1': out['v_ln_ffn1'], 'v_ln_mix': out['v_ln_mix'], 'v_ln_ffn2': out['v_ln_ffn2'], 'v_ffn1_wg': out['v_ffn1_wg'], 'v_ffn1_wu': out['v_ffn1_wu'], 'v_ffn1_wd': out['v_ffn1_wd'], 'v_w_in': out['v_w_in'], 'v_conv_w': out['v_conv_w'], 'v_a_log': out['v_a_log'], 'v_dt_bias': out['v_dt_bias'], 'v_dn_norm': out['v_dn_norm'], 'v_w_a': out['v_w_a'], 'v_w_b': out['v_w_b'], 'v_w_o': out['v_w_o'], 'v_ffn2_wg': out['v_ffn2_wg'], 'v_ffn2_wu': out['v_ffn2_wu'], 'v_ffn2_wd': out['v_ffn2_wd'], 'v_final_norm': out['v_final_norm']}


def _loss(weights, diff, rest, loss_target):
    with _jax.named_scope("forward"):
        args = {**rest, TWIN_DIFF_INPUT: diff, **{k: w.astype(_WEIGHT_DTYPES[k]) for k, w in weights.items()}}
        y = _forward(args)
    with _jax.named_scope("loss_head"):
        err = _jnp.square(y.astype(_jnp.float32) - loss_target)
        return 0.5 * _jnp.sum(_jnp.mean(err, axis=-1)) if err.ndim else 0.5 * err


def _adamw(w, g, m, v):
    m = ADAM_B1 * m + (1.0 - ADAM_B1) * g
    v = ADAM_B2 * v + (1.0 - ADAM_B2) * _jnp.square(g)
    m_hat = m / (1.0 - ADAM_B1 ** ADAM_STEP)
    v_hat = v / (1.0 - ADAM_B2 ** ADAM_STEP)
    delta = -ADAM_LR * (m_hat / (_jnp.sqrt(v_hat) + ADAM_EPS) + ADAM_WD * w)
    return delta, m, v


def reference(x, c, ada_w, ada_b, ln_ffn1, ln_mix, ln_ffn2, ffn1_wg, ffn1_wu, ffn1_wd, w_in, conv_w, a_log, dt_bias, dn_norm, w_a, w_b, w_o, ffn2_wg, ffn2_wu, ffn2_wd, final_norm, loss_target, m_ada_w, m_ada_b, m_ln_ffn1, m_ln_mix, m_ln_ffn2, m_ffn1_wg, m_ffn1_wu, m_ffn1_wd, m_w_in, m_conv_w, m_a_log, m_dt_bias, m_dn_norm, m_w_a, m_w_b, m_w_o, m_ffn2_wg, m_ffn2_wu, m_ffn2_wd, m_final_norm, v_ada_w, v_ada_b, v_ln_ffn1, v_ln_mix, v_ln_ffn2, v_ffn1_wg, v_ffn1_wu, v_ffn1_wd, v_w_in, v_conv_w, v_a_log, v_dt_bias, v_dn_norm, v_w_a, v_w_b, v_w_o, v_ffn2_wg, v_ffn2_wu, v_ffn2_wd, v_final_norm):
    given = dict(x=x, c=c, ada_w=ada_w, ada_b=ada_b, ln_ffn1=ln_ffn1, ln_mix=ln_mix, ln_ffn2=ln_ffn2, ffn1_wg=ffn1_wg, ffn1_wu=ffn1_wu, ffn1_wd=ffn1_wd, w_in=w_in, conv_w=conv_w, a_log=a_log, dt_bias=dt_bias, dn_norm=dn_norm, w_a=w_a, w_b=w_b, w_o=w_o, ffn2_wg=ffn2_wg, ffn2_wu=ffn2_wu, ffn2_wd=ffn2_wd, final_norm=final_norm, loss_target=loss_target, m_ada_w=m_ada_w, m_ada_b=m_ada_b, m_ln_ffn1=m_ln_ffn1, m_ln_mix=m_ln_mix, m_ln_ffn2=m_ln_ffn2, m_ffn1_wg=m_ffn1_wg, m_ffn1_wu=m_ffn1_wu, m_ffn1_wd=m_ffn1_wd, m_w_in=m_w_in, m_conv_w=m_conv_w, m_a_log=m_a_log, m_dt_bias=m_dt_bias, m_dn_norm=m_dn_norm, m_w_a=m_w_a, m_w_b=m_w_b, m_w_o=m_w_o, m_ffn2_wg=m_ffn2_wg, m_ffn2_wu=m_ffn2_wu, m_ffn2_wd=m_ffn2_wd, m_final_norm=m_final_norm, v_ada_w=v_ada_w, v_ada_b=v_ada_b, v_ln_ffn1=v_ln_ffn1, v_ln_mix=v_ln_mix, v_ln_ffn2=v_ln_ffn2, v_ffn1_wg=v_ffn1_wg, v_ffn1_wu=v_ffn1_wu, v_ffn1_wd=v_ffn1_wd, v_w_in=v_w_in, v_conv_w=v_conv_w, v_a_log=v_a_log, v_dt_bias=v_dt_bias, v_dn_norm=v_dn_norm, v_w_a=v_w_a, v_w_b=v_w_b, v_w_o=v_w_o, v_ffn2_wg=v_ffn2_wg, v_ffn2_wu=v_ffn2_wu, v_ffn2_wd=v_ffn2_wd, v_final_norm=v_final_norm)
    weights = {n: given[n] for n in TWIN_WEIGHTS}
    shared = {n: given[n] for n in SHARED_INPUTS}
    per_example = {n: given[n] for n in ['x', 'c']}
    grad_fn = _jax.value_and_grad(_loss, argnums=(0, 1))

    def one_microbatch(ex, loss_target):
        ex = dict(ex)
        diff = ex.pop(TWIN_DIFF_INPUT)
        return grad_fn(weights, diff, {**shared, **ex}, loss_target)

    if N_MICROBATCH == 1:
        loss, (grad_w, grad_x) = one_microbatch(per_example, given["loss_target"])
    else:
        def body(carry, xs):
            loss_sum, grad_sum = carry
            l_k, (gw_k, gx_k) = one_microbatch(xs[0], xs[1])
            with _jax.named_scope("update"):
                return (loss_sum + l_k, _jax.tree.map(_jnp.add, grad_sum, gw_k)), gx_k

        init = (_jnp.zeros((), _jnp.float32), _jax.tree.map(_jnp.zeros_like, weights))
        (loss, grad_w), grad_x = _jax.lax.scan(body, init, (per_example, given["loss_target"]))
    with _jax.named_scope("update"):
        delta_w, new_m, new_v = {}, {}, {}
        for n in TWIN_WEIGHTS:
            delta_w[n], new_m[n], new_v[n] = _adamw(weights[n], grad_w[n], given["m_" + n], given["v_" + n])
    return (loss, grad_x, *[grad_w[n] for n in TWIN_WEIGHTS], *[delta_w[n] for n in TWIN_WEIGHTS],
            *[new_m[n] for n in TWIN_WEIGHTS], *[new_v[n] for n in TWIN_WEIGHTS])
```

```python
import functools

import jax
import jax.numpy as jnp
from jax import lax
from jax.experimental import pallas as pl
from jax.experimental.pallas import tpu as pltpu

F32 = jnp.float32
BF16 = jnp.bfloat16
MESH = pl.DeviceIdType.MESH

NORM_EPS = 1e-6
DN_HEADS, DN_DIM, DN_CHUNK, DN_CONV = 8, 128, 64, 4
DN_WIDTH = DN_HEADS * DN_DIM
DA_HEADS, DA_DIM, DA_BLOCK = 12, 64, 128
DA_WIDTH = DA_HEADS * DA_DIM
DA_PATTERNS = ((128, 1), (512, 4), (2048, 16))
ALIBI_MAX_EXP = 8.0
N_ADA = 9
LANES = 128
V7X_VMEM_BYTES = 64 << 20
ADAM_LR, ADAM_B1, ADAM_B2, ADAM_EPS, ADAM_WD, ADAM_STEP = 0.001, 0.9, 0.999, 1e-08, 0.01, 10
NEG = -1e30
HI = lax.Precision.HIGHEST
NN = (((1,), (0,)), ((), ()))
NT = (((1,), (1,)), ((), ()))
TN = (((0,), (0,)), ((), ()))


def _nbytes(shape, dtype):
    n = 1
    for s in shape:
        n *= s
    return n * jnp.dtype(dtype).itemsize


def _params(block_bytes, scratch_bytes=0):
    need = 2 * block_bytes + scratch_bytes
    lim = min(max(need + need // 4 + (4 << 20), 32 << 20), V7X_VMEM_BYTES - (6 << 20))
    return pltpu.CompilerParams(vmem_limit_bytes=int(lim))


def _pick(n, cands):
    for c in cands:
        if c <= n and n % c == 0:
            return c
    return n


def _sigmoid(x):
    return jax.nn.sigmoid(x)


def _silu(x):
    return x * jax.nn.sigmoid(x)


def _softplus(x):
    return jnp.maximum(x, 0.0) + jnp.log(1.0 + jnp.exp(-jnp.abs(x)))


def _rowwise(name, fn, rows, bcast, row_outs, red_outs=(), bm=256):
    rows = [r if isinstance(r, tuple) else (r, r.shape[1], 0) for r in rows]
    s = rows[0][0].shape[0]
    bm = _pick(s, (bm, 128, 64, 32, 16, 8))
    nr, nb, no, nd = len(rows), len(bcast), len(row_outs), len(red_outs)
    in_specs = [pl.BlockSpec((bm, w), functools.partial(lambda i, ci: (i, ci), ci=ci)) for (_, w, ci) in rows]
    in_specs += [pl.BlockSpec(b.shape, lambda i: (0, 0)) for b in bcast]
    out_shape = [jax.ShapeDtypeStruct((s, w), dt) for (w, dt) in row_outs]
    out_shape += [jax.ShapeDtypeStruct((r, w), F32) for (r, w) in red_outs]
    out_specs = [pl.BlockSpec((bm, w), lambda i: (i, 0)) for (w, _) in row_outs]
    out_specs += [pl.BlockSpec((r, w), lambda i: (0, 0)) for (r, w) in red_outs]

    def body(*refs):
        ins = [r[...] for r in refs[:nr + nb]]
        outs = refs[nr + nb:nr + nb + no]
        reds = refs[nr + nb + no:]
        ov, rv = fn(*ins)
        for o, v in zip(outs, ov):
            o[...] = v.astype(o.dtype)
        if nd:
            @pl.when(pl.program_id(0) == 0)
            def _():
                for r in reds:
                    r[...] = jnp.zeros(r.shape, F32)
            for r, v in zip(reds, rv):
                r[...] += v.astype(F32)

    blk = sum(_nbytes((bm, w), a.dtype) for (a, w, _) in rows) + sum(_nbytes(b.shape, b.dtype) for b in bcast)
    blk += sum(_nbytes((bm, w), dt) for (w, dt) in row_outs) + sum(_nbytes(r, F32) for r in red_outs)
    res = pl.pallas_call(
        body, name=name, grid=(s // bm,), in_specs=in_specs, out_specs=out_specs, out_shape=out_shape,
        compiler_params=_params(3 * blk),
    )(*[a for (a, _, _) in rows], *bcast)
    return res


def _matmul(name, a, b, *, ta=False, tb=False, outs=(F32,), epi=None, epi_rows=(), epi_bcast=(),
            bm=None, bn=None, bk=None):
    if ta:
        k, m = a.shape
    else:
        m, k = a.shape
    n = b.shape[0] if tb else b.shape[1]
    assert (b.shape[1] if tb else b.shape[0]) == k, (name, a.shape, b.shape)
    if bm is None:
        bm = _pick(m, (1024, 1408, 768, 512, 384, 256, 128)) if ta else _pick(m, (1024, 512, 256, 128, 64, 32, 16))
    if bn is None:
        bn = _pick(n, (512, 384, 256, 128))
    if bk is None:
        bk = k if k <= 3072 else _pick(k, (2816, 2048, 1024, 512))
        if ta:
            bk = _pick(k, (1024, 512, 256, 128, 64, 32, 16))
    nk = k // bk
    dims = TN if ta else (NT if tb else NN)
    a_spec = pl.BlockSpec((bk, bm), lambda i, j, kk: (kk, i)) if ta else pl.BlockSpec((bm, bk), lambda i, j, kk: (i, kk))
    b_spec = pl.BlockSpec((bn, bk), lambda i, j, kk: (j, kk)) if tb else pl.BlockSpec((bk, bn), lambda i, j, kk: (kk, j))
    in_specs = [a_spec, b_spec]
    in_specs += [pl.BlockSpec((bm, bn), lambda i, j, kk: (i, j)) for _ in epi_rows]
    in_specs += [pl.BlockSpec((1, bn), lambda i, j, kk: (0, j)) for _ in epi_bcast]
    out_shape = [jax.ShapeDtypeStruct((m, n), dt) for dt in outs]
    out_specs = [pl.BlockSpec((bm, bn), lambda i, j, kk: (i, j)) for _ in outs]
    ner, neb, no = len(epi_rows), len(epi_bcast), len(outs)

    def body(*refs):
        a_ref, b_ref = refs[0], refs[1]
        extra = refs[2:2 + ner + neb]
        out_refs = refs[2 + ner + neb:2 + ner + neb + no]
        prod = lax.dot_general(a_ref[...], b_ref[...], dims, preferred_element_type=F32)

        def finish(acc):
            vals = epi(acc, *[r[...] for r in extra]) if epi is not None else (acc,)
            for o, v in zip(out_refs, vals):
                o[...] = v.astype(o.dtype)

        if nk == 1:
            finish(prod)
        else:
            acc_ref = refs[-1]
            kk = pl.program_id(2)

            @pl.when(kk == 0)
            def _():
                acc_ref[...] = prod

            @pl.when(kk > 0)
            def _():
                acc_ref[...] += prod

            @pl.when(kk == nk - 1)
            def _():
                finish(acc_ref[...])

    blk = _nbytes((bm, bk), a.dtype) + _nbytes((bk, bn), b.dtype)
    blk += sum(_nbytes((bm, bn), r.dtype) for r in epi_rows) + sum(_nbytes((bm, bn), dt) for dt in outs)
    scratch = [pltpu.VMEM((bm, bn), F32)] if nk > 1 else []
    res = pl.pallas_call(
        body, name=name, grid=(m // bm, n // bn, nk), in_specs=in_specs, out_specs=out_specs,
        out_shape=out_shape, scratch_shapes=scratch,
        compiler_params=_params(blk, 3 * _nbytes((bm, bn), F32)),
    )(a, b, *epi_rows, *epi_bcast)
    return res


def _rms_mod(h, ln, sh, sc):
    n = h * lax.rsqrt(jnp.mean(h * h, axis=-1, keepdims=True) + NORM_EPS) * ln
    return n * (1.0 + sc) + sh


def _swiglu_act(g, u):
    return _silu(g.astype(F32)) * u.astype(F32)


def _dn_prep(yc, pba, alog, dtb):
    act = _silu(yc)
    parts = []
    for idx in range(2 * DN_HEADS):
        seg = act[:, idx * DN_DIM:(idx + 1) * DN_DIM]
        seg = seg * lax.rsqrt(jnp.sum(seg * seg, axis=-1, keepdims=True) + NORM_EPS)
        if idx < DN_HEADS:
            seg = seg * (DN_DIM ** -0.5)
        parts.append(seg)
    parts.append(act[:, 2 * DN_WIDTH:])
    qkvn = jnp.concatenate(parts, axis=1)
    lane = lax.broadcasted_iota(jnp.int32, pba.shape, 1)
    beta = _sigmoid(pba)
    g = -jnp.exp(alog) * _softplus(pba + dtb)
    gb = jnp.where(lane < DN_HEADS, beta, jnp.where(lane < 2 * DN_HEADS, g, 0.0))
    return qkvn, gb


def _dn_outnorm(o_a, z, dn):
    parts = []
    for h in range(DN_HEADS):
        seg = o_a[:, h * DN_DIM:(h + 1) * DN_DIM]
        seg = seg * lax.rsqrt(jnp.mean(seg * seg, axis=-1, keepdims=True) + NORM_EPS) * dn
        parts.append(seg)
    return jnp.concatenate(parts, axis=1) * _silu(z)


def _shift_down(x, halo8, s):
    r = pltpu.roll(x, s, axis=0)
    top = pltpu.roll(halo8, s, axis=0)
    i8 = lax.broadcasted_iota(jnp.int32, top.shape, 0)
    return jnp.concatenate([jnp.where(i8 < s, top, r[0:8]), r[8:]], axis=0)


def _shift_up(x, halo8, s):
    m = x.shape[0]
    r = pltpu.roll(x, m - s, axis=0)
    bot = pltpu.roll(halo8, 8 - s, axis=0)
    i8 = lax.broadcasted_iota(jnp.int32, bot.shape, 0)
    return jnp.concatenate([r[:m - 8], jnp.where(i8 >= 8 - s, bot, r[m - 8:])], axis=0)


def _conv_prep_fwd(name, pq, convw8, pba, alog, dtb, bm=256):
    s, w = pq.shape
    nblk = s // bm
    hb = bm // 16

    def body(x_ref, halo_ref, w_ref, pba_ref, alog_ref, dtb_ref, yc_ref, qkv_ref, gb_ref):
        i = pl.program_id(0)
        x = x_ref[...].astype(F32)
        halo = jnp.where(i > 0, halo_ref[...].astype(F32)[8:16], 0.0)
        cw = w_ref[...]
        y = x * cw[DN_CONV - 1:DN_CONV]
        for sft in range(1, DN_CONV):
            y = y + _shift_down(x, halo, sft) * cw[DN_CONV - 1 - sft:DN_CONV - sft]
        ycb = y.astype(BF16)
        yc_ref[...] = ycb
        qkvn, gb = _dn_prep(ycb.astype(F32), pba_ref[...], alog_ref[...], dtb_ref[...])
        qkv_ref[...] = qkvn.astype(BF16)
        gb_ref[...] = gb

    blk = 3 * _nbytes((bm, w), BF16) + 4 * _nbytes((bm, w), F32)
    return pl.pallas_call(
        body, name=name, grid=(nblk,),
        in_specs=[pl.BlockSpec((bm, w), lambda i: (i, 0)),
                  pl.BlockSpec((16, w), lambda i: (jnp.maximum(i * hb - 1, 0), 0)),
                  pl.BlockSpec(convw8.shape, lambda i: (0, 0)),
                  pl.BlockSpec((bm, LANES), lambda i: (i, 0)),
                  pl.BlockSpec((1, LANES), lambda i: (0, 0)),
                  pl.BlockSpec((1, LANES), lambda i: (0, 0))],
        out_specs=[pl.BlockSpec((bm, w), lambda i: (i, 0)), pl.BlockSpec((bm, w), lambda i: (i, 0)),
                   pl.BlockSpec((bm, LANES), lambda i: (i, 0))],
        out_shape=[jax.ShapeDtypeStruct((s, w), BF16), jax.ShapeDtypeStruct((s, w), BF16),
                   jax.ShapeDtypeStruct((s, LANES), F32)],
        compiler_params=_params(blk),
    )(pq, pq, convw8, pba, alog, dtb)


def _conv_bwd(name, dyc, pq, convw8, bm=256):
    s, w = pq.shape
    nblk = s // bm
    hb = bm // 16

    def body(dy_ref, dyn_ref, x_ref, xh_ref, w_ref, dx_ref, dw_ref):
        i = pl.program_id(0)
        dy = dy_ref[...].astype(F32)
        nxt = jnp.where(i < nblk - 1, dyn_ref[...].astype(F32)[0:8], 0.0)
        x = x_ref[...].astype(F32)
        halo = jnp.where(i > 0, xh_ref[...].astype(F32)[8:16], 0.0)
        cw = w_ref[...]
        dx = dy * cw[DN_CONV - 1:DN_CONV]
        for sft in range(1, DN_CONV):
            dx = dx + _shift_up(dy, nxt, sft) * cw[DN_CONV - 1 - sft:DN_CONV - sft]
        dx_ref[...] = dx.astype(dx_ref.dtype)
        r8 = lax.broadcasted_iota(jnp.int32, (8, w), 0)
        dw = jnp.zeros((8, w), F32)
        for j in range(DN_CONV):
            sft = DN_CONV - 1 - j
            xs = x if sft == 0 else _shift_down(x, halo, sft)
            dw = dw + jnp.where(r8 == j, jnp.sum(dy * xs, axis=0, keepdims=True), 0.0)

        @pl.when(i == 0)
        def _():
            dw_ref[...] = jnp.zeros((8, w), F32)
        dw_ref[...] += dw

    blk = 4 * _nbytes((bm, w), BF16) + 5 * _nbytes((bm, w), F32)
    return pl.pallas_call(
        body, name=name, grid=(nblk,),
        in_specs=[pl.BlockSpec((bm, w), lambda i: (i, 0)),
                  pl.BlockSpec((16, w), lambda i: (jnp.minimum((i + 1) * hb, s // 16 - 1), 0)),
                  pl.BlockSpec((bm, w), lambda i: (i, 0)),
                  pl.BlockSpec((16, w), lambda i: (jnp.maximum(i * hb - 1, 0), 0)),
                  pl.BlockSpec(convw8.shape, lambda i: (0, 0))],
        out_specs=[pl.BlockSpec((bm, w), lambda i: (i, 0)), pl.BlockSpec((8, w), lambda i: (0, 0))],
        out_shape=[jax.ShapeDtypeStruct((s, w), BF16), jax.ShapeDtypeStruct((8, w), F32)],
        compiler_params=_params(blk),
    )(dyc, dyc, pq, pq, convw8)


def _delta_chunk(q, k, v, gcol, bcol, state):
    c = q.shape[0]

    def dot(a, b, dims=NN):
        return lax.dot_general(a, b, dims, precision=HI, preferred_element_type=F32)

    row = lax.broadcasted_iota(jnp.int32, (c, c), 0)
    col = lax.broadcasted_iota(jnp.int32, (c, c), 1)
    incl, strict, eye = row >= col, row > col, row == col
    g_b = jnp.broadcast_to(gcol, (c, c))
    gc_row = jnp.sum(jnp.where(row <= col, g_b, 0.0), axis=0, keepdims=True)
    g_r = jnp.sum(jnp.where(eye, g_b, 0.0), axis=0, keepdims=True)
    gc_col = jnp.sum(jnp.where(incl, jnp.broadcast_to(g_r, (c, c)), 0.0), axis=1, keepdims=True)
    decay = jnp.exp(jnp.where(incl, gc_col - gc_row, NEG))
    kb = k * bcol
    vb = v * bcol
    x = -jnp.where(strict, dot(kb, k, NT) * decay, 0.0)
    t = jnp.where(eye, 1.0, 0.0) + x
    p = x
    for _ in range(5):
        p = dot(p, p)
        t = t + dot(t, p)
    eg = jnp.exp(gc_col)
    u = dot(t, vb)
    w = dot(t, kb * eg)
    qk = dot(q, k, NT) * decay
    v_new = u - dot(w, state)
    o = dot(q * eg, state) + dot(qk, v_new)
    g_last = jnp.sum(g_r, axis=1, keepdims=True)
    new_state = state * jnp.exp(g_last) + dot(k * jnp.exp(g_last - gc_col), v_new, TN)
    return o, new_state


def _lane_col(blk, idx):
    lane = lax.broadcasted_iota(jnp.int32, blk.shape, 1)
    return jnp.sum(jnp.where(lane == idx, blk, 0.0), axis=1, keepdims=True)


def _delta_fwd(name, qkvn, gb):
    s = qkvn.shape[0]
    n = s // DN_CHUNK
    c = DN_CHUNK

    def body(qkv_ref, gb_ref, o_ref, st_ref, state):
        @pl.when(pl.program_id(0) == 0)
        def _():
            state[...] = jnp.zeros(state.shape, F32)

        gbv = gb_ref[...]
        for h in range(DN_HEADS):
            q = qkv_ref[:, h * DN_DIM:(h + 1) * DN_DIM].astype(F32)
            k = qkv_ref[:, DN_WIDTH + h * DN_DIM:DN_WIDTH + (h + 1) * DN_DIM].astype(F32)
            v = qkv_ref[:, 2 * DN_WIDTH + h * DN_DIM:2 * DN_WIDTH + (h + 1) * DN_DIM].astype(F32)
            st = state[h]
            st_ref[0, h] = st
            o, new = _delta_chunk(q, k, v, _lane_col(gbv, DN_HEADS + h), _lane_col(gbv, h), st)
            o_ref[:, h * DN_DIM:(h + 1) * DN_DIM] = o
            state[h] = new

    blk = _nbytes((c, 3 * DN_WIDTH), BF16) + _nbytes((c, LANES), F32) + _nbytes((c, DN_WIDTH), F32)
    blk += _nbytes((DN_HEADS, DN_DIM, DN_DIM), F32)
    return pl.pallas_call(
        body, name=name, grid=(n,),
        in_specs=[pl.BlockSpec((c, 3 * DN_WIDTH), lambda i: (i, 0)), pl.BlockSpec((c, LANES), lambda i: (i, 0))],
        out_specs=[pl.BlockSpec((c, DN_WIDTH), lambda i: (i, 0)),
                   pl.BlockSpec((1, DN_HEADS, DN_DIM, DN_DIM), lambda i: (i, 0, 0, 0))],
        out_shape=[jax.ShapeDtypeStruct((s, DN_WIDTH), F32),
                   jax.ShapeDtypeStruct((n, DN_HEADS, DN_DIM, DN_DIM), F32)],
        scratch_shapes=[pltpu.VMEM((DN_HEADS, DN_DIM, DN_DIM), F32)],
        compiler_params=_params(blk, 8 << 20),
    )(qkvn, gb)


def _delta_bwd(name, qkvn, gb, states, d_o):
    s = qkvn.shape[0]
    n = s // DN_CHUNK
    c = DN_CHUNK

    def body(qkv_ref, gb_ref, st_ref, do_ref, dqkv_ref, dgb_ref, dstate):
        @pl.when(pl.program_id(0) == 0)
        def _():
            dstate[...] = jnp.zeros(dstate.shape, F32)

        gbv = gb_ref[...]
        lane = lax.broadcasted_iota(jnp.int32, (c, LANES), 1)
        dgb = jnp.zeros((c, LANES), F32)
        for h in range(DN_HEADS):
            q = qkv_ref[:, h * DN_DIM:(h + 1) * DN_DIM].astype(F32)
            k = qkv_ref[:, DN_WIDTH + h * DN_DIM:DN_WIDTH + (h + 1) * DN_DIM].astype(F32)
            v = qkv_ref[:, 2 * DN_WIDTH + h * DN_DIM:2 * DN_WIDTH + (h + 1) * DN_DIM].astype(F32)
            _, vjp = jax.vjp(_delta_chunk, q, k, v, _lane_col(gbv, DN_HEADS + h), _lane_col(gbv, h), st_ref[0, h])
            dq, dk, dv, dg, db, dst = vjp((do_ref[:, h * DN_DIM:(h + 1) * DN_DIM], dstate[h]))
            dqkv_ref[:, h * DN_DIM:(h + 1) * DN_DIM] = dq
            dqkv_ref[:, DN_WIDTH + h * DN_DIM:DN_WIDTH + (h + 1) * DN_DIM] = dk
            dqkv_ref[:, 2 * DN_WIDTH + h * DN_DIM:2 * DN_WIDTH + (h + 1) * DN_DIM] = dv
            dgb = dgb + jnp.where(lane == h, db, 0.0) + jnp.where(lane == DN_HEADS + h, dg, 0.0)
            dstate[h] = dst
        dgb_ref[...] = dgb

    rev = lambda i: (n - 1 - i, 0)
    blk = _nbytes((c, 3 * DN_WIDTH), BF16) + 2 * _nbytes((c, LANES), F32) + _nbytes((c, DN_WIDTH), F32)
    blk += _nbytes((DN_HEADS, DN_DIM, DN_DIM), F32) + _nbytes((c, 3 * DN_WIDTH), F32)
    return pl.pallas_call(
        body, name=name, grid=(n,),
        in_specs=[pl.BlockSpec((c, 3 * DN_WIDTH), rev), pl.BlockSpec((c, LANES), rev),
                  pl.BlockSpec((1, DN_HEADS, DN_DIM, DN_DIM), lambda i: (n - 1 - i, 0, 0, 0)),
                  pl.BlockSpec((c, DN_WIDTH), rev)],
        out_specs=[pl.BlockSpec((c, 3 * DN_WIDTH), rev), pl.BlockSpec((c, LANES), rev)],
        out_shape=[jax.ShapeDtypeStruct((s, 3 * DN_WIDTH), F32), jax.ShapeDtypeStruct((s, LANES), F32)],
        scratch_shapes=[pltpu.VMEM((DN_HEADS, DN_DIM, DN_DIM), F32)],
        compiler_params=_params(blk, 16 << 20),
    )(qkvn, gb, states, d_o)


def _da_scores(q2f, k2, sub, valid, distf, head):
    lane = lax.broadcasted_iota(jnp.int32, q2f.shape, 1)
    hmask = (lane < DA_DIM) if sub == 0 else (lane >= DA_DIM)
    qm = jnp.where(hmask, q2f, 0.0).astype(BF16)
    slope = 2.0 ** (-ALIBI_MAX_EXP * (head + 1) / DA_HEADS)
    sc = lax.dot_general(qm, k2, NT, preferred_element_type=F32) * (DA_DIM ** -0.5)
    return jnp.where(valid, sc - slope * distf, NEG), qm, hmask


def _da_mask(i, r):
    qi = lax.broadcasted_iota(jnp.int32, (DA_BLOCK, 2 * DA_BLOCK), 0)
    ki = lax.broadcasted_iota(jnp.int32, (DA_BLOCK, 2 * DA_BLOCK), 1)
    dist = qi + DA_BLOCK - ki
    valid = (dist >= 0) & (dist <= DA_BLOCK) & ((ki >= DA_BLOCK) | (i > 0))
    return valid, (dist * r).astype(F32)


def _da_fwd(name, pda, r):
    s = pda.shape[0]
    n = s // r
    nb = n // DA_BLOCK
    w = DA_WIDTH
    dav = pda.reshape(n, r * 3 * w)

    def body(q_ref, kc_ref, kp_ref, vc_ref, vp_ref, o_ref, lse_ref):
        i = pl.program_id(1)
        valid, distf = _da_mask(i, r)
        lane = lax.broadcasted_iota(jnp.int32, (DA_BLOCK, LANES), 1)
        lse = jnp.zeros((DA_BLOCK, LANES), F32)
        for hp in range(DA_HEADS // 2):
            sl = slice(hp * LANES, (hp + 1) * LANES)
            q2f = q_ref[:, sl].astype(F32)
            k2 = jnp.concatenate([kp_ref[:, sl], kc_ref[:, sl]], axis=0)
            v2 = jnp.concatenate([vp_ref[:, sl], vc_ref[:, sl]], axis=0)
            o2 = None
            for sub in range(2):
                head = 2 * hp + sub
                sc, _, hmask = _da_scores(q2f, k2, sub, valid, distf, head)
                mx = jnp.max(sc, axis=1, keepdims=True)
                p = jnp.exp(sc - mx)
                l = jnp.sum(p, axis=1, keepdims=True)
                pv = lax.dot_general(p.astype(BF16), v2, NN, preferred_element_type=F32) / l
                o2 = pv if sub == 0 else jnp.where(hmask, pv, o2)
                lse = jnp.where(lane == head, mx + jnp.log(l), lse)
            o_ref[:, sl] = o2
        lse_ref[...] = lse

    prev = lambda col: (lambda p, i: (jnp.maximum(i - 1, 0), 3 * p + col))
    cur = lambda col: (lambda p, i: (i, 3 * p + col))
    blk = 5 * _nbytes((DA_BLOCK, w), BF16) + _nbytes((DA_BLOCK, w), F32) + _nbytes((DA_BLOCK, LANES), F32)
    o, lse = pl.pallas_call(
        body, name=name, grid=(r, nb),
        in_specs=[pl.BlockSpec((DA_BLOCK, w), cur(0)), pl.BlockSpec((DA_BLOCK, w), cur(1)),
                  pl.BlockSpec((DA_BLOCK, w), prev(1)), pl.BlockSpec((DA_BLOCK, w), cur(2)),
                  pl.BlockSpec((DA_BLOCK, w), prev(2))],
        out_specs=[pl.BlockSpec((DA_BLOCK, w), lambda p, i: (i, p)),
                   pl.BlockSpec((DA_BLOCK, LANES), lambda p, i: (i, p))],
        out_shape=[jax.ShapeDtypeStruct((n, r * w), F32), jax.ShapeDtypeStruct((n, r * LANES), F32)],
        compiler_params=_params(blk, 8 << 20),
    )(dav, dav, dav, dav, dav)
    return o.reshape(s, w), lse.reshape(s, LANES)


def _da_bwd(name, pda, d_ob, lse_tot, delta, r):
    s = pda.shape[0]
    n = s // r
    nb = n // DA_BLOCK
    w = DA_WIDTH
    dav = pda.reshape(n, r * 3 * w)
    dov = d_ob.reshape(n, r * w)
    lv = lse_tot.reshape(n, r * LANES)
    dlv = delta.reshape(n, r * LANES)

    def body(q_ref, kc_ref, kp_ref, vc_ref, vp_ref, do_ref, l_ref, dl_ref, dq_ref, dk_ref, dv_ref, ck, cv):
        i = pl.program_id(1)

        @pl.when(i == 0)
        def _():
            ck[...] = jnp.zeros(ck.shape, F32)
            cv[...] = jnp.zeros(cv.shape, F32)

        @pl.when(i < nb)
        def _():
            valid, distf = _da_mask(i, r)
            lsev = l_ref[...]
            dlt = dl_ref[...]
            for hp in range(DA_HEADS // 2):
                sl = slice(hp * LANES, (hp + 1) * LANES)
                q2f = q_ref[:, sl].astype(F32)
                k2 = jnp.concatenate([kp_ref[:, sl], kc_ref[:, sl]], axis=0)
                v2 = jnp.concatenate([vp_ref[:, sl], vc_ref[:, sl]], axis=0)
                do2f = do_ref[:, sl].astype(F32)
                dq2 = jnp.zeros((DA_BLOCK, LANES), F32)
                dk2 = jnp.zeros((2 * DA_BLOCK, LANES), F32)
                dv2 = jnp.zeros((2 * DA_BLOCK, LANES), F32)
                for sub in range(2):
                    head = 2 * hp + sub
                    sc, qm, hmask = _da_scores(q2f, k2, sub, valid, distf, head)
                    p = jnp.exp(sc - _lane_col(lsev, head))
                    dom = jnp.where(hmask, do2f, 0.0).astype(BF16)
                    dp = lax.dot_general(dom, v2, NT, preferred_element_type=F32)
                    ds = (p * (dp - _lane_col(dlt, head)) * (DA_DIM ** -0.5)).astype(BF16)
                    dq2 = dq2 + jnp.where(hmask, lax.dot_general(ds, k2, NN, preferred_element_type=F32), 0.0)
                    dk2 = dk2 + lax.dot_general(ds, qm, TN, preferred_element_type=F32)
                    dv2 = dv2 + lax.dot_general(p.astype(BF16), dom, TN, preferred_element_type=F32)
                dq_ref[:, sl] = dq2
                dk_ref[:, sl] = ck[:, sl] + dk2[:DA_BLOCK]
                dv_ref[:, sl] = cv[:, sl] + dv2[:DA_BLOCK]
                ck[:, sl] = dk2[DA_BLOCK:]
                cv[:, sl] = dv2[DA_BLOCK:]

        @pl.when(i == nb)
        def _():
            dk_ref[...] = ck[...]
            dv_ref[...] = cv[...]

    qrow = lambda i: jnp.minimum(i, nb - 1)
    prev = lambda col: (lambda p, i: (jnp.maximum(qrow(i) - 1, 0), 3 * p + col))
    cur = lambda col: (lambda p, i: (qrow(i), 3 * p + col))
    same = lambda p, i: (qrow(i), p)
    late = lambda p, i: (jnp.maximum(i - 1, 0), p)
    blk = 6 * _nbytes((DA_BLOCK, w), BF16) + 2 * _nbytes((DA_BLOCK, LANES), F32) + 3 * _nbytes((DA_BLOCK, w), F32)
    dq, dk, dv = pl.pallas_call(
        body, name=name, grid=(r, nb + 1),
        in_specs=[pl.BlockSpec((DA_BLOCK, w), cur(0)), pl.BlockSpec((DA_BLOCK, w), cur(1)),
                  pl.BlockSpec((DA_BLOCK, w), prev(1)), pl.BlockSpec((DA_BLOCK, w), cur(2)),
                  pl.BlockSpec((DA_BLOCK, w), prev(2)), pl.BlockSpec((DA_BLOCK, w), same),
                  pl.BlockSpec((DA_BLOCK, LANES), same), pl.BlockSpec((DA_BLOCK, LANES), same)],
        out_specs=[pl.BlockSpec((DA_BLOCK, w), same), pl.BlockSpec((DA_BLOCK, w), late),
                   pl.BlockSpec((DA_BLOCK, w), late)],
        out_shape=[jax.ShapeDtypeStruct((n, r * w), F32)] * 3,
        scratch_shapes=[pltpu.VMEM((DA_BLOCK, w), F32), pltpu.VMEM((DA_BLOCK, w), F32)],
        compiler_params=_params(blk, 12 << 20),
    )(dav, dav, dav, dav, dav, dov, lv, dlv)
    return dq.reshape(s, w), dk.reshape(s, w), dv.reshape(s, w)


def _head_expand():
    hrow = lax.broadcasted_iota(jnp.int32, (LANES, DA_WIDTH), 0)
    lcol = lax.broadcasted_iota(jnp.int32, (LANES, DA_WIDTH), 1)
    return jnp.where(lcol // DA_DIM == hrow, 1.0, 0.0).astype(F32)


def _ffn_fwd(tag, h_in, ln, sh, sc, gt, wgu, wd, weight):
    d = h_in.shape[1]
    ff = wd.shape[0]
    (a,) = _rowwise(tag + "_norm", lambda h, l, s1, s2: ((_rms_mod(h, l, s1, s2),), ()), [h_in], [ln, sh, sc],
                    [(d, BF16)])
    (gu,) = _matmul(tag + "_gu", a, wgu, outs=(BF16,))
    (f,) = _rowwise(tag + "_act", lambda g, u: ((_swiglu_act(g, u),), ()), [(gu, ff, 0), (gu, ff, 1)], [],
                    [(ff, BF16)])
    h_out, o = _matmul(tag + "_down", f, wd, outs=(F32, BF16), epi_rows=[h_in], epi_bcast=[gt],
                       epi=lambda acc, h, g: (h + weight * g * acc, acc))
    return h_out, dict(a=a, gu=gu, f=f, o=o)


def _resid_bwd(tag, dh_out, o, gt, weight):
    d = dh_out.shape[1]

    def fn(dh, ov, g):
        return (weight * g * dh,), (jnp.sum(weight * dh * ov.astype(F32), axis=0, keepdims=True),)

    do, d_gt = _rowwise(tag + "_resid_bwd", fn, [dh_out, o], [gt], [(d, BF16)], [(1, d)])
    return do, d_gt


def _norm_bwd(tag, h_in, da, dh_out, ln, sh, sc):
    d = h_in.shape[1]

    def fn(h, dav, dh, l, s1, s2):
        _, vjp = jax.vjp(_rms_mod, h, l, s1, s2)
        gh, gl, gs1, gs2 = vjp(dav)
        return (dh + gh,), (gl, gs1, gs2)

    return _rowwise(tag + "_norm_bwd", fn, [h_in, da, dh_out], [ln, sh, sc], [(d, F32)], [(1, d)] * 3)


def _ffn_bwd(tag, h_in, dh_out, sv, ln, sh, sc, gt, wgu, wd, weight):
    ff = wd.shape[0]
    do, d_gt = _resid_bwd(tag, dh_out, sv["o"], gt, weight)
    (df,) = _matmul(tag + "_down_dx", do, wd, tb=True, outs=(BF16,))
    (d_wd,) = _matmul(tag + "_down_dw", sv["f"], do, ta=True)

    def act_bwd(dfv, g, u):
        _, vjp = jax.vjp(_swiglu_act, g.astype(F32), u.astype(F32))
        dg, du = vjp(dfv.astype(F32))
        return (jnp.concatenate([dg, du], axis=1),), ()

    (dgu,) = _rowwise(tag + "_act_bwd", act_bwd, [df, (sv["gu"], ff, 0), (sv["gu"], ff, 1)], [], [(2 * ff, BF16)])
    (da,) = _matmul(tag + "_gu_dx", dgu, wgu, tb=True)
    (d_wgu,) = _matmul(tag + "_gu_dw", sv["a"], dgu, ta=True)
    dh_in, d_ln, d_sh, d_sc = _norm_bwd(tag, h_in, da, dh_out, ln, sh, sc)
    return dh_in, dict(wgu=d_wgu, wd=d_wd), dict(ln=d_ln, sh=d_sh, sc=d_sc, gt=d_gt)


def _mixer_fwd(tag, h_in, ln, sh, sc, gt, w, sp):
    d = h_in.shape[1]
    (a,) = _rowwise(tag + "_norm", lambda h, l, s1, s2: ((_rms_mod(h, l, s1, s2),), ()), [h_in], [ln, sh, sc],
                    [(d, BF16)])
    (pq,) = _matmul(tag + "_pq", a, w["wq"], outs=(BF16,))
    (pz,) = _matmul(tag + "_pz", a, w["wz"], outs=(BF16,))
    (pba,) = _matmul(tag + "_pba", a, w["wba"])
    (pda,) = _matmul(tag + "_pda", a, w["wda"], outs=(BF16,))
    (pg,) = _matmul(tag + "_pg", a, w["wg"], outs=(BF16,))
    yc, qkvn, gb = _conv_prep_fwd(tag + "_conv", pq, sp["conv8"], pba, sp["alog"], sp["dtb"])
    o_a, states = _delta_fwd(tag + "_delta", qkvn, gb)
    (o_an,) = _rowwise(tag + "_dnorm", lambda o, z, dn: ((_dn_outnorm(o, z.astype(F32), dn),), ()), [o_a, pz],
                       [sp["dn"]], [(DN_WIDTH, BF16)])
    ops, lses = [], []
    for (_, r) in DA_PATTERNS:
        o_p, lse_p = _da_fwd(f"{tag}_da{r}", pda, r)
        ops.append(o_p)
        lses.append(lse_p)

    def merge(o1, o2, o3, l1, l2, l3):
        mx = jnp.maximum(jnp.maximum(l1, l2), l3)
        e1, e2, e3 = jnp.exp(l1 - mx), jnp.exp(l2 - mx), jnp.exp(l3 - mx)
        tot = e1 + e2 + e3
        ex = _head_expand()
        up = lambda wgt: lax.dot_general(wgt / tot, ex, NN, precision=HI, preferred_element_type=F32)
        return (up(e1) * o1 + up(e2) * o2 + up(e3) * o3, mx + jnp.log(tot)), ()

    o_b, lse_tot = _rowwise(tag + "_merge", merge, ops + lses, [], [(DA_WIDTH, BF16), (LANES, F32)])
    (y_a,) = _matmul(tag + "_wa", o_an, w["w_a"], outs=(BF16,))
    (y_b,) = _matmul(tag + "_wb", o_b, w["w_b"], outs=(BF16,))

    def gate(ga, gbv, ya, yb):
        return _sigmoid(ga.astype(F32)) * ya.astype(F32) + _sigmoid(gbv.astype(F32)) * yb.astype(F32)

    (merged,) = _rowwise(tag + "_gate", lambda *v: ((gate(*v),), ()), [(pg, d, 0), (pg, d, 1), y_a, y_b], [],
                         [(d, BF16)])
    h_out, m = _matmul(tag + "_wo", merged, w["w_o"], outs=(F32, BF16), epi_rows=[h_in], epi_bcast=[gt],
                       epi=lambda acc, h, g: (h + g * acc, acc))
    sv = dict(a=a, pq=pq, pz=pz, pba=pba, pda=pda, pg=pg, yc=yc, qkvn=qkvn, gb=gb, o_a=o_a, states=states,
              o_an=o_an, o_b=o_b, lse=lse_tot, y_a=y_a, y_b=y_b, merged=merged, m=m, gate=gate)
    return h_out, sv


def _mixer_bwd(tag, h_in, dh_out, sv, ln, sh, sc, gt, w, sp):
    d = h_in.shape[1]
    dm, d_gt = _resid_bwd(tag, dh_out, sv["m"], gt, 1.0)
    (d_merged,) = _matmul(tag + "_wo_dx", dm, w["w_o"], tb=True, outs=(BF16,))
    (d_wo,) = _matmul(tag + "_wo_dw", sv["merged"], dm, ta=True)
    gate = sv["gate"]

    def gate_bwd(dmg, ga, gbv, ya, yb):
        _, vjp = jax.vjp(gate, ga.astype(F32), gbv.astype(F32), ya.astype(F32), yb.astype(F32))
        dga, dgb, dya, dyb = vjp(dmg.astype(F32))
        return (jnp.concatenate([dga, dgb], axis=1), dya, dyb), ()

    pg = sv["pg"]
    d_pg, d_ya, d_yb = _rowwise(tag + "_gate_bwd", gate_bwd, [d_merged, (pg, d, 0), (pg, d, 1), sv["y_a"], sv["y_b"]],
                                [], [(2 * d, BF16), (d, BF16), (d, BF16)])
    (d_oan,) = _matmul(tag + "_wa_dx", d_ya, w["w_a"], tb=True)
    (d_wa,) = _matmul(tag + "_wa_dw", sv["o_an"], d_ya, ta=True)
    (d_ob,) = _matmul(tag + "_wb_dx", d_yb, w["w_b"], tb=True, outs=(BF16,))
    (d_wb,) = _matmul(tag + "_wb_dw", sv["o_b"], d_yb, ta=True)

    def dnorm_bwd(doan, o, z, dn):
        _, vjp = jax.vjp(_dn_outnorm, o, z.astype(F32), dn)
        go, gz, gdn = vjp(doan)
        return (go, gz), (gdn,)

    d_oa, d_pz, d_dn = _rowwise(tag + "_dnorm_bwd", dnorm_bwd, [d_oan, sv["o_a"], sv["pz"]], [sp["dn"]],
                                [(DN_WIDTH, F32), (DN_WIDTH, BF16)], [(1, DN_DIM)])
    d_qkvn, d_gb = _delta_bwd(tag + "_delta_bwd", sv["qkvn"], sv["gb"], sv["states"], d_oa)

    def prep_bwd(dq, dgbv, yc, pba, alog, dtb):
        _, vjp = jax.vjp(_dn_prep, yc.astype(F32), pba, alog, dtb)
        gyc, gpba, galog, gdtb = vjp((dq, dgbv))
        return (gyc, gpba), (galog, gdtb)

    d_yc, d_pba, d_alog, d_dtb = _rowwise(tag + "_prep_bwd", prep_bwd, [d_qkvn, d_gb, sv["yc"], sv["pba"]],
                                          [sp["alog"], sp["dtb"]], [(3 * DN_WIDTH, BF16), (LANES, BF16)],
                                          [(1, LANES), (1, LANES)], bm=128)
    d_pq, d_conv = _conv_bwd(tag + "_conv_bwd", d_yc, sv["pq"], sp["conv8"])

    def delta_fn(dob, ob):
        prod = dob.astype(F32) * ob.astype(F32)
        return (lax.dot_general(prod, _head_expand(), NT, precision=HI, preferred_element_type=F32),), ()

    (delta,) = _rowwise(tag + "_da_delta", delta_fn, [d_ob, sv["o_b"]], [], [(LANES, F32)])
    grads = [_da_bwd(f"{tag}_da{r}_bwd", sv["pda"], d_ob, sv["lse"], delta, r) for (_, r) in DA_PATTERNS]

    def sum3(q1, k1, v1, q2, k2, v2, q3, k3, v3):
        return (jnp.concatenate([q1 + q2 + q3, k1 + k2 + k3, v1 + v2 + v3], axis=1),), ()

    (d_pda,) = _rowwise(tag + "_da_sum", sum3, [t for g in grads for t in g], [], [(3 * DA_WIDTH, BF16)])

    a = sv["a"]
    (da,) = _matmul(tag + "_pq_dx", d_pq, w["wq"], tb=True)
    add = lambda acc, prev: (acc + prev,)
    (da,) = _matmul(tag + "_pz_dx", d_pz, w["wz"], tb=True, epi_rows=[da], epi=add)
    (da,) = _matmul(tag + "_pba_dx", d_pba, w["wba"], tb=True, epi_rows=[da], epi=add)
    (da,) = _matmul(tag + "_pda_dx", d_pda, w["wda"], tb=True, epi_rows=[da], epi=add)
    (da,) = _matmul(tag + "_pg_dx", d_pg, w["wg"], tb=True, epi_rows=[da], epi=add)
    (d_wq,) = _matmul(tag + "_pq_dw", a, d_pq, ta=True)
    (d_wz,) = _matmul(tag + "_pz_dw", a, d_pz, ta=True)
    (d_wba,) = _matmul(tag + "_pba_dw", a, d_pba, ta=True)
    (d_wda,) = _matmul(tag + "_pda_dw", a, d_pda, ta=True)
    (d_wg,) = _matmul(tag + "_pg_dw", a, d_pg, ta=True)
    dh_in, d_ln, d_sh, d_sc = _norm_bwd(tag, h_in, da, dh_out, ln, sh, sc)
    wgrads = dict(wq=d_wq, wz=d_wz, wba=d_wba, wda=d_wda, wg=d_wg, w_a=d_wa, w_b=d_wb, w_o=d_wo)
    small = dict(ln=d_ln, sh=d_sh, sc=d_sc, gt=d_gt, dn=d_dn, alog=d_alog, dtb=d_dtb, conv=d_conv)
    return dh_in, wgrads, small


def _loss_head(h, target, fnorm):
    d = h.shape[1]

    def fn(hv, tv, fw):
        def lossf(hh, ww):
            y = hh * lax.rsqrt(jnp.mean(hh * hh, axis=-1, keepdims=True) + NORM_EPS) * ww
            return 0.5 * jnp.sum(jnp.mean(jnp.square(y - tv), axis=-1))

        val, (dh, dw) = jax.value_and_grad(lossf, argnums=(0, 1))(hv, fw)
        return (dh,), (jnp.full((1, LANES), val, F32), dw)

    return _rowwise("loss_head", fn, [h, target], [fnorm], [(d, F32)], [(1, LANES), (1, d)])


def _row(v):
    return v.reshape(1, -1)


def _pad_lanes(v, offset):
    return jnp.zeros((1, LANES), F32).at[0, offset:offset + v.shape[0]].set(v)


def _local_step(x2, target, mod, wl, small):
    depth = mod.shape[0]
    d = x2.shape[1]
    h = x2
    saved = []
    mods = []
    for l in range(depth):
        m9 = [_row(mod[l, i * d:(i + 1) * d]) for i in range(N_ADA)]
        sp = dict(conv8=jnp.zeros((8, 3 * DN_WIDTH), F32).at[:DN_CONV].set(small["conv_w"][l]),
                  alog=_pad_lanes(small["a_log"][l], DN_HEADS), dtb=_pad_lanes(small["dt_bias"][l], DN_HEADS),
                  dn=_row(small["dn_norm"][l]))
        w = wl[l]
        h0 = h
        h1, sv1 = _ffn_fwd(f"l{l}_ffn1", h0, _row(small["ln_ffn1"][l]), m9[0], m9[1], m9[2], w["wgu1"], w["wd1"], 0.5)
        h2, sv2 = _mixer_fwd(f"l{l}_mix", h1, _row(small["ln_mix"][l]), m9[3], m9[4], m9[5], w, sp)
        h3, sv3 = _ffn_fwd(f"l{l}_ffn2", h2, _row(small["ln_ffn2"][l]), m9[6], m9[7], m9[8], w["wgu2"], w["wd2"], 0.5)
        saved.append((h0, h1, h2, sv1, sv2, sv3, sp))
        mods.append(m9)
        h = h3
    dh, loss_part, d_fnorm = _loss_head(h, target, _row(small["final_norm"]))
    wgrads, sgrads, dmods = [], [], []
    for l in reversed(range(depth)):
        h0, h1, h2, sv1, sv2, sv3, sp = saved[l]
        m9 = mods[l]
        w = wl[l]
        dh, g3, s3 = _ffn_bwd(f"l{l}_ffn2", h2, dh, sv3, _row(small["ln_ffn2"][l]), m9[6], m9[7], m9[8],
                              w["wgu2"], w["wd2"], 0.5)
        dh, g2, s2 = _mixer_bwd(f"l{l}_mix", h1, dh, sv2, _row(small["ln_mix"][l]), m9[3], m9[4], m9[5], w, sp)
        dh, g1, s1 = _ffn_bwd(f"l{l}_ffn1", h0, dh, sv1, _row(small["ln_ffn1"][l]), m9[0], m9[1], m9[2],
                              w["wgu1"], w["wd1"], 0.5)
        wgrads.append(dict(wgu1=g1["wgu"], wd1=g1["wd"], wgu2=g3["wgu"], wd2=g3["wd"], **g2))
        dmods.append(jnp.concatenate([s1["sh"], s1["sc"], s1["gt"], s2["sh"], s2["sc"], s2["gt"],
                                      s3["sh"], s3["sc"], s3["gt"]], axis=1))
        sgrads.append(dict(ln_ffn1=s1["ln"][0], ln_mix=s2["ln"][0], ln_ffn2=s3["ln"][0],
                           a_log=s2["alog"][0, DN_HEADS:2 * DN_HEADS], dt_bias=s2["dtb"][0, DN_HEADS:2 * DN_HEADS],
                           dn_norm=s2["dn"][0], conv_w=s2["conv"][:DN_CONV]))
    wgrads.reverse()
    sgrads.reverse()
    dmods.reverse()
    return loss_part[0, 0], dh, jnp.concatenate(dmods, axis=0), wgrads, sgrads, d_fnorm[0]


def _flip(v, bit):
    return 1 - v if bit else v


def _allgather8(name, x):
    r, c = x.shape

    def body(x_ref, out_ref, send_sems, recv_sems, local_sem):
        mx, my, mc = lax.axis_index("x"), lax.axis_index("y"), lax.axis_index("c")
        me = 4 * mx + 2 * my + mc
        mine = pltpu.make_async_copy(x_ref, out_ref.at[me], local_sem)
        mine.start()
        sends = []
        for k in range(1, 8):
            peer = (_flip(mx, k & 4), _flip(my, k & 2), _flip(mc, k & 1))
            cp = pltpu.make_async_remote_copy(src_ref=x_ref, dst_ref=out_ref.at[me], send_sem=send_sems.at[k - 1],
                                              recv_sem=recv_sems.at[k - 1], device_id=peer, device_id_type=MESH)
            cp.start()
            sends.append(cp)
        for k in range(1, 8):
            peer = (_flip(mx, k & 4), _flip(my, k & 2), _flip(mc, k & 1))
            src = 4 * peer[0] + 2 * peer[1] + peer[2]
            pltpu.make_async_remote_copy(src_ref=x_ref, dst_ref=out_ref.at[src], send_sem=send_sems.at[k - 1],
                                         recv_sem=recv_sems.at[k - 1], device_id=peer, device_id_type=MESH).wait_recv()
        for cp in sends:
            cp.wait_send()
        mine.wait()

    return pl.pallas_call(
        body, name=name, out_shape=jax.ShapeDtypeStruct((8, r, c), x.dtype),
        in_specs=[pl.BlockSpec(memory_space=pltpu.VMEM)], out_specs=pl.BlockSpec(memory_space=pltpu.VMEM),
        scratch_shapes=[pltpu.SemaphoreType.DMA((7,)), pltpu.SemaphoreType.DMA((7,)), pltpu.SemaphoreType.DMA],
        compiler_params=_params(9 * _nbytes((r, c), x.dtype)),
    )(x)


def _chip_peers(mx, my):
    chips = [(1 - mx, my), (mx, 1 - my), (1 - mx, 1 - my)]
    return chips, [2 * cx + cy for (cx, cy) in chips]


def _gather_shards(name, blob):
    r, wd_ = blob.shape
    half = r // 2

    def body(x_ref, out_ref, send_sems, recv_sems, local_sem):
        mx, my, mc = lax.axis_index("x"), lax.axis_index("y"), lax.axis_index("c")
        j = 2 * mx + my
        chips, idxs = _chip_peers(mx, my)
        mine_rows = pl.ds(pl.multiple_of(mc * half, 16), half)
        sib_rows = pl.ds(pl.multiple_of((1 - mc) * half, 16), half)
        local = pltpu.make_async_copy(x_ref, out_ref.at[j], local_sem)
        local.start()

        def copy(k, src, dst, to):
            return pltpu.make_async_remote_copy(src_ref=src, dst_ref=dst, send_sem=send_sems.at[k],
                                                recv_sem=recv_sems.at[k], device_id=to, device_id_type=MESH)

        first = [copy(t, x_ref.at[mine_rows], out_ref.at[j, mine_rows], (*chip, mc)) for t, chip in enumerate(chips)]
        for cp in first:
            cp.start()
        passed = []
        for t, chip in enumerate(chips):
            landed = out_ref.at[idxs[t], mine_rows]
            copy(t, landed, landed, (*chip, mc)).wait_recv()
            fwd = copy(3 + t, landed, landed, (mx, my, 1 - mc))
            fwd.start()
            passed.append(fwd)
        for t in range(3):
            theirs = out_ref.at[idxs[t], sib_rows]
            copy(3 + t, theirs, theirs, (mx, my, 1 - mc)).wait_recv()
        for cp in first + passed:
            cp.wait_send()
        local.wait()

    return pl.pallas_call(
        body, name=name, out_shape=jax.ShapeDtypeStruct((4, r, wd_), blob.dtype),
        in_specs=[pl.BlockSpec(memory_space=pl.ANY)], out_specs=pl.BlockSpec(memory_space=pl.ANY),
        scratch_shapes=[pltpu.SemaphoreType.DMA((6,)), pltpu.SemaphoreType.DMA((6,)), pltpu.SemaphoreType.DMA],
    )(blob)


def _swap_sibling(name, x):
    def body(x_ref, out_ref, send_sem, recv_sem):
        sib = (lax.axis_index("x"), lax.axis_index("y"), 1 - lax.axis_index("c"))
        cp = pltpu.make_async_remote_copy(src_ref=x_ref, dst_ref=out_ref, send_sem=send_sem, recv_sem=recv_sem,
                                          device_id=sib, device_id_type=MESH)
        cp.start()
        cp.wait()

    return pl.pallas_call(
        body, name=name, out_shape=jax.ShapeDtypeStruct(x.shape, x.dtype),
        in_specs=[pl.BlockSpec(memory_space=pl.ANY)], out_specs=pl.BlockSpec(memory_space=pl.ANY),
        scratch_shapes=[pltpu.SemaphoreType.DMA, pltpu.SemaphoreType.DMA],
    )(x)


def _scatter_chips(name, p):
    def body(p_ref, out_ref, send_sems, recv_sems, local_sem):
        mx, my, mc = lax.axis_index("x"), lax.axis_index("y"), lax.axis_index("c")
        j = 2 * mx + my
        chips, idxs = _chip_peers(mx, my)
        local = pltpu.make_async_copy(p_ref.at[j], out_ref.at[j], local_sem)
        local.start()
        sends = []
        for t, chip in enumerate(chips):
            cp = pltpu.make_async_remote_copy(src_ref=p_ref.at[idxs[t]], dst_ref=out_ref.at[j],
                                              send_sem=send_sems.at[t], recv_sem=recv_sems.at[t],
                                              device_id=(*chip, mc), device_id_type=MESH)
            cp.start()
            sends.append(cp)
        for t, chip in enumerate(chips):
            pltpu.make_async_remote_copy(src_ref=p_ref.at[idxs[t]], dst_ref=out_ref.at[idxs[t]],
                                         send_sem=send_sems.at[t], recv_sem=recv_sems.at[t],
                                         device_id=(*chip, mc), device_id_type=MESH).wait_recv()
        for cp in sends:
            cp.wait_send()
        local.wait()

    return pl.pallas_call(
        body, name=name, out_shape=jax.ShapeDtypeStruct(p.shape, p.dtype),
        in_specs=[pl.BlockSpec(memory_space=pl.ANY)], out_specs=pl.BlockSpec(memory_space=pl.ANY),
        scratch_shapes=[pltpu.SemaphoreType.DMA((3,)), pltpu.SemaphoreType.DMA((3,)), pltpu.SemaphoreType.DMA],
    )(p)


def _sum_slabs(name, parts, out_dtype, bm=336):
    if len(parts) == 1:
        nsl, hh, ww = parts[0].shape
        bm = _pick(hh, (bm, 256, 128, 64, 32, 16))

        def body(p_ref, o_ref):
            acc = p_ref[0].astype(F32)
            for s in range(1, nsl):
                acc = acc + p_ref[s].astype(F32)
            o_ref[...] = acc.astype(o_ref.dtype)

        return pl.pallas_call(
            body, name=name, grid=(hh // bm,), in_specs=[pl.BlockSpec((nsl, bm, ww), lambda i: (0, i, 0))],
            out_specs=pl.BlockSpec((bm, ww), lambda i: (i, 0)), out_shape=jax.ShapeDtypeStruct((hh, ww), out_dtype),
            compiler_params=_params(_nbytes((nsl, bm, ww), parts[0].dtype) + _nbytes((bm, ww), F32)),
        )(parts[0])
    nsl, hh, ww = parts[0].shape
    bm = _pick(hh, (bm, 256, 128, 64, 32, 16))

    def body2(a_ref, b_ref, o_ref):
        o_ref[...] = (a_ref[...].astype(F32) + b_ref[...].astype(F32)).astype(o_ref.dtype)

    spec = pl.BlockSpec((nsl, bm, ww), lambda i: (0, i, 0))
    return pl.pallas_call(
        body2, name=name, grid=(hh // bm,), in_specs=[spec, spec], out_specs=spec,
        out_shape=jax.ShapeDtypeStruct((nsl, hh, ww), out_dtype),
        compiler_params=_params(3 * _nbytes((nsl, bm, ww), F32)),
    )(parts[0], parts[1])


def _reduce_scatter(gblob):
    _, r, wd_ = gblob.shape
    half = r // 2
    mc = lax.axis_index("c")
    g2 = gblob.reshape(4, 2, half, wd_)
    mine = lax.dynamic_index_in_dim(g2, mc, axis=1, keepdims=False)
    other = lax.dynamic_index_in_dim(g2, 1 - mc, axis=1, keepdims=False)
    got = _swap_sibling("rs_pair_swap", other)
    pair = _sum_slabs("rs_pair_sum", [mine, got], BF16)
    chips = _scatter_chips("rs_chip_scatter", pair)
    fin = _sum_slabs("rs_chip_sum", [chips], F32)
    theirs = _swap_sibling("rs_final_swap", fin)
    lo = jnp.where(mc == 0, fin, theirs)
    hi = jnp.where(mc == 0, theirs, fin)
    return jnp.concatenate([lo, hi], axis=0)


_BIG = (("ffn1_wg", 1), ("ffn1_wu", 1), ("ffn1_wd", 0), ("w_in", 1), ("w_a", 0), ("w_b", 1), ("w_o", 0),
        ("ffn2_wg", 1), ("ffn2_wu", 1), ("ffn2_wd", 0))
_PACK_W = 1024
_PACK_ALIGN = 32


def _pack(shards, dtype):
    depth = shards[_BIG[0][0]].shape[0]
    pieces = []
    for l in range(depth):
        for name, _ in _BIG:
            pieces.append(shards[name][l].astype(dtype).reshape(-1, _PACK_W))
    rows = sum(p.shape[0] for p in pieces)
    pad = (-rows) % _PACK_ALIGN
    if pad:
        pieces.append(jnp.zeros((pad, _PACK_W), dtype))
    return jnp.concatenate(pieces, axis=0)


def _unpack(blob, shapes):
    depth = shapes[_BIG[0][0]][0]
    out = {name: [] for name, _ in _BIG}
    off = 0
    for l in range(depth):
        for name, _ in _BIG:
            _, rr, ww = shapes[name]
            nrow = rr * ww // _PACK_W
            out[name].append(blob[off:off + nrow].reshape(rr, ww))
            off += nrow
    return {k: jnp.stack(v, axis=0) for k, v in out.items()}


def _in_cols(d):
    o1 = 3 * DN_WIDTH
    o2 = o1 + DN_WIDTH
    o3 = o2 + 2 * DN_HEADS
    o4 = o3 + 3 * DA_WIDTH
    return dict(wq=(0, o1), wz=(o1, o2), wba=(o2, o3), wda=(o3, o4), wg=(o4, o4 + 2 * d))


def _layer_weights(full, l, d):
    cols = _in_cols(d)
    w_in = full["w_in"][l]
    w = {k: w_in[:, a:b] for k, (a, b) in cols.items()}
    w["wba"] = jnp.pad(w["wba"], ((0, 0), (0, LANES - 2 * DN_HEADS)))
    w["wgu1"] = jnp.concatenate([full["ffn1_wg"][l], full["ffn1_wu"][l]], axis=1)
    w["wgu2"] = jnp.concatenate([full["ffn2_wg"][l], full["ffn2_wu"][l]], axis=1)
    w["wd1"], w["wd2"] = full["ffn1_wd"][l], full["ffn2_wd"][l]
    w["w_a"], w["w_b"], w["w_o"] = full["w_a"][l], full["w_b"][l], full["w_o"][l]
    return w


def _full_grads(wg, d):
    ff = wg["wd1"].shape[0]
    w_in = jnp.concatenate([wg["wq"], wg["wz"], wg["wba"][:, :2 * DN_HEADS], wg["wda"], wg["wg"]], axis=1)
    return dict(ffn1_wg=wg["wgu1"][:, :ff], ffn1_wu=wg["wgu1"][:, ff:], ffn1_wd=wg["wd1"], w_in=w_in,
                w_a=wg["w_a"], w_b=wg["w_b"], w_o=wg["w_o"], ffn2_wg=wg["wgu2"][:, :ff], ffn2_wu=wg["wgu2"][:, ff:],
                ffn2_wd=wg["wd2"])


def _adamw(name, w, g, m, v):
    shape = w.shape
    cols = shape[-1]
    w2, g2, m2, v2 = (t.reshape(-1, cols) for t in (w, g, m, v))

    def fn(wv, gv, mv, vv):
        mn = ADAM_B1 * mv + (1.0 - ADAM_B1) * gv
        vn = ADAM_B2 * vv + (1.0 - ADAM_B2) * jnp.square(gv)
        m_hat = mn / (1.0 - ADAM_B1 ** ADAM_STEP)
        v_hat = vn / (1.0 - ADAM_B2 ** ADAM_STEP)
        delta = -ADAM_LR * (m_hat / (jnp.sqrt(v_hat) + ADAM_EPS) + ADAM_WD * wv)
        return (delta, mn, vn), ()

    rows = w2.shape[0]
    bm = _pick(rows, (256, 128, 64, 32, 16, 8)) if rows >= 8 else rows
    delta, mn, vn = _rowwise(name, fn, [w2, g2, m2, v2], [], [(cols, F32)] * 3, bm=bm)
    return delta.reshape(shape), mn.reshape(shape), vn.reshape(shape)


def kernel(x, c, ada_w, ada_b, ln_ffn1, ln_mix, ln_ffn2, ffn1_wg, ffn1_wu, ffn1_wd, w_in, conv_w, a_log, dt_bias, dn_norm, w_a, w_b, w_o, ffn2_wg, ffn2_wu, ffn2_wd, final_norm, loss_target, m_ada_w, m_ada_b, m_ln_ffn1, m_ln_mix, m_ln_ffn2, m_ffn1_wg, m_ffn1_wu, m_ffn1_wd, m_w_in, m_conv_w, m_a_log, m_dt_bias, m_dn_norm, m_w_a, m_w_b, m_w_o, m_ffn2_wg, m_ffn2_wu, m_ffn2_wd, m_final_norm, v_ada_w, v_ada_b, v_ln_ffn1, v_ln_mix, v_ln_ffn2, v_ffn1_wg, v_ffn1_wu, v_ffn1_wd, v_w_in, v_conv_w, v_a_log, v_dt_bias, v_dn_norm, v_w_a, v_w_b, v_w_o, v_ffn2_wg, v_ffn2_wu, v_ffn2_wd, v_final_norm):
    names = ["ada_w", "ada_b", "ln_ffn1", "ln_mix", "ln_ffn2", "ffn1_wg", "ffn1_wu", "ffn1_wd", "w_in", "conv_w",
             "a_log", "dt_bias", "dn_norm", "w_a", "w_b", "w_o", "ffn2_wg", "ffn2_wu", "ffn2_wd", "final_norm"]
    wts = dict(zip(names, (ada_w, ada_b, ln_ffn1, ln_mix, ln_ffn2, ffn1_wg, ffn1_wu, ffn1_wd, w_in, conv_w, a_log,
                           dt_bias, dn_norm, w_a, w_b, w_o, ffn2_wg, ffn2_wu, ffn2_wd, final_norm)))
    mom = dict(zip(names, (m_ada_w, m_ada_b, m_ln_ffn1, m_ln_mix, m_ln_ffn2, m_ffn1_wg, m_ffn1_wu, m_ffn1_wd, m_w_in,
                           m_conv_w, m_a_log, m_dt_bias, m_dn_norm, m_w_a, m_w_b, m_w_o, m_ffn2_wg, m_ffn2_wu,
                           m_ffn2_wd, m_final_norm)))
    var = dict(zip(names, (v_ada_w, v_ada_b, v_ln_ffn1, v_ln_mix, v_ln_ffn2, v_ffn1_wg, v_ffn1_wu, v_ffn1_wd, v_w_in,
                           v_conv_w, v_a_log, v_dt_bias, v_dn_norm, v_w_a, v_w_b, v_w_o, v_ffn2_wg, v_ffn2_wu,
                           v_ffn2_wd, v_final_norm)))
    _, s, d = x.shape
    depth = ada_w.shape[0]
    mx, my, mc = lax.axis_index("x"), lax.axis_index("y"), lax.axis_index("c")
    chip = 2 * mx + my
    me = 2 * chip + mc
    nshard = ada_w.shape[2]

    cact = _rowwise("c_silu", lambda cv: ((_silu(cv),), ()), [jnp.pad(c, ((0, 7), (0, 0)))], [], [(d, F32)], bm=8)[0]
    c_all = _allgather8("ag_c", cact)[:, 0, :]
    conv_all = _allgather8("ag_conv", jnp.pad(conv_w.reshape(depth * DN_CONV, -1), ((0, 8 - depth * DN_CONV), (0, 0))))
    conv_full = jnp.concatenate([conv_all[2 * j, :depth * DN_CONV] for j in range(4)], axis=1)
    conv_full = conv_full.reshape(depth, DN_CONV, 3 * DN_WIDTH)
    big_shapes = {name: wts[name].shape for name, _ in _BIG}
    gathered = _gather_shards("ag_weights", _pack({name: wts[name] for name, _ in _BIG}, BF16))
    per_chip = [_unpack(gathered[j], big_shapes) for j in range(4)]
    full = {name: jnp.concatenate([pc[name] for pc in per_chip], axis=1 + ax) for name, ax in _BIG}

    c16 = jnp.pad(c_all, ((0, 8), (0, 0))).astype(BF16)
    parts = []
    for l in range(depth):
        bias = lax.dynamic_slice(ada_b[l], (chip * nshard,), (nshard,)).reshape(1, nshard)
        (mp,) = _matmul(f"ada_fwd{l}", c16, ada_w[l].astype(BF16), epi_bcast=[bias], epi=lambda acc, b: (acc + b,))
        parts.append(mp)
    mod_all = _allgather8("ag_mod", jnp.concatenate(parts, axis=0))
    mod_rows = jnp.concatenate([mod_all[2 * j] for j in range(4)], axis=1)
    mod = jnp.stack([lax.dynamic_index_in_dim(mod_rows, l * 16 + me, axis=0, keepdims=False) for l in range(depth)])

    wl = [_layer_weights(full, l, d) for l in range(depth)]
    small = dict(conv_w=conv_full, a_log=a_log, dt_bias=dt_bias, dn_norm=dn_norm, ln_ffn1=ln_ffn1, ln_mix=ln_mix,
                 ln_ffn2=ln_ffn2, final_norm=final_norm)
    loss_part, dx, dmod, wgrads, sgrads, d_fnorm = _local_step(x[0], loss_target[0], mod, wl, small)

    dmod_all = _allgather8("ag_dmod", jnp.pad(dmod, ((0, 8 - depth), (0, 0))))
    g_ada_w, g_ada_b = [], []
    for l in range(depth):
        dm_l = dmod_all[:, l, :]
        (gb_l,) = _rowwise(f"ada_b_grad{l}", lambda v: ((), (jnp.sum(v, axis=0, keepdims=True),)), [dm_l], [], [],
                           [(1, N_ADA * d)], bm=8)
        g_ada_b.append(gb_l[0])
        dm_sh = lax.dynamic_slice(dm_l, (0, chip * nshard), (8, nshard))
        (gw_l,) = _matmul(f"ada_w_grad{l}", c16, jnp.pad(dm_sh, ((0, 8), (0, 0))).astype(BF16), ta=True)
        g_ada_w.append(gw_l)
    grads = dict(ada_w=jnp.stack(g_ada_w), ada_b=jnp.stack(g_ada_b))

    smalls = [loss_part.reshape(1), d_fnorm]
    for l in range(depth):
        sg = sgrads[l]
        smalls += [sg["ln_ffn1"], sg["ln_mix"], sg["ln_ffn2"], sg["a_log"], sg["dt_bias"], sg["dn_norm"],
                   sg["conv_w"].reshape(-1)]
    sizes = [t.shape[0] for t in smalls]
    flat = jnp.concatenate(smalls)
    rows = -(-flat.shape[0] // (8 * LANES)) * 8
    flat = jnp.pad(flat, (0, rows * LANES - flat.shape[0])).reshape(rows, LANES)
    tot = _sum_slabs("small_sum", [_allgather8("ag_small", flat)], F32, bm=rows).reshape(-1)
    offs, acc = [], 0
    for n_ in sizes:
        offs.append(acc)
        acc += n_
    take = lambda i: tot[offs[i]:offs[i] + sizes[i]]
    loss = take(0)[0]
    grads["final_norm"] = take(1)
    per = 7
    for key_i, key in enumerate(["ln_ffn1", "ln_mix", "ln_ffn2", "a_log", "dt_bias", "dn_norm"]):
        grads[key] = jnp.stack([take(2 + per * l + key_i) for l in range(depth)])
    conv_g = jnp.stack([take(2 + per * l + 6).reshape(DN_CONV, 3 * DN_WIDTH) for l in range(depth)])
    csh = conv_w.shape[2]
    grads["conv_w"] = lax.dynamic_slice(conv_g, (0, 0, chip * csh), (depth, DN_CONV, csh))

    fg = [_full_grads(wgrads[l], d) for l in range(depth)]
    slabs = []
    for j in range(4):
        sh = {}
        for name, ax in _BIG:
            n_sh = big_shapes[name][1 + ax]
            sh[name] = jnp.stack([lax.slice_in_dim(fg[l][name], j * n_sh, (j + 1) * n_sh, axis=ax) for l in range(depth)])
        slabs.append(_pack(sh, BF16))
    reduced = _reduce_scatter(jnp.stack(slabs))
    grads.update(_unpack(reduced, big_shapes))

    deltas, new_m, new_v = {}, {}, {}
    for name in names:
        wv, gv, mv, vv = wts[name], grads[name], mom[name], var[name]
        if wv.ndim == 1:
            wv, gv, mv, vv = (t.reshape(-1, LANES) for t in (wv, gv, mv, vv))
        dl, mn, vn = _adamw("adamw_" + name, wv, gv, mv, vv)
        deltas[name], new_m[name], new_v[name] = (t.reshape(wts[name].shape) for t in (dl, mn, vn))
    return (loss, dx.reshape(1, s, d), *[grads[n_] for n_ in names], *[deltas[n_] for n_ in names],
            *[new_m[n_] for n_ in names], *[new_v[n_] for n_ in names])
```

```python
import functools

import jax
import jax.numpy as jnp
from jax import lax
from jax.experimental import pallas as pl
from jax.experimental.pallas import tpu as pltpu

F32 = jnp.float32
BF16 = jnp.bfloat16
MESH = pl.DeviceIdType.MESH

NORM_EPS = 1e-6
DN_HEADS, DN_DIM, DN_CHUNK, DN_CONV = 8, 128, 64, 4
DN_WIDTH = DN_HEADS * DN_DIM
DA_HEADS, DA_DIM, DA_BLOCK = 12, 64, 128
DA_WIDTH = DA_HEADS * DA_DIM
DA_PATTERNS = ((128, 1), (512, 4), (2048, 16))
ALIBI_MAX_EXP = 8.0
N_ADA = 9
LANES = 128
V7X_VMEM_BYTES = 64 << 20
ADAM_LR, ADAM_B1, ADAM_B2, ADAM_EPS, ADAM_WD, ADAM_STEP = 0.001, 0.9, 0.999, 1e-08, 0.01, 10
NEG = -1e30
HI = lax.Precision.HIGHEST
NN = (((1,), (0,)), ((), ()))
NT = (((1,), (1,)), ((), ()))
TN = (((0,), (0,)), ((), ()))


def _nbytes(shape, dtype):
    n = 1
    for s in shape:
        n *= s
    return n * jnp.dtype(dtype).itemsize


def _params(block_bytes, scratch_bytes=0):
    need = 2 * block_bytes + scratch_bytes
    lim = min(max(need + need // 4 + (4 << 20), 32 << 20), V7X_VMEM_BYTES - (6 << 20))
    return pltpu.CompilerParams(vmem_limit_bytes=int(lim))


def _pick(n, cands):
    for c in cands:
        if c <= n and n % c == 0:
            return c
    return n


def _sigmoid(x):
    return jax.nn.sigmoid(x)


def _silu(x):
    return x * jax.nn.sigmoid(x)


def _softplus(x):
    return jnp.maximum(x, 0.0) + jnp.log(1.0 + jnp.exp(-jnp.abs(x)))


def _rowwise(name, fn, rows, bcast, row_outs, red_outs=(), bm=256):
    rows = [r if isinstance(r, tuple) else (r, r.shape[1], 0) for r in rows]
    s = rows[0][0].shape[0]
    bm = _pick(s, (bm, 128, 64, 32, 16, 8))
    nr, nb, no, nd = len(rows), len(bcast), len(row_outs), len(red_outs)
    in_specs = [pl.BlockSpec((bm, w), functools.partial(lambda i, ci: (i, ci), ci=ci)) for (_, w, ci) in rows]
    in_specs += [pl.BlockSpec(b.shape, lambda i: (0, 0)) for b in bcast]
    out_shape = [jax.ShapeDtypeStruct((s, w), dt) for (w, dt) in row_outs]
    out_shape += [jax.ShapeDtypeStruct((r, w), F32) for (r, w) in red_outs]
    out_specs = [pl.BlockSpec((bm, w), lambda i: (i, 0)) for (w, _) in row_outs]
    out_specs += [pl.BlockSpec((r, w), lambda i: (0, 0)) for (r, w) in red_outs]

    def body(*refs):
        ins = [r[...] for r in refs[:nr + nb]]
        outs = refs[nr + nb:nr + nb + no]
        reds = refs[nr + nb + no:]
        ov, rv = fn(*ins)
        for o, v in zip(outs, ov):
            o[...] = v.astype(o.dtype)
        if nd:
            @pl.when(pl.program_id(0) == 0)
            def _():
                for r in reds:
                    r[...] = jnp.zeros(r.shape, F32)
            for r, v in zip(reds, rv):
                r[...] += v.astype(F32)

    blk = sum(_nbytes((bm, w), a.dtype) for (a, w, _) in rows) + sum(_nbytes(b.shape, b.dtype) for b in bcast)
    blk += sum(_nbytes((bm, w), dt) for (w, dt) in row_outs) + sum(_nbytes(r, F32) for r in red_outs)
    res = pl.pallas_call(
        body, name=name, grid=(s // bm,), in_specs=in_specs, out_specs=out_specs, out_shape=out_shape,
        compiler_params=_params(3 * blk),
    )(*[a for (a, _, _) in rows], *bcast)
    return res


def _matmul(name, a, b, *, ta=False, tb=False, outs=(F32,), epi=None, epi_rows=(), epi_bcast=(),
            bm=None, bn=None, bk=None):
    if ta:
        k, m = a.shape
    else:
        m, k = a.shape
    n = b.shape[0] if tb else b.shape[1]
    assert (b.shape[1] if tb else b.shape[0]) == k, (name, a.shape, b.shape)
    if bm is None:
        bm = _pick(m, (1024, 1408, 768, 512, 384, 256, 128)) if ta else _pick(m, (1024, 512, 256, 128, 64, 32, 16))
    if bn is None:
        bn = _pick(n, (512, 384, 256, 128))
    if bk is None:
        bk = k if k <= 3072 else _pick(k, (2816, 2048, 1024, 512))
        if ta:
            bk = _pick(k, (1024, 512, 256, 128, 64, 32, 16))
    nk = k // bk
    dims = TN if ta else (NT if tb else NN)
    a_spec = pl.BlockSpec((bk, bm), lambda i, j, kk: (kk, i)) if ta else pl.BlockSpec((bm, bk), lambda i, j, kk: (i, kk))
    b_spec = pl.BlockSpec((bn, bk), lambda i, j, kk: (j, kk)) if tb else pl.BlockSpec((bk, bn), lambda i, j, kk: (kk, j))
    in_specs = [a_spec, b_spec]
    in_specs += [pl.BlockSpec((bm, bn), lambda i, j, kk: (i, j)) for _ in epi_rows]
    in_specs += [pl.BlockSpec((1, bn), lambda i, j, kk: (0, j)) for _ in epi_bcast]
    out_shape = [jax.ShapeDtypeStruct((m, n), dt) for dt in outs]
    out_specs = [pl.BlockSpec((bm, bn), lambda i, j, kk: (i, j)) for _ in outs]
    ner, neb, no = len(epi_rows), len(epi_bcast), len(outs)

    def body(*refs):
        a_ref, b_ref = refs[0], refs[1]
        extra = refs[2:2 + ner + neb]
        out_refs = refs[2 + ner + neb:2 + ner + neb + no]
        prod = lax.dot_general(a_ref[...], b_ref[...], dims, preferred_element_type=F32)

        def finish(acc):
            vals = epi(acc, *[r[...] for r in extra]) if epi is not None else (acc,)
            for o, v in zip(out_refs, vals):
                o[...] = v.astype(o.dtype)

        if nk == 1:
            finish(prod)
        else:
            acc_ref = refs[-1]
            kk = pl.program_id(2)

            @pl.when(kk == 0)
            def _():
                acc_ref[...] = prod

            @pl.when(kk > 0)
            def _():
                acc_ref[...] += prod

            @pl.when(kk == nk - 1)
            def _():
                finish(acc_ref[...])

    blk = _nbytes((bm, bk), a.dtype) + _nbytes((bk, bn), b.dtype)
    blk += sum(_nbytes((bm, bn), r.dtype) for r in epi_rows) + sum(_nbytes((bm, bn), dt) for dt in outs)
    scratch = [pltpu.VMEM((bm, bn), F32)] if nk > 1 else []
    res = pl.pallas_call(
        body, name=name, grid=(m // bm, n // bn, nk), in_specs=in_specs, out_specs=out_specs,
        out_shape=out_shape, scratch_shapes=scratch,
        compiler_params=_params(blk, 3 * _nbytes((bm, bn), F32)),
    )(a, b, *epi_rows, *epi_bcast)
    return res


def _mm_core(name, grid, nk, pairs, out_defs, acc_shape, epi=None, epi_ins=()):
    npair, nep, no = len(pairs), len(epi_ins), len(out_defs)

    def body(*refs):
        extra = refs[2 * npair:2 * npair + nep]
        out_refs = refs[2 * npair + nep:2 * npair + nep + no]
        prod = None
        for p in range(npair):
            d = lax.dot_general(refs[2 * p][...], refs[2 * p + 1][...], pairs[p][4], preferred_element_type=F32)
            prod = d if prod is None else prod + d

        def finish(acc):
            vals = epi(acc, *[r[...] for r in extra]) if epi is not None else (acc,)
            for o, v in zip(out_refs, vals):
                o[...] = v.astype(o.dtype)

        if nk == 1:
            finish(prod)
        else:
            acc_ref = refs[-1]
            kk = pl.program_id(2)

            @pl.when(kk == 0)
            def _():
                acc_ref[...] = prod

            @pl.when(kk > 0)
            def _():
                acc_ref[...] += prod

            @pl.when(kk == nk - 1)
            def _():
                finish(acc_ref[...])

    def blk_bytes(spec, dtype):
        return _nbytes([s for s in spec.block_shape if s is not None], dtype)

    blk = sum(blk_bytes(sa, a.dtype) + blk_bytes(sb, b.dtype) for (a, sa, b, sb, _) in pairs)
    blk += sum(blk_bytes(sp, arr.dtype) for (arr, sp) in epi_ins) + sum(blk_bytes(sp, dt) for (_, dt, sp) in out_defs)
    ins, in_specs = [], []
    for (a, sa, b, sb, _) in pairs:
        ins += [a, b]
        in_specs += [sa, sb]
    ins += [arr for (arr, _) in epi_ins]
    in_specs += [sp for (_, sp) in epi_ins]
    return pl.pallas_call(
        body, name=name, grid=grid, in_specs=in_specs, out_specs=[sp for (_, _, sp) in out_defs],
        out_shape=[jax.ShapeDtypeStruct(sh, dt) for (sh, dt, _) in out_defs],
        scratch_shapes=[pltpu.VMEM(acc_shape, F32)] if nk > 1 else [],
        compiler_params=_params(blk, 3 * _nbytes(acc_shape, F32)),
    )(*ins)


def _rms_mod(h, ln, sh, sc):
    n = h * lax.rsqrt(jnp.mean(h * h, axis=-1, keepdims=True) + NORM_EPS) * ln
    return n * (1.0 + sc) + sh


def _swiglu_act(g, u):
    return _silu(g.astype(F32)) * u.astype(F32)


def _dn_prep(yc, pba, alog, dtb):
    act = _silu(yc)
    parts = []
    for idx in range(2 * DN_HEADS):
        seg = act[:, idx * DN_DIM:(idx + 1) * DN_DIM]
        seg = seg * lax.rsqrt(jnp.sum(seg * seg, axis=-1, keepdims=True) + NORM_EPS)
        if idx < DN_HEADS:
            seg = seg * (DN_DIM ** -0.5)
        parts.append(seg)
    parts.append(act[:, 2 * DN_WIDTH:])
    qkvn = jnp.concatenate(parts, axis=1)
    lane = lax.broadcasted_iota(jnp.int32, pba.shape, 1)
    beta = _sigmoid(pba)
    g = -jnp.exp(alog) * _softplus(pba + dtb)
    gb = jnp.where(lane < DN_HEADS, beta, jnp.where(lane < 2 * DN_HEADS, g, 0.0))
    return qkvn, gb


def _dn_outnorm(o_a, z, dn):
    parts = []
    for h in range(DN_HEADS):
        seg = o_a[:, h * DN_DIM:(h + 1) * DN_DIM]
        seg = seg * lax.rsqrt(jnp.mean(seg * seg, axis=-1, keepdims=True) + NORM_EPS) * dn
        parts.append(seg)
    return jnp.concatenate(parts, axis=1) * _silu(z)


def _shift_down(x, halo8, s):
    r = pltpu.roll(x, s, axis=0)
    top = pltpu.roll(halo8, s, axis=0)
    i8 = lax.broadcasted_iota(jnp.int32, top.shape, 0)
    return jnp.concatenate([jnp.where(i8 < s, top, r[0:8]), r[8:]], axis=0)


def _shift_up(x, halo8, s):
    m = x.shape[0]
    r = pltpu.roll(x, m - s, axis=0)
    bot = pltpu.roll(halo8, 8 - s, axis=0)
    i8 = lax.broadcasted_iota(jnp.int32, bot.shape, 0)
    return jnp.concatenate([r[:m - 8], jnp.where(i8 >= 8 - s, bot, r[m - 8:])], axis=0)


def _conv_prep_fwd(name, pq, convw8, pba, alog, dtb, bm=256):
    s, w = pq.shape
    nblk = s // bm
    hb = bm // 16

    def body(x_ref, halo_ref, w_ref, pba_ref, alog_ref, dtb_ref, yc_ref, qkv_ref, gb_ref):
        i = pl.program_id(0)
        x = x_ref[...].astype(F32)
        halo = jnp.where(i > 0, halo_ref[...].astype(F32)[8:16], 0.0)
        cw = w_ref[...]
        y = x * cw[DN_CONV - 1:DN_CONV]
        for sft in range(1, DN_CONV):
            y = y + _shift_down(x, halo, sft) * cw[DN_CONV - 1 - sft:DN_CONV - sft]
        ycb = y.astype(BF16)
        yc_ref[...] = ycb
        qkvn, gb = _dn_prep(ycb.astype(F32), pba_ref[...], alog_ref[...], dtb_ref[...])
        qkv_ref[...] = qkvn.astype(BF16)
        gb_ref[...] = gb

    blk = 3 * _nbytes((bm, w), BF16) + 4 * _nbytes((bm, w), F32)
    return pl.pallas_call(
        body, name=name, grid=(nblk,),
        in_specs=[pl.BlockSpec((bm, w), lambda i: (i, 0)),
                  pl.BlockSpec((16, w), lambda i: (jnp.maximum(i * hb - 1, 0), 0)),
                  pl.BlockSpec(convw8.shape, lambda i: (0, 0)),
                  pl.BlockSpec((bm, LANES), lambda i: (i, 0)),
                  pl.BlockSpec((1, LANES), lambda i: (0, 0)),
                  pl.BlockSpec((1, LANES), lambda i: (0, 0))],
        out_specs=[pl.BlockSpec((bm, w), lambda i: (i, 0)), pl.BlockSpec((bm, w), lambda i: (i, 0)),
                   pl.BlockSpec((bm, LANES), lambda i: (i, 0))],
        out_shape=[jax.ShapeDtypeStruct((s, w), BF16), jax.ShapeDtypeStruct((s, w), BF16),
                   jax.ShapeDtypeStruct((s, LANES), F32)],
        compiler_params=_params(blk),
    )(pq, pq, convw8, pba, alog, dtb)


def _conv_bwd(name, dyc, pq, convw8, bm=256):
    s, w = pq.shape
    nblk = s // bm
    hb = bm // 16

    def body(dy_ref, dyn_ref, x_ref, xh_ref, w_ref, dx_ref, dw_ref):
        i = pl.program_id(0)
        dy = dy_ref[...].astype(F32)
        nxt = jnp.where(i < nblk - 1, dyn_ref[...].astype(F32)[0:8], 0.0)
        x = x_ref[...].astype(F32)
        halo = jnp.where(i > 0, xh_ref[...].astype(F32)[8:16], 0.0)
        cw = w_ref[...]
        dx = dy * cw[DN_CONV - 1:DN_CONV]
        for sft in range(1, DN_CONV):
            dx = dx + _shift_up(dy, nxt, sft) * cw[DN_CONV - 1 - sft:DN_CONV - sft]
        dx_ref[...] = dx.astype(dx_ref.dtype)
        r8 = lax.broadcasted_iota(jnp.int32, (8, w), 0)
        dw = jnp.zeros((8, w), F32)
        for j in range(DN_CONV):
            sft = DN_CONV - 1 - j
            xs = x if sft == 0 else _shift_down(x, halo, sft)
            dw = dw + jnp.where(r8 == j, jnp.sum(dy * xs, axis=0, keepdims=True), 0.0)

        @pl.when(i == 0)
        def _():
            dw_ref[...] = jnp.zeros((8, w), F32)
        dw_ref[...] += dw

    blk = 4 * _nbytes((bm, w), BF16) + 5 * _nbytes((bm, w), F32)
    return pl.pallas_call(
        body, name=name, grid=(nblk,),
        in_specs=[pl.BlockSpec((bm, w), lambda i: (i, 0)),
                  pl.BlockSpec((16, w), lambda i: (jnp.minimum((i + 1) * hb, s // 16 - 1), 0)),
                  pl.BlockSpec((bm, w), lambda i: (i, 0)),
                  pl.BlockSpec((16, w), lambda i: (jnp.maximum(i * hb - 1, 0), 0)),
                  pl.BlockSpec(convw8.shape, lambda i: (0, 0))],
        out_specs=[pl.BlockSpec((bm, w), lambda i: (i, 0)), pl.BlockSpec((8, w), lambda i: (0, 0))],
        out_shape=[jax.ShapeDtypeStruct((s, w), BF16), jax.ShapeDtypeStruct((8, w), F32)],
        compiler_params=_params(blk),
    )(dyc, dyc, pq, pq, convw8)


BNN = (((2,), (1,)), ((0,), (0,)))
BNT = (((2,), (2,)), ((0,), (0,)))
BTN = (((1,), (1,)), ((0,), (0,)))


def _raw_dot_1pass(a, b, dims):
    return lax.dot_general(a.astype(BF16), b.astype(BF16), dims, preferred_element_type=F32)


def _raw_dot_3pass(a, b, dims):
    ah = a.astype(BF16)
    al = (a - ah.astype(F32)).astype(BF16)
    bh = b.astype(BF16)
    bl = (b - bh.astype(F32)).astype(BF16)
    d = lambda x, y: lax.dot_general(x, y, dims, preferred_element_type=F32)
    return d(ah, bh) + (d(ah, bl) + d(al, bh))


def _with_same_precision_vjp(raw):
    @functools.partial(jax.custom_vjp, nondiff_argnums=(2,))
    def dot(a, b, dims):
        return raw(a, b, dims)

    def fwd(a, b, dims):
        return raw(a, b, dims), (a, b)

    def bwd(dims, res, ct):
        a, b = res
        if dims == BNN:
            return raw(ct, b, BNT), raw(a, ct, BTN)
        if dims == BNT:
            return raw(ct, b, BNN), raw(ct, a, BTN)
        assert dims == BTN
        return raw(b, ct, BNT), raw(a, ct, BNN)

    dot.defvjp(fwd, bwd)
    return dot


_dot_1pass_vjp = _with_same_precision_vjp(_raw_dot_1pass)
_dot_3pass_vjp = _with_same_precision_vjp(_raw_dot_3pass)


def _dot_bf16(a, b, dims=BNN):
    return _dot_1pass_vjp(a, b, dims)


def _dot_3pass(a, b, dims=BNN):
    return _dot_3pass_vjp(a, b, dims)


def _delta_chunk(q, k, v, gcol, bcol, state):
    h, c, _ = q.shape
    row = lax.broadcasted_iota(jnp.int32, (h, c, c), 1)
    col = lax.broadcasted_iota(jnp.int32, (h, c, c), 2)
    incl, strict, eye = row >= col, row > col, row == col
    g_b = jnp.broadcast_to(gcol, (h, c, c))
    gc_row = jnp.sum(jnp.where(row <= col, g_b, 0.0), axis=1, keepdims=True)
    g_r = jnp.sum(jnp.where(eye, g_b, 0.0), axis=1, keepdims=True)
    gc_col = jnp.sum(jnp.where(incl, jnp.broadcast_to(g_r, (h, c, c)), 0.0), axis=2, keepdims=True)
    decay = jnp.exp(jnp.where(incl, gc_col - gc_row, NEG))
    kb = k * bcol
    vb = v * bcol
    x = -jnp.where(strict, _dot_bf16(kb, k, BNT) * decay, 0.0)
    t = jnp.where(eye, 1.0, 0.0) + x
    p = x
    for _ in range(5):
        p = _dot_3pass(p, p)
        t = t + _dot_3pass(t, p)
    eg = jnp.exp(gc_col)
    u = _dot_3pass(t, vb)
    w = _dot_3pass(t, kb * eg)
    qk = _dot_bf16(q, k, BNT) * decay
    v_new = u - _dot_bf16(w, state)
    o = _dot_bf16(q * eg, state) + _dot_bf16(qk, v_new)
    g_last = jnp.sum(g_r, axis=2, keepdims=True)
    new_state = state * jnp.exp(g_last) + _dot_bf16(k * jnp.exp(g_last - gc_col), v_new, BTN)
    return o, new_state


def _lane_col(blk, idx):
    lane = lax.broadcasted_iota(jnp.int32, blk.shape, 1)
    return jnp.sum(jnp.where(lane == idx, blk, 0.0), axis=1, keepdims=True)


def _dn_heads(ref, base):
    return jnp.stack([ref[:, base + h * DN_DIM:base + (h + 1) * DN_DIM] for h in range(DN_HEADS)], axis=0).astype(F32)


def _dn_cols(gbv, base):
    return jnp.stack([_lane_col(gbv, base + h) for h in range(DN_HEADS)], axis=0)


def _delta_fwd(name, qkvn, gb):
    s = qkvn.shape[0]
    n = s // DN_CHUNK
    c = DN_CHUNK

    def body(qkv_ref, gb_ref, o_ref, st_ref, state):
        @pl.when(pl.program_id(0) == 0)
        def _():
            state[...] = jnp.zeros(state.shape, F32)

        gbv = gb_ref[...]
        st = state[...]
        st_ref[0] = st
        o, new = _delta_chunk(_dn_heads(qkv_ref, 0), _dn_heads(qkv_ref, DN_WIDTH), _dn_heads(qkv_ref, 2 * DN_WIDTH),
                              _dn_cols(gbv, DN_HEADS), _dn_cols(gbv, 0), st)
        for h in range(DN_HEADS):
            o_ref[:, h * DN_DIM:(h + 1) * DN_DIM] = o[h]
        state[...] = new

    blk = _nbytes((c, 3 * DN_WIDTH), BF16) + _nbytes((c, LANES), F32) + _nbytes((c, DN_WIDTH), F32)
    blk += _nbytes((DN_HEADS, DN_DIM, DN_DIM), F32)
    return pl.pallas_call(
        body, name=name, grid=(n,),
        in_specs=[pl.BlockSpec((c, 3 * DN_WIDTH), lambda i: (i, 0)), pl.BlockSpec((c, LANES), lambda i: (i, 0))],
        out_specs=[pl.BlockSpec((c, DN_WIDTH), lambda i: (i, 0)),
                   pl.BlockSpec((1, DN_HEADS, DN_DIM, DN_DIM), lambda i: (i, 0, 0, 0))],
        out_shape=[jax.ShapeDtypeStruct((s, DN_WIDTH), F32),
                   jax.ShapeDtypeStruct((n, DN_HEADS, DN_DIM, DN_DIM), F32)],
        scratch_shapes=[pltpu.VMEM((DN_HEADS, DN_DIM, DN_DIM), F32)],
        compiler_params=_params(blk, 8 << 20),
    )(qkvn, gb)


def _delta_bwd(name, qkvn, gb, states, d_o):
    s = qkvn.shape[0]
    n = s // DN_CHUNK
    c = DN_CHUNK

    def body(qkv_ref, gb_ref, st_ref, do_ref, dqkv_ref, dgb_ref, dstate):
        @pl.when(pl.program_id(0) == 0)
        def _():
            dstate[...] = jnp.zeros(dstate.shape, F32)

        gbv = gb_ref[...]
        lane = lax.broadcasted_iota(jnp.int32, (c, LANES), 1)
        _, vjp = jax.vjp(_delta_chunk, _dn_heads(qkv_ref, 0), _dn_heads(qkv_ref, DN_WIDTH),
                         _dn_heads(qkv_ref, 2 * DN_WIDTH), _dn_cols(gbv, DN_HEADS), _dn_cols(gbv, 0), st_ref[0])
        dq, dk, dv, dg, db, dst = vjp((_dn_heads(do_ref, 0), dstate[...]))
        dgb = jnp.zeros((c, LANES), F32)
        for h in range(DN_HEADS):
            dqkv_ref[:, h * DN_DIM:(h + 1) * DN_DIM] = dq[h]
            dqkv_ref[:, DN_WIDTH + h * DN_DIM:DN_WIDTH + (h + 1) * DN_DIM] = dk[h]
            dqkv_ref[:, 2 * DN_WIDTH + h * DN_DIM:2 * DN_WIDTH + (h + 1) * DN_DIM] = dv[h]
            dgb = dgb + jnp.where(lane == h, db[h], 0.0) + jnp.where(lane == DN_HEADS + h, dg[h], 0.0)
        dstate[...] = dst
        dgb_ref[...] = dgb

    rev = lambda i: (n - 1 - i, 0)
    blk = _nbytes((c, 3 * DN_WIDTH), BF16) + 2 * _nbytes((c, LANES), F32) + _nbytes((c, DN_WIDTH), F32)
    blk += _nbytes((DN_HEADS, DN_DIM, DN_DIM), F32) + _nbytes((c, 3 * DN_WIDTH), F32)
    return pl.pallas_call(
        body, name=name, grid=(n,),
        in_specs=[pl.BlockSpec((c, 3 * DN_WIDTH), rev), pl.BlockSpec((c, LANES), rev),
                  pl.BlockSpec((1, DN_HEADS, DN_DIM, DN_DIM), lambda i: (n - 1 - i, 0, 0, 0)),
                  pl.BlockSpec((c, DN_WIDTH), rev)],
        out_specs=[pl.BlockSpec((c, 3 * DN_WIDTH), rev), pl.BlockSpec((c, LANES), rev)],
        out_shape=[jax.ShapeDtypeStruct((s, 3 * DN_WIDTH), F32), jax.ShapeDtypeStruct((s, LANES), F32)],
        scratch_shapes=[pltpu.VMEM((DN_HEADS, DN_DIM, DN_DIM), F32)],
        compiler_params=_params(blk, 16 << 20),
    )(qkvn, gb, states, d_o)


def _da_scores(q2f, k2, sub, valid, distf, head):
    lane = lax.broadcasted_iota(jnp.int32, q2f.shape, 1)
    hmask = (lane < DA_DIM) if sub == 0 else (lane >= DA_DIM)
    qm = jnp.where(hmask, q2f, 0.0).astype(BF16)
    slope = 2.0 ** (-ALIBI_MAX_EXP * (head + 1) / DA_HEADS)
    sc = lax.dot_general(qm, k2, NT, preferred_element_type=F32) * (DA_DIM ** -0.5)
    return jnp.where(valid, sc - slope * distf, NEG), qm, hmask


def _da_mask(i, r):
    qi = lax.broadcasted_iota(jnp.int32, (DA_BLOCK, 2 * DA_BLOCK), 0)
    ki = lax.broadcasted_iota(jnp.int32, (DA_BLOCK, 2 * DA_BLOCK), 1)
    dist = qi + DA_BLOCK - ki
    valid = (dist >= 0) & (dist <= DA_BLOCK) & ((ki >= DA_BLOCK) | (i > 0))
    return valid, (dist * r).astype(F32)


def _da_fwd(name, pda, r):
    s = pda.shape[0]
    n = s // r
    nb = n // DA_BLOCK
    w = DA_WIDTH
    dav = pda.reshape(n, r * 3 * w)

    def body(q_ref, kc_ref, kp_ref, vc_ref, vp_ref, o_ref, lse_ref):
        i = pl.program_id(1)
        valid, distf = _da_mask(i, r)
        lane = lax.broadcasted_iota(jnp.int32, (DA_BLOCK, LANES), 1)
        lse = jnp.zeros((DA_BLOCK, LANES), F32)
        for hp in range(DA_HEADS // 2):
            sl = slice(hp * LANES, (hp + 1) * LANES)
            q2f = q_ref[:, sl].astype(F32)
            k2 = jnp.concatenate([kp_ref[:, sl], kc_ref[:, sl]], axis=0)
            v2 = jnp.concatenate([vp_ref[:, sl], vc_ref[:, sl]], axis=0)
            o2 = None
            for sub in range(2):
                head = 2 * hp + sub
                sc, _, hmask = _da_scores(q2f, k2, sub, valid, distf, head)
                mx = jnp.max(sc, axis=1, keepdims=True)
                p = jnp.exp(sc - mx)
                l = jnp.sum(p, axis=1, keepdims=True)
                pv = lax.dot_general(p.astype(BF16), v2, NN, preferred_element_type=F32) / l
                o2 = pv if sub == 0 else jnp.where(hmask, pv, o2)
                lse = jnp.where(lane == head, mx + jnp.log(l), lse)
            o_ref[:, sl] = o2
        lse_ref[...] = lse

    prev = lambda col: (lambda p, i: (jnp.maximum(i - 1, 0), 3 * p + col))
    cur = lambda col: (lambda p, i: (i, 3 * p + col))
    blk = 5 * _nbytes((DA_BLOCK, w), BF16) + _nbytes((DA_BLOCK, w), F32) + _nbytes((DA_BLOCK, LANES), F32)
    o, lse = pl.pallas_call(
        body, name=name, grid=(r, nb),
        in_specs=[pl.BlockSpec((DA_BLOCK, w), cur(0)), pl.BlockSpec((DA_BLOCK, w), cur(1)),
                  pl.BlockSpec((DA_BLOCK, w), prev(1)), pl.BlockSpec((DA_BLOCK, w), cur(2)),
                  pl.BlockSpec((DA_BLOCK, w), prev(2))],
        out_specs=[pl.BlockSpec((DA_BLOCK, w), lambda p, i: (i, p)),
                   pl.BlockSpec((DA_BLOCK, LANES), lambda p, i: (i, p))],
        out_shape=[jax.ShapeDtypeStruct((n, r * w), F32), jax.ShapeDtypeStruct((n, r * LANES), F32)],
        compiler_params=_params(blk, 8 << 20),
    )(dav, dav, dav, dav, dav)
    return o.reshape(s, w), lse.reshape(s, LANES)


def _da_bwd(name, pda, d_ob, lse_tot, delta, r):
    s = pda.shape[0]
    n = s // r
    nb = n // DA_BLOCK
    w = DA_WIDTH
    dav = pda.reshape(n, r * 3 * w)
    dov = d_ob.reshape(n, r * w)
    lv = lse_tot.reshape(n, r * LANES)
    dlv = delta.reshape(n, r * LANES)

    def body(q_ref, kc_ref, kp_ref, vc_ref, vp_ref, do_ref, l_ref, dl_ref, dq_ref, dk_ref, dv_ref, ck, cv):
        i = pl.program_id(1)

        @pl.when(i == 0)
        def _():
            ck[...] = jnp.zeros(ck.shape, F32)
            cv[...] = jnp.zeros(cv.shape, F32)

        @pl.when(i < nb)
        def _():
            valid, distf = _da_mask(i, r)
            lsev = l_ref[...]
            dlt = dl_ref[...]
            for hp in range(DA_HEADS // 2):
                sl = slice(hp * LANES, (hp + 1) * LANES)
                q2f = q_ref[:, sl].astype(F32)
                k2 = jnp.concatenate([kp_ref[:, sl], kc_ref[:, sl]], axis=0)
                v2 = jnp.concatenate([vp_ref[:, sl], vc_ref[:, sl]], axis=0)
                do2f = do_ref[:, sl].astype(F32)
                dq2 = jnp.zeros((DA_BLOCK, LANES), F32)
                dk2 = jnp.zeros((2 * DA_BLOCK, LANES), F32)
                dv2 = jnp.zeros((2 * DA_BLOCK, LANES), F32)
                for sub in range(2):
                    head = 2 * hp + sub
                    sc, qm, hmask = _da_scores(q2f, k2, sub, valid, distf, head)
                    p = jnp.exp(sc - _lane_col(lsev, head))
                    dom = jnp.where(hmask, do2f, 0.0).astype(BF16)
                    dp = lax.dot_general(dom, v2, NT, preferred_element_type=F32)
                    ds = (p * (dp - _lane_col(dlt, head)) * (DA_DIM ** -0.5)).astype(BF16)
                    dq2 = dq2 + jnp.where(hmask, lax.dot_general(ds, k2, NN, preferred_element_type=F32), 0.0)
                    dk2 = dk2 + lax.dot_general(ds, qm, TN, preferred_element_type=F32)
                    dv2 = dv2 + lax.dot_general(p.astype(BF16), dom, TN, preferred_element_type=F32)
                dq_ref[:, sl] = dq2
                dk_ref[:, sl] = ck[:, sl] + dk2[:DA_BLOCK]
                dv_ref[:, sl] = cv[:, sl] + dv2[:DA_BLOCK]
                ck[:, sl] = dk2[DA_BLOCK:]
                cv[:, sl] = dv2[DA_BLOCK:]

        @pl.when(i == nb)
        def _():
            dk_ref[...] = ck[...]
            dv_ref[...] = cv[...]

    qrow = lambda i: jnp.minimum(i, nb - 1)
    prev = lambda col: (lambda p, i: (jnp.maximum(qrow(i) - 1, 0), 3 * p + col))
    cur = lambda col: (lambda p, i: (qrow(i), 3 * p + col))
    same = lambda p, i: (qrow(i), p)
    late = lambda p, i: (jnp.maximum(i - 1, 0), p)
    blk = 6 * _nbytes((DA_BLOCK, w), BF16) + 2 * _nbytes((DA_BLOCK, LANES), F32) + 3 * _nbytes((DA_BLOCK, w), F32)
    dq, dk, dv = pl.pallas_call(
        body, name=name, grid=(r, nb + 1),
        in_specs=[pl.BlockSpec((DA_BLOCK, w), cur(0)), pl.BlockSpec((DA_BLOCK, w), cur(1)),
                  pl.BlockSpec((DA_BLOCK, w), prev(1)), pl.BlockSpec((DA_BLOCK, w), cur(2)),
                  pl.BlockSpec((DA_BLOCK, w), prev(2)), pl.BlockSpec((DA_BLOCK, w), same),
                  pl.BlockSpec((DA_BLOCK, LANES), same), pl.BlockSpec((DA_BLOCK, LANES), same)],
        out_specs=[pl.BlockSpec((DA_BLOCK, w), same), pl.BlockSpec((DA_BLOCK, w), late),
                   pl.BlockSpec((DA_BLOCK, w), late)],
        out_shape=[jax.ShapeDtypeStruct((n, r * w), F32)] * 3,
        scratch_shapes=[pltpu.VMEM((DA_BLOCK, w), F32), pltpu.VMEM((DA_BLOCK, w), F32)],
        compiler_params=_params(blk, 12 << 20),
    )(dav, dav, dav, dav, dav, dov, lv, dlv)
    return dq.reshape(s, w), dk.reshape(s, w), dv.reshape(s, w)


def _head_expand():
    hrow = lax.broadcasted_iota(jnp.int32, (LANES, DA_WIDTH), 0)
    lcol = lax.broadcasted_iota(jnp.int32, (LANES, DA_WIDTH), 1)
    return jnp.where(lcol // DA_DIM == hrow, 1.0, 0.0).astype(F32)


def _ffn_up(name, a, ga, tg, tu):
    s, d = a.shape
    nsh, _, _, ffs = ga.shape
    bm = _pick(s, (1024, 512, 256, 128))

    def body(a_ref, wg_ref, wu_ref, g_ref, u_ref, f_ref):
        av = a_ref[...]
        g = lax.dot_general(av, wg_ref[...], NN, preferred_element_type=F32)
        u = lax.dot_general(av, wu_ref[...], NN, preferred_element_type=F32)
        g_ref[...] = g.astype(BF16)
        u_ref[...] = u.astype(BF16)
        f_ref[...] = (_silu(g) * u).astype(BF16)

    wspec = lambda t: pl.BlockSpec((None, None, d, ffs), lambda i, j: (j, t, 0, 0))
    ospec = pl.BlockSpec((None, bm, ffs), lambda i, j: (j, i, 0))
    blk = _nbytes((bm, d), BF16) + 2 * _nbytes((d, ffs), BF16) + 3 * _nbytes((bm, ffs), BF16)
    return pl.pallas_call(
        body, name=name, grid=(s // bm, nsh),
        in_specs=[pl.BlockSpec((bm, d), lambda i, j: (i, 0)), wspec(tg), wspec(tu)],
        out_specs=[ospec] * 3, out_shape=[jax.ShapeDtypeStruct((nsh, s, ffs), BF16)] * 3,
        compiler_params=_params(blk, 4 * _nbytes((bm, ffs), F32)),
    )(a, ga, ga)


def _ffn_fwd(tag, h_in, ln, sh, sc, gt, ga, tg, tu, gb, td, weight):
    s, d = h_in.shape
    nsh, _, ffs, _ = gb.shape
    (a,) = _rowwise(tag + "_norm", lambda h, l, s1, s2: ((_rms_mod(h, l, s1, s2),), ()), [h_in], [ln, sh, sc],
                    [(d, BF16)])
    g, u, f = _ffn_up(tag + "_up", a, ga, tg, tu)
    bm, bn = _pick(s, (1024, 512, 256, 128)), _pick(d, (512, 256, 128))
    io = pl.BlockSpec((bm, bn), lambda i, j, kk: (i, j))
    h_out, o = _mm_core(
        tag + "_down", (s // bm, d // bn, nsh), nsh,
        [(f, pl.BlockSpec((None, bm, ffs), lambda i, j, kk: (kk, i, 0)),
          gb, pl.BlockSpec((None, None, ffs, bn), lambda i, j, kk: (kk, td, 0, j)), NN)],
        [((s, d), F32, io), ((s, d), BF16, io)], (bm, bn),
        epi=lambda acc, h, gv: (h + weight * gv * acc, acc),
        epi_ins=[(h_in, io), (gt, pl.BlockSpec((1, bn), lambda i, j, kk: (0, j)))])
    return h_out, dict(a=a, g=g, u=u, f=f, o=o)


def _resid_bwd(tag, dh_out, o, gt, weight):
    d = dh_out.shape[1]

    def fn(dh, ov, g):
        return (weight * g * dh,), (jnp.sum(weight * dh * ov.astype(F32), axis=0, keepdims=True),)

    do, d_gt = _rowwise(tag + "_resid_bwd", fn, [dh_out, o], [gt], [(d, BF16)], [(1, d)])
    return do, d_gt


def _norm_bwd(tag, h_in, da, dh_out, ln, sh, sc):
    d = h_in.shape[1]

    def fn(h, dav, dh, l, s1, s2):
        _, vjp = jax.vjp(_rms_mod, h, l, s1, s2)
        gh, gl, gs1, gs2 = vjp(dav)
        return (dh + gh,), (gl, gs1, gs2)

    return _rowwise(tag + "_norm_bwd", fn, [h_in, da, dh_out], [ln, sh, sc], [(d, F32)], [(1, d)] * 3)


def _ffn_bwd(tag, h_in, dh_out, sv, ln, sh, sc, gt, ga, tg, tu, gb, td, weight):
    s, d = h_in.shape
    nsh, _, ffs, _ = gb.shape
    bm, bn = _pick(s, (1024, 512, 256, 128)), _pick(d, (512, 256, 128))
    bk = _pick(s, (1024, 512, 256, 128))
    do, d_gt = _resid_bwd(tag, dh_out, sv["o"], gt, weight)

    def act_bwd(df, g, u):
        _, vjp = jax.vjp(_swiglu_act, g, u)
        return vjp(df)

    hid = pl.BlockSpec((None, bm, ffs), lambda i, j, kk: (j, i, 0))
    dg, du = _mm_core(
        tag + "_down_dx", (s // bm, nsh, 1), 1,
        [(do, pl.BlockSpec((bm, d), lambda i, j, kk: (i, 0)),
          gb, pl.BlockSpec((None, None, ffs, d), lambda i, j, kk: (j, td, 0, 0)), NT)],
        [((nsh, s, ffs), BF16, hid)] * 2, (bm, ffs), epi=act_bwd, epi_ins=[(sv["g"], hid), (sv["u"], hid)])
    (d_wd,) = _mm_core(
        tag + "_down_dw", (nsh, d // bn, s // bk), s // bk,
        [(sv["f"], pl.BlockSpec((None, bk, ffs), lambda i, j, kk: (i, kk, 0)),
          do, pl.BlockSpec((bk, bn), lambda i, j, kk: (kk, j)), TN)],
        [((nsh, ffs, d), BF16, pl.BlockSpec((None, ffs, bn), lambda i, j, kk: (i, 0, j)))], (ffs, bn))
    kmaj = pl.BlockSpec((None, bm, ffs), lambda i, j, kk: (kk, i, 0))
    wsp = lambda t: pl.BlockSpec((None, None, bn, ffs), functools.partial(lambda i, j, kk, t: (kk, t, j, 0), t=t))
    (da,) = _mm_core(
        tag + "_up_dx", (s // bm, d // bn, nsh), nsh, [(dg, kmaj, ga, wsp(tg), NT), (du, kmaj, ga, wsp(tu), NT)],
        [((s, d), F32, pl.BlockSpec((bm, bn), lambda i, j, kk: (i, j)))], (bm, bn))
    dws = []
    for nm, dh in (("_wg_dw", dg), ("_wu_dw", du)):
        (dw,) = _mm_core(
            tag + nm, (1, nsh, s // bk), s // bk,
            [(sv["a"], pl.BlockSpec((bk, d), lambda i, j, kk: (kk, 0)),
              dh, pl.BlockSpec((None, bk, ffs), lambda i, j, kk: (j, kk, 0)), TN)],
            [((nsh, d, ffs), BF16, pl.BlockSpec((None, d, ffs), lambda i, j, kk: (j, 0, 0)))], (d, ffs))
        dws.append(dw)
    dh_in, d_ln, d_sh, d_sc = _norm_bwd(tag, h_in, da, dh_out, ln, sh, sc)
    return dh_in, dict(wg=dws[0], wu=dws[1], wd=d_wd), dict(ln=d_ln, sh=d_sh, sc=d_sc, gt=d_gt)


def _mixer_fwd(tag, h_in, ln, sh, sc, gt, w, sp):
    d = h_in.shape[1]
    (a,) = _rowwise(tag + "_norm", lambda h, l, s1, s2: ((_rms_mod(h, l, s1, s2),), ()), [h_in], [ln, sh, sc],
                    [(d, BF16)])
    (pq,) = _matmul(tag + "_pq", a, w["wq"], outs=(BF16,))
    (pz,) = _matmul(tag + "_pz", a, w["wz"], outs=(BF16,))
    (pba,) = _matmul(tag + "_pba", a, w["wba"])
    (pda,) = _matmul(tag + "_pda", a, w["wda"], outs=(BF16,))
    (pg,) = _matmul(tag + "_pg", a, w["wg"], outs=(BF16,))
    yc, qkvn, gb = _conv_prep_fwd(tag + "_conv", pq, sp["conv8"], pba, sp["alog"], sp["dtb"])
    o_a, states = _delta_fwd(tag + "_delta", qkvn, gb)
    (o_an,) = _rowwise(tag + "_dnorm", lambda o, z, dn: ((_dn_outnorm(o, z.astype(F32), dn),), ()), [o_a, pz],
                       [sp["dn"]], [(DN_WIDTH, BF16)])
    ops, lses = [], []
    for (_, r) in DA_PATTERNS:
        o_p, lse_p = _da_fwd(f"{tag}_da{r}", pda, r)
        ops.append(o_p)
        lses.append(lse_p)

    def merge(o1, o2, o3, l1, l2, l3):
        mx = jnp.maximum(jnp.maximum(l1, l2), l3)
        e1, e2, e3 = jnp.exp(l1 - mx), jnp.exp(l2 - mx), jnp.exp(l3 - mx)
        tot = e1 + e2 + e3
        ex = _head_expand()
        up = lambda wgt: lax.dot_general(wgt / tot, ex, NN, precision=HI, preferred_element_type=F32)
        return (up(e1) * o1 + up(e2) * o2 + up(e3) * o3, mx + jnp.log(tot)), ()

    o_b, lse_tot = _rowwise(tag + "_merge", merge, ops + lses, [], [(DA_WIDTH, BF16), (LANES, F32)])
    (y_a,) = _matmul(tag + "_wa", o_an, w["w_a"], outs=(BF16,))
    (y_b,) = _matmul(tag + "_wb", o_b, w["w_b"], outs=(BF16,))

    def gate(ga, gbv, ya, yb):
        return _sigmoid(ga.astype(F32)) * ya.astype(F32) + _sigmoid(gbv.astype(F32)) * yb.astype(F32)

    (merged,) = _rowwise(tag + "_gate", lambda *v: ((gate(*v),), ()), [(pg, d, 0), (pg, d, 1), y_a, y_b], [],
                         [(d, BF16)])
    h_out, m = _matmul(tag + "_wo", merged, w["w_o"], outs=(F32, BF16), epi_rows=[h_in], epi_bcast=[gt],
                       epi=lambda acc, h, g: (h + g * acc, acc))
    sv = dict(a=a, pq=pq, pz=pz, pba=pba, pda=pda, pg=pg, yc=yc, qkvn=qkvn, gb=gb, o_a=o_a, states=states,
              o_an=o_an, o_b=o_b, lse=lse_tot, y_a=y_a, y_b=y_b, merged=merged, m=m, gate=gate)
    return h_out, sv


def _mixer_bwd(tag, h_in, dh_out, sv, ln, sh, sc, gt, w, sp):
    d = h_in.shape[1]
    dm, d_gt = _resid_bwd(tag, dh_out, sv["m"], gt, 1.0)
    (d_merged,) = _matmul(tag + "_wo_dx", dm, w["w_o"], tb=True, outs=(BF16,))
    (d_wo,) = _matmul(tag + "_wo_dw", sv["merged"], dm, ta=True, outs=(BF16,))
    gate = sv["gate"]

    def gate_bwd(dmg, ga, gbv, ya, yb):
        _, vjp = jax.vjp(gate, ga.astype(F32), gbv.astype(F32), ya.astype(F32), yb.astype(F32))
        dga, dgb, dya, dyb = vjp(dmg.astype(F32))
        return (jnp.concatenate([dga, dgb], axis=1), dya, dyb), ()

    pg = sv["pg"]
    d_pg, d_ya, d_yb = _rowwise(tag + "_gate_bwd", gate_bwd, [d_merged, (pg, d, 0), (pg, d, 1), sv["y_a"], sv["y_b"]],
                                [], [(2 * d, BF16), (d, BF16), (d, BF16)])
    (d_oan,) = _matmul(tag + "_wa_dx", d_ya, w["w_a"], tb=True)
    (d_wa,) = _matmul(tag + "_wa_dw", sv["o_an"], d_ya, ta=True, outs=(BF16,))
    (d_ob,) = _matmul(tag + "_wb_dx", d_yb, w["w_b"], tb=True, outs=(BF16,))
    (d_wb,) = _matmul(tag + "_wb_dw", sv["o_b"], d_yb, ta=True, outs=(BF16,))

    def dnorm_bwd(doan, o, z, dn):
        _, vjp = jax.vjp(_dn_outnorm, o, z.astype(F32), dn)
        go, gz, gdn = vjp(doan)
        return (go, gz), (gdn,)

    d_oa, d_pz, d_dn = _rowwise(tag + "_dnorm_bwd", dnorm_bwd, [d_oan, sv["o_a"], sv["pz"]], [sp["dn"]],
                                [(DN_WIDTH, F32), (DN_WIDTH, BF16)], [(1, DN_DIM)])
    d_qkvn, d_gb = _delta_bwd(tag + "_delta_bwd", sv["qkvn"], sv["gb"], sv["states"], d_oa)

    def prep_bwd(dq, dgbv, yc, pba, alog, dtb):
        _, vjp = jax.vjp(_dn_prep, yc.astype(F32), pba, alog, dtb)
        gyc, gpba, galog, gdtb = vjp((dq, dgbv))
        return (gyc, gpba), (galog, gdtb)

    d_yc, d_pba, d_alog, d_dtb = _rowwise(tag + "_prep_bwd", prep_bwd, [d_qkvn, d_gb, sv["yc"], sv["pba"]],
                                          [sp["alog"], sp["dtb"]], [(3 * DN_WIDTH, BF16), (LANES, BF16)],
                                          [(1, LANES), (1, LANES)], bm=128)
    d_pq, d_conv = _conv_bwd(tag + "_conv_bwd", d_yc, sv["pq"], sp["conv8"])

    def delta_fn(dob, ob):
        prod = dob.astype(F32) * ob.astype(F32)
        return (lax.dot_general(prod, _head_expand(), NT, precision=HI, preferred_element_type=F32),), ()

    (delta,) = _rowwise(tag + "_da_delta", delta_fn, [d_ob, sv["o_b"]], [], [(LANES, F32)])
    grads = [_da_bwd(f"{tag}_da{r}_bwd", sv["pda"], d_ob, sv["lse"], delta, r) for (_, r) in DA_PATTERNS]

    def sum3(q1, k1, v1, q2, k2, v2, q3, k3, v3):
        return (jnp.concatenate([q1 + q2 + q3, k1 + k2 + k3, v1 + v2 + v3], axis=1),), ()

    (d_pda,) = _rowwise(tag + "_da_sum", sum3, [t for g in grads for t in g], [], [(3 * DA_WIDTH, BF16)])

    a = sv["a"]
    (da,) = _matmul(tag + "_pq_dx", d_pq, w["wq"], tb=True)
    add = lambda acc, prev: (acc + prev,)
    (da,) = _matmul(tag + "_pz_dx", d_pz, w["wz"], tb=True, epi_rows=[da], epi=add)
    (da,) = _matmul(tag + "_pba_dx", d_pba, w["wba"], tb=True, epi_rows=[da], epi=add)
    (da,) = _matmul(tag + "_pda_dx", d_pda, w["wda"], tb=True, epi_rows=[da], epi=add)
    (da,) = _matmul(tag + "_pg_dx", d_pg, w["wg"], tb=True, epi_rows=[da], epi=add)
    (d_wq,) = _matmul(tag + "_pq_dw", a, d_pq, ta=True, outs=(BF16,))
    (d_wz,) = _matmul(tag + "_pz_dw", a, d_pz, ta=True, outs=(BF16,))
    (d_wba,) = _matmul(tag + "_pba_dw", a, d_pba, ta=True, outs=(BF16,))
    (d_wda,) = _matmul(tag + "_pda_dw", a, d_pda, ta=True, outs=(BF16,))
    (d_wg,) = _matmul(tag + "_pg_dw", a, d_pg, ta=True, outs=(BF16,))
    dh_in, d_ln, d_sh, d_sc = _norm_bwd(tag, h_in, da, dh_out, ln, sh, sc)
    wgrads = dict(wq=d_wq, wz=d_wz, wba=d_wba, wda=d_wda, wg=d_wg, w_a=d_wa, w_b=d_wb, w_o=d_wo)
    small = dict(ln=d_ln, sh=d_sh, sc=d_sc, gt=d_gt, dn=d_dn, alog=d_alog, dtb=d_dtb, conv=d_conv)
    return dh_in, wgrads, small


def _loss_head(h, target, fnorm):
    d = h.shape[1]

    def fn(hv, tv, fw):
        def lossf(hh, ww):
            y = hh * lax.rsqrt(jnp.mean(hh * hh, axis=-1, keepdims=True) + NORM_EPS) * ww
            return 0.5 * jnp.sum(jnp.mean(jnp.square(y - tv), axis=-1))

        val, (dh, dw) = jax.value_and_grad(lossf, argnums=(0, 1))(hv, fw)
        return (dh,), (jnp.full((1, LANES), val, F32), dw)

    return _rowwise("loss_head", fn, [h, target], [fnorm], [(d, F32)], [(1, LANES), (1, d)])


def _row(v):
    return v.reshape(1, -1)


def _pad_lanes(v, offset):
    return jnp.pad(v.reshape(1, -1), ((0, 0), (offset, LANES - offset - v.shape[0])))


_UP_SLOTS = dict(ffn1_wg=0, ffn1_wu=1, ffn2_wg=2, ffn2_wu=3)
_DOWN_SLOTS = dict(ffn1_wd=0, ffn2_wd=1)


def _local_step(x2, target, mod, ga, gb, wl, small):
    depth = mod.shape[0]
    d = x2.shape[1]
    h = x2
    saved = []
    mods = []
    up = lambda l, nm: len(_UP_SLOTS) * l + _UP_SLOTS[nm]
    down = lambda l, nm: len(_DOWN_SLOTS) * l + _DOWN_SLOTS[nm]
    for l in range(depth):
        m9 = [_row(mod[l, i * d:(i + 1) * d]) for i in range(N_ADA)]
        sp = dict(conv8=jnp.pad(small["conv_w"][l], ((0, 8 - DN_CONV), (0, 0))),
                  alog=_pad_lanes(small["a_log"][l], DN_HEADS), dtb=_pad_lanes(small["dt_bias"][l], DN_HEADS),
                  dn=_row(small["dn_norm"][l]))
        w = wl[l]
        h0 = h
        h1, sv1 = _ffn_fwd(f"l{l}_ffn1", h0, _row(small["ln_ffn1"][l]), m9[0], m9[1], m9[2], ga, up(l, "ffn1_wg"),
                           up(l, "ffn1_wu"), gb, down(l, "ffn1_wd"), 0.5)
        h2, sv2 = _mixer_fwd(f"l{l}_mix", h1, _row(small["ln_mix"][l]), m9[3], m9[4], m9[5], w, sp)
        h3, sv3 = _ffn_fwd(f"l{l}_ffn2", h2, _row(small["ln_ffn2"][l]), m9[6], m9[7], m9[8], ga, up(l, "ffn2_wg"),
                           up(l, "ffn2_wu"), gb, down(l, "ffn2_wd"), 0.5)
        saved.append((h0, h1, h2, sv1, sv2, sv3, sp))
        mods.append(m9)
        h = h3
    dh, loss_part, d_fnorm = _loss_head(h, target, _row(small["final_norm"]))
    wgrads, sgrads, dmods = [], [], []
    for l in reversed(range(depth)):
        h0, h1, h2, sv1, sv2, sv3, sp = saved[l]
        m9 = mods[l]
        w = wl[l]
        dh, g3, s3 = _ffn_bwd(f"l{l}_ffn2", h2, dh, sv3, _row(small["ln_ffn2"][l]), m9[6], m9[7], m9[8], ga,
                              up(l, "ffn2_wg"), up(l, "ffn2_wu"), gb, down(l, "ffn2_wd"), 0.5)
        dh, g2, s2 = _mixer_bwd(f"l{l}_mix", h1, dh, sv2, _row(small["ln_mix"][l]), m9[3], m9[4], m9[5], w, sp)
        dh, g1, s1 = _ffn_bwd(f"l{l}_ffn1", h0, dh, sv1, _row(small["ln_ffn1"][l]), m9[0], m9[1], m9[2], ga,
                              up(l, "ffn1_wg"), up(l, "ffn1_wu"), gb, down(l, "ffn1_wd"), 0.5)
        wgrads.append(dict(ffn1_wg=g1["wg"], ffn1_wu=g1["wu"], ffn1_wd=g1["wd"], ffn2_wg=g3["wg"], ffn2_wu=g3["wu"],
                           ffn2_wd=g3["wd"], **g2))
        dmods.append(jnp.concatenate([s1["sh"], s1["sc"], s1["gt"], s2["sh"], s2["sc"], s2["gt"],
                                      s3["sh"], s3["sc"], s3["gt"]], axis=1))
        sgrads.append(dict(ln_ffn1=s1["ln"][0], ln_mix=s2["ln"][0], ln_ffn2=s3["ln"][0],
                           a_log=s2["alog"][0, DN_HEADS:2 * DN_HEADS], dt_bias=s2["dtb"][0, DN_HEADS:2 * DN_HEADS],
                           dn_norm=s2["dn"][0], conv_w=s2["conv"][:DN_CONV]))
    wgrads.reverse()
    sgrads.reverse()
    dmods.reverse()
    return loss_part[0, 0], dh, jnp.concatenate(dmods, axis=0), wgrads, sgrads, d_fnorm[0]


def _flip(v, bit):
    return 1 - v if bit else v


def _allgather8(name, x):
    r, c = x.shape

    def body(x_ref, out_ref, send_sems, recv_sems, local_sem):
        mx, my, mc = lax.axis_index("x"), lax.axis_index("y"), lax.axis_index("c")
        me = 4 * mx + 2 * my + mc
        mine = pltpu.make_async_copy(x_ref, out_ref.at[me], local_sem)
        mine.start()
        sends = []
        for k in range(1, 8):
            peer = (_flip(mx, k & 4), _flip(my, k & 2), _flip(mc, k & 1))
            cp = pltpu.make_async_remote_copy(src_ref=x_ref, dst_ref=out_ref.at[me], send_sem=send_sems.at[k - 1],
                                              recv_sem=recv_sems.at[k - 1], device_id=peer, device_id_type=MESH)
            cp.start()
            sends.append(cp)
        for k in range(1, 8):
            peer = (_flip(mx, k & 4), _flip(my, k & 2), _flip(mc, k & 1))
            src = 4 * peer[0] + 2 * peer[1] + peer[2]
            pltpu.make_async_remote_copy(src_ref=x_ref, dst_ref=out_ref.at[src], send_sem=send_sems.at[k - 1],
                                         recv_sem=recv_sems.at[k - 1], device_id=peer, device_id_type=MESH).wait_recv()
        for cp in sends:
            cp.wait_send()
        mine.wait()

    return pl.pallas_call(
        body, name=name, out_shape=jax.ShapeDtypeStruct((8, r, c), x.dtype),
        in_specs=[pl.BlockSpec(memory_space=pltpu.VMEM)], out_specs=pl.BlockSpec(memory_space=pltpu.VMEM),
        scratch_shapes=[pltpu.SemaphoreType.DMA((7,)), pltpu.SemaphoreType.DMA((7,)), pltpu.SemaphoreType.DMA],
        compiler_params=_params(9 * _nbytes((r, c), x.dtype)),
    )(x)


def _chip_peers(mx, my):
    chips = [(1 - mx, my), (mx, 1 - my), (1 - mx, 1 - my)]
    return chips, [2 * cx + cy for (cx, cy) in chips]


_ANY = pl.BlockSpec(memory_space=pl.ANY)


def _half(mc, t):
    return pl.ds(mc * (t // 2), t // 2)


def _gather_groups(name, shards):
    ng = len(shards)

    def body(*refs):
        xs, outs = refs[:ng], refs[ng:2 * ng]
        send_sems, recv_sems, local_sems = refs[2 * ng:]
        mx, my, mc = lax.axis_index("x"), lax.axis_index("y"), lax.axis_index("c")
        j = 2 * mx + my
        chips, idxs = _chip_peers(mx, my)
        sib = (mx, my, 1 - mc)

        def copy(k, src, dst, to):
            return pltpu.make_async_remote_copy(src_ref=src, dst_ref=dst, send_sem=send_sems.at[k],
                                                recv_sem=recv_sems.at[k], device_id=to, device_id_type=MESH)

        locals_, first, passed = [], [], []
        for g in range(ng):
            lc = pltpu.make_async_copy(xs[g], outs[g].at[j], local_sems.at[g])
            lc.start()
            locals_.append(lc)
        for g in range(ng):
            mine = _half(mc, shards[g].shape[0])
            for t, chip in enumerate(chips):
                cp = copy(6 * g + t, xs[g].at[mine], outs[g].at[j, mine], (*chip, mc))
                cp.start()
                first.append(cp)
        for g in range(ng):
            mine = _half(mc, shards[g].shape[0])
            for t, chip in enumerate(chips):
                landed = outs[g].at[idxs[t], mine]
                copy(6 * g + t, landed, landed, (*chip, mc)).wait_recv()
                fwd = copy(6 * g + 3 + t, landed, landed, sib)
                fwd.start()
                passed.append(fwd)
        for g in range(ng):
            theirs_half = _half(1 - mc, shards[g].shape[0])
            for t in range(3):
                theirs = outs[g].at[idxs[t], theirs_half]
                copy(6 * g + 3 + t, theirs, theirs, sib).wait_recv()
        for cp in first + passed:
            cp.wait_send()
        for lc in locals_:
            lc.wait()

    return pl.pallas_call(
        body, name=name, out_shape=[jax.ShapeDtypeStruct((4,) + x.shape, x.dtype) for x in shards],
        in_specs=[_ANY] * ng, out_specs=[_ANY] * ng,
        scratch_shapes=[pltpu.SemaphoreType.DMA((6 * ng,)), pltpu.SemaphoreType.DMA((6 * ng,)),
                        pltpu.SemaphoreType.DMA((ng,))],
    )(*shards)


def _pair_swap_groups(name, gs):
    ng = len(gs)

    def body(*refs):
        xs, outs = refs[:ng], refs[ng:2 * ng]
        send_sems, recv_sems = refs[2 * ng:]
        mx, my, mc = lax.axis_index("x"), lax.axis_index("y"), lax.axis_index("c")
        cps = []
        for g in range(ng):
            cp = pltpu.make_async_remote_copy(src_ref=xs[g].at[:, _half(1 - mc, gs[g].shape[1])], dst_ref=outs[g],
                                              send_sem=send_sems.at[g], recv_sem=recv_sems.at[g],
                                              device_id=(mx, my, 1 - mc), device_id_type=MESH)
            cp.start()
            cps.append(cp)
        for cp in cps:
            cp.wait()

    return pl.pallas_call(
        body, name=name,
        out_shape=[jax.ShapeDtypeStruct((x.shape[0], x.shape[1] // 2) + x.shape[2:], x.dtype) for x in gs],
        in_specs=[_ANY] * ng, out_specs=[_ANY] * ng,
        scratch_shapes=[pltpu.SemaphoreType.DMA((ng,)), pltpu.SemaphoreType.DMA((ng,))],
    )(*gs)


def _chip_scatter_groups(name, ps):
    ng = len(ps)

    def body(*refs):
        xs, outs = refs[:ng], refs[ng:2 * ng]
        send_sems, recv_sems, local_sems = refs[2 * ng:]
        mx, my, mc = lax.axis_index("x"), lax.axis_index("y"), lax.axis_index("c")
        j = 2 * mx + my
        chips, idxs = _chip_peers(mx, my)
        locals_, sends = [], []
        for g in range(ng):
            lc = pltpu.make_async_copy(xs[g].at[j], outs[g].at[j], local_sems.at[g])
            lc.start()
            locals_.append(lc)
        for g in range(ng):
            for t, chip in enumerate(chips):
                cp = pltpu.make_async_remote_copy(src_ref=xs[g].at[idxs[t]], dst_ref=outs[g].at[j],
                                                  send_sem=send_sems.at[3 * g + t], recv_sem=recv_sems.at[3 * g + t],
                                                  device_id=(*chip, mc), device_id_type=MESH)
                cp.start()
                sends.append(cp)
        for g in range(ng):
            for t, chip in enumerate(chips):
                pltpu.make_async_remote_copy(src_ref=xs[g].at[idxs[t]], dst_ref=outs[g].at[idxs[t]],
                                             send_sem=send_sems.at[3 * g + t], recv_sem=recv_sems.at[3 * g + t],
                                             device_id=(*chip, mc), device_id_type=MESH).wait_recv()
        for cp in sends:
            cp.wait_send()
        for lc in locals_:
            lc.wait()

    return pl.pallas_call(
        body, name=name, out_shape=[jax.ShapeDtypeStruct(x.shape, x.dtype) for x in ps],
        in_specs=[_ANY] * ng, out_specs=[_ANY] * ng,
        scratch_shapes=[pltpu.SemaphoreType.DMA((3 * ng,)), pltpu.SemaphoreType.DMA((3 * ng,)),
                        pltpu.SemaphoreType.DMA((ng,))],
    )(*ps)


def _pair_merge_groups(name, fs):
    ng = len(fs)

    def body(*refs):
        xs, outs = refs[:ng], refs[ng:2 * ng]
        send_sems, recv_sems, local_sems = refs[2 * ng:]
        mx, my, mc = lax.axis_index("x"), lax.axis_index("y"), lax.axis_index("c")
        locals_, cps = [], []
        for g in range(ng):
            mine = _half(mc, 2 * fs[g].shape[0])
            lc = pltpu.make_async_copy(xs[g], outs[g].at[mine], local_sems.at[g])
            lc.start()
            locals_.append(lc)
            cp = pltpu.make_async_remote_copy(src_ref=xs[g], dst_ref=outs[g].at[mine], send_sem=send_sems.at[g],
                                              recv_sem=recv_sems.at[g], device_id=(mx, my, 1 - mc), device_id_type=MESH)
            cp.start()
            cps.append(cp)
        for g in range(ng):
            theirs = outs[g].at[_half(1 - mc, 2 * fs[g].shape[0])]
            pltpu.make_async_remote_copy(src_ref=xs[g], dst_ref=theirs, send_sem=send_sems.at[g],
                                         recv_sem=recv_sems.at[g], device_id=(mx, my, 1 - mc),
                                         device_id_type=MESH).wait_recv()
        for cp in cps:
            cp.wait_send()
        for lc in locals_:
            lc.wait()

    return pl.pallas_call(
        body, name=name, out_shape=[jax.ShapeDtypeStruct((2 * x.shape[0],) + x.shape[1:], x.dtype) for x in fs],
        in_specs=[_ANY] * ng, out_specs=[_ANY] * ng,
        scratch_shapes=[pltpu.SemaphoreType.DMA((ng,)), pltpu.SemaphoreType.DMA((ng,)), pltpu.SemaphoreType.DMA((ng,))],
    )(*fs)


def _block_rows(r, w, itemsize=4, budget=4 << 20):
    for c in (r, 2048, 1024, 512, 256, 128, 64, 32, 16):
        if c <= r and r % c == 0 and c * w * itemsize <= budget:
            return c
    return r


def _pair_sum(name, g, got, cidx):
    ns, t, r, w = g.shape
    th = t // 2
    bm = _block_rows(r, w)

    def body(c_ref, a_ref, b_ref, o_ref):
        o_ref[...] = (a_ref[...].astype(F32) + b_ref[...].astype(F32)).astype(o_ref.dtype)

    blk = (None, None, bm, w)
    return pl.pallas_call(
        body, name=name,
        grid_spec=pltpu.PrefetchScalarGridSpec(
            num_scalar_prefetch=1, grid=(ns, th, r // bm),
            in_specs=[pl.BlockSpec(blk, lambda s, tt, i, c: (s, c[0] * th + tt, i, 0)),
                      pl.BlockSpec(blk, lambda s, tt, i, c: (s, tt, i, 0))],
            out_specs=pl.BlockSpec(blk, lambda s, tt, i, c: (s, tt, i, 0))),
        out_shape=jax.ShapeDtypeStruct((ns, th, r, w), BF16),
        compiler_params=_params(3 * _nbytes((bm, w), F32)),
    )(cidx, g, got)


def _chip_sum(name, p):
    ns, th, r, w = p.shape
    bm = _block_rows(r, w, budget=2 << 20)

    def body(p_ref, o_ref):
        acc = p_ref[0].astype(F32)
        for s in range(1, ns):
            acc = acc + p_ref[s].astype(F32)
        o_ref[...] = acc

    return pl.pallas_call(
        body, name=name, grid=(th, r // bm),
        in_specs=[pl.BlockSpec((ns, None, bm, w), lambda tt, i: (0, tt, i, 0))],
        out_specs=pl.BlockSpec((None, bm, w), lambda tt, i: (tt, i, 0)),
        out_shape=jax.ShapeDtypeStruct((th, r, w), F32),
        compiler_params=_params(ns * _nbytes((bm, w), BF16) + 2 * _nbytes((bm, w), F32)),
    )(p)


def _sum_leading(name, x):
    n = x.shape[0]

    def body(p_ref, o_ref):
        acc = p_ref[0]
        for s in range(1, n):
            acc = acc + p_ref[s]
        o_ref[...] = acc

    return pl.pallas_call(body, name=name, out_shape=jax.ShapeDtypeStruct(x.shape[1:], F32),
                          compiler_params=_params(2 * _nbytes(x.shape, F32)))(x)


def _reduce_scatter_groups(gs):
    cidx = lax.axis_index("c").astype(jnp.int32).reshape(1)
    got = _pair_swap_groups("rs_pair_swap", gs)
    pair = [_pair_sum(f"rs_pair_sum{i}", g, r_, cidx) for i, (g, r_) in enumerate(zip(gs, got))]
    chips = _chip_scatter_groups("rs_chip_scatter", pair)
    fin = [_chip_sum(f"rs_chip_sum{i}", p) for i, p in enumerate(chips)]
    return _pair_merge_groups("rs_pair_merge", fin)


_GROUPS = ((("ffn1_wg", "ffn1_wu", "ffn2_wg", "ffn2_wu"), 1), (("ffn1_wd", "ffn2_wd"), 0), (("w_a",), 0),
           (("w_o",), 0), (("w_in",), 1), (("w_b",), 1))


def _stack_group(per_layer, names, depth):
    return jnp.stack([per_layer[l][nm] for l in range(depth) for nm in names], axis=0)


def _shard_major(g, ax):
    k, n = g.shape
    if ax == 0:
        return g.reshape(4, k // 4, n)
    return g.reshape(k, 4, n // 4).transpose(1, 0, 2)


def _in_cols(d):
    o1 = 3 * DN_WIDTH
    o2 = o1 + DN_WIDTH
    o3 = o2 + 2 * DN_HEADS
    o4 = o3 + 3 * DA_WIDTH
    return dict(wq=(0, o1), wz=(o1, o2), wba=(o2, o3), wda=(o3, o4), wg=(o4, o4 + 2 * d))


def _mixer_weights(w_in, w_a, w_b, w_o, d):
    w = {k: w_in[:, a:b] for k, (a, b) in _in_cols(d).items()}
    w["wba"] = jnp.pad(w["wba"], ((0, 0), (0, LANES - 2 * DN_HEADS)))
    w["w_a"], w["w_b"], w["w_o"] = w_a, w_b, w_o
    return w


def _w_in_grad(wg):
    return jnp.concatenate([wg["wq"], wg["wz"], wg["wba"][:, :2 * DN_HEADS], wg["wda"], wg["wg"]], axis=1)


def _adam_math(wv, gv, mv, vv):
    mn = ADAM_B1 * mv + (1.0 - ADAM_B1) * gv
    vn = ADAM_B2 * vv + (1.0 - ADAM_B2) * jnp.square(gv)
    m_hat = mn / (1.0 - ADAM_B1 ** ADAM_STEP)
    v_hat = vn / (1.0 - ADAM_B2 ** ADAM_STEP)
    delta = -ADAM_LR * (m_hat / (jnp.sqrt(v_hat) + ADAM_EPS) + ADAM_WD * wv)
    return delta, mn, vn


def _adamw(name, w, g, m, v):
    shape = w.shape
    cols = shape[-1]
    w2, g2, m2, v2 = (t.reshape(-1, cols) for t in (w, g, m, v))
    rows = w2.shape[0]
    bm = _pick(rows, (256, 128, 64, 32, 16, 8)) if rows >= 8 else rows
    delta, mn, vn = _rowwise(name, lambda *t: (_adam_math(*t), ()), [w2, g2, m2, v2], [], [(cols, F32)] * 3, bm=bm)
    return delta.reshape(shape), mn.reshape(shape), vn.reshape(shape)


def _adamw_stacked(name, w, m, v, gstack, stride, slot):
    depth, r, cdim = w.shape
    bm = _block_rows(r, cdim, budget=1 << 20)

    def body(w_ref, g_ref, m_ref, v_ref, go_ref, d_ref, mo_ref, vo_ref):
        gv = g_ref[...]
        go_ref[...] = gv
        d_ref[...], mo_ref[...], vo_ref[...] = _adam_math(w_ref[...], gv, m_ref[...], v_ref[...])

    nat = pl.BlockSpec((None, bm, cdim), lambda l, i: (l, i, 0))
    return pl.pallas_call(
        body, name=name, grid=(depth, r // bm),
        in_specs=[nat, pl.BlockSpec((None, bm, cdim), lambda l, i: (stride * l + slot, i, 0)), nat, nat],
        out_specs=[nat] * 4, out_shape=[jax.ShapeDtypeStruct(w.shape, F32)] * 4,
        compiler_params=_params(8 * _nbytes((bm, cdim), F32)),
    )(w, gstack, m, v)


def kernel(x, c, ada_w, ada_b, ln_ffn1, ln_mix, ln_ffn2, ffn1_wg, ffn1_wu, ffn1_wd, w_in, conv_w, a_log, dt_bias, dn_norm, w_a, w_b, w_o, ffn2_wg, ffn2_wu, ffn2_wd, final_norm, loss_target, m_ada_w, m_ada_b, m_ln_ffn1, m_ln_mix, m_ln_ffn2, m_ffn1_wg, m_ffn1_wu, m_ffn1_wd, m_w_in, m_conv_w, m_a_log, m_dt_bias, m_dn_norm, m_w_a, m_w_b, m_w_o, m_ffn2_wg, m_ffn2_wu, m_ffn2_wd, m_final_norm, v_ada_w, v_ada_b, v_ln_ffn1, v_ln_mix, v_ln_ffn2, v_ffn1_wg, v_ffn1_wu, v_ffn1_wd, v_w_in, v_conv_w, v_a_log, v_dt_bias, v_dn_norm, v_w_a, v_w_b, v_w_o, v_ffn2_wg, v_ffn2_wu, v_ffn2_wd, v_final_norm):
    names = ["ada_w", "ada_b", "ln_ffn1", "ln_mix", "ln_ffn2", "ffn1_wg", "ffn1_wu", "ffn1_wd", "w_in", "conv_w",
             "a_log", "dt_bias", "dn_norm", "w_a", "w_b", "w_o", "ffn2_wg", "ffn2_wu", "ffn2_wd", "final_norm"]
    wts = dict(zip(names, (ada_w, ada_b, ln_ffn1, ln_mix, ln_ffn2, ffn1_wg, ffn1_wu, ffn1_wd, w_in, conv_w, a_log,
                           dt_bias, dn_norm, w_a, w_b, w_o, ffn2_wg, ffn2_wu, ffn2_wd, final_norm)))
    mom = dict(zip(names, (m_ada_w, m_ada_b, m_ln_ffn1, m_ln_mix, m_ln_ffn2, m_ffn1_wg, m_ffn1_wu, m_ffn1_wd, m_w_in,
                           m_conv_w, m_a_log, m_dt_bias, m_dn_norm, m_w_a, m_w_b, m_w_o, m_ffn2_wg, m_ffn2_wu,
                           m_ffn2_wd, m_final_norm)))
    var = dict(zip(names, (v_ada_w, v_ada_b, v_ln_ffn1, v_ln_mix, v_ln_ffn2, v_ffn1_wg, v_ffn1_wu, v_ffn1_wd, v_w_in,
                           v_conv_w, v_a_log, v_dt_bias, v_dn_norm, v_w_a, v_w_b, v_w_o, v_ffn2_wg, v_ffn2_wu,
                           v_ffn2_wd, v_final_norm)))
    _, s, d = x.shape
    depth = ada_w.shape[0]
    mx, my, mc = lax.axis_index("x"), lax.axis_index("y"), lax.axis_index("c")
    chip = 2 * mx + my
    me = 2 * chip + mc
    nshard = ada_w.shape[2]

    cact = _rowwise("c_silu", lambda cv: ((_silu(cv),), ()), [jnp.pad(c, ((0, 7), (0, 0)))], [], [(d, F32)], bm=8)[0]
    c_all = _allgather8("ag_c", cact)[:, 0, :]
    conv_all = _allgather8("ag_conv", jnp.pad(conv_w.reshape(depth * DN_CONV, -1), ((0, 8 - depth * DN_CONV), (0, 0))))
    conv_full = jnp.concatenate([conv_all[2 * j, :depth * DN_CONV] for j in range(4)], axis=1)
    conv_full = conv_full.reshape(depth, DN_CONV, 3 * DN_WIDTH)
    shard_stacks = [_stack_group([{nm: wts[nm][l].astype(BF16) for nm in nms} for l in range(depth)], nms, depth)
                    for nms, _ in _GROUPS]
    ga, gb, g_wa, g_wo, g_win, g_wb = _gather_groups("ag_weights", shard_stacks)
    rows_of = lambda st, l: st[:, l].reshape(-1, st.shape[-1])
    cols_of = lambda st, l: jnp.concatenate([st[j, l] for j in range(4)], axis=1)

    c16 = jnp.pad(c_all, ((0, 8), (0, 0))).astype(BF16)
    parts = []
    for l in range(depth):
        bias = lax.dynamic_slice(ada_b[l], (chip * nshard,), (nshard,)).reshape(1, nshard)
        (mp,) = _matmul(f"ada_fwd{l}", c16, ada_w[l].astype(BF16), epi_bcast=[bias], epi=lambda acc, b: (acc + b,))
        parts.append(mp)
    mod_all = _allgather8("ag_mod", jnp.concatenate(parts, axis=0))
    mod_rows = jnp.concatenate([mod_all[2 * j] for j in range(4)], axis=1)
    mod = jnp.stack([lax.dynamic_index_in_dim(mod_rows, l * 16 + me, axis=0, keepdims=False) for l in range(depth)])

    wl = [_mixer_weights(cols_of(g_win, l), rows_of(g_wa, l), cols_of(g_wb, l), rows_of(g_wo, l), d)
          for l in range(depth)]
    small = dict(conv_w=conv_full, a_log=a_log, dt_bias=dt_bias, dn_norm=dn_norm, ln_ffn1=ln_ffn1, ln_mix=ln_mix,
                 ln_ffn2=ln_ffn2, final_norm=final_norm)
    loss_part, dx, dmod, wgrads, sgrads, d_fnorm = _local_step(x[0], loss_target[0], mod, ga, gb, wl, small)

    dmod_all = _allgather8("ag_dmod", jnp.pad(dmod, ((0, 8 - depth), (0, 0))))
    g_ada_w, g_ada_b = [], []
    for l in range(depth):
        dm_l = dmod_all[:, l, :]
        (gb_l,) = _rowwise(f"ada_b_grad{l}", lambda v: ((), (jnp.sum(v, axis=0, keepdims=True),)), [dm_l], [], [],
                           [(1, N_ADA * d)], bm=8)
        g_ada_b.append(gb_l[0])
        dm_sh = lax.dynamic_slice(dm_l, (0, chip * nshard), (8, nshard))
        (gw_l,) = _matmul(f"ada_w_grad{l}", c16, jnp.pad(dm_sh, ((0, 8), (0, 0))).astype(BF16), ta=True)
        g_ada_w.append(gw_l)
    grads = dict(ada_w=jnp.stack(g_ada_w), ada_b=jnp.stack(g_ada_b))

    smalls = [loss_part.reshape(1), d_fnorm]
    for l in range(depth):
        sg = sgrads[l]
        smalls += [sg["ln_ffn1"], sg["ln_mix"], sg["ln_ffn2"], sg["a_log"], sg["dt_bias"], sg["dn_norm"],
                   sg["conv_w"].reshape(-1)]
    sizes = [t.shape[0] for t in smalls]
    tile = 8 * LANES
    flat = jnp.concatenate([jnp.pad(t, (0, (-t.shape[0]) % tile)).reshape(-1, LANES) for t in smalls], axis=0)
    tot = _sum_leading("small_sum", _allgather8("ag_small", flat))
    offs, acc = [], 0
    for n_ in sizes:
        offs.append(acc)
        acc += -(-n_ // tile) * 8
    take = lambda i: tot[offs[i]:offs[i] + -(-sizes[i] // tile) * 8].reshape(-1)[:sizes[i]]
    loss = take(0)[0]
    grads["final_norm"] = take(1)
    per = 7
    for key_i, key in enumerate(["ln_ffn1", "ln_mix", "ln_ffn2", "a_log", "dt_bias", "dn_norm"]):
        grads[key] = jnp.stack([take(2 + per * l + key_i) for l in range(depth)])
    conv_g = jnp.stack([take(2 + per * l + 6).reshape(DN_CONV, 3 * DN_WIDTH) for l in range(depth)])
    csh = conv_w.shape[2]
    grads["conv_w"] = lax.dynamic_slice(conv_g, (0, 0, chip * csh), (depth, DN_CONV, csh))

    for l in range(depth):
        wgrads[l]["w_in"] = _w_in_grad(wgrads[l])
    ffn_names = _GROUPS[0][0] + _GROUPS[1][0]
    gstacks = []
    for nms, ax in _GROUPS:
        per_layer = [{nm: (wgrads[l][nm] if nm in ffn_names else _shard_major(wgrads[l][nm], ax)) for nm in nms}
                     for l in range(depth)]
        gstacks.append(jnp.stack([per_layer[l][nm] for l in range(depth) for nm in nms], axis=1))
    reduced = _reduce_scatter_groups(gstacks)
    deltas, new_m, new_v = {}, {}, {}
    for (nms, _), red in zip(_GROUPS, reduced):
        for q, nm in enumerate(nms):
            grads[nm], deltas[nm], new_m[nm], new_v[nm] = _adamw_stacked("adamw_" + nm, wts[nm], mom[nm], var[nm], red,
                                                                         len(nms), q)

    for name in names:
        if name in deltas:
            continue
        wv, gv, mv, vv = wts[name], grads[name], mom[name], var[name]
        if wv.ndim == 1:
            wv, gv, mv, vv = (t.reshape(-1, LANES) for t in (wv, gv, mv, vv))
        dl, mn, vn = _adamw("adamw_" + name, wv, gv, mv, vv)
        deltas[name], new_m[name], new_v[name] = (t.reshape(wts[name].shape) for t in (dl, mn, vn))
    return (loss, dx.reshape(1, s, d), *[grads[n_] for n_ in names], *[deltas[n_] for n_ in names],
            *[new_m[n_] for n_ in names], *[new_v[n_] for n_ in names])
```

```python
import functools

import jax
import jax.numpy as jnp
from jax import lax
from jax.experimental import pallas as pl
from jax.experimental.pallas import tpu as pltpu

F32 = jnp.float32
BF16 = jnp.bfloat16
MESH = pl.DeviceIdType.MESH

NORM_EPS = 1e-6
DN_HEADS, DN_DIM, DN_CHUNK, DN_CONV = 8, 128, 64, 4
DN_WIDTH = DN_HEADS * DN_DIM
DA_HEADS, DA_DIM, DA_BLOCK = 12, 64, 128
DA_WIDTH = DA_HEADS * DA_DIM
DA_PATTERNS = ((128, 1), (512, 4), (2048, 16))
ALIBI_MAX_EXP = 8.0
N_ADA = 9
LANES = 128
V7X_VMEM_BYTES = 64 << 20
ADAM_LR, ADAM_B1, ADAM_B2, ADAM_EPS, ADAM_WD, ADAM_STEP = 0.001, 0.9, 0.999, 1e-08, 0.01, 10
NEG = -1e30
HI = lax.Precision.HIGHEST
NN = (((1,), (0,)), ((), ()))
NT = (((1,), (1,)), ((), ()))
TN = (((0,), (0,)), ((), ()))


def _nbytes(shape, dtype):
    n = 1
    for s in shape:
        n *= s
    return n * jnp.dtype(dtype).itemsize


def _params(block_bytes, scratch_bytes=0):
    need = 2 * block_bytes + scratch_bytes
    lim = min(max(need + need // 4 + (4 << 20), 32 << 20), V7X_VMEM_BYTES - (6 << 20))
    return pltpu.CompilerParams(vmem_limit_bytes=int(lim))


def _pick(n, cands):
    for c in cands:
        if c <= n and n % c == 0:
            return c
    return n


def _sigmoid(x):
    return jax.nn.sigmoid(x)


def _silu(x):
    return x * jax.nn.sigmoid(x)


def _softplus(x):
    return jnp.maximum(x, 0.0) + jnp.log(1.0 + jnp.exp(-jnp.abs(x)))


def _rowwise(name, fn, rows, bcast, row_outs, red_outs=(), bm=256):
    rows = [r if isinstance(r, tuple) else (r, r.shape[1], 0) for r in rows]
    s = rows[0][0].shape[0]
    bm = _pick(s, (bm, 128, 64, 32, 16, 8))
    nr, nb, no, nd = len(rows), len(bcast), len(row_outs), len(red_outs)
    in_specs = [pl.BlockSpec((bm, w), functools.partial(lambda i, ci: (i, ci), ci=ci)) for (_, w, ci) in rows]
    in_specs += [pl.BlockSpec(b.shape, lambda i: (0, 0)) for b in bcast]
    out_shape = [jax.ShapeDtypeStruct((s, w), dt) for (w, dt) in row_outs]
    out_shape += [jax.ShapeDtypeStruct((r, w), F32) for (r, w) in red_outs]
    out_specs = [pl.BlockSpec((bm, w), lambda i: (i, 0)) for (w, _) in row_outs]
    out_specs += [pl.BlockSpec((r, w), lambda i: (0, 0)) for (r, w) in red_outs]

    def body(*refs):
        ins = [r[...] for r in refs[:nr + nb]]
        outs = refs[nr + nb:nr + nb + no]
        reds = refs[nr + nb + no:]
        ov, rv = fn(*ins)
        for o, v in zip(outs, ov):
            o[...] = v.astype(o.dtype)
        if nd:
            @pl.when(pl.program_id(0) == 0)
            def _():
                for r in reds:
                    r[...] = jnp.zeros(r.shape, F32)
            for r, v in zip(reds, rv):
                r[...] += v.astype(F32)

    blk = sum(_nbytes((bm, w), a.dtype) for (a, w, _) in rows) + sum(_nbytes(b.shape, b.dtype) for b in bcast)
    blk += sum(_nbytes((bm, w), dt) for (w, dt) in row_outs) + sum(_nbytes(r, F32) for r in red_outs)
    res = pl.pallas_call(
        body, name=name, grid=(s // bm,), in_specs=in_specs, out_specs=out_specs, out_shape=out_shape,
        compiler_params=_params(3 * blk),
    )(*[a for (a, _, _) in rows], *bcast)
    return res


def _matmul(name, a, b, *, ta=False, tb=False, outs=(F32,), epi=None, epi_rows=(), epi_bcast=(),
            bm=None, bn=None, bk=None):
    if ta:
        k, m = a.shape
    else:
        m, k = a.shape
    n = b.shape[0] if tb else b.shape[1]
    assert (b.shape[1] if tb else b.shape[0]) == k, (name, a.shape, b.shape)
    if bm is None:
        bm = _pick(m, (1024, 1408, 768, 512, 384, 256, 128)) if ta else _pick(m, (1024, 512, 256, 128, 64, 32, 16))
    if bn is None:
        bn = _pick(n, (512, 384, 256, 128))
    if bk is None:
        bk = k if k <= 3072 else _pick(k, (2816, 2048, 1024, 512))
        if ta:
            bk = _pick(k, (1024, 512, 256, 128, 64, 32, 16))
    nk = k // bk
    dims = TN if ta else (NT if tb else NN)
    a_spec = pl.BlockSpec((bk, bm), lambda i, j, kk: (kk, i)) if ta else pl.BlockSpec((bm, bk), lambda i, j, kk: (i, kk))
    b_spec = pl.BlockSpec((bn, bk), lambda i, j, kk: (j, kk)) if tb else pl.BlockSpec((bk, bn), lambda i, j, kk: (kk, j))
    in_specs = [a_spec, b_spec]
    in_specs += [pl.BlockSpec((bm, bn), lambda i, j, kk: (i, j)) for _ in epi_rows]
    in_specs += [pl.BlockSpec((1, bn), lambda i, j, kk: (0, j)) for _ in epi_bcast]
    out_shape = [jax.ShapeDtypeStruct((m, n), dt) for dt in outs]
    out_specs = [pl.BlockSpec((bm, bn), lambda i, j, kk: (i, j)) for _ in outs]
    ner, neb, no = len(epi_rows), len(epi_bcast), len(outs)

    def body(*refs):
        a_ref, b_ref = refs[0], refs[1]
        extra = refs[2:2 + ner + neb]
        out_refs = refs[2 + ner + neb:2 + ner + neb + no]
        prod = lax.dot_general(a_ref[...], b_ref[...], dims, preferred_element_type=F32)

        def finish(acc):
            vals = epi(acc, *[r[...] for r in extra]) if epi is not None else (acc,)
            for o, v in zip(out_refs, vals):
                o[...] = v.astype(o.dtype)

        if nk == 1:
            finish(prod)
        else:
            acc_ref = refs[-1]
            kk = pl.program_id(2)

            @pl.when(kk == 0)
            def _():
                acc_ref[...] = prod

            @pl.when(kk > 0)
            def _():
                acc_ref[...] += prod

            @pl.when(kk == nk - 1)
            def _():
                finish(acc_ref[...])

    blk = _nbytes((bm, bk), a.dtype) + _nbytes((bk, bn), b.dtype)
    blk += sum(_nbytes((bm, bn), r.dtype) for r in epi_rows) + sum(_nbytes((bm, bn), dt) for dt in outs)
    scratch = [pltpu.VMEM((bm, bn), F32)] if nk > 1 else []
    res = pl.pallas_call(
        body, name=name, grid=(m // bm, n // bn, nk), in_specs=in_specs, out_specs=out_specs,
        out_shape=out_shape, scratch_shapes=scratch,
        compiler_params=_params(blk, 3 * _nbytes((bm, bn), F32)),
    )(a, b, *epi_rows, *epi_bcast)
    return res


def _mm_core(name, grid, nk, pairs, out_defs, acc_shape, epi=None, epi_ins=()):
    npair, nep, no = len(pairs), len(epi_ins), len(out_defs)

    def body(*refs):
        extra = refs[2 * npair:2 * npair + nep]
        out_refs = refs[2 * npair + nep:2 * npair + nep + no]
        prod = None
        for p in range(npair):
            d = lax.dot_general(refs[2 * p][...], refs[2 * p + 1][...], pairs[p][4], preferred_element_type=F32)
            prod = d if prod is None else prod + d

        def finish(acc):
            vals = epi(acc, *[r[...] for r in extra]) if epi is not None else (acc,)
            for o, v in zip(out_refs, vals):
                o[...] = v.astype(o.dtype)

        if nk == 1:
            finish(prod)
        else:
            acc_ref = refs[-1]
            kk = pl.program_id(2)

            @pl.when(kk == 0)
            def _():
                acc_ref[...] = prod

            @pl.when(kk > 0)
            def _():
                acc_ref[...] += prod

            @pl.when(kk == nk - 1)
            def _():
                finish(acc_ref[...])

    def blk_bytes(spec, dtype):
        return _nbytes([s for s in spec.block_shape if s is not None], dtype)

    blk = sum(blk_bytes(sa, a.dtype) + blk_bytes(sb, b.dtype) for (a, sa, b, sb, _) in pairs)
    blk += sum(blk_bytes(sp, arr.dtype) for (arr, sp) in epi_ins) + sum(blk_bytes(sp, dt) for (_, dt, sp) in out_defs)
    ins, in_specs = [], []
    for (a, sa, b, sb, _) in pairs:
        ins += [a, b]
        in_specs += [sa, sb]
    ins += [arr for (arr, _) in epi_ins]
    in_specs += [sp for (_, sp) in epi_ins]
    return pl.pallas_call(
        body, name=name, grid=grid, in_specs=in_specs, out_specs=[sp for (_, _, sp) in out_defs],
        out_shape=[jax.ShapeDtypeStruct(sh, dt) for (sh, dt, _) in out_defs],
        scratch_shapes=[pltpu.VMEM(acc_shape, F32)] if nk > 1 else [],
        compiler_params=_params(blk, 3 * _nbytes(acc_shape, F32)),
    )(*ins)


def _rms_mod(h, ln, sh, sc):
    n = h * lax.rsqrt(jnp.mean(h * h, axis=-1, keepdims=True) + NORM_EPS) * ln
    return n * (1.0 + sc) + sh


def _swiglu_act(g, u):
    return _silu(g.astype(F32)) * u.astype(F32)


def _dn_prep(yc, pba, alog, dtb):
    act = _silu(yc)
    parts = []
    for idx in range(2 * DN_HEADS):
        seg = act[:, idx * DN_DIM:(idx + 1) * DN_DIM]
        seg = seg * lax.rsqrt(jnp.sum(seg * seg, axis=-1, keepdims=True) + NORM_EPS)
        if idx < DN_HEADS:
            seg = seg * (DN_DIM ** -0.5)
        parts.append(seg)
    parts.append(act[:, 2 * DN_WIDTH:])
    qkvn = jnp.concatenate(parts, axis=1)
    lane = lax.broadcasted_iota(jnp.int32, pba.shape, 1)
    beta = _sigmoid(pba)
    g = -jnp.exp(alog) * _softplus(pba + dtb)
    gb = jnp.where(lane < DN_HEADS, beta, jnp.where(lane < 2 * DN_HEADS, g, 0.0))
    return qkvn, gb


def _dn_outnorm(o_a, z, dn):
    parts = []
    for h in range(DN_HEADS):
        seg = o_a[:, h * DN_DIM:(h + 1) * DN_DIM]
        seg = seg * lax.rsqrt(jnp.mean(seg * seg, axis=-1, keepdims=True) + NORM_EPS) * dn
        parts.append(seg)
    return jnp.concatenate(parts, axis=1) * _silu(z)


def _shift_down(x, halo8, s):
    r = pltpu.roll(x, s, axis=0)
    top = pltpu.roll(halo8, s, axis=0)
    i8 = lax.broadcasted_iota(jnp.int32, top.shape, 0)
    return jnp.concatenate([jnp.where(i8 < s, top, r[0:8]), r[8:]], axis=0)


def _shift_up(x, halo8, s):
    m = x.shape[0]
    r = pltpu.roll(x, m - s, axis=0)
    bot = pltpu.roll(halo8, 8 - s, axis=0)
    i8 = lax.broadcasted_iota(jnp.int32, bot.shape, 0)
    return jnp.concatenate([r[:m - 8], jnp.where(i8 >= 8 - s, bot, r[m - 8:])], axis=0)


def _conv_prep_fwd(name, pq, convw8, pba, alog, dtb, bm=256):
    s, w = pq.shape
    nblk = s // bm
    hb = bm // 16

    def body(x_ref, halo_ref, w_ref, pba_ref, alog_ref, dtb_ref, yc_ref, qkv_ref, gb_ref):
        i = pl.program_id(0)
        x = x_ref[...].astype(F32)
        halo = jnp.where(i > 0, halo_ref[...].astype(F32)[8:16], 0.0)
        cw = w_ref[...]
        y = x * cw[DN_CONV - 1:DN_CONV]
        for sft in range(1, DN_CONV):
            y = y + _shift_down(x, halo, sft) * cw[DN_CONV - 1 - sft:DN_CONV - sft]
        ycb = y.astype(BF16)
        yc_ref[...] = ycb
        qkvn, gb = _dn_prep(ycb.astype(F32), pba_ref[...], alog_ref[...], dtb_ref[...])
        qkv_ref[...] = qkvn.astype(BF16)
        gb_ref[...] = gb

    blk = 3 * _nbytes((bm, w), BF16) + 4 * _nbytes((bm, w), F32)
    return pl.pallas_call(
        body, name=name, grid=(nblk,),
        in_specs=[pl.BlockSpec((bm, w), lambda i: (i, 0)),
                  pl.BlockSpec((16, w), lambda i: (jnp.maximum(i * hb - 1, 0), 0)),
                  pl.BlockSpec(convw8.shape, lambda i: (0, 0)),
                  pl.BlockSpec((bm, LANES), lambda i: (i, 0)),
                  pl.BlockSpec((1, LANES), lambda i: (0, 0)),
                  pl.BlockSpec((1, LANES), lambda i: (0, 0))],
        out_specs=[pl.BlockSpec((bm, w), lambda i: (i, 0)), pl.BlockSpec((bm, w), lambda i: (i, 0)),
                   pl.BlockSpec((bm, LANES), lambda i: (i, 0))],
        out_shape=[jax.ShapeDtypeStruct((s, w), BF16), jax.ShapeDtypeStruct((s, w), BF16),
                   jax.ShapeDtypeStruct((s, LANES), F32)],
        compiler_params=_params(blk),
    )(pq, pq, convw8, pba, alog, dtb)


def _conv_bwd(name, dyc, pq, convw8, bm=256):
    s, w = pq.shape
    nblk = s // bm
    hb = bm // 16

    def body(dy_ref, dyn_ref, x_ref, xh_ref, w_ref, dx_ref, dw_ref):
        i = pl.program_id(0)
        dy = dy_ref[...].astype(F32)
        nxt = jnp.where(i < nblk - 1, dyn_ref[...].astype(F32)[0:8], 0.0)
        x = x_ref[...].astype(F32)
        halo = jnp.where(i > 0, xh_ref[...].astype(F32)[8:16], 0.0)
        cw = w_ref[...]
        dx = dy * cw[DN_CONV - 1:DN_CONV]
        for sft in range(1, DN_CONV):
            dx = dx + _shift_up(dy, nxt, sft) * cw[DN_CONV - 1 - sft:DN_CONV - sft]
        dx_ref[...] = dx.astype(dx_ref.dtype)
        r8 = lax.broadcasted_iota(jnp.int32, (8, w), 0)
        dw = jnp.zeros((8, w), F32)
        for j in range(DN_CONV):
            sft = DN_CONV - 1 - j
            xs = x if sft == 0 else _shift_down(x, halo, sft)
            dw = dw + jnp.where(r8 == j, jnp.sum(dy * xs, axis=0, keepdims=True), 0.0)

        @pl.when(i == 0)
        def _():
            dw_ref[...] = jnp.zeros((8, w), F32)
        dw_ref[...] += dw

    blk = 4 * _nbytes((bm, w), BF16) + 5 * _nbytes((bm, w), F32)
    return pl.pallas_call(
        body, name=name, grid=(nblk,),
        in_specs=[pl.BlockSpec((bm, w), lambda i: (i, 0)),
                  pl.BlockSpec((16, w), lambda i: (jnp.minimum((i + 1) * hb, s // 16 - 1), 0)),
                  pl.BlockSpec((bm, w), lambda i: (i, 0)),
                  pl.BlockSpec((16, w), lambda i: (jnp.maximum(i * hb - 1, 0), 0)),
                  pl.BlockSpec(convw8.shape, lambda i: (0, 0))],
        out_specs=[pl.BlockSpec((bm, w), lambda i: (i, 0)), pl.BlockSpec((8, w), lambda i: (0, 0))],
        out_shape=[jax.ShapeDtypeStruct((s, w), BF16), jax.ShapeDtypeStruct((8, w), F32)],
        compiler_params=_params(blk),
    )(dyc, dyc, pq, pq, convw8)


BNN = (((2,), (1,)), ((0,), (0,)))
BNT = (((2,), (2,)), ((0,), (0,)))
BTN = (((1,), (1,)), ((0,), (0,)))


def _raw_dot_1pass(a, b, dims):
    return lax.dot_general(a.astype(BF16), b.astype(BF16), dims, preferred_element_type=F32)


def _raw_dot_3pass(a, b, dims):
    ah = a.astype(BF16)
    al = (a - ah.astype(F32)).astype(BF16)
    bh = b.astype(BF16)
    bl = (b - bh.astype(F32)).astype(BF16)
    d = lambda x, y: lax.dot_general(x, y, dims, preferred_element_type=F32)
    return d(ah, bh) + (d(ah, bl) + d(al, bh))


def _with_same_precision_vjp(raw):
    @functools.partial(jax.custom_vjp, nondiff_argnums=(2,))
    def dot(a, b, dims):
        return raw(a, b, dims)

    def fwd(a, b, dims):
        return raw(a, b, dims), (a, b)

    def bwd(dims, res, ct):
        a, b = res
        if dims == BNN:
            return raw(ct, b, BNT), raw(a, ct, BTN)
        if dims == BNT:
            return raw(ct, b, BNN), raw(ct, a, BTN)
        assert dims == BTN
        return raw(b, ct, BNT), raw(a, ct, BNN)

    dot.defvjp(fwd, bwd)
    return dot


_dot_1pass_vjp = _with_same_precision_vjp(_raw_dot_1pass)
_dot_3pass_vjp = _with_same_precision_vjp(_raw_dot_3pass)


def _dot_bf16(a, b, dims=BNN):
    return _dot_1pass_vjp(a, b, dims)


def _dot_3pass(a, b, dims=BNN):
    return _dot_3pass_vjp(a, b, dims)


def _delta_chunk(q, k, v, gcol, bcol, state):
    h, c, _ = q.shape
    row = lax.broadcasted_iota(jnp.int32, (h, c, c), 1)
    col = lax.broadcasted_iota(jnp.int32, (h, c, c), 2)
    incl, strict, eye = row >= col, row > col, row == col
    g_b = jnp.broadcast_to(gcol, (h, c, c))
    gc_row = jnp.sum(jnp.where(row <= col, g_b, 0.0), axis=1, keepdims=True)
    g_r = jnp.sum(jnp.where(eye, g_b, 0.0), axis=1, keepdims=True)
    gc_col = jnp.sum(jnp.where(incl, jnp.broadcast_to(g_r, (h, c, c)), 0.0), axis=2, keepdims=True)
    decay = jnp.exp(jnp.where(incl, gc_col - gc_row, NEG))
    kb = k * bcol
    vb = v * bcol
    x = -jnp.where(strict, _dot_bf16(kb, k, BNT) * decay, 0.0)
    t = jnp.where(eye, 1.0, 0.0) + x
    p = x
    for _ in range(5):
        p = _dot_3pass(p, p)
        t = t + _dot_3pass(t, p)
    eg = jnp.exp(gc_col)
    u = _dot_3pass(t, vb)
    w = _dot_3pass(t, kb * eg)
    qk = _dot_bf16(q, k, BNT) * decay
    v_new = u - _dot_bf16(w, state)
    o = _dot_bf16(q * eg, state) + _dot_bf16(qk, v_new)
    g_last = jnp.sum(g_r, axis=2, keepdims=True)
    new_state = state * jnp.exp(g_last) + _dot_bf16(k * jnp.exp(g_last - gc_col), v_new, BTN)
    return o, new_state


def _lane_col(blk, idx):
    lane = lax.broadcasted_iota(jnp.int32, blk.shape, 1)
    return jnp.sum(jnp.where(lane == idx, blk, 0.0), axis=1, keepdims=True)


def _dn_heads(ref, base):
    return jnp.stack([ref[:, base + h * DN_DIM:base + (h + 1) * DN_DIM] for h in range(DN_HEADS)], axis=0).astype(F32)


def _dn_cols(gbv, base):
    return jnp.stack([_lane_col(gbv, base + h) for h in range(DN_HEADS)], axis=0)


def _delta_fwd(name, qkvn, gb):
    s = qkvn.shape[0]
    n = s // DN_CHUNK
    c = DN_CHUNK

    def body(qkv_ref, gb_ref, o_ref, st_ref, state):
        @pl.when(pl.program_id(0) == 0)
        def _():
            state[...] = jnp.zeros(state.shape, F32)

        gbv = gb_ref[...]
        st = state[...]
        st_ref[0] = st
        o, new = _delta_chunk(_dn_heads(qkv_ref, 0), _dn_heads(qkv_ref, DN_WIDTH), _dn_heads(qkv_ref, 2 * DN_WIDTH),
                              _dn_cols(gbv, DN_HEADS), _dn_cols(gbv, 0), st)
        for h in range(DN_HEADS):
            o_ref[:, h * DN_DIM:(h + 1) * DN_DIM] = o[h]
        state[...] = new

    blk = _nbytes((c, 3 * DN_WIDTH), BF16) + _nbytes((c, LANES), F32) + _nbytes((c, DN_WIDTH), F32)
    blk += _nbytes((DN_HEADS, DN_DIM, DN_DIM), F32)
    return pl.pallas_call(
        body, name=name, grid=(n,),
        in_specs=[pl.BlockSpec((c, 3 * DN_WIDTH), lambda i: (i, 0)), pl.BlockSpec((c, LANES), lambda i: (i, 0))],
        out_specs=[pl.BlockSpec((c, DN_WIDTH), lambda i: (i, 0)),
                   pl.BlockSpec((1, DN_HEADS, DN_DIM, DN_DIM), lambda i: (i, 0, 0, 0))],
        out_shape=[jax.ShapeDtypeStruct((s, DN_WIDTH), F32),
                   jax.ShapeDtypeStruct((n, DN_HEADS, DN_DIM, DN_DIM), F32)],
        scratch_shapes=[pltpu.VMEM((DN_HEADS, DN_DIM, DN_DIM), F32)],
        compiler_params=_params(blk, 8 << 20),
    )(qkvn, gb)


def _delta_bwd(name, qkvn, gb, states, d_o):
    s = qkvn.shape[0]
    n = s // DN_CHUNK
    c = DN_CHUNK

    def body(qkv_ref, gb_ref, st_ref, do_ref, dqkv_ref, dgb_ref, dstate):
        @pl.when(pl.program_id(0) == 0)
        def _():
            dstate[...] = jnp.zeros(dstate.shape, F32)

        gbv = gb_ref[...]
        lane = lax.broadcasted_iota(jnp.int32, (c, LANES), 1)
        _, vjp = jax.vjp(_delta_chunk, _dn_heads(qkv_ref, 0), _dn_heads(qkv_ref, DN_WIDTH),
                         _dn_heads(qkv_ref, 2 * DN_WIDTH), _dn_cols(gbv, DN_HEADS), _dn_cols(gbv, 0), st_ref[0])
        dq, dk, dv, dg, db, dst = vjp((_dn_heads(do_ref, 0), dstate[...]))
        dgb = jnp.zeros((c, LANES), F32)
        for h in range(DN_HEADS):
            dqkv_ref[:, h * DN_DIM:(h + 1) * DN_DIM] = dq[h]
            dqkv_ref[:, DN_WIDTH + h * DN_DIM:DN_WIDTH + (h + 1) * DN_DIM] = dk[h]
            dqkv_ref[:, 2 * DN_WIDTH + h * DN_DIM:2 * DN_WIDTH + (h + 1) * DN_DIM] = dv[h]
            dgb = dgb + jnp.where(lane == h, db[h], 0.0) + jnp.where(lane == DN_HEADS + h, dg[h], 0.0)
        dstate[...] = dst
        dgb_ref[...] = dgb

    rev = lambda i: (n - 1 - i, 0)
    blk = _nbytes((c, 3 * DN_WIDTH), BF16) + 2 * _nbytes((c, LANES), F32) + _nbytes((c, DN_WIDTH), F32)
    blk += _nbytes((DN_HEADS, DN_DIM, DN_DIM), F32) + _nbytes((c, 3 * DN_WIDTH), F32)
    return pl.pallas_call(
        body, name=name, grid=(n,),
        in_specs=[pl.BlockSpec((c, 3 * DN_WIDTH), rev), pl.BlockSpec((c, LANES), rev),
                  pl.BlockSpec((1, DN_HEADS, DN_DIM, DN_DIM), lambda i: (n - 1 - i, 0, 0, 0)),
                  pl.BlockSpec((c, DN_WIDTH), rev)],
        out_specs=[pl.BlockSpec((c, 3 * DN_WIDTH), rev), pl.BlockSpec((c, LANES), rev)],
        out_shape=[jax.ShapeDtypeStruct((s, 3 * DN_WIDTH), F32), jax.ShapeDtypeStruct((s, LANES), F32)],
        scratch_shapes=[pltpu.VMEM((DN_HEADS, DN_DIM, DN_DIM), F32)],
        compiler_params=_params(blk, 16 << 20),
    )(qkvn, gb, states, d_o)


def _da_scores(q2f, k2, sub, valid, distf, head):
    lane = lax.broadcasted_iota(jnp.int32, q2f.shape, 1)
    hmask = (lane < DA_DIM) if sub == 0 else (lane >= DA_DIM)
    qm = jnp.where(hmask, q2f, 0.0).astype(BF16)
    slope = 2.0 ** (-ALIBI_MAX_EXP * (head + 1) / DA_HEADS)
    sc = lax.dot_general(qm, k2, NT, preferred_element_type=F32) * (DA_DIM ** -0.5)
    return jnp.where(valid, sc - slope * distf, NEG), qm, hmask


def _da_mask(i, r):
    qi = lax.broadcasted_iota(jnp.int32, (DA_BLOCK, 2 * DA_BLOCK), 0)
    ki = lax.broadcasted_iota(jnp.int32, (DA_BLOCK, 2 * DA_BLOCK), 1)
    dist = qi + DA_BLOCK - ki
    valid = (dist >= 0) & (dist <= DA_BLOCK) & ((ki >= DA_BLOCK) | (i > 0))
    return valid, (dist * r).astype(F32)


def _da_fwd(name, pda, r):
    s = pda.shape[0]
    n = s // r
    nb = n // DA_BLOCK
    w = DA_WIDTH
    dav = pda.reshape(n, r * 3 * w)

    def body(q_ref, kc_ref, kp_ref, vc_ref, vp_ref, o_ref, lse_ref):
        i = pl.program_id(1)
        valid, distf = _da_mask(i, r)
        lane = lax.broadcasted_iota(jnp.int32, (DA_BLOCK, LANES), 1)
        lse = jnp.zeros((DA_BLOCK, LANES), F32)
        for hp in range(DA_HEADS // 2):
            sl = slice(hp * LANES, (hp + 1) * LANES)
            q2f = q_ref[:, sl].astype(F32)
            k2 = jnp.concatenate([kp_ref[:, sl], kc_ref[:, sl]], axis=0)
            v2 = jnp.concatenate([vp_ref[:, sl], vc_ref[:, sl]], axis=0)
            o2 = None
            for sub in range(2):
                head = 2 * hp + sub
                sc, _, hmask = _da_scores(q2f, k2, sub, valid, distf, head)
                mx = jnp.max(sc, axis=1, keepdims=True)
                p = jnp.exp(sc - mx)
                l = jnp.sum(p, axis=1, keepdims=True)
                pv = lax.dot_general(p.astype(BF16), v2, NN, preferred_element_type=F32) / l
                o2 = pv if sub == 0 else jnp.where(hmask, pv, o2)
                lse = jnp.where(lane == head, mx + jnp.log(l), lse)
            o_ref[:, sl] = o2
        lse_ref[...] = lse

    prev = lambda col: (lambda p, i: (jnp.maximum(i - 1, 0), 3 * p + col))
    cur = lambda col: (lambda p, i: (i, 3 * p + col))
    blk = 5 * _nbytes((DA_BLOCK, w), BF16) + _nbytes((DA_BLOCK, w), F32) + _nbytes((DA_BLOCK, LANES), F32)
    o, lse = pl.pallas_call(
        body, name=name, grid=(r, nb),
        in_specs=[pl.BlockSpec((DA_BLOCK, w), cur(0)), pl.BlockSpec((DA_BLOCK, w), cur(1)),
                  pl.BlockSpec((DA_BLOCK, w), prev(1)), pl.BlockSpec((DA_BLOCK, w), cur(2)),
                  pl.BlockSpec((DA_BLOCK, w), prev(2))],
        out_specs=[pl.BlockSpec((DA_BLOCK, w), lambda p, i: (i, p)),
                   pl.BlockSpec((DA_BLOCK, LANES), lambda p, i: (i, p))],
        out_shape=[jax.ShapeDtypeStruct((n, r * w), F32), jax.ShapeDtypeStruct((n, r * LANES), F32)],
        compiler_params=_params(blk, 8 << 20),
    )(dav, dav, dav, dav, dav)
    return o.reshape(s, w), lse.reshape(s, LANES)


def _da_bwd(name, pda, d_ob, lse_tot, delta, r):
    s = pda.shape[0]
    n = s // r
    nb = n // DA_BLOCK
    w = DA_WIDTH
    dav = pda.reshape(n, r * 3 * w)
    dov = d_ob.reshape(n, r * w)
    lv = lse_tot.reshape(n, r * LANES)
    dlv = delta.reshape(n, r * LANES)

    def body(q_ref, kc_ref, kp_ref, vc_ref, vp_ref, do_ref, l_ref, dl_ref, dq_ref, dk_ref, dv_ref, ck, cv):
        i = pl.program_id(1)

        @pl.when(i == 0)
        def _():
            ck[...] = jnp.zeros(ck.shape, F32)
            cv[...] = jnp.zeros(cv.shape, F32)

        @pl.when(i < nb)
        def _():
            valid, distf = _da_mask(i, r)
            lsev = l_ref[...]
            dlt = dl_ref[...]
            for hp in range(DA_HEADS // 2):
                sl = slice(hp * LANES, (hp + 1) * LANES)
                q2f = q_ref[:, sl].astype(F32)
                k2 = jnp.concatenate([kp_ref[:, sl], kc_ref[:, sl]], axis=0)
                v2 = jnp.concatenate([vp_ref[:, sl], vc_ref[:, sl]], axis=0)
                do2f = do_ref[:, sl].astype(F32)
                dq2 = jnp.zeros((DA_BLOCK, LANES), F32)
                dk2 = jnp.zeros((2 * DA_BLOCK, LANES), F32)
                dv2 = jnp.zeros((2 * DA_BLOCK, LANES), F32)
                for sub in range(2):
                    head = 2 * hp + sub
                    sc, qm, hmask = _da_scores(q2f, k2, sub, valid, distf, head)
                    p = jnp.exp(sc - _lane_col(lsev, head))
                    dom = jnp.where(hmask, do2f, 0.0).astype(BF16)
                    dp = lax.dot_general(dom, v2, NT, preferred_element_type=F32)
                    ds = (p * (dp - _lane_col(dlt, head)) * (DA_DIM ** -0.5)).astype(BF16)
                    dq2 = dq2 + jnp.where(hmask, lax.dot_general(ds, k2, NN, preferred_element_type=F32), 0.0)
                    dk2 = dk2 + lax.dot_general(ds, qm, TN, preferred_element_type=F32)
                    dv2 = dv2 + lax.dot_general(p.astype(BF16), dom, TN, preferred_element_type=F32)
                dq_ref[:, sl] = dq2
                dk_ref[:, sl] = ck[:, sl] + dk2[:DA_BLOCK]
                dv_ref[:, sl] = cv[:, sl] + dv2[:DA_BLOCK]
                ck[:, sl] = dk2[DA_BLOCK:]
                cv[:, sl] = dv2[DA_BLOCK:]

        @pl.when(i == nb)
        def _():
            dk_ref[...] = ck[...]
            dv_ref[...] = cv[...]

    qrow = lambda i: jnp.minimum(i, nb - 1)
    prev = lambda col: (lambda p, i: (jnp.maximum(qrow(i) - 1, 0), 3 * p + col))
    cur = lambda col: (lambda p, i: (qrow(i), 3 * p + col))
    same = lambda p, i: (qrow(i), p)
    late = lambda p, i: (jnp.maximum(i - 1, 0), p)
    blk = 6 * _nbytes((DA_BLOCK, w), BF16) + 2 * _nbytes((DA_BLOCK, LANES), F32) + 3 * _nbytes((DA_BLOCK, w), F32)
    dq, dk, dv = pl.pallas_call(
        body, name=name, grid=(r, nb + 1),
        in_specs=[pl.BlockSpec((DA_BLOCK, w), cur(0)), pl.BlockSpec((DA_BLOCK, w), cur(1)),
                  pl.BlockSpec((DA_BLOCK, w), prev(1)), pl.BlockSpec((DA_BLOCK, w), cur(2)),
                  pl.BlockSpec((DA_BLOCK, w), prev(2)), pl.BlockSpec((DA_BLOCK, w), same),
                  pl.BlockSpec((DA_BLOCK, LANES), same), pl.BlockSpec((DA_BLOCK, LANES), same)],
        out_specs=[pl.BlockSpec((DA_BLOCK, w), same), pl.BlockSpec((DA_BLOCK, w), late),
                   pl.BlockSpec((DA_BLOCK, w), late)],
        out_shape=[jax.ShapeDtypeStruct((n, r * w), F32)] * 3,
        scratch_shapes=[pltpu.VMEM((DA_BLOCK, w), F32), pltpu.VMEM((DA_BLOCK, w), F32)],
        compiler_params=_params(blk, 12 << 20),
    )(dav, dav, dav, dav, dav, dov, lv, dlv)
    return dq.reshape(s, w), dk.reshape(s, w), dv.reshape(s, w)


def _head_expand():
    hrow = lax.broadcasted_iota(jnp.int32, (LANES, DA_WIDTH), 0)
    lcol = lax.broadcasted_iota(jnp.int32, (LANES, DA_WIDTH), 1)
    return jnp.where(lcol // DA_DIM == hrow, 1.0, 0.0).astype(F32)


def _ffn_up(name, a, ga, tg, tu):
    s, d = a.shape
    nsh, _, _, ffs = ga.shape
    bm = _pick(s, (1024, 512, 256, 128))

    def body(a_ref, wg_ref, wu_ref, g_ref, u_ref, f_ref):
        av = a_ref[...]
        g = lax.dot_general(av, wg_ref[...], NN, preferred_element_type=F32)
        u = lax.dot_general(av, wu_ref[...], NN, preferred_element_type=F32)
        g_ref[...] = g.astype(BF16)
        u_ref[...] = u.astype(BF16)
        f_ref[...] = (_silu(g) * u).astype(BF16)

    wspec = lambda t: pl.BlockSpec((None, None, d, ffs), lambda i, j: (j, t, 0, 0))
    ospec = pl.BlockSpec((None, bm, ffs), lambda i, j: (j, i, 0))
    blk = _nbytes((bm, d), BF16) + 2 * _nbytes((d, ffs), BF16) + 3 * _nbytes((bm, ffs), BF16)
    return pl.pallas_call(
        body, name=name, grid=(s // bm, nsh),
        in_specs=[pl.BlockSpec((bm, d), lambda i, j: (i, 0)), wspec(tg), wspec(tu)],
        out_specs=[ospec] * 3, out_shape=[jax.ShapeDtypeStruct((nsh, s, ffs), BF16)] * 3,
        compiler_params=_params(blk, 4 * _nbytes((bm, ffs), F32)),
    )(a, ga, ga)


def _ffn_fwd(tag, h_in, ln, sh, sc, gt, ga, tg, tu, gb, td, weight):
    s, d = h_in.shape
    nsh, _, ffs, _ = gb.shape
    (a,) = _rowwise(tag + "_norm", lambda h, l, s1, s2: ((_rms_mod(h, l, s1, s2),), ()), [h_in], [ln, sh, sc],
                    [(d, BF16)])
    g, u, f = _ffn_up(tag + "_up", a, ga, tg, tu)
    bm, bn = _pick(s, (1024, 512, 256, 128)), _pick(d, (512, 256, 128))
    io = pl.BlockSpec((bm, bn), lambda i, j, kk: (i, j))
    h_out, o = _mm_core(
        tag + "_down", (s // bm, d // bn, nsh), nsh,
        [(f, pl.BlockSpec((None, bm, ffs), lambda i, j, kk: (kk, i, 0)),
          gb, pl.BlockSpec((None, None, ffs, bn), lambda i, j, kk: (kk, td, 0, j)), NN)],
        [((s, d), F32, io), ((s, d), BF16, io)], (bm, bn),
        epi=lambda acc, h, gv: (h + weight * gv * acc, acc),
        epi_ins=[(h_in, io), (gt, pl.BlockSpec((1, bn), lambda i, j, kk: (0, j)))])
    return h_out, dict(a=a, g=g, u=u, f=f, o=o)


def _resid_bwd(tag, dh_out, o, gt, weight):
    d = dh_out.shape[1]

    def fn(dh, ov, g):
        return (weight * g * dh,), (jnp.sum(weight * dh * ov.astype(F32), axis=0, keepdims=True),)

    do, d_gt = _rowwise(tag + "_resid_bwd", fn, [dh_out, o], [gt], [(d, BF16)], [(1, d)])
    return do, d_gt


def _norm_bwd(tag, h_in, da, dh_out, ln, sh, sc):
    d = h_in.shape[1]

    def fn(h, dav, dh, l, s1, s2):
        _, vjp = jax.vjp(_rms_mod, h, l, s1, s2)
        gh, gl, gs1, gs2 = vjp(dav)
        return (dh + gh,), (gl, gs1, gs2)

    return _rowwise(tag + "_norm_bwd", fn, [h_in, da, dh_out], [ln, sh, sc], [(d, F32)], [(1, d)] * 3)


def _ffn_bwd(tag, h_in, dh_out, sv, ln, sh, sc, gt, ga, tg, tu, gb, td, weight):
    s, d = h_in.shape
    nsh, _, ffs, _ = gb.shape
    bm, bn = _pick(s, (1024, 512, 256, 128)), _pick(d, (512, 256, 128))
    bk = _pick(s, (1024, 512, 256, 128))
    do, d_gt = _resid_bwd(tag, dh_out, sv["o"], gt, weight)

    def act_bwd(df, g, u):
        _, vjp = jax.vjp(_swiglu_act, g, u)
        return vjp(df)

    hid = pl.BlockSpec((None, bm, ffs), lambda i, j, kk: (j, i, 0))
    dg, du = _mm_core(
        tag + "_down_dx", (s // bm, nsh, 1), 1,
        [(do, pl.BlockSpec((bm, d), lambda i, j, kk: (i, 0)),
          gb, pl.BlockSpec((None, None, ffs, d), lambda i, j, kk: (j, td, 0, 0)), NT)],
        [((nsh, s, ffs), BF16, hid)] * 2, (bm, ffs), epi=act_bwd, epi_ins=[(sv["g"], hid), (sv["u"], hid)])
    (d_wd,) = _mm_core(
        tag + "_down_dw", (nsh, d // bn, s // bk), s // bk,
        [(sv["f"], pl.BlockSpec((None, bk, ffs), lambda i, j, kk: (i, kk, 0)),
          do, pl.BlockSpec((bk, bn), lambda i, j, kk: (kk, j)), TN)],
        [((nsh, ffs, d), BF16, pl.BlockSpec((None, ffs, bn), lambda i, j, kk: (i, 0, j)))], (ffs, bn))
    kmaj = pl.BlockSpec((None, bm, ffs), lambda i, j, kk: (kk, i, 0))
    wsp = lambda t: pl.BlockSpec((None, None, bn, ffs), functools.partial(lambda i, j, kk, t: (kk, t, j, 0), t=t))
    (da,) = _mm_core(
        tag + "_up_dx", (s // bm, d // bn, nsh), nsh, [(dg, kmaj, ga, wsp(tg), NT), (du, kmaj, ga, wsp(tu), NT)],
        [((s, d), F32, pl.BlockSpec((bm, bn), lambda i, j, kk: (i, j)))], (bm, bn))
    dws = []
    for nm, dh in (("_wg_dw", dg), ("_wu_dw", du)):
        (dw,) = _mm_core(
            tag + nm, (1, nsh, s // bk), s // bk,
            [(sv["a"], pl.BlockSpec((bk, d), lambda i, j, kk: (kk, 0)),
              dh, pl.BlockSpec((None, bk, ffs), lambda i, j, kk: (j, kk, 0)), TN)],
            [((nsh, d, ffs), BF16, pl.BlockSpec((None, d, ffs), lambda i, j, kk: (j, 0, 0)))], (d, ffs))
        dws.append(dw)
    dh_in, d_ln, d_sh, d_sc = _norm_bwd(tag, h_in, da, dh_out, ln, sh, sc)
    return dh_in, dict(wg=dws[0], wu=dws[1], wd=d_wd), dict(ln=d_ln, sh=d_sh, sc=d_sc, gt=d_gt)


def _mixer_fwd(tag, h_in, ln, sh, sc, gt, w, sp):
    d = h_in.shape[1]
    (a,) = _rowwise(tag + "_norm", lambda h, l, s1, s2: ((_rms_mod(h, l, s1, s2),), ()), [h_in], [ln, sh, sc],
                    [(d, BF16)])
    (pq,) = _matmul(tag + "_pq", a, w["wq"], outs=(BF16,))
    (pz,) = _matmul(tag + "_pz", a, w["wz"], outs=(BF16,))
    (pba,) = _matmul(tag + "_pba", a, w["wba"])
    (pda,) = _matmul(tag + "_pda", a, w["wda"], outs=(BF16,))
    (pg,) = _matmul(tag + "_pg", a, w["wg"], outs=(BF16,))
    yc, qkvn, gb = _conv_prep_fwd(tag + "_conv", pq, sp["conv8"], pba, sp["alog"], sp["dtb"])
    o_a, states = _delta_fwd(tag + "_delta", qkvn, gb)
    (o_an,) = _rowwise(tag + "_dnorm", lambda o, z, dn: ((_dn_outnorm(o, z.astype(F32), dn),), ()), [o_a, pz],
                       [sp["dn"]], [(DN_WIDTH, BF16)])
    ops, lses = [], []
    for (_, r) in DA_PATTERNS:
        o_p, lse_p = _da_fwd(f"{tag}_da{r}", pda, r)
        ops.append(o_p)
        lses.append(lse_p)

    def merge(o1, o2, o3, l1, l2, l3):
        mx = jnp.maximum(jnp.maximum(l1, l2), l3)
        e1, e2, e3 = jnp.exp(l1 - mx), jnp.exp(l2 - mx), jnp.exp(l3 - mx)
        tot = e1 + e2 + e3
        ex = _head_expand()
        up = lambda wgt: lax.dot_general(wgt / tot, ex, NN, precision=HI, preferred_element_type=F32)
        return (up(e1) * o1 + up(e2) * o2 + up(e3) * o3, mx + jnp.log(tot)), ()

    o_b, lse_tot = _rowwise(tag + "_merge", merge, ops + lses, [], [(DA_WIDTH, BF16), (LANES, F32)])
    (y_a,) = _matmul(tag + "_wa", o_an, w["w_a"], outs=(BF16,))
    (y_b,) = _matmul(tag + "_wb", o_b, w["w_b"], outs=(BF16,))

    def gate(ga, gbv, ya, yb):
        return _sigmoid(ga.astype(F32)) * ya.astype(F32) + _sigmoid(gbv.astype(F32)) * yb.astype(F32)

    (merged,) = _rowwise(tag + "_gate", lambda *v: ((gate(*v),), ()), [(pg, d, 0), (pg, d, 1), y_a, y_b], [],
                         [(d, BF16)])
    h_out, m = _matmul(tag + "_wo", merged, w["w_o"], outs=(F32, BF16), epi_rows=[h_in], epi_bcast=[gt],
                       epi=lambda acc, h, g: (h + g * acc, acc))
    sv = dict(a=a, pq=pq, pz=pz, pba=pba, pda=pda, pg=pg, yc=yc, qkvn=qkvn, gb=gb, o_a=o_a, states=states,
              o_an=o_an, o_b=o_b, lse=lse_tot, y_a=y_a, y_b=y_b, merged=merged, m=m, gate=gate)
    return h_out, sv


def _mixer_bwd(tag, h_in, dh_out, sv, ln, sh, sc, gt, w, sp):
    d = h_in.shape[1]
    dm, d_gt = _resid_bwd(tag, dh_out, sv["m"], gt, 1.0)
    (d_merged,) = _matmul(tag + "_wo_dx", dm, w["w_o"], tb=True, outs=(BF16,))
    (d_wo,) = _matmul(tag + "_wo_dw", sv["merged"], dm, ta=True, outs=(BF16,))
    gate = sv["gate"]

    def gate_bwd(dmg, ga, gbv, ya, yb):
        _, vjp = jax.vjp(gate, ga.astype(F32), gbv.astype(F32), ya.astype(F32), yb.astype(F32))
        dga, dgb, dya, dyb = vjp(dmg.astype(F32))
        return (jnp.concatenate([dga, dgb], axis=1), dya, dyb), ()

    pg = sv["pg"]
    d_pg, d_ya, d_yb = _rowwise(tag + "_gate_bwd", gate_bwd, [d_merged, (pg, d, 0), (pg, d, 1), sv["y_a"], sv["y_b"]],
                                [], [(2 * d, BF16), (d, BF16), (d, BF16)])
    (d_oan,) = _matmul(tag + "_wa_dx", d_ya, w["w_a"], tb=True)
    (d_wa,) = _matmul(tag + "_wa_dw", sv["o_an"], d_ya, ta=True, outs=(BF16,))
    (d_ob,) = _matmul(tag + "_wb_dx", d_yb, w["w_b"], tb=True, outs=(BF16,))
    (d_wb,) = _matmul(tag + "_wb_dw", sv["o_b"], d_yb, ta=True, outs=(BF16,))

    def dnorm_bwd(doan, o, z, dn):
        _, vjp = jax.vjp(_dn_outnorm, o, z.astype(F32), dn)
        go, gz, gdn = vjp(doan)
        return (go, gz), (gdn,)

    d_oa, d_pz, d_dn = _rowwise(tag + "_dnorm_bwd", dnorm_bwd, [d_oan, sv["o_a"], sv["pz"]], [sp["dn"]],
                                [(DN_WIDTH, F32), (DN_WIDTH, BF16)], [(1, DN_DIM)])
    d_qkvn, d_gb = _delta_bwd(tag + "_delta_bwd", sv["qkvn"], sv["gb"], sv["states"], d_oa)

    def prep_bwd(dq, dgbv, yc, pba, alog, dtb):
        _, vjp = jax.vjp(_dn_prep, yc.astype(F32), pba, alog, dtb)
        gyc, gpba, galog, gdtb = vjp((dq, dgbv))
        return (gyc, gpba), (galog, gdtb)

    d_yc, d_pba, d_alog, d_dtb = _rowwise(tag + "_prep_bwd", prep_bwd, [d_qkvn, d_gb, sv["yc"], sv["pba"]],
                                          [sp["alog"], sp["dtb"]], [(3 * DN_WIDTH, BF16), (LANES, BF16)],
                                          [(1, LANES), (1, LANES)], bm=128)
    d_pq, d_conv = _conv_bwd(tag + "_conv_bwd", d_yc, sv["pq"], sp["conv8"])

    def delta_fn(dob, ob):
        prod = dob.astype(F32) * ob.astype(F32)
        return (lax.dot_general(prod, _head_expand(), NT, precision=HI, preferred_element_type=F32),), ()

    (delta,) = _rowwise(tag + "_da_delta", delta_fn, [d_ob, sv["o_b"]], [], [(LANES, F32)])
    grads = [_da_bwd(f"{tag}_da{r}_bwd", sv["pda"], d_ob, sv["lse"], delta, r) for (_, r) in DA_PATTERNS]

    def sum3(q1, k1, v1, q2, k2, v2, q3, k3, v3):
        return (jnp.concatenate([q1 + q2 + q3, k1 + k2 + k3, v1 + v2 + v3], axis=1),), ()

    (d_pda,) = _rowwise(tag + "_da_sum", sum3, [t for g in grads for t in g], [], [(3 * DA_WIDTH, BF16)])

    a = sv["a"]
    (da,) = _matmul(tag + "_pq_dx", d_pq, w["wq"], tb=True)
    add = lambda acc, prev: (acc + prev,)
    (da,) = _matmul(tag + "_pz_dx", d_pz, w["wz"], tb=True, epi_rows=[da], epi=add)
    (da,) = _matmul(tag + "_pba_dx", d_pba, w["wba"], tb=True, epi_rows=[da], epi=add)
    (da,) = _matmul(tag + "_pda_dx", d_pda, w["wda"], tb=True, epi_rows=[da], epi=add)
    (da,) = _matmul(tag + "_pg_dx", d_pg, w["wg"], tb=True, epi_rows=[da], epi=add)
    (d_wq,) = _matmul(tag + "_pq_dw", a, d_pq, ta=True, outs=(BF16,))
    (d_wz,) = _matmul(tag + "_pz_dw", a, d_pz, ta=True, outs=(BF16,))
    (d_wba,) = _matmul(tag + "_pba_dw", a, d_pba, ta=True, outs=(BF16,))
    (d_wda,) = _matmul(tag + "_pda_dw", a, d_pda, ta=True, outs=(BF16,))
    (d_wg,) = _matmul(tag + "_pg_dw", a, d_pg, ta=True, outs=(BF16,))
    dh_in, d_ln, d_sh, d_sc = _norm_bwd(tag, h_in, da, dh_out, ln, sh, sc)
    wgrads = dict(wq=d_wq, wz=d_wz, wba=d_wba, wda=d_wda, wg=d_wg, w_a=d_wa, w_b=d_wb, w_o=d_wo)
    small = dict(ln=d_ln, sh=d_sh, sc=d_sc, gt=d_gt, dn=d_dn, alog=d_alog, dtb=d_dtb, conv=d_conv)
    return dh_in, wgrads, small


def _loss_head(h, target, fnorm):
    d = h.shape[1]

    def fn(hv, tv, fw):
        def lossf(hh, ww):
            y = hh * lax.rsqrt(jnp.mean(hh * hh, axis=-1, keepdims=True) + NORM_EPS) * ww
            return 0.5 * jnp.sum(jnp.mean(jnp.square(y - tv), axis=-1))

        val, (dh, dw) = jax.value_and_grad(lossf, argnums=(0, 1))(hv, fw)
        return (dh,), (jnp.full((1, LANES), val, F32), dw)

    return _rowwise("loss_head", fn, [h, target], [fnorm], [(d, F32)], [(1, LANES), (1, d)])


def _row(v):
    return v.reshape(1, -1)


def _pad_lanes(v, offset):
    return jnp.pad(v.reshape(1, -1), ((0, 0), (offset, LANES - offset - v.shape[0])))


_UP_SLOTS = dict(ffn1_wg=0, ffn1_wu=1, ffn2_wg=2, ffn2_wu=3)
_DOWN_SLOTS = dict(ffn1_wd=0, ffn2_wd=1)


def _local_step(x2, target, mod, ga, gb, wl, small):
    depth = mod.shape[0]
    d = x2.shape[1]
    h = x2
    saved = []
    mods = []
    up = lambda l, nm: len(_UP_SLOTS) * l + _UP_SLOTS[nm]
    down = lambda l, nm: len(_DOWN_SLOTS) * l + _DOWN_SLOTS[nm]
    for l in range(depth):
        m9 = [_row(mod[l, i * d:(i + 1) * d]) for i in range(N_ADA)]
        sp = dict(conv8=jnp.pad(small["conv_w"][l], ((0, 8 - DN_CONV), (0, 0))),
                  alog=_pad_lanes(small["a_log"][l], DN_HEADS), dtb=_pad_lanes(small["dt_bias"][l], DN_HEADS),
                  dn=_row(small["dn_norm"][l]))
        w = wl[l]
        h0 = h
        h1, sv1 = _ffn_fwd(f"l{l}_ffn1", h0, _row(small["ln_ffn1"][l]), m9[0], m9[1], m9[2], ga, up(l, "ffn1_wg"),
                           up(l, "ffn1_wu"), gb, down(l, "ffn1_wd"), 0.5)
        h2, sv2 = _mixer_fwd(f"l{l}_mix", h1, _row(small["ln_mix"][l]), m9[3], m9[4], m9[5], w, sp)
        h3, sv3 = _ffn_fwd(f"l{l}_ffn2", h2, _row(small["ln_ffn2"][l]), m9[6], m9[7], m9[8], ga, up(l, "ffn2_wg"),
                           up(l, "ffn2_wu"), gb, down(l, "ffn2_wd"), 0.5)
        saved.append((h0, h1, h2, sv1, sv2, sv3, sp))
        mods.append(m9)
        h = h3
    dh, loss_part, d_fnorm = _loss_head(h, target, _row(small["final_norm"]))
    wgrads, sgrads, dmods = [], [], []
    for l in reversed(range(depth)):
        h0, h1, h2, sv1, sv2, sv3, sp = saved[l]
        m9 = mods[l]
        w = wl[l]
        dh, g3, s3 = _ffn_bwd(f"l{l}_ffn2", h2, dh, sv3, _row(small["ln_ffn2"][l]), m9[6], m9[7], m9[8], ga,
                              up(l, "ffn2_wg"), up(l, "ffn2_wu"), gb, down(l, "ffn2_wd"), 0.5)
        dh, g2, s2 = _mixer_bwd(f"l{l}_mix", h1, dh, sv2, _row(small["ln_mix"][l]), m9[3], m9[4], m9[5], w, sp)
        dh, g1, s1 = _ffn_bwd(f"l{l}_ffn1", h0, dh, sv1, _row(small["ln_ffn1"][l]), m9[0], m9[1], m9[2], ga,
                              up(l, "ffn1_wg"), up(l, "ffn1_wu"), gb, down(l, "ffn1_wd"), 0.5)
        wgrads.append(dict(ffn1_wg=g1["wg"], ffn1_wu=g1["wu"], ffn1_wd=g1["wd"], ffn2_wg=g3["wg"], ffn2_wu=g3["wu"],
                           ffn2_wd=g3["wd"], **g2))
        dmods.append(jnp.concatenate([s1["sh"], s1["sc"], s1["gt"], s2["sh"], s2["sc"], s2["gt"],
                                      s3["sh"], s3["sc"], s3["gt"]], axis=1))
        sgrads.append(dict(ln_ffn1=s1["ln"][0], ln_mix=s2["ln"][0], ln_ffn2=s3["ln"][0],
                           a_log=s2["alog"][0, DN_HEADS:2 * DN_HEADS], dt_bias=s2["dtb"][0, DN_HEADS:2 * DN_HEADS],
                           dn_norm=s2["dn"][0], conv_w=s2["conv"][:DN_CONV]))
    wgrads.reverse()
    sgrads.reverse()
    dmods.reverse()
    return loss_part[0, 0], dh, jnp.concatenate(dmods, axis=0), wgrads, sgrads, d_fnorm[0]


def _flip(v, bit):
    return 1 - v if bit else v


def _allgather8(name, x):
    r, c = x.shape

    def body(x_ref, out_ref, send_sems, recv_sems, local_sem):
        mx, my, mc = lax.axis_index("x"), lax.axis_index("y"), lax.axis_index("c")
        me = 4 * mx + 2 * my + mc
        mine = pltpu.make_async_copy(x_ref, out_ref.at[me], local_sem)
        mine.start()
        sends = []
        for k in range(1, 8):
            peer = (_flip(mx, k & 4), _flip(my, k & 2), _flip(mc, k & 1))
            cp = pltpu.make_async_remote_copy(src_ref=x_ref, dst_ref=out_ref.at[me], send_sem=send_sems.at[k - 1],
                                              recv_sem=recv_sems.at[k - 1], device_id=peer, device_id_type=MESH)
            cp.start()
            sends.append(cp)
        for k in range(1, 8):
            peer = (_flip(mx, k & 4), _flip(my, k & 2), _flip(mc, k & 1))
            src = 4 * peer[0] + 2 * peer[1] + peer[2]
            pltpu.make_async_remote_copy(src_ref=x_ref, dst_ref=out_ref.at[src], send_sem=send_sems.at[k - 1],
                                         recv_sem=recv_sems.at[k - 1], device_id=peer, device_id_type=MESH).wait_recv()
        for cp in sends:
            cp.wait_send()
        mine.wait()

    return pl.pallas_call(
        body, name=name, out_shape=jax.ShapeDtypeStruct((8, r, c), x.dtype),
        in_specs=[pl.BlockSpec(memory_space=pltpu.VMEM)], out_specs=pl.BlockSpec(memory_space=pltpu.VMEM),
        scratch_shapes=[pltpu.SemaphoreType.DMA((7,)), pltpu.SemaphoreType.DMA((7,)), pltpu.SemaphoreType.DMA],
        compiler_params=_params(9 * _nbytes((r, c), x.dtype)),
    )(x)


def _chip_peers(mx, my):
    chips = [(1 - mx, my), (mx, 1 - my), (1 - mx, 1 - my)]
    return chips, [2 * cx + cy for (cx, cy) in chips]


_ANY = pl.BlockSpec(memory_space=pl.ANY)


def _half(mc, t):
    return pl.ds(mc * (t // 2), t // 2)


def _gather_groups(name, shards):
    ng = len(shards)

    def body(*refs):
        xs, outs = refs[:ng], refs[ng:2 * ng]
        send_sems, recv_sems = refs[2 * ng:]
        mx, my, mc = lax.axis_index("x"), lax.axis_index("y"), lax.axis_index("c")
        j = 2 * mx + my
        chips, idxs = _chip_peers(mx, my)
        sib = (mx, my, 1 - mc)

        def copy(k, src, dst, to):
            return pltpu.make_async_remote_copy(src_ref=src, dst_ref=dst, send_sem=send_sems.at[k],
                                                recv_sem=recv_sems.at[k], device_id=to, device_id_type=MESH)

        first, passed = [], []
        for g in range(ng):
            mine = _half(mc, shards[g].shape[0])
            for t, chip in enumerate(chips):
                cp = copy(6 * g + t, xs[g].at[mine], outs[g].at[j, mine], (*chip, mc))
                cp.start()
                first.append(cp)
        for g in range(ng):
            mine = _half(mc, shards[g].shape[0])
            for t, chip in enumerate(chips):
                landed = outs[g].at[idxs[t], mine]
                copy(6 * g + t, landed, landed, (*chip, mc)).wait_recv()
                fwd = copy(6 * g + 3 + t, landed, landed, sib)
                fwd.start()
                passed.append(fwd)
        for g in range(ng):
            theirs_half = _half(1 - mc, shards[g].shape[0])
            for t in range(3):
                theirs = outs[g].at[idxs[t], theirs_half]
                copy(6 * g + 3 + t, theirs, theirs, sib).wait_recv()
        for cp in first + passed:
            cp.wait_send()

    outs = pl.pallas_call(
        body, name=name, out_shape=[jax.ShapeDtypeStruct((4,) + x.shape, x.dtype) for x in shards],
        in_specs=[_ANY] * ng, out_specs=[_ANY] * ng,
        scratch_shapes=[pltpu.SemaphoreType.DMA((6 * ng,)), pltpu.SemaphoreType.DMA((6 * ng,))],
    )(*shards)
    chip = 2 * lax.axis_index("x") + lax.axis_index("y")
    return [lax.dynamic_update_slice(o, x[None], (chip,) + (0,) * x.ndim) for o, x in zip(outs, shards)]


def _pair_swap_groups(name, gs):
    ng = len(gs)

    def body(*refs):
        xs, outs = refs[:ng], refs[ng:2 * ng]
        send_sems, recv_sems = refs[2 * ng:]
        mx, my, mc = lax.axis_index("x"), lax.axis_index("y"), lax.axis_index("c")
        cps = []
        for g in range(ng):
            cp = pltpu.make_async_remote_copy(src_ref=xs[g].at[:, _half(1 - mc, gs[g].shape[1])], dst_ref=outs[g],
                                              send_sem=send_sems.at[g], recv_sem=recv_sems.at[g],
                                              device_id=(mx, my, 1 - mc), device_id_type=MESH)
            cp.start()
            cps.append(cp)
        for cp in cps:
            cp.wait()

    return pl.pallas_call(
        body, name=name,
        out_shape=[jax.ShapeDtypeStruct((x.shape[0], x.shape[1] // 2) + x.shape[2:], x.dtype) for x in gs],
        in_specs=[_ANY] * ng, out_specs=[_ANY] * ng,
        scratch_shapes=[pltpu.SemaphoreType.DMA((ng,)), pltpu.SemaphoreType.DMA((ng,))],
    )(*gs)


def _chip_scatter_groups(name, ps):
    ng = len(ps)

    def body(*refs):
        xs, outs = refs[:ng], refs[ng:2 * ng]
        send_sems, recv_sems = refs[2 * ng:]
        mx, my, mc = lax.axis_index("x"), lax.axis_index("y"), lax.axis_index("c")
        j = 2 * mx + my
        chips, idxs = _chip_peers(mx, my)
        sends = []
        for g in range(ng):
            for t, chip in enumerate(chips):
                cp = pltpu.make_async_remote_copy(src_ref=xs[g].at[idxs[t]], dst_ref=outs[g].at[j],
                                                  send_sem=send_sems.at[3 * g + t], recv_sem=recv_sems.at[3 * g + t],
                                                  device_id=(*chip, mc), device_id_type=MESH)
                cp.start()
                sends.append(cp)
        for g in range(ng):
            for t, chip in enumerate(chips):
                pltpu.make_async_remote_copy(src_ref=xs[g].at[idxs[t]], dst_ref=outs[g].at[idxs[t]],
                                             send_sem=send_sems.at[3 * g + t], recv_sem=recv_sems.at[3 * g + t],
                                             device_id=(*chip, mc), device_id_type=MESH).wait_recv()
        for cp in sends:
            cp.wait_send()

    outs = pl.pallas_call(
        body, name=name, out_shape=[jax.ShapeDtypeStruct(x.shape, x.dtype) for x in ps],
        in_specs=[_ANY] * ng, out_specs=[_ANY] * ng,
        scratch_shapes=[pltpu.SemaphoreType.DMA((3 * ng,)), pltpu.SemaphoreType.DMA((3 * ng,))],
    )(*ps)
    chip = 2 * lax.axis_index("x") + lax.axis_index("y")
    return [lax.dynamic_update_slice(o, lax.dynamic_index_in_dim(x, chip, 0, keepdims=True), (chip,) + (0,) * (x.ndim - 1))
            for o, x in zip(outs, ps)]


def _pair_merge_groups(name, fs):
    ng = len(fs)

    def body(*refs):
        xs, outs = refs[:ng], refs[ng:2 * ng]
        send_sems, recv_sems = refs[2 * ng:]
        mx, my, mc = lax.axis_index("x"), lax.axis_index("y"), lax.axis_index("c")
        cps = []
        for g in range(ng):
            mine = _half(mc, 2 * fs[g].shape[0])
            cp = pltpu.make_async_remote_copy(src_ref=xs[g], dst_ref=outs[g].at[mine], send_sem=send_sems.at[g],
                                              recv_sem=recv_sems.at[g], device_id=(mx, my, 1 - mc), device_id_type=MESH)
            cp.start()
            cps.append(cp)
        for g in range(ng):
            theirs = outs[g].at[_half(1 - mc, 2 * fs[g].shape[0])]
            pltpu.make_async_remote_copy(src_ref=xs[g], dst_ref=theirs, send_sem=send_sems.at[g],
                                         recv_sem=recv_sems.at[g], device_id=(mx, my, 1 - mc),
                                         device_id_type=MESH).wait_recv()
        for cp in cps:
            cp.wait_send()

    outs = pl.pallas_call(
        body, name=name, out_shape=[jax.ShapeDtypeStruct((2 * x.shape[0],) + x.shape[1:], x.dtype) for x in fs],
        in_specs=[_ANY] * ng, out_specs=[_ANY] * ng,
        scratch_shapes=[pltpu.SemaphoreType.DMA((ng,)), pltpu.SemaphoreType.DMA((ng,))],
    )(*fs)
    mc = lax.axis_index("c")
    return [lax.dynamic_update_slice(o, x, (mc * x.shape[0],) + (0,) * (x.ndim - 1)) for o, x in zip(outs, fs)]


def _block_rows(r, w, itemsize=4, budget=4 << 20):
    for c in (r, 2048, 1024, 512, 256, 128, 64, 32, 16):
        if c <= r and r % c == 0 and c * w * itemsize <= budget:
            return c
    return r


def _pair_sum(name, g, got, cidx):
    ns, t, r, w = g.shape
    th = t // 2
    bm = _block_rows(r, w)

    def body(c_ref, a_ref, b_ref, o_ref):
        o_ref[...] = (a_ref[...].astype(F32) + b_ref[...].astype(F32)).astype(o_ref.dtype)

    blk = (None, None, bm, w)
    return pl.pallas_call(
        body, name=name,
        grid_spec=pltpu.PrefetchScalarGridSpec(
            num_scalar_prefetch=1, grid=(ns, th, r // bm),
            in_specs=[pl.BlockSpec(blk, lambda s, tt, i, c: (s, c[0] * th + tt, i, 0)),
                      pl.BlockSpec(blk, lambda s, tt, i, c: (s, tt, i, 0))],
            out_specs=pl.BlockSpec(blk, lambda s, tt, i, c: (s, tt, i, 0))),
        out_shape=jax.ShapeDtypeStruct((ns, th, r, w), BF16),
        compiler_params=_params(3 * _nbytes((bm, w), F32)),
    )(cidx, g, got)


def _chip_sum(name, p):
    ns, th, r, w = p.shape
    bm = _block_rows(r, w, budget=2 << 20)

    def body(p_ref, o_ref):
        acc = p_ref[0].astype(F32)
        for s in range(1, ns):
            acc = acc + p_ref[s].astype(F32)
        o_ref[...] = acc

    return pl.pallas_call(
        body, name=name, grid=(th, r // bm),
        in_specs=[pl.BlockSpec((ns, None, bm, w), lambda tt, i: (0, tt, i, 0))],
        out_specs=pl.BlockSpec((None, bm, w), lambda tt, i: (tt, i, 0)),
        out_shape=jax.ShapeDtypeStruct((th, r, w), F32),
        compiler_params=_params(ns * _nbytes((bm, w), BF16) + 2 * _nbytes((bm, w), F32)),
    )(p)


def _sum_leading(name, x):
    n = x.shape[0]

    def body(p_ref, o_ref):
        acc = p_ref[0]
        for s in range(1, n):
            acc = acc + p_ref[s]
        o_ref[...] = acc

    return pl.pallas_call(body, name=name, out_shape=jax.ShapeDtypeStruct(x.shape[1:], F32),
                          compiler_params=_params(2 * _nbytes(x.shape, F32)))(x)


def _reduce_scatter_groups(gs):
    cidx = lax.axis_index("c").astype(jnp.int32).reshape(1)
    got = _pair_swap_groups("rs_pair_swap", gs)
    pair = [_pair_sum(f"rs_pair_sum{i}", g, r_, cidx) for i, (g, r_) in enumerate(zip(gs, got))]
    chips = _chip_scatter_groups("rs_chip_scatter", pair)
    fin = [_chip_sum(f"rs_chip_sum{i}", p) for i, p in enumerate(chips)]
    return _pair_merge_groups("rs_pair_merge", fin)


_GROUPS = ((("ffn1_wg", "ffn1_wu", "ffn2_wg", "ffn2_wu"), 1), (("ffn1_wd", "ffn2_wd"), 0), (("w_a",), 0),
           (("w_o",), 0), (("w_in",), 1), (("w_b",), 1))


def _stack_group(per_layer, names, depth):
    return jnp.stack([per_layer[l][nm] for l in range(depth) for nm in names], axis=0)


def _shard_major(g, ax):
    k, n = g.shape
    if ax == 0:
        return g.reshape(4, k // 4, n)
    return g.reshape(k, 4, n // 4).transpose(1, 0, 2)


def _in_cols(d):
    o1 = 3 * DN_WIDTH
    o2 = o1 + DN_WIDTH
    o3 = o2 + 2 * DN_HEADS
    o4 = o3 + 3 * DA_WIDTH
    return dict(wq=(0, o1), wz=(o1, o2), wba=(o2, o3), wda=(o3, o4), wg=(o4, o4 + 2 * d))


def _mixer_weights(w_in, w_a, w_b, w_o, d):
    w = {k: w_in[:, a:b] for k, (a, b) in _in_cols(d).items()}
    w["wba"] = jnp.pad(w["wba"], ((0, 0), (0, LANES - 2 * DN_HEADS)))
    w["w_a"], w["w_b"], w["w_o"] = w_a, w_b, w_o
    return w


def _w_in_grad(wg):
    return jnp.concatenate([wg["wq"], wg["wz"], wg["wba"][:, :2 * DN_HEADS], wg["wda"], wg["wg"]], axis=1)


def _adam_math(wv, gv, mv, vv):
    mn = ADAM_B1 * mv + (1.0 - ADAM_B1) * gv
    vn = ADAM_B2 * vv + (1.0 - ADAM_B2) * jnp.square(gv)
    m_hat = mn / (1.0 - ADAM_B1 ** ADAM_STEP)
    v_hat = vn / (1.0 - ADAM_B2 ** ADAM_STEP)
    delta = -ADAM_LR * (m_hat / (jnp.sqrt(v_hat) + ADAM_EPS) + ADAM_WD * wv)
    return delta, mn, vn


def _adamw(name, w, g, m, v):
    shape = w.shape
    cols = shape[-1]
    w2, g2, m2, v2 = (t.reshape(-1, cols) for t in (w, g, m, v))
    rows = w2.shape[0]
    bm = _pick(rows, (256, 128, 64, 32, 16, 8)) if rows >= 8 else rows
    delta, mn, vn = _rowwise(name, lambda *t: (_adam_math(*t), ()), [w2, g2, m2, v2], [], [(cols, F32)] * 3, bm=bm)
    return delta.reshape(shape), mn.reshape(shape), vn.reshape(shape)


def _adamw_stacked(name, w, m, v, gstack, stride, slot):
    depth, r, cdim = w.shape
    bm = _block_rows(r, cdim, budget=1 << 20)

    def body(w_ref, g_ref, m_ref, v_ref, go_ref, d_ref, mo_ref, vo_ref):
        gv = g_ref[...]
        go_ref[...] = gv
        d_ref[...], mo_ref[...], vo_ref[...] = _adam_math(w_ref[...], gv, m_ref[...], v_ref[...])

    nat = pl.BlockSpec((None, bm, cdim), lambda l, i: (l, i, 0))
    return pl.pallas_call(
        body, name=name, grid=(depth, r // bm),
        in_specs=[nat, pl.BlockSpec((None, bm, cdim), lambda l, i: (stride * l + slot, i, 0)), nat, nat],
        out_specs=[nat] * 4, out_shape=[jax.ShapeDtypeStruct(w.shape, F32)] * 4,
        compiler_params=_params(8 * _nbytes((bm, cdim), F32)),
    )(w, gstack, m, v)


def kernel(x, c, ada_w, ada_b, ln_ffn1, ln_mix, ln_ffn2, ffn1_wg, ffn1_wu, ffn1_wd, w_in, conv_w, a_log, dt_bias, dn_norm, w_a, w_b, w_o, ffn2_wg, ffn2_wu, ffn2_wd, final_norm, loss_target, m_ada_w, m_ada_b, m_ln_ffn1, m_ln_mix, m_ln_ffn2, m_ffn1_wg, m_ffn1_wu, m_ffn1_wd, m_w_in, m_conv_w, m_a_log, m_dt_bias, m_dn_norm, m_w_a, m_w_b, m_w_o, m_ffn2_wg, m_ffn2_wu, m_ffn2_wd, m_final_norm, v_ada_w, v_ada_b, v_ln_ffn1, v_ln_mix, v_ln_ffn2, v_ffn1_wg, v_ffn1_wu, v_ffn1_wd, v_w_in, v_conv_w, v_a_log, v_dt_bias, v_dn_norm, v_w_a, v_w_b, v_w_o, v_ffn2_wg, v_ffn2_wu, v_ffn2_wd, v_final_norm):
    names = ["ada_w", "ada_b", "ln_ffn1", "ln_mix", "ln_ffn2", "ffn1_wg", "ffn1_wu", "ffn1_wd", "w_in", "conv_w",
             "a_log", "dt_bias", "dn_norm", "w_a", "w_b", "w_o", "ffn2_wg", "ffn2_wu", "ffn2_wd", "final_norm"]
    wts = dict(zip(names, (ada_w, ada_b, ln_ffn1, ln_mix, ln_ffn2, ffn1_wg, ffn1_wu, ffn1_wd, w_in, conv_w, a_log,
                           dt_bias, dn_norm, w_a, w_b, w_o, ffn2_wg, ffn2_wu, ffn2_wd, final_norm)))
    mom = dict(zip(names, (m_ada_w, m_ada_b, m_ln_ffn1, m_ln_mix, m_ln_ffn2, m_ffn1_wg, m_ffn1_wu, m_ffn1_wd, m_w_in,
                           m_conv_w, m_a_log, m_dt_bias, m_dn_norm, m_w_a, m_w_b, m_w_o, m_ffn2_wg, m_ffn2_wu,
                           m_ffn2_wd, m_final_norm)))
    var = dict(zip(names, (v_ada_w, v_ada_b, v_ln_ffn1, v_ln_mix, v_ln_ffn2, v_ffn1_wg, v_ffn1_wu, v_ffn1_wd, v_w_in,
                           v_conv_w, v_a_log, v_dt_bias, v_dn_norm, v_w_a, v_w_b, v_w_o, v_ffn2_wg, v_ffn2_wu,
                           v_ffn2_wd, v_final_norm)))
    _, s, d = x.shape
    depth = ada_w.shape[0]
    mx, my, mc = lax.axis_index("x"), lax.axis_index("y"), lax.axis_index("c")
    chip = 2 * mx + my
    me = 2 * chip + mc
    nshard = ada_w.shape[2]

    cact = _rowwise("c_silu", lambda cv: ((_silu(cv),), ()), [jnp.pad(c, ((0, 7), (0, 0)))], [], [(d, F32)], bm=8)[0]
    c_all = _allgather8("ag_c", cact)[:, 0, :]
    conv_all = _allgather8("ag_conv", jnp.pad(conv_w.reshape(depth * DN_CONV, -1), ((0, 8 - depth * DN_CONV), (0, 0))))
    conv_full = jnp.concatenate([conv_all[2 * j, :depth * DN_CONV] for j in range(4)], axis=1)
    conv_full = conv_full.reshape(depth, DN_CONV, 3 * DN_WIDTH)
    shard_stacks = [_stack_group([{nm: wts[nm][l].astype(BF16) for nm in nms} for l in range(depth)], nms, depth)
                    for nms, _ in _GROUPS]
    ga, gb, g_wa, g_wo, g_win, g_wb = _gather_groups("ag_weights", shard_stacks)
    rows_of = lambda st, l: st[:, l].reshape(-1, st.shape[-1])
    cols_of = lambda st, l: jnp.concatenate([st[j, l] for j in range(4)], axis=1)

    c16 = jnp.pad(c_all, ((0, 8), (0, 0))).astype(BF16)
    parts = []
    for l in range(depth):
        bias = lax.dynamic_slice(ada_b[l], (chip * nshard,), (nshard,)).reshape(1, nshard)
        (mp,) = _matmul(f"ada_fwd{l}", c16, ada_w[l].astype(BF16), epi_bcast=[bias], epi=lambda acc, b: (acc + b,))
        parts.append(mp)
    mod_all = _allgather8("ag_mod", jnp.concatenate(parts, axis=0))
    mod_rows = jnp.concatenate([mod_all[2 * j] for j in range(4)], axis=1)
    mod = jnp.stack([lax.dynamic_index_in_dim(mod_rows, l * 16 + me, axis=0, keepdims=False) for l in range(depth)])

    wl = [_mixer_weights(cols_of(g_win, l), rows_of(g_wa, l), cols_of(g_wb, l), rows_of(g_wo, l), d)
          for l in range(depth)]
    small = dict(conv_w=conv_full, a_log=a_log, dt_bias=dt_bias, dn_norm=dn_norm, ln_ffn1=ln_ffn1, ln_mix=ln_mix,
                 ln_ffn2=ln_ffn2, final_norm=final_norm)
    loss_part, dx, dmod, wgrads, sgrads, d_fnorm = _local_step(x[0], loss_target[0], mod, ga, gb, wl, small)

    dmod_all = _allgather8("ag_dmod", jnp.pad(dmod, ((0, 8 - depth), (0, 0))))
    g_ada_w, g_ada_b = [], []
    for l in range(depth):
        dm_l = dmod_all[:, l, :]
        (gb_l,) = _rowwise(f"ada_b_grad{l}", lambda v: ((), (jnp.sum(v, axis=0, keepdims=True),)), [dm_l], [], [],
                           [(1, N_ADA * d)], bm=8)
        g_ada_b.append(gb_l[0])
        dm_sh = lax.dynamic_slice(dm_l, (0, chip * nshard), (8, nshard))
        (gw_l,) = _matmul(f"ada_w_grad{l}", c16, jnp.pad(dm_sh, ((0, 8), (0, 0))).astype(BF16), ta=True)
        g_ada_w.append(gw_l)
    grads = dict(ada_w=jnp.stack(g_ada_w), ada_b=jnp.stack(g_ada_b))

    smalls = [loss_part.reshape(1), d_fnorm]
    for l in range(depth):
        sg = sgrads[l]
        smalls += [sg["ln_ffn1"], sg["ln_mix"], sg["ln_ffn2"], sg["a_log"], sg["dt_bias"], sg["dn_norm"],
                   sg["conv_w"].reshape(-1)]
    sizes = [t.shape[0] for t in smalls]
    tile = 8 * LANES
    flat = jnp.concatenate([jnp.pad(t, (0, (-t.shape[0]) % tile)).reshape(-1, LANES) for t in smalls], axis=0)
    tot = _sum_leading("small_sum", _allgather8("ag_small", flat))
    offs, acc = [], 0
    for n_ in sizes:
        offs.append(acc)
        acc += -(-n_ // tile) * 8
    take = lambda i: tot[offs[i]:offs[i] + -(-sizes[i] // tile) * 8].reshape(-1)[:sizes[i]]
    loss = take(0)[0]
    grads["final_norm"] = take(1)
    per = 7
    for key_i, key in enumerate(["ln_ffn1", "ln_mix", "ln_ffn2", "a_log", "dt_bias", "dn_norm"]):
        grads[key] = jnp.stack([take(2 + per * l + key_i) for l in range(depth)])
    conv_g = jnp.stack([take(2 + per * l + 6).reshape(DN_CONV, 3 * DN_WIDTH) for l in range(depth)])
    csh = conv_w.shape[2]
    grads["conv_w"] = lax.dynamic_slice(conv_g, (0, 0, chip * csh), (depth, DN_CONV, csh))

    for l in range(depth):
        wgrads[l]["w_in"] = _w_in_grad(wgrads[l])
    ffn_names = _GROUPS[0][0] + _GROUPS[1][0]
    gstacks = []
    for nms, ax in _GROUPS:
        per_layer = [{nm: (wgrads[l][nm] if nm in ffn_names else _shard_major(wgrads[l][nm], ax)) for nm in nms}
                     for l in range(depth)]
        gstacks.append(jnp.stack([per_layer[l][nm] for l in range(depth) for nm in nms], axis=1))
    reduced = _reduce_scatter_groups(gstacks)
    deltas, new_m, new_v = {}, {}, {}
    for (nms, _), red in zip(_GROUPS, reduced):
        for q, nm in enumerate(nms):
            grads[nm], deltas[nm], new_m[nm], new_v[nm] = _adamw_stacked("adamw_" + nm, wts[nm], mom[nm], var[nm], red,
                                                                         len(nms), q)

    for name in names:
        if name in deltas:
            continue
        wv, gv, mv, vv = wts[name], grads[name], mom[name], var[name]
        if wv.ndim == 1:
            wv, gv, mv, vv = (t.reshape(-1, LANES) for t in (wv, gv, mv, vv))
        dl, mn, vn = _adamw("adamw_" + name, wv, gv, mv, vv)
        deltas[name], new_m[name], new_v[name] = (t.reshape(wts[name].shape) for t in (dl, mn, vn))
    return (loss, dx.reshape(1, s, d), *[grads[n_] for n_ in names], *[deltas[n_] for n_ in names],
            *[new_m[n_] for n_ in names], *[new_v[n_] for n_ in names])
```

```python
import functools

import jax
import jax.numpy as jnp
from jax import lax
from jax.experimental import pallas as pl
from jax.experimental.pallas import tpu as pltpu

F32 = jnp.float32
BF16 = jnp.bfloat16
MESH = pl.DeviceIdType.MESH

NORM_EPS = 1e-6
DN_HEADS, DN_DIM, DN_CHUNK, DN_CONV = 8, 128, 64, 4
DN_WIDTH = DN_HEADS * DN_DIM
DA_HEADS, DA_DIM, DA_BLOCK = 12, 64, 128
DA_WIDTH = DA_HEADS * DA_DIM
DA_PATTERNS = ((128, 1), (512, 4), (2048, 16))
ALIBI_MAX_EXP = 8.0
N_ADA = 9
LANES = 128
V7X_VMEM_BYTES = 64 << 20
ADAM_LR, ADAM_B1, ADAM_B2, ADAM_EPS, ADAM_WD, ADAM_STEP = 0.001, 0.9, 0.999, 1e-08, 0.01, 10
NEG = -1e30
HI = lax.Precision.HIGHEST
NN = (((1,), (0,)), ((), ()))
NT = (((1,), (1,)), ((), ()))
TN = (((0,), (0,)), ((), ()))


def _nbytes(shape, dtype):
    n = 1
    for s in shape:
        n *= s
    return n * jnp.dtype(dtype).itemsize


def _params(block_bytes, scratch_bytes=0):
    need = 2 * block_bytes + scratch_bytes
    lim = min(max(need + need // 4 + (4 << 20), 32 << 20), V7X_VMEM_BYTES - (6 << 20))
    return pltpu.CompilerParams(vmem_limit_bytes=int(lim))


def _pick(n, cands):
    for c in cands:
        if c <= n and n % c == 0:
            return c
    return n


def _sigmoid(x):
    return jax.nn.sigmoid(x)


def _silu(x):
    return x * jax.nn.sigmoid(x)


def _softplus(x):
    return jnp.maximum(x, 0.0) + jnp.log(1.0 + jnp.exp(-jnp.abs(x)))


def _rowwise(name, fn, rows, bcast, row_outs, red_outs=(), bm=256):
    rows = [r if isinstance(r, tuple) else (r, r.shape[1], 0) for r in rows]
    s = rows[0][0].shape[0]
    bm = _pick(s, (bm, 128, 64, 32, 16, 8))
    nr, nb, no, nd = len(rows), len(bcast), len(row_outs), len(red_outs)
    in_specs = [pl.BlockSpec((bm, w), functools.partial(lambda i, ci: (i, ci), ci=ci)) for (_, w, ci) in rows]
    in_specs += [pl.BlockSpec(b.shape, lambda i: (0, 0)) for b in bcast]
    out_shape = [jax.ShapeDtypeStruct((s, w), dt) for (w, dt) in row_outs]
    out_shape += [jax.ShapeDtypeStruct((r, w), F32) for (r, w) in red_outs]
    out_specs = [pl.BlockSpec((bm, w), lambda i: (i, 0)) for (w, _) in row_outs]
    out_specs += [pl.BlockSpec((r, w), lambda i: (0, 0)) for (r, w) in red_outs]

    def body(*refs):
        ins = [r[...] for r in refs[:nr + nb]]
        outs = refs[nr + nb:nr + nb + no]
        reds = refs[nr + nb + no:]
        ov, rv = fn(*ins)
        for o, v in zip(outs, ov):
            o[...] = v.astype(o.dtype)
        if nd:
            @pl.when(pl.program_id(0) == 0)
            def _():
                for r in reds:
                    r[...] = jnp.zeros(r.shape, F32)
            for r, v in zip(reds, rv):
                r[...] += v.astype(F32)

    blk = sum(_nbytes((bm, w), a.dtype) for (a, w, _) in rows) + sum(_nbytes(b.shape, b.dtype) for b in bcast)
    blk += sum(_nbytes((bm, w), dt) for (w, dt) in row_outs) + sum(_nbytes(r, F32) for r in red_outs)
    res = pl.pallas_call(
        body, name=name, grid=(s // bm,), in_specs=in_specs, out_specs=out_specs, out_shape=out_shape,
        compiler_params=_params(3 * blk),
    )(*[a for (a, _, _) in rows], *bcast)
    return res


def _matmul(name, a, b, *, ta=False, tb=False, outs=(F32,), epi=None, epi_rows=(), epi_bcast=(),
            bm=None, bn=None, bk=None):
    if ta:
        k, m = a.shape
    else:
        m, k = a.shape
    n = b.shape[0] if tb else b.shape[1]
    assert (b.shape[1] if tb else b.shape[0]) == k, (name, a.shape, b.shape)
    if bm is None:
        bm = _pick(m, (1024, 1408, 768, 512, 384, 256, 128)) if ta else _pick(m, (1024, 512, 256, 128, 64, 32, 16))
    if bn is None:
        bn = _pick(n, (512, 384, 256, 128))
    if bk is None:
        bk = k if k <= 3072 else _pick(k, (2816, 2048, 1024, 512))
        if ta:
            bk = _pick(k, (1024, 512, 256, 128, 64, 32, 16))
    nk = k // bk
    dims = TN if ta else (NT if tb else NN)
    a_spec = pl.BlockSpec((bk, bm), lambda i, j, kk: (kk, i)) if ta else pl.BlockSpec((bm, bk), lambda i, j, kk: (i, kk))
    b_spec = pl.BlockSpec((bn, bk), lambda i, j, kk: (j, kk)) if tb else pl.BlockSpec((bk, bn), lambda i, j, kk: (kk, j))
    in_specs = [a_spec, b_spec]
    in_specs += [pl.BlockSpec((bm, bn), lambda i, j, kk: (i, j)) for _ in epi_rows]
    in_specs += [pl.BlockSpec((1, bn), lambda i, j, kk: (0, j)) for _ in epi_bcast]
    out_shape = [jax.ShapeDtypeStruct((m, n), dt) for dt in outs]
    out_specs = [pl.BlockSpec((bm, bn), lambda i, j, kk: (i, j)) for _ in outs]
    ner, neb, no = len(epi_rows), len(epi_bcast), len(outs)

    def body(*refs):
        a_ref, b_ref = refs[0], refs[1]
        extra = refs[2:2 + ner + neb]
        out_refs = refs[2 + ner + neb:2 + ner + neb + no]
        prod = lax.dot_general(a_ref[...], b_ref[...], dims, preferred_element_type=F32)

        def finish(acc):
            vals = epi(acc, *[r[...] for r in extra]) if epi is not None else (acc,)
            for o, v in zip(out_refs, vals):
                o[...] = v.astype(o.dtype)

        if nk == 1:
            finish(prod)
        else:
            acc_ref = refs[-1]
            kk = pl.program_id(2)

            @pl.when(kk == 0)
            def _():
                acc_ref[...] = prod

            @pl.when(kk > 0)
            def _():
                acc_ref[...] += prod

            @pl.when(kk == nk - 1)
            def _():
                finish(acc_ref[...])

    blk = _nbytes((bm, bk), a.dtype) + _nbytes((bk, bn), b.dtype)
    blk += sum(_nbytes((bm, bn), r.dtype) for r in epi_rows) + sum(_nbytes((bm, bn), dt) for dt in outs)
    scratch = [pltpu.VMEM((bm, bn), F32)] if nk > 1 else []
    res = pl.pallas_call(
        body, name=name, grid=(m // bm, n // bn, nk), in_specs=in_specs, out_specs=out_specs,
        out_shape=out_shape, scratch_shapes=scratch,
        compiler_params=_params(blk, 3 * _nbytes((bm, bn), F32)),
    )(a, b, *epi_rows, *epi_bcast)
    return res


def _mm_core(name, grid, nk, pairs, out_defs, acc_shape, epi=None, epi_ins=()):
    npair, nep, no = len(pairs), len(epi_ins), len(out_defs)

    def body(*refs):
        extra = refs[2 * npair:2 * npair + nep]
        out_refs = refs[2 * npair + nep:2 * npair + nep + no]
        prod = None
        for p in range(npair):
            d = lax.dot_general(refs[2 * p][...], refs[2 * p + 1][...], pairs[p][4], preferred_element_type=F32)
            prod = d if prod is None else prod + d

        def finish(acc):
            vals = epi(acc, *[r[...] for r in extra]) if epi is not None else (acc,)
            for o, v in zip(out_refs, vals):
                o[...] = v.astype(o.dtype)

        if nk == 1:
            finish(prod)
        else:
            acc_ref = refs[-1]
            kk = pl.program_id(2)

            @pl.when(kk == 0)
            def _():
                acc_ref[...] = prod

            @pl.when(kk > 0)
            def _():
                acc_ref[...] += prod

            @pl.when(kk == nk - 1)
            def _():
                finish(acc_ref[...])

    def blk_bytes(spec, dtype):
        return _nbytes([s for s in spec.block_shape if s is not None], dtype)

    blk = sum(blk_bytes(sa, a.dtype) + blk_bytes(sb, b.dtype) for (a, sa, b, sb, _) in pairs)
    blk += sum(blk_bytes(sp, arr.dtype) for (arr, sp) in epi_ins) + sum(blk_bytes(sp, dt) for (_, dt, sp) in out_defs)
    ins, in_specs = [], []
    for (a, sa, b, sb, _) in pairs:
        ins += [a, b]
        in_specs += [sa, sb]
    ins += [arr for (arr, _) in epi_ins]
    in_specs += [sp for (_, sp) in epi_ins]
    return pl.pallas_call(
        body, name=name, grid=grid, in_specs=in_specs, out_specs=[sp for (_, _, sp) in out_defs],
        out_shape=[jax.ShapeDtypeStruct(sh, dt) for (sh, dt, _) in out_defs],
        scratch_shapes=[pltpu.VMEM(acc_shape, F32)] if nk > 1 else [],
        compiler_params=_params(blk, 3 * _nbytes(acc_shape, F32)),
    )(*ins)


def _rms_mod(h, ln, sh, sc):
    n = h * lax.rsqrt(jnp.mean(h * h, axis=-1, keepdims=True) + NORM_EPS) * ln
    return n * (1.0 + sc) + sh


def _swiglu_act(g, u):
    return _silu(g.astype(F32)) * u.astype(F32)


def _dn_prep(yc, pba, alog, dtb):
    act = _silu(yc)
    parts = []
    for idx in range(2 * DN_HEADS):
        seg = act[:, idx * DN_DIM:(idx + 1) * DN_DIM]
        seg = seg * lax.rsqrt(jnp.sum(seg * seg, axis=-1, keepdims=True) + NORM_EPS)
        if idx < DN_HEADS:
            seg = seg * (DN_DIM ** -0.5)
        parts.append(seg)
    parts.append(act[:, 2 * DN_WIDTH:])
    qkvn = jnp.concatenate(parts, axis=1)
    lane = lax.broadcasted_iota(jnp.int32, pba.shape, 1)
    beta = _sigmoid(pba)
    g = -jnp.exp(alog) * _softplus(pba + dtb)
    gb = jnp.where(lane < DN_HEADS, beta, jnp.where(lane < 2 * DN_HEADS, g, 0.0))
    return qkvn, gb


def _dn_outnorm(o_a, z, dn):
    parts = []
    for h in range(DN_HEADS):
        seg = o_a[:, h * DN_DIM:(h + 1) * DN_DIM]
        seg = seg * lax.rsqrt(jnp.mean(seg * seg, axis=-1, keepdims=True) + NORM_EPS) * dn
        parts.append(seg)
    return jnp.concatenate(parts, axis=1) * _silu(z)


def _shift_down(x, halo8, s):
    r = pltpu.roll(x, s, axis=0)
    top = pltpu.roll(halo8, s, axis=0)
    i8 = lax.broadcasted_iota(jnp.int32, top.shape, 0)
    return jnp.concatenate([jnp.where(i8 < s, top, r[0:8]), r[8:]], axis=0)


def _shift_up(x, halo8, s):
    m = x.shape[0]
    r = pltpu.roll(x, m - s, axis=0)
    bot = pltpu.roll(halo8, 8 - s, axis=0)
    i8 = lax.broadcasted_iota(jnp.int32, bot.shape, 0)
    return jnp.concatenate([r[:m - 8], jnp.where(i8 >= 8 - s, bot, r[m - 8:])], axis=0)


def _conv_prep_fwd(name, pq, convw8, pba, alog, dtb, bm=256):
    s, w = pq.shape
    nblk = s // bm
    hb = bm // 16

    def body(x_ref, halo_ref, w_ref, pba_ref, alog_ref, dtb_ref, yc_ref, qkv_ref, gb_ref):
        i = pl.program_id(0)
        x = x_ref[...].astype(F32)
        halo = jnp.where(i > 0, halo_ref[...].astype(F32)[8:16], 0.0)
        cw = w_ref[...]
        y = x * cw[DN_CONV - 1:DN_CONV]
        for sft in range(1, DN_CONV):
            y = y + _shift_down(x, halo, sft) * cw[DN_CONV - 1 - sft:DN_CONV - sft]
        ycb = y.astype(BF16)
        yc_ref[...] = ycb
        qkvn, gb = _dn_prep(ycb.astype(F32), pba_ref[...], alog_ref[...], dtb_ref[...])
        qkv_ref[...] = qkvn.astype(BF16)
        gb_ref[...] = gb

    blk = 3 * _nbytes((bm, w), BF16) + 4 * _nbytes((bm, w), F32)
    return pl.pallas_call(
        body, name=name, grid=(nblk,),
        in_specs=[pl.BlockSpec((bm, w), lambda i: (i, 0)),
                  pl.BlockSpec((16, w), lambda i: (jnp.maximum(i * hb - 1, 0), 0)),
                  pl.BlockSpec(convw8.shape, lambda i: (0, 0)),
                  pl.BlockSpec((bm, LANES), lambda i: (i, 0)),
                  pl.BlockSpec((1, LANES), lambda i: (0, 0)),
                  pl.BlockSpec((1, LANES), lambda i: (0, 0))],
        out_specs=[pl.BlockSpec((bm, w), lambda i: (i, 0)), pl.BlockSpec((bm, w), lambda i: (i, 0)),
                   pl.BlockSpec((bm, LANES), lambda i: (i, 0))],
        out_shape=[jax.ShapeDtypeStruct((s, w), BF16), jax.ShapeDtypeStruct((s, w), BF16),
                   jax.ShapeDtypeStruct((s, LANES), F32)],
        compiler_params=_params(blk),
    )(pq, pq, convw8, pba, alog, dtb)


def _conv_bwd(name, dyc, pq, convw8, bm=256):
    s, w = pq.shape
    nblk = s // bm
    hb = bm // 16

    def body(dy_ref, dyn_ref, x_ref, xh_ref, w_ref, dx_ref, dw_ref):
        i = pl.program_id(0)
        dy = dy_ref[...].astype(F32)
        nxt = jnp.where(i < nblk - 1, dyn_ref[...].astype(F32)[0:8], 0.0)
        x = x_ref[...].astype(F32)
        halo = jnp.where(i > 0, xh_ref[...].astype(F32)[8:16], 0.0)
        cw = w_ref[...]
        dx = dy * cw[DN_CONV - 1:DN_CONV]
        for sft in range(1, DN_CONV):
            dx = dx + _shift_up(dy, nxt, sft) * cw[DN_CONV - 1 - sft:DN_CONV - sft]
        dx_ref[...] = dx.astype(dx_ref.dtype)
        r8 = lax.broadcasted_iota(jnp.int32, (8, w), 0)
        dw = jnp.zeros((8, w), F32)
        for j in range(DN_CONV):
            sft = DN_CONV - 1 - j
            xs = x if sft == 0 else _shift_down(x, halo, sft)
            dw = dw + jnp.where(r8 == j, jnp.sum(dy * xs, axis=0, keepdims=True), 0.0)

        @pl.when(i == 0)
        def _():
            dw_ref[...] = jnp.zeros((8, w), F32)
        dw_ref[...] += dw

    blk = 4 * _nbytes((bm, w), BF16) + 5 * _nbytes((bm, w), F32)
    return pl.pallas_call(
        body, name=name, grid=(nblk,),
        in_specs=[pl.BlockSpec((bm, w), lambda i: (i, 0)),
                  pl.BlockSpec((16, w), lambda i: (jnp.minimum((i + 1) * hb, s // 16 - 1), 0)),
                  pl.BlockSpec((bm, w), lambda i: (i, 0)),
                  pl.BlockSpec((16, w), lambda i: (jnp.maximum(i * hb - 1, 0), 0)),
                  pl.BlockSpec(convw8.shape, lambda i: (0, 0))],
        out_specs=[pl.BlockSpec((bm, w), lambda i: (i, 0)), pl.BlockSpec((8, w), lambda i: (0, 0))],
        out_shape=[jax.ShapeDtypeStruct((s, w), BF16), jax.ShapeDtypeStruct((8, w), F32)],
        compiler_params=_params(blk),
    )(dyc, dyc, pq, pq, convw8)


BNN = (((2,), (1,)), ((0,), (0,)))
BNT = (((2,), (2,)), ((0,), (0,)))
BTN = (((1,), (1,)), ((0,), (0,)))


def _raw_dot_1pass(a, b, dims):
    return lax.dot_general(a.astype(BF16), b.astype(BF16), dims, preferred_element_type=F32)


def _raw_dot_3pass(a, b, dims):
    ah = a.astype(BF16)
    al = (a - ah.astype(F32)).astype(BF16)
    bh = b.astype(BF16)
    bl = (b - bh.astype(F32)).astype(BF16)
    d = lambda x, y: lax.dot_general(x, y, dims, preferred_element_type=F32)
    return d(ah, bh) + (d(ah, bl) + d(al, bh))


def _with_same_precision_vjp(raw):
    @functools.partial(jax.custom_vjp, nondiff_argnums=(2,))
    def dot(a, b, dims):
        return raw(a, b, dims)

    def fwd(a, b, dims):
        return raw(a, b, dims), (a, b)

    def bwd(dims, res, ct):
        a, b = res
        if dims == BNN:
            return raw(ct, b, BNT), raw(a, ct, BTN)
        if dims == BNT:
            return raw(ct, b, BNN), raw(ct, a, BTN)
        assert dims == BTN
        return raw(b, ct, BNT), raw(a, ct, BNN)

    dot.defvjp(fwd, bwd)
    return dot


_dot_1pass_vjp = _with_same_precision_vjp(_raw_dot_1pass)
_dot_3pass_vjp = _with_same_precision_vjp(_raw_dot_3pass)


def _dot_bf16(a, b, dims=BNN):
    return _dot_1pass_vjp(a, b, dims)


def _dot_3pass(a, b, dims=BNN):
    return _dot_3pass_vjp(a, b, dims)


def _neumann_inverse(x):
    h, c, _ = x.shape
    eye = lax.broadcasted_iota(jnp.int32, (h, c, c), 1) == lax.broadcasted_iota(jnp.int32, (h, c, c), 2)
    t = jnp.where(eye, 1.0, 0.0) + x
    p = x
    for _ in range(5):
        p = _raw_dot_3pass(p, p, BNN)
        t = t + _raw_dot_3pass(t, p, BNN)
    return t


@jax.custom_vjp
def _known_inverse(x, t):
    return t


def _known_inverse_fwd(x, t):
    return t, t


def _known_inverse_bwd(t, ct):
    return _raw_dot_3pass(_raw_dot_3pass(t, ct, BTN), t, BNT), jnp.zeros_like(t)


_known_inverse.defvjp(_known_inverse_fwd, _known_inverse_bwd)


def _delta_chunk(q, k, v, gcol, bcol, state, t_known=None):
    h, c, _ = q.shape
    row = lax.broadcasted_iota(jnp.int32, (h, c, c), 1)
    col = lax.broadcasted_iota(jnp.int32, (h, c, c), 2)
    incl, strict, eye = row >= col, row > col, row == col
    g_b = jnp.broadcast_to(gcol, (h, c, c))
    gc_row = jnp.sum(jnp.where(row <= col, g_b, 0.0), axis=1, keepdims=True)
    g_r = jnp.sum(jnp.where(eye, g_b, 0.0), axis=1, keepdims=True)
    gc_col = jnp.sum(jnp.where(incl, jnp.broadcast_to(g_r, (h, c, c)), 0.0), axis=2, keepdims=True)
    decay = jnp.exp(jnp.where(incl, gc_col - gc_row, NEG))
    kb = k * bcol
    vb = v * bcol
    x = -jnp.where(strict, _dot_bf16(kb, k, BNT) * decay, 0.0)
    t = _neumann_inverse(x) if t_known is None else _known_inverse(x, t_known)
    eg = jnp.exp(gc_col)
    u = _dot_3pass(t, vb)
    w = _dot_3pass(t, kb * eg)
    qk = _dot_bf16(q, k, BNT) * decay
    v_new = u - _dot_bf16(w, state)
    o = _dot_bf16(q * eg, state) + _dot_bf16(qk, v_new)
    g_last = jnp.sum(g_r, axis=2, keepdims=True)
    new_state = state * jnp.exp(g_last) + _dot_bf16(k * jnp.exp(g_last - gc_col), v_new, BTN)
    return o, new_state, t


def _lane_col(blk, idx):
    lane = lax.broadcasted_iota(jnp.int32, blk.shape, 1)
    return jnp.sum(jnp.where(lane == idx, blk, 0.0), axis=1, keepdims=True)


def _dn_heads(ref, base):
    return jnp.stack([ref[:, base + h * DN_DIM:base + (h + 1) * DN_DIM] for h in range(DN_HEADS)], axis=0).astype(F32)


def _dn_cols(gbv, base):
    return jnp.stack([_lane_col(gbv, base + h) for h in range(DN_HEADS)], axis=0)


def _delta_fwd(name, qkvn, gb):
    s = qkvn.shape[0]
    n = s // DN_CHUNK
    c = DN_CHUNK

    def body(qkv_ref, gb_ref, o_ref, st_ref, t_ref, state):
        @pl.when(pl.program_id(0) == 0)
        def _():
            state[...] = jnp.zeros(state.shape, F32)

        gbv = gb_ref[...]
        st = state[...]
        st_ref[0] = st
        o, new, t = _delta_chunk(_dn_heads(qkv_ref, 0), _dn_heads(qkv_ref, DN_WIDTH), _dn_heads(qkv_ref, 2 * DN_WIDTH),
                                 _dn_cols(gbv, DN_HEADS), _dn_cols(gbv, 0), st)
        for h in range(DN_HEADS):
            o_ref[:, h * DN_DIM:(h + 1) * DN_DIM] = o[h]
        t_ref[0] = t
        state[...] = new

    blk = _nbytes((c, 3 * DN_WIDTH), BF16) + _nbytes((c, LANES), F32) + _nbytes((c, DN_WIDTH), F32)
    blk += _nbytes((DN_HEADS, DN_DIM, DN_DIM), F32) + _nbytes((DN_HEADS, c, c), F32)
    return pl.pallas_call(
        body, name=name, grid=(n,),
        in_specs=[pl.BlockSpec((c, 3 * DN_WIDTH), lambda i: (i, 0)), pl.BlockSpec((c, LANES), lambda i: (i, 0))],
        out_specs=[pl.BlockSpec((c, DN_WIDTH), lambda i: (i, 0)),
                   pl.BlockSpec((1, DN_HEADS, DN_DIM, DN_DIM), lambda i: (i, 0, 0, 0)),
                   pl.BlockSpec((1, DN_HEADS, c, c), lambda i: (i, 0, 0, 0))],
        out_shape=[jax.ShapeDtypeStruct((s, DN_WIDTH), F32),
                   jax.ShapeDtypeStruct((n, DN_HEADS, DN_DIM, DN_DIM), F32),
                   jax.ShapeDtypeStruct((n, DN_HEADS, c, c), F32)],
        scratch_shapes=[pltpu.VMEM((DN_HEADS, DN_DIM, DN_DIM), F32)],
        compiler_params=_params(blk, 8 << 20),
    )(qkvn, gb)


def _delta_bwd(name, qkvn, gb, states, tinv, d_o):
    s = qkvn.shape[0]
    n = s // DN_CHUNK
    c = DN_CHUNK

    def body(qkv_ref, gb_ref, st_ref, t_ref, do_ref, dqkv_ref, dgb_ref, dstate):
        @pl.when(pl.program_id(0) == 0)
        def _():
            dstate[...] = jnp.zeros(dstate.shape, F32)

        gbv = gb_ref[...]
        lane = lax.broadcasted_iota(jnp.int32, (c, LANES), 1)
        t_known = t_ref[0]
        chunk = lambda *args: _delta_chunk(*args, t_known=t_known)[:2]
        _, vjp = jax.vjp(chunk, _dn_heads(qkv_ref, 0), _dn_heads(qkv_ref, DN_WIDTH),
                         _dn_heads(qkv_ref, 2 * DN_WIDTH), _dn_cols(gbv, DN_HEADS), _dn_cols(gbv, 0), st_ref[0])
        dq, dk, dv, dg, db, dst = vjp((_dn_heads(do_ref, 0), dstate[...]))
        dgb = jnp.zeros((c, LANES), F32)
        for h in range(DN_HEADS):
            dqkv_ref[:, h * DN_DIM:(h + 1) * DN_DIM] = dq[h]
            dqkv_ref[:, DN_WIDTH + h * DN_DIM:DN_WIDTH + (h + 1) * DN_DIM] = dk[h]
            dqkv_ref[:, 2 * DN_WIDTH + h * DN_DIM:2 * DN_WIDTH + (h + 1) * DN_DIM] = dv[h]
            dgb = dgb + jnp.where(lane == h, db[h], 0.0) + jnp.where(lane == DN_HEADS + h, dg[h], 0.0)
        dstate[...] = dst
        dgb_ref[...] = dgb

    rev = lambda i: (n - 1 - i, 0)
    blk = _nbytes((c, 3 * DN_WIDTH), BF16) + 2 * _nbytes((c, LANES), F32) + _nbytes((c, DN_WIDTH), F32)
    blk += _nbytes((DN_HEADS, DN_DIM, DN_DIM), F32) + _nbytes((c, 3 * DN_WIDTH), F32)
    return pl.pallas_call(
        body, name=name, grid=(n,),
        in_specs=[pl.BlockSpec((c, 3 * DN_WIDTH), rev), pl.BlockSpec((c, LANES), rev),
                  pl.BlockSpec((1, DN_HEADS, DN_DIM, DN_DIM), lambda i: (n - 1 - i, 0, 0, 0)),
                  pl.BlockSpec((1, DN_HEADS, c, c), lambda i: (n - 1 - i, 0, 0, 0)),
                  pl.BlockSpec((c, DN_WIDTH), rev)],
        out_specs=[pl.BlockSpec((c, 3 * DN_WIDTH), rev), pl.BlockSpec((c, LANES), rev)],
        out_shape=[jax.ShapeDtypeStruct((s, 3 * DN_WIDTH), F32), jax.ShapeDtypeStruct((s, LANES), F32)],
        scratch_shapes=[pltpu.VMEM((DN_HEADS, DN_DIM, DN_DIM), F32)],
        compiler_params=_params(blk, 16 << 20),
    )(qkvn, gb, states, tinv, d_o)


def _da_scores(q2f, k2, sub, valid, distf, head):
    lane = lax.broadcasted_iota(jnp.int32, q2f.shape, 1)
    hmask = (lane < DA_DIM) if sub == 0 else (lane >= DA_DIM)
    qm = jnp.where(hmask, q2f, 0.0).astype(BF16)
    slope = 2.0 ** (-ALIBI_MAX_EXP * (head + 1) / DA_HEADS)
    sc = lax.dot_general(qm, k2, NT, preferred_element_type=F32) * (DA_DIM ** -0.5)
    return jnp.where(valid, sc - slope * distf, NEG), qm, hmask


def _da_mask(i, r):
    qi = lax.broadcasted_iota(jnp.int32, (DA_BLOCK, 2 * DA_BLOCK), 0)
    ki = lax.broadcasted_iota(jnp.int32, (DA_BLOCK, 2 * DA_BLOCK), 1)
    dist = qi + DA_BLOCK - ki
    valid = (dist >= 0) & (dist <= DA_BLOCK) & ((ki >= DA_BLOCK) | (i > 0))
    return valid, (dist * r).astype(F32)


def _da_fwd(name, pda, r):
    s = pda.shape[0]
    n = s // r
    nb = n // DA_BLOCK
    w = DA_WIDTH
    dav = pda.reshape(n, r * 3 * w)

    def body(q_ref, kc_ref, kp_ref, vc_ref, vp_ref, o_ref, lse_ref):
        i = pl.program_id(1)
        valid, distf = _da_mask(i, r)
        lane = lax.broadcasted_iota(jnp.int32, (DA_BLOCK, LANES), 1)
        lse = jnp.zeros((DA_BLOCK, LANES), F32)
        for hp in range(DA_HEADS // 2):
            sl = slice(hp * LANES, (hp + 1) * LANES)
            q2f = q_ref[:, sl].astype(F32)
            k2 = jnp.concatenate([kp_ref[:, sl], kc_ref[:, sl]], axis=0)
            v2 = jnp.concatenate([vp_ref[:, sl], vc_ref[:, sl]], axis=0)
            o2 = None
            for sub in range(2):
                head = 2 * hp + sub
                sc, _, hmask = _da_scores(q2f, k2, sub, valid, distf, head)
                mx = jnp.max(sc, axis=1, keepdims=True)
                p = jnp.exp(sc - mx)
                l = jnp.sum(p, axis=1, keepdims=True)
                pv = lax.dot_general(p.astype(BF16), v2, NN, preferred_element_type=F32) / l
                o2 = pv if sub == 0 else jnp.where(hmask, pv, o2)
                lse = jnp.where(lane == head, mx + jnp.log(l), lse)
            o_ref[:, sl] = o2.astype(o_ref.dtype)
        lse_ref[...] = lse

    prev = lambda col: (lambda p, i: (jnp.maximum(i - 1, 0), 3 * p + col))
    cur = lambda col: (lambda p, i: (i, 3 * p + col))
    blk = 5 * _nbytes((DA_BLOCK, w), BF16) + _nbytes((DA_BLOCK, w), F32) + _nbytes((DA_BLOCK, LANES), F32)
    o, lse = pl.pallas_call(
        body, name=name, grid=(r, nb),
        in_specs=[pl.BlockSpec((DA_BLOCK, w), cur(0)), pl.BlockSpec((DA_BLOCK, w), cur(1)),
                  pl.BlockSpec((DA_BLOCK, w), prev(1)), pl.BlockSpec((DA_BLOCK, w), cur(2)),
                  pl.BlockSpec((DA_BLOCK, w), prev(2))],
        out_specs=[pl.BlockSpec((DA_BLOCK, w), lambda p, i: (i, p)),
                   pl.BlockSpec((DA_BLOCK, LANES), lambda p, i: (i, p))],
        out_shape=[jax.ShapeDtypeStruct((n, r * w), BF16), jax.ShapeDtypeStruct((n, r * LANES), F32)],
        compiler_params=_params(blk, 8 << 20),
    )(dav, dav, dav, dav, dav)
    return o.reshape(s, w), lse.reshape(s, LANES)


def _da_bwd(name, pda, d_ob, lse_tot, delta, r):
    s = pda.shape[0]
    n = s // r
    nb = n // DA_BLOCK
    w = DA_WIDTH
    dav = pda.reshape(n, r * 3 * w)
    dov = d_ob.reshape(n, r * w)
    lv = lse_tot.reshape(n, r * LANES)
    dlv = delta.reshape(n, r * LANES)

    def body(q_ref, kc_ref, kp_ref, vc_ref, vp_ref, do_ref, l_ref, dl_ref, dq_ref, dk_ref, dv_ref, ck, cv):
        i = pl.program_id(1)

        @pl.when(i == 0)
        def _():
            ck[...] = jnp.zeros(ck.shape, F32)
            cv[...] = jnp.zeros(cv.shape, F32)

        @pl.when(i < nb)
        def _():
            valid, distf = _da_mask(i, r)
            lsev = l_ref[...]
            dlt = dl_ref[...]
            for hp in range(DA_HEADS // 2):
                sl = slice(hp * LANES, (hp + 1) * LANES)
                q2f = q_ref[:, sl].astype(F32)
                k2 = jnp.concatenate([kp_ref[:, sl], kc_ref[:, sl]], axis=0)
                v2 = jnp.concatenate([vp_ref[:, sl], vc_ref[:, sl]], axis=0)
                do2f = do_ref[:, sl].astype(F32)
                dq2 = jnp.zeros((DA_BLOCK, LANES), F32)
                dk2 = jnp.zeros((2 * DA_BLOCK, LANES), F32)
                dv2 = jnp.zeros((2 * DA_BLOCK, LANES), F32)
                for sub in range(2):
                    head = 2 * hp + sub
                    sc, qm, hmask = _da_scores(q2f, k2, sub, valid, distf, head)
                    p = jnp.exp(sc - _lane_col(lsev, head))
                    dom = jnp.where(hmask, do2f, 0.0).astype(BF16)
                    dp = lax.dot_general(dom, v2, NT, preferred_element_type=F32)
                    ds = (p * (dp - _lane_col(dlt, head)) * (DA_DIM ** -0.5)).astype(BF16)
                    dq2 = dq2 + jnp.where(hmask, lax.dot_general(ds, k2, NN, preferred_element_type=F32), 0.0)
                    dk2 = dk2 + lax.dot_general(ds, qm, TN, preferred_element_type=F32)
                    dv2 = dv2 + lax.dot_general(p.astype(BF16), dom, TN, preferred_element_type=F32)
                dq_ref[:, sl] = dq2.astype(dq_ref.dtype)
                dk_ref[:, sl] = (ck[:, sl] + dk2[:DA_BLOCK]).astype(dk_ref.dtype)
                dv_ref[:, sl] = (cv[:, sl] + dv2[:DA_BLOCK]).astype(dv_ref.dtype)
                ck[:, sl] = dk2[DA_BLOCK:]
                cv[:, sl] = dv2[DA_BLOCK:]

        @pl.when(i == nb)
        def _():
            dk_ref[...] = ck[...].astype(dk_ref.dtype)
            dv_ref[...] = cv[...].astype(dv_ref.dtype)

    qrow = lambda i: jnp.minimum(i, nb - 1)
    prev = lambda col: (lambda p, i: (jnp.maximum(qrow(i) - 1, 0), 3 * p + col))
    cur = lambda col: (lambda p, i: (qrow(i), 3 * p + col))
    same = lambda p, i: (qrow(i), p)
    late = lambda p, i: (jnp.maximum(i - 1, 0), p)
    blk = 6 * _nbytes((DA_BLOCK, w), BF16) + 2 * _nbytes((DA_BLOCK, LANES), F32) + 3 * _nbytes((DA_BLOCK, w), F32)
    dq, dk, dv = pl.pallas_call(
        body, name=name, grid=(r, nb + 1),
        in_specs=[pl.BlockSpec((DA_BLOCK, w), cur(0)), pl.BlockSpec((DA_BLOCK, w), cur(1)),
                  pl.BlockSpec((DA_BLOCK, w), prev(1)), pl.BlockSpec((DA_BLOCK, w), cur(2)),
                  pl.BlockSpec((DA_BLOCK, w), prev(2)), pl.BlockSpec((DA_BLOCK, w), same),
                  pl.BlockSpec((DA_BLOCK, LANES), same), pl.BlockSpec((DA_BLOCK, LANES), same)],
        out_specs=[pl.BlockSpec((DA_BLOCK, w), same), pl.BlockSpec((DA_BLOCK, w), late),
                   pl.BlockSpec((DA_BLOCK, w), late)],
        out_shape=[jax.ShapeDtypeStruct((n, r * w), BF16)] * 3,
        scratch_shapes=[pltpu.VMEM((DA_BLOCK, w), F32), pltpu.VMEM((DA_BLOCK, w), F32)],
        compiler_params=_params(blk, 12 << 20),
    )(dav, dav, dav, dav, dav, dov, lv, dlv)
    return dq.reshape(s, w), dk.reshape(s, w), dv.reshape(s, w)


def _head_expand():
    hrow = lax.broadcasted_iota(jnp.int32, (LANES, DA_WIDTH), 0)
    lcol = lax.broadcasted_iota(jnp.int32, (LANES, DA_WIDTH), 1)
    return jnp.where(lcol // DA_DIM == hrow, 1.0, 0.0).astype(F32)


def _ffn_up(name, a, ga, tg, tu):
    s, d = a.shape
    nsh, _, _, ffs = ga.shape
    bm = _pick(s, (1024, 512, 256, 128))

    def body(a_ref, wg_ref, wu_ref, g_ref, u_ref, f_ref):
        av = a_ref[...]
        g = lax.dot_general(av, wg_ref[...], NN, preferred_element_type=F32)
        u = lax.dot_general(av, wu_ref[...], NN, preferred_element_type=F32)
        g_ref[...] = g.astype(BF16)
        u_ref[...] = u.astype(BF16)
        f_ref[...] = (_silu(g) * u).astype(BF16)

    wspec = lambda t: pl.BlockSpec((None, None, d, ffs), lambda i, j: (j, t, 0, 0))
    ospec = pl.BlockSpec((None, bm, ffs), lambda i, j: (j, i, 0))
    blk = _nbytes((bm, d), BF16) + 2 * _nbytes((d, ffs), BF16) + 3 * _nbytes((bm, ffs), BF16)
    return pl.pallas_call(
        body, name=name, grid=(s // bm, nsh),
        in_specs=[pl.BlockSpec((bm, d), lambda i, j: (i, 0)), wspec(tg), wspec(tu)],
        out_specs=[ospec] * 3, out_shape=[jax.ShapeDtypeStruct((nsh, s, ffs), BF16)] * 3,
        compiler_params=_params(blk, 4 * _nbytes((bm, ffs), F32)),
    )(a, ga, ga)


def _ffn_fwd(tag, h_in, ln, sh, sc, gt, ga, tg, tu, gb, td, weight):
    s, d = h_in.shape
    nsh, _, ffs, _ = gb.shape
    (a,) = _rowwise(tag + "_norm", lambda h, l, s1, s2: ((_rms_mod(h, l, s1, s2),), ()), [h_in], [ln, sh, sc],
                    [(d, BF16)])
    g, u, f = _ffn_up(tag + "_up", a, ga, tg, tu)
    bm, bn = _pick(s, (1024, 512, 256, 128)), _pick(d, (512, 256, 128))
    io = pl.BlockSpec((bm, bn), lambda i, j, kk: (i, j))
    h_out, o = _mm_core(
        tag + "_down", (s // bm, d // bn, nsh), nsh,
        [(f, pl.BlockSpec((None, bm, ffs), lambda i, j, kk: (kk, i, 0)),
          gb, pl.BlockSpec((None, None, ffs, bn), lambda i, j, kk: (kk, td, 0, j)), NN)],
        [((s, d), F32, io), ((s, d), BF16, io)], (bm, bn),
        epi=lambda acc, h, gv: (h + weight * gv * acc, acc),
        epi_ins=[(h_in, io), (gt, pl.BlockSpec((1, bn), lambda i, j, kk: (0, j)))])
    return h_out, dict(a=a, g=g, u=u, f=f, o=o)


def _resid_bwd(tag, dh_out, o, gt, weight):
    d = dh_out.shape[1]

    def fn(dh, ov, g):
        return (weight * g * dh,), (jnp.sum(weight * dh * ov.astype(F32), axis=0, keepdims=True),)

    do, d_gt = _rowwise(tag + "_resid_bwd", fn, [dh_out, o], [gt], [(d, BF16)], [(1, d)])
    return do, d_gt


def _norm_bwd(tag, h_in, da, dh_out, ln, sh, sc):
    d = h_in.shape[1]

    def fn(h, dav, dh, l, s1, s2):
        _, vjp = jax.vjp(_rms_mod, h, l, s1, s2)
        gh, gl, gs1, gs2 = vjp(dav)
        return (dh + gh,), (gl, gs1, gs2)

    return _rowwise(tag + "_norm_bwd", fn, [h_in, da, dh_out], [ln, sh, sc], [(d, F32)], [(1, d)] * 3)


def _ffn_bwd(tag, h_in, dh_out, sv, ln, sh, sc, gt, ga, tg, tu, gb, td, weight):
    s, d = h_in.shape
    nsh, _, ffs, _ = gb.shape
    bm, bn = _pick(s, (1024, 512, 256, 128)), _pick(d, (512, 256, 128))
    bk = _pick(s, (1024, 512, 256, 128))
    do, d_gt = _resid_bwd(tag, dh_out, sv["o"], gt, weight)

    def act_bwd(df, g, u):
        _, vjp = jax.vjp(_swiglu_act, g, u)
        return vjp(df)

    hid = pl.BlockSpec((None, bm, ffs), lambda i, j, kk: (j, i, 0))
    dg, du = _mm_core(
        tag + "_down_dx", (s // bm, nsh, 1), 1,
        [(do, pl.BlockSpec((bm, d), lambda i, j, kk: (i, 0)),
          gb, pl.BlockSpec((None, None, ffs, d), lambda i, j, kk: (j, td, 0, 0)), NT)],
        [((nsh, s, ffs), BF16, hid)] * 2, (bm, ffs), epi=act_bwd, epi_ins=[(sv["g"], hid), (sv["u"], hid)])
    (d_wd,) = _mm_core(
        tag + "_down_dw", (nsh, d // bn, s // bk), s // bk,
        [(sv["f"], pl.BlockSpec((None, bk, ffs), lambda i, j, kk: (i, kk, 0)),
          do, pl.BlockSpec((bk, bn), lambda i, j, kk: (kk, j)), TN)],
        [((nsh, ffs, d), BF16, pl.BlockSpec((None, ffs, bn), lambda i, j, kk: (i, 0, j)))], (ffs, bn))
    kmaj = pl.BlockSpec((None, bm, ffs), lambda i, j, kk: (kk, i, 0))
    wsp = lambda t: pl.BlockSpec((None, None, bn, ffs), functools.partial(lambda i, j, kk, t: (kk, t, j, 0), t=t))
    (da,) = _mm_core(
        tag + "_up_dx", (s // bm, d // bn, nsh), nsh, [(dg, kmaj, ga, wsp(tg), NT), (du, kmaj, ga, wsp(tu), NT)],
        [((s, d), F32, pl.BlockSpec((bm, bn), lambda i, j, kk: (i, j)))], (bm, bn))
    dws = []
    for nm, dh in (("_wg_dw", dg), ("_wu_dw", du)):
        (dw,) = _mm_core(
            tag + nm, (1, nsh, s // bk), s // bk,
            [(sv["a"], pl.BlockSpec((bk, d), lambda i, j, kk: (kk, 0)),
              dh, pl.BlockSpec((None, bk, ffs), lambda i, j, kk: (j, kk, 0)), TN)],
            [((nsh, d, ffs), BF16, pl.BlockSpec((None, d, ffs), lambda i, j, kk: (j, 0, 0)))], (d, ffs))
        dws.append(dw)
    dh_in, d_ln, d_sh, d_sc = _norm_bwd(tag, h_in, da, dh_out, ln, sh, sc)
    return dh_in, dict(wg=dws[0], wu=dws[1], wd=d_wd), dict(ln=d_ln, sh=d_sh, sc=d_sc, gt=d_gt)


def _mixer_fwd(tag, h_in, ln, sh, sc, gt, w, sp):
    d = h_in.shape[1]
    (a,) = _rowwise(tag + "_norm", lambda h, l, s1, s2: ((_rms_mod(h, l, s1, s2),), ()), [h_in], [ln, sh, sc],
                    [(d, BF16)])
    (pq,) = _matmul(tag + "_pq", a, w["wq"], outs=(BF16,))
    (pz,) = _matmul(tag + "_pz", a, w["wz"], outs=(BF16,))
    (pba,) = _matmul(tag + "_pba", a, w["wba"])
    (pda,) = _matmul(tag + "_pda", a, w["wda"], outs=(BF16,))
    (pg,) = _matmul(tag + "_pg", a, w["wg"], outs=(BF16,))
    yc, qkvn, gb = _conv_prep_fwd(tag + "_conv", pq, sp["conv8"], pba, sp["alog"], sp["dtb"])
    o_a, states, tinv = _delta_fwd(tag + "_delta", qkvn, gb)
    (o_an,) = _rowwise(tag + "_dnorm", lambda o, z, dn: ((_dn_outnorm(o, z.astype(F32), dn),), ()), [o_a, pz],
                       [sp["dn"]], [(DN_WIDTH, BF16)])
    ops, lses = [], []
    for (_, r) in DA_PATTERNS:
        o_p, lse_p = _da_fwd(f"{tag}_da{r}", pda, r)
        ops.append(o_p)
        lses.append(lse_p)

    def merge(o1, o2, o3, l1, l2, l3):
        mx = jnp.maximum(jnp.maximum(l1, l2), l3)
        e1, e2, e3 = jnp.exp(l1 - mx), jnp.exp(l2 - mx), jnp.exp(l3 - mx)
        tot = e1 + e2 + e3
        ex = _head_expand()
        up = lambda wgt: lax.dot_general(wgt / tot, ex, NN, precision=HI, preferred_element_type=F32)
        return (up(e1) * o1 + up(e2) * o2 + up(e3) * o3, mx + jnp.log(tot)), ()

    o_b, lse_tot = _rowwise(tag + "_merge", merge, ops + lses, [], [(DA_WIDTH, BF16), (LANES, F32)])
    (y_a,) = _matmul(tag + "_wa", o_an, w["w_a"], outs=(BF16,))
    (y_b,) = _matmul(tag + "_wb", o_b, w["w_b"], outs=(BF16,))

    def gate(ga, gbv, ya, yb):
        return _sigmoid(ga.astype(F32)) * ya.astype(F32) + _sigmoid(gbv.astype(F32)) * yb.astype(F32)

    (merged,) = _rowwise(tag + "_gate", lambda *v: ((gate(*v),), ()), [(pg, d, 0), (pg, d, 1), y_a, y_b], [],
                         [(d, BF16)])
    h_out, m = _matmul(tag + "_wo", merged, w["w_o"], outs=(F32, BF16), epi_rows=[h_in], epi_bcast=[gt],
                       epi=lambda acc, h, g: (h + g * acc, acc))
    sv = dict(a=a, pq=pq, pz=pz, pba=pba, pda=pda, pg=pg, yc=yc, qkvn=qkvn, gb=gb, o_a=o_a, states=states, tinv=tinv,
              o_an=o_an, o_b=o_b, lse=lse_tot, y_a=y_a, y_b=y_b, merged=merged, m=m, gate=gate)
    return h_out, sv


def _mixer_bwd(tag, h_in, dh_out, sv, ln, sh, sc, gt, w, sp):
    d = h_in.shape[1]
    dm, d_gt = _resid_bwd(tag, dh_out, sv["m"], gt, 1.0)
    (d_merged,) = _matmul(tag + "_wo_dx", dm, w["w_o"], tb=True, outs=(BF16,))
    (d_wo,) = _matmul(tag + "_wo_dw", sv["merged"], dm, ta=True, outs=(BF16,))
    gate = sv["gate"]

    def gate_bwd(dmg, ga, gbv, ya, yb):
        _, vjp = jax.vjp(gate, ga.astype(F32), gbv.astype(F32), ya.astype(F32), yb.astype(F32))
        dga, dgb, dya, dyb = vjp(dmg.astype(F32))
        return (jnp.concatenate([dga, dgb], axis=1), dya, dyb), ()

    pg = sv["pg"]
    d_pg, d_ya, d_yb = _rowwise(tag + "_gate_bwd", gate_bwd, [d_merged, (pg, d, 0), (pg, d, 1), sv["y_a"], sv["y_b"]],
                                [], [(2 * d, BF16), (d, BF16), (d, BF16)])
    (d_oan,) = _matmul(tag + "_wa_dx", d_ya, w["w_a"], tb=True)
    (d_wa,) = _matmul(tag + "_wa_dw", sv["o_an"], d_ya, ta=True, outs=(BF16,))
    (d_ob,) = _matmul(tag + "_wb_dx", d_yb, w["w_b"], tb=True, outs=(BF16,))
    (d_wb,) = _matmul(tag + "_wb_dw", sv["o_b"], d_yb, ta=True, outs=(BF16,))

    def dnorm_bwd(doan, o, z, dn):
        _, vjp = jax.vjp(_dn_outnorm, o, z.astype(F32), dn)
        go, gz, gdn = vjp(doan)
        return (go, gz), (gdn,)

    d_oa, d_pz, d_dn = _rowwise(tag + "_dnorm_bwd", dnorm_bwd, [d_oan, sv["o_a"], sv["pz"]], [sp["dn"]],
                                [(DN_WIDTH, F32), (DN_WIDTH, BF16)], [(1, DN_DIM)])
    d_qkvn, d_gb = _delta_bwd(tag + "_delta_bwd", sv["qkvn"], sv["gb"], sv["states"], sv["tinv"], d_oa)

    def prep_bwd(dq, dgbv, yc, pba, alog, dtb):
        _, vjp = jax.vjp(_dn_prep, yc.astype(F32), pba, alog, dtb)
        gyc, gpba, galog, gdtb = vjp((dq, dgbv))
        return (gyc, gpba), (galog, gdtb)

    d_yc, d_pba, d_alog, d_dtb = _rowwise(tag + "_prep_bwd", prep_bwd, [d_qkvn, d_gb, sv["yc"], sv["pba"]],
                                          [sp["alog"], sp["dtb"]], [(3 * DN_WIDTH, BF16), (LANES, BF16)],
                                          [(1, LANES), (1, LANES)], bm=128)
    d_pq, d_conv = _conv_bwd(tag + "_conv_bwd", d_yc, sv["pq"], sp["conv8"])

    def delta_fn(dob, ob):
        prod = dob.astype(F32) * ob.astype(F32)
        return (lax.dot_general(prod, _head_expand(), NT, precision=HI, preferred_element_type=F32),), ()

    (delta,) = _rowwise(tag + "_da_delta", delta_fn, [d_ob, sv["o_b"]], [], [(LANES, F32)])
    grads = [_da_bwd(f"{tag}_da{r}_bwd", sv["pda"], d_ob, sv["lse"], delta, r) for (_, r) in DA_PATTERNS]

    def sum3(*parts):
        q1, k1, v1, q2, k2, v2, q3, k3, v3 = (p.astype(F32) for p in parts)
        return (jnp.concatenate([q1 + q2 + q3, k1 + k2 + k3, v1 + v2 + v3], axis=1),), ()

    (d_pda,) = _rowwise(tag + "_da_sum", sum3, [t for g in grads for t in g], [], [(3 * DA_WIDTH, BF16)])

    a = sv["a"]
    (da,) = _matmul(tag + "_pq_dx", d_pq, w["wq"], tb=True)
    add = lambda acc, prev: (acc + prev,)
    (da,) = _matmul(tag + "_pz_dx", d_pz, w["wz"], tb=True, epi_rows=[da], epi=add)
    (da,) = _matmul(tag + "_pba_dx", d_pba, w["wba"], tb=True, epi_rows=[da], epi=add)
    (da,) = _matmul(tag + "_pda_dx", d_pda, w["wda"], tb=True, epi_rows=[da], epi=add)
    (da,) = _matmul(tag + "_pg_dx", d_pg, w["wg"], tb=True, epi_rows=[da], epi=add)
    (d_wq,) = _matmul(tag + "_pq_dw", a, d_pq, ta=True, outs=(BF16,))
    (d_wz,) = _matmul(tag + "_pz_dw", a, d_pz, ta=True, outs=(BF16,))
    (d_wba,) = _matmul(tag + "_pba_dw", a, d_pba, ta=True, outs=(BF16,))
    (d_wda,) = _matmul(tag + "_pda_dw", a, d_pda, ta=True, outs=(BF16,))
    (d_wg,) = _matmul(tag + "_pg_dw", a, d_pg, ta=True, outs=(BF16,))
    dh_in, d_ln, d_sh, d_sc = _norm_bwd(tag, h_in, da, dh_out, ln, sh, sc)
    wgrads = dict(wq=d_wq, wz=d_wz, wba=d_wba, wda=d_wda, wg=d_wg, w_a=d_wa, w_b=d_wb, w_o=d_wo)
    small = dict(ln=d_ln, sh=d_sh, sc=d_sc, gt=d_gt, dn=d_dn, alog=d_alog, dtb=d_dtb, conv=d_conv)
    return dh_in, wgrads, small


def _loss_head(h, target, fnorm):
    d = h.shape[1]

    def fn(hv, tv, fw):
        def lossf(hh, ww):
            y = hh * lax.rsqrt(jnp.mean(hh * hh, axis=-1, keepdims=True) + NORM_EPS) * ww
            return 0.5 * jnp.sum(jnp.mean(jnp.square(y - tv), axis=-1))

        val, (dh, dw) = jax.value_and_grad(lossf, argnums=(0, 1))(hv, fw)
        return (dh,), (jnp.full((1, LANES), val, F32), dw)

    return _rowwise("loss_head", fn, [h, target], [fnorm], [(d, F32)], [(1, LANES), (1, d)])


def _row(v):
    return v.reshape(1, -1)


def _pad_lanes(v, offset):
    return jnp.pad(v.reshape(1, -1), ((0, 0), (offset, LANES - offset - v.shape[0])))


_UP_SLOTS = dict(ffn1_wg=0, ffn1_wu=1, ffn2_wg=2, ffn2_wu=3)
_DOWN_SLOTS = dict(ffn1_wd=0, ffn2_wd=1)


def _local_step(x2, target, mod, ga, gb, wl, small):
    depth = mod.shape[0]
    d = x2.shape[1]
    h = x2
    saved = []
    mods = []
    up = lambda l, nm: len(_UP_SLOTS) * l + _UP_SLOTS[nm]
    down = lambda l, nm: len(_DOWN_SLOTS) * l + _DOWN_SLOTS[nm]
    for l in range(depth):
        m9 = [_row(mod[l, i * d:(i + 1) * d]) for i in range(N_ADA)]
        sp = dict(conv8=jnp.pad(small["conv_w"][l], ((0, 8 - DN_CONV), (0, 0))),
                  alog=_pad_lanes(small["a_log"][l], DN_HEADS), dtb=_pad_lanes(small["dt_bias"][l], DN_HEADS),
                  dn=_row(small["dn_norm"][l]))
        w = wl[l]
        h0 = h
        h1, sv1 = _ffn_fwd(f"l{l}_ffn1", h0, _row(small["ln_ffn1"][l]), m9[0], m9[1], m9[2], ga, up(l, "ffn1_wg"),
                           up(l, "ffn1_wu"), gb, down(l, "ffn1_wd"), 0.5)
        h2, sv2 = _mixer_fwd(f"l{l}_mix", h1, _row(small["ln_mix"][l]), m9[3], m9[4], m9[5], w, sp)
        h3, sv3 = _ffn_fwd(f"l{l}_ffn2", h2, _row(small["ln_ffn2"][l]), m9[6], m9[7], m9[8], ga, up(l, "ffn2_wg"),
                           up(l, "ffn2_wu"), gb, down(l, "ffn2_wd"), 0.5)
        saved.append((h0, h1, h2, sv1, sv2, sv3, sp))
        mods.append(m9)
        h = h3
    dh, loss_part, d_fnorm = _loss_head(h, target, _row(small["final_norm"]))
    wgrads, sgrads, dmods = [], [], []
    for l in reversed(range(depth)):
        h0, h1, h2, sv1, sv2, sv3, sp = saved[l]
        m9 = mods[l]
        w = wl[l]
        dh, g3, s3 = _ffn_bwd(f"l{l}_ffn2", h2, dh, sv3, _row(small["ln_ffn2"][l]), m9[6], m9[7], m9[8], ga,
                              up(l, "ffn2_wg"), up(l, "ffn2_wu"), gb, down(l, "ffn2_wd"), 0.5)
        dh, g2, s2 = _mixer_bwd(f"l{l}_mix", h1, dh, sv2, _row(small["ln_mix"][l]), m9[3], m9[4], m9[5], w, sp)
        dh, g1, s1 = _ffn_bwd(f"l{l}_ffn1", h0, dh, sv1, _row(small["ln_ffn1"][l]), m9[0], m9[1], m9[2], ga,
                              up(l, "ffn1_wg"), up(l, "ffn1_wu"), gb, down(l, "ffn1_wd"), 0.5)
        wgrads.append(dict(ffn1_wg=g1["wg"], ffn1_wu=g1["wu"], ffn1_wd=g1["wd"], ffn2_wg=g3["wg"], ffn2_wu=g3["wu"],
                           ffn2_wd=g3["wd"], **g2))
        dmods.append(jnp.concatenate([s1["sh"], s1["sc"], s1["gt"], s2["sh"], s2["sc"], s2["gt"],
                                      s3["sh"], s3["sc"], s3["gt"]], axis=1))
        sgrads.append(dict(ln_ffn1=s1["ln"][0], ln_mix=s2["ln"][0], ln_ffn2=s3["ln"][0],
                           a_log=s2["alog"][0, DN_HEADS:2 * DN_HEADS], dt_bias=s2["dtb"][0, DN_HEADS:2 * DN_HEADS],
                           dn_norm=s2["dn"][0], conv_w=s2["conv"][:DN_CONV]))
    wgrads.reverse()
    sgrads.reverse()
    dmods.reverse()
    return loss_part[0, 0], dh, jnp.concatenate(dmods, axis=0), wgrads, sgrads, d_fnorm[0]


def _flip(v, bit):
    return 1 - v if bit else v


def _allgather8(name, x):
    r, c = x.shape

    def body(x_ref, out_ref, send_sems, recv_sems, local_sem):
        mx, my, mc = lax.axis_index("x"), lax.axis_index("y"), lax.axis_index("c")
        me = 4 * mx + 2 * my + mc
        mine = pltpu.make_async_copy(x_ref, out_ref.at[me], local_sem)
        mine.start()
        sends = []
        for k in range(1, 8):
            peer = (_flip(mx, k & 4), _flip(my, k & 2), _flip(mc, k & 1))
            cp = pltpu.make_async_remote_copy(src_ref=x_ref, dst_ref=out_ref.at[me], send_sem=send_sems.at[k - 1],
                                              recv_sem=recv_sems.at[k - 1], device_id=peer, device_id_type=MESH)
            cp.start()
            sends.append(cp)
        for k in range(1, 8):
            peer = (_flip(mx, k & 4), _flip(my, k & 2), _flip(mc, k & 1))
            src = 4 * peer[0] + 2 * peer[1] + peer[2]
            pltpu.make_async_remote_copy(src_ref=x_ref, dst_ref=out_ref.at[src], send_sem=send_sems.at[k - 1],
                                         recv_sem=recv_sems.at[k - 1], device_id=peer, device_id_type=MESH).wait_recv()
        for cp in sends:
            cp.wait_send()
        mine.wait()

    return pl.pallas_call(
        body, name=name, out_shape=jax.ShapeDtypeStruct((8, r, c), x.dtype),
        in_specs=[pl.BlockSpec(memory_space=pltpu.VMEM)], out_specs=pl.BlockSpec(memory_space=pltpu.VMEM),
        scratch_shapes=[pltpu.SemaphoreType.DMA((7,)), pltpu.SemaphoreType.DMA((7,)), pltpu.SemaphoreType.DMA],
        compiler_params=_params(9 * _nbytes((r, c), x.dtype)),
    )(x)


def _chip_peers(mx, my):
    chips = [(1 - mx, my), (mx, 1 - my), (1 - mx, 1 - my)]
    return chips, [2 * cx + cy for (cx, cy) in chips]


_ANY = pl.BlockSpec(memory_space=pl.ANY)


def _half(mc, t):
    return pl.ds(mc * (t // 2), t // 2)


def _gather_groups(name, shards):
    ng = len(shards)

    def body(*refs):
        xs, outs = refs[:ng], refs[ng:2 * ng]
        send_sems, recv_sems = refs[2 * ng:]
        mx, my, mc = lax.axis_index("x"), lax.axis_index("y"), lax.axis_index("c")
        j = 2 * mx + my
        chips, idxs = _chip_peers(mx, my)
        sib = (mx, my, 1 - mc)

        def copy(k, src, dst, to):
            return pltpu.make_async_remote_copy(src_ref=src, dst_ref=dst, send_sem=send_sems.at[k],
                                                recv_sem=recv_sems.at[k], device_id=to, device_id_type=MESH)

        first, passed = [], []
        for g in range(ng):
            mine = _half(mc, shards[g].shape[0])
            for t, chip in enumerate(chips):
                cp = copy(6 * g + t, xs[g].at[mine], outs[g].at[j, mine], (*chip, mc))
                cp.start()
                first.append(cp)
        for g in range(ng):
            mine = _half(mc, shards[g].shape[0])
            for t, chip in enumerate(chips):
                landed = outs[g].at[idxs[t], mine]
                copy(6 * g + t, landed, landed, (*chip, mc)).wait_recv()
                fwd = copy(6 * g + 3 + t, landed, landed, sib)
                fwd.start()
                passed.append(fwd)
        for g in range(ng):
            theirs_half = _half(1 - mc, shards[g].shape[0])
            for t in range(3):
                theirs = outs[g].at[idxs[t], theirs_half]
                copy(6 * g + 3 + t, theirs, theirs, sib).wait_recv()
        for cp in first + passed:
            cp.wait_send()

    outs = pl.pallas_call(
        body, name=name, out_shape=[jax.ShapeDtypeStruct((4,) + x.shape, x.dtype) for x in shards],
        in_specs=[_ANY] * ng, out_specs=[_ANY] * ng,
        scratch_shapes=[pltpu.SemaphoreType.DMA((6 * ng,)), pltpu.SemaphoreType.DMA((6 * ng,))],
    )(*shards)
    chip = 2 * lax.axis_index("x") + lax.axis_index("y")
    return [lax.dynamic_update_slice(o, x[None], (chip,) + (0,) * x.ndim) for o, x in zip(outs, shards)]


def _pair_swap_groups(name, gs):
    ng = len(gs)

    def body(*refs):
        xs, outs = refs[:ng], refs[ng:2 * ng]
        send_sems, recv_sems = refs[2 * ng:]
        mx, my, mc = lax.axis_index("x"), lax.axis_index("y"), lax.axis_index("c")
        cps = []
        for g in range(ng):
            cp = pltpu.make_async_remote_copy(src_ref=xs[g].at[:, _half(1 - mc, gs[g].shape[1])], dst_ref=outs[g],
                                              send_sem=send_sems.at[g], recv_sem=recv_sems.at[g],
                                              device_id=(mx, my, 1 - mc), device_id_type=MESH)
            cp.start()
            cps.append(cp)
        for cp in cps:
            cp.wait()

    return pl.pallas_call(
        body, name=name,
        out_shape=[jax.ShapeDtypeStruct((x.shape[0], x.shape[1] // 2) + x.shape[2:], x.dtype) for x in gs],
        in_specs=[_ANY] * ng, out_specs=[_ANY] * ng,
        scratch_shapes=[pltpu.SemaphoreType.DMA((ng,)), pltpu.SemaphoreType.DMA((ng,))],
    )(*gs)


def _chip_scatter_groups(name, ps):
    ng = len(ps)

    def body(*refs):
        xs, outs = refs[:ng], refs[ng:2 * ng]
        send_sems, recv_sems = refs[2 * ng:]
        mx, my, mc = lax.axis_index("x"), lax.axis_index("y"), lax.axis_index("c")
        j = 2 * mx + my
        chips, idxs = _chip_peers(mx, my)
        sends = []
        for g in range(ng):
            for t, chip in enumerate(chips):
                cp = pltpu.make_async_remote_copy(src_ref=xs[g].at[idxs[t]], dst_ref=outs[g].at[j],
                                                  send_sem=send_sems.at[3 * g + t], recv_sem=recv_sems.at[3 * g + t],
                                                  device_id=(*chip, mc), device_id_type=MESH)
                cp.start()
                sends.append(cp)
        for g in range(ng):
            for t, chip in enumerate(chips):
                pltpu.make_async_remote_copy(src_ref=xs[g].at[idxs[t]], dst_ref=outs[g].at[idxs[t]],
                                             send_sem=send_sems.at[3 * g + t], recv_sem=recv_sems.at[3 * g + t],
                                             device_id=(*chip, mc), device_id_type=MESH).wait_recv()
        for cp in sends:
            cp.wait_send()

    outs = pl.pallas_call(
        body, name=name, out_shape=[jax.ShapeDtypeStruct(x.shape, x.dtype) for x in ps],
        in_specs=[_ANY] * ng, out_specs=[_ANY] * ng,
        scratch_shapes=[pltpu.SemaphoreType.DMA((3 * ng,)), pltpu.SemaphoreType.DMA((3 * ng,))],
    )(*ps)
    chip = 2 * lax.axis_index("x") + lax.axis_index("y")
    return [lax.dynamic_update_slice(o, lax.dynamic_index_in_dim(x, chip, 0, keepdims=True), (chip,) + (0,) * (x.ndim - 1))
            for o, x in zip(outs, ps)]


def _pair_merge_groups(name, fs):
    ng = len(fs)

    def body(*refs):
        xs, outs = refs[:ng], refs[ng:2 * ng]
        send_sems, recv_sems = refs[2 * ng:]
        mx, my, mc = lax.axis_index("x"), lax.axis_index("y"), lax.axis_index("c")
        cps = []
        for g in range(ng):
            mine = _half(mc, 2 * fs[g].shape[0])
            cp = pltpu.make_async_remote_copy(src_ref=xs[g], dst_ref=outs[g].at[mine], send_sem=send_sems.at[g],
                                              recv_sem=recv_sems.at[g], device_id=(mx, my, 1 - mc), device_id_type=MESH)
            cp.start()
            cps.append(cp)
        for g in range(ng):
            theirs = outs[g].at[_half(1 - mc, 2 * fs[g].shape[0])]
            pltpu.make_async_remote_copy(src_ref=xs[g], dst_ref=theirs, send_sem=send_sems.at[g],
                                         recv_sem=recv_sems.at[g], device_id=(mx, my, 1 - mc),
                                         device_id_type=MESH).wait_recv()
        for cp in cps:
            cp.wait_send()

    outs = pl.pallas_call(
        body, name=name, out_shape=[jax.ShapeDtypeStruct((2 * x.shape[0],) + x.shape[1:], x.dtype) for x in fs],
        in_specs=[_ANY] * ng, out_specs=[_ANY] * ng,
        scratch_shapes=[pltpu.SemaphoreType.DMA((ng,)), pltpu.SemaphoreType.DMA((ng,))],
    )(*fs)
    mc = lax.axis_index("c")
    return [lax.dynamic_update_slice(o, x, (mc * x.shape[0],) + (0,) * (x.ndim - 1)) for o, x in zip(outs, fs)]


def _block_rows(r, w, itemsize=4, budget=4 << 20):
    for c in (r, 2048, 1024, 512, 256, 128, 64, 32, 16):
        if c <= r and r % c == 0 and c * w * itemsize <= budget:
            return c
    return r


def _pair_sum(name, g, got, cidx):
    ns, t, r, w = g.shape
    th = t // 2
    bm = _block_rows(r, w)

    def body(c_ref, a_ref, b_ref, o_ref):
        o_ref[...] = (a_ref[...].astype(F32) + b_ref[...].astype(F32)).astype(o_ref.dtype)

    blk = (None, None, bm, w)
    return pl.pallas_call(
        body, name=name,
        grid_spec=pltpu.PrefetchScalarGridSpec(
            num_scalar_prefetch=1, grid=(ns, th, r // bm),
            in_specs=[pl.BlockSpec(blk, lambda s, tt, i, c: (s, c[0] * th + tt, i, 0)),
                      pl.BlockSpec(blk, lambda s, tt, i, c: (s, tt, i, 0))],
            out_specs=pl.BlockSpec(blk, lambda s, tt, i, c: (s, tt, i, 0))),
        out_shape=jax.ShapeDtypeStruct((ns, th, r, w), BF16),
        compiler_params=_params(3 * _nbytes((bm, w), F32)),
    )(cidx, g, got)


def _chip_sum(name, p):
    ns, th, r, w = p.shape
    bm = _block_rows(r, w, budget=2 << 20)

    def body(p_ref, o_ref):
        acc = p_ref[0].astype(F32)
        for s in range(1, ns):
            acc = acc + p_ref[s].astype(F32)
        o_ref[...] = acc

    return pl.pallas_call(
        body, name=name, grid=(th, r // bm),
        in_specs=[pl.BlockSpec((ns, None, bm, w), lambda tt, i: (0, tt, i, 0))],
        out_specs=pl.BlockSpec((None, bm, w), lambda tt, i: (tt, i, 0)),
        out_shape=jax.ShapeDtypeStruct((th, r, w), F32),
        compiler_params=_params(ns * _nbytes((bm, w), BF16) + 2 * _nbytes((bm, w), F32)),
    )(p)


def _sum_leading(name, x):
    n = x.shape[0]

    def body(p_ref, o_ref):
        acc = p_ref[0]
        for s in range(1, n):
            acc = acc + p_ref[s]
        o_ref[...] = acc

    return pl.pallas_call(body, name=name, out_shape=jax.ShapeDtypeStruct(x.shape[1:], F32),
                          compiler_params=_params(2 * _nbytes(x.shape, F32)))(x)


def _reduce_scatter_groups(gs):
    cidx = lax.axis_index("c").astype(jnp.int32).reshape(1)
    got = _pair_swap_groups("rs_pair_swap", gs)
    pair = [_pair_sum(f"rs_pair_sum{i}", g, r_, cidx) for i, (g, r_) in enumerate(zip(gs, got))]
    chips = _chip_scatter_groups("rs_chip_scatter", pair)
    fin = [_chip_sum(f"rs_chip_sum{i}", p) for i, p in enumerate(chips)]
    return _pair_merge_groups("rs_pair_merge", fin)


_GROUPS = ((("ffn1_wg", "ffn1_wu", "ffn2_wg", "ffn2_wu"), 1), (("ffn1_wd", "ffn2_wd"), 0), (("w_a",), 0),
           (("w_o",), 0), (("w_in",), 1), (("w_b",), 1))


def _stack_group(per_layer, names, depth):
    return jnp.stack([per_layer[l][nm] for l in range(depth) for nm in names], axis=0)


def _shard_major(g, ax):
    k, n = g.shape
    if ax == 0:
        return g.reshape(4, k // 4, n)
    return g.reshape(k, 4, n // 4).transpose(1, 0, 2)


def _in_cols(d):
    o1 = 3 * DN_WIDTH
    o2 = o1 + DN_WIDTH
    o3 = o2 + 2 * DN_HEADS
    o4 = o3 + 3 * DA_WIDTH
    return dict(wq=(0, o1), wz=(o1, o2), wba=(o2, o3), wda=(o3, o4), wg=(o4, o4 + 2 * d))


def _mixer_weights(w_in, w_a, w_b, w_o, d):
    w = {k: w_in[:, a:b] for k, (a, b) in _in_cols(d).items()}
    w["wba"] = jnp.pad(w["wba"], ((0, 0), (0, LANES - 2 * DN_HEADS)))
    w["w_a"], w["w_b"], w["w_o"] = w_a, w_b, w_o
    return w


def _w_in_grad(wg):
    return jnp.concatenate([wg["wq"], wg["wz"], wg["wba"][:, :2 * DN_HEADS], wg["wda"], wg["wg"]], axis=1)


def _adam_math(wv, gv, mv, vv):
    mn = ADAM_B1 * mv + (1.0 - ADAM_B1) * gv
    vn = ADAM_B2 * vv + (1.0 - ADAM_B2) * jnp.square(gv)
    m_hat = mn / (1.0 - ADAM_B1 ** ADAM_STEP)
    v_hat = vn / (1.0 - ADAM_B2 ** ADAM_STEP)
    delta = -ADAM_LR * (m_hat / (jnp.sqrt(v_hat) + ADAM_EPS) + ADAM_WD * wv)
    return delta, mn, vn


def _adamw(name, w, g, m, v):
    shape = w.shape
    cols = shape[-1]
    w2, g2, m2, v2 = (t.reshape(-1, cols) for t in (w, g, m, v))
    rows = w2.shape[0]
    bm = _pick(rows, (256, 128, 64, 32, 16, 8)) if rows >= 8 else rows
    delta, mn, vn = _rowwise(name, lambda *t: (_adam_math(*t), ()), [w2, g2, m2, v2], [], [(cols, F32)] * 3, bm=bm)
    return delta.reshape(shape), mn.reshape(shape), vn.reshape(shape)


def _adamw_stacked(name, w, m, v, gstack, stride, slot):
    depth, r, cdim = w.shape
    bm = _block_rows(r, cdim, budget=1 << 20)

    def body(w_ref, g_ref, m_ref, v_ref, go_ref, d_ref, mo_ref, vo_ref):
        gv = g_ref[...]
        go_ref[...] = gv
        d_ref[...], mo_ref[...], vo_ref[...] = _adam_math(w_ref[...], gv, m_ref[...], v_ref[...])

    nat = pl.BlockSpec((None, bm, cdim), lambda l, i: (l, i, 0))
    return pl.pallas_call(
        body, name=name, grid=(depth, r // bm),
        in_specs=[nat, pl.BlockSpec((None, bm, cdim), lambda l, i: (stride * l + slot, i, 0)), nat, nat],
        out_specs=[nat] * 4, out_shape=[jax.ShapeDtypeStruct(w.shape, F32)] * 4,
        compiler_params=_params(8 * _nbytes((bm, cdim), F32)),
    )(w, gstack, m, v)


def kernel(x, c, ada_w, ada_b, ln_ffn1, ln_mix, ln_ffn2, ffn1_wg, ffn1_wu, ffn1_wd, w_in, conv_w, a_log, dt_bias, dn_norm, w_a, w_b, w_o, ffn2_wg, ffn2_wu, ffn2_wd, final_norm, loss_target, m_ada_w, m_ada_b, m_ln_ffn1, m_ln_mix, m_ln_ffn2, m_ffn1_wg, m_ffn1_wu, m_ffn1_wd, m_w_in, m_conv_w, m_a_log, m_dt_bias, m_dn_norm, m_w_a, m_w_b, m_w_o, m_ffn2_wg, m_ffn2_wu, m_ffn2_wd, m_final_norm, v_ada_w, v_ada_b, v_ln_ffn1, v_ln_mix, v_ln_ffn2, v_ffn1_wg, v_ffn1_wu, v_ffn1_wd, v_w_in, v_conv_w, v_a_log, v_dt_bias, v_dn_norm, v_w_a, v_w_b, v_w_o, v_ffn2_wg, v_ffn2_wu, v_ffn2_wd, v_final_norm):
    names = ["ada_w", "ada_b", "ln_ffn1", "ln_mix", "ln_ffn2", "ffn1_wg", "ffn1_wu", "ffn1_wd", "w_in", "conv_w",
             "a_log", "dt_bias", "dn_norm", "w_a", "w_b", "w_o", "ffn2_wg", "ffn2_wu", "ffn2_wd", "final_norm"]
    wts = dict(zip(names, (ada_w, ada_b, ln_ffn1, ln_mix, ln_ffn2, ffn1_wg, ffn1_wu, ffn1_wd, w_in, conv_w, a_log,
                           dt_bias, dn_norm, w_a, w_b, w_o, ffn2_wg, ffn2_wu, ffn2_wd, final_norm)))
    mom = dict(zip(names, (m_ada_w, m_ada_b, m_ln_ffn1, m_ln_mix, m_ln_ffn2, m_ffn1_wg, m_ffn1_wu, m_ffn1_wd, m_w_in,
                           m_conv_w, m_a_log, m_dt_bias, m_dn_norm, m_w_a, m_w_b, m_w_o, m_ffn2_wg, m_ffn2_wu,
                           m_ffn2_wd, m_final_norm)))
    var = dict(zip(names, (v_ada_w, v_ada_b, v_ln_ffn1, v_ln_mix, v_ln_ffn2, v_ffn1_wg, v_ffn1_wu, v_ffn1_wd, v_w_in,
                           v_conv_w, v_a_log, v_dt_bias, v_dn_norm, v_w_a, v_w_b, v_w_o, v_ffn2_wg, v_ffn2_wu,
                           v_ffn2_wd, v_final_norm)))
    _, s, d = x.shape
    depth = ada_w.shape[0]
    mx, my, mc = lax.axis_index("x"), lax.axis_index("y"), lax.axis_index("c")
    chip = 2 * mx + my
    me = 2 * chip + mc
    nshard = ada_w.shape[2]

    cact = _rowwise("c_silu", lambda cv: ((_silu(cv),), ()), [jnp.pad(c, ((0, 7), (0, 0)))], [], [(d, F32)], bm=8)[0]
    c_all = _allgather8("ag_c", cact)[:, 0, :]
    conv_all = _allgather8("ag_conv", jnp.pad(conv_w.reshape(depth * DN_CONV, -1), ((0, 8 - depth * DN_CONV), (0, 0))))
    conv_full = jnp.concatenate([conv_all[2 * j, :depth * DN_CONV] for j in range(4)], axis=1)
    conv_full = conv_full.reshape(depth, DN_CONV, 3 * DN_WIDTH)
    shard_stacks = [_stack_group([{nm: wts[nm][l].astype(BF16) for nm in nms} for l in range(depth)], nms, depth)
                    for nms, _ in _GROUPS]
    ga, gb, g_wa, g_wo, g_win, g_wb = _gather_groups("ag_weights", shard_stacks)
    rows_of = lambda st, l: st[:, l].reshape(-1, st.shape[-1])
    cols_of = lambda st, l: jnp.concatenate([st[j, l] for j in range(4)], axis=1)

    c16 = jnp.pad(c_all, ((0, 8), (0, 0))).astype(BF16)
    parts = []
    for l in range(depth):
        bias = lax.dynamic_slice(ada_b[l], (chip * nshard,), (nshard,)).reshape(1, nshard)
        (mp,) = _matmul(f"ada_fwd{l}", c16, ada_w[l].astype(BF16), epi_bcast=[bias], epi=lambda acc, b: (acc + b,))
        parts.append(mp)
    mod_all = _allgather8("ag_mod", jnp.concatenate(parts, axis=0))
    mod_rows = jnp.concatenate([mod_all[2 * j] for j in range(4)], axis=1)
    mod = jnp.stack([lax.dynamic_index_in_dim(mod_rows, l * 16 + me, axis=0, keepdims=False) for l in range(depth)])

    wl = [_mixer_weights(cols_of(g_win, l), rows_of(g_wa, l), cols_of(g_wb, l), rows_of(g_wo, l), d)
          for l in range(depth)]
    small = dict(conv_w=conv_full, a_log=a_log, dt_bias=dt_bias, dn_norm=dn_norm, ln_ffn1=ln_ffn1, ln_mix=ln_mix,
                 ln_ffn2=ln_ffn2, final_norm=final_norm)
    loss_part, dx, dmod, wgrads, sgrads, d_fnorm = _local_step(x[0], loss_target[0], mod, ga, gb, wl, small)

    dmod_all = _allgather8("ag_dmod", jnp.pad(dmod, ((0, 8 - depth), (0, 0))))
    g_ada_w, g_ada_b = [], []
    for l in range(depth):
        dm_l = dmod_all[:, l, :]
        (gb_l,) = _rowwise(f"ada_b_grad{l}", lambda v: ((), (jnp.sum(v, axis=0, keepdims=True),)), [dm_l], [], [],
                           [(1, N_ADA * d)], bm=8)
        g_ada_b.append(gb_l[0])
        dm_sh = lax.dynamic_slice(dm_l, (0, chip * nshard), (8, nshard))
        (gw_l,) = _matmul(f"ada_w_grad{l}", c16, jnp.pad(dm_sh, ((0, 8), (0, 0))).astype(BF16), ta=True)
        g_ada_w.append(gw_l)
    grads = dict(ada_w=jnp.stack(g_ada_w), ada_b=jnp.stack(g_ada_b))

    smalls = [loss_part.reshape(1), d_fnorm]
    for l in range(depth):
        sg = sgrads[l]
        smalls += [sg["ln_ffn1"], sg["ln_mix"], sg["ln_ffn2"], sg["a_log"], sg["dt_bias"], sg["dn_norm"],
                   sg["conv_w"].reshape(-1)]
    sizes = [t.shape[0] for t in smalls]
    tile = 8 * LANES
    flat = jnp.concatenate([jnp.pad(t, (0, (-t.shape[0]) % tile)).reshape(-1, LANES) for t in smalls], axis=0)
    tot = _sum_leading("small_sum", _allgather8("ag_small", flat))
    offs, acc = [], 0
    for n_ in sizes:
        offs.append(acc)
        acc += -(-n_ // tile) * 8
    take = lambda i: tot[offs[i]:offs[i] + -(-sizes[i] // tile) * 8].reshape(-1)[:sizes[i]]
    loss = take(0)[0]
    grads["final_norm"] = take(1)
    per = 7
    for key_i, key in enumerate(["ln_ffn1", "ln_mix", "ln_ffn2", "a_log", "dt_bias", "dn_norm"]):
        grads[key] = jnp.stack([take(2 + per * l + key_i) for l in range(depth)])
    conv_g = jnp.stack([take(2 + per * l + 6).reshape(DN_CONV, 3 * DN_WIDTH) for l in range(depth)])
    csh = conv_w.shape[2]
    grads["conv_w"] = lax.dynamic_slice(conv_g, (0, 0, chip * csh), (depth, DN_CONV, csh))

    for l in range(depth):
        wgrads[l]["w_in"] = _w_in_grad(wgrads[l])
    ffn_names = _GROUPS[0][0] + _GROUPS[1][0]
    gstacks = []
    for nms, ax in _GROUPS:
        per_layer = [{nm: (wgrads[l][nm] if nm in ffn_names else _shard_major(wgrads[l][nm], ax)) for nm in nms}
                     for l in range(depth)]
        gstacks.append(jnp.stack([per_layer[l][nm] for l in range(depth) for nm in nms], axis=1))
    reduced = _reduce_scatter_groups(gstacks)
    deltas, new_m, new_v = {}, {}, {}
    for (nms, _), red in zip(_GROUPS, reduced):
        for q, nm in enumerate(nms):
            grads[nm], deltas[nm], new_m[nm], new_v[nm] = _adamw_stacked("adamw_" + nm, wts[nm], mom[nm], var[nm], red,
                                                                         len(nms), q)

    for name in names:
        if name in deltas:
            continue
        wv, gv, mv, vv = wts[name], grads[name], mom[name], var[name]
        if wv.ndim == 1:
            wv, gv, mv, vv = (t.reshape(-1, LANES) for t in (wv, gv, mv, vv))
        dl, mn, vn = _adamw("adamw_" + name, wv, gv, mv, vv)
        deltas[name], new_m[name], new_v[name] = (t.reshape(wts[name].shape) for t in (dl, mn, vn))
    return (loss, dx.reshape(1, s, d), *[grads[n_] for n_ in names], *[deltas[n_] for n_ in names],
            *[new_m[n_] for n_ in names], *[new_v[n_] for n_ in names])
```

```python
import functools

import jax
import jax.numpy as jnp
from jax import lax
from jax.experimental import pallas as pl
from jax.experimental.pallas import tpu as pltpu

F32 = jnp.float32
BF16 = jnp.bfloat16
MESH = pl.DeviceIdType.MESH

NORM_EPS = 1e-6
DN_HEADS, DN_DIM, DN_CHUNK, DN_CONV = 8, 128, 64, 4
DN_WIDTH = DN_HEADS * DN_DIM
DA_HEADS, DA_DIM, DA_BLOCK = 12, 64, 128
DA_WIDTH = DA_HEADS * DA_DIM
DA_PATTERNS = ((128, 1), (512, 4), (2048, 16))
ALIBI_MAX_EXP = 8.0
N_ADA = 9
LANES = 128
V7X_VMEM_BYTES = 64 << 20
ADAM_LR, ADAM_B1, ADAM_B2, ADAM_EPS, ADAM_WD, ADAM_STEP = 0.001, 0.9, 0.999, 1e-08, 0.01, 10
NEG = -1e30
HI = lax.Precision.HIGHEST
NN = (((1,), (0,)), ((), ()))
NT = (((1,), (1,)), ((), ()))
TN = (((0,), (0,)), ((), ()))


def _nbytes(shape, dtype):
    n = 1
    for s in shape:
        n *= s
    return n * jnp.dtype(dtype).itemsize


def _params(block_bytes, scratch_bytes=0):
    need = 2 * block_bytes + scratch_bytes
    lim = min(max(need + need // 4 + (4 << 20), 32 << 20), V7X_VMEM_BYTES - (6 << 20))
    return pltpu.CompilerParams(vmem_limit_bytes=int(lim))


def _pick(n, cands):
    for c in cands:
        if c <= n and n % c == 0:
            return c
    return n


def _sigmoid(x):
    return jax.nn.sigmoid(x)


def _silu(x):
    return x * jax.nn.sigmoid(x)


def _softplus(x):
    return jnp.maximum(x, 0.0) + jnp.log(1.0 + jnp.exp(-jnp.abs(x)))


def _rowwise(name, fn, rows, bcast, row_outs, red_outs=(), bm=256):
    rows = [r if isinstance(r, tuple) else (r, r.shape[1], 0) for r in rows]
    s = rows[0][0].shape[0]
    bm = _pick(s, (bm, 128, 64, 32, 16, 8))
    nr, nb, no, nd = len(rows), len(bcast), len(row_outs), len(red_outs)
    in_specs = [pl.BlockSpec((bm, w), functools.partial(lambda i, ci: (i, ci), ci=ci)) for (_, w, ci) in rows]
    in_specs += [pl.BlockSpec(b.shape, lambda i: (0, 0)) for b in bcast]
    out_shape = [jax.ShapeDtypeStruct((s, w), dt) for (w, dt) in row_outs]
    out_shape += [jax.ShapeDtypeStruct((r, w), F32) for (r, w) in red_outs]
    out_specs = [pl.BlockSpec((bm, w), lambda i: (i, 0)) for (w, _) in row_outs]
    out_specs += [pl.BlockSpec((r, w), lambda i: (0, 0)) for (r, w) in red_outs]

    def body(*refs):
        ins = [r[...] for r in refs[:nr + nb]]
        outs = refs[nr + nb:nr + nb + no]
        reds = refs[nr + nb + no:]
        ov, rv = fn(*ins)
        for o, v in zip(outs, ov):
            o[...] = v.astype(o.dtype)
        if nd:
            @pl.when(pl.program_id(0) == 0)
            def _():
                for r in reds:
                    r[...] = jnp.zeros(r.shape, F32)
            for r, v in zip(reds, rv):
                r[...] += v.astype(F32)

    blk = sum(_nbytes((bm, w), a.dtype) for (a, w, _) in rows) + sum(_nbytes(b.shape, b.dtype) for b in bcast)
    blk += sum(_nbytes((bm, w), dt) for (w, dt) in row_outs) + sum(_nbytes(r, F32) for r in red_outs)
    res = pl.pallas_call(
        body, name=name, grid=(s // bm,), in_specs=in_specs, out_specs=out_specs, out_shape=out_shape,
        compiler_params=_params(3 * blk),
    )(*[a for (a, _, _) in rows], *bcast)
    return res


def _matmul(name, a, b, *, ta=False, tb=False, outs=(F32,), epi=None, epi_rows=(), epi_bcast=(),
            bm=None, bn=None, bk=None):
    if ta:
        k, m = a.shape
    else:
        m, k = a.shape
    n = b.shape[0] if tb else b.shape[1]
    assert (b.shape[1] if tb else b.shape[0]) == k, (name, a.shape, b.shape)
    if bm is None:
        bm = _pick(m, (1024, 1408, 768, 512, 384, 256, 128)) if ta else _pick(m, (1024, 512, 256, 128, 64, 32, 16))
    if bn is None:
        bn = _pick(n, (512, 384, 256, 128))
    if bk is None:
        bk = k if k <= 3072 else _pick(k, (2816, 2048, 1024, 512))
        if ta:
            bk = _pick(k, (1024, 512, 256, 128, 64, 32, 16))
    nk = k // bk
    dims = TN if ta else (NT if tb else NN)
    a_spec = pl.BlockSpec((bk, bm), lambda i, j, kk: (kk, i)) if ta else pl.BlockSpec((bm, bk), lambda i, j, kk: (i, kk))
    b_spec = pl.BlockSpec((bn, bk), lambda i, j, kk: (j, kk)) if tb else pl.BlockSpec((bk, bn), lambda i, j, kk: (kk, j))
    in_specs = [a_spec, b_spec]
    in_specs += [pl.BlockSpec((bm, bn), lambda i, j, kk: (i, j)) for _ in epi_rows]
    in_specs += [pl.BlockSpec((1, bn), lambda i, j, kk: (0, j)) for _ in epi_bcast]
    out_shape = [jax.ShapeDtypeStruct((m, n), dt) for dt in outs]
    out_specs = [pl.BlockSpec((bm, bn), lambda i, j, kk: (i, j)) for _ in outs]
    ner, neb, no = len(epi_rows), len(epi_bcast), len(outs)

    def body(*refs):
        a_ref, b_ref = refs[0], refs[1]
        extra = refs[2:2 + ner + neb]
        out_refs = refs[2 + ner + neb:2 + ner + neb + no]
        prod = lax.dot_general(a_ref[...], b_ref[...], dims, preferred_element_type=F32)

        def finish(acc):
            vals = epi(acc, *[r[...] for r in extra]) if epi is not None else (acc,)
            for o, v in zip(out_refs, vals):
                o[...] = v.astype(o.dtype)

        if nk == 1:
            finish(prod)
        else:
            acc_ref = refs[-1]
            kk = pl.program_id(2)

            @pl.when(kk == 0)
            def _():
                acc_ref[...] = prod

            @pl.when(kk > 0)
            def _():
                acc_ref[...] += prod

            @pl.when(kk == nk - 1)
            def _():
                finish(acc_ref[...])

    blk = _nbytes((bm, bk), a.dtype) + _nbytes((bk, bn), b.dtype)
    blk += sum(_nbytes((bm, bn), r.dtype) for r in epi_rows) + sum(_nbytes((bm, bn), dt) for dt in outs)
    scratch = [pltpu.VMEM((bm, bn), F32)] if nk > 1 else []
    res = pl.pallas_call(
        body, name=name, grid=(m // bm, n // bn, nk), in_specs=in_specs, out_specs=out_specs,
        out_shape=out_shape, scratch_shapes=scratch,
        compiler_params=_params(blk, 3 * _nbytes((bm, bn), F32)),
    )(a, b, *epi_rows, *epi_bcast)
    return res


def _mm_core(name, grid, nk, pairs, out_defs, acc_shape, epi=None, epi_ins=()):
    npair, nep, no = len(pairs), len(epi_ins), len(out_defs)

    def body(*refs):
        extra = refs[2 * npair:2 * npair + nep]
        out_refs = refs[2 * npair + nep:2 * npair + nep + no]
        prod = None
        for p in range(npair):
            d = lax.dot_general(refs[2 * p][...], refs[2 * p + 1][...], pairs[p][4], preferred_element_type=F32)
            prod = d if prod is None else prod + d

        def finish(acc):
            vals = epi(acc, *[r[...] for r in extra]) if epi is not None else (acc,)
            for o, v in zip(out_refs, vals):
                o[...] = v.astype(o.dtype)

        if nk == 1:
            finish(prod)
        else:
            acc_ref = refs[-1]
            kk = pl.program_id(2)

            @pl.when(kk == 0)
            def _():
                acc_ref[...] = prod

            @pl.when(kk > 0)
            def _():
                acc_ref[...] += prod

            @pl.when(kk == nk - 1)
            def _():
                finish(acc_ref[...])

    def blk_bytes(spec, dtype):
        return _nbytes([s for s in spec.block_shape if s is not None], dtype)

    blk = sum(blk_bytes(sa, a.dtype) + blk_bytes(sb, b.dtype) for (a, sa, b, sb, _) in pairs)
    blk += sum(blk_bytes(sp, arr.dtype) for (arr, sp) in epi_ins) + sum(blk_bytes(sp, dt) for (_, dt, sp) in out_defs)
    ins, in_specs = [], []
    for (a, sa, b, sb, _) in pairs:
        ins += [a, b]
        in_specs += [sa, sb]
    ins += [arr for (arr, _) in epi_ins]
    in_specs += [sp for (_, sp) in epi_ins]
    return pl.pallas_call(
        body, name=name, grid=grid, in_specs=in_specs, out_specs=[sp for (_, _, sp) in out_defs],
        out_shape=[jax.ShapeDtypeStruct(sh, dt) for (sh, dt, _) in out_defs],
        scratch_shapes=[pltpu.VMEM(acc_shape, F32)] if nk > 1 else [],
        compiler_params=_params(blk, 3 * _nbytes(acc_shape, F32)),
    )(*ins)


def _rms_mod(h, ln, sh, sc):
    n = h * lax.rsqrt(jnp.mean(h * h, axis=-1, keepdims=True) + NORM_EPS) * ln
    return n * (1.0 + sc) + sh


def _swiglu_act(g, u):
    return _silu(g.astype(F32)) * u.astype(F32)


def _dn_prep(yc, pba, alog, dtb):
    act = _silu(yc)
    parts = []
    for idx in range(2 * DN_HEADS):
        seg = act[:, idx * DN_DIM:(idx + 1) * DN_DIM]
        seg = seg * lax.rsqrt(jnp.sum(seg * seg, axis=-1, keepdims=True) + NORM_EPS)
        if idx < DN_HEADS:
            seg = seg * (DN_DIM ** -0.5)
        parts.append(seg)
    parts.append(act[:, 2 * DN_WIDTH:])
    qkvn = jnp.concatenate(parts, axis=1)
    lane = lax.broadcasted_iota(jnp.int32, pba.shape, 1)
    beta = _sigmoid(pba)
    g = -jnp.exp(alog) * _softplus(pba + dtb)
    gb = jnp.where(lane < DN_HEADS, beta, jnp.where(lane < 2 * DN_HEADS, g, 0.0))
    return qkvn, gb


def _dn_outnorm(o_a, z, dn):
    parts = []
    for h in range(DN_HEADS):
        seg = o_a[:, h * DN_DIM:(h + 1) * DN_DIM]
        seg = seg * lax.rsqrt(jnp.mean(seg * seg, axis=-1, keepdims=True) + NORM_EPS) * dn
        parts.append(seg)
    return jnp.concatenate(parts, axis=1) * _silu(z)


def _shift_down(x, halo8, s):
    r = pltpu.roll(x, s, axis=0)
    top = pltpu.roll(halo8, s, axis=0)
    i8 = lax.broadcasted_iota(jnp.int32, top.shape, 0)
    return jnp.concatenate([jnp.where(i8 < s, top, r[0:8]), r[8:]], axis=0)


def _shift_up(x, halo8, s):
    m = x.shape[0]
    r = pltpu.roll(x, m - s, axis=0)
    bot = pltpu.roll(halo8, 8 - s, axis=0)
    i8 = lax.broadcasted_iota(jnp.int32, bot.shape, 0)
    return jnp.concatenate([r[:m - 8], jnp.where(i8 >= 8 - s, bot, r[m - 8:])], axis=0)


def _conv_prep_fwd(name, pq, convw8, pba, alog, dtb, bm=256):
    s, w = pq.shape
    nblk = s // bm
    hb = bm // 16

    def body(x_ref, halo_ref, w_ref, pba_ref, alog_ref, dtb_ref, yc_ref, qkv_ref, gb_ref):
        i = pl.program_id(0)
        x = x_ref[...].astype(F32)
        halo = jnp.where(i > 0, halo_ref[...].astype(F32)[8:16], 0.0)
        cw = w_ref[...]
        y = x * cw[DN_CONV - 1:DN_CONV]
        for sft in range(1, DN_CONV):
            y = y + _shift_down(x, halo, sft) * cw[DN_CONV - 1 - sft:DN_CONV - sft]
        ycb = y.astype(BF16)
        yc_ref[...] = ycb
        qkvn, gb = _dn_prep(ycb.astype(F32), pba_ref[...], alog_ref[...], dtb_ref[...])
        qkv_ref[...] = qkvn.astype(BF16)
        gb_ref[...] = gb

    blk = 3 * _nbytes((bm, w), BF16) + 4 * _nbytes((bm, w), F32)
    return pl.pallas_call(
        body, name=name, grid=(nblk,),
        in_specs=[pl.BlockSpec((bm, w), lambda i: (i, 0)),
                  pl.BlockSpec((16, w), lambda i: (jnp.maximum(i * hb - 1, 0), 0)),
                  pl.BlockSpec(convw8.shape, lambda i: (0, 0)),
                  pl.BlockSpec((bm, LANES), lambda i: (i, 0)),
                  pl.BlockSpec((1, LANES), lambda i: (0, 0)),
                  pl.BlockSpec((1, LANES), lambda i: (0, 0))],
        out_specs=[pl.BlockSpec((bm, w), lambda i: (i, 0)), pl.BlockSpec((bm, w), lambda i: (i, 0)),
                   pl.BlockSpec((bm, LANES), lambda i: (i, 0))],
        out_shape=[jax.ShapeDtypeStruct((s, w), BF16), jax.ShapeDtypeStruct((s, w), BF16),
                   jax.ShapeDtypeStruct((s, LANES), F32)],
        compiler_params=_params(blk),
    )(pq, pq, convw8, pba, alog, dtb)


def _conv_bwd(name, dyc, pq, convw8, bm=256):
    s, w = pq.shape
    nblk = s // bm
    hb = bm // 16

    def body(dy_ref, dyn_ref, x_ref, xh_ref, w_ref, dx_ref, dw_ref):
        i = pl.program_id(0)
        dy = dy_ref[...].astype(F32)
        nxt = jnp.where(i < nblk - 1, dyn_ref[...].astype(F32)[0:8], 0.0)
        x = x_ref[...].astype(F32)
        halo = jnp.where(i > 0, xh_ref[...].astype(F32)[8:16], 0.0)
        cw = w_ref[...]
        dx = dy * cw[DN_CONV - 1:DN_CONV]
        for sft in range(1, DN_CONV):
            dx = dx + _shift_up(dy, nxt, sft) * cw[DN_CONV - 1 - sft:DN_CONV - sft]
        dx_ref[...] = dx.astype(dx_ref.dtype)
        r8 = lax.broadcasted_iota(jnp.int32, (8, w), 0)
        dw = jnp.zeros((8, w), F32)
        for j in range(DN_CONV):
            sft = DN_CONV - 1 - j
            xs = x if sft == 0 else _shift_down(x, halo, sft)
            dw = dw + jnp.where(r8 == j, jnp.sum(dy * xs, axis=0, keepdims=True), 0.0)

        @pl.when(i == 0)
        def _():
            dw_ref[...] = jnp.zeros((8, w), F32)
        dw_ref[...] += dw

    blk = 4 * _nbytes((bm, w), BF16) + 5 * _nbytes((bm, w), F32)
    return pl.pallas_call(
        body, name=name, grid=(nblk,),
        in_specs=[pl.BlockSpec((bm, w), lambda i: (i, 0)),
                  pl.BlockSpec((16, w), lambda i: (jnp.minimum((i + 1) * hb, s // 16 - 1), 0)),
                  pl.BlockSpec((bm, w), lambda i: (i, 0)),
                  pl.BlockSpec((16, w), lambda i: (jnp.maximum(i * hb - 1, 0), 0)),
                  pl.BlockSpec(convw8.shape, lambda i: (0, 0))],
        out_specs=[pl.BlockSpec((bm, w), lambda i: (i, 0)), pl.BlockSpec((8, w), lambda i: (0, 0))],
        out_shape=[jax.ShapeDtypeStruct((s, w), BF16), jax.ShapeDtypeStruct((8, w), F32)],
        compiler_params=_params(blk),
    )(dyc, dyc, pq, pq, convw8)


BNN = (((2,), (1,)), ((0,), (0,)))
BNT = (((2,), (2,)), ((0,), (0,)))
BTN = (((1,), (1,)), ((0,), (0,)))


def _raw_dot_1pass(a, b, dims):
    return lax.dot_general(a.astype(BF16), b.astype(BF16), dims, preferred_element_type=F32)


def _raw_dot_3pass(a, b, dims):
    ah = a.astype(BF16)
    al = (a - ah.astype(F32)).astype(BF16)
    bh = b.astype(BF16)
    bl = (b - bh.astype(F32)).astype(BF16)
    d = lambda x, y: lax.dot_general(x, y, dims, preferred_element_type=F32)
    return d(ah, bh) + (d(ah, bl) + d(al, bh))


def _with_same_precision_vjp(raw):
    @functools.partial(jax.custom_vjp, nondiff_argnums=(2,))
    def dot(a, b, dims):
        return raw(a, b, dims)

    def fwd(a, b, dims):
        return raw(a, b, dims), (a, b)

    def bwd(dims, res, ct):
        a, b = res
        if dims == BNN:
            return raw(ct, b, BNT), raw(a, ct, BTN)
        if dims == BNT:
            return raw(ct, b, BNN), raw(ct, a, BTN)
        assert dims == BTN
        return raw(b, ct, BNT), raw(a, ct, BNN)

    dot.defvjp(fwd, bwd)
    return dot


_dot_1pass_vjp = _with_same_precision_vjp(_raw_dot_1pass)
_dot_3pass_vjp = _with_same_precision_vjp(_raw_dot_3pass)


def _dot_bf16(a, b, dims=BNN):
    return _dot_1pass_vjp(a, b, dims)


def _dot_3pass(a, b, dims=BNN):
    return _dot_3pass_vjp(a, b, dims)


def _neumann_inverse(x):
    h, c, _ = x.shape
    eye = lax.broadcasted_iota(jnp.int32, (h, c, c), 1) == lax.broadcasted_iota(jnp.int32, (h, c, c), 2)
    t = jnp.where(eye, 1.0, 0.0) + x
    p = x
    for _ in range(5):
        p = _raw_dot_3pass(p, p, BNN)
        t = t + _raw_dot_3pass(t, p, BNN)
    return t


@jax.custom_vjp
def _known_inverse(x, t):
    return t


def _known_inverse_fwd(x, t):
    return t, t


def _known_inverse_bwd(t, ct):
    return _raw_dot_3pass(_raw_dot_3pass(t, ct, BTN), t, BNT), jnp.zeros_like(t)


_known_inverse.defvjp(_known_inverse_fwd, _known_inverse_bwd)


def _delta_chunk(q, k, v, gcol, bcol, state, t_known=None):
    h, c, _ = q.shape
    row = lax.broadcasted_iota(jnp.int32, (h, c, c), 1)
    col = lax.broadcasted_iota(jnp.int32, (h, c, c), 2)
    incl, strict, eye = row >= col, row > col, row == col
    g_b = jnp.broadcast_to(gcol, (h, c, c))
    gc_row = jnp.sum(jnp.where(row <= col, g_b, 0.0), axis=1, keepdims=True)
    g_r = jnp.sum(jnp.where(eye, g_b, 0.0), axis=1, keepdims=True)
    gc_col = jnp.sum(jnp.where(incl, jnp.broadcast_to(g_r, (h, c, c)), 0.0), axis=2, keepdims=True)
    decay = jnp.exp(jnp.where(incl, gc_col - gc_row, NEG))
    kb = k * bcol
    vb = v * bcol
    x = -jnp.where(strict, _dot_bf16(kb, k, BNT) * decay, 0.0)
    t = _neumann_inverse(x) if t_known is None else _known_inverse(x, t_known)
    eg = jnp.exp(gc_col)
    u = _dot_3pass(t, vb)
    w = _dot_3pass(t, kb * eg)
    qk = _dot_bf16(q, k, BNT) * decay
    v_new = u - _dot_bf16(w, state)
    o = _dot_bf16(q * eg, state) + _dot_bf16(qk, v_new)
    g_last = jnp.sum(g_r, axis=2, keepdims=True)
    new_state = state * jnp.exp(g_last) + _dot_bf16(k * jnp.exp(g_last - gc_col), v_new, BTN)
    return o, new_state, t


def _lane_col(blk, idx):
    lane = lax.broadcasted_iota(jnp.int32, blk.shape, 1)
    return jnp.sum(jnp.where(lane == idx, blk, 0.0), axis=1, keepdims=True)


def _dn_heads(ref, base):
    return jnp.stack([ref[:, base + h * DN_DIM:base + (h + 1) * DN_DIM] for h in range(DN_HEADS)], axis=0).astype(F32)


def _dn_cols(gbv, base):
    return jnp.stack([_lane_col(gbv, base + h) for h in range(DN_HEADS)], axis=0)


def _delta_fwd(name, qkvn, gb):
    s = qkvn.shape[0]
    n = s // DN_CHUNK
    c = DN_CHUNK

    def body(qkv_ref, gb_ref, o_ref, st_ref, t_ref, state):
        @pl.when(pl.program_id(0) == 0)
        def _():
            state[...] = jnp.zeros(state.shape, F32)

        gbv = gb_ref[...]
        st = state[...]
        st_ref[0] = st
        o, new, t = _delta_chunk(_dn_heads(qkv_ref, 0), _dn_heads(qkv_ref, DN_WIDTH), _dn_heads(qkv_ref, 2 * DN_WIDTH),
                                 _dn_cols(gbv, DN_HEADS), _dn_cols(gbv, 0), st)
        for h in range(DN_HEADS):
            o_ref[:, h * DN_DIM:(h + 1) * DN_DIM] = o[h]
        t_ref[0] = t
        state[...] = new

    blk = _nbytes((c, 3 * DN_WIDTH), BF16) + _nbytes((c, LANES), F32) + _nbytes((c, DN_WIDTH), F32)
    blk += _nbytes((DN_HEADS, DN_DIM, DN_DIM), F32) + _nbytes((DN_HEADS, c, c), F32)
    return pl.pallas_call(
        body, name=name, grid=(n,),
        in_specs=[pl.BlockSpec((c, 3 * DN_WIDTH), lambda i: (i, 0)), pl.BlockSpec((c, LANES), lambda i: (i, 0))],
        out_specs=[pl.BlockSpec((c, DN_WIDTH), lambda i: (i, 0)),
                   pl.BlockSpec((1, DN_HEADS, DN_DIM, DN_DIM), lambda i: (i, 0, 0, 0)),
                   pl.BlockSpec((1, DN_HEADS, c, c), lambda i: (i, 0, 0, 0))],
        out_shape=[jax.ShapeDtypeStruct((s, DN_WIDTH), F32),
                   jax.ShapeDtypeStruct((n, DN_HEADS, DN_DIM, DN_DIM), F32),
                   jax.ShapeDtypeStruct((n, DN_HEADS, c, c), F32)],
        scratch_shapes=[pltpu.VMEM((DN_HEADS, DN_DIM, DN_DIM), F32)],
        compiler_params=_params(blk, 8 << 20),
    )(qkvn, gb)


def _delta_bwd(name, qkvn, gb, states, tinv, d_o):
    s = qkvn.shape[0]
    n = s // DN_CHUNK
    c = DN_CHUNK

    def body(qkv_ref, gb_ref, st_ref, t_ref, do_ref, dqkv_ref, dgb_ref, dstate):
        @pl.when(pl.program_id(0) == 0)
        def _():
            dstate[...] = jnp.zeros(dstate.shape, F32)

        gbv = gb_ref[...]
        lane = lax.broadcasted_iota(jnp.int32, (c, LANES), 1)
        t_known = t_ref[0]
        chunk = lambda *args: _delta_chunk(*args, t_known=t_known)[:2]
        _, vjp = jax.vjp(chunk, _dn_heads(qkv_ref, 0), _dn_heads(qkv_ref, DN_WIDTH),
                         _dn_heads(qkv_ref, 2 * DN_WIDTH), _dn_cols(gbv, DN_HEADS), _dn_cols(gbv, 0), st_ref[0])
        dq, dk, dv, dg, db, dst = vjp((_dn_heads(do_ref, 0), dstate[...]))
        dgb = jnp.zeros((c, LANES), F32)
        for h in range(DN_HEADS):
            dqkv_ref[:, h * DN_DIM:(h + 1) * DN_DIM] = dq[h]
            dqkv_ref[:, DN_WIDTH + h * DN_DIM:DN_WIDTH + (h + 1) * DN_DIM] = dk[h]
            dqkv_ref[:, 2 * DN_WIDTH + h * DN_DIM:2 * DN_WIDTH + (h + 1) * DN_DIM] = dv[h]
            dgb = dgb + jnp.where(lane == h, db[h], 0.0) + jnp.where(lane == DN_HEADS + h, dg[h], 0.0)
        dstate[...] = dst
        dgb_ref[...] = dgb

    rev = lambda i: (n - 1 - i, 0)
    blk = _nbytes((c, 3 * DN_WIDTH), BF16) + 2 * _nbytes((c, LANES), F32) + _nbytes((c, DN_WIDTH), F32)
    blk += _nbytes((DN_HEADS, DN_DIM, DN_DIM), F32) + _nbytes((c, 3 * DN_WIDTH), F32)
    return pl.pallas_call(
        body, name=name, grid=(n,),
        in_specs=[pl.BlockSpec((c, 3 * DN_WIDTH), rev), pl.BlockSpec((c, LANES), rev),
                  pl.BlockSpec((1, DN_HEADS, DN_DIM, DN_DIM), lambda i: (n - 1 - i, 0, 0, 0)),
                  pl.BlockSpec((1, DN_HEADS, c, c), lambda i: (n - 1 - i, 0, 0, 0)),
                  pl.BlockSpec((c, DN_WIDTH), rev)],
        out_specs=[pl.BlockSpec((c, 3 * DN_WIDTH), rev), pl.BlockSpec((c, LANES), rev)],
        out_shape=[jax.ShapeDtypeStruct((s, 3 * DN_WIDTH), F32), jax.ShapeDtypeStruct((s, LANES), F32)],
        scratch_shapes=[pltpu.VMEM((DN_HEADS, DN_DIM, DN_DIM), F32)],
        compiler_params=_params(blk, 16 << 20),
    )(qkvn, gb, states, tinv, d_o)


def _da_scores(q2f, k2, sub, valid, distf, head):
    lane = lax.broadcasted_iota(jnp.int32, q2f.shape, 1)
    hmask = (lane < DA_DIM) if sub == 0 else (lane >= DA_DIM)
    qm = jnp.where(hmask, q2f, 0.0).astype(BF16)
    slope = 2.0 ** (-ALIBI_MAX_EXP * (head + 1) / DA_HEADS)
    sc = lax.dot_general(qm, k2, NT, preferred_element_type=F32) * (DA_DIM ** -0.5)
    return jnp.where(valid, sc - slope * distf, NEG), qm, hmask


def _da_mask(i, r):
    qi = lax.broadcasted_iota(jnp.int32, (DA_BLOCK, 2 * DA_BLOCK), 0)
    ki = lax.broadcasted_iota(jnp.int32, (DA_BLOCK, 2 * DA_BLOCK), 1)
    dist = qi + DA_BLOCK - ki
    valid = (dist >= 0) & (dist <= DA_BLOCK) & ((ki >= DA_BLOCK) | (i > 0))
    return valid, (dist * r).astype(F32)


def _da_fwd(name, pda, r):
    s = pda.shape[0]
    n = s // r
    nb = n // DA_BLOCK
    w = DA_WIDTH
    dav = pda.reshape(n, r * 3 * w)

    def body(q_ref, kc_ref, kp_ref, vc_ref, vp_ref, o_ref, lse_ref):
        i = pl.program_id(1)
        valid, distf = _da_mask(i, r)
        lane = lax.broadcasted_iota(jnp.int32, (DA_BLOCK, LANES), 1)
        lse = jnp.zeros((DA_BLOCK, LANES), F32)
        for hp in range(DA_HEADS // 2):
            sl = slice(hp * LANES, (hp + 1) * LANES)
            q2f = q_ref[:, sl].astype(F32)
            k2 = jnp.concatenate([kp_ref[:, sl], kc_ref[:, sl]], axis=0)
            v2 = jnp.concatenate([vp_ref[:, sl], vc_ref[:, sl]], axis=0)
            o2 = None
            for sub in range(2):
                head = 2 * hp + sub
                sc, _, hmask = _da_scores(q2f, k2, sub, valid, distf, head)
                mx = jnp.max(sc, axis=1, keepdims=True)
                p = jnp.exp(sc - mx)
                l = jnp.sum(p, axis=1, keepdims=True)
                pv = lax.dot_general(p.astype(BF16), v2, NN, preferred_element_type=F32) / l
                o2 = pv if sub == 0 else jnp.where(hmask, pv, o2)
                lse = jnp.where(lane == head, mx + jnp.log(l), lse)
            o_ref[:, sl] = o2.astype(o_ref.dtype)
        lse_ref[...] = lse

    prev = lambda col: (lambda p, i: (jnp.maximum(i - 1, 0), 3 * p + col))
    cur = lambda col: (lambda p, i: (i, 3 * p + col))
    blk = 5 * _nbytes((DA_BLOCK, w), BF16) + _nbytes((DA_BLOCK, w), F32) + _nbytes((DA_BLOCK, LANES), F32)
    o, lse = pl.pallas_call(
        body, name=name, grid=(r, nb),
        in_specs=[pl.BlockSpec((DA_BLOCK, w), cur(0)), pl.BlockSpec((DA_BLOCK, w), cur(1)),
                  pl.BlockSpec((DA_BLOCK, w), prev(1)), pl.BlockSpec((DA_BLOCK, w), cur(2)),
                  pl.BlockSpec((DA_BLOCK, w), prev(2))],
        out_specs=[pl.BlockSpec((DA_BLOCK, w), lambda p, i: (i, p)),
                   pl.BlockSpec((DA_BLOCK, LANES), lambda p, i: (i, p))],
        out_shape=[jax.ShapeDtypeStruct((n, r * w), BF16), jax.ShapeDtypeStruct((n, r * LANES), F32)],
        compiler_params=_params(blk, 8 << 20),
    )(dav, dav, dav, dav, dav)
    return o.reshape(s, w), lse.reshape(s, LANES)


def _da_bwd(name, pda, d_ob, lse_tot, delta, r):
    s = pda.shape[0]
    n = s // r
    nb = n // DA_BLOCK
    w = DA_WIDTH
    dav = pda.reshape(n, r * 3 * w)
    dov = d_ob.reshape(n, r * w)
    lv = lse_tot.reshape(n, r * LANES)
    dlv = delta.reshape(n, r * LANES)

    def body(q_ref, kc_ref, kp_ref, vc_ref, vp_ref, do_ref, l_ref, dl_ref, dq_ref, dk_ref, dv_ref, ck, cv):
        i = pl.program_id(1)

        @pl.when(i == 0)
        def _():
            ck[...] = jnp.zeros(ck.shape, F32)
            cv[...] = jnp.zeros(cv.shape, F32)

        @pl.when(i < nb)
        def _():
            valid, distf = _da_mask(i, r)
            lsev = l_ref[...]
            dlt = dl_ref[...]
            for hp in range(DA_HEADS // 2):
                sl = slice(hp * LANES, (hp + 1) * LANES)
                q2f = q_ref[:, sl].astype(F32)
                k2 = jnp.concatenate([kp_ref[:, sl], kc_ref[:, sl]], axis=0)
                v2 = jnp.concatenate([vp_ref[:, sl], vc_ref[:, sl]], axis=0)
                do2f = do_ref[:, sl].astype(F32)
                dq2 = jnp.zeros((DA_BLOCK, LANES), F32)
                dk2 = jnp.zeros((2 * DA_BLOCK, LANES), F32)
                dv2 = jnp.zeros((2 * DA_BLOCK, LANES), F32)
                for sub in range(2):
                    head = 2 * hp + sub
                    sc, qm, hmask = _da_scores(q2f, k2, sub, valid, distf, head)
                    p = jnp.exp(sc - _lane_col(lsev, head))
                    dom = jnp.where(hmask, do2f, 0.0).astype(BF16)
                    dp = lax.dot_general(dom, v2, NT, preferred_element_type=F32)
                    ds = (p * (dp - _lane_col(dlt, head)) * (DA_DIM ** -0.5)).astype(BF16)
                    dq2 = dq2 + jnp.where(hmask, lax.dot_general(ds, k2, NN, preferred_element_type=F32), 0.0)
                    dk2 = dk2 + lax.dot_general(ds, qm, TN, preferred_element_type=F32)
                    dv2 = dv2 + lax.dot_general(p.astype(BF16), dom, TN, preferred_element_type=F32)
                dq_ref[:, sl] = dq2.astype(dq_ref.dtype)
                dk_ref[:, sl] = (ck[:, sl] + dk2[:DA_BLOCK]).astype(dk_ref.dtype)
                dv_ref[:, sl] = (cv[:, sl] + dv2[:DA_BLOCK]).astype(dv_ref.dtype)
                ck[:, sl] = dk2[DA_BLOCK:]
                cv[:, sl] = dv2[DA_BLOCK:]

        @pl.when(i == nb)
        def _():
            dk_ref[...] = ck[...].astype(dk_ref.dtype)
            dv_ref[...] = cv[...].astype(dv_ref.dtype)

    qrow = lambda i: jnp.minimum(i, nb - 1)
    prev = lambda col: (lambda p, i: (jnp.maximum(qrow(i) - 1, 0), 3 * p + col))
    cur = lambda col: (lambda p, i: (qrow(i), 3 * p + col))
    same = lambda p, i: (qrow(i), p)
    late = lambda p, i: (jnp.maximum(i - 1, 0), p)
    blk = 6 * _nbytes((DA_BLOCK, w), BF16) + 2 * _nbytes((DA_BLOCK, LANES), F32) + 3 * _nbytes((DA_BLOCK, w), F32)
    dq, dk, dv = pl.pallas_call(
        body, name=name, grid=(r, nb + 1),
        in_specs=[pl.BlockSpec((DA_BLOCK, w), cur(0)), pl.BlockSpec((DA_BLOCK, w), cur(1)),
                  pl.BlockSpec((DA_BLOCK, w), prev(1)), pl.BlockSpec((DA_BLOCK, w), cur(2)),
                  pl.BlockSpec((DA_BLOCK, w), prev(2)), pl.BlockSpec((DA_BLOCK, w), same),
                  pl.BlockSpec((DA_BLOCK, LANES), same), pl.BlockSpec((DA_BLOCK, LANES), same)],
        out_specs=[pl.BlockSpec((DA_BLOCK, w), same), pl.BlockSpec((DA_BLOCK, w), late),
                   pl.BlockSpec((DA_BLOCK, w), late)],
        out_shape=[jax.ShapeDtypeStruct((n, r * w), BF16)] * 3,
        scratch_shapes=[pltpu.VMEM((DA_BLOCK, w), F32), pltpu.VMEM((DA_BLOCK, w), F32)],
        compiler_params=_params(blk, 12 << 20),
    )(dav, dav, dav, dav, dav, dov, lv, dlv)
    return dq.reshape(s, w), dk.reshape(s, w), dv.reshape(s, w)


def _head_expand():
    hrow = lax.broadcasted_iota(jnp.int32, (LANES, DA_WIDTH), 0)
    lcol = lax.broadcasted_iota(jnp.int32, (LANES, DA_WIDTH), 1)
    return jnp.where(lcol // DA_DIM == hrow, 1.0, 0.0).astype(F32)


def _ffn_up(name, a, ga, tg, tu):
    s, d = a.shape
    nsh, _, _, ffs = ga.shape
    bm = _pick(s, (1024, 512, 256, 128))

    def body(a_ref, wg_ref, wu_ref, g_ref, u_ref, f_ref):
        av = a_ref[...]
        g = lax.dot_general(av, wg_ref[...], NN, preferred_element_type=F32)
        u = lax.dot_general(av, wu_ref[...], NN, preferred_element_type=F32)
        g_ref[...] = g.astype(BF16)
        u_ref[...] = u.astype(BF16)
        f_ref[...] = (_silu(g) * u).astype(BF16)

    wspec = lambda t: pl.BlockSpec((None, None, d, ffs), lambda i, j: (j, t, 0, 0))
    ospec = pl.BlockSpec((None, bm, ffs), lambda i, j: (j, i, 0))
    blk = _nbytes((bm, d), BF16) + 2 * _nbytes((d, ffs), BF16) + 3 * _nbytes((bm, ffs), BF16)
    return pl.pallas_call(
        body, name=name, grid=(s // bm, nsh),
        in_specs=[pl.BlockSpec((bm, d), lambda i, j: (i, 0)), wspec(tg), wspec(tu)],
        out_specs=[ospec] * 3, out_shape=[jax.ShapeDtypeStruct((nsh, s, ffs), BF16)] * 3,
        compiler_params=_params(blk, 4 * _nbytes((bm, ffs), F32)),
    )(a, ga, ga)


def _ffn_fwd(tag, h_in, ln, sh, sc, gt, ga, tg, tu, gb, td, weight):
    s, d = h_in.shape
    nsh, _, ffs, _ = gb.shape
    (a,) = _rowwise(tag + "_norm", lambda h, l, s1, s2: ((_rms_mod(h, l, s1, s2),), ()), [h_in], [ln, sh, sc],
                    [(d, BF16)])
    g, u, f = _ffn_up(tag + "_up", a, ga, tg, tu)
    bm, bn = _pick(s, (1024, 512, 256, 128)), _pick(d, (512, 256, 128))
    io = pl.BlockSpec((bm, bn), lambda i, j, kk: (i, j))
    h_out, o = _mm_core(
        tag + "_down", (s // bm, d // bn, nsh), nsh,
        [(f, pl.BlockSpec((None, bm, ffs), lambda i, j, kk: (kk, i, 0)),
          gb, pl.BlockSpec((None, None, ffs, bn), lambda i, j, kk: (kk, td, 0, j)), NN)],
        [((s, d), F32, io), ((s, d), BF16, io)], (bm, bn),
        epi=lambda acc, h, gv: (h + weight * gv * acc, acc),
        epi_ins=[(h_in, io), (gt, pl.BlockSpec((1, bn), lambda i, j, kk: (0, j)))])
    return h_out, dict(a=a, g=g, u=u, f=f, o=o)


def _resid_bwd(tag, dh_out, o, gt, weight):
    d = dh_out.shape[1]

    def fn(dh, ov, g):
        return (weight * g * dh,), (jnp.sum(weight * dh * ov.astype(F32), axis=0, keepdims=True),)

    do, d_gt = _rowwise(tag + "_resid_bwd", fn, [dh_out, o], [gt], [(d, BF16)], [(1, d)])
    return do, d_gt


def _norm_bwd(tag, h_in, da, dh_out, ln, sh, sc):
    d = h_in.shape[1]

    def fn(h, dav, dh, l, s1, s2):
        _, vjp = jax.vjp(_rms_mod, h, l, s1, s2)
        gh, gl, gs1, gs2 = vjp(dav)
        return (dh + gh,), (gl, gs1, gs2)

    return _rowwise(tag + "_norm_bwd", fn, [h_in, da, dh_out], [ln, sh, sc], [(d, F32)], [(1, d)] * 3)


def _ffn_bwd(tag, h_in, dh_out, sv, ln, sh, sc, gt, ga, tg, tu, gb, td, weight):
    s, d = h_in.shape
    nsh, _, ffs, _ = gb.shape
    bm, bn = _pick(s, (1024, 512, 256, 128)), _pick(d, (512, 256, 128))
    bk = _pick(s, (1024, 512, 256, 128))
    do, d_gt = _resid_bwd(tag, dh_out, sv["o"], gt, weight)

    def act_bwd(df, g, u):
        _, vjp = jax.vjp(_swiglu_act, g, u)
        return vjp(df)

    hid = pl.BlockSpec((None, bm, ffs), lambda i, j, kk: (j, i, 0))
    dg, du = _mm_core(
        tag + "_down_dx", (s // bm, nsh, 1), 1,
        [(do, pl.BlockSpec((bm, d), lambda i, j, kk: (i, 0)),
          gb, pl.BlockSpec((None, None, ffs, d), lambda i, j, kk: (j, td, 0, 0)), NT)],
        [((nsh, s, ffs), BF16, hid)] * 2, (bm, ffs), epi=act_bwd, epi_ins=[(sv["g"], hid), (sv["u"], hid)])
    (d_wd,) = _mm_core(
        tag + "_down_dw", (nsh, d // bn, s // bk), s // bk,
        [(sv["f"], pl.BlockSpec((None, bk, ffs), lambda i, j, kk: (i, kk, 0)),
          do, pl.BlockSpec((bk, bn), lambda i, j, kk: (kk, j)), TN)],
        [((nsh, ffs, d), BF16, pl.BlockSpec((None, ffs, bn), lambda i, j, kk: (i, 0, j)))], (ffs, bn))
    kmaj = pl.BlockSpec((None, bm, ffs), lambda i, j, kk: (kk, i, 0))
    wsp = lambda t: pl.BlockSpec((None, None, bn, ffs), functools.partial(lambda i, j, kk, t: (kk, t, j, 0), t=t))
    (da,) = _mm_core(
        tag + "_up_dx", (s // bm, d // bn, nsh), nsh, [(dg, kmaj, ga, wsp(tg), NT), (du, kmaj, ga, wsp(tu), NT)],
        [((s, d), F32, pl.BlockSpec((bm, bn), lambda i, j, kk: (i, j)))], (bm, bn))
    dws = []
    for nm, dh in (("_wg_dw", dg), ("_wu_dw", du)):
        (dw,) = _mm_core(
            tag + nm, (1, nsh, s // bk), s // bk,
            [(sv["a"], pl.BlockSpec((bk, d), lambda i, j, kk: (kk, 0)),
              dh, pl.BlockSpec((None, bk, ffs), lambda i, j, kk: (j, kk, 0)), TN)],
            [((nsh, d, ffs), BF16, pl.BlockSpec((None, d, ffs), lambda i, j, kk: (j, 0, 0)))], (d, ffs))
        dws.append(dw)
    dh_in, d_ln, d_sh, d_sc = _norm_bwd(tag, h_in, da, dh_out, ln, sh, sc)
    return dh_in, dict(wg=dws[0], wu=dws[1], wd=d_wd), dict(ln=d_ln, sh=d_sh, sc=d_sc, gt=d_gt)


def _mixer_fwd(tag, h_in, ln, sh, sc, gt, w, sp):
    d = h_in.shape[1]
    (a,) = _rowwise(tag + "_norm", lambda h, l, s1, s2: ((_rms_mod(h, l, s1, s2),), ()), [h_in], [ln, sh, sc],
                    [(d, BF16)])
    (pq,) = _matmul(tag + "_pq", a, w["wq"], outs=(BF16,))
    (pz,) = _matmul(tag + "_pz", a, w["wz"], outs=(BF16,))
    (pba,) = _matmul(tag + "_pba", a, w["wba"])
    (pda,) = _matmul(tag + "_pda", a, w["wda"], outs=(BF16,))
    (pg,) = _matmul(tag + "_pg", a, w["wg"], outs=(BF16,))
    yc, qkvn, gb = _conv_prep_fwd(tag + "_conv", pq, sp["conv8"], pba, sp["alog"], sp["dtb"])
    o_a, states, tinv = _delta_fwd(tag + "_delta", qkvn, gb)
    (o_an,) = _rowwise(tag + "_dnorm", lambda o, z, dn: ((_dn_outnorm(o, z.astype(F32), dn),), ()), [o_a, pz],
                       [sp["dn"]], [(DN_WIDTH, BF16)])
    ops, lses = [], []
    for (_, r) in DA_PATTERNS:
        o_p, lse_p = _da_fwd(f"{tag}_da{r}", pda, r)
        ops.append(o_p)
        lses.append(lse_p)

    def merge(o1, o2, o3, l1, l2, l3):
        mx = jnp.maximum(jnp.maximum(l1, l2), l3)
        e1, e2, e3 = jnp.exp(l1 - mx), jnp.exp(l2 - mx), jnp.exp(l3 - mx)
        tot = e1 + e2 + e3
        ex = _head_expand()
        up = lambda wgt: lax.dot_general(wgt / tot, ex, NN, precision=HI, preferred_element_type=F32)
        return (up(e1) * o1 + up(e2) * o2 + up(e3) * o3, mx + jnp.log(tot)), ()

    o_b, lse_tot = _rowwise(tag + "_merge", merge, ops + lses, [], [(DA_WIDTH, BF16), (LANES, F32)])
    (y_a,) = _matmul(tag + "_wa", o_an, w["w_a"], outs=(BF16,))
    (y_b,) = _matmul(tag + "_wb", o_b, w["w_b"], outs=(BF16,))

    def gate(ga, gbv, ya, yb):
        return _sigmoid(ga.astype(F32)) * ya.astype(F32) + _sigmoid(gbv.astype(F32)) * yb.astype(F32)

    (merged,) = _rowwise(tag + "_gate", lambda *v: ((gate(*v),), ()), [(pg, d, 0), (pg, d, 1), y_a, y_b], [],
                         [(d, BF16)])
    h_out, m = _matmul(tag + "_wo", merged, w["w_o"], outs=(F32, BF16), epi_rows=[h_in], epi_bcast=[gt],
                       epi=lambda acc, h, g: (h + g * acc, acc))
    sv = dict(a=a, pq=pq, pz=pz, pba=pba, pda=pda, pg=pg, yc=yc, qkvn=qkvn, gb=gb, o_a=o_a, states=states, tinv=tinv,
              o_an=o_an, o_b=o_b, lse=lse_tot, y_a=y_a, y_b=y_b, merged=merged, m=m, gate=gate)
    return h_out, sv


def _mixer_bwd(tag, h_in, dh_out, sv, ln, sh, sc, gt, w, sp):
    d = h_in.shape[1]
    dm, d_gt = _resid_bwd(tag, dh_out, sv["m"], gt, 1.0)
    (d_merged,) = _matmul(tag + "_wo_dx", dm, w["w_o"], tb=True, outs=(BF16,))
    (d_wo,) = _matmul(tag + "_wo_dw", sv["merged"], dm, ta=True, outs=(BF16,))
    gate = sv["gate"]

    def gate_bwd(dmg, ga, gbv, ya, yb):
        _, vjp = jax.vjp(gate, ga.astype(F32), gbv.astype(F32), ya.astype(F32), yb.astype(F32))
        dga, dgb, dya, dyb = vjp(dmg.astype(F32))
        return (jnp.concatenate([dga, dgb], axis=1), dya, dyb), ()

    pg = sv["pg"]
    d_pg, d_ya, d_yb = _rowwise(tag + "_gate_bwd", gate_bwd, [d_merged, (pg, d, 0), (pg, d, 1), sv["y_a"], sv["y_b"]],
                                [], [(2 * d, BF16), (d, BF16), (d, BF16)])
    (d_oan,) = _matmul(tag + "_wa_dx", d_ya, w["w_a"], tb=True)
    (d_wa,) = _matmul(tag + "_wa_dw", sv["o_an"], d_ya, ta=True, outs=(BF16,))
    (d_ob,) = _matmul(tag + "_wb_dx", d_yb, w["w_b"], tb=True, outs=(BF16,))
    (d_wb,) = _matmul(tag + "_wb_dw", sv["o_b"], d_yb, ta=True, outs=(BF16,))

    def dnorm_bwd(doan, o, z, dn):
        _, vjp = jax.vjp(_dn_outnorm, o, z.astype(F32), dn)
        go, gz, gdn = vjp(doan)
        return (go, gz), (gdn,)

    d_oa, d_pz, d_dn = _rowwise(tag + "_dnorm_bwd", dnorm_bwd, [d_oan, sv["o_a"], sv["pz"]], [sp["dn"]],
                                [(DN_WIDTH, F32), (DN_WIDTH, BF16)], [(1, DN_DIM)])
    d_qkvn, d_gb = _delta_bwd(tag + "_delta_bwd", sv["qkvn"], sv["gb"], sv["states"], sv["tinv"], d_oa)

    def prep_bwd(dq, dgbv, yc, pba, alog, dtb):
        _, vjp = jax.vjp(_dn_prep, yc.astype(F32), pba, alog, dtb)
        gyc, gpba, galog, gdtb = vjp((dq, dgbv))
        return (gyc, gpba), (galog, gdtb)

    d_yc, d_pba, d_alog, d_dtb = _rowwise(tag + "_prep_bwd", prep_bwd, [d_qkvn, d_gb, sv["yc"], sv["pba"]],
                                          [sp["alog"], sp["dtb"]], [(3 * DN_WIDTH, BF16), (LANES, BF16)],
                                          [(1, LANES), (1, LANES)], bm=128)
    d_pq, d_conv = _conv_bwd(tag + "_conv_bwd", d_yc, sv["pq"], sp["conv8"])

    def delta_fn(dob, ob):
        prod = dob.astype(F32) * ob.astype(F32)
        return (lax.dot_general(prod, _head_expand(), NT, precision=HI, preferred_element_type=F32),), ()

    (delta,) = _rowwise(tag + "_da_delta", delta_fn, [d_ob, sv["o_b"]], [], [(LANES, F32)])
    grads = [_da_bwd(f"{tag}_da{r}_bwd", sv["pda"], d_ob, sv["lse"], delta, r) for (_, r) in DA_PATTERNS]

    def sum3(*parts):
        q1, k1, v1, q2, k2, v2, q3, k3, v3 = (p.astype(F32) for p in parts)
        return (jnp.concatenate([q1 + q2 + q3, k1 + k2 + k3, v1 + v2 + v3], axis=1),), ()

    (d_pda,) = _rowwise(tag + "_da_sum", sum3, [t for g in grads for t in g], [], [(3 * DA_WIDTH, BF16)])

    a = sv["a"]
    (da,) = _matmul(tag + "_pq_dx", d_pq, w["wq"], tb=True)
    add = lambda acc, prev: (acc + prev,)
    (da,) = _matmul(tag + "_pz_dx", d_pz, w["wz"], tb=True, epi_rows=[da], epi=add)
    (da,) = _matmul(tag + "_pba_dx", d_pba, w["wba"], tb=True, epi_rows=[da], epi=add)
    (da,) = _matmul(tag + "_pda_dx", d_pda, w["wda"], tb=True, epi_rows=[da], epi=add)
    (da,) = _matmul(tag + "_pg_dx", d_pg, w["wg"], tb=True, epi_rows=[da], epi=add)
    (d_wq,) = _matmul(tag + "_pq_dw", a, d_pq, ta=True, outs=(BF16,))
    (d_wz,) = _matmul(tag + "_pz_dw", a, d_pz, ta=True, outs=(BF16,))
    (d_wba,) = _matmul(tag + "_pba_dw", a, d_pba, ta=True, outs=(BF16,))
    (d_wda,) = _matmul(tag + "_pda_dw", a, d_pda, ta=True, outs=(BF16,))
    (d_wg,) = _matmul(tag + "_pg_dw", a, d_pg, ta=True, outs=(BF16,))
    dh_in, d_ln, d_sh, d_sc = _norm_bwd(tag, h_in, da, dh_out, ln, sh, sc)
    wgrads = dict(wq=d_wq, wz=d_wz, wba=d_wba, wda=d_wda, wg=d_wg, w_a=d_wa, w_b=d_wb, w_o=d_wo)
    small = dict(ln=d_ln, sh=d_sh, sc=d_sc, gt=d_gt, dn=d_dn, alog=d_alog, dtb=d_dtb, conv=d_conv)
    return dh_in, wgrads, small


def _loss_head(h, target, fnorm):
    d = h.shape[1]

    def fn(hv, tv, fw):
        def lossf(hh, ww):
            y = hh * lax.rsqrt(jnp.mean(hh * hh, axis=-1, keepdims=True) + NORM_EPS) * ww
            return 0.5 * jnp.sum(jnp.mean(jnp.square(y - tv), axis=-1))

        val, (dh, dw) = jax.value_and_grad(lossf, argnums=(0, 1))(hv, fw)
        return (dh,), (jnp.full((1, LANES), val, F32), dw)

    return _rowwise("loss_head", fn, [h, target], [fnorm], [(d, F32)], [(1, LANES), (1, d)])


def _row(v):
    return v.reshape(1, -1)


def _pad_lanes(v, offset):
    return jnp.pad(v.reshape(1, -1), ((0, 0), (offset, LANES - offset - v.shape[0])))


_UP_SLOTS = dict(ffn1_wg=0, ffn1_wu=1, ffn2_wg=2, ffn2_wu=3)
_DOWN_SLOTS = dict(ffn1_wd=0, ffn2_wd=1)


def _local_step(x2, target, mod, layer_weights, small):
    depth = mod.shape[0]
    d = x2.shape[1]
    h = x2
    saved = []
    mods = []
    up = lambda l, nm: _UP_SLOTS[nm]
    down = lambda l, nm: _DOWN_SLOTS[nm]
    for l in range(depth):
        m9 = [_row(mod[l, i * d:(i + 1) * d]) for i in range(N_ADA)]
        sp = dict(conv8=jnp.pad(small["conv_w"][l], ((0, 8 - DN_CONV), (0, 0))),
                  alog=_pad_lanes(small["a_log"][l], DN_HEADS), dtb=_pad_lanes(small["dt_bias"][l], DN_HEADS),
                  dn=_row(small["dn_norm"][l]))
        ga, gb, w = layer_weights(l, h)
        h0 = h
        h1, sv1 = _ffn_fwd(f"l{l}_ffn1", h0, _row(small["ln_ffn1"][l]), m9[0], m9[1], m9[2], ga, up(l, "ffn1_wg"),
                           up(l, "ffn1_wu"), gb, down(l, "ffn1_wd"), 0.5)
        h2, sv2 = _mixer_fwd(f"l{l}_mix", h1, _row(small["ln_mix"][l]), m9[3], m9[4], m9[5], w, sp)
        h3, sv3 = _ffn_fwd(f"l{l}_ffn2", h2, _row(small["ln_ffn2"][l]), m9[6], m9[7], m9[8], ga, up(l, "ffn2_wg"),
                           up(l, "ffn2_wu"), gb, down(l, "ffn2_wd"), 0.5)
        saved.append((h0, h1, h2, sv1, sv2, sv3, sp, ga, gb, w))
        mods.append(m9)
        h = h3
    dh, loss_part, d_fnorm = _loss_head(h, target, _row(small["final_norm"]))
    wgrads, sgrads, dmods = [], [], []
    for l in reversed(range(depth)):
        h0, h1, h2, sv1, sv2, sv3, sp, ga, gb, w = saved[l]
        m9 = mods[l]
        dh, g3, s3 = _ffn_bwd(f"l{l}_ffn2", h2, dh, sv3, _row(small["ln_ffn2"][l]), m9[6], m9[7], m9[8], ga,
                              up(l, "ffn2_wg"), up(l, "ffn2_wu"), gb, down(l, "ffn2_wd"), 0.5)
        dh, g2, s2 = _mixer_bwd(f"l{l}_mix", h1, dh, sv2, _row(small["ln_mix"][l]), m9[3], m9[4], m9[5], w, sp)
        dh, g1, s1 = _ffn_bwd(f"l{l}_ffn1", h0, dh, sv1, _row(small["ln_ffn1"][l]), m9[0], m9[1], m9[2], ga,
                              up(l, "ffn1_wg"), up(l, "ffn1_wu"), gb, down(l, "ffn1_wd"), 0.5)
        wgrads.append(dict(ffn1_wg=g1["wg"], ffn1_wu=g1["wu"], ffn1_wd=g1["wd"], ffn2_wg=g3["wg"], ffn2_wu=g3["wu"],
                           ffn2_wd=g3["wd"], **g2))
        dmods.append(jnp.concatenate([s1["sh"], s1["sc"], s1["gt"], s2["sh"], s2["sc"], s2["gt"],
                                      s3["sh"], s3["sc"], s3["gt"]], axis=1))
        sgrads.append(dict(ln_ffn1=s1["ln"][0], ln_mix=s2["ln"][0], ln_ffn2=s3["ln"][0],
                           a_log=s2["alog"][0, DN_HEADS:2 * DN_HEADS], dt_bias=s2["dtb"][0, DN_HEADS:2 * DN_HEADS],
                           dn_norm=s2["dn"][0], conv_w=s2["conv"][:DN_CONV]))
    wgrads.reverse()
    sgrads.reverse()
    dmods.reverse()
    return loss_part[0, 0], dh, jnp.concatenate(dmods, axis=0), wgrads, sgrads, d_fnorm[0]


def _flip(v, bit):
    return 1 - v if bit else v


def _allgather8(name, x):
    r, c = x.shape

    def body(x_ref, out_ref, send_sems, recv_sems, local_sem):
        mx, my, mc = lax.axis_index("x"), lax.axis_index("y"), lax.axis_index("c")
        me = 4 * mx + 2 * my + mc
        mine = pltpu.make_async_copy(x_ref, out_ref.at[me], local_sem)
        mine.start()
        sends = []
        for k in range(1, 8):
            peer = (_flip(mx, k & 4), _flip(my, k & 2), _flip(mc, k & 1))
            cp = pltpu.make_async_remote_copy(src_ref=x_ref, dst_ref=out_ref.at[me], send_sem=send_sems.at[k - 1],
                                              recv_sem=recv_sems.at[k - 1], device_id=peer, device_id_type=MESH)
            cp.start()
            sends.append(cp)
        for k in range(1, 8):
            peer = (_flip(mx, k & 4), _flip(my, k & 2), _flip(mc, k & 1))
            src = 4 * peer[0] + 2 * peer[1] + peer[2]
            pltpu.make_async_remote_copy(src_ref=x_ref, dst_ref=out_ref.at[src], send_sem=send_sems.at[k - 1],
                                         recv_sem=recv_sems.at[k - 1], device_id=peer, device_id_type=MESH).wait_recv()
        for cp in sends:
            cp.wait_send()
        mine.wait()

    return pl.pallas_call(
        body, name=name, out_shape=jax.ShapeDtypeStruct((8, r, c), x.dtype),
        in_specs=[pl.BlockSpec(memory_space=pltpu.VMEM)], out_specs=pl.BlockSpec(memory_space=pltpu.VMEM),
        scratch_shapes=[pltpu.SemaphoreType.DMA((7,)), pltpu.SemaphoreType.DMA((7,)), pltpu.SemaphoreType.DMA],
        compiler_params=_params(9 * _nbytes((r, c), x.dtype)),
    )(x)


def _chip_peers(mx, my):
    chips = [(1 - mx, my), (mx, 1 - my), (1 - mx, 1 - my)]
    return chips, [2 * cx + cy for (cx, cy) in chips]


_ANY = pl.BlockSpec(memory_space=pl.ANY)


def _half(mc, t):
    return pl.ds(mc * (t // 2), t // 2)


def _row_half(mc, r):
    return pl.ds(pl.multiple_of(mc * (r // 2), 16), r // 2)


def _gather_groups(name, shards):
    ng = len(shards)

    def body(*refs):
        xs, outs = refs[:ng], refs[ng:2 * ng]
        send_sems, recv_sems = refs[2 * ng:]
        mx, my, mc = lax.axis_index("x"), lax.axis_index("y"), lax.axis_index("c")
        j = 2 * mx + my
        chips, idxs = _chip_peers(mx, my)
        sib = (mx, my, 1 - mc)

        def copy(k, src, dst, to):
            return pltpu.make_async_remote_copy(src_ref=src, dst_ref=dst, send_sem=send_sems.at[k],
                                                recv_sem=recv_sems.at[k], device_id=to, device_id_type=MESH)

        first, passed = [], []
        for g in range(ng):
            mine = _row_half(mc, shards[g].shape[1])
            for t, chip in enumerate(chips):
                cp = copy(6 * g + t, xs[g].at[:, mine], outs[g].at[j, :, mine], (*chip, mc))
                cp.start()
                first.append(cp)
        for g in range(ng):
            mine = _row_half(mc, shards[g].shape[1])
            for t, chip in enumerate(chips):
                landed = outs[g].at[idxs[t], :, mine]
                copy(6 * g + t, landed, landed, (*chip, mc)).wait_recv()
                fwd = copy(6 * g + 3 + t, landed, landed, sib)
                fwd.start()
                passed.append(fwd)
        for g in range(ng):
            theirs_half = _row_half(1 - mc, shards[g].shape[1])
            for t in range(3):
                theirs = outs[g].at[idxs[t], :, theirs_half]
                copy(6 * g + 3 + t, theirs, theirs, sib).wait_recv()
        for cp in first + passed:
            cp.wait_send()

    outs = pl.pallas_call(
        body, name=name, out_shape=[jax.ShapeDtypeStruct((4,) + x.shape, x.dtype) for x in shards],
        in_specs=[_ANY] * ng, out_specs=[_ANY] * ng,
        scratch_shapes=[pltpu.SemaphoreType.DMA((6 * ng,)), pltpu.SemaphoreType.DMA((6 * ng,))],
    )(*shards)
    return _place_own_slab(outs, shards)


def _place_own_slab(outs, shards):
    chip = 2 * lax.axis_index("x") + lax.axis_index("y")
    return [lax.dynamic_update_slice(o, x[None], (chip,) + (0,) * x.ndim) for o, x in zip(outs, shards)]


_HBM = pl.BlockSpec(memory_space=pltpu.HBM)
_SEM = pl.BlockSpec(memory_space=pltpu.SEMAPHORE)
_DATAFLOW = pltpu.SideEffectType.DATAFLOW_SIDE_EFFECTING


def _ici_gather_copies(src_refs, land_refs, send_sems, recv_sems):
    mx, my, mc = lax.axis_index("x"), lax.axis_index("y"), lax.axis_index("c")
    j = 2 * mx + my
    chips, idxs = _chip_peers(mx, my)
    sends, recvs = [], []
    for g, src in enumerate(src_refs):
        mine = _row_half(mc, src.shape[1])
        for t, chip in enumerate(chips):
            common = dict(send_sem=send_sems.at[3 * g + t], recv_sem=recv_sems.at[3 * g + t], device_id=(*chip, mc),
                          device_id_type=MESH)
            sends.append(pltpu.make_async_remote_copy(src_ref=src.at[:, mine], dst_ref=land_refs[g].at[j, :, mine],
                                                      **common))
            recvs.append(pltpu.make_async_remote_copy(src_ref=src.at[:, mine],
                                                      dst_ref=land_refs[g].at[idxs[t], :, mine], **common))
    return sends, recvs


def _gather_start(name, shards):
    ng = len(shards)

    def body(*refs):
        srcs, lands = refs[:ng], refs[ng:2 * ng]
        send_sems, recv_sems = refs[2 * ng], refs[2 * ng + 1]
        token = refs[-1]
        sends, _ = _ici_gather_copies(srcs, lands, send_sems, recv_sems)
        for cp in sends:
            cp.start()
        token[...] = jnp.zeros(token.shape, token.dtype)

    srcs = [pltpu.with_memory_space_constraint(x, pltpu.HBM) for x in shards]
    lands = [pltpu.with_memory_space_constraint(lax.empty((4,) + x.shape, x.dtype), pltpu.HBM) for x in shards]
    res = pl.pallas_call(
        body, name=name,
        out_shape=(pltpu.SemaphoreType.DMA((3 * ng,)), pltpu.SemaphoreType.DMA((3 * ng,)),
                   *[pltpu.HBM(x.shape, x.dtype) for x in srcs], *[pltpu.HBM(x.shape, x.dtype) for x in lands],
                   jax.ShapeDtypeStruct((8, LANES), F32)),
        in_specs=[_HBM] * (2 * ng),
        out_specs=(_SEM, _SEM, *[_HBM] * (2 * ng), pl.BlockSpec(memory_space=pltpu.VMEM)),
        input_output_aliases={i: 2 + i for i in range(2 * ng)},
        compiler_params=pltpu.CompilerParams(has_side_effects=_DATAFLOW),
    )(*srcs, *lands)
    return dict(send_sems=res[0], recv_sems=res[1], srcs=list(res[2:2 + ng]), lands=list(res[2 + ng:2 + 2 * ng]),
                token=res[-1])


def _gather_wait(name, started, after):
    ng = len(started["srcs"])

    def body(*refs):
        srcs, lands = refs[:ng], refs[ng:2 * ng]
        send_sems, recv_sems = refs[2 * ng], refs[2 * ng + 1]
        sends, recvs = _ici_gather_copies(srcs, lands, send_sems, recv_sems)
        for cp in sends:
            cp.wait_send()
        for cp in recvs:
            cp.wait_recv()

    res = pl.pallas_call(
        body, name=name,
        out_shape=[pltpu.HBM(x.shape, x.dtype) for x in started["srcs"] + started["lands"]],
        in_specs=[_HBM] * (2 * ng) + [_SEM, _SEM, _ANY], out_specs=[_HBM] * (2 * ng),
        input_output_aliases={i: i for i in range(2 * ng)},
        compiler_params=pltpu.CompilerParams(has_side_effects=_DATAFLOW),
    )(*started["srcs"], *started["lands"], started["send_sems"], started["recv_sems"], after)
    return list(res[:ng]), list(res[ng:])


def _pair_forward_groups(name, lands, shards):
    ng = len(lands)

    def body(*refs):
        ins, outs = refs[:ng], refs[ng:2 * ng]
        send_sems, recv_sems = refs[2 * ng:]
        mx, my, mc = lax.axis_index("x"), lax.axis_index("y"), lax.axis_index("c")
        _, idxs = _chip_peers(mx, my)
        sib = (mx, my, 1 - mc)
        cps = []
        for g in range(ng):
            mine = _row_half(mc, lands[g].shape[2])
            for t in range(3):
                cp = pltpu.make_async_remote_copy(src_ref=ins[g].at[idxs[t], :, mine], dst_ref=outs[g].at[idxs[t], :, mine],
                                                  send_sem=send_sems.at[3 * g + t], recv_sem=recv_sems.at[3 * g + t],
                                                  device_id=sib, device_id_type=MESH)
                cp.start()
                cps.append(cp)
        for g in range(ng):
            theirs = _row_half(1 - mc, lands[g].shape[2])
            for t in range(3):
                pltpu.make_async_remote_copy(src_ref=ins[g].at[idxs[t], :, theirs], dst_ref=outs[g].at[idxs[t], :, theirs],
                                             send_sem=send_sems.at[3 * g + t], recv_sem=recv_sems.at[3 * g + t],
                                             device_id=sib, device_id_type=MESH).wait_recv()
        for cp in cps:
            cp.wait_send()

    outs = pl.pallas_call(
        body, name=name, out_shape=[jax.ShapeDtypeStruct(x.shape, x.dtype) for x in lands],
        in_specs=[_ANY] * ng, out_specs=[_ANY] * ng, input_output_aliases={i: i for i in range(ng)},
        scratch_shapes=[pltpu.SemaphoreType.DMA((3 * ng,)), pltpu.SemaphoreType.DMA((3 * ng,))],
    )(*lands)
    return _place_own_slab(outs, shards)


def _pair_swap_groups(name, gs):
    ng = len(gs)

    def body(*refs):
        xs, outs = refs[:ng], refs[ng:2 * ng]
        send_sems, recv_sems = refs[2 * ng:]
        mx, my, mc = lax.axis_index("x"), lax.axis_index("y"), lax.axis_index("c")
        cps = []
        for g in range(ng):
            cp = pltpu.make_async_remote_copy(src_ref=xs[g].at[:, _half(1 - mc, gs[g].shape[1])], dst_ref=outs[g],
                                              send_sem=send_sems.at[g], recv_sem=recv_sems.at[g],
                                              device_id=(mx, my, 1 - mc), device_id_type=MESH)
            cp.start()
            cps.append(cp)
        for cp in cps:
            cp.wait()

    return pl.pallas_call(
        body, name=name,
        out_shape=[jax.ShapeDtypeStruct((x.shape[0], x.shape[1] // 2) + x.shape[2:], x.dtype) for x in gs],
        in_specs=[_ANY] * ng, out_specs=[_ANY] * ng,
        scratch_shapes=[pltpu.SemaphoreType.DMA((ng,)), pltpu.SemaphoreType.DMA((ng,))],
    )(*gs)


def _chip_scatter_groups(name, ps):
    ng = len(ps)

    def body(*refs):
        xs, outs = refs[:ng], refs[ng:2 * ng]
        send_sems, recv_sems = refs[2 * ng:]
        mx, my, mc = lax.axis_index("x"), lax.axis_index("y"), lax.axis_index("c")
        j = 2 * mx + my
        chips, idxs = _chip_peers(mx, my)
        sends = []
        for g in range(ng):
            for t, chip in enumerate(chips):
                cp = pltpu.make_async_remote_copy(src_ref=xs[g].at[idxs[t]], dst_ref=outs[g].at[j],
                                                  send_sem=send_sems.at[3 * g + t], recv_sem=recv_sems.at[3 * g + t],
                                                  device_id=(*chip, mc), device_id_type=MESH)
                cp.start()
                sends.append(cp)
        for g in range(ng):
            for t, chip in enumerate(chips):
                pltpu.make_async_remote_copy(src_ref=xs[g].at[idxs[t]], dst_ref=outs[g].at[idxs[t]],
                                             send_sem=send_sems.at[3 * g + t], recv_sem=recv_sems.at[3 * g + t],
                                             device_id=(*chip, mc), device_id_type=MESH).wait_recv()
        for cp in sends:
            cp.wait_send()

    outs = pl.pallas_call(
        body, name=name, out_shape=[jax.ShapeDtypeStruct(x.shape, x.dtype) for x in ps],
        in_specs=[_ANY] * ng, out_specs=[_ANY] * ng,
        scratch_shapes=[pltpu.SemaphoreType.DMA((3 * ng,)), pltpu.SemaphoreType.DMA((3 * ng,))],
    )(*ps)
    chip = 2 * lax.axis_index("x") + lax.axis_index("y")
    return [lax.dynamic_update_slice(o, lax.dynamic_index_in_dim(x, chip, 0, keepdims=True), (chip,) + (0,) * (x.ndim - 1))
            for o, x in zip(outs, ps)]


def _pair_merge_groups(name, fs):
    ng = len(fs)

    def body(*refs):
        xs, outs = refs[:ng], refs[ng:2 * ng]
        send_sems, recv_sems = refs[2 * ng:]
        mx, my, mc = lax.axis_index("x"), lax.axis_index("y"), lax.axis_index("c")
        cps = []
        for g in range(ng):
            mine = _half(mc, 2 * fs[g].shape[0])
            cp = pltpu.make_async_remote_copy(src_ref=xs[g], dst_ref=outs[g].at[mine], send_sem=send_sems.at[g],
                                              recv_sem=recv_sems.at[g], device_id=(mx, my, 1 - mc), device_id_type=MESH)
            cp.start()
            cps.append(cp)
        for g in range(ng):
            theirs = outs[g].at[_half(1 - mc, 2 * fs[g].shape[0])]
            pltpu.make_async_remote_copy(src_ref=xs[g], dst_ref=theirs, send_sem=send_sems.at[g],
                                         recv_sem=recv_sems.at[g], device_id=(mx, my, 1 - mc),
                                         device_id_type=MESH).wait_recv()
        for cp in cps:
            cp.wait_send()

    outs = pl.pallas_call(
        body, name=name, out_shape=[jax.ShapeDtypeStruct((2 * x.shape[0],) + x.shape[1:], x.dtype) for x in fs],
        in_specs=[_ANY] * ng, out_specs=[_ANY] * ng,
        scratch_shapes=[pltpu.SemaphoreType.DMA((ng,)), pltpu.SemaphoreType.DMA((ng,))],
    )(*fs)
    mc = lax.axis_index("c")
    return [lax.dynamic_update_slice(o, x, (mc * x.shape[0],) + (0,) * (x.ndim - 1)) for o, x in zip(outs, fs)]


def _block_rows(r, w, itemsize=4, budget=4 << 20):
    for c in (r, 2048, 1024, 512, 256, 128, 64, 32, 16):
        if c <= r and r % c == 0 and c * w * itemsize <= budget:
            return c
    return r


def _pair_sum(name, g, got, cidx):
    ns, t, r, w = g.shape
    th = t // 2
    bm = _block_rows(r, w)

    def body(c_ref, a_ref, b_ref, o_ref):
        o_ref[...] = (a_ref[...].astype(F32) + b_ref[...].astype(F32)).astype(o_ref.dtype)

    blk = (None, None, bm, w)
    return pl.pallas_call(
        body, name=name,
        grid_spec=pltpu.PrefetchScalarGridSpec(
            num_scalar_prefetch=1, grid=(ns, th, r // bm),
            in_specs=[pl.BlockSpec(blk, lambda s, tt, i, c: (s, c[0] * th + tt, i, 0)),
                      pl.BlockSpec(blk, lambda s, tt, i, c: (s, tt, i, 0))],
            out_specs=pl.BlockSpec(blk, lambda s, tt, i, c: (s, tt, i, 0))),
        out_shape=jax.ShapeDtypeStruct((ns, th, r, w), BF16),
        compiler_params=_params(3 * _nbytes((bm, w), F32)),
    )(cidx, g, got)


def _chip_sum(name, p):
    ns, th, r, w = p.shape
    bm = _block_rows(r, w, budget=2 << 20)

    def body(p_ref, o_ref):
        acc = p_ref[0].astype(F32)
        for s in range(1, ns):
            acc = acc + p_ref[s].astype(F32)
        o_ref[...] = acc

    return pl.pallas_call(
        body, name=name, grid=(th, r // bm),
        in_specs=[pl.BlockSpec((ns, None, bm, w), lambda tt, i: (0, tt, i, 0))],
        out_specs=pl.BlockSpec((None, bm, w), lambda tt, i: (tt, i, 0)),
        out_shape=jax.ShapeDtypeStruct((th, r, w), F32),
        compiler_params=_params(ns * _nbytes((bm, w), BF16) + 2 * _nbytes((bm, w), F32)),
    )(p)


def _sum_leading(name, x):
    n = x.shape[0]

    def body(p_ref, o_ref):
        acc = p_ref[0]
        for s in range(1, n):
            acc = acc + p_ref[s]
        o_ref[...] = acc

    return pl.pallas_call(body, name=name, out_shape=jax.ShapeDtypeStruct(x.shape[1:], F32),
                          compiler_params=_params(2 * _nbytes(x.shape, F32)))(x)


def _reduce_scatter_groups(gs):
    cidx = lax.axis_index("c").astype(jnp.int32).reshape(1)
    got = _pair_swap_groups("rs_pair_swap", gs)
    pair = [_pair_sum(f"rs_pair_sum{i}", g, r_, cidx) for i, (g, r_) in enumerate(zip(gs, got))]
    chips = _chip_scatter_groups("rs_chip_scatter", pair)
    fin = [_chip_sum(f"rs_chip_sum{i}", p) for i, p in enumerate(chips)]
    return _pair_merge_groups("rs_pair_merge", fin)


_GROUPS = ((("ffn1_wg", "ffn1_wu", "ffn2_wg", "ffn2_wu"), 1), (("ffn1_wd", "ffn2_wd"), 0), (("w_a",), 0),
           (("w_o",), 0), (("w_in",), 1), (("w_b",), 1))


def _stack_group(per_layer, names, depth):
    return jnp.stack([per_layer[l][nm] for l in range(depth) for nm in names], axis=0)


def _shard_major(g, ax):
    k, n = g.shape
    if ax == 0:
        return g.reshape(4, k // 4, n)
    return g.reshape(k, 4, n // 4).transpose(1, 0, 2)


def _in_cols(d):
    o1 = 3 * DN_WIDTH
    o2 = o1 + DN_WIDTH
    o3 = o2 + 2 * DN_HEADS
    o4 = o3 + 3 * DA_WIDTH
    return dict(wq=(0, o1), wz=(o1, o2), wba=(o2, o3), wda=(o3, o4), wg=(o4, o4 + 2 * d))


def _mixer_weights(w_in, w_a, w_b, w_o, d):
    w = {k: w_in[:, a:b] for k, (a, b) in _in_cols(d).items()}
    w["wba"] = jnp.pad(w["wba"], ((0, 0), (0, LANES - 2 * DN_HEADS)))
    w["w_a"], w["w_b"], w["w_o"] = w_a, w_b, w_o
    return w


def _w_in_grad(wg):
    return jnp.concatenate([wg["wq"], wg["wz"], wg["wba"][:, :2 * DN_HEADS], wg["wda"], wg["wg"]], axis=1)


def _adam_math(wv, gv, mv, vv):
    mn = ADAM_B1 * mv + (1.0 - ADAM_B1) * gv
    vn = ADAM_B2 * vv + (1.0 - ADAM_B2) * jnp.square(gv)
    m_hat = mn / (1.0 - ADAM_B1 ** ADAM_STEP)
    v_hat = vn / (1.0 - ADAM_B2 ** ADAM_STEP)
    delta = -ADAM_LR * (m_hat / (jnp.sqrt(v_hat) + ADAM_EPS) + ADAM_WD * wv)
    return delta, mn, vn


def _adamw(name, w, g, m, v):
    shape = w.shape
    cols = shape[-1]
    w2, g2, m2, v2 = (t.reshape(-1, cols) for t in (w, g, m, v))
    rows = w2.shape[0]
    bm = _pick(rows, (256, 128, 64, 32, 16, 8)) if rows >= 8 else rows
    delta, mn, vn = _rowwise(name, lambda *t: (_adam_math(*t), ()), [w2, g2, m2, v2], [], [(cols, F32)] * 3, bm=bm)
    return delta.reshape(shape), mn.reshape(shape), vn.reshape(shape)


def _adamw_stacked(name, w, m, v, gstack, stride, slot):
    depth, r, cdim = w.shape
    bm = _block_rows(r, cdim, budget=1 << 20)

    def body(w_ref, g_ref, m_ref, v_ref, go_ref, d_ref, mo_ref, vo_ref):
        gv = g_ref[...]
        go_ref[...] = gv
        d_ref[...], mo_ref[...], vo_ref[...] = _adam_math(w_ref[...], gv, m_ref[...], v_ref[...])

    nat = pl.BlockSpec((None, bm, cdim), lambda l, i: (l, i, 0))
    return pl.pallas_call(
        body, name=name, grid=(depth, r // bm),
        in_specs=[nat, pl.BlockSpec((None, bm, cdim), lambda l, i: (stride * l + slot, i, 0)), nat, nat],
        out_specs=[nat] * 4, out_shape=[jax.ShapeDtypeStruct(w.shape, F32)] * 4,
        compiler_params=_params(8 * _nbytes((bm, cdim), F32)),
    )(w, gstack, m, v)


def kernel(x, c, ada_w, ada_b, ln_ffn1, ln_mix, ln_ffn2, ffn1_wg, ffn1_wu, ffn1_wd, w_in, conv_w, a_log, dt_bias, dn_norm, w_a, w_b, w_o, ffn2_wg, ffn2_wu, ffn2_wd, final_norm, loss_target, m_ada_w, m_ada_b, m_ln_ffn1, m_ln_mix, m_ln_ffn2, m_ffn1_wg, m_ffn1_wu, m_ffn1_wd, m_w_in, m_conv_w, m_a_log, m_dt_bias, m_dn_norm, m_w_a, m_w_b, m_w_o, m_ffn2_wg, m_ffn2_wu, m_ffn2_wd, m_final_norm, v_ada_w, v_ada_b, v_ln_ffn1, v_ln_mix, v_ln_ffn2, v_ffn1_wg, v_ffn1_wu, v_ffn1_wd, v_w_in, v_conv_w, v_a_log, v_dt_bias, v_dn_norm, v_w_a, v_w_b, v_w_o, v_ffn2_wg, v_ffn2_wu, v_ffn2_wd, v_final_norm):
    names = ["ada_w", "ada_b", "ln_ffn1", "ln_mix", "ln_ffn2", "ffn1_wg", "ffn1_wu", "ffn1_wd", "w_in", "conv_w",
             "a_log", "dt_bias", "dn_norm", "w_a", "w_b", "w_o", "ffn2_wg", "ffn2_wu", "ffn2_wd", "final_norm"]
    wts = dict(zip(names, (ada_w, ada_b, ln_ffn1, ln_mix, ln_ffn2, ffn1_wg, ffn1_wu, ffn1_wd, w_in, conv_w, a_log,
                           dt_bias, dn_norm, w_a, w_b, w_o, ffn2_wg, ffn2_wu, ffn2_wd, final_norm)))
    mom = dict(zip(names, (m_ada_w, m_ada_b, m_ln_ffn1, m_ln_mix, m_ln_ffn2, m_ffn1_wg, m_ffn1_wu, m_ffn1_wd, m_w_in,
                           m_conv_w, m_a_log, m_dt_bias, m_dn_norm, m_w_a, m_w_b, m_w_o, m_ffn2_wg, m_ffn2_wu,
                           m_ffn2_wd, m_final_norm)))
    var = dict(zip(names, (v_ada_w, v_ada_b, v_ln_ffn1, v_ln_mix, v_ln_ffn2, v_ffn1_wg, v_ffn1_wu, v_ffn1_wd, v_w_in,
                           v_conv_w, v_a_log, v_dt_bias, v_dn_norm, v_w_a, v_w_b, v_w_o, v_ffn2_wg, v_ffn2_wu,
                           v_ffn2_wd, v_final_norm)))
    _, s, d = x.shape
    depth = ada_w.shape[0]
    mx, my, mc = lax.axis_index("x"), lax.axis_index("y"), lax.axis_index("c")
    chip = 2 * mx + my
    me = 2 * chip + mc
    nshard = ada_w.shape[2]

    cact = _rowwise("c_silu", lambda cv: ((_silu(cv),), ()), [jnp.pad(c, ((0, 7), (0, 0)))], [], [(d, F32)], bm=8)[0]
    c_all = _allgather8("ag_c", cact)[:, 0, :]
    conv_all = _allgather8("ag_conv", jnp.pad(conv_w.reshape(depth * DN_CONV, -1), ((0, 8 - depth * DN_CONV), (0, 0))))
    conv_full = jnp.concatenate([conv_all[2 * j, :depth * DN_CONV] for j in range(4)], axis=1)
    conv_full = conv_full.reshape(depth, DN_CONV, 3 * DN_WIDTH)
    layer_shards = [[jnp.stack([wts[nm][l].astype(BF16) for nm in nms], axis=0) for nms, _ in _GROUPS]
                    for l in range(depth)]
    gathered0 = _gather_groups("ag_weights0", layer_shards[0])
    started = {l: _gather_start(f"ag_weights{l}_start", layer_shards[l]) for l in range(1, depth)}
    rows_of = lambda st: st[:, 0].reshape(-1, st.shape[-1])
    cols_of = lambda st: jnp.concatenate([st[j, 0] for j in range(4)], axis=1)

    def layer_weights(l, after):
        if l == 0:
            got = gathered0
        else:
            srcs, lands = _gather_wait(f"ag_weights{l}_wait", started[l], after)
            got = _pair_forward_groups(f"ag_weights{l}_pair", lands, srcs)
        ga, gb, g_wa, g_wo, g_win, g_wb = got
        return ga, gb, _mixer_weights(cols_of(g_win), rows_of(g_wa), cols_of(g_wb), rows_of(g_wo), d)

    c16 = jnp.pad(c_all, ((0, 8), (0, 0))).astype(BF16)
    parts = []
    for l in range(depth):
        bias = lax.dynamic_slice(ada_b[l], (chip * nshard,), (nshard,)).reshape(1, nshard)
        (mp,) = _matmul(f"ada_fwd{l}", c16, ada_w[l].astype(BF16), epi_bcast=[bias], epi=lambda acc, b: (acc + b,))
        parts.append(mp)
    mod_all = _allgather8("ag_mod", jnp.concatenate(parts, axis=0))
    mod_rows = jnp.concatenate([mod_all[2 * j] for j in range(4)], axis=1)
    mod = jnp.stack([lax.dynamic_index_in_dim(mod_rows, l * 16 + me, axis=0, keepdims=False) for l in range(depth)])

    for st in started.values():
        mod = mod + st["token"][0, 0]
    small = dict(conv_w=conv_full, a_log=a_log, dt_bias=dt_bias, dn_norm=dn_norm, ln_ffn1=ln_ffn1, ln_mix=ln_mix,
                 ln_ffn2=ln_ffn2, final_norm=final_norm)
    loss_part, dx, dmod, wgrads, sgrads, d_fnorm = _local_step(x[0], loss_target[0], mod, layer_weights, small)

    dmod_all = _allgather8("ag_dmod", jnp.pad(dmod, ((0, 8 - depth), (0, 0))))
    g_ada_w, g_ada_b = [], []
    for l in range(depth):
        dm_l = dmod_all[:, l, :]
        (gb_l,) = _rowwise(f"ada_b_grad{l}", lambda v: ((), (jnp.sum(v, axis=0, keepdims=True),)), [dm_l], [], [],
                           [(1, N_ADA * d)], bm=8)
        g_ada_b.append(gb_l[0])
        dm_sh = lax.dynamic_slice(dm_l, (0, chip * nshard), (8, nshard))
        (gw_l,) = _matmul(f"ada_w_grad{l}", c16, jnp.pad(dm_sh, ((0, 8), (0, 0))).astype(BF16), ta=True)
        g_ada_w.append(gw_l)
    grads = dict(ada_w=jnp.stack(g_ada_w), ada_b=jnp.stack(g_ada_b))

    smalls = [loss_part.reshape(1), d_fnorm]
    for l in range(depth):
        sg = sgrads[l]
        smalls += [sg["ln_ffn1"], sg["ln_mix"], sg["ln_ffn2"], sg["a_log"], sg["dt_bias"], sg["dn_norm"],
                   sg["conv_w"].reshape(-1)]
    sizes = [t.shape[0] for t in smalls]
    tile = 8 * LANES
    flat = jnp.concatenate([jnp.pad(t, (0, (-t.shape[0]) % tile)).reshape(-1, LANES) for t in smalls], axis=0)
    tot = _sum_leading("small_sum", _allgather8("ag_small", flat))
    offs, acc = [], 0
    for n_ in sizes:
        offs.append(acc)
        acc += -(-n_ // tile) * 8
    take = lambda i: tot[offs[i]:offs[i] + -(-sizes[i] // tile) * 8].reshape(-1)[:sizes[i]]
    loss = take(0)[0]
    grads["final_norm"] = take(1)
    per = 7
    for key_i, key in enumerate(["ln_ffn1", "ln_mix", "ln_ffn2", "a_log", "dt_bias", "dn_norm"]):
        grads[key] = jnp.stack([take(2 + per * l + key_i) for l in range(depth)])
    conv_g = jnp.stack([take(2 + per * l + 6).reshape(DN_CONV, 3 * DN_WIDTH) for l in range(depth)])
    csh = conv_w.shape[2]
    grads["conv_w"] = lax.dynamic_slice(conv_g, (0, 0, chip * csh), (depth, DN_CONV, csh))

    for l in range(depth):
        wgrads[l]["w_in"] = _w_in_grad(wgrads[l])
    ffn_names = _GROUPS[0][0] + _GROUPS[1][0]
    gstacks = []
    for nms, ax in _GROUPS:
        per_layer = [{nm: (wgrads[l][nm] if nm in ffn_names else _shard_major(wgrads[l][nm], ax)) for nm in nms}
                     for l in range(depth)]
        gstacks.append(jnp.stack([per_layer[l][nm] for l in range(depth) for nm in nms], axis=1))
    reduced = _reduce_scatter_groups(gstacks)
    deltas, new_m, new_v = {}, {}, {}
    for (nms, _), red in zip(_GROUPS, reduced):
        for q, nm in enumerate(nms):
            grads[nm], deltas[nm], new_m[nm], new_v[nm] = _adamw_stacked("adamw_" + nm, wts[nm], mom[nm], var[nm], red,
                                                                         len(nms), q)

    for name in names:
        if name in deltas:
            continue
        wv, gv, mv, vv = wts[name], grads[name], mom[name], var[name]
        if wv.ndim == 1:
            wv, gv, mv, vv = (t.reshape(-1, LANES) for t in (wv, gv, mv, vv))
        dl, mn, vn = _adamw("adamw_" + name, wv, gv, mv, vv)
        deltas[name], new_m[name], new_v[name] = (t.reshape(wts[name].shape) for t in (dl, mn, vn))
    return (loss, dx.reshape(1, s, d), *[grads[n_] for n_ in names], *[deltas[n_] for n_ in names],
            *[new_m[n_] for n_ in names], *[new_v[n_] for n_ in names])
```

```python
import functools

import jax
import jax.numpy as jnp
from jax import lax
from jax.experimental import pallas as pl
from jax.experimental.pallas import tpu as pltpu

F32 = jnp.float32
BF16 = jnp.bfloat16
MESH = pl.DeviceIdType.MESH

NORM_EPS = 1e-6
DN_HEADS, DN_DIM, DN_CHUNK, DN_CONV = 8, 128, 64, 4
DN_WIDTH = DN_HEADS * DN_DIM
DA_HEADS, DA_DIM, DA_BLOCK = 12, 64, 128
DA_WIDTH = DA_HEADS * DA_DIM
DA_PATTERNS = ((128, 1), (512, 4), (2048, 16))
ALIBI_MAX_EXP = 8.0
N_ADA = 9
LANES = 128
V7X_VMEM_BYTES = 64 << 20
ADAM_LR, ADAM_B1, ADAM_B2, ADAM_EPS, ADAM_WD, ADAM_STEP = 0.001, 0.9, 0.999, 1e-08, 0.01, 10
NEG = -1e30
HI = lax.Precision.HIGHEST
NN = (((1,), (0,)), ((), ()))
NT = (((1,), (1,)), ((), ()))
TN = (((0,), (0,)), ((), ()))


def _nbytes(shape, dtype):
    n = 1
    for s in shape:
        n *= s
    return n * jnp.dtype(dtype).itemsize


def _params(block_bytes, scratch_bytes=0):
    need = 2 * block_bytes + scratch_bytes
    lim = min(max(need + need // 4 + (4 << 20), 32 << 20), V7X_VMEM_BYTES - (6 << 20))
    return pltpu.CompilerParams(vmem_limit_bytes=int(lim))


def _pick(n, cands):
    for c in cands:
        if c <= n and n % c == 0:
            return c
    return n


def _sigmoid(x):
    return jax.nn.sigmoid(x)


def _silu(x):
    return x * jax.nn.sigmoid(x)


def _softplus(x):
    return jnp.maximum(x, 0.0) + jnp.log(1.0 + jnp.exp(-jnp.abs(x)))


def _rowwise(name, fn, rows, bcast, row_outs, red_outs=(), bm=256):
    rows = [r if isinstance(r, tuple) else (r, r.shape[1], 0) for r in rows]
    s = rows[0][0].shape[0]
    bm = _pick(s, (bm, 128, 64, 32, 16, 8))
    nr, nb, no, nd = len(rows), len(bcast), len(row_outs), len(red_outs)
    in_specs = [pl.BlockSpec((bm, w), functools.partial(lambda i, ci: (i, ci), ci=ci)) for (_, w, ci) in rows]
    in_specs += [pl.BlockSpec(b.shape, lambda i: (0, 0)) for b in bcast]
    out_shape = [jax.ShapeDtypeStruct((s, w), dt) for (w, dt) in row_outs]
    out_shape += [jax.ShapeDtypeStruct((r, w), F32) for (r, w) in red_outs]
    out_specs = [pl.BlockSpec((bm, w), lambda i: (i, 0)) for (w, _) in row_outs]
    out_specs += [pl.BlockSpec((r, w), lambda i: (0, 0)) for (r, w) in red_outs]

    def body(*refs):
        ins = [r[...] for r in refs[:nr + nb]]
        outs = refs[nr + nb:nr + nb + no]
        reds = refs[nr + nb + no:]
        ov, rv = fn(*ins)
        for o, v in zip(outs, ov):
            o[...] = v.astype(o.dtype)
        if nd:
            @pl.when(pl.program_id(0) == 0)
            def _():
                for r in reds:
                    r[...] = jnp.zeros(r.shape, F32)
            for r, v in zip(reds, rv):
                r[...] += v.astype(F32)

    blk = sum(_nbytes((bm, w), a.dtype) for (a, w, _) in rows) + sum(_nbytes(b.shape, b.dtype) for b in bcast)
    blk += sum(_nbytes((bm, w), dt) for (w, dt) in row_outs) + sum(_nbytes(r, F32) for r in red_outs)
    res = pl.pallas_call(
        body, name=name, grid=(s // bm,), in_specs=in_specs, out_specs=out_specs, out_shape=out_shape,
        compiler_params=_params(3 * blk),
    )(*[a for (a, _, _) in rows], *bcast)
    return res


def _matmul(name, a, b, *, ta=False, tb=False, outs=(F32,), epi=None, epi_rows=(), epi_bcast=(),
            bm=None, bn=None, bk=None):
    if ta:
        k, m = a.shape
    else:
        m, k = a.shape
    n = b.shape[0] if tb else b.shape[1]
    assert (b.shape[1] if tb else b.shape[0]) == k, (name, a.shape, b.shape)
    if bm is None:
        bm = _pick(m, (1024, 1408, 768, 512, 384, 256, 128)) if ta else _pick(m, (1024, 512, 256, 128, 64, 32, 16))
    if bn is None:
        bn = _pick(n, (512, 384, 256, 128))
    if bk is None:
        bk = k if k <= 3072 else _pick(k, (2816, 2048, 1024, 512))
        if ta:
            bk = _pick(k, (1024, 512, 256, 128, 64, 32, 16))
    nk = k // bk
    dims = TN if ta else (NT if tb else NN)
    a_spec = pl.BlockSpec((bk, bm), lambda i, j, kk: (kk, i)) if ta else pl.BlockSpec((bm, bk), lambda i, j, kk: (i, kk))
    b_spec = pl.BlockSpec((bn, bk), lambda i, j, kk: (j, kk)) if tb else pl.BlockSpec((bk, bn), lambda i, j, kk: (kk, j))
    in_specs = [a_spec, b_spec]
    in_specs += [pl.BlockSpec((bm, bn), lambda i, j, kk: (i, j)) for _ in epi_rows]
    in_specs += [pl.BlockSpec((1, bn), lambda i, j, kk: (0, j)) for _ in epi_bcast]
    out_shape = [jax.ShapeDtypeStruct((m, n), dt) for dt in outs]
    out_specs = [pl.BlockSpec((bm, bn), lambda i, j, kk: (i, j)) for _ in outs]
    ner, neb, no = len(epi_rows), len(epi_bcast), len(outs)

    def body(*refs):
        a_ref, b_ref = refs[0], refs[1]
        extra = refs[2:2 + ner + neb]
        out_refs = refs[2 + ner + neb:2 + ner + neb + no]
        prod = lax.dot_general(a_ref[...], b_ref[...], dims, preferred_element_type=F32)

        def finish(acc):
            vals = epi(acc, *[r[...] for r in extra]) if epi is not None else (acc,)
            for o, v in zip(out_refs, vals):
                o[...] = v.astype(o.dtype)

        if nk == 1:
            finish(prod)
        else:
            acc_ref = refs[-1]
            kk = pl.program_id(2)

            @pl.when(kk == 0)
            def _():
                acc_ref[...] = prod

            @pl.when(kk > 0)
            def _():
                acc_ref[...] += prod

            @pl.when(kk == nk - 1)
            def _():
                finish(acc_ref[...])

    blk = _nbytes((bm, bk), a.dtype) + _nbytes((bk, bn), b.dtype)
    blk += sum(_nbytes((bm, bn), r.dtype) for r in epi_rows) + sum(_nbytes((bm, bn), dt) for dt in outs)
    scratch = [pltpu.VMEM((bm, bn), F32)] if nk > 1 else []
    res = pl.pallas_call(
        body, name=name, grid=(m // bm, n // bn, nk), in_specs=in_specs, out_specs=out_specs,
        out_shape=out_shape, scratch_shapes=scratch,
        compiler_params=_params(blk, 3 * _nbytes((bm, bn), F32)),
    )(a, b, *epi_rows, *epi_bcast)
    return res


def _mm_core(name, grid, nk, pairs, out_defs, acc_shape, epi=None, epi_ins=()):
    npair, nep, no = len(pairs), len(epi_ins), len(out_defs)

    def body(*refs):
        extra = refs[2 * npair:2 * npair + nep]
        out_refs = refs[2 * npair + nep:2 * npair + nep + no]
        prod = None
        for p in range(npair):
            d = lax.dot_general(refs[2 * p][...], refs[2 * p + 1][...], pairs[p][4], preferred_element_type=F32)
            prod = d if prod is None else prod + d

        def finish(acc):
            vals = epi(acc, *[r[...] for r in extra]) if epi is not None else (acc,)
            for o, v in zip(out_refs, vals):
                o[...] = v.astype(o.dtype)

        if nk == 1:
            finish(prod)
        else:
            acc_ref = refs[-1]
            kk = pl.program_id(2)

            @pl.when(kk == 0)
            def _():
                acc_ref[...] = prod

            @pl.when(kk > 0)
            def _():
                acc_ref[...] += prod

            @pl.when(kk == nk - 1)
            def _():
                finish(acc_ref[...])

    def blk_bytes(spec, dtype):
        return _nbytes([s for s in spec.block_shape if s is not None], dtype)

    blk = sum(blk_bytes(sa, a.dtype) + blk_bytes(sb, b.dtype) for (a, sa, b, sb, _) in pairs)
    blk += sum(blk_bytes(sp, arr.dtype) for (arr, sp) in epi_ins) + sum(blk_bytes(sp, dt) for (_, dt, sp) in out_defs)
    ins, in_specs = [], []
    for (a, sa, b, sb, _) in pairs:
        ins += [a, b]
        in_specs += [sa, sb]
    ins += [arr for (arr, _) in epi_ins]
    in_specs += [sp for (_, sp) in epi_ins]
    return pl.pallas_call(
        body, name=name, grid=grid, in_specs=in_specs, out_specs=[sp for (_, _, sp) in out_defs],
        out_shape=[jax.ShapeDtypeStruct(sh, dt) for (sh, dt, _) in out_defs],
        scratch_shapes=[pltpu.VMEM(acc_shape, F32)] if nk > 1 else [],
        compiler_params=_params(blk, 3 * _nbytes(acc_shape, F32)),
    )(*ins)


def _rms_mod(h, ln, sh, sc):
    n = h * lax.rsqrt(jnp.mean(h * h, axis=-1, keepdims=True) + NORM_EPS) * ln
    return n * (1.0 + sc) + sh


def _swiglu_act(g, u):
    return _silu(g.astype(F32)) * u.astype(F32)


def _dn_prep(yc, pba, alog, dtb):
    act = _silu(yc)
    parts = []
    for idx in range(2 * DN_HEADS):
        seg = act[:, idx * DN_DIM:(idx + 1) * DN_DIM]
        seg = seg * lax.rsqrt(jnp.sum(seg * seg, axis=-1, keepdims=True) + NORM_EPS)
        if idx < DN_HEADS:
            seg = seg * (DN_DIM ** -0.5)
        parts.append(seg)
    parts.append(act[:, 2 * DN_WIDTH:])
    qkvn = jnp.concatenate(parts, axis=1)
    lane = lax.broadcasted_iota(jnp.int32, pba.shape, 1)
    beta = _sigmoid(pba)
    g = -jnp.exp(alog) * _softplus(pba + dtb)
    gb = jnp.where(lane < DN_HEADS, beta, jnp.where(lane < 2 * DN_HEADS, g, 0.0))
    return qkvn, gb


def _dn_outnorm(o_a, z, dn):
    parts = []
    for h in range(DN_HEADS):
        seg = o_a[:, h * DN_DIM:(h + 1) * DN_DIM]
        seg = seg * lax.rsqrt(jnp.mean(seg * seg, axis=-1, keepdims=True) + NORM_EPS) * dn
        parts.append(seg)
    return jnp.concatenate(parts, axis=1) * _silu(z)


def _shift_down(x, halo8, s):
    r = pltpu.roll(x, s, axis=0)
    top = pltpu.roll(halo8, s, axis=0)
    i8 = lax.broadcasted_iota(jnp.int32, top.shape, 0)
    return jnp.concatenate([jnp.where(i8 < s, top, r[0:8]), r[8:]], axis=0)


def _shift_up(x, halo8, s):
    m = x.shape[0]
    r = pltpu.roll(x, m - s, axis=0)
    bot = pltpu.roll(halo8, 8 - s, axis=0)
    i8 = lax.broadcasted_iota(jnp.int32, bot.shape, 0)
    return jnp.concatenate([r[:m - 8], jnp.where(i8 >= 8 - s, bot, r[m - 8:])], axis=0)


def _conv_prep_fwd(name, pq, convw8, pba, alog, dtb, bm=256):
    s, w = pq.shape
    nblk = s // bm
    hb = bm // 16

    def body(x_ref, halo_ref, w_ref, pba_ref, alog_ref, dtb_ref, yc_ref, qkv_ref, gb_ref):
        i = pl.program_id(0)
        x = x_ref[...].astype(F32)
        halo = jnp.where(i > 0, halo_ref[...].astype(F32)[8:16], 0.0)
        cw = w_ref[...]
        y = x * cw[DN_CONV - 1:DN_CONV]
        for sft in range(1, DN_CONV):
            y = y + _shift_down(x, halo, sft) * cw[DN_CONV - 1 - sft:DN_CONV - sft]
        ycb = y.astype(BF16)
        yc_ref[...] = ycb
        qkvn, gb = _dn_prep(ycb.astype(F32), pba_ref[...], alog_ref[...], dtb_ref[...])
        qkv_ref[...] = qkvn.astype(BF16)
        gb_ref[...] = gb

    blk = 3 * _nbytes((bm, w), BF16) + 4 * _nbytes((bm, w), F32)
    return pl.pallas_call(
        body, name=name, grid=(nblk,),
        in_specs=[pl.BlockSpec((bm, w), lambda i: (i, 0)),
                  pl.BlockSpec((16, w), lambda i: (jnp.maximum(i * hb - 1, 0), 0)),
                  pl.BlockSpec(convw8.shape, lambda i: (0, 0)),
                  pl.BlockSpec((bm, LANES), lambda i: (i, 0)),
                  pl.BlockSpec((1, LANES), lambda i: (0, 0)),
                  pl.BlockSpec((1, LANES), lambda i: (0, 0))],
        out_specs=[pl.BlockSpec((bm, w), lambda i: (i, 0)), pl.BlockSpec((bm, w), lambda i: (i, 0)),
                   pl.BlockSpec((bm, LANES), lambda i: (i, 0))],
        out_shape=[jax.ShapeDtypeStruct((s, w), BF16), jax.ShapeDtypeStruct((s, w), BF16),
                   jax.ShapeDtypeStruct((s, LANES), F32)],
        compiler_params=_params(blk),
    )(pq, pq, convw8, pba, alog, dtb)


def _conv_bwd(name, dyc, pq, convw8, bm=256):
    s, w = pq.shape
    nblk = s // bm
    hb = bm // 16

    def body(dy_ref, dyn_ref, x_ref, xh_ref, w_ref, dx_ref, dw_ref):
        i = pl.program_id(0)
        dy = dy_ref[...].astype(F32)
        nxt = jnp.where(i < nblk - 1, dyn_ref[...].astype(F32)[0:8], 0.0)
        x = x_ref[...].astype(F32)
        halo = jnp.where(i > 0, xh_ref[...].astype(F32)[8:16], 0.0)
        cw = w_ref[...]
        dx = dy * cw[DN_CONV - 1:DN_CONV]
        for sft in range(1, DN_CONV):
            dx = dx + _shift_up(dy, nxt, sft) * cw[DN_CONV - 1 - sft:DN_CONV - sft]
        dx_ref[...] = dx.astype(dx_ref.dtype)
        r8 = lax.broadcasted_iota(jnp.int32, (8, w), 0)
        dw = jnp.zeros((8, w), F32)
        for j in range(DN_CONV):
            sft = DN_CONV - 1 - j
            xs = x if sft == 0 else _shift_down(x, halo, sft)
            dw = dw + jnp.where(r8 == j, jnp.sum(dy * xs, axis=0, keepdims=True), 0.0)

        @pl.when(i == 0)
        def _():
            dw_ref[...] = jnp.zeros((8, w), F32)
        dw_ref[...] += dw

    blk = 4 * _nbytes((bm, w), BF16) + 5 * _nbytes((bm, w), F32)
    return pl.pallas_call(
        body, name=name, grid=(nblk,),
        in_specs=[pl.BlockSpec((bm, w), lambda i: (i, 0)),
                  pl.BlockSpec((16, w), lambda i: (jnp.minimum((i + 1) * hb, s // 16 - 1), 0)),
                  pl.BlockSpec((bm, w), lambda i: (i, 0)),
                  pl.BlockSpec((16, w), lambda i: (jnp.maximum(i * hb - 1, 0), 0)),
                  pl.BlockSpec(convw8.shape, lambda i: (0, 0))],
        out_specs=[pl.BlockSpec((bm, w), lambda i: (i, 0)), pl.BlockSpec((8, w), lambda i: (0, 0))],
        out_shape=[jax.ShapeDtypeStruct((s, w), BF16), jax.ShapeDtypeStruct((8, w), F32)],
        compiler_params=_params(blk),
    )(dyc, dyc, pq, pq, convw8)


BNN = (((2,), (1,)), ((0,), (0,)))
BNT = (((2,), (2,)), ((0,), (0,)))
BTN = (((1,), (1,)), ((0,), (0,)))


def _raw_dot_1pass(a, b, dims):
    return lax.dot_general(a.astype(BF16), b.astype(BF16), dims, preferred_element_type=F32)


def _raw_dot_3pass(a, b, dims):
    ah = a.astype(BF16)
    al = (a - ah.astype(F32)).astype(BF16)
    bh = b.astype(BF16)
    bl = (b - bh.astype(F32)).astype(BF16)
    d = lambda x, y: lax.dot_general(x, y, dims, preferred_element_type=F32)
    return d(ah, bh) + (d(ah, bl) + d(al, bh))


def _with_same_precision_vjp(raw):
    @functools.partial(jax.custom_vjp, nondiff_argnums=(2,))
    def dot(a, b, dims):
        return raw(a, b, dims)

    def fwd(a, b, dims):
        return raw(a, b, dims), (a, b)

    def bwd(dims, res, ct):
        a, b = res
        if dims == BNN:
            return raw(ct, b, BNT), raw(a, ct, BTN)
        if dims == BNT:
            return raw(ct, b, BNN), raw(ct, a, BTN)
        assert dims == BTN
        return raw(b, ct, BNT), raw(a, ct, BNN)

    dot.defvjp(fwd, bwd)
    return dot


_dot_1pass_vjp = _with_same_precision_vjp(_raw_dot_1pass)
_dot_3pass_vjp = _with_same_precision_vjp(_raw_dot_3pass)


def _dot_bf16(a, b, dims=BNN):
    return _dot_1pass_vjp(a, b, dims)


def _dot_3pass(a, b, dims=BNN):
    return _dot_3pass_vjp(a, b, dims)


def _neumann_inverse(x):
    h, c, _ = x.shape
    eye = lax.broadcasted_iota(jnp.int32, (h, c, c), 1) == lax.broadcasted_iota(jnp.int32, (h, c, c), 2)
    t = jnp.where(eye, 1.0, 0.0) + x
    p = x
    for _ in range(5):
        p = _raw_dot_3pass(p, p, BNN)
        t = t + _raw_dot_3pass(t, p, BNN)
    return t


@jax.custom_vjp
def _known_inverse(x, t):
    return t


def _known_inverse_fwd(x, t):
    return t, t


def _known_inverse_bwd(t, ct):
    return _raw_dot_3pass(_raw_dot_3pass(t, ct, BTN), t, BNT), jnp.zeros_like(t)


_known_inverse.defvjp(_known_inverse_fwd, _known_inverse_bwd)


def _delta_chunk(q, k, v, gcol, bcol, state, t_known=None):
    h, c, _ = q.shape
    row = lax.broadcasted_iota(jnp.int32, (h, c, c), 1)
    col = lax.broadcasted_iota(jnp.int32, (h, c, c), 2)
    incl, strict, eye = row >= col, row > col, row == col
    g_b = jnp.broadcast_to(gcol, (h, c, c))
    gc_row = jnp.sum(jnp.where(row <= col, g_b, 0.0), axis=1, keepdims=True)
    g_r = jnp.sum(jnp.where(eye, g_b, 0.0), axis=1, keepdims=True)
    gc_col = jnp.sum(jnp.where(incl, jnp.broadcast_to(g_r, (h, c, c)), 0.0), axis=2, keepdims=True)
    decay = jnp.exp(jnp.where(incl, gc_col - gc_row, NEG))
    kb = k * bcol
    vb = v * bcol
    x = -jnp.where(strict, _dot_bf16(kb, k, BNT) * decay, 0.0)
    t = _neumann_inverse(x) if t_known is None else _known_inverse(x, t_known)
    eg = jnp.exp(gc_col)
    u = _dot_3pass(t, vb)
    w = _dot_3pass(t, kb * eg)
    qk = _dot_bf16(q, k, BNT) * decay
    v_new = u - _dot_bf16(w, state)
    o = _dot_bf16(q * eg, state) + _dot_bf16(qk, v_new)
    g_last = jnp.sum(g_r, axis=2, keepdims=True)
    new_state = state * jnp.exp(g_last) + _dot_bf16(k * jnp.exp(g_last - gc_col), v_new, BTN)
    return o, new_state, t


def _lane_col(blk, idx):
    lane = lax.broadcasted_iota(jnp.int32, blk.shape, 1)
    return jnp.sum(jnp.where(lane == idx, blk, 0.0), axis=1, keepdims=True)


def _dn_heads(ref, base):
    return jnp.stack([ref[:, base + h * DN_DIM:base + (h + 1) * DN_DIM] for h in range(DN_HEADS)], axis=0).astype(F32)


def _dn_cols(gbv, base):
    return jnp.stack([_lane_col(gbv, base + h) for h in range(DN_HEADS)], axis=0)


def _delta_fwd(name, qkvn, gb):
    s = qkvn.shape[0]
    n = s // DN_CHUNK
    c = DN_CHUNK

    def body(qkv_ref, gb_ref, o_ref, st_ref, t_ref, state):
        @pl.when(pl.program_id(0) == 0)
        def _():
            state[...] = jnp.zeros(state.shape, F32)

        gbv = gb_ref[...]
        st = state[...]
        st_ref[0] = st
        o, new, t = _delta_chunk(_dn_heads(qkv_ref, 0), _dn_heads(qkv_ref, DN_WIDTH), _dn_heads(qkv_ref, 2 * DN_WIDTH),
                                 _dn_cols(gbv, DN_HEADS), _dn_cols(gbv, 0), st)
        for h in range(DN_HEADS):
            o_ref[:, h * DN_DIM:(h + 1) * DN_DIM] = o[h]
        t_ref[0] = t
        state[...] = new

    blk = _nbytes((c, 3 * DN_WIDTH), BF16) + _nbytes((c, LANES), F32) + _nbytes((c, DN_WIDTH), F32)
    blk += _nbytes((DN_HEADS, DN_DIM, DN_DIM), F32) + _nbytes((DN_HEADS, c, c), F32)
    return pl.pallas_call(
        body, name=name, grid=(n,),
        in_specs=[pl.BlockSpec((c, 3 * DN_WIDTH), lambda i: (i, 0)), pl.BlockSpec((c, LANES), lambda i: (i, 0))],
        out_specs=[pl.BlockSpec((c, DN_WIDTH), lambda i: (i, 0)),
                   pl.BlockSpec((1, DN_HEADS, DN_DIM, DN_DIM), lambda i: (i, 0, 0, 0)),
                   pl.BlockSpec((1, DN_HEADS, c, c), lambda i: (i, 0, 0, 0))],
        out_shape=[jax.ShapeDtypeStruct((s, DN_WIDTH), F32),
                   jax.ShapeDtypeStruct((n, DN_HEADS, DN_DIM, DN_DIM), F32),
                   jax.ShapeDtypeStruct((n, DN_HEADS, c, c), F32)],
        scratch_shapes=[pltpu.VMEM((DN_HEADS, DN_DIM, DN_DIM), F32)],
        compiler_params=_params(blk, 8 << 20),
    )(qkvn, gb)


def _delta_bwd(name, qkvn, gb, states, tinv, d_o):
    s = qkvn.shape[0]
    n = s // DN_CHUNK
    c = DN_CHUNK

    def body(qkv_ref, gb_ref, st_ref, t_ref, do_ref, dqkv_ref, dgb_ref, dstate):
        @pl.when(pl.program_id(0) == 0)
        def _():
            dstate[...] = jnp.zeros(dstate.shape, F32)

        gbv = gb_ref[...]
        lane = lax.broadcasted_iota(jnp.int32, (c, LANES), 1)
        t_known = t_ref[0]
        chunk = lambda *args: _delta_chunk(*args, t_known=t_known)[:2]
        _, vjp = jax.vjp(chunk, _dn_heads(qkv_ref, 0), _dn_heads(qkv_ref, DN_WIDTH),
                         _dn_heads(qkv_ref, 2 * DN_WIDTH), _dn_cols(gbv, DN_HEADS), _dn_cols(gbv, 0), st_ref[0])
        dq, dk, dv, dg, db, dst = vjp((_dn_heads(do_ref, 0), dstate[...]))
        dgb = jnp.zeros((c, LANES), F32)
        for h in range(DN_HEADS):
            dqkv_ref[:, h * DN_DIM:(h + 1) * DN_DIM] = dq[h]
            dqkv_ref[:, DN_WIDTH + h * DN_DIM:DN_WIDTH + (h + 1) * DN_DIM] = dk[h]
            dqkv_ref[:, 2 * DN_WIDTH + h * DN_DIM:2 * DN_WIDTH + (h + 1) * DN_DIM] = dv[h]
            dgb = dgb + jnp.where(lane == h, db[h], 0.0) + jnp.where(lane == DN_HEADS + h, dg[h], 0.0)
        dstate[...] = dst
        dgb_ref[...] = dgb

    rev = lambda i: (n - 1 - i, 0)
    blk = _nbytes((c, 3 * DN_WIDTH), BF16) + 2 * _nbytes((c, LANES), F32) + _nbytes((c, DN_WIDTH), F32)
    blk += _nbytes((DN_HEADS, DN_DIM, DN_DIM), F32) + _nbytes((c, 3 * DN_WIDTH), F32)
    return pl.pallas_call(
        body, name=name, grid=(n,),
        in_specs=[pl.BlockSpec((c, 3 * DN_WIDTH), rev), pl.BlockSpec((c, LANES), rev),
                  pl.BlockSpec((1, DN_HEADS, DN_DIM, DN_DIM), lambda i: (n - 1 - i, 0, 0, 0)),
                  pl.BlockSpec((1, DN_HEADS, c, c), lambda i: (n - 1 - i, 0, 0, 0)),
                  pl.BlockSpec((c, DN_WIDTH), rev)],
        out_specs=[pl.BlockSpec((c, 3 * DN_WIDTH), rev), pl.BlockSpec((c, LANES), rev)],
        out_shape=[jax.ShapeDtypeStruct((s, 3 * DN_WIDTH), F32), jax.ShapeDtypeStruct((s, LANES), F32)],
        scratch_shapes=[pltpu.VMEM((DN_HEADS, DN_DIM, DN_DIM), F32)],
        compiler_params=_params(blk, 16 << 20),
    )(qkvn, gb, states, tinv, d_o)


def _da_scores(q2f, k2, sub, valid, distf, head):
    lane = lax.broadcasted_iota(jnp.int32, q2f.shape, 1)
    hmask = (lane < DA_DIM) if sub == 0 else (lane >= DA_DIM)
    qm = jnp.where(hmask, q2f, 0.0).astype(BF16)
    slope = 2.0 ** (-ALIBI_MAX_EXP * (head + 1) / DA_HEADS)
    sc = lax.dot_general(qm, k2, NT, preferred_element_type=F32) * (DA_DIM ** -0.5)
    return jnp.where(valid, sc - slope * distf, NEG), qm, hmask


def _da_mask(i, r):
    qi = lax.broadcasted_iota(jnp.int32, (DA_BLOCK, 2 * DA_BLOCK), 0)
    ki = lax.broadcasted_iota(jnp.int32, (DA_BLOCK, 2 * DA_BLOCK), 1)
    dist = qi + DA_BLOCK - ki
    valid = (dist >= 0) & (dist <= DA_BLOCK) & ((ki >= DA_BLOCK) | (i > 0))
    return valid, (dist * r).astype(F32)


def _da_fwd(name, pda, r):
    s = pda.shape[0]
    n = s // r
    nb = n // DA_BLOCK
    w = DA_WIDTH
    dav = pda.reshape(n, r * 3 * w)

    def body(q_ref, kc_ref, kp_ref, vc_ref, vp_ref, o_ref, lse_ref):
        i = pl.program_id(1)
        valid, distf = _da_mask(i, r)
        lane = lax.broadcasted_iota(jnp.int32, (DA_BLOCK, LANES), 1)
        lse = jnp.zeros((DA_BLOCK, LANES), F32)
        for hp in range(DA_HEADS // 2):
            sl = slice(hp * LANES, (hp + 1) * LANES)
            q2f = q_ref[:, sl].astype(F32)
            k2 = jnp.concatenate([kp_ref[:, sl], kc_ref[:, sl]], axis=0)
            v2 = jnp.concatenate([vp_ref[:, sl], vc_ref[:, sl]], axis=0)
            o2 = None
            for sub in range(2):
                head = 2 * hp + sub
                sc, _, hmask = _da_scores(q2f, k2, sub, valid, distf, head)
                mx = jnp.max(sc, axis=1, keepdims=True)
                p = jnp.exp(sc - mx)
                l = jnp.sum(p, axis=1, keepdims=True)
                pv = lax.dot_general(p.astype(BF16), v2, NN, preferred_element_type=F32) / l
                o2 = pv if sub == 0 else jnp.where(hmask, pv, o2)
                lse = jnp.where(lane == head, mx + jnp.log(l), lse)
            o_ref[:, sl] = o2.astype(o_ref.dtype)
        lse_ref[...] = lse

    prev = lambda col: (lambda p, i: (jnp.maximum(i - 1, 0), 3 * p + col))
    cur = lambda col: (lambda p, i: (i, 3 * p + col))
    blk = 5 * _nbytes((DA_BLOCK, w), BF16) + _nbytes((DA_BLOCK, w), F32) + _nbytes((DA_BLOCK, LANES), F32)
    o, lse = pl.pallas_call(
        body, name=name, grid=(r, nb),
        in_specs=[pl.BlockSpec((DA_BLOCK, w), cur(0)), pl.BlockSpec((DA_BLOCK, w), cur(1)),
                  pl.BlockSpec((DA_BLOCK, w), prev(1)), pl.BlockSpec((DA_BLOCK, w), cur(2)),
                  pl.BlockSpec((DA_BLOCK, w), prev(2))],
        out_specs=[pl.BlockSpec((DA_BLOCK, w), lambda p, i: (i, p)),
                   pl.BlockSpec((DA_BLOCK, LANES), lambda p, i: (i, p))],
        out_shape=[jax.ShapeDtypeStruct((n, r * w), BF16), jax.ShapeDtypeStruct((n, r * LANES), F32)],
        compiler_params=_params(blk, 8 << 20),
    )(dav, dav, dav, dav, dav)
    return o.reshape(s, w), lse.reshape(s, LANES)


def _da_bwd(name, pda, d_ob, lse_tot, delta, r):
    s = pda.shape[0]
    n = s // r
    nb = n // DA_BLOCK
    w = DA_WIDTH
    dav = pda.reshape(n, r * 3 * w)
    dov = d_ob.reshape(n, r * w)
    lv = lse_tot.reshape(n, r * LANES)
    dlv = delta.reshape(n, r * LANES)

    def body(q_ref, kc_ref, kp_ref, vc_ref, vp_ref, do_ref, l_ref, dl_ref, dq_ref, dk_ref, dv_ref, ck, cv):
        i = pl.program_id(1)

        @pl.when(i == 0)
        def _():
            ck[...] = jnp.zeros(ck.shape, F32)
            cv[...] = jnp.zeros(cv.shape, F32)

        @pl.when(i < nb)
        def _():
            valid, distf = _da_mask(i, r)
            lsev = l_ref[...]
            dlt = dl_ref[...]
            for hp in range(DA_HEADS // 2):
                sl = slice(hp * LANES, (hp + 1) * LANES)
                q2f = q_ref[:, sl].astype(F32)
                k2 = jnp.concatenate([kp_ref[:, sl], kc_ref[:, sl]], axis=0)
                v2 = jnp.concatenate([vp_ref[:, sl], vc_ref[:, sl]], axis=0)
                do2f = do_ref[:, sl].astype(F32)
                dq2 = jnp.zeros((DA_BLOCK, LANES), F32)
                dk2 = jnp.zeros((2 * DA_BLOCK, LANES), F32)
                dv2 = jnp.zeros((2 * DA_BLOCK, LANES), F32)
                for sub in range(2):
                    head = 2 * hp + sub
                    sc, qm, hmask = _da_scores(q2f, k2, sub, valid, distf, head)
                    p = jnp.exp(sc - _lane_col(lsev, head))
                    dom = jnp.where(hmask, do2f, 0.0).astype(BF16)
                    dp = lax.dot_general(dom, v2, NT, preferred_element_type=F32)
                    ds = (p * (dp - _lane_col(dlt, head)) * (DA_DIM ** -0.5)).astype(BF16)
                    dq2 = dq2 + jnp.where(hmask, lax.dot_general(ds, k2, NN, preferred_element_type=F32), 0.0)
                    dk2 = dk2 + lax.dot_general(ds, qm, TN, preferred_element_type=F32)
                    dv2 = dv2 + lax.dot_general(p.astype(BF16), dom, TN, preferred_element_type=F32)
                dq_ref[:, sl] = dq2.astype(dq_ref.dtype)
                dk_ref[:, sl] = (ck[:, sl] + dk2[:DA_BLOCK]).astype(dk_ref.dtype)
                dv_ref[:, sl] = (cv[:, sl] + dv2[:DA_BLOCK]).astype(dv_ref.dtype)
                ck[:, sl] = dk2[DA_BLOCK:]
                cv[:, sl] = dv2[DA_BLOCK:]

        @pl.when(i == nb)
        def _():
            dk_ref[...] = ck[...].astype(dk_ref.dtype)
            dv_ref[...] = cv[...].astype(dv_ref.dtype)

    qrow = lambda i: jnp.minimum(i, nb - 1)
    prev = lambda col: (lambda p, i: (jnp.maximum(qrow(i) - 1, 0), 3 * p + col))
    cur = lambda col: (lambda p, i: (qrow(i), 3 * p + col))
    same = lambda p, i: (qrow(i), p)
    late = lambda p, i: (jnp.maximum(i - 1, 0), p)
    blk = 6 * _nbytes((DA_BLOCK, w), BF16) + 2 * _nbytes((DA_BLOCK, LANES), F32) + 3 * _nbytes((DA_BLOCK, w), F32)
    dq, dk, dv = pl.pallas_call(
        body, name=name, grid=(r, nb + 1),
        in_specs=[pl.BlockSpec((DA_BLOCK, w), cur(0)), pl.BlockSpec((DA_BLOCK, w), cur(1)),
                  pl.BlockSpec((DA_BLOCK, w), prev(1)), pl.BlockSpec((DA_BLOCK, w), cur(2)),
                  pl.BlockSpec((DA_BLOCK, w), prev(2)), pl.BlockSpec((DA_BLOCK, w), same),
                  pl.BlockSpec((DA_BLOCK, LANES), same), pl.BlockSpec((DA_BLOCK, LANES), same)],
        out_specs=[pl.BlockSpec((DA_BLOCK, w), same), pl.BlockSpec((DA_BLOCK, w), late),
                   pl.BlockSpec((DA_BLOCK, w), late)],
        out_shape=[jax.ShapeDtypeStruct((n, r * w), BF16)] * 3,
        scratch_shapes=[pltpu.VMEM((DA_BLOCK, w), F32), pltpu.VMEM((DA_BLOCK, w), F32)],
        compiler_params=_params(blk, 12 << 20),
    )(dav, dav, dav, dav, dav, dov, lv, dlv)
    return dq.reshape(s, w), dk.reshape(s, w), dv.reshape(s, w)


def _head_expand():
    hrow = lax.broadcasted_iota(jnp.int32, (LANES, DA_WIDTH), 0)
    lcol = lax.broadcasted_iota(jnp.int32, (LANES, DA_WIDTH), 1)
    return jnp.where(lcol // DA_DIM == hrow, 1.0, 0.0).astype(F32)


def _ffn_up(name, a, ga, tg, tu):
    s, d = a.shape
    nsh, _, _, ffs = ga.shape
    bm = _pick(s, (1024, 512, 256, 128))

    def body(a_ref, wg_ref, wu_ref, g_ref, u_ref, f_ref):
        av = a_ref[...]
        g = lax.dot_general(av, wg_ref[...], NN, preferred_element_type=F32)
        u = lax.dot_general(av, wu_ref[...], NN, preferred_element_type=F32)
        g_ref[...] = g.astype(BF16)
        u_ref[...] = u.astype(BF16)
        f_ref[...] = (_silu(g) * u).astype(BF16)

    wspec = lambda t: pl.BlockSpec((None, None, d, ffs), lambda i, j: (j, t, 0, 0))
    ospec = pl.BlockSpec((None, bm, ffs), lambda i, j: (j, i, 0))
    blk = _nbytes((bm, d), BF16) + 2 * _nbytes((d, ffs), BF16) + 3 * _nbytes((bm, ffs), BF16)
    return pl.pallas_call(
        body, name=name, grid=(s // bm, nsh),
        in_specs=[pl.BlockSpec((bm, d), lambda i, j: (i, 0)), wspec(tg), wspec(tu)],
        out_specs=[ospec] * 3, out_shape=[jax.ShapeDtypeStruct((nsh, s, ffs), BF16)] * 3,
        compiler_params=_params(blk, 4 * _nbytes((bm, ffs), F32)),
    )(a, ga, ga)


def _ffn_fwd(tag, h_in, ln, sh, sc, gt, ga, tg, tu, gb, td, weight):
    s, d = h_in.shape
    nsh, _, ffs, _ = gb.shape
    (a,) = _rowwise(tag + "_norm", lambda h, l, s1, s2: ((_rms_mod(h, l, s1, s2),), ()), [h_in], [ln, sh, sc],
                    [(d, BF16)])
    g, u, f = _ffn_up(tag + "_up", a, ga, tg, tu)
    bm, bn = _pick(s, (1024, 512, 256, 128)), _pick(d, (512, 256, 128))
    io = pl.BlockSpec((bm, bn), lambda i, j, kk: (i, j))
    h_out, o = _mm_core(
        tag + "_down", (s // bm, d // bn, nsh), nsh,
        [(f, pl.BlockSpec((None, bm, ffs), lambda i, j, kk: (kk, i, 0)),
          gb, pl.BlockSpec((None, None, ffs, bn), lambda i, j, kk: (kk, td, 0, j)), NN)],
        [((s, d), F32, io), ((s, d), BF16, io)], (bm, bn),
        epi=lambda acc, h, gv: (h + weight * gv * acc, acc),
        epi_ins=[(h_in, io), (gt, pl.BlockSpec((1, bn), lambda i, j, kk: (0, j)))])
    return h_out, dict(a=a, g=g, u=u, f=f, o=o)


def _resid_bwd(tag, dh_out, o, gt, weight):
    d = dh_out.shape[1]

    def fn(dh, ov, g):
        return (weight * g * dh,), (jnp.sum(weight * dh * ov.astype(F32), axis=0, keepdims=True),)

    do, d_gt = _rowwise(tag + "_resid_bwd", fn, [dh_out, o], [gt], [(d, BF16)], [(1, d)])
    return do, d_gt


def _norm_bwd(tag, h_in, da, dh_out, ln, sh, sc):
    d = h_in.shape[1]

    def fn(h, dav, dh, l, s1, s2):
        _, vjp = jax.vjp(_rms_mod, h, l, s1, s2)
        gh, gl, gs1, gs2 = vjp(dav)
        return (dh + gh,), (gl, gs1, gs2)

    return _rowwise(tag + "_norm_bwd", fn, [h_in, da, dh_out], [ln, sh, sc], [(d, F32)], [(1, d)] * 3)


def _ffn_bwd(tag, h_in, dh_out, sv, ln, sh, sc, gt, ga, tg, tu, gb, td, weight):
    s, d = h_in.shape
    nsh, _, ffs, _ = gb.shape
    bm, bn = _pick(s, (1024, 512, 256, 128)), _pick(d, (512, 256, 128))
    bk = _pick(s, (1024, 512, 256, 128))
    do, d_gt = _resid_bwd(tag, dh_out, sv["o"], gt, weight)

    def act_bwd(df, g, u):
        _, vjp = jax.vjp(_swiglu_act, g, u)
        return vjp(df)

    hid = pl.BlockSpec((None, bm, ffs), lambda i, j, kk: (j, i, 0))
    dg, du = _mm_core(
        tag + "_down_dx", (s // bm, nsh, 1), 1,
        [(do, pl.BlockSpec((bm, d), lambda i, j, kk: (i, 0)),
          gb, pl.BlockSpec((None, None, ffs, d), lambda i, j, kk: (j, td, 0, 0)), NT)],
        [((nsh, s, ffs), BF16, hid)] * 2, (bm, ffs), epi=act_bwd, epi_ins=[(sv["g"], hid), (sv["u"], hid)])
    (d_wd,) = _mm_core(
        tag + "_down_dw", (nsh, d // bn, s // bk), s // bk,
        [(sv["f"], pl.BlockSpec((None, bk, ffs), lambda i, j, kk: (i, kk, 0)),
          do, pl.BlockSpec((bk, bn), lambda i, j, kk: (kk, j)), TN)],
        [((nsh, ffs, d), BF16, pl.BlockSpec((None, ffs, bn), lambda i, j, kk: (i, 0, j)))], (ffs, bn))
    kmaj = pl.BlockSpec((None, bm, ffs), lambda i, j, kk: (kk, i, 0))
    wsp = lambda t: pl.BlockSpec((None, None, bn, ffs), functools.partial(lambda i, j, kk, t: (kk, t, j, 0), t=t))
    (da,) = _mm_core(
        tag + "_up_dx", (s // bm, d // bn, nsh), nsh, [(dg, kmaj, ga, wsp(tg), NT), (du, kmaj, ga, wsp(tu), NT)],
        [((s, d), F32, pl.BlockSpec((bm, bn), lambda i, j, kk: (i, j)))], (bm, bn))
    dws = []
    for nm, dh in (("_wg_dw", dg), ("_wu_dw", du)):
        (dw,) = _mm_core(
            tag + nm, (1, nsh, s // bk), s // bk,
            [(sv["a"], pl.BlockSpec((bk, d), lambda i, j, kk: (kk, 0)),
              dh, pl.BlockSpec((None, bk, ffs), lambda i, j, kk: (j, kk, 0)), TN)],
            [((nsh, d, ffs), BF16, pl.BlockSpec((None, d, ffs), lambda i, j, kk: (j, 0, 0)))], (d, ffs))
        dws.append(dw)
    dh_in, d_ln, d_sh, d_sc = _norm_bwd(tag, h_in, da, dh_out, ln, sh, sc)
    return dh_in, dict(wg=dws[0], wu=dws[1], wd=d_wd), dict(ln=d_ln, sh=d_sh, sc=d_sc, gt=d_gt)


def _mixer_fwd(tag, h_in, ln, sh, sc, gt, w, sp):
    d = h_in.shape[1]
    (a,) = _rowwise(tag + "_norm", lambda h, l, s1, s2: ((_rms_mod(h, l, s1, s2),), ()), [h_in], [ln, sh, sc],
                    [(d, BF16)])
    (pq,) = _matmul(tag + "_pq", a, w["wq"], outs=(BF16,))
    (pz,) = _matmul(tag + "_pz", a, w["wz"], outs=(BF16,))
    (pba,) = _matmul(tag + "_pba", a, w["wba"])
    (pda,) = _matmul(tag + "_pda", a, w["wda"], outs=(BF16,))
    (pg,) = _matmul(tag + "_pg", a, w["wg"], outs=(BF16,))
    yc, qkvn, gb = _conv_prep_fwd(tag + "_conv", pq, sp["conv8"], pba, sp["alog"], sp["dtb"])
    o_a, states, tinv = _delta_fwd(tag + "_delta", qkvn, gb)
    (o_an,) = _rowwise(tag + "_dnorm", lambda o, z, dn: ((_dn_outnorm(o, z.astype(F32), dn),), ()), [o_a, pz],
                       [sp["dn"]], [(DN_WIDTH, BF16)])
    ops, lses = [], []
    for (_, r) in DA_PATTERNS:
        o_p, lse_p = _da_fwd(f"{tag}_da{r}", pda, r)
        ops.append(o_p)
        lses.append(lse_p)

    def merge(o1, o2, o3, l1, l2, l3):
        mx = jnp.maximum(jnp.maximum(l1, l2), l3)
        e1, e2, e3 = jnp.exp(l1 - mx), jnp.exp(l2 - mx), jnp.exp(l3 - mx)
        tot = e1 + e2 + e3
        ex = _head_expand()
        up = lambda wgt: lax.dot_general(wgt / tot, ex, NN, precision=HI, preferred_element_type=F32)
        return (up(e1) * o1 + up(e2) * o2 + up(e3) * o3, mx + jnp.log(tot)), ()

    o_b, lse_tot = _rowwise(tag + "_merge", merge, ops + lses, [], [(DA_WIDTH, BF16), (LANES, F32)])
    (y_a,) = _matmul(tag + "_wa", o_an, w["w_a"], outs=(BF16,))
    (y_b,) = _matmul(tag + "_wb", o_b, w["w_b"], outs=(BF16,))

    def gate(ga, gbv, ya, yb):
        return _sigmoid(ga.astype(F32)) * ya.astype(F32) + _sigmoid(gbv.astype(F32)) * yb.astype(F32)

    (merged,) = _rowwise(tag + "_gate", lambda *v: ((gate(*v),), ()), [(pg, d, 0), (pg, d, 1), y_a, y_b], [],
                         [(d, BF16)])
    h_out, m = _matmul(tag + "_wo", merged, w["w_o"], outs=(F32, BF16), epi_rows=[h_in], epi_bcast=[gt],
                       epi=lambda acc, h, g: (h + g * acc, acc))
    sv = dict(a=a, pq=pq, pz=pz, pba=pba, pda=pda, pg=pg, yc=yc, qkvn=qkvn, gb=gb, o_a=o_a, states=states, tinv=tinv,
              o_an=o_an, o_b=o_b, lse=lse_tot, y_a=y_a, y_b=y_b, merged=merged, m=m, gate=gate)
    return h_out, sv


def _mixer_bwd(tag, h_in, dh_out, sv, ln, sh, sc, gt, w, sp):
    d = h_in.shape[1]
    dm, d_gt = _resid_bwd(tag, dh_out, sv["m"], gt, 1.0)
    (d_merged,) = _matmul(tag + "_wo_dx", dm, w["w_o"], tb=True, outs=(BF16,))
    (d_wo,) = _matmul(tag + "_wo_dw", sv["merged"], dm, ta=True, outs=(BF16,))
    gate = sv["gate"]

    def gate_bwd(dmg, ga, gbv, ya, yb):
        _, vjp = jax.vjp(gate, ga.astype(F32), gbv.astype(F32), ya.astype(F32), yb.astype(F32))
        dga, dgb, dya, dyb = vjp(dmg.astype(F32))
        return (jnp.concatenate([dga, dgb], axis=1), dya, dyb), ()

    pg = sv["pg"]
    d_pg, d_ya, d_yb = _rowwise(tag + "_gate_bwd", gate_bwd, [d_merged, (pg, d, 0), (pg, d, 1), sv["y_a"], sv["y_b"]],
                                [], [(2 * d, BF16), (d, BF16), (d, BF16)])
    (d_oan,) = _matmul(tag + "_wa_dx", d_ya, w["w_a"], tb=True)
    (d_wa,) = _matmul(tag + "_wa_dw", sv["o_an"], d_ya, ta=True, outs=(BF16,))
    (d_ob,) = _matmul(tag + "_wb_dx", d_yb, w["w_b"], tb=True, outs=(BF16,))
    (d_wb,) = _matmul(tag + "_wb_dw", sv["o_b"], d_yb, ta=True, outs=(BF16,))

    def dnorm_bwd(doan, o, z, dn):
        _, vjp = jax.vjp(_dn_outnorm, o, z.astype(F32), dn)
        go, gz, gdn = vjp(doan)
        return (go, gz), (gdn,)

    d_oa, d_pz, d_dn = _rowwise(tag + "_dnorm_bwd", dnorm_bwd, [d_oan, sv["o_a"], sv["pz"]], [sp["dn"]],
                                [(DN_WIDTH, F32), (DN_WIDTH, BF16)], [(1, DN_DIM)])
    d_qkvn, d_gb = _delta_bwd(tag + "_delta_bwd", sv["qkvn"], sv["gb"], sv["states"], sv["tinv"], d_oa)

    def prep_bwd(dq, dgbv, yc, pba, alog, dtb):
        _, vjp = jax.vjp(_dn_prep, yc.astype(F32), pba, alog, dtb)
        gyc, gpba, galog, gdtb = vjp((dq, dgbv))
        return (gyc, gpba), (galog, gdtb)

    d_yc, d_pba, d_alog, d_dtb = _rowwise(tag + "_prep_bwd", prep_bwd, [d_qkvn, d_gb, sv["yc"], sv["pba"]],
                                          [sp["alog"], sp["dtb"]], [(3 * DN_WIDTH, BF16), (LANES, BF16)],
                                          [(1, LANES), (1, LANES)], bm=128)
    d_pq, d_conv = _conv_bwd(tag + "_conv_bwd", d_yc, sv["pq"], sp["conv8"])

    def delta_fn(dob, ob):
        prod = dob.astype(F32) * ob.astype(F32)
        return (lax.dot_general(prod, _head_expand(), NT, precision=HI, preferred_element_type=F32),), ()

    (delta,) = _rowwise(tag + "_da_delta", delta_fn, [d_ob, sv["o_b"]], [], [(LANES, F32)])
    grads = [_da_bwd(f"{tag}_da{r}_bwd", sv["pda"], d_ob, sv["lse"], delta, r) for (_, r) in DA_PATTERNS]

    def sum3(*parts):
        q1, k1, v1, q2, k2, v2, q3, k3, v3 = (p.astype(F32) for p in parts)
        return (jnp.concatenate([q1 + q2 + q3, k1 + k2 + k3, v1 + v2 + v3], axis=1),), ()

    (d_pda,) = _rowwise(tag + "_da_sum", sum3, [t for g in grads for t in g], [], [(3 * DA_WIDTH, BF16)])

    a = sv["a"]
    (da,) = _matmul(tag + "_pq_dx", d_pq, w["wq"], tb=True)
    add = lambda acc, prev: (acc + prev,)
    (da,) = _matmul(tag + "_pz_dx", d_pz, w["wz"], tb=True, epi_rows=[da], epi=add)
    (da,) = _matmul(tag + "_pba_dx", d_pba, w["wba"], tb=True, epi_rows=[da], epi=add)
    (da,) = _matmul(tag + "_pda_dx", d_pda, w["wda"], tb=True, epi_rows=[da], epi=add)
    (da,) = _matmul(tag + "_pg_dx", d_pg, w["wg"], tb=True, epi_rows=[da], epi=add)
    (d_wq,) = _matmul(tag + "_pq_dw", a, d_pq, ta=True, outs=(BF16,))
    (d_wz,) = _matmul(tag + "_pz_dw", a, d_pz, ta=True, outs=(BF16,))
    (d_wba,) = _matmul(tag + "_pba_dw", a, d_pba, ta=True, outs=(BF16,))
    (d_wda,) = _matmul(tag + "_pda_dw", a, d_pda, ta=True, outs=(BF16,))
    (d_wg,) = _matmul(tag + "_pg_dw", a, d_pg, ta=True, outs=(BF16,))
    dh_in, d_ln, d_sh, d_sc = _norm_bwd(tag, h_in, da, dh_out, ln, sh, sc)
    wgrads = dict(wq=d_wq, wz=d_wz, wba=d_wba, wda=d_wda, wg=d_wg, w_a=d_wa, w_b=d_wb, w_o=d_wo)
    small = dict(ln=d_ln, sh=d_sh, sc=d_sc, gt=d_gt, dn=d_dn, alog=d_alog, dtb=d_dtb, conv=d_conv)
    return dh_in, wgrads, small


def _loss_head(h, target, fnorm):
    d = h.shape[1]

    def fn(hv, tv, fw):
        def lossf(hh, ww):
            y = hh * lax.rsqrt(jnp.mean(hh * hh, axis=-1, keepdims=True) + NORM_EPS) * ww
            return 0.5 * jnp.sum(jnp.mean(jnp.square(y - tv), axis=-1))

        val, (dh, dw) = jax.value_and_grad(lossf, argnums=(0, 1))(hv, fw)
        return (dh,), (jnp.full((1, LANES), val, F32), dw)

    return _rowwise("loss_head", fn, [h, target], [fnorm], [(d, F32)], [(1, LANES), (1, d)])


def _row(v):
    return v.reshape(1, -1)


def _pad_lanes(v, offset):
    return jnp.pad(v.reshape(1, -1), ((0, 0), (offset, LANES - offset - v.shape[0])))


_UP_SLOTS = dict(ffn1_wg=0, ffn1_wu=1, ffn2_wg=2, ffn2_wu=3)
_DOWN_SLOTS = dict(ffn1_wd=0, ffn2_wd=1)


def _local_step(x2, target, mod, layer_weights, small, on_layer_grads):
    depth = mod.shape[0]
    d = x2.shape[1]
    h = x2
    saved = []
    mods = []
    up = lambda l, nm: _UP_SLOTS[nm]
    down = lambda l, nm: _DOWN_SLOTS[nm]
    for l in range(depth):
        m9 = [_row(mod[l, i * d:(i + 1) * d]) for i in range(N_ADA)]
        sp = dict(conv8=jnp.pad(small["conv_w"][l], ((0, 8 - DN_CONV), (0, 0))),
                  alog=_pad_lanes(small["a_log"][l], DN_HEADS), dtb=_pad_lanes(small["dt_bias"][l], DN_HEADS),
                  dn=_row(small["dn_norm"][l]))
        ga, gb, w = layer_weights(l, h)
        h0 = h
        h1, sv1 = _ffn_fwd(f"l{l}_ffn1", h0, _row(small["ln_ffn1"][l]), m9[0], m9[1], m9[2], ga, up(l, "ffn1_wg"),
                           up(l, "ffn1_wu"), gb, down(l, "ffn1_wd"), 0.5)
        h2, sv2 = _mixer_fwd(f"l{l}_mix", h1, _row(small["ln_mix"][l]), m9[3], m9[4], m9[5], w, sp)
        h3, sv3 = _ffn_fwd(f"l{l}_ffn2", h2, _row(small["ln_ffn2"][l]), m9[6], m9[7], m9[8], ga, up(l, "ffn2_wg"),
                           up(l, "ffn2_wu"), gb, down(l, "ffn2_wd"), 0.5)
        saved.append((h0, h1, h2, sv1, sv2, sv3, sp, ga, gb, w))
        mods.append(m9)
        h = h3
    dh, loss_part, d_fnorm = _loss_head(h, target, _row(small["final_norm"]))
    sgrads, dmods = [], []
    token = None
    for l in reversed(range(depth)):
        h0, h1, h2, sv1, sv2, sv3, sp, ga, gb, w = saved[l]
        m9 = mods[l] if token is None else [r + token for r in mods[l]]
        dh, g3, s3 = _ffn_bwd(f"l{l}_ffn2", h2, dh, sv3, _row(small["ln_ffn2"][l]), m9[6], m9[7], m9[8], ga,
                              up(l, "ffn2_wg"), up(l, "ffn2_wu"), gb, down(l, "ffn2_wd"), 0.5)
        dh, g2, s2 = _mixer_bwd(f"l{l}_mix", h1, dh, sv2, _row(small["ln_mix"][l]), m9[3], m9[4], m9[5], w, sp)
        dh, g1, s1 = _ffn_bwd(f"l{l}_ffn1", h0, dh, sv1, _row(small["ln_ffn1"][l]), m9[0], m9[1], m9[2], ga,
                              up(l, "ffn1_wg"), up(l, "ffn1_wu"), gb, down(l, "ffn1_wd"), 0.5)
        token = on_layer_grads(l, dict(ffn1_wg=g1["wg"], ffn1_wu=g1["wu"], ffn1_wd=g1["wd"], ffn2_wg=g3["wg"],
                                       ffn2_wu=g3["wu"], ffn2_wd=g3["wd"], **g2))
        dmods.append(jnp.concatenate([s1["sh"], s1["sc"], s1["gt"], s2["sh"], s2["sc"], s2["gt"],
                                      s3["sh"], s3["sc"], s3["gt"]], axis=1))
        sgrads.append(dict(ln_ffn1=s1["ln"][0], ln_mix=s2["ln"][0], ln_ffn2=s3["ln"][0],
                           a_log=s2["alog"][0, DN_HEADS:2 * DN_HEADS], dt_bias=s2["dtb"][0, DN_HEADS:2 * DN_HEADS],
                           dn_norm=s2["dn"][0], conv_w=s2["conv"][:DN_CONV]))
    sgrads.reverse()
    dmods.reverse()
    return loss_part[0, 0], dh, jnp.concatenate(dmods, axis=0), sgrads, d_fnorm[0]


def _flip(v, bit):
    return 1 - v if bit else v


def _allgather8(name, x):
    r, c = x.shape

    def body(x_ref, out_ref, send_sems, recv_sems, local_sem):
        mx, my, mc = lax.axis_index("x"), lax.axis_index("y"), lax.axis_index("c")
        me = 4 * mx + 2 * my + mc
        mine = pltpu.make_async_copy(x_ref, out_ref.at[me], local_sem)
        mine.start()
        sends = []
        for k in range(1, 8):
            peer = (_flip(mx, k & 4), _flip(my, k & 2), _flip(mc, k & 1))
            cp = pltpu.make_async_remote_copy(src_ref=x_ref, dst_ref=out_ref.at[me], send_sem=send_sems.at[k - 1],
                                              recv_sem=recv_sems.at[k - 1], device_id=peer, device_id_type=MESH)
            cp.start()
            sends.append(cp)
        for k in range(1, 8):
            peer = (_flip(mx, k & 4), _flip(my, k & 2), _flip(mc, k & 1))
            src = 4 * peer[0] + 2 * peer[1] + peer[2]
            pltpu.make_async_remote_copy(src_ref=x_ref, dst_ref=out_ref.at[src], send_sem=send_sems.at[k - 1],
                                         recv_sem=recv_sems.at[k - 1], device_id=peer, device_id_type=MESH).wait_recv()
        for cp in sends:
            cp.wait_send()
        mine.wait()

    return pl.pallas_call(
        body, name=name, out_shape=jax.ShapeDtypeStruct((8, r, c), x.dtype),
        in_specs=[pl.BlockSpec(memory_space=pltpu.VMEM)], out_specs=pl.BlockSpec(memory_space=pltpu.VMEM),
        scratch_shapes=[pltpu.SemaphoreType.DMA((7,)), pltpu.SemaphoreType.DMA((7,)), pltpu.SemaphoreType.DMA],
        compiler_params=_params(9 * _nbytes((r, c), x.dtype)),
    )(x)


def _chip_peers(mx, my):
    chips = [(1 - mx, my), (mx, 1 - my), (1 - mx, 1 - my)]
    return chips, [2 * cx + cy for (cx, cy) in chips]


_ANY = pl.BlockSpec(memory_space=pl.ANY)


def _row_half(mc, r):
    return pl.ds(pl.multiple_of(mc * (r // 2), 16), r // 2)


def _gather_groups(name, shards):
    ng = len(shards)

    def body(*refs):
        xs, outs = refs[:ng], refs[ng:2 * ng]
        send_sems, recv_sems = refs[2 * ng:]
        mx, my, mc = lax.axis_index("x"), lax.axis_index("y"), lax.axis_index("c")
        j = 2 * mx + my
        chips, idxs = _chip_peers(mx, my)
        sib = (mx, my, 1 - mc)

        def copy(k, src, dst, to):
            return pltpu.make_async_remote_copy(src_ref=src, dst_ref=dst, send_sem=send_sems.at[k],
                                                recv_sem=recv_sems.at[k], device_id=to, device_id_type=MESH)

        first, passed = [], []
        for g in range(ng):
            mine = _row_half(mc, shards[g].shape[1])
            for t, chip in enumerate(chips):
                cp = copy(6 * g + t, xs[g].at[:, mine], outs[g].at[j, :, mine], (*chip, mc))
                cp.start()
                first.append(cp)
        for g in range(ng):
            mine = _row_half(mc, shards[g].shape[1])
            for t, chip in enumerate(chips):
                landed = outs[g].at[idxs[t], :, mine]
                copy(6 * g + t, landed, landed, (*chip, mc)).wait_recv()
                fwd = copy(6 * g + 3 + t, landed, landed, sib)
                fwd.start()
                passed.append(fwd)
        for g in range(ng):
            theirs_half = _row_half(1 - mc, shards[g].shape[1])
            for t in range(3):
                theirs = outs[g].at[idxs[t], :, theirs_half]
                copy(6 * g + 3 + t, theirs, theirs, sib).wait_recv()
        for cp in first + passed:
            cp.wait_send()

    outs = pl.pallas_call(
        body, name=name, out_shape=[jax.ShapeDtypeStruct((4,) + x.shape, x.dtype) for x in shards],
        in_specs=[_ANY] * ng, out_specs=[_ANY] * ng,
        scratch_shapes=[pltpu.SemaphoreType.DMA((6 * ng,)), pltpu.SemaphoreType.DMA((6 * ng,))],
    )(*shards)
    return _place_own_slab(outs, shards)


def _place_own_slab(outs, shards):
    chip = 2 * lax.axis_index("x") + lax.axis_index("y")
    return [lax.dynamic_update_slice(o, x[None], (chip,) + (0,) * x.ndim) for o, x in zip(outs, shards)]


_HBM = pl.BlockSpec(memory_space=pltpu.HBM)
_SEM = pl.BlockSpec(memory_space=pltpu.SEMAPHORE)
_DATAFLOW = pltpu.SideEffectType.DATAFLOW_SIDE_EFFECTING


def _ici_gather_copies(src_refs, land_refs, send_sems, recv_sems, scatter=False):
    mx, my, mc = lax.axis_index("x"), lax.axis_index("y"), lax.axis_index("c")
    j = 2 * mx + my
    chips, idxs = _chip_peers(mx, my)
    sends, recvs = [], []
    for g, src in enumerate(src_refs):
        for t, chip in enumerate(chips):
            common = dict(send_sem=send_sems.at[3 * g + t], recv_sem=recv_sems.at[3 * g + t], device_id=(*chip, mc),
                          device_id_type=MESH)
            if scatter:
                out, to, frm = src.at[idxs[t]], land_refs[g].at[j], land_refs[g].at[idxs[t]]
            else:
                mine = _row_half(mc, src.shape[1])
                out, to, frm = src.at[:, mine], land_refs[g].at[j, :, mine], land_refs[g].at[idxs[t], :, mine]
            sends.append(pltpu.make_async_remote_copy(src_ref=out, dst_ref=to, **common))
            recvs.append(pltpu.make_async_remote_copy(src_ref=out, dst_ref=frm, **common))
    return sends, recvs


def _gather_start(name, shards, scatter=False):
    ng = len(shards)

    def body(*refs):
        srcs, lands = refs[:ng], refs[ng:2 * ng]
        send_sems, recv_sems = refs[2 * ng], refs[2 * ng + 1]
        token = refs[-1]
        sends, _ = _ici_gather_copies(srcs, lands, send_sems, recv_sems, scatter)
        for cp in sends:
            cp.start()
        token[...] = jnp.zeros(token.shape, token.dtype)

    land_shape = lambda x: x.shape if scatter else (4,) + x.shape
    srcs = [pltpu.with_memory_space_constraint(x, pltpu.HBM) for x in shards]
    lands = [pltpu.with_memory_space_constraint(lax.empty(land_shape(x), x.dtype), pltpu.HBM) for x in shards]
    res = pl.pallas_call(
        body, name=name,
        out_shape=(pltpu.SemaphoreType.DMA((3 * ng,)), pltpu.SemaphoreType.DMA((3 * ng,)),
                   *[pltpu.HBM(x.shape, x.dtype) for x in srcs], *[pltpu.HBM(x.shape, x.dtype) for x in lands],
                   jax.ShapeDtypeStruct((8, LANES), F32)),
        in_specs=[_HBM] * (2 * ng),
        out_specs=(_SEM, _SEM, *[_HBM] * (2 * ng), pl.BlockSpec(memory_space=pltpu.VMEM)),
        input_output_aliases={i: 2 + i for i in range(2 * ng)},
        compiler_params=pltpu.CompilerParams(has_side_effects=_DATAFLOW),
    )(*srcs, *lands)
    return dict(send_sems=res[0], recv_sems=res[1], srcs=list(res[2:2 + ng]), lands=list(res[2 + ng:2 + 2 * ng]),
                token=res[-1])


def _gather_wait(name, started, after, scatter=False):
    ng = len(started["srcs"])

    def body(*refs):
        srcs, lands = refs[:ng], refs[ng:2 * ng]
        send_sems, recv_sems = refs[2 * ng], refs[2 * ng + 1]
        sends, recvs = _ici_gather_copies(srcs, lands, send_sems, recv_sems, scatter)
        for cp in sends:
            cp.wait_send()
        for cp in recvs:
            cp.wait_recv()

    res = pl.pallas_call(
        body, name=name,
        out_shape=[pltpu.HBM(x.shape, x.dtype) for x in started["srcs"] + started["lands"]],
        in_specs=[_HBM] * (2 * ng) + [_SEM, _SEM, _ANY], out_specs=[_HBM] * (2 * ng),
        input_output_aliases={i: i for i in range(2 * ng)},
        compiler_params=pltpu.CompilerParams(has_side_effects=_DATAFLOW),
    )(*started["srcs"], *started["lands"], started["send_sems"], started["recv_sems"], after)
    return list(res[:ng]), list(res[ng:])


def _pair_forward_groups(name, lands, shards):
    ng = len(lands)

    def body(*refs):
        ins, outs = refs[:ng], refs[ng:2 * ng]
        send_sems, recv_sems = refs[2 * ng:]
        mx, my, mc = lax.axis_index("x"), lax.axis_index("y"), lax.axis_index("c")
        _, idxs = _chip_peers(mx, my)
        sib = (mx, my, 1 - mc)
        cps = []
        for g in range(ng):
            mine = _row_half(mc, lands[g].shape[2])
            for t in range(3):
                cp = pltpu.make_async_remote_copy(src_ref=ins[g].at[idxs[t], :, mine], dst_ref=outs[g].at[idxs[t], :, mine],
                                                  send_sem=send_sems.at[3 * g + t], recv_sem=recv_sems.at[3 * g + t],
                                                  device_id=sib, device_id_type=MESH)
                cp.start()
                cps.append(cp)
        for g in range(ng):
            theirs = _row_half(1 - mc, lands[g].shape[2])
            for t in range(3):
                pltpu.make_async_remote_copy(src_ref=ins[g].at[idxs[t], :, theirs], dst_ref=outs[g].at[idxs[t], :, theirs],
                                             send_sem=send_sems.at[3 * g + t], recv_sem=recv_sems.at[3 * g + t],
                                             device_id=sib, device_id_type=MESH).wait_recv()
        for cp in cps:
            cp.wait_send()

    outs = pl.pallas_call(
        body, name=name, out_shape=[jax.ShapeDtypeStruct(x.shape, x.dtype) for x in lands],
        in_specs=[_ANY] * ng, out_specs=[_ANY] * ng, input_output_aliases={i: i for i in range(ng)},
        scratch_shapes=[pltpu.SemaphoreType.DMA((3 * ng,)), pltpu.SemaphoreType.DMA((3 * ng,))],
    )(*lands)
    return _place_own_slab(outs, shards)


def _pair_swap_groups(name, gs):
    ng = len(gs)

    def body(*refs):
        xs, outs = refs[:ng], refs[ng:2 * ng]
        send_sems, recv_sems = refs[2 * ng:]
        mx, my, mc = lax.axis_index("x"), lax.axis_index("y"), lax.axis_index("c")
        cps = []
        for g in range(ng):
            cp = pltpu.make_async_remote_copy(src_ref=xs[g].at[:, :, _row_half(1 - mc, gs[g].shape[2])], dst_ref=outs[g],
                                              send_sem=send_sems.at[g], recv_sem=recv_sems.at[g],
                                              device_id=(mx, my, 1 - mc), device_id_type=MESH)
            cp.start()
            cps.append(cp)
        for cp in cps:
            cp.wait()

    return pl.pallas_call(
        body, name=name,
        out_shape=[jax.ShapeDtypeStruct(x.shape[:2] + (x.shape[2] // 2, x.shape[3]), x.dtype) for x in gs],
        in_specs=[_ANY] * ng, out_specs=[_ANY] * ng,
        scratch_shapes=[pltpu.SemaphoreType.DMA((ng,)), pltpu.SemaphoreType.DMA((ng,))],
    )(*gs)


def _chip_scatter_groups(name, ps):
    ng = len(ps)

    def body(*refs):
        xs, outs = refs[:ng], refs[ng:2 * ng]
        send_sems, recv_sems = refs[2 * ng:]
        mx, my, mc = lax.axis_index("x"), lax.axis_index("y"), lax.axis_index("c")
        j = 2 * mx + my
        chips, idxs = _chip_peers(mx, my)
        sends = []
        for g in range(ng):
            for t, chip in enumerate(chips):
                cp = pltpu.make_async_remote_copy(src_ref=xs[g].at[idxs[t]], dst_ref=outs[g].at[j],
                                                  send_sem=send_sems.at[3 * g + t], recv_sem=recv_sems.at[3 * g + t],
                                                  device_id=(*chip, mc), device_id_type=MESH)
                cp.start()
                sends.append(cp)
        for g in range(ng):
            for t, chip in enumerate(chips):
                pltpu.make_async_remote_copy(src_ref=xs[g].at[idxs[t]], dst_ref=outs[g].at[idxs[t]],
                                             send_sem=send_sems.at[3 * g + t], recv_sem=recv_sems.at[3 * g + t],
                                             device_id=(*chip, mc), device_id_type=MESH).wait_recv()
        for cp in sends:
            cp.wait_send()

    outs = pl.pallas_call(
        body, name=name, out_shape=[jax.ShapeDtypeStruct(x.shape, x.dtype) for x in ps],
        in_specs=[_ANY] * ng, out_specs=[_ANY] * ng,
        scratch_shapes=[pltpu.SemaphoreType.DMA((3 * ng,)), pltpu.SemaphoreType.DMA((3 * ng,))],
    )(*ps)
    return _place_own_part(outs, ps)


def _place_own_part(outs, ps):
    chip = 2 * lax.axis_index("x") + lax.axis_index("y")
    return [lax.dynamic_update_slice(o, lax.dynamic_index_in_dim(x, chip, 0, keepdims=True), (chip,) + (0,) * (x.ndim - 1))
            for o, x in zip(outs, ps)]


def _pair_merge_groups(name, fs):
    ng = len(fs)

    def body(*refs):
        xs, outs = refs[:ng], refs[ng:2 * ng]
        send_sems, recv_sems = refs[2 * ng:]
        mx, my, mc = lax.axis_index("x"), lax.axis_index("y"), lax.axis_index("c")
        cps = []
        for g in range(ng):
            mine = _row_half(mc, 2 * fs[g].shape[1])
            cp = pltpu.make_async_remote_copy(src_ref=xs[g], dst_ref=outs[g].at[:, mine], send_sem=send_sems.at[g],
                                              recv_sem=recv_sems.at[g], device_id=(mx, my, 1 - mc), device_id_type=MESH)
            cp.start()
            cps.append(cp)
        for g in range(ng):
            theirs = outs[g].at[:, _row_half(1 - mc, 2 * fs[g].shape[1])]
            pltpu.make_async_remote_copy(src_ref=xs[g], dst_ref=theirs, send_sem=send_sems.at[g],
                                         recv_sem=recv_sems.at[g], device_id=(mx, my, 1 - mc),
                                         device_id_type=MESH).wait_recv()
        for cp in cps:
            cp.wait_send()

    outs = pl.pallas_call(
        body, name=name,
        out_shape=[jax.ShapeDtypeStruct((x.shape[0], 2 * x.shape[1], x.shape[2]), x.dtype) for x in fs],
        in_specs=[_ANY] * ng, out_specs=[_ANY] * ng,
        scratch_shapes=[pltpu.SemaphoreType.DMA((ng,)), pltpu.SemaphoreType.DMA((ng,))],
    )(*fs)
    mc = lax.axis_index("c")
    return [lax.dynamic_update_slice(o, x, (0, mc * x.shape[1], 0)) for o, x in zip(outs, fs)]


def _block_rows(r, w, itemsize=4, budget=4 << 20):
    for c in (r, 2048, 1024, 512, 256, 128, 64, 32, 16):
        if c <= r and r % c == 0 and c * w * itemsize <= budget:
            return c
    return r


def _pair_sum(name, g, got, cidx):
    ns, t, r, w = g.shape
    rh = r // 2
    bm = _block_rows(rh, w)
    nb = rh // bm

    def body(c_ref, a_ref, b_ref, o_ref):
        o_ref[...] = (a_ref[...].astype(F32) + b_ref[...].astype(F32)).astype(o_ref.dtype)

    blk = (None, None, bm, w)
    return pl.pallas_call(
        body, name=name,
        grid_spec=pltpu.PrefetchScalarGridSpec(
            num_scalar_prefetch=1, grid=(ns, t, nb),
            in_specs=[pl.BlockSpec(blk, lambda s, tt, i, c: (s, tt, c[0] * nb + i, 0)),
                      pl.BlockSpec(blk, lambda s, tt, i, c: (s, tt, i, 0))],
            out_specs=pl.BlockSpec(blk, lambda s, tt, i, c: (s, tt, i, 0))),
        out_shape=jax.ShapeDtypeStruct((ns, t, rh, w), BF16),
        compiler_params=_params(3 * _nbytes((bm, w), F32)),
    )(cidx, g, got)


def _chip_sum(name, p):
    ns, th, r, w = p.shape
    bm = _block_rows(r, w, budget=2 << 20)

    def body(p_ref, o_ref):
        acc = p_ref[0].astype(F32)
        for s in range(1, ns):
            acc = acc + p_ref[s].astype(F32)
        o_ref[...] = acc

    return pl.pallas_call(
        body, name=name, grid=(th, r // bm),
        in_specs=[pl.BlockSpec((ns, None, bm, w), lambda tt, i: (0, tt, i, 0))],
        out_specs=pl.BlockSpec((None, bm, w), lambda tt, i: (tt, i, 0)),
        out_shape=jax.ShapeDtypeStruct((th, r, w), F32),
        compiler_params=_params(ns * _nbytes((bm, w), BF16) + 2 * _nbytes((bm, w), F32)),
    )(p)


def _sum_leading(name, x):
    n = x.shape[0]

    def body(p_ref, o_ref):
        acc = p_ref[0]
        for s in range(1, n):
            acc = acc + p_ref[s]
        o_ref[...] = acc

    return pl.pallas_call(body, name=name, out_shape=jax.ShapeDtypeStruct(x.shape[1:], F32),
                          compiler_params=_params(2 * _nbytes(x.shape, F32)))(x)


def _reduce_scatter_begin(tag, gs, overlap):
    cidx = lax.axis_index("c").astype(jnp.int32).reshape(1)
    got = _pair_swap_groups(tag + "_pair_swap", gs)
    pair = [_pair_sum(f"{tag}_pair_sum{i}", g, r_, cidx) for i, (g, r_) in enumerate(zip(gs, got))]
    if overlap:
        return _gather_start(tag + "_scatter_start", pair, scatter=True)
    return _chip_scatter_groups(tag + "_chip_scatter", pair)


def _reduce_scatter_end(tag, state, overlap, after):
    if overlap:
        srcs, lands = _gather_wait(tag + "_scatter_wait", state, after, scatter=True)
        state = _place_own_part(lands, srcs)
    fin = [_chip_sum(f"{tag}_chip_sum{i}", p) for i, p in enumerate(state)]
    return _pair_merge_groups(tag + "_pair_merge", fin)


_GROUPS = ((("ffn1_wg", "ffn1_wu", "ffn2_wg", "ffn2_wu"), 1), (("ffn1_wd", "ffn2_wd"), 0), (("w_a",), 0),
           (("w_o",), 0), (("w_in",), 1), (("w_b",), 1))


def _shard_major(g, ax):
    k, n = g.shape
    if ax == 0:
        return g.reshape(4, k // 4, n)
    return g.reshape(k, 4, n // 4).transpose(1, 0, 2)


def _in_cols(d):
    o1 = 3 * DN_WIDTH
    o2 = o1 + DN_WIDTH
    o3 = o2 + 2 * DN_HEADS
    o4 = o3 + 3 * DA_WIDTH
    return dict(wq=(0, o1), wz=(o1, o2), wba=(o2, o3), wda=(o3, o4), wg=(o4, o4 + 2 * d))


def _mixer_weights(w_in, w_a, w_b, w_o, d):
    w = {k: w_in[:, a:b] for k, (a, b) in _in_cols(d).items()}
    w["wba"] = jnp.pad(w["wba"], ((0, 0), (0, LANES - 2 * DN_HEADS)))
    w["w_a"], w["w_b"], w["w_o"] = w_a, w_b, w_o
    return w


def _w_in_grad(wg):
    return jnp.concatenate([wg["wq"], wg["wz"], wg["wba"][:, :2 * DN_HEADS], wg["wda"], wg["wg"]], axis=1)


def _adam_math(wv, gv, mv, vv):
    mn = ADAM_B1 * mv + (1.0 - ADAM_B1) * gv
    vn = ADAM_B2 * vv + (1.0 - ADAM_B2) * jnp.square(gv)
    m_hat = mn / (1.0 - ADAM_B1 ** ADAM_STEP)
    v_hat = vn / (1.0 - ADAM_B2 ** ADAM_STEP)
    delta = -ADAM_LR * (m_hat / (jnp.sqrt(v_hat) + ADAM_EPS) + ADAM_WD * wv)
    return delta, mn, vn


def _adamw(name, w, g, m, v):
    shape = w.shape
    cols = shape[-1]
    w2, g2, m2, v2 = (t.reshape(-1, cols) for t in (w, g, m, v))
    rows = w2.shape[0]
    bm = _pick(rows, (256, 128, 64, 32, 16, 8)) if rows >= 8 else rows
    delta, mn, vn = _rowwise(name, lambda *t: (_adam_math(*t), ()), [w2, g2, m2, v2], [], [(cols, F32)] * 3, bm=bm)
    return delta.reshape(shape), mn.reshape(shape), vn.reshape(shape)


def _adamw_stacked(name, w, m, v, gstacks, slot):
    depth, r, cdim = w.shape
    bm = _block_rows(r, cdim, budget=1 << 20)

    def body(w_ref, m_ref, v_ref, *rest):
        g_refs, (go_ref, d_ref, mo_ref, vo_ref) = rest[:depth], rest[depth:]
        layer = pl.program_id(0)
        gv = g_refs[0][...]
        for l in range(1, depth):
            gv = jnp.where(layer == l, g_refs[l][...], gv)
        go_ref[...] = gv
        d_ref[...], mo_ref[...], vo_ref[...] = _adam_math(w_ref[...], gv, m_ref[...], v_ref[...])

    nat = pl.BlockSpec((None, bm, cdim), lambda l, i: (l, i, 0))
    return pl.pallas_call(
        body, name=name, grid=(depth, r // bm),
        in_specs=[nat, nat, nat] + [pl.BlockSpec((None, bm, cdim), lambda l, i: (slot, i, 0))] * depth,
        out_specs=[nat] * 4, out_shape=[jax.ShapeDtypeStruct(w.shape, F32)] * 4,
        compiler_params=_params((7 + depth) * _nbytes((bm, cdim), F32)),
    )(w, m, v, *gstacks)


def kernel(x, c, ada_w, ada_b, ln_ffn1, ln_mix, ln_ffn2, ffn1_wg, ffn1_wu, ffn1_wd, w_in, conv_w, a_log, dt_bias, dn_norm, w_a, w_b, w_o, ffn2_wg, ffn2_wu, ffn2_wd, final_norm, loss_target, m_ada_w, m_ada_b, m_ln_ffn1, m_ln_mix, m_ln_ffn2, m_ffn1_wg, m_ffn1_wu, m_ffn1_wd, m_w_in, m_conv_w, m_a_log, m_dt_bias, m_dn_norm, m_w_a, m_w_b, m_w_o, m_ffn2_wg, m_ffn2_wu, m_ffn2_wd, m_final_norm, v_ada_w, v_ada_b, v_ln_ffn1, v_ln_mix, v_ln_ffn2, v_ffn1_wg, v_ffn1_wu, v_ffn1_wd, v_w_in, v_conv_w, v_a_log, v_dt_bias, v_dn_norm, v_w_a, v_w_b, v_w_o, v_ffn2_wg, v_ffn2_wu, v_ffn2_wd, v_final_norm):
    names = ["ada_w", "ada_b", "ln_ffn1", "ln_mix", "ln_ffn2", "ffn1_wg", "ffn1_wu", "ffn1_wd", "w_in", "conv_w",
             "a_log", "dt_bias", "dn_norm", "w_a", "w_b", "w_o", "ffn2_wg", "ffn2_wu", "ffn2_wd", "final_norm"]
    wts = dict(zip(names, (ada_w, ada_b, ln_ffn1, ln_mix, ln_ffn2, ffn1_wg, ffn1_wu, ffn1_wd, w_in, conv_w, a_log,
                           dt_bias, dn_norm, w_a, w_b, w_o, ffn2_wg, ffn2_wu, ffn2_wd, final_norm)))
    mom = dict(zip(names, (m_ada_w, m_ada_b, m_ln_ffn1, m_ln_mix, m_ln_ffn2, m_ffn1_wg, m_ffn1_wu, m_ffn1_wd, m_w_in,
                           m_conv_w, m_a_log, m_dt_bias, m_dn_norm, m_w_a, m_w_b, m_w_o, m_ffn2_wg, m_ffn2_wu,
                           m_ffn2_wd, m_final_norm)))
    var = dict(zip(names, (v_ada_w, v_ada_b, v_ln_ffn1, v_ln_mix, v_ln_ffn2, v_ffn1_wg, v_ffn1_wu, v_ffn1_wd, v_w_in,
                           v_conv_w, v_a_log, v_dt_bias, v_dn_norm, v_w_a, v_w_b, v_w_o, v_ffn2_wg, v_ffn2_wu,
                           v_ffn2_wd, v_final_norm)))
    _, s, d = x.shape
    depth = ada_w.shape[0]
    mx, my, mc = lax.axis_index("x"), lax.axis_index("y"), lax.axis_index("c")
    chip = 2 * mx + my
    me = 2 * chip + mc
    nshard = ada_w.shape[2]

    cact = _rowwise("c_silu", lambda cv: ((_silu(cv),), ()), [jnp.pad(c, ((0, 7), (0, 0)))], [], [(d, F32)], bm=8)[0]
    c_all = _allgather8("ag_c", cact)[:, 0, :]
    conv_all = _allgather8("ag_conv", jnp.pad(conv_w.reshape(depth * DN_CONV, -1), ((0, 8 - depth * DN_CONV), (0, 0))))
    conv_full = jnp.concatenate([conv_all[2 * j, :depth * DN_CONV] for j in range(4)], axis=1)
    conv_full = conv_full.reshape(depth, DN_CONV, 3 * DN_WIDTH)
    layer_shards = [[jnp.stack([wts[nm][l].astype(BF16) for nm in nms], axis=0) for nms, _ in _GROUPS]
                    for l in range(depth)]
    gathered0 = _gather_groups("ag_weights0", layer_shards[0])
    gathered0, later = lax.optimization_barrier((gathered0, layer_shards[1:]))
    started = {l: _gather_start(f"ag_weights{l}_start", later[l - 1]) for l in range(1, depth)}
    rows_of = lambda st: st[:, 0].reshape(-1, st.shape[-1])
    cols_of = lambda st: jnp.concatenate([st[j, 0] for j in range(4)], axis=1)

    def layer_weights(l, after):
        if l == 0:
            got = gathered0
        else:
            srcs, lands = _gather_wait(f"ag_weights{l}_wait", started[l], after)
            got = _pair_forward_groups(f"ag_weights{l}_pair", lands, srcs)
        ga, gb, g_wa, g_wo, g_win, g_wb = got
        return ga, gb, _mixer_weights(cols_of(g_win), rows_of(g_wa), cols_of(g_wb), rows_of(g_wo), d)

    c16 = jnp.pad(c_all, ((0, 8), (0, 0))).astype(BF16)
    parts = []
    for l in range(depth):
        bias = lax.dynamic_slice(ada_b[l], (chip * nshard,), (nshard,)).reshape(1, nshard)
        (mp,) = _matmul(f"ada_fwd{l}", c16, ada_w[l].astype(BF16), epi_bcast=[bias], epi=lambda acc, b: (acc + b,))
        parts.append(mp)
    mod_all = _allgather8("ag_mod", jnp.concatenate(parts, axis=0))
    mod_rows = jnp.concatenate([mod_all[2 * j] for j in range(4)], axis=1)
    mod = jnp.stack([lax.dynamic_index_in_dim(mod_rows, l * 16 + me, axis=0, keepdims=False) for l in range(depth)])

    for st in started.values():
        mod = mod + st["token"][0, 0]
    small = dict(conv_w=conv_full, a_log=a_log, dt_bias=dt_bias, dn_norm=dn_norm, ln_ffn1=ln_ffn1, ln_mix=ln_mix,
                 ln_ffn2=ln_ffn2, final_norm=final_norm)
    ffn_names = _GROUPS[0][0] + _GROUPS[1][0]
    rs_state = {}

    def on_layer_grads(l, wg):
        wg["w_in"] = _w_in_grad(wg)
        gs = [jnp.stack([wg[nm] if nm in ffn_names else _shard_major(wg[nm], ax) for nm in nms], axis=1)
              for nms, ax in _GROUPS]
        rs_state[l] = _reduce_scatter_begin(f"rs{l}", gs, overlap=l > 0)
        return rs_state[l]["token"][0, 0] if l > 0 else None

    loss_part, dx, dmod, sgrads, d_fnorm = _local_step(x[0], loss_target[0], mod, layer_weights, small,
                                                       on_layer_grads)
    reduced = [_reduce_scatter_end(f"rs{l}", rs_state[l], l > 0, dx) for l in range(depth)]

    dmod_all = _allgather8("ag_dmod", jnp.pad(dmod, ((0, 8 - depth), (0, 0))))
    g_ada_w, g_ada_b = [], []
    for l in range(depth):
        dm_l = dmod_all[:, l, :]
        (gb_l,) = _rowwise(f"ada_b_grad{l}", lambda v: ((), (jnp.sum(v, axis=0, keepdims=True),)), [dm_l], [], [],
                           [(1, N_ADA * d)], bm=8)
        g_ada_b.append(gb_l[0])
        dm_sh = lax.dynamic_slice(dm_l, (0, chip * nshard), (8, nshard))
        (gw_l,) = _matmul(f"ada_w_grad{l}", c16, jnp.pad(dm_sh, ((0, 8), (0, 0))).astype(BF16), ta=True)
        g_ada_w.append(gw_l)
    grads = dict(ada_w=jnp.stack(g_ada_w), ada_b=jnp.stack(g_ada_b))

    smalls = [loss_part.reshape(1), d_fnorm]
    for l in range(depth):
        sg = sgrads[l]
        smalls += [sg["ln_ffn1"], sg["ln_mix"], sg["ln_ffn2"], sg["a_log"], sg["dt_bias"], sg["dn_norm"],
                   sg["conv_w"].reshape(-1)]
    sizes = [t.shape[0] for t in smalls]
    tile = 8 * LANES
    flat = jnp.concatenate([jnp.pad(t, (0, (-t.shape[0]) % tile)).reshape(-1, LANES) for t in smalls], axis=0)
    tot = _sum_leading("small_sum", _allgather8("ag_small", flat))
    offs, acc = [], 0
    for n_ in sizes:
        offs.append(acc)
        acc += -(-n_ // tile) * 8
    take = lambda i: tot[offs[i]:offs[i] + -(-sizes[i] // tile) * 8].reshape(-1)[:sizes[i]]
    loss = take(0)[0]
    grads["final_norm"] = take(1)
    per = 7
    for key_i, key in enumerate(["ln_ffn1", "ln_mix", "ln_ffn2", "a_log", "dt_bias", "dn_norm"]):
        grads[key] = jnp.stack([take(2 + per * l + key_i) for l in range(depth)])
    conv_g = jnp.stack([take(2 + per * l + 6).reshape(DN_CONV, 3 * DN_WIDTH) for l in range(depth)])
    csh = conv_w.shape[2]
    grads["conv_w"] = lax.dynamic_slice(conv_g, (0, 0, chip * csh), (depth, DN_CONV, csh))

    deltas, new_m, new_v = {}, {}, {}
    for gi, (nms, _) in enumerate(_GROUPS):
        for q, nm in enumerate(nms):
            grads[nm], deltas[nm], new_m[nm], new_v[nm] = _adamw_stacked(
                "adamw_" + nm, wts[nm], mom[nm], var[nm], [reduced[l][gi] for l in range(depth)], q)

    for name in names:
        if name in deltas:
            continue
        wv, gv, mv, vv = wts[name], grads[name], mom[name], var[name]
        if wv.ndim == 1:
            wv, gv, mv, vv = (t.reshape(-1, LANES) for t in (wv, gv, mv, vv))
        dl, mn, vn = _adamw("adamw_" + name, wv, gv, mv, vv)
        deltas[name], new_m[name], new_v[name] = (t.reshape(wts[name].shape) for t in (dl, mn, vn))
    return (loss, dx.reshape(1, s, d), *[grads[n_] for n_ in names], *[deltas[n_] for n_ in names],
            *[new_m[n_] for n_ in names], *[new_v[n_] for n_ in names])
```

```python
import functools

import jax
import jax.numpy as jnp
from jax import lax
from jax.experimental import pallas as pl
from jax.experimental.pallas import tpu as pltpu

F32 = jnp.float32
BF16 = jnp.bfloat16
MESH = pl.DeviceIdType.MESH

NORM_EPS = 1e-6
DN_HEADS, DN_DIM, DN_CHUNK, DN_CONV = 8, 128, 64, 4
DN_WIDTH = DN_HEADS * DN_DIM
DA_HEADS, DA_DIM, DA_BLOCK = 12, 64, 128
DA_WIDTH = DA_HEADS * DA_DIM
DA_PATTERNS = ((128, 1), (512, 4), (2048, 16))
ALIBI_MAX_EXP = 8.0
N_ADA = 9
LANES = 128
V7X_VMEM_BYTES = 64 << 20
ADAM_LR, ADAM_B1, ADAM_B2, ADAM_EPS, ADAM_WD, ADAM_STEP = 0.001, 0.9, 0.999, 1e-08, 0.01, 10
NEG = -1e30
HI = lax.Precision.HIGHEST
NN = (((1,), (0,)), ((), ()))
NT = (((1,), (1,)), ((), ()))
TN = (((0,), (0,)), ((), ()))


def _nbytes(shape, dtype):
    n = 1
    for s in shape:
        n *= s
    return n * jnp.dtype(dtype).itemsize


def _params(block_bytes, scratch_bytes=0):
    need = 2 * block_bytes + scratch_bytes
    lim = min(max(need + need // 4 + (4 << 20), 32 << 20), V7X_VMEM_BYTES - (6 << 20))
    return pltpu.CompilerParams(vmem_limit_bytes=int(lim))


def _pick(n, cands):
    for c in cands:
        if c <= n and n % c == 0:
            return c
    return n


def _sigmoid(x):
    return jax.nn.sigmoid(x)


def _silu(x):
    return x * jax.nn.sigmoid(x)


def _softplus(x):
    return jnp.maximum(x, 0.0) + jnp.log(1.0 + jnp.exp(-jnp.abs(x)))


def _rowwise(name, fn, rows, bcast, row_outs, red_outs=(), bm=256):
    rows = [r if isinstance(r, tuple) else (r, r.shape[1], 0) for r in rows]
    s = rows[0][0].shape[0]
    bm = _pick(s, (bm, 128, 64, 32, 16, 8))
    nr, nb, no, nd = len(rows), len(bcast), len(row_outs), len(red_outs)
    in_specs = [pl.BlockSpec((bm, w), functools.partial(lambda i, ci: (i, ci), ci=ci)) for (_, w, ci) in rows]
    in_specs += [pl.BlockSpec(b.shape, lambda i: (0, 0)) for b in bcast]
    out_shape = [jax.ShapeDtypeStruct((s, w), dt) for (w, dt) in row_outs]
    out_shape += [jax.ShapeDtypeStruct((r, w), F32) for (r, w) in red_outs]
    out_specs = [pl.BlockSpec((bm, w), lambda i: (i, 0)) for (w, _) in row_outs]
    out_specs += [pl.BlockSpec((r, w), lambda i: (0, 0)) for (r, w) in red_outs]

    def body(*refs):
        ins = [r[...] for r in refs[:nr + nb]]
        outs = refs[nr + nb:nr + nb + no]
        reds = refs[nr + nb + no:]
        ov, rv = fn(*ins)
        for o, v in zip(outs, ov):
            o[...] = v.astype(o.dtype)
        if nd:
            @pl.when(pl.program_id(0) == 0)
            def _():
                for r in reds:
                    r[...] = jnp.zeros(r.shape, F32)
            for r, v in zip(reds, rv):
                r[...] += v.astype(F32)

    blk = sum(_nbytes((bm, w), a.dtype) for (a, w, _) in rows) + sum(_nbytes(b.shape, b.dtype) for b in bcast)
    blk += sum(_nbytes((bm, w), dt) for (w, dt) in row_outs) + sum(_nbytes(r, F32) for r in red_outs)
    res = pl.pallas_call(
        body, name=name, grid=(s // bm,), in_specs=in_specs, out_specs=out_specs, out_shape=out_shape,
        compiler_params=_params(3 * blk),
    )(*[a for (a, _, _) in rows], *bcast)
    return res


def _matmul(name, a, b, *, ta=False, tb=False, outs=(F32,), epi=None, epi_rows=(), epi_bcast=(),
            bm=None, bn=None, bk=None):
    if ta:
        k, m = a.shape
    else:
        m, k = a.shape
    n = b.shape[0] if tb else b.shape[1]
    assert (b.shape[1] if tb else b.shape[0]) == k, (name, a.shape, b.shape)
    if bm is None:
        bm = _pick(m, (1024, 1408, 768, 512, 384, 256, 128)) if ta else _pick(m, (1024, 512, 256, 128, 64, 32, 16))
    if bn is None:
        bn = _pick(n, (512, 384, 256, 128))
    if bk is None:
        bk = k if k <= 3072 else _pick(k, (2816, 2048, 1024, 512))
        if ta:
            bk = _pick(k, (1024, 512, 256, 128, 64, 32, 16))
    nk = k // bk
    dims = TN if ta else (NT if tb else NN)
    a_spec = pl.BlockSpec((bk, bm), lambda i, j, kk: (kk, i)) if ta else pl.BlockSpec((bm, bk), lambda i, j, kk: (i, kk))
    b_spec = pl.BlockSpec((bn, bk), lambda i, j, kk: (j, kk)) if tb else pl.BlockSpec((bk, bn), lambda i, j, kk: (kk, j))
    in_specs = [a_spec, b_spec]
    in_specs += [pl.BlockSpec((bm, bn), lambda i, j, kk: (i, j)) for _ in epi_rows]
    in_specs += [pl.BlockSpec((1, bn), lambda i, j, kk: (0, j)) for _ in epi_bcast]
    out_shape = [jax.ShapeDtypeStruct((m, n), dt) for dt in outs]
    out_specs = [pl.BlockSpec((bm, bn), lambda i, j, kk: (i, j)) for _ in outs]
    ner, neb, no = len(epi_rows), len(epi_bcast), len(outs)

    def body(*refs):
        a_ref, b_ref = refs[0], refs[1]
        extra = refs[2:2 + ner + neb]
        out_refs = refs[2 + ner + neb:2 + ner + neb + no]
        prod = lax.dot_general(a_ref[...], b_ref[...], dims, preferred_element_type=F32)

        def finish(acc):
            vals = epi(acc, *[r[...] for r in extra]) if epi is not None else (acc,)
            for o, v in zip(out_refs, vals):
                o[...] = v.astype(o.dtype)

        if nk == 1:
            finish(prod)
        else:
            acc_ref = refs[-1]
            kk = pl.program_id(2)

            @pl.when(kk == 0)
            def _():
                acc_ref[...] = prod

            @pl.when(kk > 0)
            def _():
                acc_ref[...] += prod

            @pl.when(kk == nk - 1)
            def _():
                finish(acc_ref[...])

    blk = _nbytes((bm, bk), a.dtype) + _nbytes((bk, bn), b.dtype)
    blk += sum(_nbytes((bm, bn), r.dtype) for r in epi_rows) + sum(_nbytes((bm, bn), dt) for dt in outs)
    scratch = [pltpu.VMEM((bm, bn), F32)] if nk > 1 else []
    res = pl.pallas_call(
        body, name=name, grid=(m // bm, n // bn, nk), in_specs=in_specs, out_specs=out_specs,
        out_shape=out_shape, scratch_shapes=scratch,
        compiler_params=_params(blk, 3 * _nbytes((bm, bn), F32)),
    )(a, b, *epi_rows, *epi_bcast)
    return res


def _mm_core(name, grid, nk, pairs, out_defs, acc_shape, epi=None, epi_ins=()):
    npair, nep, no = len(pairs), len(epi_ins), len(out_defs)

    def body(*refs):
        extra = refs[2 * npair:2 * npair + nep]
        out_refs = refs[2 * npair + nep:2 * npair + nep + no]
        prod = None
        for p in range(npair):
            d = lax.dot_general(refs[2 * p][...], refs[2 * p + 1][...], pairs[p][4], preferred_element_type=F32)
            prod = d if prod is None else prod + d

        def finish(acc):
            vals = epi(acc, *[r[...] for r in extra]) if epi is not None else (acc,)
            for o, v in zip(out_refs, vals):
                o[...] = v.astype(o.dtype)

        if nk == 1:
            finish(prod)
        else:
            acc_ref = refs[-1]
            kk = pl.program_id(2)

            @pl.when(kk == 0)
            def _():
                acc_ref[...] = prod

            @pl.when(kk > 0)
            def _():
                acc_ref[...] += prod

            @pl.when(kk == nk - 1)
            def _():
                finish(acc_ref[...])

    def blk_bytes(spec, dtype):
        return _nbytes([s for s in spec.block_shape if s is not None], dtype)

    blk = sum(blk_bytes(sa, a.dtype) + blk_bytes(sb, b.dtype) for (a, sa, b, sb, _) in pairs)
    blk += sum(blk_bytes(sp, arr.dtype) for (arr, sp) in epi_ins) + sum(blk_bytes(sp, dt) for (_, dt, sp) in out_defs)
    ins, in_specs = [], []
    for (a, sa, b, sb, _) in pairs:
        ins += [a, b]
        in_specs += [sa, sb]
    ins += [arr for (arr, _) in epi_ins]
    in_specs += [sp for (_, sp) in epi_ins]
    return pl.pallas_call(
        body, name=name, grid=grid, in_specs=in_specs, out_specs=[sp for (_, _, sp) in out_defs],
        out_shape=[jax.ShapeDtypeStruct(sh, dt) for (sh, dt, _) in out_defs],
        scratch_shapes=[pltpu.VMEM(acc_shape, F32)] if nk > 1 else [],
        compiler_params=_params(blk, 3 * _nbytes(acc_shape, F32)),
    )(*ins)


def _rms_mod(h, ln, sh, sc):
    n = h * lax.rsqrt(jnp.mean(h * h, axis=-1, keepdims=True) + NORM_EPS) * ln
    return n * (1.0 + sc) + sh


def _swiglu_act(g, u):
    return _silu(g.astype(F32)) * u.astype(F32)


def _dn_prep(yc, pba, alog, dtb):
    act = _silu(yc)
    parts = []
    for idx in range(2 * DN_HEADS):
        seg = act[:, idx * DN_DIM:(idx + 1) * DN_DIM]
        seg = seg * lax.rsqrt(jnp.sum(seg * seg, axis=-1, keepdims=True) + NORM_EPS)
        if idx < DN_HEADS:
            seg = seg * (DN_DIM ** -0.5)
        parts.append(seg)
    parts.append(act[:, 2 * DN_WIDTH:])
    qkvn = jnp.concatenate(parts, axis=1)
    lane = lax.broadcasted_iota(jnp.int32, pba.shape, 1)
    beta = _sigmoid(pba)
    g = -jnp.exp(alog) * _softplus(pba + dtb)
    gb = jnp.where(lane < DN_HEADS, beta, jnp.where(lane < 2 * DN_HEADS, g, 0.0))
    return qkvn, gb


def _dn_outnorm(o_a, z, dn):
    parts = []
    for h in range(DN_HEADS):
        seg = o_a[:, h * DN_DIM:(h + 1) * DN_DIM]
        seg = seg * lax.rsqrt(jnp.mean(seg * seg, axis=-1, keepdims=True) + NORM_EPS) * dn
        parts.append(seg)
    return jnp.concatenate(parts, axis=1) * _silu(z)


def _shift_down(x, halo8, s):
    r = pltpu.roll(x, s, axis=0)
    top = pltpu.roll(halo8, s, axis=0)
    i8 = lax.broadcasted_iota(jnp.int32, top.shape, 0)
    return jnp.concatenate([jnp.where(i8 < s, top, r[0:8]), r[8:]], axis=0)


def _shift_up(x, halo8, s):
    m = x.shape[0]
    r = pltpu.roll(x, m - s, axis=0)
    bot = pltpu.roll(halo8, 8 - s, axis=0)
    i8 = lax.broadcasted_iota(jnp.int32, bot.shape, 0)
    return jnp.concatenate([r[:m - 8], jnp.where(i8 >= 8 - s, bot, r[m - 8:])], axis=0)


def _conv_prep_fwd(name, pq, convw8, pba, alog, dtb, bm=256):
    s, w = pq.shape
    nblk = s // bm
    hb = bm // 16

    def body(x_ref, halo_ref, w_ref, pba_ref, alog_ref, dtb_ref, yc_ref, qkv_ref, gb_ref):
        i = pl.program_id(0)
        x = x_ref[...].astype(F32)
        halo = jnp.where(i > 0, halo_ref[...].astype(F32)[8:16], 0.0)
        cw = w_ref[...]
        y = x * cw[DN_CONV - 1:DN_CONV]
        for sft in range(1, DN_CONV):
            y = y + _shift_down(x, halo, sft) * cw[DN_CONV - 1 - sft:DN_CONV - sft]
        ycb = y.astype(BF16)
        yc_ref[...] = ycb
        qkvn, gb = _dn_prep(ycb.astype(F32), pba_ref[...], alog_ref[...], dtb_ref[...])
        qkv_ref[...] = qkvn.astype(BF16)
        gb_ref[...] = gb

    blk = 3 * _nbytes((bm, w), BF16) + 4 * _nbytes((bm, w), F32)
    return pl.pallas_call(
        body, name=name, grid=(nblk,),
        in_specs=[pl.BlockSpec((bm, w), lambda i: (i, 0)),
                  pl.BlockSpec((16, w), lambda i: (jnp.maximum(i * hb - 1, 0), 0)),
                  pl.BlockSpec(convw8.shape, lambda i: (0, 0)),
                  pl.BlockSpec((bm, LANES), lambda i: (i, 0)),
                  pl.BlockSpec((1, LANES), lambda i: (0, 0)),
                  pl.BlockSpec((1, LANES), lambda i: (0, 0))],
        out_specs=[pl.BlockSpec((bm, w), lambda i: (i, 0)), pl.BlockSpec((bm, w), lambda i: (i, 0)),
                   pl.BlockSpec((bm, LANES), lambda i: (i, 0))],
        out_shape=[jax.ShapeDtypeStruct((s, w), BF16), jax.ShapeDtypeStruct((s, w), BF16),
                   jax.ShapeDtypeStruct((s, LANES), F32)],
        compiler_params=_params(blk),
    )(pq, pq, convw8, pba, alog, dtb)


def _conv_bwd(name, dyc, pq, convw8, bm=256):
    s, w = pq.shape
    nblk = s // bm
    hb = bm // 16

    def body(dy_ref, dyn_ref, x_ref, xh_ref, w_ref, dx_ref, dw_ref):
        i = pl.program_id(0)
        dy = dy_ref[...].astype(F32)
        nxt = jnp.where(i < nblk - 1, dyn_ref[...].astype(F32)[0:8], 0.0)
        x = x_ref[...].astype(F32)
        halo = jnp.where(i > 0, xh_ref[...].astype(F32)[8:16], 0.0)
        cw = w_ref[...]
        dx = dy * cw[DN_CONV - 1:DN_CONV]
        for sft in range(1, DN_CONV):
            dx = dx + _shift_up(dy, nxt, sft) * cw[DN_CONV - 1 - sft:DN_CONV - sft]
        dx_ref[...] = dx.astype(dx_ref.dtype)
        r8 = lax.broadcasted_iota(jnp.int32, (8, w), 0)
        dw = jnp.zeros((8, w), F32)
        for j in range(DN_CONV):
            sft = DN_CONV - 1 - j
            xs = x if sft == 0 else _shift_down(x, halo, sft)
            dw = dw + jnp.where(r8 == j, jnp.sum(dy * xs, axis=0, keepdims=True), 0.0)

        @pl.when(i == 0)
        def _():
            dw_ref[...] = jnp.zeros((8, w), F32)
        dw_ref[...] += dw

    blk = 4 * _nbytes((bm, w), BF16) + 5 * _nbytes((bm, w), F32)
    return pl.pallas_call(
        body, name=name, grid=(nblk,),
        in_specs=[pl.BlockSpec((bm, w), lambda i: (i, 0)),
                  pl.BlockSpec((16, w), lambda i: (jnp.minimum((i + 1) * hb, s // 16 - 1), 0)),
                  pl.BlockSpec((bm, w), lambda i: (i, 0)),
                  pl.BlockSpec((16, w), lambda i: (jnp.maximum(i * hb - 1, 0), 0)),
                  pl.BlockSpec(convw8.shape, lambda i: (0, 0))],
        out_specs=[pl.BlockSpec((bm, w), lambda i: (i, 0)), pl.BlockSpec((8, w), lambda i: (0, 0))],
        out_shape=[jax.ShapeDtypeStruct((s, w), BF16), jax.ShapeDtypeStruct((8, w), F32)],
        compiler_params=_params(blk),
    )(dyc, dyc, pq, pq, convw8)


BNN = (((2,), (1,)), ((0,), (0,)))
BNT = (((2,), (2,)), ((0,), (0,)))
BTN = (((1,), (1,)), ((0,), (0,)))


def _raw_dot_1pass(a, b, dims):
    return lax.dot_general(a.astype(BF16), b.astype(BF16), dims, preferred_element_type=F32)


def _raw_dot_3pass(a, b, dims):
    ah = a.astype(BF16)
    al = (a - ah.astype(F32)).astype(BF16)
    bh = b.astype(BF16)
    bl = (b - bh.astype(F32)).astype(BF16)
    d = lambda x, y: lax.dot_general(x, y, dims, preferred_element_type=F32)
    return d(ah, bh) + (d(ah, bl) + d(al, bh))


def _with_same_precision_vjp(raw):
    @functools.partial(jax.custom_vjp, nondiff_argnums=(2,))
    def dot(a, b, dims):
        return raw(a, b, dims)

    def fwd(a, b, dims):
        return raw(a, b, dims), (a, b)

    def bwd(dims, res, ct):
        a, b = res
        if dims == BNN:
            return raw(ct, b, BNT), raw(a, ct, BTN)
        if dims == BNT:
            return raw(ct, b, BNN), raw(ct, a, BTN)
        assert dims == BTN
        return raw(b, ct, BNT), raw(a, ct, BNN)

    dot.defvjp(fwd, bwd)
    return dot


_dot_1pass_vjp = _with_same_precision_vjp(_raw_dot_1pass)
_dot_3pass_vjp = _with_same_precision_vjp(_raw_dot_3pass)


def _dot_bf16(a, b, dims=BNN):
    return _dot_1pass_vjp(a, b, dims)


def _dot_3pass(a, b, dims=BNN):
    return _dot_3pass_vjp(a, b, dims)


def _neumann_inverse(x):
    h, c, _ = x.shape
    eye = lax.broadcasted_iota(jnp.int32, (h, c, c), 1) == lax.broadcasted_iota(jnp.int32, (h, c, c), 2)
    t = jnp.where(eye, 1.0, 0.0) + x
    p = x
    for _ in range(5):
        p = _raw_dot_3pass(p, p, BNN)
        t = t + _raw_dot_3pass(t, p, BNN)
    return t


@jax.custom_vjp
def _known_inverse(x, t):
    return t


def _known_inverse_fwd(x, t):
    return t, t


def _known_inverse_bwd(t, ct):
    return _raw_dot_3pass(_raw_dot_3pass(t, ct, BTN), t, BNT), jnp.zeros_like(t)


_known_inverse.defvjp(_known_inverse_fwd, _known_inverse_bwd)


def _delta_chunk(q, k, v, gcol, bcol, state, t_known=None):
    h, c, _ = q.shape
    row = lax.broadcasted_iota(jnp.int32, (h, c, c), 1)
    col = lax.broadcasted_iota(jnp.int32, (h, c, c), 2)
    incl, strict, eye = row >= col, row > col, row == col
    g_b = jnp.broadcast_to(gcol, (h, c, c))
    gc_row = jnp.sum(jnp.where(row <= col, g_b, 0.0), axis=1, keepdims=True)
    g_r = jnp.sum(jnp.where(eye, g_b, 0.0), axis=1, keepdims=True)
    gc_col = jnp.sum(jnp.where(incl, jnp.broadcast_to(g_r, (h, c, c)), 0.0), axis=2, keepdims=True)
    decay = jnp.exp(jnp.where(incl, gc_col - gc_row, NEG))
    kb = k * bcol
    vb = v * bcol
    x = -jnp.where(strict, _dot_bf16(kb, k, BNT) * decay, 0.0)
    t = _neumann_inverse(x) if t_known is None else _known_inverse(x, t_known)
    eg = jnp.exp(gc_col)
    u = _dot_3pass(t, vb)
    w = _dot_3pass(t, kb * eg)
    qk = _dot_bf16(q, k, BNT) * decay
    v_new = u - _dot_bf16(w, state)
    o = _dot_bf16(q * eg, state) + _dot_bf16(qk, v_new)
    g_last = jnp.sum(g_r, axis=2, keepdims=True)
    new_state = state * jnp.exp(g_last) + _dot_bf16(k * jnp.exp(g_last - gc_col), v_new, BTN)
    return o, new_state, t


def _lane_col(blk, idx):
    lane = lax.broadcasted_iota(jnp.int32, blk.shape, 1)
    return jnp.sum(jnp.where(lane == idx, blk, 0.0), axis=1, keepdims=True)


def _dn_heads(ref, base):
    return jnp.stack([ref[:, base + h * DN_DIM:base + (h + 1) * DN_DIM] for h in range(DN_HEADS)], axis=0).astype(F32)


def _dn_cols(gbv, base):
    return jnp.stack([_lane_col(gbv, base + h) for h in range(DN_HEADS)], axis=0)


def _delta_fwd(name, qkvn, gb):
    s = qkvn.shape[0]
    n = s // DN_CHUNK
    c = DN_CHUNK

    def body(qkv_ref, gb_ref, o_ref, st_ref, t_ref, state):
        @pl.when(pl.program_id(0) == 0)
        def _():
            state[...] = jnp.zeros(state.shape, F32)

        gbv = gb_ref[...]
        st = state[...]
        st_ref[0] = st
        o, new, t = _delta_chunk(_dn_heads(qkv_ref, 0), _dn_heads(qkv_ref, DN_WIDTH), _dn_heads(qkv_ref, 2 * DN_WIDTH),
                                 _dn_cols(gbv, DN_HEADS), _dn_cols(gbv, 0), st)
        for h in range(DN_HEADS):
            o_ref[:, h * DN_DIM:(h + 1) * DN_DIM] = o[h]
        t_ref[0] = t
        state[...] = new

    blk = _nbytes((c, 3 * DN_WIDTH), BF16) + _nbytes((c, LANES), F32) + _nbytes((c, DN_WIDTH), F32)
    blk += _nbytes((DN_HEADS, DN_DIM, DN_DIM), F32) + _nbytes((DN_HEADS, c, c), F32)
    return pl.pallas_call(
        body, name=name, grid=(n,),
        in_specs=[pl.BlockSpec((c, 3 * DN_WIDTH), lambda i: (i, 0)), pl.BlockSpec((c, LANES), lambda i: (i, 0))],
        out_specs=[pl.BlockSpec((c, DN_WIDTH), lambda i: (i, 0)),
                   pl.BlockSpec((1, DN_HEADS, DN_DIM, DN_DIM), lambda i: (i, 0, 0, 0)),
                   pl.BlockSpec((1, DN_HEADS, c, c), lambda i: (i, 0, 0, 0))],
        out_shape=[jax.ShapeDtypeStruct((s, DN_WIDTH), F32),
                   jax.ShapeDtypeStruct((n, DN_HEADS, DN_DIM, DN_DIM), F32),
                   jax.ShapeDtypeStruct((n, DN_HEADS, c, c), F32)],
        scratch_shapes=[pltpu.VMEM((DN_HEADS, DN_DIM, DN_DIM), F32)],
        compiler_params=_params(blk, 8 << 20),
    )(qkvn, gb)


def _delta_bwd(name, qkvn, gb, states, tinv, d_o):
    s = qkvn.shape[0]
    n = s // DN_CHUNK
    c = DN_CHUNK

    def body(qkv_ref, gb_ref, st_ref, t_ref, do_ref, dqkv_ref, dgb_ref, dstate):
        @pl.when(pl.program_id(0) == 0)
        def _():
            dstate[...] = jnp.zeros(dstate.shape, F32)

        gbv = gb_ref[...]
        lane = lax.broadcasted_iota(jnp.int32, (c, LANES), 1)
        t_known = t_ref[0]
        chunk = lambda *args: _delta_chunk(*args, t_known=t_known)[:2]
        _, vjp = jax.vjp(chunk, _dn_heads(qkv_ref, 0), _dn_heads(qkv_ref, DN_WIDTH),
                         _dn_heads(qkv_ref, 2 * DN_WIDTH), _dn_cols(gbv, DN_HEADS), _dn_cols(gbv, 0), st_ref[0])
        dq, dk, dv, dg, db, dst = vjp((_dn_heads(do_ref, 0), dstate[...]))
        dgb = jnp.zeros((c, LANES), F32)
        for h in range(DN_HEADS):
            dqkv_ref[:, h * DN_DIM:(h + 1) * DN_DIM] = dq[h]
            dqkv_ref[:, DN_WIDTH + h * DN_DIM:DN_WIDTH + (h + 1) * DN_DIM] = dk[h]
            dqkv_ref[:, 2 * DN_WIDTH + h * DN_DIM:2 * DN_WIDTH + (h + 1) * DN_DIM] = dv[h]
            dgb = dgb + jnp.where(lane == h, db[h], 0.0) + jnp.where(lane == DN_HEADS + h, dg[h], 0.0)
        dstate[...] = dst
        dgb_ref[...] = dgb

    rev = lambda i: (n - 1 - i, 0)
    blk = _nbytes((c, 3 * DN_WIDTH), BF16) + 2 * _nbytes((c, LANES), F32) + _nbytes((c, DN_WIDTH), F32)
    blk += _nbytes((DN_HEADS, DN_DIM, DN_DIM), F32) + _nbytes((c, 3 * DN_WIDTH), F32)
    return pl.pallas_call(
        body, name=name, grid=(n,),
        in_specs=[pl.BlockSpec((c, 3 * DN_WIDTH), rev), pl.BlockSpec((c, LANES), rev),
                  pl.BlockSpec((1, DN_HEADS, DN_DIM, DN_DIM), lambda i: (n - 1 - i, 0, 0, 0)),
                  pl.BlockSpec((1, DN_HEADS, c, c), lambda i: (n - 1 - i, 0, 0, 0)),
                  pl.BlockSpec((c, DN_WIDTH), rev)],
        out_specs=[pl.BlockSpec((c, 3 * DN_WIDTH), rev), pl.BlockSpec((c, LANES), rev)],
        out_shape=[jax.ShapeDtypeStruct((s, 3 * DN_WIDTH), F32), jax.ShapeDtypeStruct((s, LANES), F32)],
        scratch_shapes=[pltpu.VMEM((DN_HEADS, DN_DIM, DN_DIM), F32)],
        compiler_params=_params(blk, 16 << 20),
    )(qkvn, gb, states, tinv, d_o)


def _da_scores(q2f, k2, sub, valid, distf, head):
    lane = lax.broadcasted_iota(jnp.int32, q2f.shape, 1)
    hmask = (lane < DA_DIM) if sub == 0 else (lane >= DA_DIM)
    qm = jnp.where(hmask, q2f, 0.0).astype(BF16)
    slope = 2.0 ** (-ALIBI_MAX_EXP * (head + 1) / DA_HEADS)
    sc = lax.dot_general(qm, k2, NT, preferred_element_type=F32) * (DA_DIM ** -0.5)
    return jnp.where(valid, sc - slope * distf, NEG), qm, hmask


def _da_mask(i, r):
    qi = lax.broadcasted_iota(jnp.int32, (DA_BLOCK, 2 * DA_BLOCK), 0)
    ki = lax.broadcasted_iota(jnp.int32, (DA_BLOCK, 2 * DA_BLOCK), 1)
    dist = qi + DA_BLOCK - ki
    valid = (dist >= 0) & (dist <= DA_BLOCK) & ((ki >= DA_BLOCK) | (i > 0))
    return valid, (dist * r).astype(F32)


def _da_fwd(name, pda, r):
    s = pda.shape[0]
    n = s // r
    nb = n // DA_BLOCK
    w = DA_WIDTH
    dav = pda.reshape(n, r * 3 * w)

    def body(q_ref, kc_ref, kp_ref, vc_ref, vp_ref, o_ref, lse_ref):
        i = pl.program_id(1)
        valid, distf = _da_mask(i, r)
        lane = lax.broadcasted_iota(jnp.int32, (DA_BLOCK, LANES), 1)
        lse = jnp.zeros((DA_BLOCK, LANES), F32)
        for hp in range(DA_HEADS // 2):
            sl = slice(hp * LANES, (hp + 1) * LANES)
            q2f = q_ref[:, sl].astype(F32)
            k2 = jnp.concatenate([kp_ref[:, sl], kc_ref[:, sl]], axis=0)
            v2 = jnp.concatenate([vp_ref[:, sl], vc_ref[:, sl]], axis=0)
            o2 = None
            for sub in range(2):
                head = 2 * hp + sub
                sc, _, hmask = _da_scores(q2f, k2, sub, valid, distf, head)
                mx = jnp.max(sc, axis=1, keepdims=True)
                p = jnp.exp(sc - mx)
                l = jnp.sum(p, axis=1, keepdims=True)
                pv = lax.dot_general(p.astype(BF16), v2, NN, preferred_element_type=F32) / l
                o2 = pv if sub == 0 else jnp.where(hmask, pv, o2)
                lse = jnp.where(lane == head, mx + jnp.log(l), lse)
            o_ref[:, sl] = o2.astype(o_ref.dtype)
        lse_ref[...] = lse

    prev = lambda col: (lambda p, i: (jnp.maximum(i - 1, 0), 3 * p + col))
    cur = lambda col: (lambda p, i: (i, 3 * p + col))
    blk = 5 * _nbytes((DA_BLOCK, w), BF16) + _nbytes((DA_BLOCK, w), F32) + _nbytes((DA_BLOCK, LANES), F32)
    o, lse = pl.pallas_call(
        body, name=name, grid=(r, nb),
        in_specs=[pl.BlockSpec((DA_BLOCK, w), cur(0)), pl.BlockSpec((DA_BLOCK, w), cur(1)),
                  pl.BlockSpec((DA_BLOCK, w), prev(1)), pl.BlockSpec((DA_BLOCK, w), cur(2)),
                  pl.BlockSpec((DA_BLOCK, w), prev(2))],
        out_specs=[pl.BlockSpec((DA_BLOCK, w), lambda p, i: (i, p)),
                   pl.BlockSpec((DA_BLOCK, LANES), lambda p, i: (i, p))],
        out_shape=[jax.ShapeDtypeStruct((n, r * w), BF16), jax.ShapeDtypeStruct((n, r * LANES), F32)],
        compiler_params=_params(blk, 8 << 20),
    )(dav, dav, dav, dav, dav)
    return o.reshape(s, w), lse.reshape(s, LANES)


def _da_bwd(name, pda, d_ob, lse_tot, delta, r):
    s = pda.shape[0]
    n = s // r
    nb = n // DA_BLOCK
    w = DA_WIDTH
    dav = pda.reshape(n, r * 3 * w)
    dov = d_ob.reshape(n, r * w)
    lv = lse_tot.reshape(n, r * LANES)
    dlv = delta.reshape(n, r * LANES)

    def body(q_ref, kc_ref, kp_ref, vc_ref, vp_ref, do_ref, l_ref, dl_ref, dq_ref, dk_ref, dv_ref, ck, cv):
        i = pl.program_id(1)

        @pl.when(i == 0)
        def _():
            ck[...] = jnp.zeros(ck.shape, F32)
            cv[...] = jnp.zeros(cv.shape, F32)

        @pl.when(i < nb)
        def _():
            valid, distf = _da_mask(i, r)
            lsev = l_ref[...]
            dlt = dl_ref[...]
            for hp in range(DA_HEADS // 2):
                sl = slice(hp * LANES, (hp + 1) * LANES)
                q2f = q_ref[:, sl].astype(F32)
                k2 = jnp.concatenate([kp_ref[:, sl], kc_ref[:, sl]], axis=0)
                v2 = jnp.concatenate([vp_ref[:, sl], vc_ref[:, sl]], axis=0)
                do2f = do_ref[:, sl].astype(F32)
                dq2 = jnp.zeros((DA_BLOCK, LANES), F32)
                dk2 = jnp.zeros((2 * DA_BLOCK, LANES), F32)
                dv2 = jnp.zeros((2 * DA_BLOCK, LANES), F32)
                for sub in range(2):
                    head = 2 * hp + sub
                    sc, qm, hmask = _da_scores(q2f, k2, sub, valid, distf, head)
                    p = jnp.exp(sc - _lane_col(lsev, head))
                    dom = jnp.where(hmask, do2f, 0.0).astype(BF16)
                    dp = lax.dot_general(dom, v2, NT, preferred_element_type=F32)
                    ds = (p * (dp - _lane_col(dlt, head)) * (DA_DIM ** -0.5)).astype(BF16)
                    dq2 = dq2 + jnp.where(hmask, lax.dot_general(ds, k2, NN, preferred_element_type=F32), 0.0)
                    dk2 = dk2 + lax.dot_general(ds, qm, TN, preferred_element_type=F32)
                    dv2 = dv2 + lax.dot_general(p.astype(BF16), dom, TN, preferred_element_type=F32)
                dq_ref[:, sl] = dq2.astype(dq_ref.dtype)
                dk_ref[:, sl] = (ck[:, sl] + dk2[:DA_BLOCK]).astype(dk_ref.dtype)
                dv_ref[:, sl] = (cv[:, sl] + dv2[:DA_BLOCK]).astype(dv_ref.dtype)
                ck[:, sl] = dk2[DA_BLOCK:]
                cv[:, sl] = dv2[DA_BLOCK:]

        @pl.when(i == nb)
        def _():
            dk_ref[...] = ck[...].astype(dk_ref.dtype)
            dv_ref[...] = cv[...].astype(dv_ref.dtype)

    qrow = lambda i: jnp.minimum(i, nb - 1)
    prev = lambda col: (lambda p, i: (jnp.maximum(qrow(i) - 1, 0), 3 * p + col))
    cur = lambda col: (lambda p, i: (qrow(i), 3 * p + col))
    same = lambda p, i: (qrow(i), p)
    late = lambda p, i: (jnp.maximum(i - 1, 0), p)
    blk = 6 * _nbytes((DA_BLOCK, w), BF16) + 2 * _nbytes((DA_BLOCK, LANES), F32) + 3 * _nbytes((DA_BLOCK, w), F32)
    dq, dk, dv = pl.pallas_call(
        body, name=name, grid=(r, nb + 1),
        in_specs=[pl.BlockSpec((DA_BLOCK, w), cur(0)), pl.BlockSpec((DA_BLOCK, w), cur(1)),
                  pl.BlockSpec((DA_BLOCK, w), prev(1)), pl.BlockSpec((DA_BLOCK, w), cur(2)),
                  pl.BlockSpec((DA_BLOCK, w), prev(2)), pl.BlockSpec((DA_BLOCK, w), same),
                  pl.BlockSpec((DA_BLOCK, LANES), same), pl.BlockSpec((DA_BLOCK, LANES), same)],
        out_specs=[pl.BlockSpec((DA_BLOCK, w), same), pl.BlockSpec((DA_BLOCK, w), late),
                   pl.BlockSpec((DA_BLOCK, w), late)],
        out_shape=[jax.ShapeDtypeStruct((n, r * w), BF16)] * 3,
        scratch_shapes=[pltpu.VMEM((DA_BLOCK, w), F32), pltpu.VMEM((DA_BLOCK, w), F32)],
        compiler_params=_params(blk, 12 << 20),
    )(dav, dav, dav, dav, dav, dov, lv, dlv)
    return dq.reshape(s, w), dk.reshape(s, w), dv.reshape(s, w)


def _head_expand():
    hrow = lax.broadcasted_iota(jnp.int32, (LANES, DA_WIDTH), 0)
    lcol = lax.broadcasted_iota(jnp.int32, (LANES, DA_WIDTH), 1)
    return jnp.where(lcol // DA_DIM == hrow, 1.0, 0.0).astype(F32)


def _ffn_up(name, a, ga, tg, tu):
    s, d = a.shape
    nsh, _, _, ffs = ga.shape
    bm = _pick(s, (1024, 512, 256, 128))

    def body(a_ref, wg_ref, wu_ref, g_ref, u_ref, f_ref):
        av = a_ref[...]
        g = lax.dot_general(av, wg_ref[...], NN, preferred_element_type=F32)
        u = lax.dot_general(av, wu_ref[...], NN, preferred_element_type=F32)
        g_ref[...] = g.astype(BF16)
        u_ref[...] = u.astype(BF16)
        f_ref[...] = (_silu(g) * u).astype(BF16)

    wspec = lambda t: pl.BlockSpec((None, None, d, ffs), lambda i, j: (j, t, 0, 0))
    ospec = pl.BlockSpec((None, bm, ffs), lambda i, j: (j, i, 0))
    blk = _nbytes((bm, d), BF16) + 2 * _nbytes((d, ffs), BF16) + 3 * _nbytes((bm, ffs), BF16)
    return pl.pallas_call(
        body, name=name, grid=(s // bm, nsh),
        in_specs=[pl.BlockSpec((bm, d), lambda i, j: (i, 0)), wspec(tg), wspec(tu)],
        out_specs=[ospec] * 3, out_shape=[jax.ShapeDtypeStruct((nsh, s, ffs), BF16)] * 3,
        compiler_params=_params(blk, 4 * _nbytes((bm, ffs), F32)),
    )(a, ga, ga)


def _ffn_fwd(tag, h_in, ln, sh, sc, gt, ga, tg, tu, gb, td, weight):
    s, d = h_in.shape
    nsh, _, ffs, _ = gb.shape
    (a,) = _rowwise(tag + "_norm", lambda h, l, s1, s2: ((_rms_mod(h, l, s1, s2),), ()), [h_in], [ln, sh, sc],
                    [(d, BF16)])
    g, u, f = _ffn_up(tag + "_up", a, ga, tg, tu)
    bm, bn = _pick(s, (1024, 512, 256, 128)), _pick(d, (512, 256, 128))
    io = pl.BlockSpec((bm, bn), lambda i, j, kk: (i, j))
    h_out, o = _mm_core(
        tag + "_down", (s // bm, d // bn, nsh), nsh,
        [(f, pl.BlockSpec((None, bm, ffs), lambda i, j, kk: (kk, i, 0)),
          gb, pl.BlockSpec((None, None, ffs, bn), lambda i, j, kk: (kk, td, 0, j)), NN)],
        [((s, d), F32, io), ((s, d), BF16, io)], (bm, bn),
        epi=lambda acc, h, gv: (h + weight * gv * acc, acc),
        epi_ins=[(h_in, io), (gt, pl.BlockSpec((1, bn), lambda i, j, kk: (0, j)))])
    return h_out, dict(a=a, g=g, u=u, f=f, o=o)


def _resid_bwd(tag, dh_out, o, gt, weight):
    d = dh_out.shape[1]

    def fn(dh, ov, g):
        return (weight * g * dh,), (jnp.sum(weight * dh * ov.astype(F32), axis=0, keepdims=True),)

    do, d_gt = _rowwise(tag + "_resid_bwd", fn, [dh_out, o], [gt], [(d, BF16)], [(1, d)])
    return do, d_gt


def _norm_bwd(tag, h_in, da, dh_out, ln, sh, sc):
    d = h_in.shape[1]

    def fn(h, dav, dh, l, s1, s2):
        _, vjp = jax.vjp(_rms_mod, h, l, s1, s2)
        gh, gl, gs1, gs2 = vjp(dav)
        return (dh + gh,), (gl, gs1, gs2)

    return _rowwise(tag + "_norm_bwd", fn, [h_in, da, dh_out], [ln, sh, sc], [(d, F32)], [(1, d)] * 3)


def _ffn_bwd(tag, h_in, dh_out, sv, ln, sh, sc, gt, ga, tg, tu, gb, td, weight):
    s, d = h_in.shape
    nsh, _, ffs, _ = gb.shape
    bm, bn = _pick(s, (1024, 512, 256, 128)), _pick(d, (512, 256, 128))
    bk = _pick(s, (1024, 512, 256, 128))
    do, d_gt = _resid_bwd(tag, dh_out, sv["o"], gt, weight)

    def act_bwd(df, g, u):
        _, vjp = jax.vjp(_swiglu_act, g, u)
        return vjp(df)

    hid = pl.BlockSpec((None, bm, ffs), lambda i, j, kk: (j, i, 0))
    dg, du = _mm_core(
        tag + "_down_dx", (s // bm, nsh, 1), 1,
        [(do, pl.BlockSpec((bm, d), lambda i, j, kk: (i, 0)),
          gb, pl.BlockSpec((None, None, ffs, d), lambda i, j, kk: (j, td, 0, 0)), NT)],
        [((nsh, s, ffs), BF16, hid)] * 2, (bm, ffs), epi=act_bwd, epi_ins=[(sv["g"], hid), (sv["u"], hid)])
    (d_wd,) = _mm_core(
        tag + "_down_dw", (nsh, d // bn, s // bk), s // bk,
        [(sv["f"], pl.BlockSpec((None, bk, ffs), lambda i, j, kk: (i, kk, 0)),
          do, pl.BlockSpec((bk, bn), lambda i, j, kk: (kk, j)), TN)],
        [((nsh, ffs, d), BF16, pl.BlockSpec((None, ffs, bn), lambda i, j, kk: (i, 0, j)))], (ffs, bn))
    kmaj = pl.BlockSpec((None, bm, ffs), lambda i, j, kk: (kk, i, 0))
    wsp = lambda t: pl.BlockSpec((None, None, bn, ffs), functools.partial(lambda i, j, kk, t: (kk, t, j, 0), t=t))
    (da,) = _mm_core(
        tag + "_up_dx", (s // bm, d // bn, nsh), nsh, [(dg, kmaj, ga, wsp(tg), NT), (du, kmaj, ga, wsp(tu), NT)],
        [((s, d), F32, pl.BlockSpec((bm, bn), lambda i, j, kk: (i, j)))], (bm, bn))
    dws = []
    for nm, dh in (("_wg_dw", dg), ("_wu_dw", du)):
        (dw,) = _mm_core(
            tag + nm, (1, nsh, s // bk), s // bk,
            [(sv["a"], pl.BlockSpec((bk, d), lambda i, j, kk: (kk, 0)),
              dh, pl.BlockSpec((None, bk, ffs), lambda i, j, kk: (j, kk, 0)), TN)],
            [((nsh, d, ffs), BF16, pl.BlockSpec((None, d, ffs), lambda i, j, kk: (j, 0, 0)))], (d, ffs))
        dws.append(dw)
    dh_in, d_ln, d_sh, d_sc = _norm_bwd(tag, h_in, da, dh_out, ln, sh, sc)
    return dh_in, dict(wg=dws[0], wu=dws[1], wd=d_wd), dict(ln=d_ln, sh=d_sh, sc=d_sc, gt=d_gt)


def _mixer_fwd(tag, h_in, ln, sh, sc, gt, w, sp):
    d = h_in.shape[1]
    (a,) = _rowwise(tag + "_norm", lambda h, l, s1, s2: ((_rms_mod(h, l, s1, s2),), ()), [h_in], [ln, sh, sc],
                    [(d, BF16)])
    (pq,) = _matmul(tag + "_pq", a, w["wq"], outs=(BF16,))
    (pz,) = _matmul(tag + "_pz", a, w["wz"], outs=(BF16,))
    (pba,) = _matmul(tag + "_pba", a, w["wba"])
    (pda,) = _matmul(tag + "_pda", a, w["wda"], outs=(BF16,))
    (pg,) = _matmul(tag + "_pg", a, w["wg"], outs=(BF16,))
    yc, qkvn, gb = _conv_prep_fwd(tag + "_conv", pq, sp["conv8"], pba, sp["alog"], sp["dtb"])
    o_a, states, tinv = _delta_fwd(tag + "_delta", qkvn, gb)
    (o_an,) = _rowwise(tag + "_dnorm", lambda o, z, dn: ((_dn_outnorm(o, z.astype(F32), dn),), ()), [o_a, pz],
                       [sp["dn"]], [(DN_WIDTH, BF16)])
    ops, lses = [], []
    for (_, r) in DA_PATTERNS:
        o_p, lse_p = _da_fwd(f"{tag}_da{r}", pda, r)
        ops.append(o_p)
        lses.append(lse_p)

    def merge(o1, o2, o3, l1, l2, l3):
        mx = jnp.maximum(jnp.maximum(l1, l2), l3)
        e1, e2, e3 = jnp.exp(l1 - mx), jnp.exp(l2 - mx), jnp.exp(l3 - mx)
        tot = e1 + e2 + e3
        ex = _head_expand()
        up = lambda wgt: lax.dot_general(wgt / tot, ex, NN, precision=HI, preferred_element_type=F32)
        return (up(e1) * o1 + up(e2) * o2 + up(e3) * o3, mx + jnp.log(tot)), ()

    o_b, lse_tot = _rowwise(tag + "_merge", merge, ops + lses, [], [(DA_WIDTH, BF16), (LANES, F32)])
    (y_a,) = _matmul(tag + "_wa", o_an, w["w_a"], outs=(BF16,))
    (y_b,) = _matmul(tag + "_wb", o_b, w["w_b"], outs=(BF16,))

    def gate(ga, gbv, ya, yb):
        return _sigmoid(ga.astype(F32)) * ya.astype(F32) + _sigmoid(gbv.astype(F32)) * yb.astype(F32)

    (merged,) = _rowwise(tag + "_gate", lambda *v: ((gate(*v),), ()), [(pg, d, 0), (pg, d, 1), y_a, y_b], [],
                         [(d, BF16)])
    h_out, m = _matmul(tag + "_wo", merged, w["w_o"], outs=(F32, BF16), epi_rows=[h_in], epi_bcast=[gt],
                       epi=lambda acc, h, g: (h + g * acc, acc))
    sv = dict(a=a, pq=pq, pz=pz, pba=pba, pda=pda, pg=pg, yc=yc, qkvn=qkvn, gb=gb, o_a=o_a, states=states, tinv=tinv,
              o_an=o_an, o_b=o_b, lse=lse_tot, y_a=y_a, y_b=y_b, merged=merged, m=m, gate=gate)
    return h_out, sv


def _mixer_bwd(tag, h_in, dh_out, sv, ln, sh, sc, gt, w, sp):
    d = h_in.shape[1]
    dm, d_gt = _resid_bwd(tag, dh_out, sv["m"], gt, 1.0)
    (d_merged,) = _matmul(tag + "_wo_dx", dm, w["w_o"], tb=True, outs=(BF16,))
    (d_wo,) = _matmul(tag + "_wo_dw", sv["merged"], dm, ta=True, outs=(BF16,))
    gate = sv["gate"]

    def gate_bwd(dmg, ga, gbv, ya, yb):
        _, vjp = jax.vjp(gate, ga.astype(F32), gbv.astype(F32), ya.astype(F32), yb.astype(F32))
        dga, dgb, dya, dyb = vjp(dmg.astype(F32))
        return (jnp.concatenate([dga, dgb], axis=1), dya, dyb), ()

    pg = sv["pg"]
    d_pg, d_ya, d_yb = _rowwise(tag + "_gate_bwd", gate_bwd, [d_merged, (pg, d, 0), (pg, d, 1), sv["y_a"], sv["y_b"]],
                                [], [(2 * d, BF16), (d, BF16), (d, BF16)])
    (d_oan,) = _matmul(tag + "_wa_dx", d_ya, w["w_a"], tb=True)
    (d_wa,) = _matmul(tag + "_wa_dw", sv["o_an"], d_ya, ta=True, outs=(BF16,))
    (d_ob,) = _matmul(tag + "_wb_dx", d_yb, w["w_b"], tb=True, outs=(BF16,))
    (d_wb,) = _matmul(tag + "_wb_dw", sv["o_b"], d_yb, ta=True, outs=(BF16,))

    def dnorm_bwd(doan, o, z, dn):
        _, vjp = jax.vjp(_dn_outnorm, o, z.astype(F32), dn)
        go, gz, gdn = vjp(doan)
        return (go, gz), (gdn,)

    d_oa, d_pz, d_dn = _rowwise(tag + "_dnorm_bwd", dnorm_bwd, [d_oan, sv["o_a"], sv["pz"]], [sp["dn"]],
                                [(DN_WIDTH, F32), (DN_WIDTH, BF16)], [(1, DN_DIM)])
    d_qkvn, d_gb = _delta_bwd(tag + "_delta_bwd", sv["qkvn"], sv["gb"], sv["states"], sv["tinv"], d_oa)

    def prep_bwd(dq, dgbv, yc, pba, alog, dtb):
        _, vjp = jax.vjp(_dn_prep, yc.astype(F32), pba, alog, dtb)
        gyc, gpba, galog, gdtb = vjp((dq, dgbv))
        return (gyc, gpba), (galog, gdtb)

    d_yc, d_pba, d_alog, d_dtb = _rowwise(tag + "_prep_bwd", prep_bwd, [d_qkvn, d_gb, sv["yc"], sv["pba"]],
                                          [sp["alog"], sp["dtb"]], [(3 * DN_WIDTH, BF16), (LANES, BF16)],
                                          [(1, LANES), (1, LANES)], bm=128)
    d_pq, d_conv = _conv_bwd(tag + "_conv_bwd", d_yc, sv["pq"], sp["conv8"])

    def delta_fn(dob, ob):
        prod = dob.astype(F32) * ob.astype(F32)
        return (lax.dot_general(prod, _head_expand(), NT, precision=HI, preferred_element_type=F32),), ()

    (delta,) = _rowwise(tag + "_da_delta", delta_fn, [d_ob, sv["o_b"]], [], [(LANES, F32)])
    grads = [_da_bwd(f"{tag}_da{r}_bwd", sv["pda"], d_ob, sv["lse"], delta, r) for (_, r) in DA_PATTERNS]

    def sum3(*parts):
        q1, k1, v1, q2, k2, v2, q3, k3, v3 = (p.astype(F32) for p in parts)
        return (jnp.concatenate([q1 + q2 + q3, k1 + k2 + k3, v1 + v2 + v3], axis=1),), ()

    (d_pda,) = _rowwise(tag + "_da_sum", sum3, [t for g in grads for t in g], [], [(3 * DA_WIDTH, BF16)])

    a = sv["a"]
    (da,) = _matmul(tag + "_pq_dx", d_pq, w["wq"], tb=True)
    add = lambda acc, prev: (acc + prev,)
    (da,) = _matmul(tag + "_pz_dx", d_pz, w["wz"], tb=True, epi_rows=[da], epi=add)
    (da,) = _matmul(tag + "_pba_dx", d_pba, w["wba"], tb=True, epi_rows=[da], epi=add)
    (da,) = _matmul(tag + "_pda_dx", d_pda, w["wda"], tb=True, epi_rows=[da], epi=add)
    (da,) = _matmul(tag + "_pg_dx", d_pg, w["wg"], tb=True, epi_rows=[da], epi=add)
    (d_wq,) = _matmul(tag + "_pq_dw", a, d_pq, ta=True, outs=(BF16,))
    (d_wz,) = _matmul(tag + "_pz_dw", a, d_pz, ta=True, outs=(BF16,))
    (d_wba,) = _matmul(tag + "_pba_dw", a, d_pba, ta=True, outs=(BF16,))
    (d_wda,) = _matmul(tag + "_pda_dw", a, d_pda, ta=True, outs=(BF16,))
    (d_wg,) = _matmul(tag + "_pg_dw", a, d_pg, ta=True, outs=(BF16,))
    dh_in, d_ln, d_sh, d_sc = _norm_bwd(tag, h_in, da, dh_out, ln, sh, sc)
    wgrads = dict(wq=d_wq, wz=d_wz, wba=d_wba, wda=d_wda, wg=d_wg, w_a=d_wa, w_b=d_wb, w_o=d_wo)
    small = dict(ln=d_ln, sh=d_sh, sc=d_sc, gt=d_gt, dn=d_dn, alog=d_alog, dtb=d_dtb, conv=d_conv)
    return dh_in, wgrads, small


def _loss_head(h, target, fnorm):
    d = h.shape[1]

    def fn(hv, tv, fw):
        def lossf(hh, ww):
            y = hh * lax.rsqrt(jnp.mean(hh * hh, axis=-1, keepdims=True) + NORM_EPS) * ww
            return 0.5 * jnp.sum(jnp.mean(jnp.square(y - tv), axis=-1))

        val, (dh, dw) = jax.value_and_grad(lossf, argnums=(0, 1))(hv, fw)
        return (dh,), (jnp.full((1, LANES), val, F32), dw)

    return _rowwise("loss_head", fn, [h, target], [fnorm], [(d, F32)], [(1, LANES), (1, d)])


def _row(v):
    return v.reshape(1, -1)


def _pad_lanes(v, offset):
    return jnp.pad(v.reshape(1, -1), ((0, 0), (offset, LANES - offset - v.shape[0])))


_UP_SLOTS = dict(ffn1_wg=0, ffn1_wu=1, ffn2_wg=2, ffn2_wu=3)
_DOWN_SLOTS = dict(ffn1_wd=0, ffn2_wd=1)


def _local_step(x2, target, mod, layer_weights, small, on_layer_grads):
    depth = mod.shape[0]
    d = x2.shape[1]
    h = x2
    saved = []
    mods = []
    up = lambda l, nm: _UP_SLOTS[nm]
    down = lambda l, nm: _DOWN_SLOTS[nm]
    for l in range(depth):
        m9 = [_row(mod[l, i * d:(i + 1) * d]) for i in range(N_ADA)]
        sp = dict(conv8=jnp.pad(small["conv_w"][l], ((0, 8 - DN_CONV), (0, 0))),
                  alog=_pad_lanes(small["a_log"][l], DN_HEADS), dtb=_pad_lanes(small["dt_bias"][l], DN_HEADS),
                  dn=_row(small["dn_norm"][l]))
        ga, gb, w = layer_weights(l, h)
        h0 = h
        h1, sv1 = _ffn_fwd(f"l{l}_ffn1", h0, _row(small["ln_ffn1"][l]), m9[0], m9[1], m9[2], ga, up(l, "ffn1_wg"),
                           up(l, "ffn1_wu"), gb, down(l, "ffn1_wd"), 0.5)
        h2, sv2 = _mixer_fwd(f"l{l}_mix", h1, _row(small["ln_mix"][l]), m9[3], m9[4], m9[5], w, sp)
        h3, sv3 = _ffn_fwd(f"l{l}_ffn2", h2, _row(small["ln_ffn2"][l]), m9[6], m9[7], m9[8], ga, up(l, "ffn2_wg"),
                           up(l, "ffn2_wu"), gb, down(l, "ffn2_wd"), 0.5)
        saved.append((h0, h1, h2, sv1, sv2, sv3, sp, ga, gb, w))
        mods.append(m9)
        h = h3
    dh, loss_part, d_fnorm = _loss_head(h, target, _row(small["final_norm"]))
    sgrads, dmods = [], []
    token = None
    for l in reversed(range(depth)):
        h0, h1, h2, sv1, sv2, sv3, sp, ga, gb, w = saved[l]
        m9 = mods[l] if token is None else [r + token for r in mods[l]]
        dh, g3, s3 = _ffn_bwd(f"l{l}_ffn2", h2, dh, sv3, _row(small["ln_ffn2"][l]), m9[6], m9[7], m9[8], ga,
                              up(l, "ffn2_wg"), up(l, "ffn2_wu"), gb, down(l, "ffn2_wd"), 0.5)
        dh, g2, s2 = _mixer_bwd(f"l{l}_mix", h1, dh, sv2, _row(small["ln_mix"][l]), m9[3], m9[4], m9[5], w, sp)
        dh, g1, s1 = _ffn_bwd(f"l{l}_ffn1", h0, dh, sv1, _row(small["ln_ffn1"][l]), m9[0], m9[1], m9[2], ga,
                              up(l, "ffn1_wg"), up(l, "ffn1_wu"), gb, down(l, "ffn1_wd"), 0.5)
        token = on_layer_grads(l, dict(ffn1_wg=g1["wg"], ffn1_wu=g1["wu"], ffn1_wd=g1["wd"], ffn2_wg=g3["wg"],
                                       ffn2_wu=g3["wu"], ffn2_wd=g3["wd"], **g2))
        dmods.append(jnp.concatenate([s1["sh"], s1["sc"], s1["gt"], s2["sh"], s2["sc"], s2["gt"],
                                      s3["sh"], s3["sc"], s3["gt"]], axis=1))
        sgrads.append(dict(ln_ffn1=s1["ln"][0], ln_mix=s2["ln"][0], ln_ffn2=s3["ln"][0],
                           a_log=s2["alog"][0, DN_HEADS:2 * DN_HEADS], dt_bias=s2["dtb"][0, DN_HEADS:2 * DN_HEADS],
                           dn_norm=s2["dn"][0], conv_w=s2["conv"][:DN_CONV]))
    sgrads.reverse()
    dmods.reverse()
    return loss_part[0, 0], dh, jnp.concatenate(dmods, axis=0), sgrads, d_fnorm[0]


def _flip(v, bit):
    return 1 - v if bit else v


def _allgather8(name, x):
    r, c = x.shape

    def body(x_ref, out_ref, send_sems, recv_sems, local_sem):
        mx, my, mc = lax.axis_index("x"), lax.axis_index("y"), lax.axis_index("c")
        me = 4 * mx + 2 * my + mc
        mine = pltpu.make_async_copy(x_ref, out_ref.at[me], local_sem)
        mine.start()
        sends = []
        for k in range(1, 8):
            peer = (_flip(mx, k & 4), _flip(my, k & 2), _flip(mc, k & 1))
            cp = pltpu.make_async_remote_copy(src_ref=x_ref, dst_ref=out_ref.at[me], send_sem=send_sems.at[k - 1],
                                              recv_sem=recv_sems.at[k - 1], device_id=peer, device_id_type=MESH)
            cp.start()
            sends.append(cp)
        for k in range(1, 8):
            peer = (_flip(mx, k & 4), _flip(my, k & 2), _flip(mc, k & 1))
            src = 4 * peer[0] + 2 * peer[1] + peer[2]
            pltpu.make_async_remote_copy(src_ref=x_ref, dst_ref=out_ref.at[src], send_sem=send_sems.at[k - 1],
                                         recv_sem=recv_sems.at[k - 1], device_id=peer, device_id_type=MESH).wait_recv()
        for cp in sends:
            cp.wait_send()
        mine.wait()

    return pl.pallas_call(
        body, name=name, out_shape=jax.ShapeDtypeStruct((8, r, c), x.dtype),
        in_specs=[pl.BlockSpec(memory_space=pltpu.VMEM)], out_specs=pl.BlockSpec(memory_space=pltpu.VMEM),
        scratch_shapes=[pltpu.SemaphoreType.DMA((7,)), pltpu.SemaphoreType.DMA((7,)), pltpu.SemaphoreType.DMA],
        compiler_params=_params(9 * _nbytes((r, c), x.dtype)),
    )(x)


def _chip_peers(mx, my):
    chips = [(1 - mx, my), (mx, 1 - my), (1 - mx, 1 - my)]
    return chips, [2 * cx + cy for (cx, cy) in chips]


_ANY = pl.BlockSpec(memory_space=pl.ANY)


def _row_half(mc, r):
    return pl.ds(pl.multiple_of(mc * (r // 2), 16), r // 2)


def _gather_groups(name, shards):
    ng = len(shards)

    def body(*refs):
        xs, outs = refs[:ng], refs[ng:2 * ng]
        send_sems, recv_sems = refs[2 * ng:]
        mx, my, mc = lax.axis_index("x"), lax.axis_index("y"), lax.axis_index("c")
        j = 2 * mx + my
        chips, idxs = _chip_peers(mx, my)
        sib = (mx, my, 1 - mc)

        def copy(k, src, dst, to):
            return pltpu.make_async_remote_copy(src_ref=src, dst_ref=dst, send_sem=send_sems.at[k],
                                                recv_sem=recv_sems.at[k], device_id=to, device_id_type=MESH)

        first, passed = [], []
        for g in range(ng):
            mine = _row_half(mc, shards[g].shape[1])
            for t, chip in enumerate(chips):
                cp = copy(6 * g + t, xs[g].at[:, mine], outs[g].at[j, :, mine], (*chip, mc))
                cp.start()
                first.append(cp)
        for g in range(ng):
            mine = _row_half(mc, shards[g].shape[1])
            for t, chip in enumerate(chips):
                landed = outs[g].at[idxs[t], :, mine]
                copy(6 * g + t, landed, landed, (*chip, mc)).wait_recv()
                fwd = copy(6 * g + 3 + t, landed, landed, sib)
                fwd.start()
                passed.append(fwd)
        for g in range(ng):
            theirs_half = _row_half(1 - mc, shards[g].shape[1])
            for t in range(3):
                theirs = outs[g].at[idxs[t], :, theirs_half]
                copy(6 * g + 3 + t, theirs, theirs, sib).wait_recv()
        for cp in first + passed:
            cp.wait_send()

    outs = pl.pallas_call(
        body, name=name, out_shape=[jax.ShapeDtypeStruct((4,) + x.shape, x.dtype) for x in shards],
        in_specs=[_ANY] * ng, out_specs=[_ANY] * ng,
        scratch_shapes=[pltpu.SemaphoreType.DMA((6 * ng,)), pltpu.SemaphoreType.DMA((6 * ng,))],
    )(*shards)
    return _place_own_slab(outs, shards)


def _place_own_slab(outs, shards):
    chip = 2 * lax.axis_index("x") + lax.axis_index("y")
    return [lax.dynamic_update_slice(o, x[None], (chip,) + (0,) * x.ndim) for o, x in zip(outs, shards)]


_HBM = pl.BlockSpec(memory_space=pltpu.HBM)
_SEM = pl.BlockSpec(memory_space=pltpu.SEMAPHORE)
_DATAFLOW = pltpu.SideEffectType.DATAFLOW_SIDE_EFFECTING


def _ici_gather_copies(src_refs, land_refs, send_sems, recv_sems, scatter=False):
    mx, my, mc = lax.axis_index("x"), lax.axis_index("y"), lax.axis_index("c")
    j = 2 * mx + my
    chips, idxs = _chip_peers(mx, my)
    sends, recvs = [], []
    for g, src in enumerate(src_refs):
        for t, chip in enumerate(chips):
            common = dict(send_sem=send_sems.at[3 * g + t], recv_sem=recv_sems.at[3 * g + t], device_id=(*chip, mc),
                          device_id_type=MESH)
            if scatter:
                out, to, frm = src.at[idxs[t]], land_refs[g].at[j], land_refs[g].at[idxs[t]]
            else:
                mine = _row_half(mc, src.shape[1])
                out, to, frm = src.at[:, mine], land_refs[g].at[j, :, mine], land_refs[g].at[idxs[t], :, mine]
            sends.append(pltpu.make_async_remote_copy(src_ref=out, dst_ref=to, **common))
            recvs.append(pltpu.make_async_remote_copy(src_ref=out, dst_ref=frm, **common))
    return sends, recvs


def _gather_start(name, shards, scatter=False):
    ng = len(shards)

    def body(*refs):
        srcs, lands = refs[:ng], refs[ng:2 * ng]
        send_sems, recv_sems = refs[2 * ng], refs[2 * ng + 1]
        token = refs[-1]
        sends, _ = _ici_gather_copies(srcs, lands, send_sems, recv_sems, scatter)
        for cp in sends:
            cp.start()
        token[...] = jnp.zeros(token.shape, token.dtype)

    land_shape = lambda x: x.shape if scatter else (4,) + x.shape
    srcs = [pltpu.with_memory_space_constraint(x, pltpu.HBM) for x in shards]
    lands = [pltpu.with_memory_space_constraint(lax.empty(land_shape(x), x.dtype), pltpu.HBM) for x in shards]
    res = pl.pallas_call(
        body, name=name,
        out_shape=(pltpu.SemaphoreType.DMA((3 * ng,)), pltpu.SemaphoreType.DMA((3 * ng,)),
                   *[pltpu.HBM(x.shape, x.dtype) for x in srcs], *[pltpu.HBM(x.shape, x.dtype) for x in lands],
                   jax.ShapeDtypeStruct((8, LANES), F32)),
        in_specs=[_HBM] * (2 * ng),
        out_specs=(_SEM, _SEM, *[_HBM] * (2 * ng), pl.BlockSpec(memory_space=pltpu.VMEM)),
        input_output_aliases={i: 2 + i for i in range(2 * ng)},
        compiler_params=pltpu.CompilerParams(has_side_effects=_DATAFLOW),
    )(*srcs, *lands)
    return dict(send_sems=res[0], recv_sems=res[1], srcs=list(res[2:2 + ng]), lands=list(res[2 + ng:2 + 2 * ng]),
                token=res[-1])


def _gather_wait(name, started, after, scatter=False):
    ng = len(started["srcs"])

    def body(*refs):
        srcs, lands = refs[:ng], refs[ng:2 * ng]
        send_sems, recv_sems = refs[2 * ng], refs[2 * ng + 1]
        sends, recvs = _ici_gather_copies(srcs, lands, send_sems, recv_sems, scatter)
        for cp in sends:
            cp.wait_send()
        for cp in recvs:
            cp.wait_recv()

    res = pl.pallas_call(
        body, name=name,
        out_shape=[pltpu.HBM(x.shape, x.dtype) for x in started["srcs"] + started["lands"]],
        in_specs=[_HBM] * (2 * ng) + [_SEM, _SEM, _ANY], out_specs=[_HBM] * (2 * ng),
        input_output_aliases={i: i for i in range(2 * ng)},
        compiler_params=pltpu.CompilerParams(has_side_effects=_DATAFLOW),
    )(*started["srcs"], *started["lands"], started["send_sems"], started["recv_sems"], after)
    return list(res[:ng]), list(res[ng:])


def _pair_forward_groups(name, lands, shards):
    ng = len(lands)

    def body(*refs):
        ins, outs = refs[:ng], refs[ng:2 * ng]
        send_sems, recv_sems = refs[2 * ng:]
        mx, my, mc = lax.axis_index("x"), lax.axis_index("y"), lax.axis_index("c")
        _, idxs = _chip_peers(mx, my)
        sib = (mx, my, 1 - mc)
        cps = []
        for g in range(ng):
            mine = _row_half(mc, lands[g].shape[2])
            for t in range(3):
                cp = pltpu.make_async_remote_copy(src_ref=ins[g].at[idxs[t], :, mine], dst_ref=outs[g].at[idxs[t], :, mine],
                                                  send_sem=send_sems.at[3 * g + t], recv_sem=recv_sems.at[3 * g + t],
                                                  device_id=sib, device_id_type=MESH)
                cp.start()
                cps.append(cp)
        for g in range(ng):
            theirs = _row_half(1 - mc, lands[g].shape[2])
            for t in range(3):
                pltpu.make_async_remote_copy(src_ref=ins[g].at[idxs[t], :, theirs], dst_ref=outs[g].at[idxs[t], :, theirs],
                                             send_sem=send_sems.at[3 * g + t], recv_sem=recv_sems.at[3 * g + t],
                                             device_id=sib, device_id_type=MESH).wait_recv()
        for cp in cps:
            cp.wait_send()

    outs = pl.pallas_call(
        body, name=name, out_shape=[jax.ShapeDtypeStruct(x.shape, x.dtype) for x in lands],
        in_specs=[_ANY] * ng, out_specs=[_ANY] * ng, input_output_aliases={i: i for i in range(ng)},
        scratch_shapes=[pltpu.SemaphoreType.DMA((3 * ng,)), pltpu.SemaphoreType.DMA((3 * ng,))],
    )(*lands)
    return _place_own_slab(outs, shards)


def _pair_swap_groups(name, gs):
    ng = len(gs)

    def body(*refs):
        xs, outs = refs[:ng], refs[ng:2 * ng]
        send_sems, recv_sems = refs[2 * ng:]
        mx, my, mc = lax.axis_index("x"), lax.axis_index("y"), lax.axis_index("c")
        cps = []
        for g in range(ng):
            cp = pltpu.make_async_remote_copy(src_ref=xs[g].at[:, :, _row_half(1 - mc, gs[g].shape[2])], dst_ref=outs[g],
                                              send_sem=send_sems.at[g], recv_sem=recv_sems.at[g],
                                              device_id=(mx, my, 1 - mc), device_id_type=MESH)
            cp.start()
            cps.append(cp)
        for cp in cps:
            cp.wait()

    return pl.pallas_call(
        body, name=name,
        out_shape=[jax.ShapeDtypeStruct(x.shape[:2] + (x.shape[2] // 2, x.shape[3]), x.dtype) for x in gs],
        in_specs=[_ANY] * ng, out_specs=[_ANY] * ng,
        scratch_shapes=[pltpu.SemaphoreType.DMA((ng,)), pltpu.SemaphoreType.DMA((ng,))],
    )(*gs)


def _chip_scatter_groups(name, ps):
    ng = len(ps)

    def body(*refs):
        xs, outs = refs[:ng], refs[ng:2 * ng]
        send_sems, recv_sems = refs[2 * ng:]
        mx, my, mc = lax.axis_index("x"), lax.axis_index("y"), lax.axis_index("c")
        j = 2 * mx + my
        chips, idxs = _chip_peers(mx, my)
        sends = []
        for g in range(ng):
            for t, chip in enumerate(chips):
                cp = pltpu.make_async_remote_copy(src_ref=xs[g].at[idxs[t]], dst_ref=outs[g].at[j],
                                                  send_sem=send_sems.at[3 * g + t], recv_sem=recv_sems.at[3 * g + t],
                                                  device_id=(*chip, mc), device_id_type=MESH)
                cp.start()
                sends.append(cp)
        for g in range(ng):
            for t, chip in enumerate(chips):
                pltpu.make_async_remote_copy(src_ref=xs[g].at[idxs[t]], dst_ref=outs[g].at[idxs[t]],
                                             send_sem=send_sems.at[3 * g + t], recv_sem=recv_sems.at[3 * g + t],
                                             device_id=(*chip, mc), device_id_type=MESH).wait_recv()
        for cp in sends:
            cp.wait_send()

    outs = pl.pallas_call(
        body, name=name, out_shape=[jax.ShapeDtypeStruct(x.shape, x.dtype) for x in ps],
        in_specs=[_ANY] * ng, out_specs=[_ANY] * ng,
        scratch_shapes=[pltpu.SemaphoreType.DMA((3 * ng,)), pltpu.SemaphoreType.DMA((3 * ng,))],
    )(*ps)
    return _place_own_part(outs, ps)


def _place_own_part(outs, ps):
    chip = 2 * lax.axis_index("x") + lax.axis_index("y")
    return [lax.dynamic_update_slice(o, lax.dynamic_index_in_dim(x, chip, 0, keepdims=True), (chip,) + (0,) * (x.ndim - 1))
            for o, x in zip(outs, ps)]


def _pair_merge_groups(name, fs):
    ng = len(fs)

    def body(*refs):
        xs, outs = refs[:ng], refs[ng:2 * ng]
        send_sems, recv_sems = refs[2 * ng:]
        mx, my, mc = lax.axis_index("x"), lax.axis_index("y"), lax.axis_index("c")
        cps = []
        for g in range(ng):
            mine = _row_half(mc, 2 * fs[g].shape[1])
            cp = pltpu.make_async_remote_copy(src_ref=xs[g], dst_ref=outs[g].at[:, mine], send_sem=send_sems.at[g],
                                              recv_sem=recv_sems.at[g], device_id=(mx, my, 1 - mc), device_id_type=MESH)
            cp.start()
            cps.append(cp)
        for g in range(ng):
            theirs = outs[g].at[:, _row_half(1 - mc, 2 * fs[g].shape[1])]
            pltpu.make_async_remote_copy(src_ref=xs[g], dst_ref=theirs, send_sem=send_sems.at[g],
                                         recv_sem=recv_sems.at[g], device_id=(mx, my, 1 - mc),
                                         device_id_type=MESH).wait_recv()
        for cp in cps:
            cp.wait_send()

    outs = pl.pallas_call(
        body, name=name,
        out_shape=[jax.ShapeDtypeStruct((x.shape[0], 2 * x.shape[1], x.shape[2]), x.dtype) for x in fs],
        in_specs=[_ANY] * ng, out_specs=[_ANY] * ng,
        scratch_shapes=[pltpu.SemaphoreType.DMA((ng,)), pltpu.SemaphoreType.DMA((ng,))],
    )(*fs)
    mc = lax.axis_index("c")
    return [lax.dynamic_update_slice(o, x, (0, mc * x.shape[1], 0)) for o, x in zip(outs, fs)]


def _block_rows(r, w, itemsize=4, budget=4 << 20):
    for c in (r, 2048, 1024, 512, 256, 128, 64, 32, 16):
        if c <= r and r % c == 0 and c * w * itemsize <= budget:
            return c
    return r


def _pair_sum(name, g, got, cidx):
    ns, t, r, w = g.shape
    rh = r // 2
    bm = _block_rows(rh, w)
    nb = rh // bm

    def body(c_ref, a_ref, b_ref, o_ref):
        o_ref[...] = (a_ref[...].astype(F32) + b_ref[...].astype(F32)).astype(o_ref.dtype)

    blk = (None, None, bm, w)
    return pl.pallas_call(
        body, name=name,
        grid_spec=pltpu.PrefetchScalarGridSpec(
            num_scalar_prefetch=1, grid=(ns, t, nb),
            in_specs=[pl.BlockSpec(blk, lambda s, tt, i, c: (s, tt, c[0] * nb + i, 0)),
                      pl.BlockSpec(blk, lambda s, tt, i, c: (s, tt, i, 0))],
            out_specs=pl.BlockSpec(blk, lambda s, tt, i, c: (s, tt, i, 0))),
        out_shape=jax.ShapeDtypeStruct((ns, t, rh, w), BF16),
        compiler_params=_params(3 * _nbytes((bm, w), F32)),
    )(cidx, g, got)


def _chip_sum(name, p):
    ns, th, r, w = p.shape
    bm = _block_rows(r, w, budget=2 << 20)

    def body(p_ref, o_ref):
        acc = p_ref[0].astype(F32)
        for s in range(1, ns):
            acc = acc + p_ref[s].astype(F32)
        o_ref[...] = acc

    return pl.pallas_call(
        body, name=name, grid=(th, r // bm),
        in_specs=[pl.BlockSpec((ns, None, bm, w), lambda tt, i: (0, tt, i, 0))],
        out_specs=pl.BlockSpec((None, bm, w), lambda tt, i: (tt, i, 0)),
        out_shape=jax.ShapeDtypeStruct((th, r, w), F32),
        compiler_params=_params(ns * _nbytes((bm, w), BF16) + 2 * _nbytes((bm, w), F32)),
    )(p)


def _sum_leading(name, x):
    n = x.shape[0]

    def body(p_ref, o_ref):
        acc = p_ref[0]
        for s in range(1, n):
            acc = acc + p_ref[s]
        o_ref[...] = acc

    return pl.pallas_call(body, name=name, out_shape=jax.ShapeDtypeStruct(x.shape[1:], F32),
                          compiler_params=_params(2 * _nbytes(x.shape, F32)))(x)


def _reduce_scatter_begin(tag, gs, overlap):
    cidx = lax.axis_index("c").astype(jnp.int32).reshape(1)
    got = _pair_swap_groups(tag + "_pair_swap", gs)
    pair = [_pair_sum(f"{tag}_pair_sum{i}", g, r_, cidx) for i, (g, r_) in enumerate(zip(gs, got))]
    if overlap:
        return _gather_start(tag + "_scatter_start", pair, scatter=True)
    return _chip_scatter_groups(tag + "_chip_scatter", pair)


def _reduce_scatter_end(tag, state, overlap, after):
    if overlap:
        srcs, lands = _gather_wait(tag + "_scatter_wait", state, after, scatter=True)
        state = _place_own_part(lands, srcs)
    fin = [_chip_sum(f"{tag}_chip_sum{i}", p) for i, p in enumerate(state)]
    return _pair_merge_groups(tag + "_pair_merge", fin)


_GROUPS = ((("ffn1_wg", "ffn1_wu", "ffn2_wg", "ffn2_wu"), 1), (("ffn1_wd", "ffn2_wd"), 0), (("w_a",), 0),
           (("w_o",), 0), (("w_in",), 1), (("w_b",), 1))


def _shard_major(g, ax):
    k, n = g.shape
    if ax == 0:
        return g.reshape(4, k // 4, n)
    return g.reshape(k, 4, n // 4).transpose(1, 0, 2)


def _in_cols(d):
    o1 = 3 * DN_WIDTH
    o2 = o1 + DN_WIDTH
    o3 = o2 + 2 * DN_HEADS
    o4 = o3 + 3 * DA_WIDTH
    return dict(wq=(0, o1), wz=(o1, o2), wba=(o2, o3), wda=(o3, o4), wg=(o4, o4 + 2 * d))


def _mixer_weights(w_in, w_a, w_b, w_o, d):
    w = {k: w_in[:, a:b] for k, (a, b) in _in_cols(d).items()}
    w["wba"] = jnp.pad(w["wba"], ((0, 0), (0, LANES - 2 * DN_HEADS)))
    w["w_a"], w["w_b"], w["w_o"] = w_a, w_b, w_o
    return w


def _w_in_grad(wg):
    return jnp.concatenate([wg["wq"], wg["wz"], wg["wba"][:, :2 * DN_HEADS], wg["wda"], wg["wg"]], axis=1)


def _adam_math(wv, gv, mv, vv):
    mn = ADAM_B1 * mv + (1.0 - ADAM_B1) * gv
    vn = ADAM_B2 * vv + (1.0 - ADAM_B2) * jnp.square(gv)
    m_hat = mn / (1.0 - ADAM_B1 ** ADAM_STEP)
    v_hat = vn / (1.0 - ADAM_B2 ** ADAM_STEP)
    delta = -ADAM_LR * (m_hat / (jnp.sqrt(v_hat) + ADAM_EPS) + ADAM_WD * wv)
    return delta, mn, vn


def _adamw(name, w, g, m, v):
    shape = w.shape
    cols = shape[-1]
    w2, g2, m2, v2 = (t.reshape(-1, cols) for t in (w, g, m, v))
    rows = w2.shape[0]
    bm = _pick(rows, (256, 128, 64, 32, 16, 8)) if rows >= 8 else rows
    delta, mn, vn = _rowwise(name, lambda *t: (_adam_math(*t), ()), [w2, g2, m2, v2], [], [(cols, F32)] * 3, bm=bm)
    return delta.reshape(shape), mn.reshape(shape), vn.reshape(shape)


def _adamw_stacked(name, w, m, v, gstacks, slot):
    depth, r, cdim = w.shape
    bm = _block_rows(r, cdim, budget=1 << 20)

    def body(w_ref, m_ref, v_ref, *rest):
        g_refs, (go_ref, d_ref, mo_ref, vo_ref) = rest[:depth], rest[depth:]
        layer = pl.program_id(0)
        gv = g_refs[0][...]
        for l in range(1, depth):
            gv = jnp.where(layer == l, g_refs[l][...], gv)
        go_ref[...] = gv
        d_ref[...], mo_ref[...], vo_ref[...] = _adam_math(w_ref[...], gv, m_ref[...], v_ref[...])

    nat = pl.BlockSpec((None, bm, cdim), lambda l, i: (l, i, 0))
    return pl.pallas_call(
        body, name=name, grid=(depth, r // bm),
        in_specs=[nat, nat, nat] + [pl.BlockSpec((None, bm, cdim), lambda l, i: (slot, i, 0))] * depth,
        out_specs=[nat] * 4, out_shape=[jax.ShapeDtypeStruct(w.shape, F32)] * 4,
        compiler_params=_params((7 + depth) * _nbytes((bm, cdim), F32)),
    )(w, m, v, *gstacks)


def kernel(x, c, ada_w, ada_b, ln_ffn1, ln_mix, ln_ffn2, ffn1_wg, ffn1_wu, ffn1_wd, w_in, conv_w, a_log, dt_bias, dn_norm, w_a, w_b, w_o, ffn2_wg, ffn2_wu, ffn2_wd, final_norm, loss_target, m_ada_w, m_ada_b, m_ln_ffn1, m_ln_mix, m_ln_ffn2, m_ffn1_wg, m_ffn1_wu, m_ffn1_wd, m_w_in, m_conv_w, m_a_log, m_dt_bias, m_dn_norm, m_w_a, m_w_b, m_w_o, m_ffn2_wg, m_ffn2_wu, m_ffn2_wd, m_final_norm, v_ada_w, v_ada_b, v_ln_ffn1, v_ln_mix, v_ln_ffn2, v_ffn1_wg, v_ffn1_wu, v_ffn1_wd, v_w_in, v_conv_w, v_a_log, v_dt_bias, v_dn_norm, v_w_a, v_w_b, v_w_o, v_ffn2_wg, v_ffn2_wu, v_ffn2_wd, v_final_norm):
    names = ["ada_w", "ada_b", "ln_ffn1", "ln_mix", "ln_ffn2", "ffn1_wg", "ffn1_wu", "ffn1_wd", "w_in", "conv_w",
             "a_log", "dt_bias", "dn_norm", "w_a", "w_b", "w_o", "ffn2_wg", "ffn2_wu", "ffn2_wd", "final_norm"]
    wts = dict(zip(names, (ada_w, ada_b, ln_ffn1, ln_mix, ln_ffn2, ffn1_wg, ffn1_wu, ffn1_wd, w_in, conv_w, a_log,
                           dt_bias, dn_norm, w_a, w_b, w_o, ffn2_wg, ffn2_wu, ffn2_wd, final_norm)))
    mom = dict(zip(names, (m_ada_w, m_ada_b, m_ln_ffn1, m_ln_mix, m_ln_ffn2, m_ffn1_wg, m_ffn1_wu, m_ffn1_wd, m_w_in,
                           m_conv_w, m_a_log, m_dt_bias, m_dn_norm, m_w_a, m_w_b, m_w_o, m_ffn2_wg, m_ffn2_wu,
                           m_ffn2_wd, m_final_norm)))
    var = dict(zip(names, (v_ada_w, v_ada_b, v_ln_ffn1, v_ln_mix, v_ln_ffn2, v_ffn1_wg, v_ffn1_wu, v_ffn1_wd, v_w_in,
                           v_conv_w, v_a_log, v_dt_bias, v_dn_norm, v_w_a, v_w_b, v_w_o, v_ffn2_wg, v_ffn2_wu,
                           v_ffn2_wd, v_final_norm)))
    _, s, d = x.shape
    depth = ada_w.shape[0]
    mx, my, mc = lax.axis_index("x"), lax.axis_index("y"), lax.axis_index("c")
    chip = 2 * mx + my
    me = 2 * chip + mc
    nshard = ada_w.shape[2]

    cact = _rowwise("c_silu", lambda cv: ((_silu(cv),), ()), [jnp.pad(c, ((0, 7), (0, 0)))], [], [(d, F32)], bm=8)[0]
    c_all = _allgather8("ag_c", cact)[:, 0, :]
    conv_all = _allgather8("ag_conv", jnp.pad(conv_w.reshape(depth * DN_CONV, -1), ((0, 8 - depth * DN_CONV), (0, 0))))
    conv_full = jnp.concatenate([conv_all[2 * j, :depth * DN_CONV] for j in range(4)], axis=1)
    conv_full = conv_full.reshape(depth, DN_CONV, 3 * DN_WIDTH)
    layer_shards = [[jnp.stack([wts[nm][l].astype(BF16) for nm in nms], axis=0) for nms, _ in _GROUPS]
                    for l in range(depth)]
    gathered0 = _gather_groups("ag_weights0", layer_shards[0])
    rows_of = lambda st: st[:, 0].reshape(-1, st.shape[-1])
    cols_of = lambda st: jnp.concatenate([st[j, 0] for j in range(4)], axis=1)

    def layer_weights(l, after):
        if l == 0:
            got = gathered0
        else:
            srcs, lands = _gather_wait(f"ag_weights{l}_wait", started[l], after)
            got = _pair_forward_groups(f"ag_weights{l}_pair", lands, srcs)
        ga, gb, g_wa, g_wo, g_win, g_wb = got
        return ga, gb, _mixer_weights(cols_of(g_win), rows_of(g_wa), cols_of(g_wb), rows_of(g_wo), d)

    c16 = jnp.pad(c_all, ((0, 8), (0, 0))).astype(BF16)
    parts = []
    for l in range(depth):
        bias = lax.dynamic_slice(ada_b[l], (chip * nshard,), (nshard,)).reshape(1, nshard)
        (mp,) = _matmul(f"ada_fwd{l}", c16, ada_w[l].astype(BF16), epi_bcast=[bias], epi=lambda acc, b: (acc + b,))
        parts.append(mp)
    mod_all = _allgather8("ag_mod", jnp.concatenate(parts, axis=0))
    mod_rows = jnp.concatenate([mod_all[2 * j] for j in range(4)], axis=1)
    mod = jnp.stack([lax.dynamic_index_in_dim(mod_rows, l * 16 + me, axis=0, keepdims=False) for l in range(depth)])

    gathered0, later, mod, conv_full = lax.optimization_barrier((gathered0, layer_shards[1:], mod, conv_full))
    started = {l: _gather_start(f"ag_weights{l}_start", later[l - 1]) for l in range(1, depth)}
    for st in started.values():
        mod = mod + st["token"][0, 0]
    small = dict(conv_w=conv_full, a_log=a_log, dt_bias=dt_bias, dn_norm=dn_norm, ln_ffn1=ln_ffn1, ln_mix=ln_mix,
                 ln_ffn2=ln_ffn2, final_norm=final_norm)
    ffn_names = _GROUPS[0][0] + _GROUPS[1][0]
    rs_state = {}

    def on_layer_grads(l, wg):
        wg["w_in"] = _w_in_grad(wg)
        gs = [jnp.stack([wg[nm] if nm in ffn_names else _shard_major(wg[nm], ax) for nm in nms], axis=1)
              for nms, ax in _GROUPS]
        rs_state[l] = _reduce_scatter_begin(f"rs{l}", gs, overlap=l > 0)
        return rs_state[l]["token"][0, 0] if l > 0 else None

    loss_part, dx, dmod, sgrads, d_fnorm = _local_step(x[0], loss_target[0], mod, layer_weights, small,
                                                       on_layer_grads)
    reduced = [_reduce_scatter_end(f"rs{l}", rs_state[l], l > 0, dx) for l in range(depth)]

    dmod_all = _allgather8("ag_dmod", jnp.pad(dmod, ((0, 8 - depth), (0, 0))))
    g_ada_w, g_ada_b = [], []
    for l in range(depth):
        dm_l = dmod_all[:, l, :]
        (gb_l,) = _rowwise(f"ada_b_grad{l}", lambda v: ((), (jnp.sum(v, axis=0, keepdims=True),)), [dm_l], [], [],
                           [(1, N_ADA * d)], bm=8)
        g_ada_b.append(gb_l[0])
        dm_sh = lax.dynamic_slice(dm_l, (0, chip * nshard), (8, nshard))
        (gw_l,) = _matmul(f"ada_w_grad{l}", c16, jnp.pad(dm_sh, ((0, 8), (0, 0))).astype(BF16), ta=True)
        g_ada_w.append(gw_l)
    grads = dict(ada_w=jnp.stack(g_ada_w), ada_b=jnp.stack(g_ada_b))

    smalls = [loss_part.reshape(1), d_fnorm]
    for l in range(depth):
        sg = sgrads[l]
        smalls += [sg["ln_ffn1"], sg["ln_mix"], sg["ln_ffn2"], sg["a_log"], sg["dt_bias"], sg["dn_norm"],
                   sg["conv_w"].reshape(-1)]
    sizes = [t.shape[0] for t in smalls]
    tile = 8 * LANES
    flat = jnp.concatenate([jnp.pad(t, (0, (-t.shape[0]) % tile)).reshape(-1, LANES) for t in smalls], axis=0)
    tot = _sum_leading("small_sum", _allgather8("ag_small", flat))
    offs, acc = [], 0
    for n_ in sizes:
        offs.append(acc)
        acc += -(-n_ // tile) * 8
    take = lambda i: tot[offs[i]:offs[i] + -(-sizes[i] // tile) * 8].reshape(-1)[:sizes[i]]
    loss = take(0)[0]
    grads["final_norm"] = take(1)
    per = 7
    for key_i, key in enumerate(["ln_ffn1", "ln_mix", "ln_ffn2", "a_log", "dt_bias", "dn_norm"]):
        grads[key] = jnp.stack([take(2 + per * l + key_i) for l in range(depth)])
    conv_g = jnp.stack([take(2 + per * l + 6).reshape(DN_CONV, 3 * DN_WIDTH) for l in range(depth)])
    csh = conv_w.shape[2]
    grads["conv_w"] = lax.dynamic_slice(conv_g, (0, 0, chip * csh), (depth, DN_CONV, csh))

    deltas, new_m, new_v = {}, {}, {}
    for gi, (nms, _) in enumerate(_GROUPS):
        for q, nm in enumerate(nms):
            grads[nm], deltas[nm], new_m[nm], new_v[nm] = _adamw_stacked(
                "adamw_" + nm, wts[nm], mom[nm], var[nm], [reduced[l][gi] for l in range(depth)], q)

    for name in names:
        if name in deltas:
            continue
        wv, gv, mv, vv = wts[name], grads[name], mom[name], var[name]
        if wv.ndim == 1:
            wv, gv, mv, vv = (t.reshape(-1, LANES) for t in (wv, gv, mv, vv))
        dl, mn, vn = _adamw("adamw_" + name, wv, gv, mv, vv)
        deltas[name], new_m[name], new_v[name] = (t.reshape(wts[name].shape) for t in (dl, mn, vn))
    return (loss, dx.reshape(1, s, d), *[grads[n_] for n_ in names], *[deltas[n_] for n_ in names],
            *[new_m[n_] for n_ in names], *[new_v[n_] for n_ in names])
```

```python
import functools

import jax
import jax.numpy as jnp
from jax import lax
from jax.experimental import pallas as pl
from jax.experimental.pallas import tpu as pltpu

F32 = jnp.float32
BF16 = jnp.bfloat16
MESH = pl.DeviceIdType.MESH

NORM_EPS = 1e-6
DN_HEADS, DN_DIM, DN_CHUNK, DN_CONV = 8, 128, 64, 4
DN_WIDTH = DN_HEADS * DN_DIM
DA_HEADS, DA_DIM, DA_BLOCK = 12, 64, 128
DA_WIDTH = DA_HEADS * DA_DIM
DA_PATTERNS = ((128, 1), (512, 4), (2048, 16))
ALIBI_MAX_EXP = 8.0
N_ADA = 9
LANES = 128
V7X_VMEM_BYTES = 64 << 20
ADAM_LR, ADAM_B1, ADAM_B2, ADAM_EPS, ADAM_WD, ADAM_STEP = 0.001, 0.9, 0.999, 1e-08, 0.01, 10
NEG = -1e30
HI = lax.Precision.HIGHEST
NN = (((1,), (0,)), ((), ()))
NT = (((1,), (1,)), ((), ()))
TN = (((0,), (0,)), ((), ()))


def _nbytes(shape, dtype):
    n = 1
    for s in shape:
        n *= s
    return n * jnp.dtype(dtype).itemsize


def _params(block_bytes, scratch_bytes=0):
    need = 2 * block_bytes + scratch_bytes
    lim = min(max(need + need // 4 + (4 << 20), 32 << 20), V7X_VMEM_BYTES - (6 << 20))
    return pltpu.CompilerParams(vmem_limit_bytes=int(lim))


def _pick(n, cands):
    for c in cands:
        if c <= n and n % c == 0:
            return c
    return n


def _sigmoid(x):
    return jax.nn.sigmoid(x)


def _silu(x):
    return x * jax.nn.sigmoid(x)


def _softplus(x):
    return jnp.maximum(x, 0.0) + jnp.log(1.0 + jnp.exp(-jnp.abs(x)))


def _rowwise(name, fn, rows, bcast, row_outs, red_outs=(), bm=256):
    rows = [r if isinstance(r, tuple) else (r, r.shape[1], 0) for r in rows]
    s = rows[0][0].shape[0]
    bm = _pick(s, (bm, 128, 64, 32, 16, 8))
    nr, nb, no, nd = len(rows), len(bcast), len(row_outs), len(red_outs)
    in_specs = [pl.BlockSpec((bm, w), functools.partial(lambda i, ci: (i, ci), ci=ci)) for (_, w, ci) in rows]
    in_specs += [pl.BlockSpec(b.shape, lambda i: (0, 0)) for b in bcast]
    out_shape = [jax.ShapeDtypeStruct((s, w), dt) for (w, dt) in row_outs]
    out_shape += [jax.ShapeDtypeStruct((r, w), F32) for (r, w) in red_outs]
    out_specs = [pl.BlockSpec((bm, w), lambda i: (i, 0)) for (w, _) in row_outs]
    out_specs += [pl.BlockSpec((r, w), lambda i: (0, 0)) for (r, w) in red_outs]

    def body(*refs):
        ins = [r[...] for r in refs[:nr + nb]]
        outs = refs[nr + nb:nr + nb + no]
        reds = refs[nr + nb + no:]
        ov, rv = fn(*ins)
        for o, v in zip(outs, ov):
            o[...] = v.astype(o.dtype)
        if nd:
            @pl.when(pl.program_id(0) == 0)
            def _():
                for r in reds:
                    r[...] = jnp.zeros(r.shape, F32)
            for r, v in zip(reds, rv):
                r[...] += v.astype(F32)

    blk = sum(_nbytes((bm, w), a.dtype) for (a, w, _) in rows) + sum(_nbytes(b.shape, b.dtype) for b in bcast)
    blk += sum(_nbytes((bm, w), dt) for (w, dt) in row_outs) + sum(_nbytes(r, F32) for r in red_outs)
    res = pl.pallas_call(
        body, name=name, grid=(s // bm,), in_specs=in_specs, out_specs=out_specs, out_shape=out_shape,
        compiler_params=_params(3 * blk),
    )(*[a for (a, _, _) in rows], *bcast)
    return res


def _matmul(name, a, b, *, ta=False, tb=False, outs=(F32,), epi=None, epi_rows=(), epi_bcast=(),
            bm=None, bn=None, bk=None):
    if ta:
        k, m = a.shape
    else:
        m, k = a.shape
    n = b.shape[0] if tb else b.shape[1]
    assert (b.shape[1] if tb else b.shape[0]) == k, (name, a.shape, b.shape)
    if bm is None:
        bm = _pick(m, (1024, 1408, 768, 512, 384, 256, 128)) if ta else _pick(m, (1024, 512, 256, 128, 64, 32, 16))
    if bn is None:
        bn = _pick(n, (512, 384, 256, 128))
    if bk is None:
        bk = k if k <= 3072 else _pick(k, (2816, 2048, 1024, 512))
        if ta:
            bk = _pick(k, (1024, 512, 256, 128, 64, 32, 16))
    nk = k // bk
    dims = TN if ta else (NT if tb else NN)
    a_spec = pl.BlockSpec((bk, bm), lambda i, j, kk: (kk, i)) if ta else pl.BlockSpec((bm, bk), lambda i, j, kk: (i, kk))
    b_spec = pl.BlockSpec((bn, bk), lambda i, j, kk: (j, kk)) if tb else pl.BlockSpec((bk, bn), lambda i, j, kk: (kk, j))
    in_specs = [a_spec, b_spec]
    in_specs += [pl.BlockSpec((bm, bn), lambda i, j, kk: (i, j)) for _ in epi_rows]
    in_specs += [pl.BlockSpec((1, bn), lambda i, j, kk: (0, j)) for _ in epi_bcast]
    out_shape = [jax.ShapeDtypeStruct((m, n), dt) for dt in outs]
    out_specs = [pl.BlockSpec((bm, bn), lambda i, j, kk: (i, j)) for _ in outs]
    ner, neb, no = len(epi_rows), len(epi_bcast), len(outs)

    def body(*refs):
        a_ref, b_ref = refs[0], refs[1]
        extra = refs[2:2 + ner + neb]
        out_refs = refs[2 + ner + neb:2 + ner + neb + no]
        prod = lax.dot_general(a_ref[...], b_ref[...], dims, preferred_element_type=F32)

        def finish(acc):
            vals = epi(acc, *[r[...] for r in extra]) if epi is not None else (acc,)
            for o, v in zip(out_refs, vals):
                o[...] = v.astype(o.dtype)

        if nk == 1:
            finish(prod)
        else:
            acc_ref = refs[-1]
            kk = pl.program_id(2)

            @pl.when(kk == 0)
            def _():
                acc_ref[...] = prod

            @pl.when(kk > 0)
            def _():
                acc_ref[...] += prod

            @pl.when(kk == nk - 1)
            def _():
                finish(acc_ref[...])

    blk = _nbytes((bm, bk), a.dtype) + _nbytes((bk, bn), b.dtype)
    blk += sum(_nbytes((bm, bn), r.dtype) for r in epi_rows) + sum(_nbytes((bm, bn), dt) for dt in outs)
    scratch = [pltpu.VMEM((bm, bn), F32)] if nk > 1 else []
    res = pl.pallas_call(
        body, name=name, grid=(m // bm, n // bn, nk), in_specs=in_specs, out_specs=out_specs,
        out_shape=out_shape, scratch_shapes=scratch,
        compiler_params=_params(blk, 3 * _nbytes((bm, bn), F32)),
    )(a, b, *epi_rows, *epi_bcast)
    return res


def _mm_core(name, grid, nk, pairs, out_defs, acc_shape, epi=None, epi_ins=()):
    npair, nep, no = len(pairs), len(epi_ins), len(out_defs)

    def body(*refs):
        extra = refs[2 * npair:2 * npair + nep]
        out_refs = refs[2 * npair + nep:2 * npair + nep + no]
        prod = None
        for p in range(npair):
            d = lax.dot_general(refs[2 * p][...], refs[2 * p + 1][...], pairs[p][4], preferred_element_type=F32)
            prod = d if prod is None else prod + d

        def finish(acc):
            vals = epi(acc, *[r[...] for r in extra]) if epi is not None else (acc,)
            for o, v in zip(out_refs, vals):
                o[...] = v.astype(o.dtype)

        if nk == 1:
            finish(prod)
        else:
            acc_ref = refs[-1]
            kk = pl.program_id(2)

            @pl.when(kk == 0)
            def _():
                acc_ref[...] = prod

            @pl.when(kk > 0)
            def _():
                acc_ref[...] += prod

            @pl.when(kk == nk - 1)
            def _():
                finish(acc_ref[...])

    def blk_bytes(spec, dtype):
        return _nbytes([s for s in spec.block_shape if s is not None], dtype)

    blk = sum(blk_bytes(sa, a.dtype) + blk_bytes(sb, b.dtype) for (a, sa, b, sb, _) in pairs)
    blk += sum(blk_bytes(sp, arr.dtype) for (arr, sp) in epi_ins) + sum(blk_bytes(sp, dt) for (_, dt, sp) in out_defs)
    ins, in_specs = [], []
    for (a, sa, b, sb, _) in pairs:
        ins += [a, b]
        in_specs += [sa, sb]
    ins += [arr for (arr, _) in epi_ins]
    in_specs += [sp for (_, sp) in epi_ins]
    return pl.pallas_call(
        body, name=name, grid=grid, in_specs=in_specs, out_specs=[sp for (_, _, sp) in out_defs],
        out_shape=[jax.ShapeDtypeStruct(sh, dt) for (sh, dt, _) in out_defs],
        scratch_shapes=[pltpu.VMEM(acc_shape, F32)] if nk > 1 else [],
        compiler_params=_params(blk, 3 * _nbytes(acc_shape, F32)),
    )(*ins)


def _rms_mod(h, ln, sh, sc):
    n = h * lax.rsqrt(jnp.mean(h * h, axis=-1, keepdims=True) + NORM_EPS) * ln
    return n * (1.0 + sc) + sh


def _swiglu_act(g, u):
    return _silu(g.astype(F32)) * u.astype(F32)


def _dn_prep(yc, pba, alog, dtb):
    act = _silu(yc)
    parts = []
    for idx in range(2 * DN_HEADS):
        seg = act[:, idx * DN_DIM:(idx + 1) * DN_DIM]
        seg = seg * lax.rsqrt(jnp.sum(seg * seg, axis=-1, keepdims=True) + NORM_EPS)
        if idx < DN_HEADS:
            seg = seg * (DN_DIM ** -0.5)
        parts.append(seg)
    parts.append(act[:, 2 * DN_WIDTH:])
    qkvn = jnp.concatenate(parts, axis=1)
    lane = lax.broadcasted_iota(jnp.int32, pba.shape, 1)
    beta = _sigmoid(pba)
    g = -jnp.exp(alog) * _softplus(pba + dtb)
    gb = jnp.where(lane < DN_HEADS, beta, jnp.where(lane < 2 * DN_HEADS, g, 0.0))
    return qkvn, gb


def _dn_outnorm(o_a, z, dn):
    parts = []
    for h in range(DN_HEADS):
        seg = o_a[:, h * DN_DIM:(h + 1) * DN_DIM]
        seg = seg * lax.rsqrt(jnp.mean(seg * seg, axis=-1, keepdims=True) + NORM_EPS) * dn
        parts.append(seg)
    return jnp.concatenate(parts, axis=1) * _silu(z)


def _shift_down(x, halo8, s):
    r = pltpu.roll(x, s, axis=0)
    top = pltpu.roll(halo8, s, axis=0)
    i8 = lax.broadcasted_iota(jnp.int32, top.shape, 0)
    return jnp.concatenate([jnp.where(i8 < s, top, r[0:8]), r[8:]], axis=0)


def _shift_up(x, halo8, s):
    m = x.shape[0]
    r = pltpu.roll(x, m - s, axis=0)
    bot = pltpu.roll(halo8, 8 - s, axis=0)
    i8 = lax.broadcasted_iota(jnp.int32, bot.shape, 0)
    return jnp.concatenate([r[:m - 8], jnp.where(i8 >= 8 - s, bot, r[m - 8:])], axis=0)


def _conv_prep_fwd(name, pq, convw8, pba, alog, dtb, bm=256):
    s, w = pq.shape
    nblk = s // bm
    hb = bm // 16

    def body(x_ref, halo_ref, w_ref, pba_ref, alog_ref, dtb_ref, yc_ref, qkv_ref, gb_ref):
        i = pl.program_id(0)
        x = x_ref[...].astype(F32)
        halo = jnp.where(i > 0, halo_ref[...].astype(F32)[8:16], 0.0)
        cw = w_ref[...]
        y = x * cw[DN_CONV - 1:DN_CONV]
        for sft in range(1, DN_CONV):
            y = y + _shift_down(x, halo, sft) * cw[DN_CONV - 1 - sft:DN_CONV - sft]
        ycb = y.astype(BF16)
        yc_ref[...] = ycb
        qkvn, gb = _dn_prep(ycb.astype(F32), pba_ref[...], alog_ref[...], dtb_ref[...])
        qkv_ref[...] = qkvn.astype(BF16)
        gb_ref[...] = gb

    blk = 3 * _nbytes((bm, w), BF16) + 4 * _nbytes((bm, w), F32)
    return pl.pallas_call(
        body, name=name, grid=(nblk,),
        in_specs=[pl.BlockSpec((bm, w), lambda i: (i, 0)),
                  pl.BlockSpec((16, w), lambda i: (jnp.maximum(i * hb - 1, 0), 0)),
                  pl.BlockSpec(convw8.shape, lambda i: (0, 0)),
                  pl.BlockSpec((bm, LANES), lambda i: (i, 0)),
                  pl.BlockSpec((1, LANES), lambda i: (0, 0)),
                  pl.BlockSpec((1, LANES), lambda i: (0, 0))],
        out_specs=[pl.BlockSpec((bm, w), lambda i: (i, 0)), pl.BlockSpec((bm, w), lambda i: (i, 0)),
                   pl.BlockSpec((bm, LANES), lambda i: (i, 0))],
        out_shape=[jax.ShapeDtypeStruct((s, w), BF16), jax.ShapeDtypeStruct((s, w), BF16),
                   jax.ShapeDtypeStruct((s, LANES), F32)],
        compiler_params=_params(blk),
    )(pq, pq, convw8, pba, alog, dtb)


def _conv_bwd(name, dyc, pq, convw8, bm=256):
    s, w = pq.shape
    nblk = s // bm
    hb = bm // 16

    def body(dy_ref, dyn_ref, x_ref, xh_ref, w_ref, dx_ref, dw_ref):
        i = pl.program_id(0)
        dy = dy_ref[...].astype(F32)
        nxt = jnp.where(i < nblk - 1, dyn_ref[...].astype(F32)[0:8], 0.0)
        x = x_ref[...].astype(F32)
        halo = jnp.where(i > 0, xh_ref[...].astype(F32)[8:16], 0.0)
        cw = w_ref[...]
        dx = dy * cw[DN_CONV - 1:DN_CONV]
        for sft in range(1, DN_CONV):
            dx = dx + _shift_up(dy, nxt, sft) * cw[DN_CONV - 1 - sft:DN_CONV - sft]
        dx_ref[...] = dx.astype(dx_ref.dtype)
        r8 = lax.broadcasted_iota(jnp.int32, (8, w), 0)
        dw = jnp.zeros((8, w), F32)
        for j in range(DN_CONV):
            sft = DN_CONV - 1 - j
            xs = x if sft == 0 else _shift_down(x, halo, sft)
            dw = dw + jnp.where(r8 == j, jnp.sum(dy * xs, axis=0, keepdims=True), 0.0)

        @pl.when(i == 0)
        def _():
            dw_ref[...] = jnp.zeros((8, w), F32)
        dw_ref[...] += dw

    blk = 4 * _nbytes((bm, w), BF16) + 5 * _nbytes((bm, w), F32)
    return pl.pallas_call(
        body, name=name, grid=(nblk,),
        in_specs=[pl.BlockSpec((bm, w), lambda i: (i, 0)),
                  pl.BlockSpec((16, w), lambda i: (jnp.minimum((i + 1) * hb, s // 16 - 1), 0)),
                  pl.BlockSpec((bm, w), lambda i: (i, 0)),
                  pl.BlockSpec((16, w), lambda i: (jnp.maximum(i * hb - 1, 0), 0)),
                  pl.BlockSpec(convw8.shape, lambda i: (0, 0))],
        out_specs=[pl.BlockSpec((bm, w), lambda i: (i, 0)), pl.BlockSpec((8, w), lambda i: (0, 0))],
        out_shape=[jax.ShapeDtypeStruct((s, w), BF16), jax.ShapeDtypeStruct((8, w), F32)],
        compiler_params=_params(blk),
    )(dyc, dyc, pq, pq, convw8)


BNN = (((2,), (1,)), ((0,), (0,)))
BNT = (((2,), (2,)), ((0,), (0,)))
BTN = (((1,), (1,)), ((0,), (0,)))


def _raw_dot_1pass(a, b, dims):
    return lax.dot_general(a.astype(BF16), b.astype(BF16), dims, preferred_element_type=F32)


def _raw_dot_3pass(a, b, dims):
    ah = a.astype(BF16)
    al = (a - ah.astype(F32)).astype(BF16)
    bh = b.astype(BF16)
    bl = (b - bh.astype(F32)).astype(BF16)
    d = lambda x, y: lax.dot_general(x, y, dims, preferred_element_type=F32)
    return d(ah, bh) + (d(ah, bl) + d(al, bh))


def _with_same_precision_vjp(raw):
    @functools.partial(jax.custom_vjp, nondiff_argnums=(2,))
    def dot(a, b, dims):
        return raw(a, b, dims)

    def fwd(a, b, dims):
        return raw(a, b, dims), (a, b)

    def bwd(dims, res, ct):
        a, b = res
        if dims == BNN:
            return raw(ct, b, BNT), raw(a, ct, BTN)
        if dims == BNT:
            return raw(ct, b, BNN), raw(ct, a, BTN)
        assert dims == BTN
        return raw(b, ct, BNT), raw(a, ct, BNN)

    dot.defvjp(fwd, bwd)
    return dot


_dot_1pass_vjp = _with_same_precision_vjp(_raw_dot_1pass)
_dot_3pass_vjp = _with_same_precision_vjp(_raw_dot_3pass)


def _dot_bf16(a, b, dims=BNN):
    return _dot_1pass_vjp(a, b, dims)


def _dot_3pass(a, b, dims=BNN):
    return _dot_3pass_vjp(a, b, dims)


def _neumann_inverse(x):
    h, c, _ = x.shape
    eye = lax.broadcasted_iota(jnp.int32, (h, c, c), 1) == lax.broadcasted_iota(jnp.int32, (h, c, c), 2)
    t = jnp.where(eye, 1.0, 0.0) + x
    p = x
    for _ in range(5):
        p = _raw_dot_3pass(p, p, BNN)
        t = t + _raw_dot_3pass(t, p, BNN)
    return t


@jax.custom_vjp
def _known_inverse(x, t):
    return t


def _known_inverse_fwd(x, t):
    return t, t


def _known_inverse_bwd(t, ct):
    return _raw_dot_3pass(_raw_dot_3pass(t, ct, BTN), t, BNT), jnp.zeros_like(t)


_known_inverse.defvjp(_known_inverse_fwd, _known_inverse_bwd)


def _delta_chunk(q, k, v, gcol, bcol, state, t_known=None):
    h, c, _ = q.shape
    row = lax.broadcasted_iota(jnp.int32, (h, c, c), 1)
    col = lax.broadcasted_iota(jnp.int32, (h, c, c), 2)
    incl, strict, eye = row >= col, row > col, row == col
    g_b = jnp.broadcast_to(gcol, (h, c, c))
    gc_row = jnp.sum(jnp.where(row <= col, g_b, 0.0), axis=1, keepdims=True)
    g_r = jnp.sum(jnp.where(eye, g_b, 0.0), axis=1, keepdims=True)
    gc_col = jnp.sum(jnp.where(incl, jnp.broadcast_to(g_r, (h, c, c)), 0.0), axis=2, keepdims=True)
    decay = jnp.exp(jnp.where(incl, gc_col - gc_row, NEG))
    kb = k * bcol
    vb = v * bcol
    x = -jnp.where(strict, _dot_bf16(kb, k, BNT) * decay, 0.0)
    t = _neumann_inverse(x) if t_known is None else _known_inverse(x, t_known)
    eg = jnp.exp(gc_col)
    u = _dot_3pass(t, vb)
    w = _dot_3pass(t, kb * eg)
    qk = _dot_bf16(q, k, BNT) * decay
    v_new = u - _dot_bf16(w, state)
    o = _dot_bf16(q * eg, state) + _dot_bf16(qk, v_new)
    g_last = jnp.sum(g_r, axis=2, keepdims=True)
    new_state = state * jnp.exp(g_last) + _dot_bf16(k * jnp.exp(g_last - gc_col), v_new, BTN)
    return o, new_state, t


def _lane_col(blk, idx):
    lane = lax.broadcasted_iota(jnp.int32, blk.shape, 1)
    return jnp.sum(jnp.where(lane == idx, blk, 0.0), axis=1, keepdims=True)


def _dn_heads(ref, base):
    return jnp.stack([ref[:, base + h * DN_DIM:base + (h + 1) * DN_DIM] for h in range(DN_HEADS)], axis=0).astype(F32)


def _dn_cols(gbv, base):
    return jnp.stack([_lane_col(gbv, base + h) for h in range(DN_HEADS)], axis=0)


def _delta_fwd(name, qkvn, gb):
    s = qkvn.shape[0]
    n = s // DN_CHUNK
    c = DN_CHUNK

    def body(qkv_ref, gb_ref, o_ref, st_ref, t_ref, state):
        @pl.when(pl.program_id(0) == 0)
        def _():
            state[...] = jnp.zeros(state.shape, F32)

        gbv = gb_ref[...]
        st = state[...]
        st_ref[0] = st
        o, new, t = _delta_chunk(_dn_heads(qkv_ref, 0), _dn_heads(qkv_ref, DN_WIDTH), _dn_heads(qkv_ref, 2 * DN_WIDTH),
                                 _dn_cols(gbv, DN_HEADS), _dn_cols(gbv, 0), st)
        for h in range(DN_HEADS):
            o_ref[:, h * DN_DIM:(h + 1) * DN_DIM] = o[h]
        t_ref[0] = t
        state[...] = new

    blk = _nbytes((c, 3 * DN_WIDTH), BF16) + _nbytes((c, LANES), F32) + _nbytes((c, DN_WIDTH), F32)
    blk += _nbytes((DN_HEADS, DN_DIM, DN_DIM), F32) + _nbytes((DN_HEADS, c, c), F32)
    return pl.pallas_call(
        body, name=name, grid=(n,),
        in_specs=[pl.BlockSpec((c, 3 * DN_WIDTH), lambda i: (i, 0)), pl.BlockSpec((c, LANES), lambda i: (i, 0))],
        out_specs=[pl.BlockSpec((c, DN_WIDTH), lambda i: (i, 0)),
                   pl.BlockSpec((1, DN_HEADS, DN_DIM, DN_DIM), lambda i: (i, 0, 0, 0)),
                   pl.BlockSpec((1, DN_HEADS, c, c), lambda i: (i, 0, 0, 0))],
        out_shape=[jax.ShapeDtypeStruct((s, DN_WIDTH), F32),
                   jax.ShapeDtypeStruct((n, DN_HEADS, DN_DIM, DN_DIM), F32),
                   jax.ShapeDtypeStruct((n, DN_HEADS, c, c), F32)],
        scratch_shapes=[pltpu.VMEM((DN_HEADS, DN_DIM, DN_DIM), F32)],
        compiler_params=_params(blk, 8 << 20),
    )(qkvn, gb)


def _delta_bwd(name, qkvn, gb, states, tinv, d_o):
    s = qkvn.shape[0]
    n = s // DN_CHUNK
    c = DN_CHUNK

    def body(qkv_ref, gb_ref, st_ref, t_ref, do_ref, dqkv_ref, dgb_ref, dstate):
        @pl.when(pl.program_id(0) == 0)
        def _():
            dstate[...] = jnp.zeros(dstate.shape, F32)

        gbv = gb_ref[...]
        lane = lax.broadcasted_iota(jnp.int32, (c, LANES), 1)
        t_known = t_ref[0]
        chunk = lambda *args: _delta_chunk(*args, t_known=t_known)[:2]
        _, vjp = jax.vjp(chunk, _dn_heads(qkv_ref, 0), _dn_heads(qkv_ref, DN_WIDTH),
                         _dn_heads(qkv_ref, 2 * DN_WIDTH), _dn_cols(gbv, DN_HEADS), _dn_cols(gbv, 0), st_ref[0])
        dq, dk, dv, dg, db, dst = vjp((_dn_heads(do_ref, 0), dstate[...]))
        dgb = jnp.zeros((c, LANES), F32)
        for h in range(DN_HEADS):
            dqkv_ref[:, h * DN_DIM:(h + 1) * DN_DIM] = dq[h]
            dqkv_ref[:, DN_WIDTH + h * DN_DIM:DN_WIDTH + (h + 1) * DN_DIM] = dk[h]
            dqkv_ref[:, 2 * DN_WIDTH + h * DN_DIM:2 * DN_WIDTH + (h + 1) * DN_DIM] = dv[h]
            dgb = dgb + jnp.where(lane == h, db[h], 0.0) + jnp.where(lane == DN_HEADS + h, dg[h], 0.0)
        dstate[...] = dst
        dgb_ref[...] = dgb

    rev = lambda i: (n - 1 - i, 0)
    blk = _nbytes((c, 3 * DN_WIDTH), BF16) + 2 * _nbytes((c, LANES), F32) + _nbytes((c, DN_WIDTH), F32)
    blk += _nbytes((DN_HEADS, DN_DIM, DN_DIM), F32) + _nbytes((c, 3 * DN_WIDTH), F32)
    return pl.pallas_call(
        body, name=name, grid=(n,),
        in_specs=[pl.BlockSpec((c, 3 * DN_WIDTH), rev), pl.BlockSpec((c, LANES), rev),
                  pl.BlockSpec((1, DN_HEADS, DN_DIM, DN_DIM), lambda i: (n - 1 - i, 0, 0, 0)),
                  pl.BlockSpec((1, DN_HEADS, c, c), lambda i: (n - 1 - i, 0, 0, 0)),
                  pl.BlockSpec((c, DN_WIDTH), rev)],
        out_specs=[pl.BlockSpec((c, 3 * DN_WIDTH), rev), pl.BlockSpec((c, LANES), rev)],
        out_shape=[jax.ShapeDtypeStruct((s, 3 * DN_WIDTH), F32), jax.ShapeDtypeStruct((s, LANES), F32)],
        scratch_shapes=[pltpu.VMEM((DN_HEADS, DN_DIM, DN_DIM), F32)],
        compiler_params=_params(blk, 16 << 20),
    )(qkvn, gb, states, tinv, d_o)


def _da_scores(q2f, k2, sub, valid, distf, head):
    lane = lax.broadcasted_iota(jnp.int32, q2f.shape, 1)
    hmask = (lane < DA_DIM) if sub == 0 else (lane >= DA_DIM)
    qm = jnp.where(hmask, q2f, 0.0).astype(BF16)
    slope = 2.0 ** (-ALIBI_MAX_EXP * (head + 1) / DA_HEADS)
    sc = lax.dot_general(qm, k2, NT, preferred_element_type=F32) * (DA_DIM ** -0.5)
    return jnp.where(valid, sc - slope * distf, NEG), qm, hmask


def _da_mask(i, r):
    qi = lax.broadcasted_iota(jnp.int32, (DA_BLOCK, 2 * DA_BLOCK), 0)
    ki = lax.broadcasted_iota(jnp.int32, (DA_BLOCK, 2 * DA_BLOCK), 1)
    dist = qi + DA_BLOCK - ki
    valid = (dist >= 0) & (dist <= DA_BLOCK) & ((ki >= DA_BLOCK) | (i > 0))
    return valid, (dist * r).astype(F32)


def _da_fwd(name, pda, r):
    s = pda.shape[0]
    n = s // r
    nb = n // DA_BLOCK
    w = DA_WIDTH
    dav = pda.reshape(n, r * 3 * w)

    def body(q_ref, kc_ref, kp_ref, vc_ref, vp_ref, o_ref, lse_ref):
        i = pl.program_id(1)
        valid, distf = _da_mask(i, r)
        lane = lax.broadcasted_iota(jnp.int32, (DA_BLOCK, LANES), 1)
        lse = jnp.zeros((DA_BLOCK, LANES), F32)
        for hp in range(DA_HEADS // 2):
            sl = slice(hp * LANES, (hp + 1) * LANES)
            q2f = q_ref[:, sl].astype(F32)
            k2 = jnp.concatenate([kp_ref[:, sl], kc_ref[:, sl]], axis=0)
            v2 = jnp.concatenate([vp_ref[:, sl], vc_ref[:, sl]], axis=0)
            o2 = None
            for sub in range(2):
                head = 2 * hp + sub
                sc, _, hmask = _da_scores(q2f, k2, sub, valid, distf, head)
                mx = jnp.max(sc, axis=1, keepdims=True)
                p = jnp.exp(sc - mx)
                l = jnp.sum(p, axis=1, keepdims=True)
                pv = lax.dot_general(p.astype(BF16), v2, NN, preferred_element_type=F32) / l
                o2 = pv if sub == 0 else jnp.where(hmask, pv, o2)
                lse = jnp.where(lane == head, mx + jnp.log(l), lse)
            o_ref[:, sl] = o2.astype(o_ref.dtype)
        lse_ref[...] = lse

    prev = lambda col: (lambda p, i: (jnp.maximum(i - 1, 0), 3 * p + col))
    cur = lambda col: (lambda p, i: (i, 3 * p + col))
    blk = 5 * _nbytes((DA_BLOCK, w), BF16) + _nbytes((DA_BLOCK, w), F32) + _nbytes((DA_BLOCK, LANES), F32)
    o, lse = pl.pallas_call(
        body, name=name, grid=(r, nb),
        in_specs=[pl.BlockSpec((DA_BLOCK, w), cur(0)), pl.BlockSpec((DA_BLOCK, w), cur(1)),
                  pl.BlockSpec((DA_BLOCK, w), prev(1)), pl.BlockSpec((DA_BLOCK, w), cur(2)),
                  pl.BlockSpec((DA_BLOCK, w), prev(2))],
        out_specs=[pl.BlockSpec((DA_BLOCK, w), lambda p, i: (i, p)),
                   pl.BlockSpec((DA_BLOCK, LANES), lambda p, i: (i, p))],
        out_shape=[jax.ShapeDtypeStruct((n, r * w), BF16), jax.ShapeDtypeStruct((n, r * LANES), F32)],
        compiler_params=_params(blk, 8 << 20),
    )(dav, dav, dav, dav, dav)
    return o.reshape(s, w), lse.reshape(s, LANES)


def _da_bwd(name, pda, d_ob, lse_tot, delta, r):
    s = pda.shape[0]
    n = s // r
    nb = n // DA_BLOCK
    w = DA_WIDTH
    dav = pda.reshape(n, r * 3 * w)
    dov = d_ob.reshape(n, r * w)
    lv = lse_tot.reshape(n, r * LANES)
    dlv = delta.reshape(n, r * LANES)

    def body(q_ref, kc_ref, kp_ref, vc_ref, vp_ref, do_ref, l_ref, dl_ref, dq_ref, dk_ref, dv_ref, ck, cv):
        i = pl.program_id(1)

        @pl.when(i == 0)
        def _():
            ck[...] = jnp.zeros(ck.shape, F32)
            cv[...] = jnp.zeros(cv.shape, F32)

        @pl.when(i < nb)
        def _():
            valid, distf = _da_mask(i, r)
            lsev = l_ref[...]
            dlt = dl_ref[...]
            for hp in range(DA_HEADS // 2):
                sl = slice(hp * LANES, (hp + 1) * LANES)
                q2f = q_ref[:, sl].astype(F32)
                k2 = jnp.concatenate([kp_ref[:, sl], kc_ref[:, sl]], axis=0)
                v2 = jnp.concatenate([vp_ref[:, sl], vc_ref[:, sl]], axis=0)
                do2f = do_ref[:, sl].astype(F32)
                dq2 = jnp.zeros((DA_BLOCK, LANES), F32)
                dk2 = jnp.zeros((2 * DA_BLOCK, LANES), F32)
                dv2 = jnp.zeros((2 * DA_BLOCK, LANES), F32)
                for sub in range(2):
                    head = 2 * hp + sub
                    sc, qm, hmask = _da_scores(q2f, k2, sub, valid, distf, head)
                    p = jnp.exp(sc - _lane_col(lsev, head))
                    dom = jnp.where(hmask, do2f, 0.0).astype(BF16)
                    dp = lax.dot_general(dom, v2, NT, preferred_element_type=F32)
                    ds = (p * (dp - _lane_col(dlt, head)) * (DA_DIM ** -0.5)).astype(BF16)
                    dq2 = dq2 + jnp.where(hmask, lax.dot_general(ds, k2, NN, preferred_element_type=F32), 0.0)
                    dk2 = dk2 + lax.dot_general(ds, qm, TN, preferred_element_type=F32)
                    dv2 = dv2 + lax.dot_general(p.astype(BF16), dom, TN, preferred_element_type=F32)
                dq_ref[:, sl] = dq2.astype(dq_ref.dtype)
                dk_ref[:, sl] = (ck[:, sl] + dk2[:DA_BLOCK]).astype(dk_ref.dtype)
                dv_ref[:, sl] = (cv[:, sl] + dv2[:DA_BLOCK]).astype(dv_ref.dtype)
                ck[:, sl] = dk2[DA_BLOCK:]
                cv[:, sl] = dv2[DA_BLOCK:]

        @pl.when(i == nb)
        def _():
            dk_ref[...] = ck[...].astype(dk_ref.dtype)
            dv_ref[...] = cv[...].astype(dv_ref.dtype)

    qrow = lambda i: jnp.minimum(i, nb - 1)
    prev = lambda col: (lambda p, i: (jnp.maximum(qrow(i) - 1, 0), 3 * p + col))
    cur = lambda col: (lambda p, i: (qrow(i), 3 * p + col))
    same = lambda p, i: (qrow(i), p)
    late = lambda p, i: (jnp.maximum(i - 1, 0), p)
    blk = 6 * _nbytes((DA_BLOCK, w), BF16) + 2 * _nbytes((DA_BLOCK, LANES), F32) + 3 * _nbytes((DA_BLOCK, w), F32)
    dq, dk, dv = pl.pallas_call(
        body, name=name, grid=(r, nb + 1),
        in_specs=[pl.BlockSpec((DA_BLOCK, w), cur(0)), pl.BlockSpec((DA_BLOCK, w), cur(1)),
                  pl.BlockSpec((DA_BLOCK, w), prev(1)), pl.BlockSpec((DA_BLOCK, w), cur(2)),
                  pl.BlockSpec((DA_BLOCK, w), prev(2)), pl.BlockSpec((DA_BLOCK, w), same),
                  pl.BlockSpec((DA_BLOCK, LANES), same), pl.BlockSpec((DA_BLOCK, LANES), same)],
        out_specs=[pl.BlockSpec((DA_BLOCK, w), same), pl.BlockSpec((DA_BLOCK, w), late),
                   pl.BlockSpec((DA_BLOCK, w), late)],
        out_shape=[jax.ShapeDtypeStruct((n, r * w), BF16)] * 3,
        scratch_shapes=[pltpu.VMEM((DA_BLOCK, w), F32), pltpu.VMEM((DA_BLOCK, w), F32)],
        compiler_params=_params(blk, 12 << 20),
    )(dav, dav, dav, dav, dav, dov, lv, dlv)
    return dq.reshape(s, w), dk.reshape(s, w), dv.reshape(s, w)


def _head_expand():
    hrow = lax.broadcasted_iota(jnp.int32, (LANES, DA_WIDTH), 0)
    lcol = lax.broadcasted_iota(jnp.int32, (LANES, DA_WIDTH), 1)
    return jnp.where(lcol // DA_DIM == hrow, 1.0, 0.0).astype(F32)


def _ffn_up(name, a, ga, tg, tu):
    s, d = a.shape
    nsh, _, _, ffs = ga.shape
    bm = _pick(s, (1024, 512, 256, 128))

    def body(a_ref, wg_ref, wu_ref, g_ref, u_ref, f_ref):
        av = a_ref[...]
        g = lax.dot_general(av, wg_ref[...], NN, preferred_element_type=F32)
        u = lax.dot_general(av, wu_ref[...], NN, preferred_element_type=F32)
        g_ref[...] = g.astype(BF16)
        u_ref[...] = u.astype(BF16)
        f_ref[...] = (_silu(g) * u).astype(BF16)

    wspec = lambda t: pl.BlockSpec((None, None, d, ffs), lambda i, j: (j, t, 0, 0))
    ospec = pl.BlockSpec((None, bm, ffs), lambda i, j: (j, i, 0))
    blk = _nbytes((bm, d), BF16) + 2 * _nbytes((d, ffs), BF16) + 3 * _nbytes((bm, ffs), BF16)
    return pl.pallas_call(
        body, name=name, grid=(s // bm, nsh),
        in_specs=[pl.BlockSpec((bm, d), lambda i, j: (i, 0)), wspec(tg), wspec(tu)],
        out_specs=[ospec] * 3, out_shape=[jax.ShapeDtypeStruct((nsh, s, ffs), BF16)] * 3,
        compiler_params=_params(blk, 4 * _nbytes((bm, ffs), F32)),
    )(a, ga, ga)


def _ffn_fwd(tag, h_in, ln, sh, sc, gt, ga, tg, tu, gb, td, weight):
    s, d = h_in.shape
    nsh, _, ffs, _ = gb.shape
    (a,) = _rowwise(tag + "_norm", lambda h, l, s1, s2: ((_rms_mod(h, l, s1, s2),), ()), [h_in], [ln, sh, sc],
                    [(d, BF16)])
    g, u, f = _ffn_up(tag + "_up", a, ga, tg, tu)
    bm, bn = _pick(s, (1024, 512, 256, 128)), _pick(d, (512, 256, 128))
    io = pl.BlockSpec((bm, bn), lambda i, j, kk: (i, j))
    h_out, o = _mm_core(
        tag + "_down", (s // bm, d // bn, nsh), nsh,
        [(f, pl.BlockSpec((None, bm, ffs), lambda i, j, kk: (kk, i, 0)),
          gb, pl.BlockSpec((None, None, ffs, bn), lambda i, j, kk: (kk, td, 0, j)), NN)],
        [((s, d), F32, io), ((s, d), BF16, io)], (bm, bn),
        epi=lambda acc, h, gv: (h + weight * gv * acc, acc),
        epi_ins=[(h_in, io), (gt, pl.BlockSpec((1, bn), lambda i, j, kk: (0, j)))])
    return h_out, dict(a=a, g=g, u=u, f=f, o=o)


def _resid_bwd(tag, dh_out, o, gt, weight):
    d = dh_out.shape[1]

    def fn(dh, ov, g):
        return (weight * g * dh,), (jnp.sum(weight * dh * ov.astype(F32), axis=0, keepdims=True),)

    do, d_gt = _rowwise(tag + "_resid_bwd", fn, [dh_out, o], [gt], [(d, BF16)], [(1, d)])
    return do, d_gt


def _norm_bwd(tag, h_in, da, dh_out, ln, sh, sc):
    d = h_in.shape[1]

    def fn(h, dav, dh, l, s1, s2):
        _, vjp = jax.vjp(_rms_mod, h, l, s1, s2)
        gh, gl, gs1, gs2 = vjp(dav)
        return (dh + gh,), (gl, gs1, gs2)

    return _rowwise(tag + "_norm_bwd", fn, [h_in, da, dh_out], [ln, sh, sc], [(d, F32)], [(1, d)] * 3)


def _ffn_bwd(tag, h_in, dh_out, sv, ln, sh, sc, gt, ga, tg, tu, gb, td, weight):
    s, d = h_in.shape
    nsh, _, ffs, _ = gb.shape
    bm, bn = _pick(s, (1024, 512, 256, 128)), _pick(d, (512, 256, 128))
    bk = _pick(s, (1024, 512, 256, 128))
    do, d_gt = _resid_bwd(tag, dh_out, sv["o"], gt, weight)

    def act_bwd(df, g, u):
        _, vjp = jax.vjp(_swiglu_act, g, u)
        return vjp(df)

    hid = pl.BlockSpec((None, bm, ffs), lambda i, j, kk: (j, i, 0))
    dg, du = _mm_core(
        tag + "_down_dx", (s // bm, nsh, 1), 1,
        [(do, pl.BlockSpec((bm, d), lambda i, j, kk: (i, 0)),
          gb, pl.BlockSpec((None, None, ffs, d), lambda i, j, kk: (j, td, 0, 0)), NT)],
        [((nsh, s, ffs), BF16, hid)] * 2, (bm, ffs), epi=act_bwd, epi_ins=[(sv["g"], hid), (sv["u"], hid)])
    (d_wd,) = _mm_core(
        tag + "_down_dw", (nsh, d // bn, s // bk), s // bk,
        [(sv["f"], pl.BlockSpec((None, bk, ffs), lambda i, j, kk: (i, kk, 0)),
          do, pl.BlockSpec((bk, bn), lambda i, j, kk: (kk, j)), TN)],
        [((nsh, ffs, d), BF16, pl.BlockSpec((None, ffs, bn), lambda i, j, kk: (i, 0, j)))], (ffs, bn))
    kmaj = pl.BlockSpec((None, bm, ffs), lambda i, j, kk: (kk, i, 0))
    wsp = lambda t: pl.BlockSpec((None, None, bn, ffs), functools.partial(lambda i, j, kk, t: (kk, t, j, 0), t=t))
    (da,) = _mm_core(
        tag + "_up_dx", (s // bm, d // bn, nsh), nsh, [(dg, kmaj, ga, wsp(tg), NT), (du, kmaj, ga, wsp(tu), NT)],
        [((s, d), F32, pl.BlockSpec((bm, bn), lambda i, j, kk: (i, j)))], (bm, bn))
    dws = []
    for nm, dh in (("_wg_dw", dg), ("_wu_dw", du)):
        (dw,) = _mm_core(
            tag + nm, (1, nsh, s // bk), s // bk,
            [(sv["a"], pl.BlockSpec((bk, d), lambda i, j, kk: (kk, 0)),
              dh, pl.BlockSpec((None, bk, ffs), lambda i, j, kk: (j, kk, 0)), TN)],
            [((nsh, d, ffs), BF16, pl.BlockSpec((None, d, ffs), lambda i, j, kk: (j, 0, 0)))], (d, ffs))
        dws.append(dw)
    dh_in, d_ln, d_sh, d_sc = _norm_bwd(tag, h_in, da, dh_out, ln, sh, sc)
    return dh_in, dict(wg=dws[0], wu=dws[1], wd=d_wd), dict(ln=d_ln, sh=d_sh, sc=d_sc, gt=d_gt)


def _mixer_fwd(tag, h_in, ln, sh, sc, gt, w, sp):
    d = h_in.shape[1]
    (a,) = _rowwise(tag + "_norm", lambda h, l, s1, s2: ((_rms_mod(h, l, s1, s2),), ()), [h_in], [ln, sh, sc],
                    [(d, BF16)])
    (pq,) = _matmul(tag + "_pq", a, w["wq"], outs=(BF16,))
    (pz,) = _matmul(tag + "_pz", a, w["wz"], outs=(BF16,))
    (pba,) = _matmul(tag + "_pba", a, w["wba"])
    (pda,) = _matmul(tag + "_pda", a, w["wda"], outs=(BF16,))
    (pg,) = _matmul(tag + "_pg", a, w["wg"], outs=(BF16,))
    yc, qkvn, gb = _conv_prep_fwd(tag + "_conv", pq, sp["conv8"], pba, sp["alog"], sp["dtb"])
    o_a, states, tinv = _delta_fwd(tag + "_delta", qkvn, gb)
    (o_an,) = _rowwise(tag + "_dnorm", lambda o, z, dn: ((_dn_outnorm(o, z.astype(F32), dn),), ()), [o_a, pz],
                       [sp["dn"]], [(DN_WIDTH, BF16)])
    ops, lses = [], []
    for (_, r) in DA_PATTERNS:
        o_p, lse_p = _da_fwd(f"{tag}_da{r}", pda, r)
        ops.append(o_p)
        lses.append(lse_p)

    def merge(o1, o2, o3, l1, l2, l3):
        mx = jnp.maximum(jnp.maximum(l1, l2), l3)
        e1, e2, e3 = jnp.exp(l1 - mx), jnp.exp(l2 - mx), jnp.exp(l3 - mx)
        tot = e1 + e2 + e3
        ex = _head_expand()
        up = lambda wgt: lax.dot_general(wgt / tot, ex, NN, precision=HI, preferred_element_type=F32)
        return (up(e1) * o1 + up(e2) * o2 + up(e3) * o3, mx + jnp.log(tot)), ()

    o_b, lse_tot = _rowwise(tag + "_merge", merge, ops + lses, [], [(DA_WIDTH, BF16), (LANES, F32)])
    (y_a,) = _matmul(tag + "_wa", o_an, w["w_a"], outs=(BF16,))
    (y_b,) = _matmul(tag + "_wb", o_b, w["w_b"], outs=(BF16,))

    def gate(ga, gbv, ya, yb):
        return _sigmoid(ga.astype(F32)) * ya.astype(F32) + _sigmoid(gbv.astype(F32)) * yb.astype(F32)

    (merged,) = _rowwise(tag + "_gate", lambda *v: ((gate(*v),), ()), [(pg, d, 0), (pg, d, 1), y_a, y_b], [],
                         [(d, BF16)])
    h_out, m = _matmul(tag + "_wo", merged, w["w_o"], outs=(F32, BF16), epi_rows=[h_in], epi_bcast=[gt],
                       epi=lambda acc, h, g: (h + g * acc, acc))
    sv = dict(a=a, pq=pq, pz=pz, pba=pba, pda=pda, pg=pg, yc=yc, qkvn=qkvn, gb=gb, o_a=o_a, states=states, tinv=tinv,
              o_an=o_an, o_b=o_b, lse=lse_tot, y_a=y_a, y_b=y_b, merged=merged, m=m, gate=gate)
    return h_out, sv


def _mixer_bwd(tag, h_in, dh_out, sv, ln, sh, sc, gt, w, sp):
    d = h_in.shape[1]
    dm, d_gt = _resid_bwd(tag, dh_out, sv["m"], gt, 1.0)
    (d_merged,) = _matmul(tag + "_wo_dx", dm, w["w_o"], tb=True, outs=(BF16,))
    (d_wo,) = _matmul(tag + "_wo_dw", sv["merged"], dm, ta=True, outs=(BF16,))
    gate = sv["gate"]

    def gate_bwd(dmg, ga, gbv, ya, yb):
        _, vjp = jax.vjp(gate, ga.astype(F32), gbv.astype(F32), ya.astype(F32), yb.astype(F32))
        dga, dgb, dya, dyb = vjp(dmg.astype(F32))
        return (jnp.concatenate([dga, dgb], axis=1), dya, dyb), ()

    pg = sv["pg"]
    d_pg, d_ya, d_yb = _rowwise(tag + "_gate_bwd", gate_bwd, [d_merged, (pg, d, 0), (pg, d, 1), sv["y_a"], sv["y_b"]],
                                [], [(2 * d, BF16), (d, BF16), (d, BF16)])
    (d_oan,) = _matmul(tag + "_wa_dx", d_ya, w["w_a"], tb=True)
    (d_wa,) = _matmul(tag + "_wa_dw", sv["o_an"], d_ya, ta=True, outs=(BF16,))
    (d_ob,) = _matmul(tag + "_wb_dx", d_yb, w["w_b"], tb=True, outs=(BF16,))
    (d_wb,) = _matmul(tag + "_wb_dw", sv["o_b"], d_yb, ta=True, outs=(BF16,))

    def dnorm_bwd(doan, o, z, dn):
        _, vjp = jax.vjp(_dn_outnorm, o, z.astype(F32), dn)
        go, gz, gdn = vjp(doan)
        return (go, gz), (gdn,)

    d_oa, d_pz, d_dn = _rowwise(tag + "_dnorm_bwd", dnorm_bwd, [d_oan, sv["o_a"], sv["pz"]], [sp["dn"]],
                                [(DN_WIDTH, F32), (DN_WIDTH, BF16)], [(1, DN_DIM)])
    d_qkvn, d_gb = _delta_bwd(tag + "_delta_bwd", sv["qkvn"], sv["gb"], sv["states"], sv["tinv"], d_oa)

    def prep_bwd(dq, dgbv, yc, pba, alog, dtb):
        _, vjp = jax.vjp(_dn_prep, yc.astype(F32), pba, alog, dtb)
        gyc, gpba, galog, gdtb = vjp((dq, dgbv))
        return (gyc, gpba), (galog, gdtb)

    d_yc, d_pba, d_alog, d_dtb = _rowwise(tag + "_prep_bwd", prep_bwd, [d_qkvn, d_gb, sv["yc"], sv["pba"]],
                                          [sp["alog"], sp["dtb"]], [(3 * DN_WIDTH, BF16), (LANES, BF16)],
                                          [(1, LANES), (1, LANES)], bm=128)
    d_pq, d_conv = _conv_bwd(tag + "_conv_bwd", d_yc, sv["pq"], sp["conv8"])

    def delta_fn(dob, ob):
        prod = dob.astype(F32) * ob.astype(F32)
        return (lax.dot_general(prod, _head_expand(), NT, precision=HI, preferred_element_type=F32),), ()

    (delta,) = _rowwise(tag + "_da_delta", delta_fn, [d_ob, sv["o_b"]], [], [(LANES, F32)])
    grads = [_da_bwd(f"{tag}_da{r}_bwd", sv["pda"], d_ob, sv["lse"], delta, r) for (_, r) in DA_PATTERNS]

    def sum3(*parts):
        q1, k1, v1, q2, k2, v2, q3, k3, v3 = (p.astype(F32) for p in parts)
        return (jnp.concatenate([q1 + q2 + q3, k1 + k2 + k3, v1 + v2 + v3], axis=1),), ()

    (d_pda,) = _rowwise(tag + "_da_sum", sum3, [t for g in grads for t in g], [], [(3 * DA_WIDTH, BF16)])

    a = sv["a"]
    (da,) = _matmul(tag + "_pq_dx", d_pq, w["wq"], tb=True)
    add = lambda acc, prev: (acc + prev,)
    (da,) = _matmul(tag + "_pz_dx", d_pz, w["wz"], tb=True, epi_rows=[da], epi=add)
    (da,) = _matmul(tag + "_pba_dx", d_pba, w["wba"], tb=True, epi_rows=[da], epi=add)
    (da,) = _matmul(tag + "_pda_dx", d_pda, w["wda"], tb=True, epi_rows=[da], epi=add)
    (da,) = _matmul(tag + "_pg_dx", d_pg, w["wg"], tb=True, epi_rows=[da], epi=add)
    (d_wq,) = _matmul(tag + "_pq_dw", a, d_pq, ta=True, outs=(BF16,))
    (d_wz,) = _matmul(tag + "_pz_dw", a, d_pz, ta=True, outs=(BF16,))
    (d_wba,) = _matmul(tag + "_pba_dw", a, d_pba, ta=True, outs=(BF16,))
    (d_wda,) = _matmul(tag + "_pda_dw", a, d_pda, ta=True, outs=(BF16,))
    (d_wg,) = _matmul(tag + "_pg_dw", a, d_pg, ta=True, outs=(BF16,))
    dh_in, d_ln, d_sh, d_sc = _norm_bwd(tag, h_in, da, dh_out, ln, sh, sc)
    wgrads = dict(wq=d_wq, wz=d_wz, wba=d_wba, wda=d_wda, wg=d_wg, w_a=d_wa, w_b=d_wb, w_o=d_wo)
    small = dict(ln=d_ln, sh=d_sh, sc=d_sc, gt=d_gt, dn=d_dn, alog=d_alog, dtb=d_dtb, conv=d_conv)
    return dh_in, wgrads, small


def _loss_head(h, target, fnorm):
    d = h.shape[1]

    def fn(hv, tv, fw):
        def lossf(hh, ww):
            y = hh * lax.rsqrt(jnp.mean(hh * hh, axis=-1, keepdims=True) + NORM_EPS) * ww
            return 0.5 * jnp.sum(jnp.mean(jnp.square(y - tv), axis=-1))

        val, (dh, dw) = jax.value_and_grad(lossf, argnums=(0, 1))(hv, fw)
        return (dh,), (jnp.full((1, LANES), val, F32), dw)

    return _rowwise("loss_head", fn, [h, target], [fnorm], [(d, F32)], [(1, LANES), (1, d)])


def _row(v):
    return v.reshape(1, -1)


def _pad_lanes(v, offset):
    return jnp.pad(v.reshape(1, -1), ((0, 0), (offset, LANES - offset - v.shape[0])))


_UP_SLOTS = dict(ffn1_wg=0, ffn1_wu=1, ffn2_wg=2, ffn2_wu=3)
_DOWN_SLOTS = dict(ffn1_wd=0, ffn2_wd=1)


def _local_step(x2, target, mod, layer_weights, small, on_layer_grads):
    depth = mod.shape[0]
    d = x2.shape[1]
    h = x2
    saved = []
    mods = []
    up = lambda l, nm: _UP_SLOTS[nm]
    down = lambda l, nm: _DOWN_SLOTS[nm]
    for l in range(depth):
        m9 = [_row(mod[l, i * d:(i + 1) * d]) for i in range(N_ADA)]
        sp = dict(conv8=jnp.pad(small["conv_w"][l], ((0, 8 - DN_CONV), (0, 0))),
                  alog=_pad_lanes(small["a_log"][l], DN_HEADS), dtb=_pad_lanes(small["dt_bias"][l], DN_HEADS),
                  dn=_row(small["dn_norm"][l]))
        ga, gb, w = layer_weights(l, h)
        h0 = h
        h1, sv1 = _ffn_fwd(f"l{l}_ffn1", h0, _row(small["ln_ffn1"][l]), m9[0], m9[1], m9[2], ga, up(l, "ffn1_wg"),
                           up(l, "ffn1_wu"), gb, down(l, "ffn1_wd"), 0.5)
        h2, sv2 = _mixer_fwd(f"l{l}_mix", h1, _row(small["ln_mix"][l]), m9[3], m9[4], m9[5], w, sp)
        h3, sv3 = _ffn_fwd(f"l{l}_ffn2", h2, _row(small["ln_ffn2"][l]), m9[6], m9[7], m9[8], ga, up(l, "ffn2_wg"),
                           up(l, "ffn2_wu"), gb, down(l, "ffn2_wd"), 0.5)
        saved.append((h0, h1, h2, sv1, sv2, sv3, sp, ga, gb, w))
        mods.append(m9)
        h = h3
    dh, loss_part, d_fnorm = _loss_head(h, target, _row(small["final_norm"]))
    sgrads, dmods = [], []
    token = None
    for l in reversed(range(depth)):
        h0, h1, h2, sv1, sv2, sv3, sp, ga, gb, w = saved[l]
        m9 = mods[l] if token is None else [r + token for r in mods[l]]
        dh, g3, s3 = _ffn_bwd(f"l{l}_ffn2", h2, dh, sv3, _row(small["ln_ffn2"][l]), m9[6], m9[7], m9[8], ga,
                              up(l, "ffn2_wg"), up(l, "ffn2_wu"), gb, down(l, "ffn2_wd"), 0.5)
        dh, g2, s2 = _mixer_bwd(f"l{l}_mix", h1, dh, sv2, _row(small["ln_mix"][l]), m9[3], m9[4], m9[5], w, sp)
        dh, g1, s1 = _ffn_bwd(f"l{l}_ffn1", h0, dh, sv1, _row(small["ln_ffn1"][l]), m9[0], m9[1], m9[2], ga,
                              up(l, "ffn1_wg"), up(l, "ffn1_wu"), gb, down(l, "ffn1_wd"), 0.5)
        token = on_layer_grads(l, dict(ffn1_wg=g1["wg"], ffn1_wu=g1["wu"], ffn1_wd=g1["wd"], ffn2_wg=g3["wg"],
                                       ffn2_wu=g3["wu"], ffn2_wd=g3["wd"], **g2))
        dmods.append(jnp.concatenate([s1["sh"], s1["sc"], s1["gt"], s2["sh"], s2["sc"], s2["gt"],
                                      s3["sh"], s3["sc"], s3["gt"]], axis=1))
        sgrads.append(dict(ln_ffn1=s1["ln"][0], ln_mix=s2["ln"][0], ln_ffn2=s3["ln"][0],
                           a_log=s2["alog"][0, DN_HEADS:2 * DN_HEADS], dt_bias=s2["dtb"][0, DN_HEADS:2 * DN_HEADS],
                           dn_norm=s2["dn"][0], conv_w=s2["conv"][:DN_CONV]))
    sgrads.reverse()
    dmods.reverse()
    return loss_part[0, 0], dh, jnp.concatenate(dmods, axis=0), sgrads, d_fnorm[0]


def _flip(v, bit):
    return 1 - v if bit else v


def _allgather8(name, x):
    r, c = x.shape

    def body(x_ref, out_ref, send_sems, recv_sems, local_sem):
        mx, my, mc = lax.axis_index("x"), lax.axis_index("y"), lax.axis_index("c")
        me = 4 * mx + 2 * my + mc
        mine = pltpu.make_async_copy(x_ref, out_ref.at[me], local_sem)
        mine.start()
        sends = []
        for k in range(1, 8):
            peer = (_flip(mx, k & 4), _flip(my, k & 2), _flip(mc, k & 1))
            cp = pltpu.make_async_remote_copy(src_ref=x_ref, dst_ref=out_ref.at[me], send_sem=send_sems.at[k - 1],
                                              recv_sem=recv_sems.at[k - 1], device_id=peer, device_id_type=MESH)
            cp.start()
            sends.append(cp)
        for k in range(1, 8):
            peer = (_flip(mx, k & 4), _flip(my, k & 2), _flip(mc, k & 1))
            src = 4 * peer[0] + 2 * peer[1] + peer[2]
            pltpu.make_async_remote_copy(src_ref=x_ref, dst_ref=out_ref.at[src], send_sem=send_sems.at[k - 1],
                                         recv_sem=recv_sems.at[k - 1], device_id=peer, device_id_type=MESH).wait_recv()
        for cp in sends:
            cp.wait_send()
        mine.wait()

    return pl.pallas_call(
        body, name=name, out_shape=jax.ShapeDtypeStruct((8, r, c), x.dtype),
        in_specs=[pl.BlockSpec(memory_space=pltpu.VMEM)], out_specs=pl.BlockSpec(memory_space=pltpu.VMEM),
        scratch_shapes=[pltpu.SemaphoreType.DMA((7,)), pltpu.SemaphoreType.DMA((7,)), pltpu.SemaphoreType.DMA],
        compiler_params=_params(9 * _nbytes((r, c), x.dtype)),
    )(x)


def _chip_peers(mx, my):
    chips = [(1 - mx, my), (mx, 1 - my), (1 - mx, 1 - my)]
    return chips, [2 * cx + cy for (cx, cy) in chips]


_ANY = pl.BlockSpec(memory_space=pl.ANY)


def _row_half(mc, r):
    return pl.ds(pl.multiple_of(mc * (r // 2), 16), r // 2)


def _gather_groups(name, shards):
    ng = len(shards)

    def body(*refs):
        xs, outs = refs[:ng], refs[ng:2 * ng]
        send_sems, recv_sems = refs[2 * ng:]
        mx, my, mc = lax.axis_index("x"), lax.axis_index("y"), lax.axis_index("c")
        j = 2 * mx + my
        chips, idxs = _chip_peers(mx, my)
        sib = (mx, my, 1 - mc)

        def copy(k, src, dst, to):
            return pltpu.make_async_remote_copy(src_ref=src, dst_ref=dst, send_sem=send_sems.at[k],
                                                recv_sem=recv_sems.at[k], device_id=to, device_id_type=MESH)

        first, passed = [], []
        for g in range(ng):
            mine = _row_half(mc, shards[g].shape[1])
            for t, chip in enumerate(chips):
                cp = copy(6 * g + t, xs[g].at[:, mine], outs[g].at[j, :, mine], (*chip, mc))
                cp.start()
                first.append(cp)
        for g in range(ng):
            mine = _row_half(mc, shards[g].shape[1])
            for t, chip in enumerate(chips):
                landed = outs[g].at[idxs[t], :, mine]
                copy(6 * g + t, landed, landed, (*chip, mc)).wait_recv()
                fwd = copy(6 * g + 3 + t, landed, landed, sib)
                fwd.start()
                passed.append(fwd)
        for g in range(ng):
            theirs_half = _row_half(1 - mc, shards[g].shape[1])
            for t in range(3):
                theirs = outs[g].at[idxs[t], :, theirs_half]
                copy(6 * g + 3 + t, theirs, theirs, sib).wait_recv()
        for cp in first + passed:
            cp.wait_send()

    outs = pl.pallas_call(
        body, name=name, out_shape=[jax.ShapeDtypeStruct((4,) + x.shape, x.dtype) for x in shards],
        in_specs=[_ANY] * ng, out_specs=[_ANY] * ng,
        scratch_shapes=[pltpu.SemaphoreType.DMA((6 * ng,)), pltpu.SemaphoreType.DMA((6 * ng,))],
    )(*shards)
    return _place_own_slab(outs, shards)


def _place_own_slab(outs, shards):
    chip = 2 * lax.axis_index("x") + lax.axis_index("y")
    return [lax.dynamic_update_slice(o, x[None], (chip,) + (0,) * x.ndim) for o, x in zip(outs, shards)]


_HBM = pl.BlockSpec(memory_space=pltpu.HBM)
_SEM = pl.BlockSpec(memory_space=pltpu.SEMAPHORE)
_DATAFLOW = pltpu.SideEffectType.DATAFLOW_SIDE_EFFECTING


def _ici_gather_copies(src_refs, land_refs, send_sems, recv_sems, scatter=False):
    mx, my, mc = lax.axis_index("x"), lax.axis_index("y"), lax.axis_index("c")
    j = 2 * mx + my
    chips, idxs = _chip_peers(mx, my)
    sends, recvs = [], []
    for g, src in enumerate(src_refs):
        for t, chip in enumerate(chips):
            common = dict(send_sem=send_sems.at[3 * g + t], recv_sem=recv_sems.at[3 * g + t], device_id=(*chip, mc),
                          device_id_type=MESH)
            if scatter:
                out, to, frm = src.at[idxs[t]], land_refs[g].at[j], land_refs[g].at[idxs[t]]
            else:
                mine = _row_half(mc, src.shape[1])
                out, to, frm = src.at[:, mine], land_refs[g].at[j, :, mine], land_refs[g].at[idxs[t], :, mine]
            sends.append(pltpu.make_async_remote_copy(src_ref=out, dst_ref=to, **common))
            recvs.append(pltpu.make_async_remote_copy(src_ref=out, dst_ref=frm, **common))
    return sends, recvs


def _gather_start(name, shards, scatter=False):
    ng = len(shards)

    def body(*refs):
        srcs, lands = refs[:ng], refs[ng:2 * ng]
        send_sems, recv_sems = refs[2 * ng], refs[2 * ng + 1]
        token = refs[-1]
        sends, _ = _ici_gather_copies(srcs, lands, send_sems, recv_sems, scatter)
        for cp in sends:
            cp.start()
        token[...] = jnp.zeros(token.shape, token.dtype)

    land_shape = lambda x: x.shape if scatter else (4,) + x.shape
    srcs = [pltpu.with_memory_space_constraint(x, pltpu.HBM) for x in shards]
    lands = [pltpu.with_memory_space_constraint(lax.empty(land_shape(x), x.dtype), pltpu.HBM) for x in shards]
    res = pl.pallas_call(
        body, name=name,
        out_shape=(pltpu.SemaphoreType.DMA((3 * ng,)), pltpu.SemaphoreType.DMA((3 * ng,)),
                   *[pltpu.HBM(x.shape, x.dtype) for x in srcs], *[pltpu.HBM(x.shape, x.dtype) for x in lands],
                   jax.ShapeDtypeStruct((8, LANES), F32)),
        in_specs=[_HBM] * (2 * ng),
        out_specs=(_SEM, _SEM, *[_HBM] * (2 * ng), pl.BlockSpec(memory_space=pltpu.VMEM)),
        input_output_aliases={i: 2 + i for i in range(2 * ng)},
        compiler_params=pltpu.CompilerParams(has_side_effects=_DATAFLOW),
    )(*srcs, *lands)
    return dict(send_sems=res[0], recv_sems=res[1], srcs=list(res[2:2 + ng]), lands=list(res[2 + ng:2 + 2 * ng]),
                token=res[-1])


def _gather_wait(name, started, after, scatter=False):
    ng = len(started["srcs"])

    def body(*refs):
        srcs, lands = refs[:ng], refs[ng:2 * ng]
        send_sems, recv_sems = refs[2 * ng], refs[2 * ng + 1]
        sends, recvs = _ici_gather_copies(srcs, lands, send_sems, recv_sems, scatter)
        for cp in sends:
            cp.wait_send()
        for cp in recvs:
            cp.wait_recv()

    res = pl.pallas_call(
        body, name=name,
        out_shape=[pltpu.HBM(x.shape, x.dtype) for x in started["srcs"] + started["lands"]],
        in_specs=[_HBM] * (2 * ng) + [_SEM, _SEM, _ANY], out_specs=[_HBM] * (2 * ng),
        input_output_aliases={i: i for i in range(2 * ng)},
        compiler_params=pltpu.CompilerParams(has_side_effects=_DATAFLOW),
    )(*started["srcs"], *started["lands"], started["send_sems"], started["recv_sems"], after)
    return list(res[:ng]), list(res[ng:])


def _pair_forward_groups(name, lands, shards):
    ng = len(lands)

    def body(*refs):
        ins, outs = refs[:ng], refs[ng:2 * ng]
        send_sems, recv_sems = refs[2 * ng:]
        mx, my, mc = lax.axis_index("x"), lax.axis_index("y"), lax.axis_index("c")
        _, idxs = _chip_peers(mx, my)
        sib = (mx, my, 1 - mc)
        cps = []
        for g in range(ng):
            mine = _row_half(mc, lands[g].shape[2])
            for t in range(3):
                cp = pltpu.make_async_remote_copy(src_ref=ins[g].at[idxs[t], :, mine], dst_ref=outs[g].at[idxs[t], :, mine],
                                                  send_sem=send_sems.at[3 * g + t], recv_sem=recv_sems.at[3 * g + t],
                                                  device_id=sib, device_id_type=MESH)
                cp.start()
                cps.append(cp)
        for g in range(ng):
            theirs = _row_half(1 - mc, lands[g].shape[2])
            for t in range(3):
                pltpu.make_async_remote_copy(src_ref=ins[g].at[idxs[t], :, theirs], dst_ref=outs[g].at[idxs[t], :, theirs],
                                             send_sem=send_sems.at[3 * g + t], recv_sem=recv_sems.at[3 * g + t],
                                             device_id=sib, device_id_type=MESH).wait_recv()
        for cp in cps:
            cp.wait_send()

    outs = pl.pallas_call(
        body, name=name, out_shape=[jax.ShapeDtypeStruct(x.shape, x.dtype) for x in lands],
        in_specs=[_ANY] * ng, out_specs=[_ANY] * ng, input_output_aliases={i: i for i in range(ng)},
        scratch_shapes=[pltpu.SemaphoreType.DMA((3 * ng,)), pltpu.SemaphoreType.DMA((3 * ng,))],
    )(*lands)
    return _place_own_slab(outs, shards)


def _pair_swap_groups(name, gs):
    ng = len(gs)

    def body(*refs):
        xs, outs = refs[:ng], refs[ng:2 * ng]
        send_sems, recv_sems = refs[2 * ng:]
        mx, my, mc = lax.axis_index("x"), lax.axis_index("y"), lax.axis_index("c")
        cps = []
        for g in range(ng):
            cp = pltpu.make_async_remote_copy(src_ref=xs[g].at[:, :, _row_half(1 - mc, gs[g].shape[2])], dst_ref=outs[g],
                                              send_sem=send_sems.at[g], recv_sem=recv_sems.at[g],
                                              device_id=(mx, my, 1 - mc), device_id_type=MESH)
            cp.start()
            cps.append(cp)
        for cp in cps:
            cp.wait()

    return pl.pallas_call(
        body, name=name,
        out_shape=[jax.ShapeDtypeStruct(x.shape[:2] + (x.shape[2] // 2, x.shape[3]), x.dtype) for x in gs],
        in_specs=[_ANY] * ng, out_specs=[_ANY] * ng,
        scratch_shapes=[pltpu.SemaphoreType.DMA((ng,)), pltpu.SemaphoreType.DMA((ng,))],
    )(*gs)


def _chip_scatter_groups(name, ps):
    ng = len(ps)

    def body(*refs):
        xs, outs = refs[:ng], refs[ng:2 * ng]
        send_sems, recv_sems = refs[2 * ng:]
        mx, my, mc = lax.axis_index("x"), lax.axis_index("y"), lax.axis_index("c")
        j = 2 * mx + my
        chips, idxs = _chip_peers(mx, my)
        sends = []
        for g in range(ng):
            for t, chip in enumerate(chips):
                cp = pltpu.make_async_remote_copy(src_ref=xs[g].at[idxs[t]], dst_ref=outs[g].at[j],
                                                  send_sem=send_sems.at[3 * g + t], recv_sem=recv_sems.at[3 * g + t],
                                                  device_id=(*chip, mc), device_id_type=MESH)
                cp.start()
                sends.append(cp)
        for g in range(ng):
            for t, chip in enumerate(chips):
                pltpu.make_async_remote_copy(src_ref=xs[g].at[idxs[t]], dst_ref=outs[g].at[idxs[t]],
                                             send_sem=send_sems.at[3 * g + t], recv_sem=recv_sems.at[3 * g + t],
                                             device_id=(*chip, mc), device_id_type=MESH).wait_recv()
        for cp in sends:
            cp.wait_send()

    outs = pl.pallas_call(
        body, name=name, out_shape=[jax.ShapeDtypeStruct(x.shape, x.dtype) for x in ps],
        in_specs=[_ANY] * ng, out_specs=[_ANY] * ng,
        scratch_shapes=[pltpu.SemaphoreType.DMA((3 * ng,)), pltpu.SemaphoreType.DMA((3 * ng,))],
    )(*ps)
    return _place_own_part(outs, ps)


def _place_own_part(outs, ps):
    chip = 2 * lax.axis_index("x") + lax.axis_index("y")
    return [lax.dynamic_update_slice(o, lax.dynamic_index_in_dim(x, chip, 0, keepdims=True), (chip,) + (0,) * (x.ndim - 1))
            for o, x in zip(outs, ps)]


def _pair_merge_groups(name, fs):
    ng = len(fs)

    def body(*refs):
        xs, outs = refs[:ng], refs[ng:2 * ng]
        send_sems, recv_sems = refs[2 * ng:]
        mx, my, mc = lax.axis_index("x"), lax.axis_index("y"), lax.axis_index("c")
        cps = []
        for g in range(ng):
            mine = _row_half(mc, 2 * fs[g].shape[1])
            cp = pltpu.make_async_remote_copy(src_ref=xs[g], dst_ref=outs[g].at[:, mine], send_sem=send_sems.at[g],
                                              recv_sem=recv_sems.at[g], device_id=(mx, my, 1 - mc), device_id_type=MESH)
            cp.start()
            cps.append(cp)
        for g in range(ng):
            theirs = outs[g].at[:, _row_half(1 - mc, 2 * fs[g].shape[1])]
            pltpu.make_async_remote_copy(src_ref=xs[g], dst_ref=theirs, send_sem=send_sems.at[g],
                                         recv_sem=recv_sems.at[g], device_id=(mx, my, 1 - mc),
                                         device_id_type=MESH).wait_recv()
        for cp in cps:
            cp.wait_send()

    outs = pl.pallas_call(
        body, name=name,
        out_shape=[jax.ShapeDtypeStruct((x.shape[0], 2 * x.shape[1], x.shape[2]), x.dtype) for x in fs],
        in_specs=[_ANY] * ng, out_specs=[_ANY] * ng,
        scratch_shapes=[pltpu.SemaphoreType.DMA((ng,)), pltpu.SemaphoreType.DMA((ng,))],
    )(*fs)
    mc = lax.axis_index("c")
    return [lax.dynamic_update_slice(o, x, (0, mc * x.shape[1], 0)) for o, x in zip(outs, fs)]


def _block_rows(r, w, itemsize=4, budget=4 << 20):
    for c in (r, 2048, 1024, 512, 256, 128, 64, 32, 16):
        if c <= r and r % c == 0 and c * w * itemsize <= budget:
            return c
    return r


def _pair_sum(name, g, got, cidx):
    ns, t, r, w = g.shape
    rh = r // 2
    bm = _block_rows(rh, w)
    nb = rh // bm

    def body(c_ref, a_ref, b_ref, o_ref):
        o_ref[...] = (a_ref[...].astype(F32) + b_ref[...].astype(F32)).astype(o_ref.dtype)

    blk = (None, None, bm, w)
    return pl.pallas_call(
        body, name=name,
        grid_spec=pltpu.PrefetchScalarGridSpec(
            num_scalar_prefetch=1, grid=(ns, t, nb),
            in_specs=[pl.BlockSpec(blk, lambda s, tt, i, c: (s, tt, c[0] * nb + i, 0)),
                      pl.BlockSpec(blk, lambda s, tt, i, c: (s, tt, i, 0))],
            out_specs=pl.BlockSpec(blk, lambda s, tt, i, c: (s, tt, i, 0))),
        out_shape=jax.ShapeDtypeStruct((ns, t, rh, w), BF16),
        compiler_params=_params(3 * _nbytes((bm, w), F32)),
    )(cidx, g, got)


def _chip_sum(name, p):
    ns, th, r, w = p.shape
    bm = _block_rows(r, w, budget=2 << 20)

    def body(p_ref, o_ref):
        acc = p_ref[0].astype(F32)
        for s in range(1, ns):
            acc = acc + p_ref[s].astype(F32)
        o_ref[...] = acc

    return pl.pallas_call(
        body, name=name, grid=(th, r // bm),
        in_specs=[pl.BlockSpec((ns, None, bm, w), lambda tt, i: (0, tt, i, 0))],
        out_specs=pl.BlockSpec((None, bm, w), lambda tt, i: (tt, i, 0)),
        out_shape=jax.ShapeDtypeStruct((th, r, w), F32),
        compiler_params=_params(ns * _nbytes((bm, w), BF16) + 2 * _nbytes((bm, w), F32)),
    )(p)


def _sum_leading(name, x):
    n = x.shape[0]

    def body(p_ref, o_ref):
        acc = p_ref[0]
        for s in range(1, n):
            acc = acc + p_ref[s]
        o_ref[...] = acc

    return pl.pallas_call(body, name=name, out_shape=jax.ShapeDtypeStruct(x.shape[1:], F32),
                          compiler_params=_params(2 * _nbytes(x.shape, F32)))(x)


def _reduce_scatter_begin(tag, gs, overlap):
    cidx = lax.axis_index("c").astype(jnp.int32).reshape(1)
    got = _pair_swap_groups(tag + "_pair_swap", gs)
    pair = [_pair_sum(f"{tag}_pair_sum{i}", g, r_, cidx) for i, (g, r_) in enumerate(zip(gs, got))]
    if overlap:
        return _gather_start(tag + "_scatter_start", pair, scatter=True)
    return _chip_scatter_groups(tag + "_chip_scatter", pair)


def _reduce_scatter_end(tag, state, overlap, after):
    if overlap:
        srcs, lands = _gather_wait(tag + "_scatter_wait", state, after, scatter=True)
        state = _place_own_part(lands, srcs)
    fin = [_chip_sum(f"{tag}_chip_sum{i}", p) for i, p in enumerate(state)]
    return _pair_merge_groups(tag + "_pair_merge", fin)


_GROUPS = ((("ffn1_wg", "ffn1_wu", "ffn2_wg", "ffn2_wu"), 1), (("ffn1_wd", "ffn2_wd"), 0), (("w_a",), 0),
           (("w_o",), 0), (("w_in",), 1), (("w_b",), 1))


def _shard_major(g, ax):
    k, n = g.shape
    if ax == 0:
        return g.reshape(4, k // 4, n)
    return g.reshape(k, 4, n // 4).transpose(1, 0, 2)


def _in_cols(d):
    o1 = 3 * DN_WIDTH
    o2 = o1 + DN_WIDTH
    o3 = o2 + 2 * DN_HEADS
    o4 = o3 + 3 * DA_WIDTH
    return dict(wq=(0, o1), wz=(o1, o2), wba=(o2, o3), wda=(o3, o4), wg=(o4, o4 + 2 * d))


def _mixer_weights(w_in, w_a, w_b, w_o, d):
    w = {k: w_in[:, a:b] for k, (a, b) in _in_cols(d).items()}
    w["wba"] = jnp.pad(w["wba"], ((0, 0), (0, LANES - 2 * DN_HEADS)))
    w["w_a"], w["w_b"], w["w_o"] = w_a, w_b, w_o
    return w


def _w_in_grad(wg):
    return jnp.concatenate([wg["wq"], wg["wz"], wg["wba"][:, :2 * DN_HEADS], wg["wda"], wg["wg"]], axis=1)


def _adam_math(wv, gv, mv, vv):
    mn = ADAM_B1 * mv + (1.0 - ADAM_B1) * gv
    vn = ADAM_B2 * vv + (1.0 - ADAM_B2) * jnp.square(gv)
    m_hat = mn / (1.0 - ADAM_B1 ** ADAM_STEP)
    v_hat = vn / (1.0 - ADAM_B2 ** ADAM_STEP)
    delta = -ADAM_LR * (m_hat / (jnp.sqrt(v_hat) + ADAM_EPS) + ADAM_WD * wv)
    return delta, mn, vn


def _adamw(name, w, g, m, v):
    shape = w.shape
    cols = shape[-1]
    w2, g2, m2, v2 = (t.reshape(-1, cols) for t in (w, g, m, v))
    rows = w2.shape[0]
    bm = _pick(rows, (256, 128, 64, 32, 16, 8)) if rows >= 8 else rows
    delta, mn, vn = _rowwise(name, lambda *t: (_adam_math(*t), ()), [w2, g2, m2, v2], [], [(cols, F32)] * 3, bm=bm)
    return delta.reshape(shape), mn.reshape(shape), vn.reshape(shape)


def _adamw_leading(name, w, g, m, v):
    n = w.shape[0]
    padded_row = -(-w.shape[1] // 8) * 8 * w.shape[2] * 4
    bm = max(c for c in range(1, n + 1) if n % c == 0 and (c * padded_row <= (1 << 20) or c == 1))

    def body(w_ref, g_ref, m_ref, v_ref, d_ref, mo_ref, vo_ref):
        d_ref[...], mo_ref[...], vo_ref[...] = _adam_math(w_ref[...], g_ref[...], m_ref[...], v_ref[...])

    spec = pl.BlockSpec((bm,) + w.shape[1:], lambda i: (i, 0, 0))
    return pl.pallas_call(
        body, name=name, grid=(n // bm,), in_specs=[spec] * 4, out_specs=[spec] * 3,
        out_shape=[jax.ShapeDtypeStruct(w.shape, F32)] * 3, compiler_params=_params(7 * bm * padded_row),
    )(w, g, m, v)


def _adamw_stacked(name, w, m, v, gstacks, slot):
    depth, r, cdim = w.shape
    bm = _block_rows(r, cdim, budget=1 << 20)

    def body(w_ref, m_ref, v_ref, *rest):
        g_refs, (go_ref, d_ref, mo_ref, vo_ref) = rest[:depth], rest[depth:]
        layer = pl.program_id(0)
        gv = g_refs[0][...]
        for l in range(1, depth):
            gv = jnp.where(layer == l, g_refs[l][...], gv)
        go_ref[...] = gv
        d_ref[...], mo_ref[...], vo_ref[...] = _adam_math(w_ref[...], gv, m_ref[...], v_ref[...])

    nat = pl.BlockSpec((None, bm, cdim), lambda l, i: (l, i, 0))
    return pl.pallas_call(
        body, name=name, grid=(depth, r // bm),
        in_specs=[nat, nat, nat] + [pl.BlockSpec((None, bm, cdim), lambda l, i: (slot, i, 0))] * depth,
        out_specs=[nat] * 4, out_shape=[jax.ShapeDtypeStruct(w.shape, F32)] * 4,
        compiler_params=_params((7 + depth) * _nbytes((bm, cdim), F32)),
    )(w, m, v, *gstacks)


def kernel(x, c, ada_w, ada_b, ln_ffn1, ln_mix, ln_ffn2, ffn1_wg, ffn1_wu, ffn1_wd, w_in, conv_w, a_log, dt_bias, dn_norm, w_a, w_b, w_o, ffn2_wg, ffn2_wu, ffn2_wd, final_norm, loss_target, m_ada_w, m_ada_b, m_ln_ffn1, m_ln_mix, m_ln_ffn2, m_ffn1_wg, m_ffn1_wu, m_ffn1_wd, m_w_in, m_conv_w, m_a_log, m_dt_bias, m_dn_norm, m_w_a, m_w_b, m_w_o, m_ffn2_wg, m_ffn2_wu, m_ffn2_wd, m_final_norm, v_ada_w, v_ada_b, v_ln_ffn1, v_ln_mix, v_ln_ffn2, v_ffn1_wg, v_ffn1_wu, v_ffn1_wd, v_w_in, v_conv_w, v_a_log, v_dt_bias, v_dn_norm, v_w_a, v_w_b, v_w_o, v_ffn2_wg, v_ffn2_wu, v_ffn2_wd, v_final_norm):
    names = ["ada_w", "ada_b", "ln_ffn1", "ln_mix", "ln_ffn2", "ffn1_wg", "ffn1_wu", "ffn1_wd", "w_in", "conv_w",
             "a_log", "dt_bias", "dn_norm", "w_a", "w_b", "w_o", "ffn2_wg", "ffn2_wu", "ffn2_wd", "final_norm"]
    wts = dict(zip(names, (ada_w, ada_b, ln_ffn1, ln_mix, ln_ffn2, ffn1_wg, ffn1_wu, ffn1_wd, w_in, conv_w, a_log,
                           dt_bias, dn_norm, w_a, w_b, w_o, ffn2_wg, ffn2_wu, ffn2_wd, final_norm)))
    mom = dict(zip(names, (m_ada_w, m_ada_b, m_ln_ffn1, m_ln_mix, m_ln_ffn2, m_ffn1_wg, m_ffn1_wu, m_ffn1_wd, m_w_in,
                           m_conv_w, m_a_log, m_dt_bias, m_dn_norm, m_w_a, m_w_b, m_w_o, m_ffn2_wg, m_ffn2_wu,
                           m_ffn2_wd, m_final_norm)))
    var = dict(zip(names, (v_ada_w, v_ada_b, v_ln_ffn1, v_ln_mix, v_ln_ffn2, v_ffn1_wg, v_ffn1_wu, v_ffn1_wd, v_w_in,
                           v_conv_w, v_a_log, v_dt_bias, v_dn_norm, v_w_a, v_w_b, v_w_o, v_ffn2_wg, v_ffn2_wu,
                           v_ffn2_wd, v_final_norm)))
    _, s, d = x.shape
    depth = ada_w.shape[0]
    mx, my, mc = lax.axis_index("x"), lax.axis_index("y"), lax.axis_index("c")
    chip = 2 * mx + my
    me = 2 * chip + mc
    nshard = ada_w.shape[2]

    cact = _rowwise("c_silu", lambda cv: ((_silu(cv),), ()), [jnp.pad(c, ((0, 7), (0, 0)))], [], [(d, F32)], bm=8)[0]
    c_all = _allgather8("ag_c", cact)[:, 0, :]
    conv_all = _allgather8("ag_conv", jnp.pad(conv_w.reshape(depth * DN_CONV, -1), ((0, 8 - depth * DN_CONV), (0, 0))))
    conv_full = jnp.concatenate([conv_all[2 * j, :depth * DN_CONV] for j in range(4)], axis=1)
    conv_full = conv_full.reshape(depth, DN_CONV, 3 * DN_WIDTH)
    layer_shards = [[jnp.stack([wts[nm][l].astype(BF16) for nm in nms], axis=0) for nms, _ in _GROUPS]
                    for l in range(depth)]
    gathered0 = _gather_groups("ag_weights0", layer_shards[0])
    rows_of = lambda st: st[:, 0].reshape(-1, st.shape[-1])
    cols_of = lambda st: jnp.concatenate([st[j, 0] for j in range(4)], axis=1)

    def layer_weights(l, after):
        if l == 0:
            got = gathered0
        else:
            srcs, lands = _gather_wait(f"ag_weights{l}_wait", started[l], after)
            got = _pair_forward_groups(f"ag_weights{l}_pair", lands, srcs)
        ga, gb, g_wa, g_wo, g_win, g_wb = got
        return ga, gb, _mixer_weights(cols_of(g_win), rows_of(g_wa), cols_of(g_wb), rows_of(g_wo), d)

    c16 = jnp.pad(c_all, ((0, 8), (0, 0))).astype(BF16)
    parts = []
    for l in range(depth):
        bias = lax.dynamic_slice(ada_b[l], (chip * nshard,), (nshard,)).reshape(1, nshard)
        (mp,) = _matmul(f"ada_fwd{l}", c16, ada_w[l].astype(BF16), epi_bcast=[bias], epi=lambda acc, b: (acc + b,))
        parts.append(mp)
    mod_all = _allgather8("ag_mod", jnp.concatenate(parts, axis=0))
    mod_rows = jnp.concatenate([mod_all[2 * j] for j in range(4)], axis=1)
    mod = jnp.stack([lax.dynamic_index_in_dim(mod_rows, l * 16 + me, axis=0, keepdims=False) for l in range(depth)])

    gathered0, later, mod, conv_full = lax.optimization_barrier((gathered0, layer_shards[1:], mod, conv_full))
    started = {l: _gather_start(f"ag_weights{l}_start", later[l - 1]) for l in range(1, depth)}
    for st in started.values():
        mod = mod + st["token"][0, 0]
    small = dict(conv_w=conv_full, a_log=a_log, dt_bias=dt_bias, dn_norm=dn_norm, ln_ffn1=ln_ffn1, ln_mix=ln_mix,
                 ln_ffn2=ln_ffn2, final_norm=final_norm)
    ffn_names = _GROUPS[0][0] + _GROUPS[1][0]
    rs_state = {}

    def on_layer_grads(l, wg):
        wg["w_in"] = _w_in_grad(wg)
        gs = [jnp.stack([wg[nm] if nm in ffn_names else _shard_major(wg[nm], ax) for nm in nms], axis=1)
              for nms, ax in _GROUPS]
        rs_state[l] = _reduce_scatter_begin(f"rs{l}", gs, overlap=l > 0)
        return rs_state[l]["token"][0, 0] if l > 0 else None

    loss_part, dx, dmod, sgrads, d_fnorm = _local_step(x[0], loss_target[0], mod, layer_weights, small,
                                                       on_layer_grads)
    reduced = [_reduce_scatter_end(f"rs{l}", rs_state[l], l > 0, dx) for l in range(depth)]

    dmod_all = _allgather8("ag_dmod", jnp.pad(dmod, ((0, 8 - depth), (0, 0))))
    g_ada_w, g_ada_b = [], []
    for l in range(depth):
        dm_l = dmod_all[:, l, :]
        (gb_l,) = _rowwise(f"ada_b_grad{l}", lambda v: ((), (jnp.sum(v, axis=0, keepdims=True),)), [dm_l], [], [],
                           [(1, N_ADA * d)], bm=8)
        g_ada_b.append(gb_l[0])
        dm_sh = lax.dynamic_slice(dm_l, (0, chip * nshard), (8, nshard))
        (gw_l,) = _matmul(f"ada_w_grad{l}", c16, jnp.pad(dm_sh, ((0, 8), (0, 0))).astype(BF16), ta=True)
        g_ada_w.append(gw_l)
    grads = dict(ada_w=jnp.stack(g_ada_w), ada_b=jnp.stack(g_ada_b))

    smalls = [loss_part.reshape(1), d_fnorm]
    for l in range(depth):
        sg = sgrads[l]
        smalls += [sg["ln_ffn1"], sg["ln_mix"], sg["ln_ffn2"], sg["a_log"], sg["dt_bias"], sg["dn_norm"],
                   sg["conv_w"].reshape(-1)]
    sizes = [t.shape[0] for t in smalls]
    tile = 8 * LANES
    flat = jnp.concatenate([jnp.pad(t, (0, (-t.shape[0]) % tile)).reshape(-1, LANES) for t in smalls], axis=0)
    tot = _sum_leading("small_sum", _allgather8("ag_small", flat))
    offs, acc = [], 0
    for n_ in sizes:
        offs.append(acc)
        acc += -(-n_ // tile) * 8
    take = lambda i: tot[offs[i]:offs[i] + -(-sizes[i] // tile) * 8].reshape(-1)[:sizes[i]]
    loss = take(0)[0]
    grads["final_norm"] = take(1)
    per = 7
    for key_i, key in enumerate(["ln_ffn1", "ln_mix", "ln_ffn2", "a_log", "dt_bias", "dn_norm"]):
        grads[key] = jnp.stack([take(2 + per * l + key_i) for l in range(depth)])
    conv_g = jnp.stack([take(2 + per * l + 6).reshape(DN_CONV, 3 * DN_WIDTH) for l in range(depth)])
    csh = conv_w.shape[2]
    grads["conv_w"] = lax.dynamic_slice(conv_g, (0, 0, chip * csh), (depth, DN_CONV, csh))

    deltas, new_m, new_v = {}, {}, {}
    for gi, (nms, ax) in enumerate(_GROUPS):
        for q, nm in enumerate(nms):
            wv, mv, vv = wts[nm], mom[nm], var[nm]
            per_layer = [reduced[l][gi][q] for l in range(depth)]
            if ax == 1 and wv.shape[2] % LANES and nm != "w_in":
                tr = lambda t: jnp.swapaxes(t, 1, 2)
                gt = jnp.stack([g.T for g in per_layer], axis=0)
                dl, mn, vn = _adamw("adamw_" + nm, tr(wv), gt, tr(mv), tr(vv))
                grads[nm], deltas[nm], new_m[nm], new_v[nm] = tr(gt), tr(dl), tr(mn), tr(vn)
            elif nm == "w_in" and wv.shape[2] % LANES:
                tr = lambda t: jnp.transpose(t, (2, 0, 1))
                back = lambda t: jnp.transpose(t, (1, 2, 0))
                gt = jnp.stack([g.T for g in per_layer], axis=1)
                dl, mn, vn = _adamw_leading("adamw_" + nm, tr(wv), gt, tr(mv), tr(vv))
                grads[nm], deltas[nm], new_m[nm], new_v[nm] = back(gt), back(dl), back(mn), back(vn)
            else:
                grads[nm], deltas[nm], new_m[nm], new_v[nm] = _adamw_stacked(
                    "adamw_" + nm, wv, mv, vv, [reduced[l][gi] for l in range(depth)], q)

    for name in names:
        if name in deltas:
            continue
        wv, gv, mv, vv = wts[name], grads[name], mom[name], var[name]
        if wv.ndim == 1:
            wv, gv, mv, vv = (t.reshape(-1, LANES) for t in (wv, gv, mv, vv))
        dl, mn, vn = _adamw("adamw_" + name, wv, gv, mv, vv)
        deltas[name], new_m[name], new_v[name] = (t.reshape(wts[name].shape) for t in (dl, mn, vn))
    return (loss, dx.reshape(1, s, d), *[grads[n_] for n_ in names], *[deltas[n_] for n_ in names],
            *[new_m[n_] for n_ in names], *[new_v[n_] for n_ in names])
```

```python
import functools

import jax
import jax.numpy as jnp
from jax import lax
from jax.experimental import pallas as pl
from jax.experimental.pallas import tpu as pltpu

F32 = jnp.float32
BF16 = jnp.bfloat16
MESH = pl.DeviceIdType.MESH

NORM_EPS = 1e-6
DN_HEADS, DN_DIM, DN_CHUNK, DN_CONV = 8, 128, 64, 4
DN_WIDTH = DN_HEADS * DN_DIM
DA_HEADS, DA_DIM, DA_BLOCK = 12, 64, 128
DA_WIDTH = DA_HEADS * DA_DIM
DA_PATTERNS = ((128, 1), (512, 4), (2048, 16))
ALIBI_MAX_EXP = 8.0
N_ADA = 9
LANES = 128
V7X_VMEM_BYTES = 64 << 20
ADAM_LR, ADAM_B1, ADAM_B2, ADAM_EPS, ADAM_WD, ADAM_STEP = 0.001, 0.9, 0.999, 1e-08, 0.01, 10
NEG = -1e30
HI = lax.Precision.HIGHEST
NN = (((1,), (0,)), ((), ()))
NT = (((1,), (1,)), ((), ()))
TN = (((0,), (0,)), ((), ()))


def _nbytes(shape, dtype):
    n = 1
    for s in shape:
        n *= s
    return n * jnp.dtype(dtype).itemsize


def _params(block_bytes, scratch_bytes=0):
    need = 2 * block_bytes + scratch_bytes
    lim = min(max(need + need // 4 + (4 << 20), 56 << 20), V7X_VMEM_BYTES - (6 << 20))
    return pltpu.CompilerParams(vmem_limit_bytes=int(lim))


def _pick(n, cands):
    for c in cands:
        if c <= n and n % c == 0:
            return c
    return n


def _sigmoid(x):
    return jax.nn.sigmoid(x)


def _silu(x):
    return x * jax.nn.sigmoid(x)


def _softplus(x):
    return jnp.maximum(x, 0.0) + jnp.log(1.0 + jnp.exp(-jnp.abs(x)))


def _rowwise(name, fn, rows, bcast, row_outs, red_outs=(), bm=256):
    rows = [r if isinstance(r, tuple) else (r, r.shape[1], 0) for r in rows]
    s = rows[0][0].shape[0]
    bm = _pick(s, (bm, 128, 64, 32, 16, 8))
    nr, nb, no, nd = len(rows), len(bcast), len(row_outs), len(red_outs)
    in_specs = [pl.BlockSpec((bm, w), functools.partial(lambda i, ci: (i, ci), ci=ci)) for (_, w, ci) in rows]
    in_specs += [pl.BlockSpec(b.shape, lambda i: (0, 0)) for b in bcast]
    out_shape = [jax.ShapeDtypeStruct((s, w), dt) for (w, dt) in row_outs]
    out_shape += [jax.ShapeDtypeStruct((r, w), F32) for (r, w) in red_outs]
    out_specs = [pl.BlockSpec((bm, w), lambda i: (i, 0)) for (w, _) in row_outs]
    out_specs += [pl.BlockSpec((r, w), lambda i: (0, 0)) for (r, w) in red_outs]

    def body(*refs):
        ins = [r[...] for r in refs[:nr + nb]]
        outs = refs[nr + nb:nr + nb + no]
        reds = refs[nr + nb + no:]
        ov, rv = fn(*ins)
        for o, v in zip(outs, ov):
            o[...] = v.astype(o.dtype)
        if nd:
            @pl.when(pl.program_id(0) == 0)
            def _():
                for r in reds:
                    r[...] = jnp.zeros(r.shape, F32)
            for r, v in zip(reds, rv):
                r[...] += v.astype(F32)

    blk = sum(_nbytes((bm, w), a.dtype) for (a, w, _) in rows) + sum(_nbytes(b.shape, b.dtype) for b in bcast)
    blk += sum(_nbytes((bm, w), dt) for (w, dt) in row_outs) + sum(_nbytes(r, F32) for r in red_outs)
    res = pl.pallas_call(
        body, name=name, grid=(s // bm,), in_specs=in_specs, out_specs=out_specs, out_shape=out_shape,
        compiler_params=_params(3 * blk),
    )(*[a for (a, _, _) in rows], *bcast)
    return res


def _matmul(name, a, b, *, ta=False, tb=False, outs=(F32,), epi=None, epi_rows=(), epi_bcast=(),
            bm=None, bn=None, bk=None):
    if ta:
        k, m = a.shape
    else:
        m, k = a.shape
    n = b.shape[0] if tb else b.shape[1]
    assert (b.shape[1] if tb else b.shape[0]) == k, (name, a.shape, b.shape)
    if bm is None:
        bm = _pick(m, (1024, 1408, 768, 512, 384, 256, 128)) if ta else _pick(m, (1024, 512, 256, 128, 64, 32, 16))
    if bn is None:
        bn = _pick(n, (512, 384, 256, 128))
    if bk is None:
        bk = k if k <= 3072 else _pick(k, (2816, 2048, 1024, 512))
        if ta:
            bk = _pick(k, (1024, 512, 256, 128, 64, 32, 16))
    nk = k // bk
    dims = TN if ta else (NT if tb else NN)
    a_spec = pl.BlockSpec((bk, bm), lambda i, j, kk: (kk, i)) if ta else pl.BlockSpec((bm, bk), lambda i, j, kk: (i, kk))
    b_spec = pl.BlockSpec((bn, bk), lambda i, j, kk: (j, kk)) if tb else pl.BlockSpec((bk, bn), lambda i, j, kk: (kk, j))
    in_specs = [a_spec, b_spec]
    in_specs += [pl.BlockSpec((bm, bn), lambda i, j, kk: (i, j)) for _ in epi_rows]
    in_specs += [pl.BlockSpec((1, bn), lambda i, j, kk: (0, j)) for _ in epi_bcast]
    out_shape = [jax.ShapeDtypeStruct((m, n), dt) for dt in outs]
    out_specs = [pl.BlockSpec((bm, bn), lambda i, j, kk: (i, j)) for _ in outs]
    ner, neb, no = len(epi_rows), len(epi_bcast), len(outs)

    def body(*refs):
        a_ref, b_ref = refs[0], refs[1]
        extra = refs[2:2 + ner + neb]
        out_refs = refs[2 + ner + neb:2 + ner + neb + no]
        prod = lax.dot_general(a_ref[...], b_ref[...], dims, preferred_element_type=F32)

        def finish(acc):
            vals = epi(acc, *[r[...] for r in extra]) if epi is not None else (acc,)
            for o, v in zip(out_refs, vals):
                o[...] = v.astype(o.dtype)

        if nk == 1:
            finish(prod)
        else:
            acc_ref = refs[-1]
            kk = pl.program_id(2)

            @pl.when(kk == 0)
            def _():
                acc_ref[...] = prod

            @pl.when(kk > 0)
            def _():
                acc_ref[...] += prod

            @pl.when(kk == nk - 1)
            def _():
                finish(acc_ref[...])

    blk = _nbytes((bm, bk), a.dtype) + _nbytes((bk, bn), b.dtype)
    blk += sum(_nbytes((bm, bn), r.dtype) for r in epi_rows) + sum(_nbytes((bm, bn), dt) for dt in outs)
    scratch = [pltpu.VMEM((bm, bn), F32)] if nk > 1 else []
    res = pl.pallas_call(
        body, name=name, grid=(m // bm, n // bn, nk), in_specs=in_specs, out_specs=out_specs,
        out_shape=out_shape, scratch_shapes=scratch,
        compiler_params=_params(blk, 3 * _nbytes((bm, bn), F32)),
    )(a, b, *epi_rows, *epi_bcast)
    return res


def _mm_core(name, grid, nk, pairs, out_defs, acc_shape, epi=None, epi_ins=()):
    npair, nep, no = len(pairs), len(epi_ins), len(out_defs)

    def body(*refs):
        extra = refs[2 * npair:2 * npair + nep]
        out_refs = refs[2 * npair + nep:2 * npair + nep + no]
        prod = None
        for p in range(npair):
            d = lax.dot_general(refs[2 * p][...], refs[2 * p + 1][...], pairs[p][4], preferred_element_type=F32)
            prod = d if prod is None else prod + d

        def finish(acc):
            vals = epi(acc, *[r[...] for r in extra]) if epi is not None else (acc,)
            for o, v in zip(out_refs, vals):
                o[...] = v.astype(o.dtype)

        if nk == 1:
            finish(prod)
        else:
            acc_ref = refs[-1]
            kk = pl.program_id(2)

            @pl.when(kk == 0)
            def _():
                acc_ref[...] = prod

            @pl.when(kk > 0)
            def _():
                acc_ref[...] += prod

            @pl.when(kk == nk - 1)
            def _():
                finish(acc_ref[...])

    def blk_bytes(spec, dtype):
        return _nbytes([s for s in spec.block_shape if s is not None], dtype)

    blk = sum(blk_bytes(sa, a.dtype) + blk_bytes(sb, b.dtype) for (a, sa, b, sb, _) in pairs)
    blk += sum(blk_bytes(sp, arr.dtype) for (arr, sp) in epi_ins) + sum(blk_bytes(sp, dt) for (_, dt, sp) in out_defs)
    ins, in_specs = [], []
    for (a, sa, b, sb, _) in pairs:
        ins += [a, b]
        in_specs += [sa, sb]
    ins += [arr for (arr, _) in epi_ins]
    in_specs += [sp for (_, sp) in epi_ins]
    return pl.pallas_call(
        body, name=name, grid=grid, in_specs=in_specs, out_specs=[sp for (_, _, sp) in out_defs],
        out_shape=[jax.ShapeDtypeStruct(sh, dt) for (sh, dt, _) in out_defs],
        scratch_shapes=[pltpu.VMEM(acc_shape, F32)] if nk > 1 else [],
        compiler_params=_params(blk, 3 * _nbytes(acc_shape, F32)),
    )(*ins)


def _rms_mod(h, ln, sh, sc):
    n = h * lax.rsqrt(jnp.mean(h * h, axis=-1, keepdims=True) + NORM_EPS) * ln
    return n * (1.0 + sc) + sh


def _swiglu_act(g, u):
    return _silu(g.astype(F32)) * u.astype(F32)


def _dn_prep(yc, pba, alog, dtb):
    act = _silu(yc)
    parts = []
    for idx in range(2 * DN_HEADS):
        seg = act[:, idx * DN_DIM:(idx + 1) * DN_DIM]
        seg = seg * lax.rsqrt(jnp.sum(seg * seg, axis=-1, keepdims=True) + NORM_EPS)
        if idx < DN_HEADS:
            seg = seg * (DN_DIM ** -0.5)
        parts.append(seg)
    parts.append(act[:, 2 * DN_WIDTH:])
    qkvn = jnp.concatenate(parts, axis=1)
    lane = lax.broadcasted_iota(jnp.int32, pba.shape, 1)
    beta = _sigmoid(pba)
    g = -jnp.exp(alog) * _softplus(pba + dtb)
    gb = jnp.where(lane < DN_HEADS, beta, jnp.where(lane < 2 * DN_HEADS, g, 0.0))
    return qkvn, gb


def _dn_outnorm(o_a, z, dn):
    parts = []
    for h in range(DN_HEADS):
        seg = o_a[:, h * DN_DIM:(h + 1) * DN_DIM]
        seg = seg * lax.rsqrt(jnp.mean(seg * seg, axis=-1, keepdims=True) + NORM_EPS) * dn
        parts.append(seg)
    return jnp.concatenate(parts, axis=1) * _silu(z)


def _shift_down(x, halo8, s):
    r = pltpu.roll(x, s, axis=0)
    top = pltpu.roll(halo8, s, axis=0)
    i8 = lax.broadcasted_iota(jnp.int32, top.shape, 0)
    return jnp.concatenate([jnp.where(i8 < s, top, r[0:8]), r[8:]], axis=0)


def _shift_up(x, halo8, s):
    m = x.shape[0]
    r = pltpu.roll(x, m - s, axis=0)
    bot = pltpu.roll(halo8, 8 - s, axis=0)
    i8 = lax.broadcasted_iota(jnp.int32, bot.shape, 0)
    return jnp.concatenate([r[:m - 8], jnp.where(i8 >= 8 - s, bot, r[m - 8:])], axis=0)


def _conv_prep_fwd(name, pq, convw8, pba, alog, dtb, bm=256):
    s, w = pq.shape
    nblk = s // bm
    hb = bm // 16

    def body(x_ref, halo_ref, w_ref, pba_ref, alog_ref, dtb_ref, yc_ref, qkv_ref, gb_ref):
        i = pl.program_id(0)
        x = x_ref[...].astype(F32)
        halo = jnp.where(i > 0, halo_ref[...].astype(F32)[8:16], 0.0)
        cw = w_ref[...]
        y = x * cw[DN_CONV - 1:DN_CONV]
        for sft in range(1, DN_CONV):
            y = y + _shift_down(x, halo, sft) * cw[DN_CONV - 1 - sft:DN_CONV - sft]
        ycb = y.astype(BF16)
        yc_ref[...] = ycb
        qkvn, gb = _dn_prep(ycb.astype(F32), pba_ref[...], alog_ref[...], dtb_ref[...])
        qkv_ref[...] = qkvn.astype(BF16)
        gb_ref[...] = gb

    blk = 3 * _nbytes((bm, w), BF16) + 4 * _nbytes((bm, w), F32)
    return pl.pallas_call(
        body, name=name, grid=(nblk,),
        in_specs=[pl.BlockSpec((bm, w), lambda i: (i, 0)),
                  pl.BlockSpec((16, w), lambda i: (jnp.maximum(i * hb - 1, 0), 0)),
                  pl.BlockSpec(convw8.shape, lambda i: (0, 0)),
                  pl.BlockSpec((bm, LANES), lambda i: (i, 0)),
                  pl.BlockSpec((1, LANES), lambda i: (0, 0)),
                  pl.BlockSpec((1, LANES), lambda i: (0, 0))],
        out_specs=[pl.BlockSpec((bm, w), lambda i: (i, 0)), pl.BlockSpec((bm, w), lambda i: (i, 0)),
                   pl.BlockSpec((bm, LANES), lambda i: (i, 0))],
        out_shape=[jax.ShapeDtypeStruct((s, w), BF16), jax.ShapeDtypeStruct((s, w), BF16),
                   jax.ShapeDtypeStruct((s, LANES), F32)],
        compiler_params=_params(blk),
    )(pq, pq, convw8, pba, alog, dtb)


def _conv_bwd(name, dyc, pq, convw8, bm=256):
    s, w = pq.shape
    nblk = s // bm
    hb = bm // 16

    def body(dy_ref, dyn_ref, x_ref, xh_ref, w_ref, dx_ref, dw_ref):
        i = pl.program_id(0)
        dy = dy_ref[...].astype(F32)
        nxt = jnp.where(i < nblk - 1, dyn_ref[...].astype(F32)[0:8], 0.0)
        x = x_ref[...].astype(F32)
        halo = jnp.where(i > 0, xh_ref[...].astype(F32)[8:16], 0.0)
        cw = w_ref[...]
        dx = dy * cw[DN_CONV - 1:DN_CONV]
        for sft in range(1, DN_CONV):
            dx = dx + _shift_up(dy, nxt, sft) * cw[DN_CONV - 1 - sft:DN_CONV - sft]
        dx_ref[...] = dx.astype(dx_ref.dtype)
        r8 = lax.broadcasted_iota(jnp.int32, (8, w), 0)
        dw = jnp.zeros((8, w), F32)
        for j in range(DN_CONV):
            sft = DN_CONV - 1 - j
            xs = x if sft == 0 else _shift_down(x, halo, sft)
            dw = dw + jnp.where(r8 == j, jnp.sum(dy * xs, axis=0, keepdims=True), 0.0)

        @pl.when(i == 0)
        def _():
            dw_ref[...] = jnp.zeros((8, w), F32)
        dw_ref[...] += dw

    blk = 4 * _nbytes((bm, w), BF16) + 5 * _nbytes((bm, w), F32)
    return pl.pallas_call(
        body, name=name, grid=(nblk,),
        in_specs=[pl.BlockSpec((bm, w), lambda i: (i, 0)),
                  pl.BlockSpec((16, w), lambda i: (jnp.minimum((i + 1) * hb, s // 16 - 1), 0)),
                  pl.BlockSpec((bm, w), lambda i: (i, 0)),
                  pl.BlockSpec((16, w), lambda i: (jnp.maximum(i * hb - 1, 0), 0)),
                  pl.BlockSpec(convw8.shape, lambda i: (0, 0))],
        out_specs=[pl.BlockSpec((bm, w), lambda i: (i, 0)), pl.BlockSpec((8, w), lambda i: (0, 0))],
        out_shape=[jax.ShapeDtypeStruct((s, w), BF16), jax.ShapeDtypeStruct((8, w), F32)],
        compiler_params=_params(blk),
    )(dyc, dyc, pq, pq, convw8)


BNN = (((2,), (1,)), ((0,), (0,)))
BNT = (((2,), (2,)), ((0,), (0,)))
BTN = (((1,), (1,)), ((0,), (0,)))


def _raw_dot_1pass(a, b, dims):
    return lax.dot_general(a.astype(BF16), b.astype(BF16), dims, preferred_element_type=F32)


def _raw_dot_3pass(a, b, dims):
    ah = a.astype(BF16)
    al = (a - ah.astype(F32)).astype(BF16)
    bh = b.astype(BF16)
    bl = (b - bh.astype(F32)).astype(BF16)
    d = lambda x, y: lax.dot_general(x, y, dims, preferred_element_type=F32)
    return d(ah, bh) + (d(ah, bl) + d(al, bh))


def _with_same_precision_vjp(raw):
    @functools.partial(jax.custom_vjp, nondiff_argnums=(2,))
    def dot(a, b, dims):
        return raw(a, b, dims)

    def fwd(a, b, dims):
        return raw(a, b, dims), (a, b)

    def bwd(dims, res, ct):
        a, b = res
        if dims == BNN:
            return raw(ct, b, BNT), raw(a, ct, BTN)
        if dims == BNT:
            return raw(ct, b, BNN), raw(ct, a, BTN)
        assert dims == BTN
        return raw(b, ct, BNT), raw(a, ct, BNN)

    dot.defvjp(fwd, bwd)
    return dot


_dot_1pass_vjp = _with_same_precision_vjp(_raw_dot_1pass)
_dot_3pass_vjp = _with_same_precision_vjp(_raw_dot_3pass)


def _dot_bf16(a, b, dims=BNN):
    return _dot_1pass_vjp(a, b, dims)


def _dot_3pass(a, b, dims=BNN):
    return _dot_3pass_vjp(a, b, dims)


def _neumann_inverse(x):
    h, c, _ = x.shape
    eye = lax.broadcasted_iota(jnp.int32, (h, c, c), 1) == lax.broadcasted_iota(jnp.int32, (h, c, c), 2)
    t = jnp.where(eye, 1.0, 0.0) + x
    p = x
    for _ in range(5):
        p = _raw_dot_3pass(p, p, BNN)
        t = t + _raw_dot_3pass(t, p, BNN)
    return t


@jax.custom_vjp
def _known_inverse(x, t):
    return t


def _known_inverse_fwd(x, t):
    return t, t


def _known_inverse_bwd(t, ct):
    return _raw_dot_3pass(_raw_dot_3pass(t, ct, BTN), t, BNT), jnp.zeros_like(t)


_known_inverse.defvjp(_known_inverse_fwd, _known_inverse_bwd)


def _delta_chunk(q, k, v, gcol, bcol, state, t_known=None):
    h, c, _ = q.shape
    row = lax.broadcasted_iota(jnp.int32, (h, c, c), 1)
    col = lax.broadcasted_iota(jnp.int32, (h, c, c), 2)
    incl, strict, eye = row >= col, row > col, row == col
    g_b = jnp.broadcast_to(gcol, (h, c, c))
    gc_row = jnp.sum(jnp.where(row <= col, g_b, 0.0), axis=1, keepdims=True)
    g_r = jnp.sum(jnp.where(eye, g_b, 0.0), axis=1, keepdims=True)
    gc_col = jnp.sum(jnp.where(incl, jnp.broadcast_to(g_r, (h, c, c)), 0.0), axis=2, keepdims=True)
    decay = jnp.exp(jnp.where(incl, gc_col - gc_row, NEG))
    kb = k * bcol
    vb = v * bcol
    x = -jnp.where(strict, _dot_bf16(kb, k, BNT) * decay, 0.0)
    t = _neumann_inverse(x) if t_known is None else _known_inverse(x, t_known)
    eg = jnp.exp(gc_col)
    u = _dot_3pass(t, vb)
    w = _dot_3pass(t, kb * eg)
    qk = _dot_bf16(q, k, BNT) * decay
    v_new = u - _dot_bf16(w, state)
    o = _dot_bf16(q * eg, state) + _dot_bf16(qk, v_new)
    g_last = jnp.sum(g_r, axis=2, keepdims=True)
    new_state = state * jnp.exp(g_last) + _dot_bf16(k * jnp.exp(g_last - gc_col), v_new, BTN)
    return o, new_state, t


def _lane_col(blk, idx):
    lane = lax.broadcasted_iota(jnp.int32, blk.shape, 1)
    return jnp.sum(jnp.where(lane == idx, blk, 0.0), axis=1, keepdims=True)


def _dn_heads(ref, base):
    return jnp.stack([ref[:, base + h * DN_DIM:base + (h + 1) * DN_DIM] for h in range(DN_HEADS)], axis=0).astype(F32)


def _dn_cols(gbv, base):
    return jnp.stack([_lane_col(gbv, base + h) for h in range(DN_HEADS)], axis=0)


def _delta_fwd(name, qkvn, gb):
    s = qkvn.shape[0]
    n = s // DN_CHUNK
    c = DN_CHUNK

    def body(qkv_ref, gb_ref, o_ref, st_ref, t_ref, state):
        @pl.when(pl.program_id(0) == 0)
        def _():
            state[...] = jnp.zeros(state.shape, F32)

        gbv = gb_ref[...]
        st = state[...]
        st_ref[0] = st
        o, new, t = _delta_chunk(_dn_heads(qkv_ref, 0), _dn_heads(qkv_ref, DN_WIDTH), _dn_heads(qkv_ref, 2 * DN_WIDTH),
                                 _dn_cols(gbv, DN_HEADS), _dn_cols(gbv, 0), st)
        for h in range(DN_HEADS):
            o_ref[:, h * DN_DIM:(h + 1) * DN_DIM] = o[h]
        t_ref[0] = t
        state[...] = new

    blk = _nbytes((c, 3 * DN_WIDTH), BF16) + _nbytes((c, LANES), F32) + _nbytes((c, DN_WIDTH), F32)
    blk += _nbytes((DN_HEADS, DN_DIM, DN_DIM), F32) + _nbytes((DN_HEADS, c, c), F32)
    return pl.pallas_call(
        body, name=name, grid=(n,),
        in_specs=[pl.BlockSpec((c, 3 * DN_WIDTH), lambda i: (i, 0)), pl.BlockSpec((c, LANES), lambda i: (i, 0))],
        out_specs=[pl.BlockSpec((c, DN_WIDTH), lambda i: (i, 0)),
                   pl.BlockSpec((1, DN_HEADS, DN_DIM, DN_DIM), lambda i: (i, 0, 0, 0)),
                   pl.BlockSpec((1, DN_HEADS, c, c), lambda i: (i, 0, 0, 0))],
        out_shape=[jax.ShapeDtypeStruct((s, DN_WIDTH), F32),
                   jax.ShapeDtypeStruct((n, DN_HEADS, DN_DIM, DN_DIM), F32),
                   jax.ShapeDtypeStruct((n, DN_HEADS, c, c), F32)],
        scratch_shapes=[pltpu.VMEM((DN_HEADS, DN_DIM, DN_DIM), F32)],
        compiler_params=_params(blk, 8 << 20),
    )(qkvn, gb)


def _delta_bwd(name, qkvn, gb, states, tinv, d_o):
    s = qkvn.shape[0]
    n = s // DN_CHUNK
    c = DN_CHUNK

    def body(qkv_ref, gb_ref, st_ref, t_ref, do_ref, dqkv_ref, dgb_ref, dstate):
        @pl.when(pl.program_id(0) == 0)
        def _():
            dstate[...] = jnp.zeros(dstate.shape, F32)

        gbv = gb_ref[...]
        lane = lax.broadcasted_iota(jnp.int32, (c, LANES), 1)
        t_known = t_ref[0]
        chunk = lambda *args: _delta_chunk(*args, t_known=t_known)[:2]
        _, vjp = jax.vjp(chunk, _dn_heads(qkv_ref, 0), _dn_heads(qkv_ref, DN_WIDTH),
                         _dn_heads(qkv_ref, 2 * DN_WIDTH), _dn_cols(gbv, DN_HEADS), _dn_cols(gbv, 0), st_ref[0])
        dq, dk, dv, dg, db, dst = vjp((_dn_heads(do_ref, 0), dstate[...]))
        dgb = jnp.zeros((c, LANES), F32)
        for h in range(DN_HEADS):
            dqkv_ref[:, h * DN_DIM:(h + 1) * DN_DIM] = dq[h]
            dqkv_ref[:, DN_WIDTH + h * DN_DIM:DN_WIDTH + (h + 1) * DN_DIM] = dk[h]
            dqkv_ref[:, 2 * DN_WIDTH + h * DN_DIM:2 * DN_WIDTH + (h + 1) * DN_DIM] = dv[h]
            dgb = dgb + jnp.where(lane == h, db[h], 0.0) + jnp.where(lane == DN_HEADS + h, dg[h], 0.0)
        dstate[...] = dst
        dgb_ref[...] = dgb

    rev = lambda i: (n - 1 - i, 0)
    blk = _nbytes((c, 3 * DN_WIDTH), BF16) + 2 * _nbytes((c, LANES), F32) + _nbytes((c, DN_WIDTH), F32)
    blk += _nbytes((DN_HEADS, DN_DIM, DN_DIM), F32) + _nbytes((c, 3 * DN_WIDTH), F32)
    return pl.pallas_call(
        body, name=name, grid=(n,),
        in_specs=[pl.BlockSpec((c, 3 * DN_WIDTH), rev), pl.BlockSpec((c, LANES), rev),
                  pl.BlockSpec((1, DN_HEADS, DN_DIM, DN_DIM), lambda i: (n - 1 - i, 0, 0, 0)),
                  pl.BlockSpec((1, DN_HEADS, c, c), lambda i: (n - 1 - i, 0, 0, 0)),
                  pl.BlockSpec((c, DN_WIDTH), rev)],
        out_specs=[pl.BlockSpec((c, 3 * DN_WIDTH), rev), pl.BlockSpec((c, LANES), rev)],
        out_shape=[jax.ShapeDtypeStruct((s, 3 * DN_WIDTH), F32), jax.ShapeDtypeStruct((s, LANES), F32)],
        scratch_shapes=[pltpu.VMEM((DN_HEADS, DN_DIM, DN_DIM), F32)],
        compiler_params=_params(blk, 16 << 20),
    )(qkvn, gb, states, tinv, d_o)


def _da_scores(q2f, k2, sub, valid, distf, head):
    lane = lax.broadcasted_iota(jnp.int32, q2f.shape, 1)
    hmask = (lane < DA_DIM) if sub == 0 else (lane >= DA_DIM)
    qm = jnp.where(hmask, q2f, 0.0).astype(BF16)
    slope = 2.0 ** (-ALIBI_MAX_EXP * (head + 1) / DA_HEADS)
    sc = lax.dot_general(qm, k2, NT, preferred_element_type=F32) * (DA_DIM ** -0.5)
    return jnp.where(valid, sc - slope * distf, NEG), qm, hmask


def _da_mask(i, r):
    qi = lax.broadcasted_iota(jnp.int32, (DA_BLOCK, 2 * DA_BLOCK), 0)
    ki = lax.broadcasted_iota(jnp.int32, (DA_BLOCK, 2 * DA_BLOCK), 1)
    dist = qi + DA_BLOCK - ki
    valid = (dist >= 0) & (dist <= DA_BLOCK) & ((ki >= DA_BLOCK) | (i > 0))
    return valid, (dist * r).astype(F32)


def _da_fwd(name, pda, r):
    s = pda.shape[0]
    n = s // r
    nb = n // DA_BLOCK
    w = DA_WIDTH
    dav = pda.reshape(n, r * 3 * w)

    def body(q_ref, kc_ref, kp_ref, vc_ref, vp_ref, o_ref, lse_ref):
        i = pl.program_id(1)
        valid, distf = _da_mask(i, r)
        lane = lax.broadcasted_iota(jnp.int32, (DA_BLOCK, LANES), 1)
        lse = jnp.zeros((DA_BLOCK, LANES), F32)
        for hp in range(DA_HEADS // 2):
            sl = slice(hp * LANES, (hp + 1) * LANES)
            q2f = q_ref[:, sl].astype(F32)
            k2 = jnp.concatenate([kp_ref[:, sl], kc_ref[:, sl]], axis=0)
            v2 = jnp.concatenate([vp_ref[:, sl], vc_ref[:, sl]], axis=0)
            o2 = None
            for sub in range(2):
                head = 2 * hp + sub
                sc, _, hmask = _da_scores(q2f, k2, sub, valid, distf, head)
                mx = jnp.max(sc, axis=1, keepdims=True)
                p = jnp.exp(sc - mx)
                l = jnp.sum(p, axis=1, keepdims=True)
                pv = lax.dot_general(p.astype(BF16), v2, NN, preferred_element_type=F32) / l
                o2 = pv if sub == 0 else jnp.where(hmask, pv, o2)
                lse = jnp.where(lane == head, mx + jnp.log(l), lse)
            o_ref[:, sl] = o2.astype(o_ref.dtype)
        lse_ref[...] = lse

    prev = lambda col: (lambda p, i: (jnp.maximum(i - 1, 0), 3 * p + col))
    cur = lambda col: (lambda p, i: (i, 3 * p + col))
    blk = 5 * _nbytes((DA_BLOCK, w), BF16) + _nbytes((DA_BLOCK, w), F32) + _nbytes((DA_BLOCK, LANES), F32)
    o, lse = pl.pallas_call(
        body, name=name, grid=(r, nb),
        in_specs=[pl.BlockSpec((DA_BLOCK, w), cur(0)), pl.BlockSpec((DA_BLOCK, w), cur(1)),
                  pl.BlockSpec((DA_BLOCK, w), prev(1)), pl.BlockSpec((DA_BLOCK, w), cur(2)),
                  pl.BlockSpec((DA_BLOCK, w), prev(2))],
        out_specs=[pl.BlockSpec((DA_BLOCK, w), lambda p, i: (i, p)),
                   pl.BlockSpec((DA_BLOCK, LANES), lambda p, i: (i, p))],
        out_shape=[jax.ShapeDtypeStruct((n, r * w), BF16), jax.ShapeDtypeStruct((n, r * LANES), F32)],
        compiler_params=_params(blk, 8 << 20),
    )(dav, dav, dav, dav, dav)
    return o.reshape(s, w), lse.reshape(s, LANES)


def _da_bwd(name, pda, d_ob, lse_tot, delta, r):
    s = pda.shape[0]
    n = s // r
    nb = n // DA_BLOCK
    w = DA_WIDTH
    dav = pda.reshape(n, r * 3 * w)
    dov = d_ob.reshape(n, r * w)
    lv = lse_tot.reshape(n, r * LANES)
    dlv = delta.reshape(n, r * LANES)

    def body(q_ref, kc_ref, kp_ref, vc_ref, vp_ref, do_ref, l_ref, dl_ref, dq_ref, dk_ref, dv_ref, ck, cv):
        i = pl.program_id(1)

        @pl.when(i == 0)
        def _():
            ck[...] = jnp.zeros(ck.shape, F32)
            cv[...] = jnp.zeros(cv.shape, F32)

        @pl.when(i < nb)
        def _():
            valid, distf = _da_mask(i, r)
            lsev = l_ref[...]
            dlt = dl_ref[...]
            for hp in range(DA_HEADS // 2):
                sl = slice(hp * LANES, (hp + 1) * LANES)
                q2f = q_ref[:, sl].astype(F32)
                k2 = jnp.concatenate([kp_ref[:, sl], kc_ref[:, sl]], axis=0)
                v2 = jnp.concatenate([vp_ref[:, sl], vc_ref[:, sl]], axis=0)
                do2f = do_ref[:, sl].astype(F32)
                dq2 = jnp.zeros((DA_BLOCK, LANES), F32)
                dk2 = jnp.zeros((2 * DA_BLOCK, LANES), F32)
                dv2 = jnp.zeros((2 * DA_BLOCK, LANES), F32)
                for sub in range(2):
                    head = 2 * hp + sub
                    sc, qm, hmask = _da_scores(q2f, k2, sub, valid, distf, head)
                    p = jnp.exp(sc - _lane_col(lsev, head))
                    dom = jnp.where(hmask, do2f, 0.0).astype(BF16)
                    dp = lax.dot_general(dom, v2, NT, preferred_element_type=F32)
                    ds = (p * (dp - _lane_col(dlt, head)) * (DA_DIM ** -0.5)).astype(BF16)
                    dq2 = dq2 + jnp.where(hmask, lax.dot_general(ds, k2, NN, preferred_element_type=F32), 0.0)
                    dk2 = dk2 + lax.dot_general(ds, qm, TN, preferred_element_type=F32)
                    dv2 = dv2 + lax.dot_general(p.astype(BF16), dom, TN, preferred_element_type=F32)
                dq_ref[:, sl] = dq2.astype(dq_ref.dtype)
                dk_ref[:, sl] = (ck[:, sl] + dk2[:DA_BLOCK]).astype(dk_ref.dtype)
                dv_ref[:, sl] = (cv[:, sl] + dv2[:DA_BLOCK]).astype(dv_ref.dtype)
                ck[:, sl] = dk2[DA_BLOCK:]
                cv[:, sl] = dv2[DA_BLOCK:]

        @pl.when(i == nb)
        def _():
            dk_ref[...] = ck[...].astype(dk_ref.dtype)
            dv_ref[...] = cv[...].astype(dv_ref.dtype)

    qrow = lambda i: jnp.minimum(i, nb - 1)
    prev = lambda col: (lambda p, i: (jnp.maximum(qrow(i) - 1, 0), 3 * p + col))
    cur = lambda col: (lambda p, i: (qrow(i), 3 * p + col))
    same = lambda p, i: (qrow(i), p)
    late = lambda p, i: (jnp.maximum(i - 1, 0), p)
    blk = 6 * _nbytes((DA_BLOCK, w), BF16) + 2 * _nbytes((DA_BLOCK, LANES), F32) + 3 * _nbytes((DA_BLOCK, w), F32)
    dq, dk, dv = pl.pallas_call(
        body, name=name, grid=(r, nb + 1),
        in_specs=[pl.BlockSpec((DA_BLOCK, w), cur(0)), pl.BlockSpec((DA_BLOCK, w), cur(1)),
                  pl.BlockSpec((DA_BLOCK, w), prev(1)), pl.BlockSpec((DA_BLOCK, w), cur(2)),
                  pl.BlockSpec((DA_BLOCK, w), prev(2)), pl.BlockSpec((DA_BLOCK, w), same),
                  pl.BlockSpec((DA_BLOCK, LANES), same), pl.BlockSpec((DA_BLOCK, LANES), same)],
        out_specs=[pl.BlockSpec((DA_BLOCK, w), same), pl.BlockSpec((DA_BLOCK, w), late),
                   pl.BlockSpec((DA_BLOCK, w), late)],
        out_shape=[jax.ShapeDtypeStruct((n, r * w), BF16)] * 3,
        scratch_shapes=[pltpu.VMEM((DA_BLOCK, w), F32), pltpu.VMEM((DA_BLOCK, w), F32)],
        compiler_params=_params(blk, 12 << 20),
    )(dav, dav, dav, dav, dav, dov, lv, dlv)
    return dq.reshape(s, w), dk.reshape(s, w), dv.reshape(s, w)


def _head_expand():
    hrow = lax.broadcasted_iota(jnp.int32, (LANES, DA_WIDTH), 0)
    lcol = lax.broadcasted_iota(jnp.int32, (LANES, DA_WIDTH), 1)
    return jnp.where(lcol // DA_DIM == hrow, 1.0, 0.0).astype(F32)


def _ffn_up(name, a, ga, tg, tu):
    s, d = a.shape
    nsh, _, _, ffs = ga.shape
    bm = _pick(s, (1024, 512, 256, 128))

    def body(a_ref, wg_ref, wu_ref, g_ref, u_ref, f_ref):
        av = a_ref[...]
        g = lax.dot_general(av, wg_ref[...], NN, preferred_element_type=F32)
        u = lax.dot_general(av, wu_ref[...], NN, preferred_element_type=F32)
        g_ref[...] = g.astype(BF16)
        u_ref[...] = u.astype(BF16)
        f_ref[...] = (_silu(g) * u).astype(BF16)

    wspec = lambda t: pl.BlockSpec((None, None, d, ffs), lambda i, j: (j, t, 0, 0))
    ospec = pl.BlockSpec((None, bm, ffs), lambda i, j: (j, i, 0))
    blk = _nbytes((bm, d), BF16) + 2 * _nbytes((d, ffs), BF16) + 3 * _nbytes((bm, ffs), BF16)
    return pl.pallas_call(
        body, name=name, grid=(s // bm, nsh),
        in_specs=[pl.BlockSpec((bm, d), lambda i, j: (i, 0)), wspec(tg), wspec(tu)],
        out_specs=[ospec] * 3, out_shape=[jax.ShapeDtypeStruct((nsh, s, ffs), BF16)] * 3,
        compiler_params=_params(blk, 4 * _nbytes((bm, ffs), F32)),
    )(a, ga, ga)


def _ffn_fwd(tag, h_in, ln, sh, sc, gt, ga, tg, tu, gb, td, weight):
    s, d = h_in.shape
    nsh, _, ffs, _ = gb.shape
    (a,) = _rowwise(tag + "_norm", lambda h, l, s1, s2: ((_rms_mod(h, l, s1, s2),), ()), [h_in], [ln, sh, sc],
                    [(d, BF16)])
    g, u, f = _ffn_up(tag + "_up", a, ga, tg, tu)
    bm, bn = _pick(s, (1024, 512, 256, 128)), _pick(d, (512, 256, 128))
    io = pl.BlockSpec((bm, bn), lambda i, j, kk: (i, j))
    h_out, o = _mm_core(
        tag + "_down", (s // bm, d // bn, nsh), nsh,
        [(f, pl.BlockSpec((None, bm, ffs), lambda i, j, kk: (kk, i, 0)),
          gb, pl.BlockSpec((None, None, ffs, bn), lambda i, j, kk: (kk, td, 0, j)), NN)],
        [((s, d), F32, io), ((s, d), BF16, io)], (bm, bn),
        epi=lambda acc, h, gv: (h + weight * gv * acc, acc),
        epi_ins=[(h_in, io), (gt, pl.BlockSpec((1, bn), lambda i, j, kk: (0, j)))])
    return h_out, dict(a=a, g=g, u=u, f=f, o=o)


def _resid_bwd(tag, dh_out, o, gt, weight):
    d = dh_out.shape[1]

    def fn(dh, ov, g):
        return (weight * g * dh,), (jnp.sum(weight * dh * ov.astype(F32), axis=0, keepdims=True),)

    do, d_gt = _rowwise(tag + "_resid_bwd", fn, [dh_out, o], [gt], [(d, BF16)], [(1, d)])
    return do, d_gt


def _norm_bwd(tag, h_in, da, dh_out, ln, sh, sc):
    d = h_in.shape[1]

    def fn(h, dav, dh, l, s1, s2):
        _, vjp = jax.vjp(_rms_mod, h, l, s1, s2)
        gh, gl, gs1, gs2 = vjp(dav)
        return (dh + gh,), (gl, gs1, gs2)

    return _rowwise(tag + "_norm_bwd", fn, [h_in, da, dh_out], [ln, sh, sc], [(d, F32)], [(1, d)] * 3)


def _ffn_bwd(tag, h_in, dh_out, sv, ln, sh, sc, gt, ga, tg, tu, gb, td, weight):
    s, d = h_in.shape
    nsh, _, ffs, _ = gb.shape
    bm, bn = _pick(s, (1024, 512, 256, 128)), _pick(d, (512, 256, 128))
    bk = _pick(s, (1024, 512, 256, 128))
    do, d_gt = _resid_bwd(tag, dh_out, sv["o"], gt, weight)

    def act_bwd(df, g, u):
        _, vjp = jax.vjp(_swiglu_act, g, u)
        return vjp(df)

    hid = pl.BlockSpec((None, bm, ffs), lambda i, j, kk: (j, i, 0))
    dg, du = _mm_core(
        tag + "_down_dx", (s // bm, nsh, 1), 1,
        [(do, pl.BlockSpec((bm, d), lambda i, j, kk: (i, 0)),
          gb, pl.BlockSpec((None, None, ffs, d), lambda i, j, kk: (j, td, 0, 0)), NT)],
        [((nsh, s, ffs), BF16, hid)] * 2, (bm, ffs), epi=act_bwd, epi_ins=[(sv["g"], hid), (sv["u"], hid)])
    (d_wd,) = _mm_core(
        tag + "_down_dw", (nsh, d // bn, s // bk), s // bk,
        [(sv["f"], pl.BlockSpec((None, bk, ffs), lambda i, j, kk: (i, kk, 0)),
          do, pl.BlockSpec((bk, bn), lambda i, j, kk: (kk, j)), TN)],
        [((nsh, ffs, d), BF16, pl.BlockSpec((None, ffs, bn), lambda i, j, kk: (i, 0, j)))], (ffs, bn))
    kmaj = pl.BlockSpec((None, bm, ffs), lambda i, j, kk: (kk, i, 0))
    wsp = lambda t: pl.BlockSpec((None, None, bn, ffs), functools.partial(lambda i, j, kk, t: (kk, t, j, 0), t=t))
    (da,) = _mm_core(
        tag + "_up_dx", (s // bm, d // bn, nsh), nsh, [(dg, kmaj, ga, wsp(tg), NT), (du, kmaj, ga, wsp(tu), NT)],
        [((s, d), F32, pl.BlockSpec((bm, bn), lambda i, j, kk: (i, j)))], (bm, bn))
    dws = []
    for nm, dh in (("_wg_dw", dg), ("_wu_dw", du)):
        (dw,) = _mm_core(
            tag + nm, (1, nsh, s // bk), s // bk,
            [(sv["a"], pl.BlockSpec((bk, d), lambda i, j, kk: (kk, 0)),
              dh, pl.BlockSpec((None, bk, ffs), lambda i, j, kk: (j, kk, 0)), TN)],
            [((nsh, d, ffs), BF16, pl.BlockSpec((None, d, ffs), lambda i, j, kk: (j, 0, 0)))], (d, ffs))
        dws.append(dw)
    dh_in, d_ln, d_sh, d_sc = _norm_bwd(tag, h_in, da, dh_out, ln, sh, sc)
    return dh_in, dict(wg=dws[0], wu=dws[1], wd=d_wd), dict(ln=d_ln, sh=d_sh, sc=d_sc, gt=d_gt)


def _mixer_fwd(tag, h_in, ln, sh, sc, gt, w, sp):
    d = h_in.shape[1]
    (a,) = _rowwise(tag + "_norm", lambda h, l, s1, s2: ((_rms_mod(h, l, s1, s2),), ()), [h_in], [ln, sh, sc],
                    [(d, BF16)])
    (pq,) = _matmul(tag + "_pq", a, w["wq"], outs=(BF16,))
    (pz,) = _matmul(tag + "_pz", a, w["wz"], outs=(BF16,))
    (pba,) = _matmul(tag + "_pba", a, w["wba"])
    (pda,) = _matmul(tag + "_pda", a, w["wda"], outs=(BF16,))
    (pg,) = _matmul(tag + "_pg", a, w["wg"], outs=(BF16,))
    yc, qkvn, gb = _conv_prep_fwd(tag + "_conv", pq, sp["conv8"], pba, sp["alog"], sp["dtb"])
    o_a, states, tinv = _delta_fwd(tag + "_delta", qkvn, gb)
    (o_an,) = _rowwise(tag + "_dnorm", lambda o, z, dn: ((_dn_outnorm(o, z.astype(F32), dn),), ()), [o_a, pz],
                       [sp["dn"]], [(DN_WIDTH, BF16)])
    ops, lses = [], []
    for (_, r) in DA_PATTERNS:
        o_p, lse_p = _da_fwd(f"{tag}_da{r}", pda, r)
        ops.append(o_p)
        lses.append(lse_p)

    def merge(o1, o2, o3, l1, l2, l3):
        mx = jnp.maximum(jnp.maximum(l1, l2), l3)
        e1, e2, e3 = jnp.exp(l1 - mx), jnp.exp(l2 - mx), jnp.exp(l3 - mx)
        tot = e1 + e2 + e3
        ex = _head_expand()
        up = lambda wgt: lax.dot_general(wgt / tot, ex, NN, precision=HI, preferred_element_type=F32)
        return (up(e1) * o1 + up(e2) * o2 + up(e3) * o3, mx + jnp.log(tot)), ()

    o_b, lse_tot = _rowwise(tag + "_merge", merge, ops + lses, [], [(DA_WIDTH, BF16), (LANES, F32)])
    (y_a,) = _matmul(tag + "_wa", o_an, w["w_a"], outs=(BF16,))
    (y_b,) = _matmul(tag + "_wb", o_b, w["w_b"], outs=(BF16,))

    def gate(ga, gbv, ya, yb):
        return _sigmoid(ga.astype(F32)) * ya.astype(F32) + _sigmoid(gbv.astype(F32)) * yb.astype(F32)

    (merged,) = _rowwise(tag + "_gate", lambda *v: ((gate(*v),), ()), [(pg, d, 0), (pg, d, 1), y_a, y_b], [],
                         [(d, BF16)])
    h_out, m = _matmul(tag + "_wo", merged, w["w_o"], outs=(F32, BF16), epi_rows=[h_in], epi_bcast=[gt],
                       epi=lambda acc, h, g: (h + g * acc, acc))
    sv = dict(a=a, pq=pq, pz=pz, pba=pba, pda=pda, pg=pg, yc=yc, qkvn=qkvn, gb=gb, o_a=o_a, states=states, tinv=tinv,
              o_an=o_an, o_b=o_b, lse=lse_tot, y_a=y_a, y_b=y_b, merged=merged, m=m, gate=gate)
    return h_out, sv


def _mixer_bwd(tag, h_in, dh_out, sv, ln, sh, sc, gt, w, sp):
    d = h_in.shape[1]
    dm, d_gt = _resid_bwd(tag, dh_out, sv["m"], gt, 1.0)
    (d_merged,) = _matmul(tag + "_wo_dx", dm, w["w_o"], tb=True, outs=(BF16,))
    (d_wo,) = _matmul(tag + "_wo_dw", sv["merged"], dm, ta=True, outs=(BF16,))
    gate = sv["gate"]

    def gate_bwd(dmg, ga, gbv, ya, yb):
        _, vjp = jax.vjp(gate, ga.astype(F32), gbv.astype(F32), ya.astype(F32), yb.astype(F32))
        dga, dgb, dya, dyb = vjp(dmg.astype(F32))
        return (jnp.concatenate([dga, dgb], axis=1), dya, dyb), ()

    pg = sv["pg"]
    d_pg, d_ya, d_yb = _rowwise(tag + "_gate_bwd", gate_bwd, [d_merged, (pg, d, 0), (pg, d, 1), sv["y_a"], sv["y_b"]],
                                [], [(2 * d, BF16), (d, BF16), (d, BF16)])
    (d_oan,) = _matmul(tag + "_wa_dx", d_ya, w["w_a"], tb=True)
    (d_wa,) = _matmul(tag + "_wa_dw", sv["o_an"], d_ya, ta=True, outs=(BF16,))
    (d_ob,) = _matmul(tag + "_wb_dx", d_yb, w["w_b"], tb=True, outs=(BF16,))
    (d_wb,) = _matmul(tag + "_wb_dw", sv["o_b"], d_yb, ta=True, outs=(BF16,))

    def dnorm_bwd(doan, o, z, dn):
        _, vjp = jax.vjp(_dn_outnorm, o, z.astype(F32), dn)
        go, gz, gdn = vjp(doan)
        return (go, gz), (gdn,)

    d_oa, d_pz, d_dn = _rowwise(tag + "_dnorm_bwd", dnorm_bwd, [d_oan, sv["o_a"], sv["pz"]], [sp["dn"]],
                                [(DN_WIDTH, F32), (DN_WIDTH, BF16)], [(1, DN_DIM)])
    d_qkvn, d_gb = _delta_bwd(tag + "_delta_bwd", sv["qkvn"], sv["gb"], sv["states"], sv["tinv"], d_oa)

    def prep_bwd(dq, dgbv, yc, pba, alog, dtb):
        _, vjp = jax.vjp(_dn_prep, yc.astype(F32), pba, alog, dtb)
        gyc, gpba, galog, gdtb = vjp((dq, dgbv))
        return (gyc, gpba), (galog, gdtb)

    d_yc, d_pba, d_alog, d_dtb = _rowwise(tag + "_prep_bwd", prep_bwd, [d_qkvn, d_gb, sv["yc"], sv["pba"]],
                                          [sp["alog"], sp["dtb"]], [(3 * DN_WIDTH, BF16), (LANES, BF16)],
                                          [(1, LANES), (1, LANES)], bm=128)
    d_pq, d_conv = _conv_bwd(tag + "_conv_bwd", d_yc, sv["pq"], sp["conv8"])

    def delta_fn(dob, ob):
        prod = dob.astype(F32) * ob.astype(F32)
        return (lax.dot_general(prod, _head_expand(), NT, precision=HI, preferred_element_type=F32),), ()

    (delta,) = _rowwise(tag + "_da_delta", delta_fn, [d_ob, sv["o_b"]], [], [(LANES, F32)])
    grads = [_da_bwd(f"{tag}_da{r}_bwd", sv["pda"], d_ob, sv["lse"], delta, r) for (_, r) in DA_PATTERNS]

    def sum3(*parts):
        q1, k1, v1, q2, k2, v2, q3, k3, v3 = (p.astype(F32) for p in parts)
        return (jnp.concatenate([q1 + q2 + q3, k1 + k2 + k3, v1 + v2 + v3], axis=1),), ()

    (d_pda,) = _rowwise(tag + "_da_sum", sum3, [t for g in grads for t in g], [], [(3 * DA_WIDTH, BF16)])

    a = sv["a"]
    (da,) = _matmul(tag + "_pq_dx", d_pq, w["wq"], tb=True)
    add = lambda acc, prev: (acc + prev,)
    (da,) = _matmul(tag + "_pz_dx", d_pz, w["wz"], tb=True, epi_rows=[da], epi=add)
    (da,) = _matmul(tag + "_pba_dx", d_pba, w["wba"], tb=True, epi_rows=[da], epi=add)
    (da,) = _matmul(tag + "_pda_dx", d_pda, w["wda"], tb=True, epi_rows=[da], epi=add)
    (da,) = _matmul(tag + "_pg_dx", d_pg, w["wg"], tb=True, epi_rows=[da], epi=add)
    (d_wq,) = _matmul(tag + "_pq_dw", a, d_pq, ta=True, outs=(BF16,))
    (d_wz,) = _matmul(tag + "_pz_dw", a, d_pz, ta=True, outs=(BF16,))
    (d_wba,) = _matmul(tag + "_pba_dw", a, d_pba, ta=True, outs=(BF16,))
    (d_wda,) = _matmul(tag + "_pda_dw", a, d_pda, ta=True, outs=(BF16,))
    (d_wg,) = _matmul(tag + "_pg_dw", a, d_pg, ta=True, outs=(BF16,))
    dh_in, d_ln, d_sh, d_sc = _norm_bwd(tag, h_in, da, dh_out, ln, sh, sc)
    wgrads = dict(wq=d_wq, wz=d_wz, wba=d_wba, wda=d_wda, wg=d_wg, w_a=d_wa, w_b=d_wb, w_o=d_wo)
    small = dict(ln=d_ln, sh=d_sh, sc=d_sc, gt=d_gt, dn=d_dn, alog=d_alog, dtb=d_dtb, conv=d_conv)
    return dh_in, wgrads, small


def _loss_head(h, target, fnorm):
    d = h.shape[1]

    def fn(hv, tv, fw):
        def lossf(hh, ww):
            y = hh * lax.rsqrt(jnp.mean(hh * hh, axis=-1, keepdims=True) + NORM_EPS) * ww
            return 0.5 * jnp.sum(jnp.mean(jnp.square(y - tv), axis=-1))

        val, (dh, dw) = jax.value_and_grad(lossf, argnums=(0, 1))(hv, fw)
        return (dh,), (jnp.full((1, LANES), val, F32), dw)

    return _rowwise("loss_head", fn, [h, target], [fnorm], [(d, F32)], [(1, LANES), (1, d)])


def _row(v):
    return v.reshape(1, -1)


def _pad_lanes(v, offset):
    return jnp.pad(v.reshape(1, -1), ((0, 0), (offset, LANES - offset - v.shape[0])))


_UP_SLOTS = dict(ffn1_wg=0, ffn1_wu=1, ffn2_wg=2, ffn2_wu=3)
_DOWN_SLOTS = dict(ffn1_wd=0, ffn2_wd=1)


def _local_step(x2, target, mod, layer_weights, small, on_layer_grads):
    depth = mod.shape[0]
    d = x2.shape[1]
    h = x2
    saved = []
    mods = []
    up = lambda l, nm: _UP_SLOTS[nm]
    down = lambda l, nm: _DOWN_SLOTS[nm]
    for l in range(depth):
        m9 = [_row(mod[l, i * d:(i + 1) * d]) for i in range(N_ADA)]
        sp = dict(conv8=jnp.pad(small["conv_w"][l], ((0, 8 - DN_CONV), (0, 0))),
                  alog=_pad_lanes(small["a_log"][l], DN_HEADS), dtb=_pad_lanes(small["dt_bias"][l], DN_HEADS),
                  dn=_row(small["dn_norm"][l]))
        ga, gb, w = layer_weights(l, h)
        h0 = h
        h1, sv1 = _ffn_fwd(f"l{l}_ffn1", h0, _row(small["ln_ffn1"][l]), m9[0], m9[1], m9[2], ga, up(l, "ffn1_wg"),
                           up(l, "ffn1_wu"), gb, down(l, "ffn1_wd"), 0.5)
        h2, sv2 = _mixer_fwd(f"l{l}_mix", h1, _row(small["ln_mix"][l]), m9[3], m9[4], m9[5], w, sp)
        h3, sv3 = _ffn_fwd(f"l{l}_ffn2", h2, _row(small["ln_ffn2"][l]), m9[6], m9[7], m9[8], ga, up(l, "ffn2_wg"),
                           up(l, "ffn2_wu"), gb, down(l, "ffn2_wd"), 0.5)
        saved.append((h0, h1, h2, sv1, sv2, sv3, sp, ga, gb, w))
        mods.append(m9)
        h = h3
    dh, loss_part, d_fnorm = _loss_head(h, target, _row(small["final_norm"]))
    sgrads, dmods = [], []
    token = None
    for l in reversed(range(depth)):
        h0, h1, h2, sv1, sv2, sv3, sp, ga, gb, w = saved[l]
        m9 = mods[l] if token is None else [r + token for r in mods[l]]
        dh, g3, s3 = _ffn_bwd(f"l{l}_ffn2", h2, dh, sv3, _row(small["ln_ffn2"][l]), m9[6], m9[7], m9[8], ga,
                              up(l, "ffn2_wg"), up(l, "ffn2_wu"), gb, down(l, "ffn2_wd"), 0.5)
        dh, g2, s2 = _mixer_bwd(f"l{l}_mix", h1, dh, sv2, _row(small["ln_mix"][l]), m9[3], m9[4], m9[5], w, sp)
        dh, g1, s1 = _ffn_bwd(f"l{l}_ffn1", h0, dh, sv1, _row(small["ln_ffn1"][l]), m9[0], m9[1], m9[2], ga,
                              up(l, "ffn1_wg"), up(l, "ffn1_wu"), gb, down(l, "ffn1_wd"), 0.5)
        token = on_layer_grads(l, dict(ffn1_wg=g1["wg"], ffn1_wu=g1["wu"], ffn1_wd=g1["wd"], ffn2_wg=g3["wg"],
                                       ffn2_wu=g3["wu"], ffn2_wd=g3["wd"], **g2))
        dmods.append(jnp.concatenate([s1["sh"], s1["sc"], s1["gt"], s2["sh"], s2["sc"], s2["gt"],
                                      s3["sh"], s3["sc"], s3["gt"]], axis=1))
        sgrads.append(dict(ln_ffn1=s1["ln"][0], ln_mix=s2["ln"][0], ln_ffn2=s3["ln"][0],
                           a_log=s2["alog"][0, DN_HEADS:2 * DN_HEADS], dt_bias=s2["dtb"][0, DN_HEADS:2 * DN_HEADS],
                           dn_norm=s2["dn"][0], conv_w=s2["conv"][:DN_CONV]))
    sgrads.reverse()
    dmods.reverse()
    return loss_part[0, 0], dh, jnp.concatenate(dmods, axis=0), sgrads, d_fnorm[0]


def _flip(v, bit):
    return 1 - v if bit else v


def _allgather8(name, x):
    r, c = x.shape

    def body(x_ref, out_ref, send_sems, recv_sems, local_sem):
        mx, my, mc = lax.axis_index("x"), lax.axis_index("y"), lax.axis_index("c")
        me = 4 * mx + 2 * my + mc
        mine = pltpu.make_async_copy(x_ref, out_ref.at[me], local_sem)
        mine.start()
        sends = []
        for k in range(1, 8):
            peer = (_flip(mx, k & 4), _flip(my, k & 2), _flip(mc, k & 1))
            cp = pltpu.make_async_remote_copy(src_ref=x_ref, dst_ref=out_ref.at[me], send_sem=send_sems.at[k - 1],
                                              recv_sem=recv_sems.at[k - 1], device_id=peer, device_id_type=MESH)
            cp.start()
            sends.append(cp)
        for k in range(1, 8):
            peer = (_flip(mx, k & 4), _flip(my, k & 2), _flip(mc, k & 1))
            src = 4 * peer[0] + 2 * peer[1] + peer[2]
            pltpu.make_async_remote_copy(src_ref=x_ref, dst_ref=out_ref.at[src], send_sem=send_sems.at[k - 1],
                                         recv_sem=recv_sems.at[k - 1], device_id=peer, device_id_type=MESH).wait_recv()
        for cp in sends:
            cp.wait_send()
        mine.wait()

    return pl.pallas_call(
        body, name=name, out_shape=jax.ShapeDtypeStruct((8, r, c), x.dtype),
        in_specs=[pl.BlockSpec(memory_space=pltpu.VMEM)], out_specs=pl.BlockSpec(memory_space=pltpu.VMEM),
        scratch_shapes=[pltpu.SemaphoreType.DMA((7,)), pltpu.SemaphoreType.DMA((7,)), pltpu.SemaphoreType.DMA],
        compiler_params=_params(9 * _nbytes((r, c), x.dtype)),
    )(x)


def _chip_peers(mx, my):
    chips = [(1 - mx, my), (mx, 1 - my), (1 - mx, 1 - my)]
    return chips, [2 * cx + cy for (cx, cy) in chips]


_ANY = pl.BlockSpec(memory_space=pl.ANY)


def _row_half(mc, r):
    return pl.ds(pl.multiple_of(mc * (r // 2), 16), r // 2)


def _gather_groups(name, shards):
    ng = len(shards)

    def body(*refs):
        xs, outs = refs[:ng], refs[ng:2 * ng]
        send_sems, recv_sems = refs[2 * ng:]
        mx, my, mc = lax.axis_index("x"), lax.axis_index("y"), lax.axis_index("c")
        j = 2 * mx + my
        chips, idxs = _chip_peers(mx, my)
        sib = (mx, my, 1 - mc)

        def copy(k, src, dst, to):
            return pltpu.make_async_remote_copy(src_ref=src, dst_ref=dst, send_sem=send_sems.at[k],
                                                recv_sem=recv_sems.at[k], device_id=to, device_id_type=MESH)

        first, passed = [], []
        for g in range(ng):
            mine = _row_half(mc, shards[g].shape[1])
            for t, chip in enumerate(chips):
                cp = copy(6 * g + t, xs[g].at[:, mine], outs[g].at[j, :, mine], (*chip, mc))
                cp.start()
                first.append(cp)
        for g in range(ng):
            mine = _row_half(mc, shards[g].shape[1])
            for t, chip in enumerate(chips):
                landed = outs[g].at[idxs[t], :, mine]
                copy(6 * g + t, landed, landed, (*chip, mc)).wait_recv()
                fwd = copy(6 * g + 3 + t, landed, landed, sib)
                fwd.start()
                passed.append(fwd)
        for g in range(ng):
            theirs_half = _row_half(1 - mc, shards[g].shape[1])
            for t in range(3):
                theirs = outs[g].at[idxs[t], :, theirs_half]
                copy(6 * g + 3 + t, theirs, theirs, sib).wait_recv()
        for cp in first + passed:
            cp.wait_send()

    outs = pl.pallas_call(
        body, name=name, out_shape=[jax.ShapeDtypeStruct((4,) + x.shape, x.dtype) for x in shards],
        in_specs=[_ANY] * ng, out_specs=[_ANY] * ng,
        scratch_shapes=[pltpu.SemaphoreType.DMA((6 * ng,)), pltpu.SemaphoreType.DMA((6 * ng,))],
    )(*shards)
    return _place_own_slab(outs, shards)


def _place_own_slab(outs, shards):
    chip = 2 * lax.axis_index("x") + lax.axis_index("y")
    return [lax.dynamic_update_slice(o, x[None], (chip,) + (0,) * x.ndim) for o, x in zip(outs, shards)]


_HBM = pl.BlockSpec(memory_space=pltpu.HBM)
_SEM = pl.BlockSpec(memory_space=pltpu.SEMAPHORE)
_DATAFLOW = pltpu.SideEffectType.DATAFLOW_SIDE_EFFECTING


def _ici_gather_copies(src_refs, land_refs, send_sems, recv_sems, scatter=False):
    mx, my, mc = lax.axis_index("x"), lax.axis_index("y"), lax.axis_index("c")
    j = 2 * mx + my
    chips, idxs = _chip_peers(mx, my)
    sends, recvs = [], []
    for g, src in enumerate(src_refs):
        for t, chip in enumerate(chips):
            common = dict(send_sem=send_sems.at[3 * g + t], recv_sem=recv_sems.at[3 * g + t], device_id=(*chip, mc),
                          device_id_type=MESH)
            if scatter:
                out, to, frm = src.at[idxs[t]], land_refs[g].at[j], land_refs[g].at[idxs[t]]
            else:
                mine = _row_half(mc, src.shape[1])
                out, to, frm = src.at[:, mine], land_refs[g].at[j, :, mine], land_refs[g].at[idxs[t], :, mine]
            sends.append(pltpu.make_async_remote_copy(src_ref=out, dst_ref=to, **common))
            recvs.append(pltpu.make_async_remote_copy(src_ref=out, dst_ref=frm, **common))
    return sends, recvs


def _gather_start(name, shards, scatter=False):
    ng = len(shards)

    def body(*refs):
        srcs, lands = refs[:ng], refs[ng:2 * ng]
        send_sems, recv_sems = refs[2 * ng], refs[2 * ng + 1]
        token = refs[-1]
        sends, _ = _ici_gather_copies(srcs, lands, send_sems, recv_sems, scatter)
        for cp in sends:
            cp.start()
        token[...] = jnp.zeros(token.shape, token.dtype)

    land_shape = lambda x: x.shape if scatter else (4,) + x.shape
    srcs = [pltpu.with_memory_space_constraint(x, pltpu.HBM) for x in shards]
    lands = [pltpu.with_memory_space_constraint(lax.empty(land_shape(x), x.dtype), pltpu.HBM) for x in shards]
    res = pl.pallas_call(
        body, name=name,
        out_shape=(pltpu.SemaphoreType.DMA((3 * ng,)), pltpu.SemaphoreType.DMA((3 * ng,)),
                   *[pltpu.HBM(x.shape, x.dtype) for x in srcs], *[pltpu.HBM(x.shape, x.dtype) for x in lands],
                   jax.ShapeDtypeStruct((8, LANES), F32)),
        in_specs=[_HBM] * (2 * ng),
        out_specs=(_SEM, _SEM, *[_HBM] * (2 * ng), pl.BlockSpec(memory_space=pltpu.VMEM)),
        input_output_aliases={i: 2 + i for i in range(2 * ng)},
        compiler_params=pltpu.CompilerParams(has_side_effects=_DATAFLOW),
    )(*srcs, *lands)
    return dict(send_sems=res[0], recv_sems=res[1], srcs=list(res[2:2 + ng]), lands=list(res[2 + ng:2 + 2 * ng]),
                token=res[-1])


def _gather_wait(name, started, after, scatter=False):
    ng = len(started["srcs"])

    def body(*refs):
        srcs, lands = refs[:ng], refs[ng:2 * ng]
        send_sems, recv_sems = refs[2 * ng], refs[2 * ng + 1]
        sends, recvs = _ici_gather_copies(srcs, lands, send_sems, recv_sems, scatter)
        for cp in sends:
            cp.wait_send()
        for cp in recvs:
            cp.wait_recv()

    res = pl.pallas_call(
        body, name=name,
        out_shape=[pltpu.HBM(x.shape, x.dtype) for x in started["srcs"] + started["lands"]],
        in_specs=[_HBM] * (2 * ng) + [_SEM, _SEM, _ANY], out_specs=[_HBM] * (2 * ng),
        input_output_aliases={i: i for i in range(2 * ng)},
        compiler_params=pltpu.CompilerParams(has_side_effects=_DATAFLOW),
    )(*started["srcs"], *started["lands"], started["send_sems"], started["recv_sems"], after)
    return list(res[:ng]), list(res[ng:])


def _pair_forward_groups(name, lands, shards):
    ng = len(lands)

    def body(*refs):
        ins, outs = refs[:ng], refs[ng:2 * ng]
        send_sems, recv_sems = refs[2 * ng:]
        mx, my, mc = lax.axis_index("x"), lax.axis_index("y"), lax.axis_index("c")
        _, idxs = _chip_peers(mx, my)
        sib = (mx, my, 1 - mc)
        cps = []
        for g in range(ng):
            mine = _row_half(mc, lands[g].shape[2])
            for t in range(3):
                cp = pltpu.make_async_remote_copy(src_ref=ins[g].at[idxs[t], :, mine], dst_ref=outs[g].at[idxs[t], :, mine],
                                                  send_sem=send_sems.at[3 * g + t], recv_sem=recv_sems.at[3 * g + t],
                                                  device_id=sib, device_id_type=MESH)
                cp.start()
                cps.append(cp)
        for g in range(ng):
            theirs = _row_half(1 - mc, lands[g].shape[2])
            for t in range(3):
                pltpu.make_async_remote_copy(src_ref=ins[g].at[idxs[t], :, theirs], dst_ref=outs[g].at[idxs[t], :, theirs],
                                             send_sem=send_sems.at[3 * g + t], recv_sem=recv_sems.at[3 * g + t],
                                             device_id=sib, device_id_type=MESH).wait_recv()
        for cp in cps:
            cp.wait_send()

    outs = pl.pallas_call(
        body, name=name, out_shape=[jax.ShapeDtypeStruct(x.shape, x.dtype) for x in lands],
        in_specs=[_ANY] * ng, out_specs=[_ANY] * ng, input_output_aliases={i: i for i in range(ng)},
        scratch_shapes=[pltpu.SemaphoreType.DMA((3 * ng,)), pltpu.SemaphoreType.DMA((3 * ng,))],
    )(*lands)
    return _place_own_slab(outs, shards)


def _pair_swap_groups(name, gs):
    ng = len(gs)

    def body(*refs):
        xs, outs = refs[:ng], refs[ng:2 * ng]
        send_sems, recv_sems = refs[2 * ng:]
        mx, my, mc = lax.axis_index("x"), lax.axis_index("y"), lax.axis_index("c")
        cps = []
        for g in range(ng):
            cp = pltpu.make_async_remote_copy(src_ref=xs[g].at[:, :, _row_half(1 - mc, gs[g].shape[2])], dst_ref=outs[g],
                                              send_sem=send_sems.at[g], recv_sem=recv_sems.at[g],
                                              device_id=(mx, my, 1 - mc), device_id_type=MESH)
            cp.start()
            cps.append(cp)
        for cp in cps:
            cp.wait()

    return pl.pallas_call(
        body, name=name,
        out_shape=[jax.ShapeDtypeStruct(x.shape[:2] + (x.shape[2] // 2, x.shape[3]), x.dtype) for x in gs],
        in_specs=[_ANY] * ng, out_specs=[_ANY] * ng,
        scratch_shapes=[pltpu.SemaphoreType.DMA((ng,)), pltpu.SemaphoreType.DMA((ng,))],
    )(*gs)


def _chip_scatter_groups(name, ps):
    ng = len(ps)

    def body(*refs):
        xs, outs = refs[:ng], refs[ng:2 * ng]
        send_sems, recv_sems = refs[2 * ng:]
        mx, my, mc = lax.axis_index("x"), lax.axis_index("y"), lax.axis_index("c")
        j = 2 * mx + my
        chips, idxs = _chip_peers(mx, my)
        sends = []
        for g in range(ng):
            for t, chip in enumerate(chips):
                cp = pltpu.make_async_remote_copy(src_ref=xs[g].at[idxs[t]], dst_ref=outs[g].at[j],
                                                  send_sem=send_sems.at[3 * g + t], recv_sem=recv_sems.at[3 * g + t],
                                                  device_id=(*chip, mc), device_id_type=MESH)
                cp.start()
                sends.append(cp)
        for g in range(ng):
            for t, chip in enumerate(chips):
                pltpu.make_async_remote_copy(src_ref=xs[g].at[idxs[t]], dst_ref=outs[g].at[idxs[t]],
                                             send_sem=send_sems.at[3 * g + t], recv_sem=recv_sems.at[3 * g + t],
                                             device_id=(*chip, mc), device_id_type=MESH).wait_recv()
        for cp in sends:
            cp.wait_send()

    outs = pl.pallas_call(
        body, name=name, out_shape=[jax.ShapeDtypeStruct(x.shape, x.dtype) for x in ps],
        in_specs=[_ANY] * ng, out_specs=[_ANY] * ng,
        scratch_shapes=[pltpu.SemaphoreType.DMA((3 * ng,)), pltpu.SemaphoreType.DMA((3 * ng,))],
    )(*ps)
    return _place_own_part(outs, ps)


def _place_own_part(outs, ps):
    chip = 2 * lax.axis_index("x") + lax.axis_index("y")
    return [lax.dynamic_update_slice(o, lax.dynamic_index_in_dim(x, chip, 0, keepdims=True), (chip,) + (0,) * (x.ndim - 1))
            for o, x in zip(outs, ps)]


def _pair_merge_groups(name, fs):
    ng = len(fs)

    def body(*refs):
        xs, outs = refs[:ng], refs[ng:2 * ng]
        send_sems, recv_sems = refs[2 * ng:]
        mx, my, mc = lax.axis_index("x"), lax.axis_index("y"), lax.axis_index("c")
        cps = []
        for g in range(ng):
            mine = _row_half(mc, 2 * fs[g].shape[1])
            cp = pltpu.make_async_remote_copy(src_ref=xs[g], dst_ref=outs[g].at[:, mine], send_sem=send_sems.at[g],
                                              recv_sem=recv_sems.at[g], device_id=(mx, my, 1 - mc), device_id_type=MESH)
            cp.start()
            cps.append(cp)
        for g in range(ng):
            theirs = outs[g].at[:, _row_half(1 - mc, 2 * fs[g].shape[1])]
            pltpu.make_async_remote_copy(src_ref=xs[g], dst_ref=theirs, send_sem=send_sems.at[g],
                                         recv_sem=recv_sems.at[g], device_id=(mx, my, 1 - mc),
                                         device_id_type=MESH).wait_recv()
        for cp in cps:
            cp.wait_send()

    outs = pl.pallas_call(
        body, name=name,
        out_shape=[jax.ShapeDtypeStruct((x.shape[0], 2 * x.shape[1], x.shape[2]), x.dtype) for x in fs],
        in_specs=[_ANY] * ng, out_specs=[_ANY] * ng,
        scratch_shapes=[pltpu.SemaphoreType.DMA((ng,)), pltpu.SemaphoreType.DMA((ng,))],
    )(*fs)
    mc = lax.axis_index("c")
    return [lax.dynamic_update_slice(o, x, (0, mc * x.shape[1], 0)) for o, x in zip(outs, fs)]


def _block_rows(r, w, itemsize=4, budget=4 << 20):
    for c in (r, 2048, 1024, 512, 256, 128, 64, 32, 16):
        if c <= r and r % c == 0 and c * w * itemsize <= budget:
            return c
    return r


def _pair_sum(name, g, got, cidx):
    ns, t, r, w = g.shape
    rh = r // 2
    bm = _block_rows(rh, w)
    nb = rh // bm

    def body(c_ref, a_ref, b_ref, o_ref):
        o_ref[...] = (a_ref[...].astype(F32) + b_ref[...].astype(F32)).astype(o_ref.dtype)

    blk = (None, None, bm, w)
    return pl.pallas_call(
        body, name=name,
        grid_spec=pltpu.PrefetchScalarGridSpec(
            num_scalar_prefetch=1, grid=(ns, t, nb),
            in_specs=[pl.BlockSpec(blk, lambda s, tt, i, c: (s, tt, c[0] * nb + i, 0)),
                      pl.BlockSpec(blk, lambda s, tt, i, c: (s, tt, i, 0))],
            out_specs=pl.BlockSpec(blk, lambda s, tt, i, c: (s, tt, i, 0))),
        out_shape=jax.ShapeDtypeStruct((ns, t, rh, w), BF16),
        compiler_params=_params(3 * _nbytes((bm, w), F32)),
    )(cidx, g, got)


def _chip_sum(name, p):
    ns, th, r, w = p.shape
    bm = _block_rows(r, w, budget=2 << 20)

    def body(p_ref, o_ref):
        acc = p_ref[0].astype(F32)
        for s in range(1, ns):
            acc = acc + p_ref[s].astype(F32)
        o_ref[...] = acc

    return pl.pallas_call(
        body, name=name, grid=(th, r // bm),
        in_specs=[pl.BlockSpec((ns, None, bm, w), lambda tt, i: (0, tt, i, 0))],
        out_specs=pl.BlockSpec((None, bm, w), lambda tt, i: (tt, i, 0)),
        out_shape=jax.ShapeDtypeStruct((th, r, w), F32),
        compiler_params=_params(ns * _nbytes((bm, w), BF16) + 2 * _nbytes((bm, w), F32)),
    )(p)


def _sum_leading(name, x):
    n = x.shape[0]

    def body(p_ref, o_ref):
        acc = p_ref[0]
        for s in range(1, n):
            acc = acc + p_ref[s]
        o_ref[...] = acc

    return pl.pallas_call(body, name=name, out_shape=jax.ShapeDtypeStruct(x.shape[1:], F32),
                          compiler_params=_params(2 * _nbytes(x.shape, F32)))(x)


def _reduce_scatter_begin(tag, gs, overlap):
    cidx = lax.axis_index("c").astype(jnp.int32).reshape(1)
    got = _pair_swap_groups(tag + "_pair_swap", gs)
    pair = [_pair_sum(f"{tag}_pair_sum{i}", g, r_, cidx) for i, (g, r_) in enumerate(zip(gs, got))]
    if overlap:
        return _gather_start(tag + "_scatter_start", pair, scatter=True)
    return _chip_scatter_groups(tag + "_chip_scatter", pair)


def _reduce_scatter_end(tag, state, overlap, after):
    if overlap:
        srcs, lands = _gather_wait(tag + "_scatter_wait", state, after, scatter=True)
        state = _place_own_part(lands, srcs)
    fin = [_chip_sum(f"{tag}_chip_sum{i}", p) for i, p in enumerate(state)]
    return _pair_merge_groups(tag + "_pair_merge", fin)


_GROUPS = ((("ffn1_wg", "ffn1_wu", "ffn2_wg", "ffn2_wu"), 1), (("ffn1_wd", "ffn2_wd"), 0), (("w_a",), 0),
           (("w_o",), 0), (("w_in",), 1), (("w_b",), 1))


def _shard_major(g, ax):
    k, n = g.shape
    if ax == 0:
        return g.reshape(4, k // 4, n)
    return g.reshape(k, 4, n // 4).transpose(1, 0, 2)


def _in_cols(d):
    o1 = 3 * DN_WIDTH
    o2 = o1 + DN_WIDTH
    o3 = o2 + 2 * DN_HEADS
    o4 = o3 + 3 * DA_WIDTH
    return dict(wq=(0, o1), wz=(o1, o2), wba=(o2, o3), wda=(o3, o4), wg=(o4, o4 + 2 * d))


def _mixer_weights(w_in, w_a, w_b, w_o, d):
    w = {k: w_in[:, a:b] for k, (a, b) in _in_cols(d).items()}
    w["wba"] = jnp.pad(w["wba"], ((0, 0), (0, LANES - 2 * DN_HEADS)))
    w["w_a"], w["w_b"], w["w_o"] = w_a, w_b, w_o
    return w


def _w_in_grad(wg):
    return jnp.concatenate([wg["wq"], wg["wz"], wg["wba"][:, :2 * DN_HEADS], wg["wda"], wg["wg"]], axis=1)


def _adam_math(wv, gv, mv, vv):
    mn = ADAM_B1 * mv + (1.0 - ADAM_B1) * gv
    vn = ADAM_B2 * vv + (1.0 - ADAM_B2) * jnp.square(gv)
    m_hat = mn / (1.0 - ADAM_B1 ** ADAM_STEP)
    v_hat = vn / (1.0 - ADAM_B2 ** ADAM_STEP)
    delta = -ADAM_LR * (m_hat / (jnp.sqrt(v_hat) + ADAM_EPS) + ADAM_WD * wv)
    return delta, mn, vn


def _adamw(name, w, g, m, v):
    shape = w.shape
    cols = shape[-1]
    w2, g2, m2, v2 = (t.reshape(-1, cols) for t in (w, g, m, v))
    rows = w2.shape[0]
    bm = _pick(rows, (256, 128, 64, 32, 16, 8)) if rows >= 8 else rows
    delta, mn, vn = _rowwise(name, lambda *t: (_adam_math(*t), ()), [w2, g2, m2, v2], [], [(cols, F32)] * 3, bm=bm)
    return delta.reshape(shape), mn.reshape(shape), vn.reshape(shape)


def _adamw_leading(name, w, g, m, v):
    n = w.shape[0]
    padded_row = -(-w.shape[1] // 8) * 8 * w.shape[2] * 4
    bm = max(c for c in range(1, n + 1) if n % c == 0 and (c * padded_row <= (1 << 20) or c == 1))

    def body(w_ref, g_ref, m_ref, v_ref, d_ref, mo_ref, vo_ref):
        d_ref[...], mo_ref[...], vo_ref[...] = _adam_math(w_ref[...], g_ref[...], m_ref[...], v_ref[...])

    spec = pl.BlockSpec((bm,) + w.shape[1:], lambda i: (i, 0, 0))
    return pl.pallas_call(
        body, name=name, grid=(n // bm,), in_specs=[spec] * 4, out_specs=[spec] * 3,
        out_shape=[jax.ShapeDtypeStruct(w.shape, F32)] * 3, compiler_params=_params(7 * bm * padded_row),
    )(w, g, m, v)


def _adamw_stacked(name, w, m, v, gstacks, slot):
    depth, r, cdim = w.shape
    bm = _block_rows(r, cdim, budget=1 << 20)

    def body(w_ref, m_ref, v_ref, *rest):
        g_refs, (go_ref, d_ref, mo_ref, vo_ref) = rest[:depth], rest[depth:]
        layer = pl.program_id(0)
        gv = g_refs[0][...]
        for l in range(1, depth):
            gv = jnp.where(layer == l, g_refs[l][...], gv)
        go_ref[...] = gv
        d_ref[...], mo_ref[...], vo_ref[...] = _adam_math(w_ref[...], gv, m_ref[...], v_ref[...])

    nat = pl.BlockSpec((None, bm, cdim), lambda l, i: (l, i, 0))
    return pl.pallas_call(
        body, name=name, grid=(depth, r // bm),
        in_specs=[nat, nat, nat] + [pl.BlockSpec((None, bm, cdim), lambda l, i: (slot, i, 0))] * depth,
        out_specs=[nat] * 4, out_shape=[jax.ShapeDtypeStruct(w.shape, F32)] * 4,
        compiler_params=_params((7 + depth) * _nbytes((bm, cdim), F32)),
    )(w, m, v, *gstacks)


def kernel(x, c, ada_w, ada_b, ln_ffn1, ln_mix, ln_ffn2, ffn1_wg, ffn1_wu, ffn1_wd, w_in, conv_w, a_log, dt_bias, dn_norm, w_a, w_b, w_o, ffn2_wg, ffn2_wu, ffn2_wd, final_norm, loss_target, m_ada_w, m_ada_b, m_ln_ffn1, m_ln_mix, m_ln_ffn2, m_ffn1_wg, m_ffn1_wu, m_ffn1_wd, m_w_in, m_conv_w, m_a_log, m_dt_bias, m_dn_norm, m_w_a, m_w_b, m_w_o, m_ffn2_wg, m_ffn2_wu, m_ffn2_wd, m_final_norm, v_ada_w, v_ada_b, v_ln_ffn1, v_ln_mix, v_ln_ffn2, v_ffn1_wg, v_ffn1_wu, v_ffn1_wd, v_w_in, v_conv_w, v_a_log, v_dt_bias, v_dn_norm, v_w_a, v_w_b, v_w_o, v_ffn2_wg, v_ffn2_wu, v_ffn2_wd, v_final_norm):
    names = ["ada_w", "ada_b", "ln_ffn1", "ln_mix", "ln_ffn2", "ffn1_wg", "ffn1_wu", "ffn1_wd", "w_in", "conv_w",
             "a_log", "dt_bias", "dn_norm", "w_a", "w_b", "w_o", "ffn2_wg", "ffn2_wu", "ffn2_wd", "final_norm"]
    wts = dict(zip(names, (ada_w, ada_b, ln_ffn1, ln_mix, ln_ffn2, ffn1_wg, ffn1_wu, ffn1_wd, w_in, conv_w, a_log,
                           dt_bias, dn_norm, w_a, w_b, w_o, ffn2_wg, ffn2_wu, ffn2_wd, final_norm)))
    mom = dict(zip(names, (m_ada_w, m_ada_b, m_ln_ffn1, m_ln_mix, m_ln_ffn2, m_ffn1_wg, m_ffn1_wu, m_ffn1_wd, m_w_in,
                           m_conv_w, m_a_log, m_dt_bias, m_dn_norm, m_w_a, m_w_b, m_w_o, m_ffn2_wg, m_ffn2_wu,
                           m_ffn2_wd, m_final_norm)))
    var = dict(zip(names, (v_ada_w, v_ada_b, v_ln_ffn1, v_ln_mix, v_ln_ffn2, v_ffn1_wg, v_ffn1_wu, v_ffn1_wd, v_w_in,
                           v_conv_w, v_a_log, v_dt_bias, v_dn_norm, v_w_a, v_w_b, v_w_o, v_ffn2_wg, v_ffn2_wu,
                           v_ffn2_wd, v_final_norm)))
    _, s, d = x.shape
    depth = ada_w.shape[0]
    mx, my, mc = lax.axis_index("x"), lax.axis_index("y"), lax.axis_index("c")
    chip = 2 * mx + my
    me = 2 * chip + mc
    nshard = ada_w.shape[2]

    cact = _rowwise("c_silu", lambda cv: ((_silu(cv),), ()), [jnp.pad(c, ((0, 7), (0, 0)))], [], [(d, F32)], bm=8)[0]
    c_all = _allgather8("ag_c", cact)[:, 0, :]
    conv_all = _allgather8("ag_conv", jnp.pad(conv_w.reshape(depth * DN_CONV, -1), ((0, 8 - depth * DN_CONV), (0, 0))))
    conv_full = jnp.concatenate([conv_all[2 * j, :depth * DN_CONV] for j in range(4)], axis=1)
    conv_full = conv_full.reshape(depth, DN_CONV, 3 * DN_WIDTH)
    layer_shards = [[jnp.stack([wts[nm][l].astype(BF16) for nm in nms], axis=0) for nms, _ in _GROUPS]
                    for l in range(depth)]
    gathered0 = _gather_groups("ag_weights0", layer_shards[0])
    rows_of = lambda st: st[:, 0].reshape(-1, st.shape[-1])
    cols_of = lambda st: jnp.concatenate([st[j, 0] for j in range(4)], axis=1)

    def layer_weights(l, after):
        if l == 0:
            got = gathered0
        else:
            srcs, lands = _gather_wait(f"ag_weights{l}_wait", started[l], after)
            got = _pair_forward_groups(f"ag_weights{l}_pair", lands, srcs)
        ga, gb, g_wa, g_wo, g_win, g_wb = got
        return ga, gb, _mixer_weights(cols_of(g_win), rows_of(g_wa), cols_of(g_wb), rows_of(g_wo), d)

    c16 = jnp.pad(c_all, ((0, 8), (0, 0))).astype(BF16)
    parts = []
    for l in range(depth):
        bias = lax.dynamic_slice(ada_b[l], (chip * nshard,), (nshard,)).reshape(1, nshard)
        (mp,) = _matmul(f"ada_fwd{l}", c16, ada_w[l].astype(BF16), epi_bcast=[bias], epi=lambda acc, b: (acc + b,))
        parts.append(mp)
    mod_all = _allgather8("ag_mod", jnp.concatenate(parts, axis=0))
    mod_rows = jnp.concatenate([mod_all[2 * j] for j in range(4)], axis=1)
    mod = jnp.stack([lax.dynamic_index_in_dim(mod_rows, l * 16 + me, axis=0, keepdims=False) for l in range(depth)])

    gathered0, later, mod, conv_full = lax.optimization_barrier((gathered0, layer_shards[1:], mod, conv_full))
    started = {l: _gather_start(f"ag_weights{l}_start", later[l - 1]) for l in range(1, depth)}
    for st in started.values():
        mod = mod + st["token"][0, 0]
    small = dict(conv_w=conv_full, a_log=a_log, dt_bias=dt_bias, dn_norm=dn_norm, ln_ffn1=ln_ffn1, ln_mix=ln_mix,
                 ln_ffn2=ln_ffn2, final_norm=final_norm)
    ffn_names = _GROUPS[0][0] + _GROUPS[1][0]
    rs_state = {}

    def on_layer_grads(l, wg):
        wg["w_in"] = _w_in_grad(wg)
        gs = [jnp.stack([wg[nm] if nm in ffn_names else _shard_major(wg[nm], ax) for nm in nms], axis=1)
              for nms, ax in _GROUPS]
        rs_state[l] = _reduce_scatter_begin(f"rs{l}", gs, overlap=l > 0)
        return rs_state[l]["token"][0, 0] if l > 0 else None

    loss_part, dx, dmod, sgrads, d_fnorm = _local_step(x[0], loss_target[0], mod, layer_weights, small,
                                                       on_layer_grads)
    reduced = [_reduce_scatter_end(f"rs{l}", rs_state[l], l > 0, dx) for l in range(depth)]

    dmod_all = _allgather8("ag_dmod", jnp.pad(dmod, ((0, 8 - depth), (0, 0))))
    g_ada_w, g_ada_b = [], []
    for l in range(depth):
        dm_l = dmod_all[:, l, :]
        (gb_l,) = _rowwise(f"ada_b_grad{l}", lambda v: ((), (jnp.sum(v, axis=0, keepdims=True),)), [dm_l], [], [],
                           [(1, N_ADA * d)], bm=8)
        g_ada_b.append(gb_l[0])
        dm_sh = lax.dynamic_slice(dm_l, (0, chip * nshard), (8, nshard))
        (gw_l,) = _matmul(f"ada_w_grad{l}", c16, jnp.pad(dm_sh, ((0, 8), (0, 0))).astype(BF16), ta=True)
        g_ada_w.append(gw_l)
    grads = dict(ada_w=jnp.stack(g_ada_w), ada_b=jnp.stack(g_ada_b))

    smalls = [loss_part.reshape(1), d_fnorm]
    for l in range(depth):
        sg = sgrads[l]
        smalls += [sg["ln_ffn1"], sg["ln_mix"], sg["ln_ffn2"], sg["a_log"], sg["dt_bias"], sg["dn_norm"],
                   sg["conv_w"].reshape(-1)]
    sizes = [t.shape[0] for t in smalls]
    tile = 8 * LANES
    flat = jnp.concatenate([jnp.pad(t, (0, (-t.shape[0]) % tile)).reshape(-1, LANES) for t in smalls], axis=0)
    tot = _sum_leading("small_sum", _allgather8("ag_small", flat))
    offs, acc = [], 0
    for n_ in sizes:
        offs.append(acc)
        acc += -(-n_ // tile) * 8
    take = lambda i: tot[offs[i]:offs[i] + -(-sizes[i] // tile) * 8].reshape(-1)[:sizes[i]]
    loss = take(0)[0]
    grads["final_norm"] = take(1)
    per = 7
    for key_i, key in enumerate(["ln_ffn1", "ln_mix", "ln_ffn2", "a_log", "dt_bias", "dn_norm"]):
        grads[key] = jnp.stack([take(2 + per * l + key_i) for l in range(depth)])
    conv_g = jnp.stack([take(2 + per * l + 6).reshape(DN_CONV, 3 * DN_WIDTH) for l in range(depth)])
    csh = conv_w.shape[2]
    grads["conv_w"] = lax.dynamic_slice(conv_g, (0, 0, chip * csh), (depth, DN_CONV, csh))

    deltas, new_m, new_v = {}, {}, {}
    for gi, (nms, ax) in enumerate(_GROUPS):
        for q, nm in enumerate(nms):
            wv, mv, vv = wts[nm], mom[nm], var[nm]
            per_layer = [reduced[l][gi][q] for l in range(depth)]
            if ax == 1 and wv.shape[2] % LANES and nm != "w_in":
                tr = lambda t: jnp.swapaxes(t, 1, 2)
                gt = jnp.stack([g.T for g in per_layer], axis=0)
                dl, mn, vn = _adamw("adamw_" + nm, tr(wv), gt, tr(mv), tr(vv))
                grads[nm], deltas[nm], new_m[nm], new_v[nm] = tr(gt), tr(dl), tr(mn), tr(vn)
            elif nm == "w_in" and wv.shape[2] % LANES:
                tr = lambda t: jnp.transpose(t, (2, 0, 1))
                back = lambda t: jnp.transpose(t, (1, 2, 0))
                gt = jnp.stack([g.T for g in per_layer], axis=1)
                dl, mn, vn = _adamw_leading("adamw_" + nm, tr(wv), gt, tr(mv), tr(vv))
                grads[nm], deltas[nm], new_m[nm], new_v[nm] = back(gt), back(dl), back(mn), back(vn)
            else:
                grads[nm], deltas[nm], new_m[nm], new_v[nm] = _adamw_stacked(
                    "adamw_" + nm, wv, mv, vv, [reduced[l][gi] for l in range(depth)], q)

    for name in names:
        if name in deltas:
            continue
        wv, gv, mv, vv = wts[name], grads[name], mom[name], var[name]
        if wv.ndim == 1:
            wv, gv, mv, vv = (t.reshape(-1, LANES) for t in (wv, gv, mv, vv))
        dl, mn, vn = _adamw("adamw_" + name, wv, gv, mv, vv)
        deltas[name], new_m[name], new_v[name] = (t.reshape(wts[name].shape) for t in (dl, mn, vn))
    return (loss, dx.reshape(1, s, d), *[grads[n_] for n_ in names], *[deltas[n_] for n_ in names],
            *[new_m[n_] for n_ in names], *[new_v[n_] for n_ in names])
```

```python
import functools

import jax
import jax.numpy as jnp
from jax import lax
from jax.experimental import pallas as pl
from jax.experimental.pallas import tpu as pltpu

F32 = jnp.float32
BF16 = jnp.bfloat16
MESH = pl.DeviceIdType.MESH

NORM_EPS = 1e-6
DN_HEADS, DN_DIM, DN_CHUNK, DN_CONV = 8, 128, 64, 4
DN_WIDTH = DN_HEADS * DN_DIM
DA_HEADS, DA_DIM, DA_BLOCK = 12, 64, 128
DA_WIDTH = DA_HEADS * DA_DIM
DA_PATTERNS = ((128, 1), (512, 4), (2048, 16))
ALIBI_MAX_EXP = 8.0
N_ADA = 9
LANES = 128
V7X_VMEM_BYTES = 64 << 20
ADAM_LR, ADAM_B1, ADAM_B2, ADAM_EPS, ADAM_WD, ADAM_STEP = 0.001, 0.9, 0.999, 1e-08, 0.01, 10
NEG = -1e30
HI = lax.Precision.HIGHEST
NN = (((1,), (0,)), ((), ()))
NT = (((1,), (1,)), ((), ()))
TN = (((0,), (0,)), ((), ()))


def _nbytes(shape, dtype):
    n = 1
    for s in shape:
        n *= s
    return n * jnp.dtype(dtype).itemsize


def _params(block_bytes, scratch_bytes=0):
    need = 2 * block_bytes + scratch_bytes
    lim = min(max(need + need // 4 + (4 << 20), 32 << 20), V7X_VMEM_BYTES - (6 << 20))
    return pltpu.CompilerParams(vmem_limit_bytes=int(lim))


def _pick(n, cands):
    for c in cands:
        if c <= n and n % c == 0:
            return c
    return n


def _sigmoid(x):
    return jax.nn.sigmoid(x)


def _silu(x):
    return x * jax.nn.sigmoid(x)


def _softplus(x):
    return jnp.maximum(x, 0.0) + jnp.log(1.0 + jnp.exp(-jnp.abs(x)))


def _rowwise(name, fn, rows, bcast, row_outs, red_outs=(), bm=256):
    rows = [r if isinstance(r, tuple) else (r, r.shape[1], 0) for r in rows]
    s = rows[0][0].shape[0]
    bm = _pick(s, (bm, 128, 64, 32, 16, 8))
    nr, nb, no, nd = len(rows), len(bcast), len(row_outs), len(red_outs)
    in_specs = [pl.BlockSpec((bm, w), functools.partial(lambda i, ci: (i, ci), ci=ci)) for (_, w, ci) in rows]
    in_specs += [pl.BlockSpec(b.shape, lambda i: (0, 0)) for b in bcast]
    out_shape = [jax.ShapeDtypeStruct((s, w), dt) for (w, dt) in row_outs]
    out_shape += [jax.ShapeDtypeStruct((r, w), F32) for (r, w) in red_outs]
    out_specs = [pl.BlockSpec((bm, w), lambda i: (i, 0)) for (w, _) in row_outs]
    out_specs += [pl.BlockSpec((r, w), lambda i: (0, 0)) for (r, w) in red_outs]

    def body(*refs):
        ins = [r[...] for r in refs[:nr + nb]]
        outs = refs[nr + nb:nr + nb + no]
        reds = refs[nr + nb + no:]
        ov, rv = fn(*ins)
        for o, v in zip(outs, ov):
            o[...] = v.astype(o.dtype)
        if nd:
            @pl.when(pl.program_id(0) == 0)
            def _():
                for r in reds:
                    r[...] = jnp.zeros(r.shape, F32)
            for r, v in zip(reds, rv):
                r[...] += v.astype(F32)

    blk = sum(_nbytes((bm, w), a.dtype) for (a, w, _) in rows) + sum(_nbytes(b.shape, b.dtype) for b in bcast)
    blk += sum(_nbytes((bm, w), dt) for (w, dt) in row_outs) + sum(_nbytes(r, F32) for r in red_outs)
    res = pl.pallas_call(
        body, name=name, grid=(s // bm,), in_specs=in_specs, out_specs=out_specs, out_shape=out_shape,
        compiler_params=_params(3 * blk),
    )(*[a for (a, _, _) in rows], *bcast)
    return res


def _matmul(name, a, b, *, ta=False, tb=False, outs=(F32,), epi=None, epi_rows=(), epi_bcast=(),
            bm=None, bn=None, bk=None):
    if ta:
        k, m = a.shape
    else:
        m, k = a.shape
    n = b.shape[0] if tb else b.shape[1]
    assert (b.shape[1] if tb else b.shape[0]) == k, (name, a.shape, b.shape)
    if bm is None:
        bm = _pick(m, (1024, 1408, 768, 512, 384, 256, 128)) if ta else _pick(m, (1024, 512, 256, 128, 64, 32, 16))
    if bn is None:
        bn = _pick(n, (512, 384, 256, 128))
    if bk is None:
        bk = k if k <= 3072 else _pick(k, (2816, 2048, 1024, 512))
        if ta:
            bk = _pick(k, (1024, 512, 256, 128, 64, 32, 16))
    nk = k // bk
    dims = TN if ta else (NT if tb else NN)
    a_spec = pl.BlockSpec((bk, bm), lambda i, j, kk: (kk, i)) if ta else pl.BlockSpec((bm, bk), lambda i, j, kk: (i, kk))
    b_spec = pl.BlockSpec((bn, bk), lambda i, j, kk: (j, kk)) if tb else pl.BlockSpec((bk, bn), lambda i, j, kk: (kk, j))
    in_specs = [a_spec, b_spec]
    in_specs += [pl.BlockSpec((bm, bn), lambda i, j, kk: (i, j)) for _ in epi_rows]
    in_specs += [pl.BlockSpec((1, bn), lambda i, j, kk: (0, j)) for _ in epi_bcast]
    out_shape = [jax.ShapeDtypeStruct((m, n), dt) for dt in outs]
    out_specs = [pl.BlockSpec((bm, bn), lambda i, j, kk: (i, j)) for _ in outs]
    ner, neb, no = len(epi_rows), len(epi_bcast), len(outs)

    def body(*refs):
        a_ref, b_ref = refs[0], refs[1]
        extra = refs[2:2 + ner + neb]
        out_refs = refs[2 + ner + neb:2 + ner + neb + no]
        prod = lax.dot_general(a_ref[...], b_ref[...], dims, preferred_element_type=F32)

        def finish(acc):
            vals = epi(acc, *[r[...] for r in extra]) if epi is not None else (acc,)
            for o, v in zip(out_refs, vals):
                o[...] = v.astype(o.dtype)

        if nk == 1:
            finish(prod)
        else:
            acc_ref = refs[-1]
            kk = pl.program_id(2)

            @pl.when(kk == 0)
            def _():
                acc_ref[...] = prod

            @pl.when(kk > 0)
            def _():
                acc_ref[...] += prod

            @pl.when(kk == nk - 1)
            def _():
                finish(acc_ref[...])

    blk = _nbytes((bm, bk), a.dtype) + _nbytes((bk, bn), b.dtype)
    blk += sum(_nbytes((bm, bn), r.dtype) for r in epi_rows) + sum(_nbytes((bm, bn), dt) for dt in outs)
    scratch = [pltpu.VMEM((bm, bn), F32)] if nk > 1 else []
    res = pl.pallas_call(
        body, name=name, grid=(m // bm, n // bn, nk), in_specs=in_specs, out_specs=out_specs,
        out_shape=out_shape, scratch_shapes=scratch,
        compiler_params=_params(blk, 3 * _nbytes((bm, bn), F32)),
    )(a, b, *epi_rows, *epi_bcast)
    return res


def _mm_core(name, grid, nk, pairs, out_defs, acc_shape, epi=None, epi_ins=()):
    npair, nep, no = len(pairs), len(epi_ins), len(out_defs)

    def body(*refs):
        extra = refs[2 * npair:2 * npair + nep]
        out_refs = refs[2 * npair + nep:2 * npair + nep + no]
        prod = None
        for p in range(npair):
            d = lax.dot_general(refs[2 * p][...], refs[2 * p + 1][...], pairs[p][4], preferred_element_type=F32)
            prod = d if prod is None else prod + d

        def finish(acc):
            vals = epi(acc, *[r[...] for r in extra]) if epi is not None else (acc,)
            for o, v in zip(out_refs, vals):
                o[...] = v.astype(o.dtype)

        if nk == 1:
            finish(prod)
        else:
            acc_ref = refs[-1]
            kk = pl.program_id(2)

            @pl.when(kk == 0)
            def _():
                acc_ref[...] = prod

            @pl.when(kk > 0)
            def _():
                acc_ref[...] += prod

            @pl.when(kk == nk - 1)
            def _():
                finish(acc_ref[...])

    def blk_bytes(spec, dtype):
        return _nbytes([s for s in spec.block_shape if s is not None], dtype)

    blk = sum(blk_bytes(sa, a.dtype) + blk_bytes(sb, b.dtype) for (a, sa, b, sb, _) in pairs)
    blk += sum(blk_bytes(sp, arr.dtype) for (arr, sp) in epi_ins) + sum(blk_bytes(sp, dt) for (_, dt, sp) in out_defs)
    ins, in_specs = [], []
    for (a, sa, b, sb, _) in pairs:
        ins += [a, b]
        in_specs += [sa, sb]
    ins += [arr for (arr, _) in epi_ins]
    in_specs += [sp for (_, sp) in epi_ins]
    return pl.pallas_call(
        body, name=name, grid=grid, in_specs=in_specs, out_specs=[sp for (_, _, sp) in out_defs],
        out_shape=[jax.ShapeDtypeStruct(sh, dt) for (sh, dt, _) in out_defs],
        scratch_shapes=[pltpu.VMEM(acc_shape, F32)] if nk > 1 else [],
        compiler_params=_params(blk, 3 * _nbytes(acc_shape, F32)),
    )(*ins)


def _rms_mod(h, ln, sh, sc):
    n = h * lax.rsqrt(jnp.mean(h * h, axis=-1, keepdims=True) + NORM_EPS) * ln
    return n * (1.0 + sc) + sh


def _swiglu_act(g, u):
    return _silu(g.astype(F32)) * u.astype(F32)


def _dn_prep(yc, pba, alog, dtb):
    act = _silu(yc)
    parts = []
    for idx in range(2 * DN_HEADS):
        seg = act[:, idx * DN_DIM:(idx + 1) * DN_DIM]
        seg = seg * lax.rsqrt(jnp.sum(seg * seg, axis=-1, keepdims=True) + NORM_EPS)
        if idx < DN_HEADS:
            seg = seg * (DN_DIM ** -0.5)
        parts.append(seg)
    parts.append(act[:, 2 * DN_WIDTH:])
    qkvn = jnp.concatenate(parts, axis=1)
    lane = lax.broadcasted_iota(jnp.int32, pba.shape, 1)
    beta = _sigmoid(pba)
    g = -jnp.exp(alog) * _softplus(pba + dtb)
    gb = jnp.where(lane < DN_HEADS, beta, jnp.where(lane < 2 * DN_HEADS, g, 0.0))
    return qkvn, gb


def _dn_outnorm(o_a, z, dn):
    parts = []
    for h in range(DN_HEADS):
        seg = o_a[:, h * DN_DIM:(h + 1) * DN_DIM]
        seg = seg * lax.rsqrt(jnp.mean(seg * seg, axis=-1, keepdims=True) + NORM_EPS) * dn
        parts.append(seg)
    return jnp.concatenate(parts, axis=1) * _silu(z)


def _shift_down(x, halo8, s):
    r = pltpu.roll(x, s, axis=0)
    top = pltpu.roll(halo8, s, axis=0)
    i8 = lax.broadcasted_iota(jnp.int32, top.shape, 0)
    return jnp.concatenate([jnp.where(i8 < s, top, r[0:8]), r[8:]], axis=0)


def _shift_up(x, halo8, s):
    m = x.shape[0]
    r = pltpu.roll(x, m - s, axis=0)
    bot = pltpu.roll(halo8, 8 - s, axis=0)
    i8 = lax.broadcasted_iota(jnp.int32, bot.shape, 0)
    return jnp.concatenate([r[:m - 8], jnp.where(i8 >= 8 - s, bot, r[m - 8:])], axis=0)


def _conv_prep_fwd(name, pq, convw8, pba, alog, dtb, bm=256):
    s, w = pq.shape
    nblk = s // bm
    hb = bm // 16

    def body(x_ref, halo_ref, w_ref, pba_ref, alog_ref, dtb_ref, yc_ref, qkv_ref, gb_ref):
        i = pl.program_id(0)
        x = x_ref[...].astype(F32)
        halo = jnp.where(i > 0, halo_ref[...].astype(F32)[8:16], 0.0)
        cw = w_ref[...]
        y = x * cw[DN_CONV - 1:DN_CONV]
        for sft in range(1, DN_CONV):
            y = y + _shift_down(x, halo, sft) * cw[DN_CONV - 1 - sft:DN_CONV - sft]
        ycb = y.astype(BF16)
        yc_ref[...] = ycb
        qkvn, gb = _dn_prep(ycb.astype(F32), pba_ref[...], alog_ref[...], dtb_ref[...])
        qkv_ref[...] = qkvn.astype(BF16)
        gb_ref[...] = gb

    blk = 3 * _nbytes((bm, w), BF16) + 4 * _nbytes((bm, w), F32)
    return pl.pallas_call(
        body, name=name, grid=(nblk,),
        in_specs=[pl.BlockSpec((bm, w), lambda i: (i, 0)),
                  pl.BlockSpec((16, w), lambda i: (jnp.maximum(i * hb - 1, 0), 0)),
                  pl.BlockSpec(convw8.shape, lambda i: (0, 0)),
                  pl.BlockSpec((bm, LANES), lambda i: (i, 0)),
                  pl.BlockSpec((1, LANES), lambda i: (0, 0)),
                  pl.BlockSpec((1, LANES), lambda i: (0, 0))],
        out_specs=[pl.BlockSpec((bm, w), lambda i: (i, 0)), pl.BlockSpec((bm, w), lambda i: (i, 0)),
                   pl.BlockSpec((bm, LANES), lambda i: (i, 0))],
        out_shape=[jax.ShapeDtypeStruct((s, w), BF16), jax.ShapeDtypeStruct((s, w), BF16),
                   jax.ShapeDtypeStruct((s, LANES), F32)],
        compiler_params=_params(blk),
    )(pq, pq, convw8, pba, alog, dtb)


def _conv_bwd(name, dyc, pq, convw8, bm=256):
    s, w = pq.shape
    nblk = s // bm
    hb = bm // 16

    def body(dy_ref, dyn_ref, x_ref, xh_ref, w_ref, dx_ref, dw_ref):
        i = pl.program_id(0)
        dy = dy_ref[...].astype(F32)
        nxt = jnp.where(i < nblk - 1, dyn_ref[...].astype(F32)[0:8], 0.0)
        x = x_ref[...].astype(F32)
        halo = jnp.where(i > 0, xh_ref[...].astype(F32)[8:16], 0.0)
        cw = w_ref[...]
        dx = dy * cw[DN_CONV - 1:DN_CONV]
        for sft in range(1, DN_CONV):
            dx = dx + _shift_up(dy, nxt, sft) * cw[DN_CONV - 1 - sft:DN_CONV - sft]
        dx_ref[...] = dx.astype(dx_ref.dtype)
        r8 = lax.broadcasted_iota(jnp.int32, (8, w), 0)
        dw = jnp.zeros((8, w), F32)
        for j in range(DN_CONV):
            sft = DN_CONV - 1 - j
            xs = x if sft == 0 else _shift_down(x, halo, sft)
            dw = dw + jnp.where(r8 == j, jnp.sum(dy * xs, axis=0, keepdims=True), 0.0)

        @pl.when(i == 0)
        def _():
            dw_ref[...] = jnp.zeros((8, w), F32)
        dw_ref[...] += dw

    blk = 4 * _nbytes((bm, w), BF16) + 5 * _nbytes((bm, w), F32)
    return pl.pallas_call(
        body, name=name, grid=(nblk,),
        in_specs=[pl.BlockSpec((bm, w), lambda i: (i, 0)),
                  pl.BlockSpec((16, w), lambda i: (jnp.minimum((i + 1) * hb, s // 16 - 1), 0)),
                  pl.BlockSpec((bm, w), lambda i: (i, 0)),
                  pl.BlockSpec((16, w), lambda i: (jnp.maximum(i * hb - 1, 0), 0)),
                  pl.BlockSpec(convw8.shape, lambda i: (0, 0))],
        out_specs=[pl.BlockSpec((bm, w), lambda i: (i, 0)), pl.BlockSpec((8, w), lambda i: (0, 0))],
        out_shape=[jax.ShapeDtypeStruct((s, w), BF16), jax.ShapeDtypeStruct((8, w), F32)],
        compiler_params=_params(blk),
    )(dyc, dyc, pq, pq, convw8)


BNN = (((2,), (1,)), ((0,), (0,)))
BNT = (((2,), (2,)), ((0,), (0,)))
BTN = (((1,), (1,)), ((0,), (0,)))


def _raw_dot_1pass(a, b, dims):
    return lax.dot_general(a.astype(BF16), b.astype(BF16), dims, preferred_element_type=F32)


def _raw_dot_3pass(a, b, dims):
    ah = a.astype(BF16)
    al = (a - ah.astype(F32)).astype(BF16)
    bh = b.astype(BF16)
    bl = (b - bh.astype(F32)).astype(BF16)
    d = lambda x, y: lax.dot_general(x, y, dims, preferred_element_type=F32)
    return d(ah, bh) + (d(ah, bl) + d(al, bh))


def _with_same_precision_vjp(raw):
    @functools.partial(jax.custom_vjp, nondiff_argnums=(2,))
    def dot(a, b, dims):
        return raw(a, b, dims)

    def fwd(a, b, dims):
        return raw(a, b, dims), (a, b)

    def bwd(dims, res, ct):
        a, b = res
        if dims == BNN:
            return raw(ct, b, BNT), raw(a, ct, BTN)
        if dims == BNT:
            return raw(ct, b, BNN), raw(ct, a, BTN)
        assert dims == BTN
        return raw(b, ct, BNT), raw(a, ct, BNN)

    dot.defvjp(fwd, bwd)
    return dot


_dot_1pass_vjp = _with_same_precision_vjp(_raw_dot_1pass)
_dot_3pass_vjp = _with_same_precision_vjp(_raw_dot_3pass)


def _dot_bf16(a, b, dims=BNN):
    return _dot_1pass_vjp(a, b, dims)


def _dot_3pass(a, b, dims=BNN):
    return _dot_3pass_vjp(a, b, dims)


def _neumann_inverse(x):
    h, c, _ = x.shape
    eye = lax.broadcasted_iota(jnp.int32, (h, c, c), 1) == lax.broadcasted_iota(jnp.int32, (h, c, c), 2)
    t = jnp.where(eye, 1.0, 0.0) + x
    p = x
    for _ in range(5):
        p = _raw_dot_3pass(p, p, BNN)
        t = t + _raw_dot_3pass(t, p, BNN)
    return t


@jax.custom_vjp
def _known_inverse(x, t):
    return t


def _known_inverse_fwd(x, t):
    return t, t


def _known_inverse_bwd(t, ct):
    return _raw_dot_3pass(_raw_dot_3pass(t, ct, BTN), t, BNT), jnp.zeros_like(t)


_known_inverse.defvjp(_known_inverse_fwd, _known_inverse_bwd)


def _delta_chunk(q, k, v, gcol, bcol, state, t_known=None):
    h, c, _ = q.shape
    row = lax.broadcasted_iota(jnp.int32, (h, c, c), 1)
    col = lax.broadcasted_iota(jnp.int32, (h, c, c), 2)
    incl, strict, eye = row >= col, row > col, row == col
    g_b = jnp.broadcast_to(gcol, (h, c, c))
    gc_row = jnp.sum(jnp.where(row <= col, g_b, 0.0), axis=1, keepdims=True)
    g_r = jnp.sum(jnp.where(eye, g_b, 0.0), axis=1, keepdims=True)
    gc_col = jnp.sum(jnp.where(incl, jnp.broadcast_to(g_r, (h, c, c)), 0.0), axis=2, keepdims=True)
    decay = jnp.exp(jnp.where(incl, gc_col - gc_row, NEG))
    kb = k * bcol
    vb = v * bcol
    x = -jnp.where(strict, _dot_bf16(kb, k, BNT) * decay, 0.0)
    t = _neumann_inverse(x) if t_known is None else _known_inverse(x, t_known)
    eg = jnp.exp(gc_col)
    u = _dot_3pass(t, vb)
    w = _dot_3pass(t, kb * eg)
    qk = _dot_bf16(q, k, BNT) * decay
    v_new = u - _dot_bf16(w, state)
    o = _dot_bf16(q * eg, state) + _dot_bf16(qk, v_new)
    g_last = jnp.sum(g_r, axis=2, keepdims=True)
    new_state = state * jnp.exp(g_last) + _dot_bf16(k * jnp.exp(g_last - gc_col), v_new, BTN)
    return o, new_state, t


def _lane_col(blk, idx):
    lane = lax.broadcasted_iota(jnp.int32, blk.shape, 1)
    return jnp.sum(jnp.where(lane == idx, blk, 0.0), axis=1, keepdims=True)


def _dn_heads(ref, base):
    return jnp.stack([ref[:, base + h * DN_DIM:base + (h + 1) * DN_DIM] for h in range(DN_HEADS)], axis=0).astype(F32)


def _dn_cols(gbv, base):
    return jnp.stack([_lane_col(gbv, base + h) for h in range(DN_HEADS)], axis=0)


def _delta_fwd(name, qkvn, gb):
    s = qkvn.shape[0]
    n = s // DN_CHUNK
    c = DN_CHUNK

    def body(qkv_ref, gb_ref, o_ref, st_ref, t_ref, state):
        @pl.when(pl.program_id(0) == 0)
        def _():
            state[...] = jnp.zeros(state.shape, F32)

        gbv = gb_ref[...]
        st = state[...]
        st_ref[0] = st
        o, new, t = _delta_chunk(_dn_heads(qkv_ref, 0), _dn_heads(qkv_ref, DN_WIDTH), _dn_heads(qkv_ref, 2 * DN_WIDTH),
                                 _dn_cols(gbv, DN_HEADS), _dn_cols(gbv, 0), st)
        for h in range(DN_HEADS):
            o_ref[:, h * DN_DIM:(h + 1) * DN_DIM] = o[h]
        t_ref[0] = t
        state[...] = new

    blk = _nbytes((c, 3 * DN_WIDTH), BF16) + _nbytes((c, LANES), F32) + _nbytes((c, DN_WIDTH), F32)
    blk += _nbytes((DN_HEADS, DN_DIM, DN_DIM), F32) + _nbytes((DN_HEADS, c, c), F32)
    return pl.pallas_call(
        body, name=name, grid=(n,),
        in_specs=[pl.BlockSpec((c, 3 * DN_WIDTH), lambda i: (i, 0)), pl.BlockSpec((c, LANES), lambda i: (i, 0))],
        out_specs=[pl.BlockSpec((c, DN_WIDTH), lambda i: (i, 0)),
                   pl.BlockSpec((1, DN_HEADS, DN_DIM, DN_DIM), lambda i: (i, 0, 0, 0)),
                   pl.BlockSpec((1, DN_HEADS, c, c), lambda i: (i, 0, 0, 0))],
        out_shape=[jax.ShapeDtypeStruct((s, DN_WIDTH), F32),
                   jax.ShapeDtypeStruct((n, DN_HEADS, DN_DIM, DN_DIM), F32),
                   jax.ShapeDtypeStruct((n, DN_HEADS, c, c), F32)],
        scratch_shapes=[pltpu.VMEM((DN_HEADS, DN_DIM, DN_DIM), F32)],
        compiler_params=_params(blk, 8 << 20),
    )(qkvn, gb)


def _delta_bwd(name, qkvn, gb, states, tinv, d_o):
    s = qkvn.shape[0]
    n = s // DN_CHUNK
    c = DN_CHUNK

    def body(qkv_ref, gb_ref, st_ref, t_ref, do_ref, dqkv_ref, dgb_ref, dstate):
        @pl.when(pl.program_id(0) == 0)
        def _():
            dstate[...] = jnp.zeros(dstate.shape, F32)

        gbv = gb_ref[...]
        lane = lax.broadcasted_iota(jnp.int32, (c, LANES), 1)
        t_known = t_ref[0]
        chunk = lambda *args: _delta_chunk(*args, t_known=t_known)[:2]
        _, vjp = jax.vjp(chunk, _dn_heads(qkv_ref, 0), _dn_heads(qkv_ref, DN_WIDTH),
                         _dn_heads(qkv_ref, 2 * DN_WIDTH), _dn_cols(gbv, DN_HEADS), _dn_cols(gbv, 0), st_ref[0])
        dq, dk, dv, dg, db, dst = vjp((_dn_heads(do_ref, 0), dstate[...]))
        dgb = jnp.zeros((c, LANES), F32)
        for h in range(DN_HEADS):
            dqkv_ref[:, h * DN_DIM:(h + 1) * DN_DIM] = dq[h]
            dqkv_ref[:, DN_WIDTH + h * DN_DIM:DN_WIDTH + (h + 1) * DN_DIM] = dk[h]
            dqkv_ref[:, 2 * DN_WIDTH + h * DN_DIM:2 * DN_WIDTH + (h + 1) * DN_DIM] = dv[h]
            dgb = dgb + jnp.where(lane == h, db[h], 0.0) + jnp.where(lane == DN_HEADS + h, dg[h], 0.0)
        dstate[...] = dst
        dgb_ref[...] = dgb

    rev = lambda i: (n - 1 - i, 0)
    blk = _nbytes((c, 3 * DN_WIDTH), BF16) + 2 * _nbytes((c, LANES), F32) + _nbytes((c, DN_WIDTH), F32)
    blk += _nbytes((DN_HEADS, DN_DIM, DN_DIM), F32) + _nbytes((c, 3 * DN_WIDTH), F32)
    return pl.pallas_call(
        body, name=name, grid=(n,),
        in_specs=[pl.BlockSpec((c, 3 * DN_WIDTH), rev), pl.BlockSpec((c, LANES), rev),
                  pl.BlockSpec((1, DN_HEADS, DN_DIM, DN_DIM), lambda i: (n - 1 - i, 0, 0, 0)),
                  pl.BlockSpec((1, DN_HEADS, c, c), lambda i: (n - 1 - i, 0, 0, 0)),
                  pl.BlockSpec((c, DN_WIDTH), rev)],
        out_specs=[pl.BlockSpec((c, 3 * DN_WIDTH), rev), pl.BlockSpec((c, LANES), rev)],
        out_shape=[jax.ShapeDtypeStruct((s, 3 * DN_WIDTH), F32), jax.ShapeDtypeStruct((s, LANES), F32)],
        scratch_shapes=[pltpu.VMEM((DN_HEADS, DN_DIM, DN_DIM), F32)],
        compiler_params=_params(blk, 16 << 20),
    )(qkvn, gb, states, tinv, d_o)


def _da_scores(q2f, k2, sub, valid, distf, head):
    lane = lax.broadcasted_iota(jnp.int32, q2f.shape, 1)
    hmask = (lane < DA_DIM) if sub == 0 else (lane >= DA_DIM)
    qm = jnp.where(hmask, q2f, 0.0).astype(BF16)
    slope = 2.0 ** (-ALIBI_MAX_EXP * (head + 1) / DA_HEADS)
    sc = lax.dot_general(qm, k2, NT, preferred_element_type=F32) * (DA_DIM ** -0.5)
    return jnp.where(valid, sc - slope * distf, NEG), qm, hmask


def _da_mask(i, r):
    qi = lax.broadcasted_iota(jnp.int32, (DA_BLOCK, 2 * DA_BLOCK), 0)
    ki = lax.broadcasted_iota(jnp.int32, (DA_BLOCK, 2 * DA_BLOCK), 1)
    dist = qi + DA_BLOCK - ki
    valid = (dist >= 0) & (dist <= DA_BLOCK) & ((ki >= DA_BLOCK) | (i > 0))
    return valid, (dist * r).astype(F32)


def _da_fwd(name, pda, r):
    s = pda.shape[0]
    n = s // r
    nb = n // DA_BLOCK
    w = DA_WIDTH
    dav = pda.reshape(n, r * 3 * w)

    def body(q_ref, kc_ref, kp_ref, vc_ref, vp_ref, o_ref, lse_ref):
        i = pl.program_id(1)
        valid, distf = _da_mask(i, r)
        lane = lax.broadcasted_iota(jnp.int32, (DA_BLOCK, LANES), 1)
        lse = jnp.zeros((DA_BLOCK, LANES), F32)
        for hp in range(DA_HEADS // 2):
            sl = slice(hp * LANES, (hp + 1) * LANES)
            q2f = q_ref[:, sl].astype(F32)
            k2 = jnp.concatenate([kp_ref[:, sl], kc_ref[:, sl]], axis=0)
            v2 = jnp.concatenate([vp_ref[:, sl], vc_ref[:, sl]], axis=0)
            o2 = None
            for sub in range(2):
                head = 2 * hp + sub
                sc, _, hmask = _da_scores(q2f, k2, sub, valid, distf, head)
                mx = jnp.max(sc, axis=1, keepdims=True)
                p = jnp.exp(sc - mx)
                l = jnp.sum(p, axis=1, keepdims=True)
                pv = lax.dot_general(p.astype(BF16), v2, NN, preferred_element_type=F32) / l
                o2 = pv if sub == 0 else jnp.where(hmask, pv, o2)
                lse = jnp.where(lane == head, mx + jnp.log(l), lse)
            o_ref[:, sl] = o2.astype(o_ref.dtype)
        lse_ref[...] = lse

    prev = lambda col: (lambda p, i: (jnp.maximum(i - 1, 0), 3 * p + col))
    cur = lambda col: (lambda p, i: (i, 3 * p + col))
    blk = 5 * _nbytes((DA_BLOCK, w), BF16) + _nbytes((DA_BLOCK, w), F32) + _nbytes((DA_BLOCK, LANES), F32)
    o, lse = pl.pallas_call(
        body, name=name, grid=(r, nb),
        in_specs=[pl.BlockSpec((DA_BLOCK, w), cur(0)), pl.BlockSpec((DA_BLOCK, w), cur(1)),
                  pl.BlockSpec((DA_BLOCK, w), prev(1)), pl.BlockSpec((DA_BLOCK, w), cur(2)),
                  pl.BlockSpec((DA_BLOCK, w), prev(2))],
        out_specs=[pl.BlockSpec((DA_BLOCK, w), lambda p, i: (i, p)),
                   pl.BlockSpec((DA_BLOCK, LANES), lambda p, i: (i, p))],
        out_shape=[jax.ShapeDtypeStruct((n, r * w), BF16), jax.ShapeDtypeStruct((n, r * LANES), F32)],
        compiler_params=_params(blk, 8 << 20),
    )(dav, dav, dav, dav, dav)
    return o.reshape(s, w), lse.reshape(s, LANES)


def _da_bwd(name, pda, d_ob, lse_tot, delta, r):
    s = pda.shape[0]
    n = s // r
    nb = n // DA_BLOCK
    w = DA_WIDTH
    dav = pda.reshape(n, r * 3 * w)
    dov = d_ob.reshape(n, r * w)
    lv = lse_tot.reshape(n, r * LANES)
    dlv = delta.reshape(n, r * LANES)

    def body(q_ref, kc_ref, kp_ref, vc_ref, vp_ref, do_ref, l_ref, dl_ref, dq_ref, dk_ref, dv_ref, ck, cv):
        i = pl.program_id(1)

        @pl.when(i == 0)
        def _():
            ck[...] = jnp.zeros(ck.shape, F32)
            cv[...] = jnp.zeros(cv.shape, F32)

        @pl.when(i < nb)
        def _():
            valid, distf = _da_mask(i, r)
            lsev = l_ref[...]
            dlt = dl_ref[...]
            for hp in range(DA_HEADS // 2):
                sl = slice(hp * LANES, (hp + 1) * LANES)
                q2f = q_ref[:, sl].astype(F32)
                k2 = jnp.concatenate([kp_ref[:, sl], kc_ref[:, sl]], axis=0)
                v2 = jnp.concatenate([vp_ref[:, sl], vc_ref[:, sl]], axis=0)
                do2f = do_ref[:, sl].astype(F32)
                dq2 = jnp.zeros((DA_BLOCK, LANES), F32)
                dk2 = jnp.zeros((2 * DA_BLOCK, LANES), F32)
                dv2 = jnp.zeros((2 * DA_BLOCK, LANES), F32)
                for sub in range(2):
                    head = 2 * hp + sub
                    sc, qm, hmask = _da_scores(q2f, k2, sub, valid, distf, head)
                    p = jnp.exp(sc - _lane_col(lsev, head))
                    dom = jnp.where(hmask, do2f, 0.0).astype(BF16)
                    dp = lax.dot_general(dom, v2, NT, preferred_element_type=F32)
                    ds = (p * (dp - _lane_col(dlt, head)) * (DA_DIM ** -0.5)).astype(BF16)
                    dq2 = dq2 + jnp.where(hmask, lax.dot_general(ds, k2, NN, preferred_element_type=F32), 0.0)
                    dk2 = dk2 + lax.dot_general(ds, qm, TN, preferred_element_type=F32)
                    dv2 = dv2 + lax.dot_general(p.astype(BF16), dom, TN, preferred_element_type=F32)
                dq_ref[:, sl] = dq2.astype(dq_ref.dtype)
                dk_ref[:, sl] = (ck[:, sl] + dk2[:DA_BLOCK]).astype(dk_ref.dtype)
                dv_ref[:, sl] = (cv[:, sl] + dv2[:DA_BLOCK]).astype(dv_ref.dtype)
                ck[:, sl] = dk2[DA_BLOCK:]
                cv[:, sl] = dv2[DA_BLOCK:]

        @pl.when(i == nb)
        def _():
            dk_ref[...] = ck[...].astype(dk_ref.dtype)
            dv_ref[...] = cv[...].astype(dv_ref.dtype)

    qrow = lambda i: jnp.minimum(i, nb - 1)
    prev = lambda col: (lambda p, i: (jnp.maximum(qrow(i) - 1, 0), 3 * p + col))
    cur = lambda col: (lambda p, i: (qrow(i), 3 * p + col))
    same = lambda p, i: (qrow(i), p)
    late = lambda p, i: (jnp.maximum(i - 1, 0), p)
    blk = 6 * _nbytes((DA_BLOCK, w), BF16) + 2 * _nbytes((DA_BLOCK, LANES), F32) + 3 * _nbytes((DA_BLOCK, w), F32)
    dq, dk, dv = pl.pallas_call(
        body, name=name, grid=(r, nb + 1),
        in_specs=[pl.BlockSpec((DA_BLOCK, w), cur(0)), pl.BlockSpec((DA_BLOCK, w), cur(1)),
                  pl.BlockSpec((DA_BLOCK, w), prev(1)), pl.BlockSpec((DA_BLOCK, w), cur(2)),
                  pl.BlockSpec((DA_BLOCK, w), prev(2)), pl.BlockSpec((DA_BLOCK, w), same),
                  pl.BlockSpec((DA_BLOCK, LANES), same), pl.BlockSpec((DA_BLOCK, LANES), same)],
        out_specs=[pl.BlockSpec((DA_BLOCK, w), same), pl.BlockSpec((DA_BLOCK, w), late),
                   pl.BlockSpec((DA_BLOCK, w), late)],
        out_shape=[jax.ShapeDtypeStruct((n, r * w), BF16)] * 3,
        scratch_shapes=[pltpu.VMEM((DA_BLOCK, w), F32), pltpu.VMEM((DA_BLOCK, w), F32)],
        compiler_params=_params(blk, 12 << 20),
    )(dav, dav, dav, dav, dav, dov, lv, dlv)
    return dq.reshape(s, w), dk.reshape(s, w), dv.reshape(s, w)


def _head_expand():
    hrow = lax.broadcasted_iota(jnp.int32, (LANES, DA_WIDTH), 0)
    lcol = lax.broadcasted_iota(jnp.int32, (LANES, DA_WIDTH), 1)
    return jnp.where(lcol // DA_DIM == hrow, 1.0, 0.0).astype(F32)


def _ffn_up(name, a, ga, tg, tu):
    s, d = a.shape
    nsh, _, _, ffs = ga.shape
    bm = _pick(s, (1024, 512, 256, 128))

    def body(a_ref, wg_ref, wu_ref, g_ref, u_ref, f_ref):
        av = a_ref[...]
        g = lax.dot_general(av, wg_ref[...], NN, preferred_element_type=F32)
        u = lax.dot_general(av, wu_ref[...], NN, preferred_element_type=F32)
        g_ref[...] = g.astype(BF16)
        u_ref[...] = u.astype(BF16)
        f_ref[...] = (_silu(g) * u).astype(BF16)

    wspec = lambda t: pl.BlockSpec((None, None, d, ffs), lambda i, j: (j, t, 0, 0))
    ospec = pl.BlockSpec((None, bm, ffs), lambda i, j: (j, i, 0))
    blk = _nbytes((bm, d), BF16) + 2 * _nbytes((d, ffs), BF16) + 3 * _nbytes((bm, ffs), BF16)
    return pl.pallas_call(
        body, name=name, grid=(s // bm, nsh),
        in_specs=[pl.BlockSpec((bm, d), lambda i, j: (i, 0)), wspec(tg), wspec(tu)],
        out_specs=[ospec] * 3, out_shape=[jax.ShapeDtypeStruct((nsh, s, ffs), BF16)] * 3,
        compiler_params=_params(blk, 4 * _nbytes((bm, ffs), F32)),
    )(a, ga, ga)


def _ffn_fwd(tag, h_in, ln, sh, sc, gt, ga, tg, tu, gb, td, weight):
    s, d = h_in.shape
    nsh, _, ffs, _ = gb.shape
    (a,) = _rowwise(tag + "_norm", lambda h, l, s1, s2: ((_rms_mod(h, l, s1, s2),), ()), [h_in], [ln, sh, sc],
                    [(d, BF16)])
    g, u, f = _ffn_up(tag + "_up", a, ga, tg, tu)
    bm, bn = _pick(s, (1024, 512, 256, 128)), _pick(d, (1024, 512, 256, 128))
    io = pl.BlockSpec((bm, bn), lambda i, j, kk: (i, j))
    h_out, o = _mm_core(
        tag + "_down", (s // bm, d // bn, nsh), nsh,
        [(f, pl.BlockSpec((None, bm, ffs), lambda i, j, kk: (kk, i, 0)),
          gb, pl.BlockSpec((None, None, ffs, bn), lambda i, j, kk: (kk, td, 0, j)), NN)],
        [((s, d), F32, io), ((s, d), BF16, io)], (bm, bn),
        epi=lambda acc, h, gv: (h + weight * gv * acc, acc),
        epi_ins=[(h_in, io), (gt, pl.BlockSpec((1, bn), lambda i, j, kk: (0, j)))])
    return h_out, dict(a=a, g=g, u=u, f=f, o=o)


def _resid_bwd(tag, dh_out, o, gt, weight):
    d = dh_out.shape[1]

    def fn(dh, ov, g):
        return (weight * g * dh,), (jnp.sum(weight * dh * ov.astype(F32), axis=0, keepdims=True),)

    do, d_gt = _rowwise(tag + "_resid_bwd", fn, [dh_out, o], [gt], [(d, BF16)], [(1, d)])
    return do, d_gt


def _norm_bwd(tag, h_in, da, dh_out, ln, sh, sc):
    d = h_in.shape[1]

    def fn(h, dav, dh, l, s1, s2):
        _, vjp = jax.vjp(_rms_mod, h, l, s1, s2)
        gh, gl, gs1, gs2 = vjp(dav)
        return (dh + gh,), (gl, gs1, gs2)

    return _rowwise(tag + "_norm_bwd", fn, [h_in, da, dh_out], [ln, sh, sc], [(d, F32)], [(1, d)] * 3)


def _ffn_bwd(tag, h_in, dh_out, sv, ln, sh, sc, gt, ga, tg, tu, gb, td, weight):
    s, d = h_in.shape
    nsh, _, ffs, _ = gb.shape
    bm, bn = _pick(s, (1024, 512, 256, 128)), _pick(d, (1024, 512, 256, 128))
    bk = _pick(s, (1024, 512, 256, 128))
    do, d_gt = _resid_bwd(tag, dh_out, sv["o"], gt, weight)

    def act_bwd(df, g, u):
        _, vjp = jax.vjp(_swiglu_act, g, u)
        return vjp(df)

    hid = pl.BlockSpec((None, bm, ffs), lambda i, j, kk: (j, i, 0))
    dg, du = _mm_core(
        tag + "_down_dx", (s // bm, nsh, 1), 1,
        [(do, pl.BlockSpec((bm, d), lambda i, j, kk: (i, 0)),
          gb, pl.BlockSpec((None, None, ffs, d), lambda i, j, kk: (j, td, 0, 0)), NT)],
        [((nsh, s, ffs), BF16, hid)] * 2, (bm, ffs), epi=act_bwd, epi_ins=[(sv["g"], hid), (sv["u"], hid)])
    (d_wd,) = _mm_core(
        tag + "_down_dw", (nsh, d // bn, s // bk), s // bk,
        [(sv["f"], pl.BlockSpec((None, bk, ffs), lambda i, j, kk: (i, kk, 0)),
          do, pl.BlockSpec((bk, bn), lambda i, j, kk: (kk, j)), TN)],
        [((nsh, ffs, d), BF16, pl.BlockSpec((None, ffs, bn), lambda i, j, kk: (i, 0, j)))], (ffs, bn))
    kmaj = pl.BlockSpec((None, bm, ffs), lambda i, j, kk: (kk, i, 0))
    wsp = lambda t: pl.BlockSpec((None, None, bn, ffs), functools.partial(lambda i, j, kk, t: (kk, t, j, 0), t=t))
    (da,) = _mm_core(
        tag + "_up_dx", (s // bm, d // bn, nsh), nsh, [(dg, kmaj, ga, wsp(tg), NT), (du, kmaj, ga, wsp(tu), NT)],
        [((s, d), F32, pl.BlockSpec((bm, bn), lambda i, j, kk: (i, j)))], (bm, bn))
    dws = []
    for nm, dh in (("_wg_dw", dg), ("_wu_dw", du)):
        (dw,) = _mm_core(
            tag + nm, (1, nsh, s // bk), s // bk,
            [(sv["a"], pl.BlockSpec((bk, d), lambda i, j, kk: (kk, 0)),
              dh, pl.BlockSpec((None, bk, ffs), lambda i, j, kk: (j, kk, 0)), TN)],
            [((nsh, d, ffs), BF16, pl.BlockSpec((None, d, ffs), lambda i, j, kk: (j, 0, 0)))], (d, ffs))
        dws.append(dw)
    dh_in, d_ln, d_sh, d_sc = _norm_bwd(tag, h_in, da, dh_out, ln, sh, sc)
    return dh_in, dict(wg=dws[0], wu=dws[1], wd=d_wd), dict(ln=d_ln, sh=d_sh, sc=d_sc, gt=d_gt)


def _mixer_fwd(tag, h_in, ln, sh, sc, gt, w, sp):
    d = h_in.shape[1]
    (a,) = _rowwise(tag + "_norm", lambda h, l, s1, s2: ((_rms_mod(h, l, s1, s2),), ()), [h_in], [ln, sh, sc],
                    [(d, BF16)])
    (pq,) = _matmul(tag + "_pq", a, w["wq"], outs=(BF16,))
    (pz,) = _matmul(tag + "_pz", a, w["wz"], outs=(BF16,))
    (pba,) = _matmul(tag + "_pba", a, w["wba"])
    (pda,) = _matmul(tag + "_pda", a, w["wda"], outs=(BF16,))
    (pg,) = _matmul(tag + "_pg", a, w["wg"], outs=(BF16,))
    yc, qkvn, gb = _conv_prep_fwd(tag + "_conv", pq, sp["conv8"], pba, sp["alog"], sp["dtb"])
    o_a, states, tinv = _delta_fwd(tag + "_delta", qkvn, gb)
    (o_an,) = _rowwise(tag + "_dnorm", lambda o, z, dn: ((_dn_outnorm(o, z.astype(F32), dn),), ()), [o_a, pz],
                       [sp["dn"]], [(DN_WIDTH, BF16)])
    ops, lses = [], []
    for (_, r) in DA_PATTERNS:
        o_p, lse_p = _da_fwd(f"{tag}_da{r}", pda, r)
        ops.append(o_p)
        lses.append(lse_p)

    def merge(o1, o2, o3, l1, l2, l3):
        mx = jnp.maximum(jnp.maximum(l1, l2), l3)
        e1, e2, e3 = jnp.exp(l1 - mx), jnp.exp(l2 - mx), jnp.exp(l3 - mx)
        tot = e1 + e2 + e3
        ex = _head_expand()
        up = lambda wgt: lax.dot_general(wgt / tot, ex, NN, precision=HI, preferred_element_type=F32)
        return (up(e1) * o1 + up(e2) * o2 + up(e3) * o3, mx + jnp.log(tot)), ()

    o_b, lse_tot = _rowwise(tag + "_merge", merge, ops + lses, [], [(DA_WIDTH, BF16), (LANES, F32)])
    (y_a,) = _matmul(tag + "_wa", o_an, w["w_a"], outs=(BF16,))
    (y_b,) = _matmul(tag + "_wb", o_b, w["w_b"], outs=(BF16,))

    def gate(ga, gbv, ya, yb):
        return _sigmoid(ga.astype(F32)) * ya.astype(F32) + _sigmoid(gbv.astype(F32)) * yb.astype(F32)

    (merged,) = _rowwise(tag + "_gate", lambda *v: ((gate(*v),), ()), [(pg, d, 0), (pg, d, 1), y_a, y_b], [],
                         [(d, BF16)])
    h_out, m = _matmul(tag + "_wo", merged, w["w_o"], outs=(F32, BF16), epi_rows=[h_in], epi_bcast=[gt],
                       epi=lambda acc, h, g: (h + g * acc, acc))
    sv = dict(a=a, pq=pq, pz=pz, pba=pba, pda=pda, pg=pg, yc=yc, qkvn=qkvn, gb=gb, o_a=o_a, states=states, tinv=tinv,
              o_an=o_an, o_b=o_b, lse=lse_tot, y_a=y_a, y_b=y_b, merged=merged, m=m, gate=gate)
    return h_out, sv


def _mixer_bwd(tag, h_in, dh_out, sv, ln, sh, sc, gt, w, sp):
    d = h_in.shape[1]
    dm, d_gt = _resid_bwd(tag, dh_out, sv["m"], gt, 1.0)
    (d_merged,) = _matmul(tag + "_wo_dx", dm, w["w_o"], tb=True, outs=(BF16,))
    (d_wo,) = _matmul(tag + "_wo_dw", sv["merged"], dm, ta=True, outs=(BF16,))
    gate = sv["gate"]

    def gate_bwd(dmg, ga, gbv, ya, yb):
        _, vjp = jax.vjp(gate, ga.astype(F32), gbv.astype(F32), ya.astype(F32), yb.astype(F32))
        dga, dgb, dya, dyb = vjp(dmg.astype(F32))
        return (jnp.concatenate([dga, dgb], axis=1), dya, dyb), ()

    pg = sv["pg"]
    d_pg, d_ya, d_yb = _rowwise(tag + "_gate_bwd", gate_bwd, [d_merged, (pg, d, 0), (pg, d, 1), sv["y_a"], sv["y_b"]],
                                [], [(2 * d, BF16), (d, BF16), (d, BF16)])
    (d_oan,) = _matmul(tag + "_wa_dx", d_ya, w["w_a"], tb=True)
    (d_wa,) = _matmul(tag + "_wa_dw", sv["o_an"], d_ya, ta=True, outs=(BF16,))
    (d_ob,) = _matmul(tag + "_wb_dx", d_yb, w["w_b"], tb=True, outs=(BF16,))
    (d_wb,) = _matmul(tag + "_wb_dw", sv["o_b"], d_yb, ta=True, outs=(BF16,))

    def dnorm_bwd(doan, o, z, dn):
        _, vjp = jax.vjp(_dn_outnorm, o, z.astype(F32), dn)
        go, gz, gdn = vjp(doan)
        return (go, gz), (gdn,)

    d_oa, d_pz, d_dn = _rowwise(tag + "_dnorm_bwd", dnorm_bwd, [d_oan, sv["o_a"], sv["pz"]], [sp["dn"]],
                                [(DN_WIDTH, F32), (DN_WIDTH, BF16)], [(1, DN_DIM)])
    d_qkvn, d_gb = _delta_bwd(tag + "_delta_bwd", sv["qkvn"], sv["gb"], sv["states"], sv["tinv"], d_oa)

    def prep_bwd(dq, dgbv, yc, pba, alog, dtb):
        _, vjp = jax.vjp(_dn_prep, yc.astype(F32), pba, alog, dtb)
        gyc, gpba, galog, gdtb = vjp((dq, dgbv))
        return (gyc, gpba), (galog, gdtb)

    d_yc, d_pba, d_alog, d_dtb = _rowwise(tag + "_prep_bwd", prep_bwd, [d_qkvn, d_gb, sv["yc"], sv["pba"]],
                                          [sp["alog"], sp["dtb"]], [(3 * DN_WIDTH, BF16), (LANES, BF16)],
                                          [(1, LANES), (1, LANES)], bm=128)
    d_pq, d_conv = _conv_bwd(tag + "_conv_bwd", d_yc, sv["pq"], sp["conv8"])

    def delta_fn(dob, ob):
        prod = dob.astype(F32) * ob.astype(F32)
        return (lax.dot_general(prod, _head_expand(), NT, precision=HI, preferred_element_type=F32),), ()

    (delta,) = _rowwise(tag + "_da_delta", delta_fn, [d_ob, sv["o_b"]], [], [(LANES, F32)])
    grads = [_da_bwd(f"{tag}_da{r}_bwd", sv["pda"], d_ob, sv["lse"], delta, r) for (_, r) in DA_PATTERNS]

    def sum3(*parts):
        q1, k1, v1, q2, k2, v2, q3, k3, v3 = (p.astype(F32) for p in parts)
        return (jnp.concatenate([q1 + q2 + q3, k1 + k2 + k3, v1 + v2 + v3], axis=1),), ()

    (d_pda,) = _rowwise(tag + "_da_sum", sum3, [t for g in grads for t in g], [], [(3 * DA_WIDTH, BF16)])

    a = sv["a"]
    (da,) = _matmul(tag + "_pq_dx", d_pq, w["wq"], tb=True)
    add = lambda acc, prev: (acc + prev,)
    (da,) = _matmul(tag + "_pz_dx", d_pz, w["wz"], tb=True, epi_rows=[da], epi=add)
    (da,) = _matmul(tag + "_pba_dx", d_pba, w["wba"], tb=True, epi_rows=[da], epi=add)
    (da,) = _matmul(tag + "_pda_dx", d_pda, w["wda"], tb=True, epi_rows=[da], epi=add)
    (da,) = _matmul(tag + "_pg_dx", d_pg, w["wg"], tb=True, epi_rows=[da], epi=add)
    (d_wq,) = _matmul(tag + "_pq_dw", a, d_pq, ta=True, outs=(BF16,))
    (d_wz,) = _matmul(tag + "_pz_dw", a, d_pz, ta=True, outs=(BF16,))
    (d_wba,) = _matmul(tag + "_pba_dw", a, d_pba, ta=True, outs=(BF16,))
    (d_wda,) = _matmul(tag + "_pda_dw", a, d_pda, ta=True, outs=(BF16,))
    (d_wg,) = _matmul(tag + "_pg_dw", a, d_pg, ta=True, outs=(BF16,))
    dh_in, d_ln, d_sh, d_sc = _norm_bwd(tag, h_in, da, dh_out, ln, sh, sc)
    wgrads = dict(wq=d_wq, wz=d_wz, wba=d_wba, wda=d_wda, wg=d_wg, w_a=d_wa, w_b=d_wb, w_o=d_wo)
    small = dict(ln=d_ln, sh=d_sh, sc=d_sc, gt=d_gt, dn=d_dn, alog=d_alog, dtb=d_dtb, conv=d_conv)
    return dh_in, wgrads, small


def _loss_head(h, target, fnorm):
    d = h.shape[1]

    def fn(hv, tv, fw):
        def lossf(hh, ww):
            y = hh * lax.rsqrt(jnp.mean(hh * hh, axis=-1, keepdims=True) + NORM_EPS) * ww
            return 0.5 * jnp.sum(jnp.mean(jnp.square(y - tv), axis=-1))

        val, (dh, dw) = jax.value_and_grad(lossf, argnums=(0, 1))(hv, fw)
        return (dh,), (jnp.full((1, LANES), val, F32), dw)

    return _rowwise("loss_head", fn, [h, target], [fnorm], [(d, F32)], [(1, LANES), (1, d)])


def _row(v):
    return v.reshape(1, -1)


def _pad_lanes(v, offset):
    return jnp.pad(v.reshape(1, -1), ((0, 0), (offset, LANES - offset - v.shape[0])))


_UP_SLOTS = dict(ffn1_wg=0, ffn1_wu=1, ffn2_wg=2, ffn2_wu=3)
_DOWN_SLOTS = dict(ffn1_wd=0, ffn2_wd=1)


def _local_step(x2, target, mod, layer_weights, small, on_layer_grads):
    depth = mod.shape[0]
    d = x2.shape[1]
    h = x2
    saved = []
    mods = []
    up = lambda l, nm: _UP_SLOTS[nm]
    down = lambda l, nm: _DOWN_SLOTS[nm]
    for l in range(depth):
        m9 = [_row(mod[l, i * d:(i + 1) * d]) for i in range(N_ADA)]
        sp = dict(conv8=jnp.pad(small["conv_w"][l], ((0, 8 - DN_CONV), (0, 0))),
                  alog=_pad_lanes(small["a_log"][l], DN_HEADS), dtb=_pad_lanes(small["dt_bias"][l], DN_HEADS),
                  dn=_row(small["dn_norm"][l]))
        ga, gb, w = layer_weights(l, h)
        h0 = h
        h1, sv1 = _ffn_fwd(f"l{l}_ffn1", h0, _row(small["ln_ffn1"][l]), m9[0], m9[1], m9[2], ga, up(l, "ffn1_wg"),
                           up(l, "ffn1_wu"), gb, down(l, "ffn1_wd"), 0.5)
        h2, sv2 = _mixer_fwd(f"l{l}_mix", h1, _row(small["ln_mix"][l]), m9[3], m9[4], m9[5], w, sp)
        h3, sv3 = _ffn_fwd(f"l{l}_ffn2", h2, _row(small["ln_ffn2"][l]), m9[6], m9[7], m9[8], ga, up(l, "ffn2_wg"),
                           up(l, "ffn2_wu"), gb, down(l, "ffn2_wd"), 0.5)
        saved.append((h0, h1, h2, sv1, sv2, sv3, sp, ga, gb, w))
        mods.append(m9)
        h = h3
    dh, loss_part, d_fnorm = _loss_head(h, target, _row(small["final_norm"]))
    sgrads, dmods = [], []
    token = None
    for l in reversed(range(depth)):
        h0, h1, h2, sv1, sv2, sv3, sp, ga, gb, w = saved[l]
        m9 = mods[l] if token is None else [r + token for r in mods[l]]
        dh, g3, s3 = _ffn_bwd(f"l{l}_ffn2", h2, dh, sv3, _row(small["ln_ffn2"][l]), m9[6], m9[7], m9[8], ga,
                              up(l, "ffn2_wg"), up(l, "ffn2_wu"), gb, down(l, "ffn2_wd"), 0.5)
        dh, g2, s2 = _mixer_bwd(f"l{l}_mix", h1, dh, sv2, _row(small["ln_mix"][l]), m9[3], m9[4], m9[5], w, sp)
        dh, g1, s1 = _ffn_bwd(f"l{l}_ffn1", h0, dh, sv1, _row(small["ln_ffn1"][l]), m9[0], m9[1], m9[2], ga,
                              up(l, "ffn1_wg"), up(l, "ffn1_wu"), gb, down(l, "ffn1_wd"), 0.5)
        token = on_layer_grads(l, dict(ffn1_wg=g1["wg"], ffn1_wu=g1["wu"], ffn1_wd=g1["wd"], ffn2_wg=g3["wg"],
                                       ffn2_wu=g3["wu"], ffn2_wd=g3["wd"], **g2))
        dmods.append(jnp.concatenate([s1["sh"], s1["sc"], s1["gt"], s2["sh"], s2["sc"], s2["gt"],
                                      s3["sh"], s3["sc"], s3["gt"]], axis=1))
        sgrads.append(dict(ln_ffn1=s1["ln"][0], ln_mix=s2["ln"][0], ln_ffn2=s3["ln"][0],
                           a_log=s2["alog"][0, DN_HEADS:2 * DN_HEADS], dt_bias=s2["dtb"][0, DN_HEADS:2 * DN_HEADS],
                           dn_norm=s2["dn"][0], conv_w=s2["conv"][:DN_CONV]))
    sgrads.reverse()
    dmods.reverse()
    return loss_part[0, 0], dh, jnp.concatenate(dmods, axis=0), sgrads, d_fnorm[0]


def _flip(v, bit):
    return 1 - v if bit else v


def _allgather8(name, x):
    r, c = x.shape

    def body(x_ref, out_ref, send_sems, recv_sems, local_sem):
        mx, my, mc = lax.axis_index("x"), lax.axis_index("y"), lax.axis_index("c")
        me = 4 * mx + 2 * my + mc
        mine = pltpu.make_async_copy(x_ref, out_ref.at[me], local_sem)
        mine.start()
        sends = []
        for k in range(1, 8):
            peer = (_flip(mx, k & 4), _flip(my, k & 2), _flip(mc, k & 1))
            cp = pltpu.make_async_remote_copy(src_ref=x_ref, dst_ref=out_ref.at[me], send_sem=send_sems.at[k - 1],
                                              recv_sem=recv_sems.at[k - 1], device_id=peer, device_id_type=MESH)
            cp.start()
            sends.append(cp)
        for k in range(1, 8):
            peer = (_flip(mx, k & 4), _flip(my, k & 2), _flip(mc, k & 1))
            src = 4 * peer[0] + 2 * peer[1] + peer[2]
            pltpu.make_async_remote_copy(src_ref=x_ref, dst_ref=out_ref.at[src], send_sem=send_sems.at[k - 1],
                                         recv_sem=recv_sems.at[k - 1], device_id=peer, device_id_type=MESH).wait_recv()
        for cp in sends:
            cp.wait_send()
        mine.wait()

    return pl.pallas_call(
        body, name=name, out_shape=jax.ShapeDtypeStruct((8, r, c), x.dtype),
        in_specs=[pl.BlockSpec(memory_space=pltpu.VMEM)], out_specs=pl.BlockSpec(memory_space=pltpu.VMEM),
        scratch_shapes=[pltpu.SemaphoreType.DMA((7,)), pltpu.SemaphoreType.DMA((7,)), pltpu.SemaphoreType.DMA],
        compiler_params=_params(9 * _nbytes((r, c), x.dtype)),
    )(x)


def _chip_peers(mx, my):
    chips = [(1 - mx, my), (mx, 1 - my), (1 - mx, 1 - my)]
    return chips, [2 * cx + cy for (cx, cy) in chips]


_ANY = pl.BlockSpec(memory_space=pl.ANY)


def _row_half(mc, r):
    return pl.ds(pl.multiple_of(mc * (r // 2), 16), r // 2)


def _gather_groups(name, shards):
    ng = len(shards)

    def body(*refs):
        xs, outs = refs[:ng], refs[ng:2 * ng]
        send_sems, recv_sems = refs[2 * ng:]
        mx, my, mc = lax.axis_index("x"), lax.axis_index("y"), lax.axis_index("c")
        j = 2 * mx + my
        chips, idxs = _chip_peers(mx, my)
        sib = (mx, my, 1 - mc)

        def copy(k, src, dst, to):
            return pltpu.make_async_remote_copy(src_ref=src, dst_ref=dst, send_sem=send_sems.at[k],
                                                recv_sem=recv_sems.at[k], device_id=to, device_id_type=MESH)

        first, passed = [], []
        for g in range(ng):
            mine = _row_half(mc, shards[g].shape[1])
            for t, chip in enumerate(chips):
                cp = copy(6 * g + t, xs[g].at[:, mine], outs[g].at[j, :, mine], (*chip, mc))
                cp.start()
                first.append(cp)
        for g in range(ng):
            mine = _row_half(mc, shards[g].shape[1])
            for t, chip in enumerate(chips):
                landed = outs[g].at[idxs[t], :, mine]
                copy(6 * g + t, landed, landed, (*chip, mc)).wait_recv()
                fwd = copy(6 * g + 3 + t, landed, landed, sib)
                fwd.start()
                passed.append(fwd)
        for g in range(ng):
            theirs_half = _row_half(1 - mc, shards[g].shape[1])
            for t in range(3):
                theirs = outs[g].at[idxs[t], :, theirs_half]
                copy(6 * g + 3 + t, theirs, theirs, sib).wait_recv()
        for cp in first + passed:
            cp.wait_send()

    outs = pl.pallas_call(
        body, name=name, out_shape=[jax.ShapeDtypeStruct((4,) + x.shape, x.dtype) for x in shards],
        in_specs=[_ANY] * ng, out_specs=[_ANY] * ng,
        scratch_shapes=[pltpu.SemaphoreType.DMA((6 * ng,)), pltpu.SemaphoreType.DMA((6 * ng,))],
    )(*shards)
    return _place_own_slab(outs, shards)


def _place_own_slab(outs, shards):
    chip = 2 * lax.axis_index("x") + lax.axis_index("y")
    return [lax.dynamic_update_slice(o, x[None], (chip,) + (0,) * x.ndim) for o, x in zip(outs, shards)]


_HBM = pl.BlockSpec(memory_space=pltpu.HBM)
_SEM = pl.BlockSpec(memory_space=pltpu.SEMAPHORE)
_DATAFLOW = pltpu.SideEffectType.DATAFLOW_SIDE_EFFECTING


def _ici_gather_copies(src_refs, land_refs, send_sems, recv_sems, scatter=False):
    mx, my, mc = lax.axis_index("x"), lax.axis_index("y"), lax.axis_index("c")
    j = 2 * mx + my
    chips, idxs = _chip_peers(mx, my)
    sends, recvs = [], []
    for g, src in enumerate(src_refs):
        for t, chip in enumerate(chips):
            common = dict(send_sem=send_sems.at[3 * g + t], recv_sem=recv_sems.at[3 * g + t], device_id=(*chip, mc),
                          device_id_type=MESH)
            if scatter:
                out, to, frm = src.at[idxs[t]], land_refs[g].at[j], land_refs[g].at[idxs[t]]
            else:
                mine = _row_half(mc, src.shape[1])
                out, to, frm = src.at[:, mine], land_refs[g].at[j, :, mine], land_refs[g].at[idxs[t], :, mine]
            sends.append(pltpu.make_async_remote_copy(src_ref=out, dst_ref=to, **common))
            recvs.append(pltpu.make_async_remote_copy(src_ref=out, dst_ref=frm, **common))
    return sends, recvs


def _gather_start(name, shards, scatter=False):
    ng = len(shards)

    def body(*refs):
        srcs, lands = refs[:ng], refs[ng:2 * ng]
        send_sems, recv_sems = refs[2 * ng], refs[2 * ng + 1]
        token = refs[-1]
        sends, _ = _ici_gather_copies(srcs, lands, send_sems, recv_sems, scatter)
        for cp in sends:
            cp.start()
        token[...] = jnp.zeros(token.shape, token.dtype)

    land_shape = lambda x: x.shape if scatter else (4,) + x.shape
    srcs = [pltpu.with_memory_space_constraint(x, pltpu.HBM) for x in shards]
    lands = [pltpu.with_memory_space_constraint(lax.empty(land_shape(x), x.dtype), pltpu.HBM) for x in shards]
    res = pl.pallas_call(
        body, name=name,
        out_shape=(pltpu.SemaphoreType.DMA((3 * ng,)), pltpu.SemaphoreType.DMA((3 * ng,)),
                   *[pltpu.HBM(x.shape, x.dtype) for x in srcs], *[pltpu.HBM(x.shape, x.dtype) for x in lands],
                   jax.ShapeDtypeStruct((8, LANES), F32)),
        in_specs=[_HBM] * (2 * ng),
        out_specs=(_SEM, _SEM, *[_HBM] * (2 * ng), pl.BlockSpec(memory_space=pltpu.VMEM)),
        input_output_aliases={i: 2 + i for i in range(2 * ng)},
        compiler_params=pltpu.CompilerParams(has_side_effects=_DATAFLOW),
    )(*srcs, *lands)
    return dict(send_sems=res[0], recv_sems=res[1], srcs=list(res[2:2 + ng]), lands=list(res[2 + ng:2 + 2 * ng]),
                token=res[-1])


def _gather_wait(name, started, after, scatter=False):
    ng = len(started["srcs"])

    def body(*refs):
        srcs, lands = refs[:ng], refs[ng:2 * ng]
        send_sems, recv_sems = refs[2 * ng], refs[2 * ng + 1]
        sends, recvs = _ici_gather_copies(srcs, lands, send_sems, recv_sems, scatter)
        for cp in sends:
            cp.wait_send()
        for cp in recvs:
            cp.wait_recv()

    res = pl.pallas_call(
        body, name=name,
        out_shape=[pltpu.HBM(x.shape, x.dtype) for x in started["srcs"] + started["lands"]],
        in_specs=[_HBM] * (2 * ng) + [_SEM, _SEM, _ANY], out_specs=[_HBM] * (2 * ng),
        input_output_aliases={i: i for i in range(2 * ng)},
        compiler_params=pltpu.CompilerParams(has_side_effects=_DATAFLOW),
    )(*started["srcs"], *started["lands"], started["send_sems"], started["recv_sems"], after)
    return list(res[:ng]), list(res[ng:])


def _pair_forward_groups(name, lands, shards):
    ng = len(lands)

    def body(*refs):
        ins, outs = refs[:ng], refs[ng:2 * ng]
        send_sems, recv_sems = refs[2 * ng:]
        mx, my, mc = lax.axis_index("x"), lax.axis_index("y"), lax.axis_index("c")
        _, idxs = _chip_peers(mx, my)
        sib = (mx, my, 1 - mc)
        cps = []
        for g in range(ng):
            mine = _row_half(mc, lands[g].shape[2])
            for t in range(3):
                cp = pltpu.make_async_remote_copy(src_ref=ins[g].at[idxs[t], :, mine], dst_ref=outs[g].at[idxs[t], :, mine],
                                                  send_sem=send_sems.at[3 * g + t], recv_sem=recv_sems.at[3 * g + t],
                                                  device_id=sib, device_id_type=MESH)
                cp.start()
                cps.append(cp)
        for g in range(ng):
            theirs = _row_half(1 - mc, lands[g].shape[2])
            for t in range(3):
                pltpu.make_async_remote_copy(src_ref=ins[g].at[idxs[t], :, theirs], dst_ref=outs[g].at[idxs[t], :, theirs],
                                             send_sem=send_sems.at[3 * g + t], recv_sem=recv_sems.at[3 * g + t],
                                             device_id=sib, device_id_type=MESH).wait_recv()
        for cp in cps:
            cp.wait_send()

    outs = pl.pallas_call(
        body, name=name, out_shape=[jax.ShapeDtypeStruct(x.shape, x.dtype) for x in lands],
        in_specs=[_ANY] * ng, out_specs=[_ANY] * ng, input_output_aliases={i: i for i in range(ng)},
        scratch_shapes=[pltpu.SemaphoreType.DMA((3 * ng,)), pltpu.SemaphoreType.DMA((3 * ng,))],
    )(*lands)
    return _place_own_slab(outs, shards)


def _pair_swap_groups(name, gs):
    ng = len(gs)

    def body(*refs):
        xs, outs = refs[:ng], refs[ng:2 * ng]
        send_sems, recv_sems = refs[2 * ng:]
        mx, my, mc = lax.axis_index("x"), lax.axis_index("y"), lax.axis_index("c")
        cps = []
        for g in range(ng):
            cp = pltpu.make_async_remote_copy(src_ref=xs[g].at[:, :, _row_half(1 - mc, gs[g].shape[2])], dst_ref=outs[g],
                                              send_sem=send_sems.at[g], recv_sem=recv_sems.at[g],
                                              device_id=(mx, my, 1 - mc), device_id_type=MESH)
            cp.start()
            cps.append(cp)
        for cp in cps:
            cp.wait()

    return pl.pallas_call(
        body, name=name,
        out_shape=[jax.ShapeDtypeStruct(x.shape[:2] + (x.shape[2] // 2, x.shape[3]), x.dtype) for x in gs],
        in_specs=[_ANY] * ng, out_specs=[_ANY] * ng,
        scratch_shapes=[pltpu.SemaphoreType.DMA((ng,)), pltpu.SemaphoreType.DMA((ng,))],
    )(*gs)


def _chip_scatter_groups(name, ps):
    ng = len(ps)

    def body(*refs):
        xs, outs = refs[:ng], refs[ng:2 * ng]
        send_sems, recv_sems = refs[2 * ng:]
        mx, my, mc = lax.axis_index("x"), lax.axis_index("y"), lax.axis_index("c")
        j = 2 * mx + my
        chips, idxs = _chip_peers(mx, my)
        sends = []
        for g in range(ng):
            for t, chip in enumerate(chips):
                cp = pltpu.make_async_remote_copy(src_ref=xs[g].at[idxs[t]], dst_ref=outs[g].at[j],
                                                  send_sem=send_sems.at[3 * g + t], recv_sem=recv_sems.at[3 * g + t],
                                                  device_id=(*chip, mc), device_id_type=MESH)
                cp.start()
                sends.append(cp)
        for g in range(ng):
            for t, chip in enumerate(chips):
                pltpu.make_async_remote_copy(src_ref=xs[g].at[idxs[t]], dst_ref=outs[g].at[idxs[t]],
                                             send_sem=send_sems.at[3 * g + t], recv_sem=recv_sems.at[3 * g + t],
                                             device_id=(*chip, mc), device_id_type=MESH).wait_recv()
        for cp in sends:
            cp.wait_send()

    outs = pl.pallas_call(
        body, name=name, out_shape=[jax.ShapeDtypeStruct(x.shape, x.dtype) for x in ps],
        in_specs=[_ANY] * ng, out_specs=[_ANY] * ng,
        scratch_shapes=[pltpu.SemaphoreType.DMA((3 * ng,)), pltpu.SemaphoreType.DMA((3 * ng,))],
    )(*ps)
    return _place_own_part(outs, ps)


def _place_own_part(outs, ps):
    chip = 2 * lax.axis_index("x") + lax.axis_index("y")
    return [lax.dynamic_update_slice(o, lax.dynamic_index_in_dim(x, chip, 0, keepdims=True), (chip,) + (0,) * (x.ndim - 1))
            for o, x in zip(outs, ps)]


def _pair_merge_groups(name, fs):
    ng = len(fs)

    def body(*refs):
        xs, outs = refs[:ng], refs[ng:2 * ng]
        send_sems, recv_sems = refs[2 * ng:]
        mx, my, mc = lax.axis_index("x"), lax.axis_index("y"), lax.axis_index("c")
        cps = []
        for g in range(ng):
            mine = _row_half(mc, 2 * fs[g].shape[1])
            cp = pltpu.make_async_remote_copy(src_ref=xs[g], dst_ref=outs[g].at[:, mine], send_sem=send_sems.at[g],
                                              recv_sem=recv_sems.at[g], device_id=(mx, my, 1 - mc), device_id_type=MESH)
            cp.start()
            cps.append(cp)
        for g in range(ng):
            theirs = outs[g].at[:, _row_half(1 - mc, 2 * fs[g].shape[1])]
            pltpu.make_async_remote_copy(src_ref=xs[g], dst_ref=theirs, send_sem=send_sems.at[g],
                                         recv_sem=recv_sems.at[g], device_id=(mx, my, 1 - mc),
                                         device_id_type=MESH).wait_recv()
        for cp in cps:
            cp.wait_send()

    outs = pl.pallas_call(
        body, name=name,
        out_shape=[jax.ShapeDtypeStruct((x.shape[0], 2 * x.shape[1], x.shape[2]), x.dtype) for x in fs],
        in_specs=[_ANY] * ng, out_specs=[_ANY] * ng,
        scratch_shapes=[pltpu.SemaphoreType.DMA((ng,)), pltpu.SemaphoreType.DMA((ng,))],
    )(*fs)
    mc = lax.axis_index("c")
    return [lax.dynamic_update_slice(o, x, (0, mc * x.shape[1], 0)) for o, x in zip(outs, fs)]


def _block_rows(r, w, itemsize=4, budget=4 << 20):
    for c in (r, 2048, 1024, 512, 256, 128, 64, 32, 16):
        if c <= r and r % c == 0 and c * w * itemsize <= budget:
            return c
    return r


def _pair_sum(name, g, got, cidx):
    ns, t, r, w = g.shape
    rh = r // 2
    bm = _block_rows(rh, w)
    nb = rh // bm

    def body(c_ref, a_ref, b_ref, o_ref):
        o_ref[...] = (a_ref[...].astype(F32) + b_ref[...].astype(F32)).astype(o_ref.dtype)

    blk = (None, None, bm, w)
    return pl.pallas_call(
        body, name=name,
        grid_spec=pltpu.PrefetchScalarGridSpec(
            num_scalar_prefetch=1, grid=(ns, t, nb),
            in_specs=[pl.BlockSpec(blk, lambda s, tt, i, c: (s, tt, c[0] * nb + i, 0)),
                      pl.BlockSpec(blk, lambda s, tt, i, c: (s, tt, i, 0))],
            out_specs=pl.BlockSpec(blk, lambda s, tt, i, c: (s, tt, i, 0))),
        out_shape=jax.ShapeDtypeStruct((ns, t, rh, w), BF16),
        compiler_params=_params(3 * _nbytes((bm, w), F32)),
    )(cidx, g, got)


def _chip_sum(name, p):
    ns, th, r, w = p.shape
    bm = _block_rows(r, w, budget=2 << 20)

    def body(p_ref, o_ref):
        acc = p_ref[0].astype(F32)
        for s in range(1, ns):
            acc = acc + p_ref[s].astype(F32)
        o_ref[...] = acc

    return pl.pallas_call(
        body, name=name, grid=(th, r // bm),
        in_specs=[pl.BlockSpec((ns, None, bm, w), lambda tt, i: (0, tt, i, 0))],
        out_specs=pl.BlockSpec((None, bm, w), lambda tt, i: (tt, i, 0)),
        out_shape=jax.ShapeDtypeStruct((th, r, w), F32),
        compiler_params=_params(ns * _nbytes((bm, w), BF16) + 2 * _nbytes((bm, w), F32)),
    )(p)


def _sum_leading(name, x):
    n = x.shape[0]

    def body(p_ref, o_ref):
        acc = p_ref[0]
        for s in range(1, n):
            acc = acc + p_ref[s]
        o_ref[...] = acc

    return pl.pallas_call(body, name=name, out_shape=jax.ShapeDtypeStruct(x.shape[1:], F32),
                          compiler_params=_params(2 * _nbytes(x.shape, F32)))(x)


def _reduce_scatter_begin(tag, gs, overlap):
    cidx = lax.axis_index("c").astype(jnp.int32).reshape(1)
    got = _pair_swap_groups(tag + "_pair_swap", gs)
    pair = [_pair_sum(f"{tag}_pair_sum{i}", g, r_, cidx) for i, (g, r_) in enumerate(zip(gs, got))]
    if overlap:
        return _gather_start(tag + "_scatter_start", pair, scatter=True)
    return _chip_scatter_groups(tag + "_chip_scatter", pair)


def _reduce_scatter_end(tag, state, overlap, after):
    if overlap:
        srcs, lands = _gather_wait(tag + "_scatter_wait", state, after, scatter=True)
        state = _place_own_part(lands, srcs)
    fin = [_chip_sum(f"{tag}_chip_sum{i}", p) for i, p in enumerate(state)]
    return _pair_merge_groups(tag + "_pair_merge", fin)


_GROUPS = ((("ffn1_wg", "ffn1_wu", "ffn2_wg", "ffn2_wu"), 1), (("ffn1_wd", "ffn2_wd"), 0), (("w_a",), 0),
           (("w_o",), 0), (("w_in",), 1), (("w_b",), 1))


def _shard_major(g, ax):
    k, n = g.shape
    if ax == 0:
        return g.reshape(4, k // 4, n)
    return g.reshape(k, 4, n // 4).transpose(1, 0, 2)


def _in_cols(d):
    o1 = 3 * DN_WIDTH
    o2 = o1 + DN_WIDTH
    o3 = o2 + 2 * DN_HEADS
    o4 = o3 + 3 * DA_WIDTH
    return dict(wq=(0, o1), wz=(o1, o2), wba=(o2, o3), wda=(o3, o4), wg=(o4, o4 + 2 * d))


def _mixer_weights(w_in, w_a, w_b, w_o, d):
    w = {k: w_in[:, a:b] for k, (a, b) in _in_cols(d).items()}
    w["wba"] = jnp.pad(w["wba"], ((0, 0), (0, LANES - 2 * DN_HEADS)))
    w["w_a"], w["w_b"], w["w_o"] = w_a, w_b, w_o
    return w


def _w_in_grad(wg):
    return jnp.concatenate([wg["wq"], wg["wz"], wg["wba"][:, :2 * DN_HEADS], wg["wda"], wg["wg"]], axis=1)


def _adam_math(wv, gv, mv, vv):
    mn = ADAM_B1 * mv + (1.0 - ADAM_B1) * gv
    vn = ADAM_B2 * vv + (1.0 - ADAM_B2) * jnp.square(gv)
    m_hat = mn / (1.0 - ADAM_B1 ** ADAM_STEP)
    v_hat = vn / (1.0 - ADAM_B2 ** ADAM_STEP)
    delta = -ADAM_LR * (m_hat / (jnp.sqrt(v_hat) + ADAM_EPS) + ADAM_WD * wv)
    return delta, mn, vn


def _adamw(name, w, g, m, v):
    shape = w.shape
    cols = shape[-1]
    w2, g2, m2, v2 = (t.reshape(-1, cols) for t in (w, g, m, v))
    rows = w2.shape[0]
    bm = _pick(rows, (256, 128, 64, 32, 16, 8)) if rows >= 8 else rows
    delta, mn, vn = _rowwise(name, lambda *t: (_adam_math(*t), ()), [w2, g2, m2, v2], [], [(cols, F32)] * 3, bm=bm)
    return delta.reshape(shape), mn.reshape(shape), vn.reshape(shape)


def _adamw_leading(name, w, g, m, v):
    n = w.shape[0]
    padded_row = -(-w.shape[1] // 8) * 8 * w.shape[2] * 4
    bm = max(c for c in range(1, n + 1) if n % c == 0 and (c * padded_row <= (1 << 20) or c == 1))

    def body(w_ref, g_ref, m_ref, v_ref, d_ref, mo_ref, vo_ref):
        d_ref[...], mo_ref[...], vo_ref[...] = _adam_math(w_ref[...], g_ref[...], m_ref[...], v_ref[...])

    spec = pl.BlockSpec((bm,) + w.shape[1:], lambda i: (i, 0, 0))
    return pl.pallas_call(
        body, name=name, grid=(n // bm,), in_specs=[spec] * 4, out_specs=[spec] * 3,
        out_shape=[jax.ShapeDtypeStruct(w.shape, F32)] * 3, compiler_params=_params(7 * bm * padded_row),
    )(w, g, m, v)


def _adamw_stacked(name, w, m, v, gstacks, slot):
    depth, r, cdim = w.shape
    bm = _block_rows(r, cdim, budget=1 << 20)

    def body(w_ref, m_ref, v_ref, *rest):
        g_refs, (go_ref, d_ref, mo_ref, vo_ref) = rest[:depth], rest[depth:]
        layer = pl.program_id(0)
        gv = g_refs[0][...]
        for l in range(1, depth):
            gv = jnp.where(layer == l, g_refs[l][...], gv)
        go_ref[...] = gv
        d_ref[...], mo_ref[...], vo_ref[...] = _adam_math(w_ref[...], gv, m_ref[...], v_ref[...])

    nat = pl.BlockSpec((None, bm, cdim), lambda l, i: (l, i, 0))
    return pl.pallas_call(
        body, name=name, grid=(depth, r // bm),
        in_specs=[nat, nat, nat] + [pl.BlockSpec((None, bm, cdim), lambda l, i: (slot, i, 0))] * depth,
        out_specs=[nat] * 4, out_shape=[jax.ShapeDtypeStruct(w.shape, F32)] * 4,
        compiler_params=_params((7 + depth) * _nbytes((bm, cdim), F32)),
    )(w, m, v, *gstacks)


def kernel(x, c, ada_w, ada_b, ln_ffn1, ln_mix, ln_ffn2, ffn1_wg, ffn1_wu, ffn1_wd, w_in, conv_w, a_log, dt_bias, dn_norm, w_a, w_b, w_o, ffn2_wg, ffn2_wu, ffn2_wd, final_norm, loss_target, m_ada_w, m_ada_b, m_ln_ffn1, m_ln_mix, m_ln_ffn2, m_ffn1_wg, m_ffn1_wu, m_ffn1_wd, m_w_in, m_conv_w, m_a_log, m_dt_bias, m_dn_norm, m_w_a, m_w_b, m_w_o, m_ffn2_wg, m_ffn2_wu, m_ffn2_wd, m_final_norm, v_ada_w, v_ada_b, v_ln_ffn1, v_ln_mix, v_ln_ffn2, v_ffn1_wg, v_ffn1_wu, v_ffn1_wd, v_w_in, v_conv_w, v_a_log, v_dt_bias, v_dn_norm, v_w_a, v_w_b, v_w_o, v_ffn2_wg, v_ffn2_wu, v_ffn2_wd, v_final_norm):
    names = ["ada_w", "ada_b", "ln_ffn1", "ln_mix", "ln_ffn2", "ffn1_wg", "ffn1_wu", "ffn1_wd", "w_in", "conv_w",
             "a_log", "dt_bias", "dn_norm", "w_a", "w_b", "w_o", "ffn2_wg", "ffn2_wu", "ffn2_wd", "final_norm"]
    wts = dict(zip(names, (ada_w, ada_b, ln_ffn1, ln_mix, ln_ffn2, ffn1_wg, ffn1_wu, ffn1_wd, w_in, conv_w, a_log,
                           dt_bias, dn_norm, w_a, w_b, w_o, ffn2_wg, ffn2_wu, ffn2_wd, final_norm)))
    mom = dict(zip(names, (m_ada_w, m_ada_b, m_ln_ffn1, m_ln_mix, m_ln_ffn2, m_ffn1_wg, m_ffn1_wu, m_ffn1_wd, m_w_in,
                           m_conv_w, m_a_log, m_dt_bias, m_dn_norm, m_w_a, m_w_b, m_w_o, m_ffn2_wg, m_ffn2_wu,
                           m_ffn2_wd, m_final_norm)))
    var = dict(zip(names, (v_ada_w, v_ada_b, v_ln_ffn1, v_ln_mix, v_ln_ffn2, v_ffn1_wg, v_ffn1_wu, v_ffn1_wd, v_w_in,
                           v_conv_w, v_a_log, v_dt_bias, v_dn_norm, v_w_a, v_w_b, v_w_o, v_ffn2_wg, v_ffn2_wu,
                           v_ffn2_wd, v_final_norm)))
    _, s, d = x.shape
    depth = ada_w.shape[0]
    mx, my, mc = lax.axis_index("x"), lax.axis_index("y"), lax.axis_index("c")
    chip = 2 * mx + my
    me = 2 * chip + mc
    nshard = ada_w.shape[2]

    cact = _rowwise("c_silu", lambda cv: ((_silu(cv),), ()), [jnp.pad(c, ((0, 7), (0, 0)))], [], [(d, F32)], bm=8)[0]
    c_all = _allgather8("ag_c", cact)[:, 0, :]
    conv_all = _allgather8("ag_conv", jnp.pad(conv_w.reshape(depth * DN_CONV, -1), ((0, 8 - depth * DN_CONV), (0, 0))))
    conv_full = jnp.concatenate([conv_all[2 * j, :depth * DN_CONV] for j in range(4)], axis=1)
    conv_full = conv_full.reshape(depth, DN_CONV, 3 * DN_WIDTH)
    layer_shards = [[jnp.stack([wts[nm][l].astype(BF16) for nm in nms], axis=0) for nms, _ in _GROUPS]
                    for l in range(depth)]
    gathered0 = _gather_groups("ag_weights0", layer_shards[0])
    rows_of = lambda st: st[:, 0].reshape(-1, st.shape[-1])
    cols_of = lambda st: jnp.concatenate([st[j, 0] for j in range(4)], axis=1)

    def layer_weights(l, after):
        if l == 0:
            got = gathered0
        else:
            srcs, lands = _gather_wait(f"ag_weights{l}_wait", started[l], after)
            got = _pair_forward_groups(f"ag_weights{l}_pair", lands, srcs)
        ga, gb, g_wa, g_wo, g_win, g_wb = got
        return ga, gb, _mixer_weights(cols_of(g_win), rows_of(g_wa), cols_of(g_wb), rows_of(g_wo), d)

    c16 = jnp.pad(c_all, ((0, 8), (0, 0))).astype(BF16)
    parts = []
    for l in range(depth):
        bias = lax.dynamic_slice(ada_b[l], (chip * nshard,), (nshard,)).reshape(1, nshard)
        (mp,) = _matmul(f"ada_fwd{l}", c16, ada_w[l].astype(BF16), epi_bcast=[bias], epi=lambda acc, b: (acc + b,))
        parts.append(mp)
    mod_all = _allgather8("ag_mod", jnp.concatenate(parts, axis=0))
    mod_rows = jnp.concatenate([mod_all[2 * j] for j in range(4)], axis=1)
    mod = jnp.stack([lax.dynamic_index_in_dim(mod_rows, l * 16 + me, axis=0, keepdims=False) for l in range(depth)])

    gathered0, later, mod, conv_full = lax.optimization_barrier((gathered0, layer_shards[1:], mod, conv_full))
    started = {l: _gather_start(f"ag_weights{l}_start", later[l - 1]) for l in range(1, depth)}
    for st in started.values():
        mod = mod + st["token"][0, 0]
    small = dict(conv_w=conv_full, a_log=a_log, dt_bias=dt_bias, dn_norm=dn_norm, ln_ffn1=ln_ffn1, ln_mix=ln_mix,
                 ln_ffn2=ln_ffn2, final_norm=final_norm)
    ffn_names = _GROUPS[0][0] + _GROUPS[1][0]
    rs_state = {}

    def on_layer_grads(l, wg):
        wg["w_in"] = _w_in_grad(wg)
        gs = [jnp.stack([wg[nm] if nm in ffn_names else _shard_major(wg[nm], ax) for nm in nms], axis=1)
              for nms, ax in _GROUPS]
        rs_state[l] = _reduce_scatter_begin(f"rs{l}", gs, overlap=l > 0)
        return rs_state[l]["token"][0, 0] if l > 0 else None

    loss_part, dx, dmod, sgrads, d_fnorm = _local_step(x[0], loss_target[0], mod, layer_weights, small,
                                                       on_layer_grads)
    reduced = [_reduce_scatter_end(f"rs{l}", rs_state[l], l > 0, dx) for l in range(depth)]

    dmod_all = _allgather8("ag_dmod", jnp.pad(dmod, ((0, 8 - depth), (0, 0))))
    g_ada_w, g_ada_b = [], []
    for l in range(depth):
        dm_l = dmod_all[:, l, :]
        (gb_l,) = _rowwise(f"ada_b_grad{l}", lambda v: ((), (jnp.sum(v, axis=0, keepdims=True),)), [dm_l], [], [],
                           [(1, N_ADA * d)], bm=8)
        g_ada_b.append(gb_l[0])
        dm_sh = lax.dynamic_slice(dm_l, (0, chip * nshard), (8, nshard))
        (gw_l,) = _matmul(f"ada_w_grad{l}", c16, jnp.pad(dm_sh, ((0, 8), (0, 0))).astype(BF16), ta=True)
        g_ada_w.append(gw_l)
    grads = dict(ada_w=jnp.stack(g_ada_w), ada_b=jnp.stack(g_ada_b))

    smalls = [loss_part.reshape(1), d_fnorm]
    for l in range(depth):
        sg = sgrads[l]
        smalls += [sg["ln_ffn1"], sg["ln_mix"], sg["ln_ffn2"], sg["a_log"], sg["dt_bias"], sg["dn_norm"],
                   sg["conv_w"].reshape(-1)]
    sizes = [t.shape[0] for t in smalls]
    tile = 8 * LANES
    flat = jnp.concatenate([jnp.pad(t, (0, (-t.shape[0]) % tile)).reshape(-1, LANES) for t in smalls], axis=0)
    tot = _sum_leading("small_sum", _allgather8("ag_small", flat))
    offs, acc = [], 0
    for n_ in sizes:
        offs.append(acc)
        acc += -(-n_ // tile) * 8
    take = lambda i: tot[offs[i]:offs[i] + -(-sizes[i] // tile) * 8].reshape(-1)[:sizes[i]]
    loss = take(0)[0]
    grads["final_norm"] = take(1)
    per = 7
    for key_i, key in enumerate(["ln_ffn1", "ln_mix", "ln_ffn2", "a_log", "dt_bias", "dn_norm"]):
        grads[key] = jnp.stack([take(2 + per * l + key_i) for l in range(depth)])
    conv_g = jnp.stack([take(2 + per * l + 6).reshape(DN_CONV, 3 * DN_WIDTH) for l in range(depth)])
    csh = conv_w.shape[2]
    grads["conv_w"] = lax.dynamic_slice(conv_g, (0, 0, chip * csh), (depth, DN_CONV, csh))

    deltas, new_m, new_v = {}, {}, {}
    for gi, (nms, ax) in enumerate(_GROUPS):
        for q, nm in enumerate(nms):
            wv, mv, vv = wts[nm], mom[nm], var[nm]
            per_layer = [reduced[l][gi][q] for l in range(depth)]
            if ax == 1 and wv.shape[2] % LANES and nm != "w_in":
                tr = lambda t: jnp.swapaxes(t, 1, 2)
                gt = jnp.stack([g.T for g in per_layer], axis=0)
                dl, mn, vn = _adamw("adamw_" + nm, tr(wv), gt, tr(mv), tr(vv))
                grads[nm], deltas[nm], new_m[nm], new_v[nm] = tr(gt), tr(dl), tr(mn), tr(vn)
            elif nm == "w_in" and wv.shape[2] % LANES:
                tr = lambda t: jnp.transpose(t, (2, 0, 1))
                back = lambda t: jnp.transpose(t, (1, 2, 0))
                gt = jnp.stack([g.T for g in per_layer], axis=1)
                dl, mn, vn = _adamw_leading("adamw_" + nm, tr(wv), gt, tr(mv), tr(vv))
                grads[nm], deltas[nm], new_m[nm], new_v[nm] = back(gt), back(dl), back(mn), back(vn)
            else:
                grads[nm], deltas[nm], new_m[nm], new_v[nm] = _adamw_stacked(
                    "adamw_" + nm, wv, mv, vv, [reduced[l][gi] for l in range(depth)], q)

    for name in names:
        if name in deltas:
            continue
        wv, gv, mv, vv = wts[name], grads[name], mom[name], var[name]
        if wv.ndim == 1:
            wv, gv, mv, vv = (t.reshape(-1, LANES) for t in (wv, gv, mv, vv))
        dl, mn, vn = _adamw("adamw_" + name, wv, gv, mv, vv)
        deltas[name], new_m[name], new_v[name] = (t.reshape(wts[name].shape) for t in (dl, mn, vn))
    return (loss, dx.reshape(1, s, d), *[grads[n_] for n_ in names], *[deltas[n_] for n_ in names],
            *[new_m[n_] for n_ in names], *[new_v[n_] for n_ in names])
```

```python
import functools

import jax
import jax.numpy as jnp
from jax import lax
from jax.experimental import pallas as pl
from jax.experimental.pallas import tpu as pltpu

F32 = jnp.float32
BF16 = jnp.bfloat16
MESH = pl.DeviceIdType.MESH

NORM_EPS = 1e-6
DN_HEADS, DN_DIM, DN_CHUNK, DN_CONV = 8, 128, 64, 4
DN_WIDTH = DN_HEADS * DN_DIM
DA_HEADS, DA_DIM, DA_BLOCK = 12, 64, 128
DA_WIDTH = DA_HEADS * DA_DIM
DA_PATTERNS = ((128, 1), (512, 4), (2048, 16))
ALIBI_MAX_EXP = 8.0
N_ADA = 9
LANES = 128
V7X_VMEM_BYTES = 64 << 20
ADAM_LR, ADAM_B1, ADAM_B2, ADAM_EPS, ADAM_WD, ADAM_STEP = 0.001, 0.9, 0.999, 1e-08, 0.01, 10
NEG = -1e30
HI = lax.Precision.HIGHEST
NN = (((1,), (0,)), ((), ()))
NT = (((1,), (1,)), ((), ()))
TN = (((0,), (0,)), ((), ()))


def _nbytes(shape, dtype):
    n = 1
    for s in shape:
        n *= s
    return n * jnp.dtype(dtype).itemsize


def _params(block_bytes, scratch_bytes=0):
    need = 2 * block_bytes + scratch_bytes
    lim = min(max(need + need // 4 + (4 << 20), 32 << 20), V7X_VMEM_BYTES - (6 << 20))
    return pltpu.CompilerParams(vmem_limit_bytes=int(lim))


def _pick(n, cands):
    for c in cands:
        if c <= n and n % c == 0:
            return c
    return n


def _sigmoid(x):
    return jax.nn.sigmoid(x)


def _silu(x):
    return x * jax.nn.sigmoid(x)


def _softplus(x):
    return jnp.maximum(x, 0.0) + jnp.log(1.0 + jnp.exp(-jnp.abs(x)))


def _rowwise(name, fn, rows, bcast, row_outs, red_outs=(), bm=512):
    rows = [r if isinstance(r, tuple) else (r, r.shape[1], 0) for r in rows]
    s = rows[0][0].shape[0]
    bm = _pick(s, (bm, 128, 64, 32, 16, 8))
    nr, nb, no, nd = len(rows), len(bcast), len(row_outs), len(red_outs)
    in_specs = [pl.BlockSpec((bm, w), functools.partial(lambda i, ci: (i, ci), ci=ci)) for (_, w, ci) in rows]
    in_specs += [pl.BlockSpec(b.shape, lambda i: (0, 0)) for b in bcast]
    out_shape = [jax.ShapeDtypeStruct((s, w), dt) for (w, dt) in row_outs]
    out_shape += [jax.ShapeDtypeStruct((r, w), F32) for (r, w) in red_outs]
    out_specs = [pl.BlockSpec((bm, w), lambda i: (i, 0)) for (w, _) in row_outs]
    out_specs += [pl.BlockSpec((r, w), lambda i: (0, 0)) for (r, w) in red_outs]

    def body(*refs):
        ins = [r[...] for r in refs[:nr + nb]]
        outs = refs[nr + nb:nr + nb + no]
        reds = refs[nr + nb + no:]
        ov, rv = fn(*ins)
        for o, v in zip(outs, ov):
            o[...] = v.astype(o.dtype)
        if nd:
            @pl.when(pl.program_id(0) == 0)
            def _():
                for r in reds:
                    r[...] = jnp.zeros(r.shape, F32)
            for r, v in zip(reds, rv):
                r[...] += v.astype(F32)

    blk = sum(_nbytes((bm, w), a.dtype) for (a, w, _) in rows) + sum(_nbytes(b.shape, b.dtype) for b in bcast)
    blk += sum(_nbytes((bm, w), dt) for (w, dt) in row_outs) + sum(_nbytes(r, F32) for r in red_outs)
    res = pl.pallas_call(
        body, name=name, grid=(s // bm,), in_specs=in_specs, out_specs=out_specs, out_shape=out_shape,
        compiler_params=_params(3 * blk),
    )(*[a for (a, _, _) in rows], *bcast)
    return res


def _matmul(name, a, b, *, ta=False, tb=False, outs=(F32,), epi=None, epi_rows=(), epi_bcast=(),
            bm=None, bn=None, bk=None):
    if ta:
        k, m = a.shape
    else:
        m, k = a.shape
    n = b.shape[0] if tb else b.shape[1]
    assert (b.shape[1] if tb else b.shape[0]) == k, (name, a.shape, b.shape)
    if bm is None:
        bm = _pick(m, (1024, 1408, 768, 512, 384, 256, 128)) if ta else _pick(m, (1024, 512, 256, 128, 64, 32, 16))
    if bk is None:
        bk = k if k <= 3072 else _pick(k, (2816, 2048, 1024, 512))
        if ta:
            bk = _pick(k, (1024, 512, 256, 128, 64, 32, 16))
    if bn is None:
        bn = _pick(n, (1024, 768, 512, 384, 256, 128) if bk <= 2048 else (512, 384, 256, 128))
    nk = k // bk
    dims = TN if ta else (NT if tb else NN)
    a_spec = pl.BlockSpec((bk, bm), lambda i, j, kk: (kk, i)) if ta else pl.BlockSpec((bm, bk), lambda i, j, kk: (i, kk))
    b_spec = pl.BlockSpec((bn, bk), lambda i, j, kk: (j, kk)) if tb else pl.BlockSpec((bk, bn), lambda i, j, kk: (kk, j))
    in_specs = [a_spec, b_spec]
    in_specs += [pl.BlockSpec((bm, bn), lambda i, j, kk: (i, j)) for _ in epi_rows]
    in_specs += [pl.BlockSpec((1, bn), lambda i, j, kk: (0, j)) for _ in epi_bcast]
    out_shape = [jax.ShapeDtypeStruct((m, n), dt) for dt in outs]
    out_specs = [pl.BlockSpec((bm, bn), lambda i, j, kk: (i, j)) for _ in outs]
    ner, neb, no = len(epi_rows), len(epi_bcast), len(outs)

    def body(*refs):
        a_ref, b_ref = refs[0], refs[1]
        extra = refs[2:2 + ner + neb]
        out_refs = refs[2 + ner + neb:2 + ner + neb + no]
        prod = lax.dot_general(a_ref[...], b_ref[...], dims, preferred_element_type=F32)

        def finish(acc):
            vals = epi(acc, *[r[...] for r in extra]) if epi is not None else (acc,)
            for o, v in zip(out_refs, vals):
                o[...] = v.astype(o.dtype)

        if nk == 1:
            finish(prod)
        else:
            acc_ref = refs[-1]
            kk = pl.program_id(2)

            @pl.when(kk == 0)
            def _():
                acc_ref[...] = prod

            @pl.when(kk > 0)
            def _():
                acc_ref[...] += prod

            @pl.when(kk == nk - 1)
            def _():
                finish(acc_ref[...])

    blk = _nbytes((bm, bk), a.dtype) + _nbytes((bk, bn), b.dtype)
    blk += sum(_nbytes((bm, bn), r.dtype) for r in epi_rows) + sum(_nbytes((bm, bn), dt) for dt in outs)
    scratch = [pltpu.VMEM((bm, bn), F32)] if nk > 1 else []
    res = pl.pallas_call(
        body, name=name, grid=(m // bm, n // bn, nk), in_specs=in_specs, out_specs=out_specs,
        out_shape=out_shape, scratch_shapes=scratch,
        compiler_params=_params(blk, 3 * _nbytes((bm, bn), F32)),
    )(a, b, *epi_rows, *epi_bcast)
    return res


def _mm_core(name, grid, nk, pairs, out_defs, acc_shape, epi=None, epi_ins=()):
    npair, nep, no = len(pairs), len(epi_ins), len(out_defs)

    def body(*refs):
        extra = refs[2 * npair:2 * npair + nep]
        out_refs = refs[2 * npair + nep:2 * npair + nep + no]
        prod = None
        for p in range(npair):
            d = lax.dot_general(refs[2 * p][...], refs[2 * p + 1][...], pairs[p][4], preferred_element_type=F32)
            prod = d if prod is None else prod + d

        def finish(acc):
            vals = epi(acc, *[r[...] for r in extra]) if epi is not None else (acc,)
            for o, v in zip(out_refs, vals):
                o[...] = v.astype(o.dtype)

        if nk == 1:
            finish(prod)
        else:
            acc_ref = refs[-1]
            kk = pl.program_id(2)

            @pl.when(kk == 0)
            def _():
                acc_ref[...] = prod

            @pl.when(kk > 0)
            def _():
                acc_ref[...] += prod

            @pl.when(kk == nk - 1)
            def _():
                finish(acc_ref[...])

    def blk_bytes(spec, dtype):
        return _nbytes([s for s in spec.block_shape if s is not None], dtype)

    blk = sum(blk_bytes(sa, a.dtype) + blk_bytes(sb, b.dtype) for (a, sa, b, sb, _) in pairs)
    blk += sum(blk_bytes(sp, arr.dtype) for (arr, sp) in epi_ins) + sum(blk_bytes(sp, dt) for (_, dt, sp) in out_defs)
    ins, in_specs = [], []
    for (a, sa, b, sb, _) in pairs:
        ins += [a, b]
        in_specs += [sa, sb]
    ins += [arr for (arr, _) in epi_ins]
    in_specs += [sp for (_, sp) in epi_ins]
    return pl.pallas_call(
        body, name=name, grid=grid, in_specs=in_specs, out_specs=[sp for (_, _, sp) in out_defs],
        out_shape=[jax.ShapeDtypeStruct(sh, dt) for (sh, dt, _) in out_defs],
        scratch_shapes=[pltpu.VMEM(acc_shape, F32)] if nk > 1 else [],
        compiler_params=_params(blk, 3 * _nbytes(acc_shape, F32)),
    )(*ins)


def _rms_mod(h, ln, sh, sc):
    n = h * lax.rsqrt(jnp.mean(h * h, axis=-1, keepdims=True) + NORM_EPS) * ln
    return n * (1.0 + sc) + sh


def _swiglu_act(g, u):
    return _silu(g.astype(F32)) * u.astype(F32)


def _dn_prep(yc, pba, alog, dtb):
    act = _silu(yc)
    parts = []
    for idx in range(2 * DN_HEADS):
        seg = act[:, idx * DN_DIM:(idx + 1) * DN_DIM]
        seg = seg * lax.rsqrt(jnp.sum(seg * seg, axis=-1, keepdims=True) + NORM_EPS)
        if idx < DN_HEADS:
            seg = seg * (DN_DIM ** -0.5)
        parts.append(seg)
    parts.append(act[:, 2 * DN_WIDTH:])
    qkvn = jnp.concatenate(parts, axis=1)
    lane = lax.broadcasted_iota(jnp.int32, pba.shape, 1)
    beta = _sigmoid(pba)
    g = -jnp.exp(alog) * _softplus(pba + dtb)
    gb = jnp.where(lane < DN_HEADS, beta, jnp.where(lane < 2 * DN_HEADS, g, 0.0))
    return qkvn, gb


def _dn_outnorm(o_a, z, dn):
    parts = []
    for h in range(DN_HEADS):
        seg = o_a[:, h * DN_DIM:(h + 1) * DN_DIM]
        seg = seg * lax.rsqrt(jnp.mean(seg * seg, axis=-1, keepdims=True) + NORM_EPS) * dn
        parts.append(seg)
    return jnp.concatenate(parts, axis=1) * _silu(z)


def _shift_down(x, halo8, s):
    r = pltpu.roll(x, s, axis=0)
    top = pltpu.roll(halo8, s, axis=0)
    i8 = lax.broadcasted_iota(jnp.int32, top.shape, 0)
    return jnp.concatenate([jnp.where(i8 < s, top, r[0:8]), r[8:]], axis=0)


def _shift_up(x, halo8, s):
    m = x.shape[0]
    r = pltpu.roll(x, m - s, axis=0)
    bot = pltpu.roll(halo8, 8 - s, axis=0)
    i8 = lax.broadcasted_iota(jnp.int32, bot.shape, 0)
    return jnp.concatenate([r[:m - 8], jnp.where(i8 >= 8 - s, bot, r[m - 8:])], axis=0)


def _conv_prep_fwd(name, pq, convw8, pba, alog, dtb, bm=256):
    s, w = pq.shape
    nblk = s // bm
    hb = bm // 16

    def body(x_ref, halo_ref, w_ref, pba_ref, alog_ref, dtb_ref, yc_ref, qkv_ref, gb_ref):
        i = pl.program_id(0)
        x = x_ref[...].astype(F32)
        halo = jnp.where(i > 0, halo_ref[...].astype(F32)[8:16], 0.0)
        cw = w_ref[...]
        y = x * cw[DN_CONV - 1:DN_CONV]
        for sft in range(1, DN_CONV):
            y = y + _shift_down(x, halo, sft) * cw[DN_CONV - 1 - sft:DN_CONV - sft]
        ycb = y.astype(BF16)
        yc_ref[...] = ycb
        qkvn, gb = _dn_prep(ycb.astype(F32), pba_ref[...], alog_ref[...], dtb_ref[...])
        qkv_ref[...] = qkvn.astype(BF16)
        gb_ref[...] = gb

    blk = 3 * _nbytes((bm, w), BF16) + 4 * _nbytes((bm, w), F32)
    return pl.pallas_call(
        body, name=name, grid=(nblk,),
        in_specs=[pl.BlockSpec((bm, w), lambda i: (i, 0)),
                  pl.BlockSpec((16, w), lambda i: (jnp.maximum(i * hb - 1, 0), 0)),
                  pl.BlockSpec(convw8.shape, lambda i: (0, 0)),
                  pl.BlockSpec((bm, LANES), lambda i: (i, 0)),
                  pl.BlockSpec((1, LANES), lambda i: (0, 0)),
                  pl.BlockSpec((1, LANES), lambda i: (0, 0))],
        out_specs=[pl.BlockSpec((bm, w), lambda i: (i, 0)), pl.BlockSpec((bm, w), lambda i: (i, 0)),
                   pl.BlockSpec((bm, LANES), lambda i: (i, 0))],
        out_shape=[jax.ShapeDtypeStruct((s, w), BF16), jax.ShapeDtypeStruct((s, w), BF16),
                   jax.ShapeDtypeStruct((s, LANES), F32)],
        compiler_params=_params(blk),
    )(pq, pq, convw8, pba, alog, dtb)


def _conv_bwd(name, dyc, pq, convw8, bm=256):
    s, w = pq.shape
    nblk = s // bm
    hb = bm // 16

    def body(dy_ref, dyn_ref, x_ref, xh_ref, w_ref, dx_ref, dw_ref):
        i = pl.program_id(0)
        dy = dy_ref[...].astype(F32)
        nxt = jnp.where(i < nblk - 1, dyn_ref[...].astype(F32)[0:8], 0.0)
        x = x_ref[...].astype(F32)
        halo = jnp.where(i > 0, xh_ref[...].astype(F32)[8:16], 0.0)
        cw = w_ref[...]
        dx = dy * cw[DN_CONV - 1:DN_CONV]
        for sft in range(1, DN_CONV):
            dx = dx + _shift_up(dy, nxt, sft) * cw[DN_CONV - 1 - sft:DN_CONV - sft]
        dx_ref[...] = dx.astype(dx_ref.dtype)
        r8 = lax.broadcasted_iota(jnp.int32, (8, w), 0)
        dw = jnp.zeros((8, w), F32)
        for j in range(DN_CONV):
            sft = DN_CONV - 1 - j
            xs = x if sft == 0 else _shift_down(x, halo, sft)
            dw = dw + jnp.where(r8 == j, jnp.sum(dy * xs, axis=0, keepdims=True), 0.0)

        @pl.when(i == 0)
        def _():
            dw_ref[...] = jnp.zeros((8, w), F32)
        dw_ref[...] += dw

    blk = 4 * _nbytes((bm, w), BF16) + 5 * _nbytes((bm, w), F32)
    return pl.pallas_call(
        body, name=name, grid=(nblk,),
        in_specs=[pl.BlockSpec((bm, w), lambda i: (i, 0)),
                  pl.BlockSpec((16, w), lambda i: (jnp.minimum((i + 1) * hb, s // 16 - 1), 0)),
                  pl.BlockSpec((bm, w), lambda i: (i, 0)),
                  pl.BlockSpec((16, w), lambda i: (jnp.maximum(i * hb - 1, 0), 0)),
                  pl.BlockSpec(convw8.shape, lambda i: (0, 0))],
        out_specs=[pl.BlockSpec((bm, w), lambda i: (i, 0)), pl.BlockSpec((8, w), lambda i: (0, 0))],
        out_shape=[jax.ShapeDtypeStruct((s, w), BF16), jax.ShapeDtypeStruct((8, w), F32)],
        compiler_params=_params(blk),
    )(dyc, dyc, pq, pq, convw8)


BNN = (((2,), (1,)), ((0,), (0,)))
BNT = (((2,), (2,)), ((0,), (0,)))
BTN = (((1,), (1,)), ((0,), (0,)))


def _raw_dot_1pass(a, b, dims):
    return lax.dot_general(a.astype(BF16), b.astype(BF16), dims, preferred_element_type=F32)


def _raw_dot_3pass(a, b, dims):
    ah = a.astype(BF16)
    al = (a - ah.astype(F32)).astype(BF16)
    bh = b.astype(BF16)
    bl = (b - bh.astype(F32)).astype(BF16)
    d = lambda x, y: lax.dot_general(x, y, dims, preferred_element_type=F32)
    return d(ah, bh) + (d(ah, bl) + d(al, bh))


def _with_same_precision_vjp(raw):
    @functools.partial(jax.custom_vjp, nondiff_argnums=(2,))
    def dot(a, b, dims):
        return raw(a, b, dims)

    def fwd(a, b, dims):
        return raw(a, b, dims), (a, b)

    def bwd(dims, res, ct):
        a, b = res
        if dims == BNN:
            return raw(ct, b, BNT), raw(a, ct, BTN)
        if dims == BNT:
            return raw(ct, b, BNN), raw(ct, a, BTN)
        assert dims == BTN
        return raw(b, ct, BNT), raw(a, ct, BNN)

    dot.defvjp(fwd, bwd)
    return dot


_dot_1pass_vjp = _with_same_precision_vjp(_raw_dot_1pass)
_dot_3pass_vjp = _with_same_precision_vjp(_raw_dot_3pass)


def _dot_bf16(a, b, dims=BNN):
    return _dot_1pass_vjp(a, b, dims)


def _dot_3pass(a, b, dims=BNN):
    return _dot_3pass_vjp(a, b, dims)


def _neumann_inverse(x):
    h, c, _ = x.shape
    eye = lax.broadcasted_iota(jnp.int32, (h, c, c), 1) == lax.broadcasted_iota(jnp.int32, (h, c, c), 2)
    t = jnp.where(eye, 1.0, 0.0) + x
    p = x
    for _ in range(5):
        p = _raw_dot_3pass(p, p, BNN)
        t = t + _raw_dot_3pass(t, p, BNN)
    return t


@jax.custom_vjp
def _known_inverse(x, t):
    return t


def _known_inverse_fwd(x, t):
    return t, t


def _known_inverse_bwd(t, ct):
    return _raw_dot_3pass(_raw_dot_3pass(t, ct, BTN), t, BNT), jnp.zeros_like(t)


_known_inverse.defvjp(_known_inverse_fwd, _known_inverse_bwd)


def _delta_chunk(q, k, v, gcol, bcol, state, t_known=None):
    h, c, _ = q.shape
    row = lax.broadcasted_iota(jnp.int32, (h, c, c), 1)
    col = lax.broadcasted_iota(jnp.int32, (h, c, c), 2)
    incl, strict, eye = row >= col, row > col, row == col
    g_b = jnp.broadcast_to(gcol, (h, c, c))
    gc_row = jnp.sum(jnp.where(row <= col, g_b, 0.0), axis=1, keepdims=True)
    g_r = jnp.sum(jnp.where(eye, g_b, 0.0), axis=1, keepdims=True)
    gc_col = jnp.sum(jnp.where(incl, jnp.broadcast_to(g_r, (h, c, c)), 0.0), axis=2, keepdims=True)
    decay = jnp.exp(jnp.where(incl, gc_col - gc_row, NEG))
    kb = k * bcol
    vb = v * bcol
    x = -jnp.where(strict, _dot_bf16(kb, k, BNT) * decay, 0.0)
    t = _neumann_inverse(x) if t_known is None else _known_inverse(x, t_known)
    eg = jnp.exp(gc_col)
    u = _dot_3pass(t, vb)
    w = _dot_3pass(t, kb * eg)
    qk = _dot_bf16(q, k, BNT) * decay
    v_new = u - _dot_bf16(w, state)
    o = _dot_bf16(q * eg, state) + _dot_bf16(qk, v_new)
    g_last = jnp.sum(g_r, axis=2, keepdims=True)
    new_state = state * jnp.exp(g_last) + _dot_bf16(k * jnp.exp(g_last - gc_col), v_new, BTN)
    return o, new_state, t


def _lane_col(blk, idx):
    lane = lax.broadcasted_iota(jnp.int32, blk.shape, 1)
    return jnp.sum(jnp.where(lane == idx, blk, 0.0), axis=1, keepdims=True)


def _dn_heads(ref, base):
    return jnp.stack([ref[:, base + h * DN_DIM:base + (h + 1) * DN_DIM] for h in range(DN_HEADS)], axis=0).astype(F32)


def _dn_cols(gbv, base):
    return jnp.stack([_lane_col(gbv, base + h) for h in range(DN_HEADS)], axis=0)


def _delta_fwd(name, qkvn, gb):
    s = qkvn.shape[0]
    n = s // DN_CHUNK
    c = DN_CHUNK

    def body(qkv_ref, gb_ref, o_ref, st_ref, t_ref, state):
        @pl.when(pl.program_id(0) == 0)
        def _():
            state[...] = jnp.zeros(state.shape, F32)

        gbv = gb_ref[...]
        st = state[...]
        st_ref[0] = st
        o, new, t = _delta_chunk(_dn_heads(qkv_ref, 0), _dn_heads(qkv_ref, DN_WIDTH), _dn_heads(qkv_ref, 2 * DN_WIDTH),
                                 _dn_cols(gbv, DN_HEADS), _dn_cols(gbv, 0), st)
        for h in range(DN_HEADS):
            o_ref[:, h * DN_DIM:(h + 1) * DN_DIM] = o[h]
        t_ref[0] = t
        state[...] = new

    blk = _nbytes((c, 3 * DN_WIDTH), BF16) + _nbytes((c, LANES), F32) + _nbytes((c, DN_WIDTH), F32)
    blk += _nbytes((DN_HEADS, DN_DIM, DN_DIM), F32) + _nbytes((DN_HEADS, c, c), F32)
    return pl.pallas_call(
        body, name=name, grid=(n,),
        in_specs=[pl.BlockSpec((c, 3 * DN_WIDTH), lambda i: (i, 0)), pl.BlockSpec((c, LANES), lambda i: (i, 0))],
        out_specs=[pl.BlockSpec((c, DN_WIDTH), lambda i: (i, 0)),
                   pl.BlockSpec((1, DN_HEADS, DN_DIM, DN_DIM), lambda i: (i, 0, 0, 0)),
                   pl.BlockSpec((1, DN_HEADS, c, c), lambda i: (i, 0, 0, 0))],
        out_shape=[jax.ShapeDtypeStruct((s, DN_WIDTH), F32),
                   jax.ShapeDtypeStruct((n, DN_HEADS, DN_DIM, DN_DIM), F32),
                   jax.ShapeDtypeStruct((n, DN_HEADS, c, c), F32)],
        scratch_shapes=[pltpu.VMEM((DN_HEADS, DN_DIM, DN_DIM), F32)],
        compiler_params=_params(blk, 8 << 20),
    )(qkvn, gb)


def _delta_bwd(name, qkvn, gb, states, tinv, d_o):
    s = qkvn.shape[0]
    n = s // DN_CHUNK
    c = DN_CHUNK

    def body(qkv_ref, gb_ref, st_ref, t_ref, do_ref, dqkv_ref, dgb_ref, dstate):
        @pl.when(pl.program_id(0) == 0)
        def _():
            dstate[...] = jnp.zeros(dstate.shape, F32)

        gbv = gb_ref[...]
        lane = lax.broadcasted_iota(jnp.int32, (c, LANES), 1)
        t_known = t_ref[0]
        chunk = lambda *args: _delta_chunk(*args, t_known=t_known)[:2]
        _, vjp = jax.vjp(chunk, _dn_heads(qkv_ref, 0), _dn_heads(qkv_ref, DN_WIDTH),
                         _dn_heads(qkv_ref, 2 * DN_WIDTH), _dn_cols(gbv, DN_HEADS), _dn_cols(gbv, 0), st_ref[0])
        dq, dk, dv, dg, db, dst = vjp((_dn_heads(do_ref, 0), dstate[...]))
        dgb = jnp.zeros((c, LANES), F32)
        for h in range(DN_HEADS):
            dqkv_ref[:, h * DN_DIM:(h + 1) * DN_DIM] = dq[h]
            dqkv_ref[:, DN_WIDTH + h * DN_DIM:DN_WIDTH + (h + 1) * DN_DIM] = dk[h]
            dqkv_ref[:, 2 * DN_WIDTH + h * DN_DIM:2 * DN_WIDTH + (h + 1) * DN_DIM] = dv[h]
            dgb = dgb + jnp.where(lane == h, db[h], 0.0) + jnp.where(lane == DN_HEADS + h, dg[h], 0.0)
        dstate[...] = dst
        dgb_ref[...] = dgb

    rev = lambda i: (n - 1 - i, 0)
    blk = _nbytes((c, 3 * DN_WIDTH), BF16) + 2 * _nbytes((c, LANES), F32) + _nbytes((c, DN_WIDTH), F32)
    blk += _nbytes((DN_HEADS, DN_DIM, DN_DIM), F32) + _nbytes((c, 3 * DN_WIDTH), F32)
    return pl.pallas_call(
        body, name=name, grid=(n,),
        in_specs=[pl.BlockSpec((c, 3 * DN_WIDTH), rev), pl.BlockSpec((c, LANES), rev),
                  pl.BlockSpec((1, DN_HEADS, DN_DIM, DN_DIM), lambda i: (n - 1 - i, 0, 0, 0)),
                  pl.BlockSpec((1, DN_HEADS, c, c), lambda i: (n - 1 - i, 0, 0, 0)),
                  pl.BlockSpec((c, DN_WIDTH), rev)],
        out_specs=[pl.BlockSpec((c, 3 * DN_WIDTH), rev), pl.BlockSpec((c, LANES), rev)],
        out_shape=[jax.ShapeDtypeStruct((s, 3 * DN_WIDTH), F32), jax.ShapeDtypeStruct((s, LANES), F32)],
        scratch_shapes=[pltpu.VMEM((DN_HEADS, DN_DIM, DN_DIM), F32)],
        compiler_params=_params(blk, 16 << 20),
    )(qkvn, gb, states, tinv, d_o)


def _da_scores(q2f, k2, sub, valid, distf, head):
    lane = lax.broadcasted_iota(jnp.int32, q2f.shape, 1)
    hmask = (lane < DA_DIM) if sub == 0 else (lane >= DA_DIM)
    qm = jnp.where(hmask, q2f, 0.0).astype(BF16)
    slope = 2.0 ** (-ALIBI_MAX_EXP * (head + 1) / DA_HEADS)
    sc = lax.dot_general(qm, k2, NT, preferred_element_type=F32) * (DA_DIM ** -0.5)
    return jnp.where(valid, sc - slope * distf, NEG), qm, hmask


def _da_mask(i, r):
    qi = lax.broadcasted_iota(jnp.int32, (DA_BLOCK, 2 * DA_BLOCK), 0)
    ki = lax.broadcasted_iota(jnp.int32, (DA_BLOCK, 2 * DA_BLOCK), 1)
    dist = qi + DA_BLOCK - ki
    valid = (dist >= 0) & (dist <= DA_BLOCK) & ((ki >= DA_BLOCK) | (i > 0))
    return valid, (dist * r).astype(F32)


def _da_fwd(name, pda, r):
    s = pda.shape[0]
    n = s // r
    nb = n // DA_BLOCK
    w = DA_WIDTH
    dav = pda.reshape(n, r * 3 * w)

    def body(q_ref, kc_ref, kp_ref, vc_ref, vp_ref, o_ref, lse_ref):
        i = pl.program_id(1)
        valid, distf = _da_mask(i, r)
        lane = lax.broadcasted_iota(jnp.int32, (DA_BLOCK, LANES), 1)
        lse = jnp.zeros((DA_BLOCK, LANES), F32)
        for hp in range(DA_HEADS // 2):
            sl = slice(hp * LANES, (hp + 1) * LANES)
            q2f = q_ref[:, sl].astype(F32)
            k2 = jnp.concatenate([kp_ref[:, sl], kc_ref[:, sl]], axis=0)
            v2 = jnp.concatenate([vp_ref[:, sl], vc_ref[:, sl]], axis=0)
            o2 = None
            for sub in range(2):
                head = 2 * hp + sub
                sc, _, hmask = _da_scores(q2f, k2, sub, valid, distf, head)
                mx = jnp.max(sc, axis=1, keepdims=True)
                p = jnp.exp(sc - mx)
                l = jnp.sum(p, axis=1, keepdims=True)
                pv = lax.dot_general(p.astype(BF16), v2, NN, preferred_element_type=F32) / l
                o2 = pv if sub == 0 else jnp.where(hmask, pv, o2)
                lse = jnp.where(lane == head, mx + jnp.log(l), lse)
            o_ref[:, sl] = o2.astype(o_ref.dtype)
        lse_ref[...] = lse

    prev = lambda col: (lambda p, i: (jnp.maximum(i - 1, 0), 3 * p + col))
    cur = lambda col: (lambda p, i: (i, 3 * p + col))
    blk = 5 * _nbytes((DA_BLOCK, w), BF16) + _nbytes((DA_BLOCK, w), F32) + _nbytes((DA_BLOCK, LANES), F32)
    o, lse = pl.pallas_call(
        body, name=name, grid=(r, nb),
        in_specs=[pl.BlockSpec((DA_BLOCK, w), cur(0)), pl.BlockSpec((DA_BLOCK, w), cur(1)),
                  pl.BlockSpec((DA_BLOCK, w), prev(1)), pl.BlockSpec((DA_BLOCK, w), cur(2)),
                  pl.BlockSpec((DA_BLOCK, w), prev(2))],
        out_specs=[pl.BlockSpec((DA_BLOCK, w), lambda p, i: (i, p)),
                   pl.BlockSpec((DA_BLOCK, LANES), lambda p, i: (i, p))],
        out_shape=[jax.ShapeDtypeStruct((n, r * w), BF16), jax.ShapeDtypeStruct((n, r * LANES), F32)],
        compiler_params=_params(blk, 8 << 20),
    )(dav, dav, dav, dav, dav)
    return o.reshape(s, w), lse.reshape(s, LANES)


def _da_bwd(name, pda, d_ob, lse_tot, delta, r):
    s = pda.shape[0]
    n = s // r
    nb = n // DA_BLOCK
    w = DA_WIDTH
    dav = pda.reshape(n, r * 3 * w)
    dov = d_ob.reshape(n, r * w)
    lv = lse_tot.reshape(n, r * LANES)
    dlv = delta.reshape(n, r * LANES)

    def body(q_ref, kc_ref, kp_ref, vc_ref, vp_ref, do_ref, l_ref, dl_ref, dq_ref, dk_ref, dv_ref, ck, cv):
        i = pl.program_id(1)

        @pl.when(i == 0)
        def _():
            ck[...] = jnp.zeros(ck.shape, F32)
            cv[...] = jnp.zeros(cv.shape, F32)

        @pl.when(i < nb)
        def _():
            valid, distf = _da_mask(i, r)
            lsev = l_ref[...]
            dlt = dl_ref[...]
            for hp in range(DA_HEADS // 2):
                sl = slice(hp * LANES, (hp + 1) * LANES)
                q2f = q_ref[:, sl].astype(F32)
                k2 = jnp.concatenate([kp_ref[:, sl], kc_ref[:, sl]], axis=0)
                v2 = jnp.concatenate([vp_ref[:, sl], vc_ref[:, sl]], axis=0)
                do2f = do_ref[:, sl].astype(F32)
                dq2 = jnp.zeros((DA_BLOCK, LANES), F32)
                dk2 = jnp.zeros((2 * DA_BLOCK, LANES), F32)
                dv2 = jnp.zeros((2 * DA_BLOCK, LANES), F32)
                for sub in range(2):
                    head = 2 * hp + sub
                    sc, qm, hmask = _da_scores(q2f, k2, sub, valid, distf, head)
                    p = jnp.exp(sc - _lane_col(lsev, head))
                    dom = jnp.where(hmask, do2f, 0.0).astype(BF16)
                    dp = lax.dot_general(dom, v2, NT, preferred_element_type=F32)
                    ds = (p * (dp - _lane_col(dlt, head)) * (DA_DIM ** -0.5)).astype(BF16)
                    dq2 = dq2 + jnp.where(hmask, lax.dot_general(ds, k2, NN, preferred_element_type=F32), 0.0)
                    dk2 = dk2 + lax.dot_general(ds, qm, TN, preferred_element_type=F32)
                    dv2 = dv2 + lax.dot_general(p.astype(BF16), dom, TN, preferred_element_type=F32)
                dq_ref[:, sl] = dq2.astype(dq_ref.dtype)
                dk_ref[:, sl] = (ck[:, sl] + dk2[:DA_BLOCK]).astype(dk_ref.dtype)
                dv_ref[:, sl] = (cv[:, sl] + dv2[:DA_BLOCK]).astype(dv_ref.dtype)
                ck[:, sl] = dk2[DA_BLOCK:]
                cv[:, sl] = dv2[DA_BLOCK:]

        @pl.when(i == nb)
        def _():
            dk_ref[...] = ck[...].astype(dk_ref.dtype)
            dv_ref[...] = cv[...].astype(dv_ref.dtype)

    qrow = lambda i: jnp.minimum(i, nb - 1)
    prev = lambda col: (lambda p, i: (jnp.maximum(qrow(i) - 1, 0), 3 * p + col))
    cur = lambda col: (lambda p, i: (qrow(i), 3 * p + col))
    same = lambda p, i: (qrow(i), p)
    late = lambda p, i: (jnp.maximum(i - 1, 0), p)
    blk = 6 * _nbytes((DA_BLOCK, w), BF16) + 2 * _nbytes((DA_BLOCK, LANES), F32) + 3 * _nbytes((DA_BLOCK, w), F32)
    dq, dk, dv = pl.pallas_call(
        body, name=name, grid=(r, nb + 1),
        in_specs=[pl.BlockSpec((DA_BLOCK, w), cur(0)), pl.BlockSpec((DA_BLOCK, w), cur(1)),
                  pl.BlockSpec((DA_BLOCK, w), prev(1)), pl.BlockSpec((DA_BLOCK, w), cur(2)),
                  pl.BlockSpec((DA_BLOCK, w), prev(2)), pl.BlockSpec((DA_BLOCK, w), same),
                  pl.BlockSpec((DA_BLOCK, LANES), same), pl.BlockSpec((DA_BLOCK, LANES), same)],
        out_specs=[pl.BlockSpec((DA_BLOCK, w), same), pl.BlockSpec((DA_BLOCK, w), late),
                   pl.BlockSpec((DA_BLOCK, w), late)],
        out_shape=[jax.ShapeDtypeStruct((n, r * w), BF16)] * 3,
        scratch_shapes=[pltpu.VMEM((DA_BLOCK, w), F32), pltpu.VMEM((DA_BLOCK, w), F32)],
        compiler_params=_params(blk, 12 << 20),
    )(dav, dav, dav, dav, dav, dov, lv, dlv)
    return dq.reshape(s, w), dk.reshape(s, w), dv.reshape(s, w)


def _head_expand():
    hrow = lax.broadcasted_iota(jnp.int32, (LANES, DA_WIDTH), 0)
    lcol = lax.broadcasted_iota(jnp.int32, (LANES, DA_WIDTH), 1)
    return jnp.where(lcol // DA_DIM == hrow, 1.0, 0.0).astype(F32)


def _ffn_up(name, a, ga, tg, tu):
    s, d = a.shape
    nsh, _, _, ffs = ga.shape
    bm = _pick(s, (1024, 512, 256, 128))

    def body(a_ref, wg_ref, wu_ref, g_ref, u_ref, f_ref):
        av = a_ref[...]
        g = lax.dot_general(av, wg_ref[...], NN, preferred_element_type=F32)
        u = lax.dot_general(av, wu_ref[...], NN, preferred_element_type=F32)
        g_ref[...] = g.astype(BF16)
        u_ref[...] = u.astype(BF16)
        f_ref[...] = (_silu(g) * u).astype(BF16)

    wspec = lambda t: pl.BlockSpec((None, None, d, ffs), lambda i, j: (j, t, 0, 0))
    ospec = pl.BlockSpec((None, bm, ffs), lambda i, j: (j, i, 0))
    blk = _nbytes((bm, d), BF16) + 2 * _nbytes((d, ffs), BF16) + 3 * _nbytes((bm, ffs), BF16)
    return pl.pallas_call(
        body, name=name, grid=(s // bm, nsh),
        in_specs=[pl.BlockSpec((bm, d), lambda i, j: (i, 0)), wspec(tg), wspec(tu)],
        out_specs=[ospec] * 3, out_shape=[jax.ShapeDtypeStruct((nsh, s, ffs), BF16)] * 3,
        compiler_params=_params(blk, 4 * _nbytes((bm, ffs), F32)),
    )(a, ga, ga)


def _ffn_fwd(tag, h_in, ln, sh, sc, gt, ga, tg, tu, gb, td, weight):
    s, d = h_in.shape
    nsh, _, ffs, _ = gb.shape
    (a,) = _rowwise(tag + "_norm", lambda h, l, s1, s2: ((_rms_mod(h, l, s1, s2),), ()), [h_in], [ln, sh, sc],
                    [(d, BF16)])
    g, u, f = _ffn_up(tag + "_up", a, ga, tg, tu)
    bm, bn = _pick(s, (1024, 512, 256, 128)), _pick(d, (1024, 512, 256, 128))
    io = pl.BlockSpec((bm, bn), lambda i, j, kk: (i, j))
    h_out, o = _mm_core(
        tag + "_down", (s // bm, d // bn, nsh), nsh,
        [(f, pl.BlockSpec((None, bm, ffs), lambda i, j, kk: (kk, i, 0)),
          gb, pl.BlockSpec((None, None, ffs, bn), lambda i, j, kk: (kk, td, 0, j)), NN)],
        [((s, d), F32, io), ((s, d), BF16, io)], (bm, bn),
        epi=lambda acc, h, gv: (h + weight * gv * acc, acc),
        epi_ins=[(h_in, io), (gt, pl.BlockSpec((1, bn), lambda i, j, kk: (0, j)))])
    return h_out, dict(a=a, g=g, u=u, f=f, o=o)


def _resid_bwd(tag, dh_out, o, gt, weight):
    d = dh_out.shape[1]

    def fn(dh, ov, g):
        return (weight * g * dh,), (jnp.sum(weight * dh * ov.astype(F32), axis=0, keepdims=True),)

    do, d_gt = _rowwise(tag + "_resid_bwd", fn, [dh_out, o], [gt], [(d, BF16)], [(1, d)])
    return do, d_gt


def _norm_bwd(tag, h_in, da, dh_out, ln, sh, sc):
    d = h_in.shape[1]

    def fn(h, dav, dh, l, s1, s2):
        _, vjp = jax.vjp(_rms_mod, h, l, s1, s2)
        gh, gl, gs1, gs2 = vjp(dav)
        return (dh + gh,), (gl, gs1, gs2)

    return _rowwise(tag + "_norm_bwd", fn, [h_in, da, dh_out], [ln, sh, sc], [(d, F32)], [(1, d)] * 3)


def _ffn_bwd(tag, h_in, dh_out, sv, ln, sh, sc, gt, ga, tg, tu, gb, td, weight):
    s, d = h_in.shape
    nsh, _, ffs, _ = gb.shape
    bm, bn = _pick(s, (1024, 512, 256, 128)), _pick(d, (1024, 512, 256, 128))
    bk = _pick(s, (1024, 512, 256, 128))
    do, d_gt = _resid_bwd(tag, dh_out, sv["o"], gt, weight)

    def act_bwd(df, g, u):
        _, vjp = jax.vjp(_swiglu_act, g, u)
        return vjp(df)

    hid = pl.BlockSpec((None, bm, ffs), lambda i, j, kk: (j, i, 0))
    dg, du = _mm_core(
        tag + "_down_dx", (s // bm, nsh, 1), 1,
        [(do, pl.BlockSpec((bm, d), lambda i, j, kk: (i, 0)),
          gb, pl.BlockSpec((None, None, ffs, d), lambda i, j, kk: (j, td, 0, 0)), NT)],
        [((nsh, s, ffs), BF16, hid)] * 2, (bm, ffs), epi=act_bwd, epi_ins=[(sv["g"], hid), (sv["u"], hid)])
    (d_wd,) = _mm_core(
        tag + "_down_dw", (nsh, d // bn, s // bk), s // bk,
        [(sv["f"], pl.BlockSpec((None, bk, ffs), lambda i, j, kk: (i, kk, 0)),
          do, pl.BlockSpec((bk, bn), lambda i, j, kk: (kk, j)), TN)],
        [((nsh, ffs, d), BF16, pl.BlockSpec((None, ffs, bn), lambda i, j, kk: (i, 0, j)))], (ffs, bn))
    kmaj = pl.BlockSpec((None, bm, ffs), lambda i, j, kk: (kk, i, 0))
    wsp = lambda t: pl.BlockSpec((None, None, bn, ffs), functools.partial(lambda i, j, kk, t: (kk, t, j, 0), t=t))
    (da,) = _mm_core(
        tag + "_up_dx", (s // bm, d // bn, nsh), nsh, [(dg, kmaj, ga, wsp(tg), NT), (du, kmaj, ga, wsp(tu), NT)],
        [((s, d), F32, pl.BlockSpec((bm, bn), lambda i, j, kk: (i, j)))], (bm, bn))
    dws = []
    for nm, dh in (("_wg_dw", dg), ("_wu_dw", du)):
        (dw,) = _mm_core(
            tag + nm, (1, nsh, s // bk), s // bk,
            [(sv["a"], pl.BlockSpec((bk, d), lambda i, j, kk: (kk, 0)),
              dh, pl.BlockSpec((None, bk, ffs), lambda i, j, kk: (j, kk, 0)), TN)],
            [((nsh, d, ffs), BF16, pl.BlockSpec((None, d, ffs), lambda i, j, kk: (j, 0, 0)))], (d, ffs))
        dws.append(dw)
    dh_in, d_ln, d_sh, d_sc = _norm_bwd(tag, h_in, da, dh_out, ln, sh, sc)
    return dh_in, dict(wg=dws[0], wu=dws[1], wd=d_wd), dict(ln=d_ln, sh=d_sh, sc=d_sc, gt=d_gt)


def _mixer_fwd(tag, h_in, ln, sh, sc, gt, w, sp):
    d = h_in.shape[1]
    (a,) = _rowwise(tag + "_norm", lambda h, l, s1, s2: ((_rms_mod(h, l, s1, s2),), ()), [h_in], [ln, sh, sc],
                    [(d, BF16)])
    (pq,) = _matmul(tag + "_pq", a, w["wq"], outs=(BF16,))
    (pz,) = _matmul(tag + "_pz", a, w["wz"], outs=(BF16,))
    (pba,) = _matmul(tag + "_pba", a, w["wba"])
    (pda,) = _matmul(tag + "_pda", a, w["wda"], outs=(BF16,))
    (pg,) = _matmul(tag + "_pg", a, w["wg"], outs=(BF16,))
    yc, qkvn, gb = _conv_prep_fwd(tag + "_conv", pq, sp["conv8"], pba, sp["alog"], sp["dtb"])
    o_a, states, tinv = _delta_fwd(tag + "_delta", qkvn, gb)
    (o_an,) = _rowwise(tag + "_dnorm", lambda o, z, dn: ((_dn_outnorm(o, z.astype(F32), dn),), ()), [o_a, pz],
                       [sp["dn"]], [(DN_WIDTH, BF16)])
    ops, lses = [], []
    for (_, r) in DA_PATTERNS:
        o_p, lse_p = _da_fwd(f"{tag}_da{r}", pda, r)
        ops.append(o_p)
        lses.append(lse_p)

    def merge(o1, o2, o3, l1, l2, l3):
        mx = jnp.maximum(jnp.maximum(l1, l2), l3)
        e1, e2, e3 = jnp.exp(l1 - mx), jnp.exp(l2 - mx), jnp.exp(l3 - mx)
        tot = e1 + e2 + e3
        ex = _head_expand()
        up = lambda wgt: lax.dot_general(wgt / tot, ex, NN, precision=HI, preferred_element_type=F32)
        return (up(e1) * o1 + up(e2) * o2 + up(e3) * o3, mx + jnp.log(tot)), ()

    o_b, lse_tot = _rowwise(tag + "_merge", merge, ops + lses, [], [(DA_WIDTH, BF16), (LANES, F32)])
    (y_a,) = _matmul(tag + "_wa", o_an, w["w_a"], outs=(BF16,))
    (y_b,) = _matmul(tag + "_wb", o_b, w["w_b"], outs=(BF16,))

    def gate(ga, gbv, ya, yb):
        return _sigmoid(ga.astype(F32)) * ya.astype(F32) + _sigmoid(gbv.astype(F32)) * yb.astype(F32)

    (merged,) = _rowwise(tag + "_gate", lambda *v: ((gate(*v),), ()), [(pg, d, 0), (pg, d, 1), y_a, y_b], [],
                         [(d, BF16)])
    h_out, m = _matmul(tag + "_wo", merged, w["w_o"], outs=(F32, BF16), epi_rows=[h_in], epi_bcast=[gt],
                       epi=lambda acc, h, g: (h + g * acc, acc))
    sv = dict(a=a, pq=pq, pz=pz, pba=pba, pda=pda, pg=pg, yc=yc, qkvn=qkvn, gb=gb, o_a=o_a, states=states, tinv=tinv,
              o_an=o_an, o_b=o_b, lse=lse_tot, y_a=y_a, y_b=y_b, merged=merged, m=m, gate=gate)
    return h_out, sv


def _mixer_bwd(tag, h_in, dh_out, sv, ln, sh, sc, gt, w, sp):
    d = h_in.shape[1]
    dm, d_gt = _resid_bwd(tag, dh_out, sv["m"], gt, 1.0)
    (d_merged,) = _matmul(tag + "_wo_dx", dm, w["w_o"], tb=True, outs=(BF16,))
    (d_wo,) = _matmul(tag + "_wo_dw", sv["merged"], dm, ta=True, outs=(BF16,))
    gate = sv["gate"]

    def gate_bwd(dmg, ga, gbv, ya, yb):
        _, vjp = jax.vjp(gate, ga.astype(F32), gbv.astype(F32), ya.astype(F32), yb.astype(F32))
        dga, dgb, dya, dyb = vjp(dmg.astype(F32))
        return (jnp.concatenate([dga, dgb], axis=1), dya, dyb), ()

    pg = sv["pg"]
    d_pg, d_ya, d_yb = _rowwise(tag + "_gate_bwd", gate_bwd, [d_merged, (pg, d, 0), (pg, d, 1), sv["y_a"], sv["y_b"]],
                                [], [(2 * d, BF16), (d, BF16), (d, BF16)])
    (d_oan,) = _matmul(tag + "_wa_dx", d_ya, w["w_a"], tb=True)
    (d_wa,) = _matmul(tag + "_wa_dw", sv["o_an"], d_ya, ta=True, outs=(BF16,))
    (d_ob,) = _matmul(tag + "_wb_dx", d_yb, w["w_b"], tb=True, outs=(BF16,))
    (d_wb,) = _matmul(tag + "_wb_dw", sv["o_b"], d_yb, ta=True, outs=(BF16,))

    def dnorm_bwd(doan, o, z, dn):
        _, vjp = jax.vjp(_dn_outnorm, o, z.astype(F32), dn)
        go, gz, gdn = vjp(doan)
        return (go, gz), (gdn,)

    d_oa, d_pz, d_dn = _rowwise(tag + "_dnorm_bwd", dnorm_bwd, [d_oan, sv["o_a"], sv["pz"]], [sp["dn"]],
                                [(DN_WIDTH, F32), (DN_WIDTH, BF16)], [(1, DN_DIM)])
    d_qkvn, d_gb = _delta_bwd(tag + "_delta_bwd", sv["qkvn"], sv["gb"], sv["states"], sv["tinv"], d_oa)

    def prep_bwd(dq, dgbv, yc, pba, alog, dtb):
        _, vjp = jax.vjp(_dn_prep, yc.astype(F32), pba, alog, dtb)
        gyc, gpba, galog, gdtb = vjp((dq, dgbv))
        return (gyc, gpba), (galog, gdtb)

    d_yc, d_pba, d_alog, d_dtb = _rowwise(tag + "_prep_bwd", prep_bwd, [d_qkvn, d_gb, sv["yc"], sv["pba"]],
                                          [sp["alog"], sp["dtb"]], [(3 * DN_WIDTH, BF16), (LANES, BF16)],
                                          [(1, LANES), (1, LANES)], bm=128)
    d_pq, d_conv = _conv_bwd(tag + "_conv_bwd", d_yc, sv["pq"], sp["conv8"])

    def delta_fn(dob, ob):
        prod = dob.astype(F32) * ob.astype(F32)
        return (lax.dot_general(prod, _head_expand(), NT, precision=HI, preferred_element_type=F32),), ()

    (delta,) = _rowwise(tag + "_da_delta", delta_fn, [d_ob, sv["o_b"]], [], [(LANES, F32)])
    grads = [_da_bwd(f"{tag}_da{r}_bwd", sv["pda"], d_ob, sv["lse"], delta, r) for (_, r) in DA_PATTERNS]

    def sum3(*parts):
        q1, k1, v1, q2, k2, v2, q3, k3, v3 = (p.astype(F32) for p in parts)
        return (jnp.concatenate([q1 + q2 + q3, k1 + k2 + k3, v1 + v2 + v3], axis=1),), ()

    (d_pda,) = _rowwise(tag + "_da_sum", sum3, [t for g in grads for t in g], [], [(3 * DA_WIDTH, BF16)])

    a = sv["a"]
    (da,) = _matmul(tag + "_pq_dx", d_pq, w["wq"], tb=True)
    add = lambda acc, prev: (acc + prev,)
    (da,) = _matmul(tag + "_pz_dx", d_pz, w["wz"], tb=True, epi_rows=[da], epi=add)
    (da,) = _matmul(tag + "_pba_dx", d_pba, w["wba"], tb=True, epi_rows=[da], epi=add)
    (da,) = _matmul(tag + "_pda_dx", d_pda, w["wda"], tb=True, epi_rows=[da], epi=add)
    (da,) = _matmul(tag + "_pg_dx", d_pg, w["wg"], tb=True, epi_rows=[da], epi=add)
    (d_wq,) = _matmul(tag + "_pq_dw", a, d_pq, ta=True, outs=(BF16,))
    (d_wz,) = _matmul(tag + "_pz_dw", a, d_pz, ta=True, outs=(BF16,))
    (d_wba,) = _matmul(tag + "_pba_dw", a, d_pba, ta=True, outs=(BF16,))
    (d_wda,) = _matmul(tag + "_pda_dw", a, d_pda, ta=True, outs=(BF16,))
    (d_wg,) = _matmul(tag + "_pg_dw", a, d_pg, ta=True, outs=(BF16,))
    dh_in, d_ln, d_sh, d_sc = _norm_bwd(tag, h_in, da, dh_out, ln, sh, sc)
    wgrads = dict(wq=d_wq, wz=d_wz, wba=d_wba, wda=d_wda, wg=d_wg, w_a=d_wa, w_b=d_wb, w_o=d_wo)
    small = dict(ln=d_ln, sh=d_sh, sc=d_sc, gt=d_gt, dn=d_dn, alog=d_alog, dtb=d_dtb, conv=d_conv)
    return dh_in, wgrads, small


def _loss_head(h, target, fnorm):
    d = h.shape[1]

    def fn(hv, tv, fw):
        def lossf(hh, ww):
            y = hh * lax.rsqrt(jnp.mean(hh * hh, axis=-1, keepdims=True) + NORM_EPS) * ww
            return 0.5 * jnp.sum(jnp.mean(jnp.square(y - tv), axis=-1))

        val, (dh, dw) = jax.value_and_grad(lossf, argnums=(0, 1))(hv, fw)
        return (dh,), (jnp.full((1, LANES), val, F32), dw)

    return _rowwise("loss_head", fn, [h, target], [fnorm], [(d, F32)], [(1, LANES), (1, d)])


def _row(v):
    return v.reshape(1, -1)


def _pad_lanes(v, offset):
    return jnp.pad(v.reshape(1, -1), ((0, 0), (offset, LANES - offset - v.shape[0])))


_UP_SLOTS = dict(ffn1_wg=0, ffn1_wu=1, ffn2_wg=2, ffn2_wu=3)
_DOWN_SLOTS = dict(ffn1_wd=0, ffn2_wd=1)


def _local_step(x2, target, mod, layer_weights, small, on_layer_grads):
    depth = mod.shape[0]
    d = x2.shape[1]
    h = x2
    saved = []
    mods = []
    up = lambda l, nm: _UP_SLOTS[nm]
    down = lambda l, nm: _DOWN_SLOTS[nm]
    for l in range(depth):
        m9 = [_row(mod[l, i * d:(i + 1) * d]) for i in range(N_ADA)]
        sp = dict(conv8=jnp.pad(small["conv_w"][l], ((0, 8 - DN_CONV), (0, 0))),
                  alog=_pad_lanes(small["a_log"][l], DN_HEADS), dtb=_pad_lanes(small["dt_bias"][l], DN_HEADS),
                  dn=_row(small["dn_norm"][l]))
        ga, gb, w = layer_weights(l, h)
        h0 = h
        h1, sv1 = _ffn_fwd(f"l{l}_ffn1", h0, _row(small["ln_ffn1"][l]), m9[0], m9[1], m9[2], ga, up(l, "ffn1_wg"),
                           up(l, "ffn1_wu"), gb, down(l, "ffn1_wd"), 0.5)
        h2, sv2 = _mixer_fwd(f"l{l}_mix", h1, _row(small["ln_mix"][l]), m9[3], m9[4], m9[5], w, sp)
        h3, sv3 = _ffn_fwd(f"l{l}_ffn2", h2, _row(small["ln_ffn2"][l]), m9[6], m9[7], m9[8], ga, up(l, "ffn2_wg"),
                           up(l, "ffn2_wu"), gb, down(l, "ffn2_wd"), 0.5)
        saved.append((h0, h1, h2, sv1, sv2, sv3, sp, ga, gb, w))
        mods.append(m9)
        h = h3
    dh, loss_part, d_fnorm = _loss_head(h, target, _row(small["final_norm"]))
    sgrads, dmods = [], []
    token = None
    for l in reversed(range(depth)):
        h0, h1, h2, sv1, sv2, sv3, sp, ga, gb, w = saved[l]
        m9 = mods[l] if token is None else [r + token for r in mods[l]]
        dh, g3, s3 = _ffn_bwd(f"l{l}_ffn2", h2, dh, sv3, _row(small["ln_ffn2"][l]), m9[6], m9[7], m9[8], ga,
                              up(l, "ffn2_wg"), up(l, "ffn2_wu"), gb, down(l, "ffn2_wd"), 0.5)
        dh, g2, s2 = _mixer_bwd(f"l{l}_mix", h1, dh, sv2, _row(small["ln_mix"][l]), m9[3], m9[4], m9[5], w, sp)
        dh, g1, s1 = _ffn_bwd(f"l{l}_ffn1", h0, dh, sv1, _row(small["ln_ffn1"][l]), m9[0], m9[1], m9[2], ga,
                              up(l, "ffn1_wg"), up(l, "ffn1_wu"), gb, down(l, "ffn1_wd"), 0.5)
        token = on_layer_grads(l, dict(ffn1_wg=g1["wg"], ffn1_wu=g1["wu"], ffn1_wd=g1["wd"], ffn2_wg=g3["wg"],
                                       ffn2_wu=g3["wu"], ffn2_wd=g3["wd"], **g2))
        dmods.append(jnp.concatenate([s1["sh"], s1["sc"], s1["gt"], s2["sh"], s2["sc"], s2["gt"],
                                      s3["sh"], s3["sc"], s3["gt"]], axis=1))
        sgrads.append(dict(ln_ffn1=s1["ln"][0], ln_mix=s2["ln"][0], ln_ffn2=s3["ln"][0],
                           a_log=s2["alog"][0, DN_HEADS:2 * DN_HEADS], dt_bias=s2["dtb"][0, DN_HEADS:2 * DN_HEADS],
                           dn_norm=s2["dn"][0], conv_w=s2["conv"][:DN_CONV]))
    sgrads.reverse()
    dmods.reverse()
    return loss_part[0, 0], dh, jnp.concatenate(dmods, axis=0), sgrads, d_fnorm[0]


def _flip(v, bit):
    return 1 - v if bit else v


def _allgather8(name, x):
    r, c = x.shape

    def body(x_ref, out_ref, send_sems, recv_sems, local_sem):
        mx, my, mc = lax.axis_index("x"), lax.axis_index("y"), lax.axis_index("c")
        me = 4 * mx + 2 * my + mc
        mine = pltpu.make_async_copy(x_ref, out_ref.at[me], local_sem)
        mine.start()
        sends = []
        for k in range(1, 8):
            peer = (_flip(mx, k & 4), _flip(my, k & 2), _flip(mc, k & 1))
            cp = pltpu.make_async_remote_copy(src_ref=x_ref, dst_ref=out_ref.at[me], send_sem=send_sems.at[k - 1],
                                              recv_sem=recv_sems.at[k - 1], device_id=peer, device_id_type=MESH)
            cp.start()
            sends.append(cp)
        for k in range(1, 8):
            peer = (_flip(mx, k & 4), _flip(my, k & 2), _flip(mc, k & 1))
            src = 4 * peer[0] + 2 * peer[1] + peer[2]
            pltpu.make_async_remote_copy(src_ref=x_ref, dst_ref=out_ref.at[src], send_sem=send_sems.at[k - 1],
                                         recv_sem=recv_sems.at[k - 1], device_id=peer, device_id_type=MESH).wait_recv()
        for cp in sends:
            cp.wait_send()
        mine.wait()

    return pl.pallas_call(
        body, name=name, out_shape=jax.ShapeDtypeStruct((8, r, c), x.dtype),
        in_specs=[pl.BlockSpec(memory_space=pltpu.VMEM)], out_specs=pl.BlockSpec(memory_space=pltpu.VMEM),
        scratch_shapes=[pltpu.SemaphoreType.DMA((7,)), pltpu.SemaphoreType.DMA((7,)), pltpu.SemaphoreType.DMA],
        compiler_params=_params(9 * _nbytes((r, c), x.dtype)),
    )(x)


def _chip_peers(mx, my):
    chips = [(1 - mx, my), (mx, 1 - my), (1 - mx, 1 - my)]
    return chips, [2 * cx + cy for (cx, cy) in chips]


_ANY = pl.BlockSpec(memory_space=pl.ANY)


def _row_half(mc, r):
    return pl.ds(pl.multiple_of(mc * (r // 2), 16), r // 2)


def _gather_groups(name, shards):
    ng = len(shards)

    def body(*refs):
        xs, outs = refs[:ng], refs[ng:2 * ng]
        send_sems, recv_sems = refs[2 * ng:]
        mx, my, mc = lax.axis_index("x"), lax.axis_index("y"), lax.axis_index("c")
        j = 2 * mx + my
        chips, idxs = _chip_peers(mx, my)
        sib = (mx, my, 1 - mc)

        def copy(k, src, dst, to):
            return pltpu.make_async_remote_copy(src_ref=src, dst_ref=dst, send_sem=send_sems.at[k],
                                                recv_sem=recv_sems.at[k], device_id=to, device_id_type=MESH)

        first, passed = [], []
        for g in range(ng):
            mine = _row_half(mc, shards[g].shape[1])
            for t, chip in enumerate(chips):
                cp = copy(6 * g + t, xs[g].at[:, mine], outs[g].at[j, :, mine], (*chip, mc))
                cp.start()
                first.append(cp)
        for g in range(ng):
            mine = _row_half(mc, shards[g].shape[1])
            for t, chip in enumerate(chips):
                landed = outs[g].at[idxs[t], :, mine]
                copy(6 * g + t, landed, landed, (*chip, mc)).wait_recv()
                fwd = copy(6 * g + 3 + t, landed, landed, sib)
                fwd.start()
                passed.append(fwd)
        for g in range(ng):
            theirs_half = _row_half(1 - mc, shards[g].shape[1])
            for t in range(3):
                theirs = outs[g].at[idxs[t], :, theirs_half]
                copy(6 * g + 3 + t, theirs, theirs, sib).wait_recv()
        for cp in first + passed:
            cp.wait_send()

    outs = pl.pallas_call(
        body, name=name, out_shape=[jax.ShapeDtypeStruct((4,) + x.shape, x.dtype) for x in shards],
        in_specs=[_ANY] * ng, out_specs=[_ANY] * ng,
        scratch_shapes=[pltpu.SemaphoreType.DMA((6 * ng,)), pltpu.SemaphoreType.DMA((6 * ng,))],
    )(*shards)
    return _place_own_slab(outs, shards)


def _place_own_slab(outs, shards):
    chip = 2 * lax.axis_index("x") + lax.axis_index("y")
    return [lax.dynamic_update_slice(o, x[None], (chip,) + (0,) * x.ndim) for o, x in zip(outs, shards)]


_HBM = pl.BlockSpec(memory_space=pltpu.HBM)
_SEM = pl.BlockSpec(memory_space=pltpu.SEMAPHORE)
_DATAFLOW = pltpu.SideEffectType.DATAFLOW_SIDE_EFFECTING


def _ici_gather_copies(src_refs, land_refs, send_sems, recv_sems, scatter=False):
    mx, my, mc = lax.axis_index("x"), lax.axis_index("y"), lax.axis_index("c")
    j = 2 * mx + my
    chips, idxs = _chip_peers(mx, my)
    sends, recvs = [], []
    for g, src in enumerate(src_refs):
        for t, chip in enumerate(chips):
            common = dict(send_sem=send_sems.at[3 * g + t], recv_sem=recv_sems.at[3 * g + t], device_id=(*chip, mc),
                          device_id_type=MESH)
            if scatter:
                out, to, frm = src.at[idxs[t]], land_refs[g].at[j], land_refs[g].at[idxs[t]]
            else:
                mine = _row_half(mc, src.shape[1])
                out, to, frm = src.at[:, mine], land_refs[g].at[j, :, mine], land_refs[g].at[idxs[t], :, mine]
            sends.append(pltpu.make_async_remote_copy(src_ref=out, dst_ref=to, **common))
            recvs.append(pltpu.make_async_remote_copy(src_ref=out, dst_ref=frm, **common))
    return sends, recvs


def _gather_start(name, shards, scatter=False):
    ng = len(shards)

    def body(*refs):
        srcs, lands = refs[:ng], refs[ng:2 * ng]
        send_sems, recv_sems = refs[2 * ng], refs[2 * ng + 1]
        token = refs[-1]
        sends, _ = _ici_gather_copies(srcs, lands, send_sems, recv_sems, scatter)
        for cp in sends:
            cp.start()
        token[...] = jnp.zeros(token.shape, token.dtype)

    land_shape = lambda x: x.shape if scatter else (4,) + x.shape
    srcs = [pltpu.with_memory_space_constraint(x, pltpu.HBM) for x in shards]
    lands = [pltpu.with_memory_space_constraint(lax.empty(land_shape(x), x.dtype), pltpu.HBM) for x in shards]
    res = pl.pallas_call(
        body, name=name,
        out_shape=(pltpu.SemaphoreType.DMA((3 * ng,)), pltpu.SemaphoreType.DMA((3 * ng,)),
                   *[pltpu.HBM(x.shape, x.dtype) for x in srcs], *[pltpu.HBM(x.shape, x.dtype) for x in lands],
                   jax.ShapeDtypeStruct((8, LANES), F32)),
        in_specs=[_HBM] * (2 * ng),
        out_specs=(_SEM, _SEM, *[_HBM] * (2 * ng), pl.BlockSpec(memory_space=pltpu.VMEM)),
        input_output_aliases={i: 2 + i for i in range(2 * ng)},
        compiler_params=pltpu.CompilerParams(has_side_effects=_DATAFLOW),
    )(*srcs, *lands)
    return dict(send_sems=res[0], recv_sems=res[1], srcs=list(res[2:2 + ng]), lands=list(res[2 + ng:2 + 2 * ng]),
                token=res[-1])


def _gather_wait(name, started, after, scatter=False):
    ng = len(started["srcs"])

    def body(*refs):
        srcs, lands = refs[:ng], refs[ng:2 * ng]
        send_sems, recv_sems = refs[2 * ng], refs[2 * ng + 1]
        sends, recvs = _ici_gather_copies(srcs, lands, send_sems, recv_sems, scatter)
        for cp in sends:
            cp.wait_send()
        for cp in recvs:
            cp.wait_recv()

    res = pl.pallas_call(
        body, name=name,
        out_shape=[pltpu.HBM(x.shape, x.dtype) for x in started["srcs"] + started["lands"]],
        in_specs=[_HBM] * (2 * ng) + [_SEM, _SEM, _ANY], out_specs=[_HBM] * (2 * ng),
        input_output_aliases={i: i for i in range(2 * ng)},
        compiler_params=pltpu.CompilerParams(has_side_effects=_DATAFLOW),
    )(*started["srcs"], *started["lands"], started["send_sems"], started["recv_sems"], after)
    return list(res[:ng]), list(res[ng:])


def _pair_forward_groups(name, lands, shards):
    ng = len(lands)

    def body(*refs):
        ins, outs = refs[:ng], refs[ng:2 * ng]
        send_sems, recv_sems = refs[2 * ng:]
        mx, my, mc = lax.axis_index("x"), lax.axis_index("y"), lax.axis_index("c")
        _, idxs = _chip_peers(mx, my)
        sib = (mx, my, 1 - mc)
        cps = []
        for g in range(ng):
            mine = _row_half(mc, lands[g].shape[2])
            for t in range(3):
                cp = pltpu.make_async_remote_copy(src_ref=ins[g].at[idxs[t], :, mine], dst_ref=outs[g].at[idxs[t], :, mine],
                                                  send_sem=send_sems.at[3 * g + t], recv_sem=recv_sems.at[3 * g + t],
                                                  device_id=sib, device_id_type=MESH)
                cp.start()
                cps.append(cp)
        for g in range(ng):
            theirs = _row_half(1 - mc, lands[g].shape[2])
            for t in range(3):
                pltpu.make_async_remote_copy(src_ref=ins[g].at[idxs[t], :, theirs], dst_ref=outs[g].at[idxs[t], :, theirs],
                                             send_sem=send_sems.at[3 * g + t], recv_sem=recv_sems.at[3 * g + t],
                                             device_id=sib, device_id_type=MESH).wait_recv()
        for cp in cps:
            cp.wait_send()

    outs = pl.pallas_call(
        body, name=name, out_shape=[jax.ShapeDtypeStruct(x.shape, x.dtype) for x in lands],
        in_specs=[_ANY] * ng, out_specs=[_ANY] * ng, input_output_aliases={i: i for i in range(ng)},
        scratch_shapes=[pltpu.SemaphoreType.DMA((3 * ng,)), pltpu.SemaphoreType.DMA((3 * ng,))],
    )(*lands)
    return _place_own_slab(outs, shards)


def _pair_swap_groups(name, gs):
    ng = len(gs)

    def body(*refs):
        xs, outs = refs[:ng], refs[ng:2 * ng]
        send_sems, recv_sems = refs[2 * ng:]
        mx, my, mc = lax.axis_index("x"), lax.axis_index("y"), lax.axis_index("c")
        cps = []
        for g in range(ng):
            cp = pltpu.make_async_remote_copy(src_ref=xs[g].at[:, :, _row_half(1 - mc, gs[g].shape[2])], dst_ref=outs[g],
                                              send_sem=send_sems.at[g], recv_sem=recv_sems.at[g],
                                              device_id=(mx, my, 1 - mc), device_id_type=MESH)
            cp.start()
            cps.append(cp)
        for cp in cps:
            cp.wait()

    return pl.pallas_call(
        body, name=name,
        out_shape=[jax.ShapeDtypeStruct(x.shape[:2] + (x.shape[2] // 2, x.shape[3]), x.dtype) for x in gs],
        in_specs=[_ANY] * ng, out_specs=[_ANY] * ng,
        scratch_shapes=[pltpu.SemaphoreType.DMA((ng,)), pltpu.SemaphoreType.DMA((ng,))],
    )(*gs)


def _chip_scatter_groups(name, ps):
    ng = len(ps)

    def body(*refs):
        xs, outs = refs[:ng], refs[ng:2 * ng]
        send_sems, recv_sems = refs[2 * ng:]
        mx, my, mc = lax.axis_index("x"), lax.axis_index("y"), lax.axis_index("c")
        j = 2 * mx + my
        chips, idxs = _chip_peers(mx, my)
        sends = []
        for g in range(ng):
            for t, chip in enumerate(chips):
                cp = pltpu.make_async_remote_copy(src_ref=xs[g].at[idxs[t]], dst_ref=outs[g].at[j],
                                                  send_sem=send_sems.at[3 * g + t], recv_sem=recv_sems.at[3 * g + t],
                                                  device_id=(*chip, mc), device_id_type=MESH)
                cp.start()
                sends.append(cp)
        for g in range(ng):
            for t, chip in enumerate(chips):
                pltpu.make_async_remote_copy(src_ref=xs[g].at[idxs[t]], dst_ref=outs[g].at[idxs[t]],
                                             send_sem=send_sems.at[3 * g + t], recv_sem=recv_sems.at[3 * g + t],
                                             device_id=(*chip, mc), device_id_type=MESH).wait_recv()
        for cp in sends:
            cp.wait_send()

    outs = pl.pallas_call(
        body, name=name, out_shape=[jax.ShapeDtypeStruct(x.shape, x.dtype) for x in ps],
        in_specs=[_ANY] * ng, out_specs=[_ANY] * ng,
        scratch_shapes=[pltpu.SemaphoreType.DMA((3 * ng,)), pltpu.SemaphoreType.DMA((3 * ng,))],
    )(*ps)
    return _place_own_part(outs, ps)


def _place_own_part(outs, ps):
    chip = 2 * lax.axis_index("x") + lax.axis_index("y")
    return [lax.dynamic_update_slice(o, lax.dynamic_index_in_dim(x, chip, 0, keepdims=True), (chip,) + (0,) * (x.ndim - 1))
            for o, x in zip(outs, ps)]


def _pair_merge_groups(name, fs):
    ng = len(fs)

    def body(*refs):
        xs, outs = refs[:ng], refs[ng:2 * ng]
        send_sems, recv_sems = refs[2 * ng:]
        mx, my, mc = lax.axis_index("x"), lax.axis_index("y"), lax.axis_index("c")
        cps = []
        for g in range(ng):
            mine = _row_half(mc, 2 * fs[g].shape[1])
            cp = pltpu.make_async_remote_copy(src_ref=xs[g], dst_ref=outs[g].at[:, mine], send_sem=send_sems.at[g],
                                              recv_sem=recv_sems.at[g], device_id=(mx, my, 1 - mc), device_id_type=MESH)
            cp.start()
            cps.append(cp)
        for g in range(ng):
            theirs = outs[g].at[:, _row_half(1 - mc, 2 * fs[g].shape[1])]
            pltpu.make_async_remote_copy(src_ref=xs[g], dst_ref=theirs, send_sem=send_sems.at[g],
                                         recv_sem=recv_sems.at[g], device_id=(mx, my, 1 - mc),
                                         device_id_type=MESH).wait_recv()
        for cp in cps:
            cp.wait_send()

    outs = pl.pallas_call(
        body, name=name,
        out_shape=[jax.ShapeDtypeStruct((x.shape[0], 2 * x.shape[1], x.shape[2]), x.dtype) for x in fs],
        in_specs=[_ANY] * ng, out_specs=[_ANY] * ng,
        scratch_shapes=[pltpu.SemaphoreType.DMA((ng,)), pltpu.SemaphoreType.DMA((ng,))],
    )(*fs)
    mc = lax.axis_index("c")
    return [lax.dynamic_update_slice(o, x, (0, mc * x.shape[1], 0)) for o, x in zip(outs, fs)]


def _block_rows(r, w, itemsize=4, budget=4 << 20):
    for c in (r, 2048, 1024, 512, 256, 128, 64, 32, 16):
        if c <= r and r % c == 0 and c * w * itemsize <= budget:
            return c
    return r


def _pair_sum(name, g, got, cidx):
    ns, t, r, w = g.shape
    rh = r // 2
    bm = _block_rows(rh, w)
    nb = rh // bm

    def body(c_ref, a_ref, b_ref, o_ref):
        o_ref[...] = (a_ref[...].astype(F32) + b_ref[...].astype(F32)).astype(o_ref.dtype)

    blk = (None, None, bm, w)
    return pl.pallas_call(
        body, name=name,
        grid_spec=pltpu.PrefetchScalarGridSpec(
            num_scalar_prefetch=1, grid=(ns, t, nb),
            in_specs=[pl.BlockSpec(blk, lambda s, tt, i, c: (s, tt, c[0] * nb + i, 0)),
                      pl.BlockSpec(blk, lambda s, tt, i, c: (s, tt, i, 0))],
            out_specs=pl.BlockSpec(blk, lambda s, tt, i, c: (s, tt, i, 0))),
        out_shape=jax.ShapeDtypeStruct((ns, t, rh, w), BF16),
        compiler_params=_params(3 * _nbytes((bm, w), F32)),
    )(cidx, g, got)


def _chip_sum(name, p):
    ns, th, r, w = p.shape
    bm = _block_rows(r, w, budget=2 << 20)

    def body(p_ref, o_ref):
        acc = p_ref[0].astype(F32)
        for s in range(1, ns):
            acc = acc + p_ref[s].astype(F32)
        o_ref[...] = acc

    return pl.pallas_call(
        body, name=name, grid=(th, r // bm),
        in_specs=[pl.BlockSpec((ns, None, bm, w), lambda tt, i: (0, tt, i, 0))],
        out_specs=pl.BlockSpec((None, bm, w), lambda tt, i: (tt, i, 0)),
        out_shape=jax.ShapeDtypeStruct((th, r, w), F32),
        compiler_params=_params(ns * _nbytes((bm, w), BF16) + 2 * _nbytes((bm, w), F32)),
    )(p)


def _sum_leading(name, x):
    n = x.shape[0]

    def body(p_ref, o_ref):
        acc = p_ref[0]
        for s in range(1, n):
            acc = acc + p_ref[s]
        o_ref[...] = acc

    return pl.pallas_call(body, name=name, out_shape=jax.ShapeDtypeStruct(x.shape[1:], F32),
                          compiler_params=_params(2 * _nbytes(x.shape, F32)))(x)


def _reduce_scatter_begin(tag, gs, overlap):
    cidx = lax.axis_index("c").astype(jnp.int32).reshape(1)
    got = _pair_swap_groups(tag + "_pair_swap", gs)
    pair = [_pair_sum(f"{tag}_pair_sum{i}", g, r_, cidx) for i, (g, r_) in enumerate(zip(gs, got))]
    if overlap:
        return _gather_start(tag + "_scatter_start", pair, scatter=True)
    return _chip_scatter_groups(tag + "_chip_scatter", pair)


def _reduce_scatter_end(tag, state, overlap, after):
    if overlap:
        srcs, lands = _gather_wait(tag + "_scatter_wait", state, after, scatter=True)
        state = _place_own_part(lands, srcs)
    fin = [_chip_sum(f"{tag}_chip_sum{i}", p) for i, p in enumerate(state)]
    return _pair_merge_groups(tag + "_pair_merge", fin)


_GROUPS = ((("ffn1_wg", "ffn1_wu", "ffn2_wg", "ffn2_wu"), 1), (("ffn1_wd", "ffn2_wd"), 0), (("w_a",), 0),
           (("w_o",), 0), (("w_in",), 1), (("w_b",), 1))


def _shard_major(g, ax):
    k, n = g.shape
    if ax == 0:
        return g.reshape(4, k // 4, n)
    return g.reshape(k, 4, n // 4).transpose(1, 0, 2)


def _in_cols(d):
    o1 = 3 * DN_WIDTH
    o2 = o1 + DN_WIDTH
    o3 = o2 + 2 * DN_HEADS
    o4 = o3 + 3 * DA_WIDTH
    return dict(wq=(0, o1), wz=(o1, o2), wba=(o2, o3), wda=(o3, o4), wg=(o4, o4 + 2 * d))


def _mixer_weights(w_in, w_a, w_b, w_o, d):
    w = {k: w_in[:, a:b] for k, (a, b) in _in_cols(d).items()}
    w["wba"] = jnp.pad(w["wba"], ((0, 0), (0, LANES - 2 * DN_HEADS)))
    w["w_a"], w["w_b"], w["w_o"] = w_a, w_b, w_o
    return w


def _w_in_grad(wg):
    return jnp.concatenate([wg["wq"], wg["wz"], wg["wba"][:, :2 * DN_HEADS], wg["wda"], wg["wg"]], axis=1)


def _adam_math(wv, gv, mv, vv):
    mn = ADAM_B1 * mv + (1.0 - ADAM_B1) * gv
    vn = ADAM_B2 * vv + (1.0 - ADAM_B2) * jnp.square(gv)
    m_hat = mn / (1.0 - ADAM_B1 ** ADAM_STEP)
    v_hat = vn / (1.0 - ADAM_B2 ** ADAM_STEP)
    delta = -ADAM_LR * (m_hat / (jnp.sqrt(v_hat) + ADAM_EPS) + ADAM_WD * wv)
    return delta, mn, vn


def _adamw(name, w, g, m, v):
    shape = w.shape
    cols = shape[-1]
    w2, g2, m2, v2 = (t.reshape(-1, cols) for t in (w, g, m, v))
    rows = w2.shape[0]
    bm = _pick(rows, (256, 128, 64, 32, 16, 8)) if rows >= 8 else rows
    delta, mn, vn = _rowwise(name, lambda *t: (_adam_math(*t), ()), [w2, g2, m2, v2], [], [(cols, F32)] * 3, bm=bm)
    return delta.reshape(shape), mn.reshape(shape), vn.reshape(shape)


def _adamw_leading(name, w, g, m, v):
    n = w.shape[0]
    padded_row = -(-w.shape[1] // 8) * 8 * w.shape[2] * 4
    bm = max(c for c in range(1, n + 1) if n % c == 0 and (c * padded_row <= (1 << 20) or c == 1))

    def body(w_ref, g_ref, m_ref, v_ref, d_ref, mo_ref, vo_ref):
        d_ref[...], mo_ref[...], vo_ref[...] = _adam_math(w_ref[...], g_ref[...], m_ref[...], v_ref[...])

    spec = pl.BlockSpec((bm,) + w.shape[1:], lambda i: (i, 0, 0))
    return pl.pallas_call(
        body, name=name, grid=(n // bm,), in_specs=[spec] * 4, out_specs=[spec] * 3,
        out_shape=[jax.ShapeDtypeStruct(w.shape, F32)] * 3, compiler_params=_params(7 * bm * padded_row),
    )(w, g, m, v)


def _adamw_stacked(name, w, m, v, gstacks, slot):
    depth, r, cdim = w.shape
    bm = _block_rows(r, cdim, budget=1 << 20)

    def body(w_ref, m_ref, v_ref, *rest):
        g_refs, (go_ref, d_ref, mo_ref, vo_ref) = rest[:depth], rest[depth:]
        layer = pl.program_id(0)
        gv = g_refs[0][...]
        for l in range(1, depth):
            gv = jnp.where(layer == l, g_refs[l][...], gv)
        go_ref[...] = gv
        d_ref[...], mo_ref[...], vo_ref[...] = _adam_math(w_ref[...], gv, m_ref[...], v_ref[...])

    nat = pl.BlockSpec((None, bm, cdim), lambda l, i: (l, i, 0))
    return pl.pallas_call(
        body, name=name, grid=(depth, r // bm),
        in_specs=[nat, nat, nat] + [pl.BlockSpec((None, bm, cdim), lambda l, i: (slot, i, 0))] * depth,
        out_specs=[nat] * 4, out_shape=[jax.ShapeDtypeStruct(w.shape, F32)] * 4,
        compiler_params=_params((7 + depth) * _nbytes((bm, cdim), F32)),
    )(w, m, v, *gstacks)


def kernel(x, c, ada_w, ada_b, ln_ffn1, ln_mix, ln_ffn2, ffn1_wg, ffn1_wu, ffn1_wd, w_in, conv_w, a_log, dt_bias, dn_norm, w_a, w_b, w_o, ffn2_wg, ffn2_wu, ffn2_wd, final_norm, loss_target, m_ada_w, m_ada_b, m_ln_ffn1, m_ln_mix, m_ln_ffn2, m_ffn1_wg, m_ffn1_wu, m_ffn1_wd, m_w_in, m_conv_w, m_a_log, m_dt_bias, m_dn_norm, m_w_a, m_w_b, m_w_o, m_ffn2_wg, m_ffn2_wu, m_ffn2_wd, m_final_norm, v_ada_w, v_ada_b, v_ln_ffn1, v_ln_mix, v_ln_ffn2, v_ffn1_wg, v_ffn1_wu, v_ffn1_wd, v_w_in, v_conv_w, v_a_log, v_dt_bias, v_dn_norm, v_w_a, v_w_b, v_w_o, v_ffn2_wg, v_ffn2_wu, v_ffn2_wd, v_final_norm):
    names = ["ada_w", "ada_b", "ln_ffn1", "ln_mix", "ln_ffn2", "ffn1_wg", "ffn1_wu", "ffn1_wd", "w_in", "conv_w",
             "a_log", "dt_bias", "dn_norm", "w_a", "w_b", "w_o", "ffn2_wg", "ffn2_wu", "ffn2_wd", "final_norm"]
    wts = dict(zip(names, (ada_w, ada_b, ln_ffn1, ln_mix, ln_ffn2, ffn1_wg, ffn1_wu, ffn1_wd, w_in, conv_w, a_log,
                           dt_bias, dn_norm, w_a, w_b, w_o, ffn2_wg, ffn2_wu, ffn2_wd, final_norm)))
    mom = dict(zip(names, (m_ada_w, m_ada_b, m_ln_ffn1, m_ln_mix, m_ln_ffn2, m_ffn1_wg, m_ffn1_wu, m_ffn1_wd, m_w_in,
                           m_conv_w, m_a_log, m_dt_bias, m_dn_norm, m_w_a, m_w_b, m_w_o, m_ffn2_wg, m_ffn2_wu,
                           m_ffn2_wd, m_final_norm)))
    var = dict(zip(names, (v_ada_w, v_ada_b, v_ln_ffn1, v_ln_mix, v_ln_ffn2, v_ffn1_wg, v_ffn1_wu, v_ffn1_wd, v_w_in,
                           v_conv_w, v_a_log, v_dt_bias, v_dn_norm, v_w_a, v_w_b, v_w_o, v_ffn2_wg, v_ffn2_wu,
                           v_ffn2_wd, v_final_norm)))
    _, s, d = x.shape
    depth = ada_w.shape[0]
    mx, my, mc = lax.axis_index("x"), lax.axis_index("y"), lax.axis_index("c")
    chip = 2 * mx + my
    me = 2 * chip + mc
    nshard = ada_w.shape[2]

    cact = _rowwise("c_silu", lambda cv: ((_silu(cv),), ()), [jnp.pad(c, ((0, 7), (0, 0)))], [], [(d, F32)], bm=8)[0]
    c_all = _allgather8("ag_c", cact)[:, 0, :]
    conv_all = _allgather8("ag_conv", jnp.pad(conv_w.reshape(depth * DN_CONV, -1), ((0, 8 - depth * DN_CONV), (0, 0))))
    conv_full = jnp.concatenate([conv_all[2 * j, :depth * DN_CONV] for j in range(4)], axis=1)
    conv_full = conv_full.reshape(depth, DN_CONV, 3 * DN_WIDTH)
    layer_shards = [[jnp.stack([wts[nm][l].astype(BF16) for nm in nms], axis=0) for nms, _ in _GROUPS]
                    for l in range(depth)]
    gathered0 = _gather_groups("ag_weights0", layer_shards[0])
    rows_of = lambda st: st[:, 0].reshape(-1, st.shape[-1])
    cols_of = lambda st: jnp.concatenate([st[j, 0] for j in range(4)], axis=1)

    def layer_weights(l, after):
        if l == 0:
            got = gathered0
        else:
            srcs, lands = _gather_wait(f"ag_weights{l}_wait", started[l], after)
            got = _pair_forward_groups(f"ag_weights{l}_pair", lands, srcs)
        ga, gb, g_wa, g_wo, g_win, g_wb = got
        return ga, gb, _mixer_weights(cols_of(g_win), rows_of(g_wa), cols_of(g_wb), rows_of(g_wo), d)

    c16 = jnp.pad(c_all, ((0, 8), (0, 0))).astype(BF16)
    parts = []
    for l in range(depth):
        bias = lax.dynamic_slice(ada_b[l], (chip * nshard,), (nshard,)).reshape(1, nshard)
        (mp,) = _matmul(f"ada_fwd{l}", c16, ada_w[l].astype(BF16), epi_bcast=[bias], epi=lambda acc, b: (acc + b,))
        parts.append(mp)
    mod_all = _allgather8("ag_mod", jnp.concatenate(parts, axis=0))
    mod_rows = jnp.concatenate([mod_all[2 * j] for j in range(4)], axis=1)
    mod = jnp.stack([lax.dynamic_index_in_dim(mod_rows, l * 16 + me, axis=0, keepdims=False) for l in range(depth)])

    gathered0, later, mod, conv_full = lax.optimization_barrier((gathered0, layer_shards[1:], mod, conv_full))
    started = {l: _gather_start(f"ag_weights{l}_start", later[l - 1]) for l in range(1, depth)}
    for st in started.values():
        mod = mod + st["token"][0, 0]
    small = dict(conv_w=conv_full, a_log=a_log, dt_bias=dt_bias, dn_norm=dn_norm, ln_ffn1=ln_ffn1, ln_mix=ln_mix,
                 ln_ffn2=ln_ffn2, final_norm=final_norm)
    ffn_names = _GROUPS[0][0] + _GROUPS[1][0]
    rs_state = {}

    def on_layer_grads(l, wg):
        wg["w_in"] = _w_in_grad(wg)
        gs = [jnp.stack([wg[nm] if nm in ffn_names else _shard_major(wg[nm], ax) for nm in nms], axis=1)
              for nms, ax in _GROUPS]
        rs_state[l] = _reduce_scatter_begin(f"rs{l}", gs, overlap=l > 0)
        return rs_state[l]["token"][0, 0] if l > 0 else None

    loss_part, dx, dmod, sgrads, d_fnorm = _local_step(x[0], loss_target[0], mod, layer_weights, small,
                                                       on_layer_grads)
    reduced = [_reduce_scatter_end(f"rs{l}", rs_state[l], l > 0, dx) for l in range(depth)]

    dmod_all = _allgather8("ag_dmod", jnp.pad(dmod, ((0, 8 - depth), (0, 0))))
    g_ada_w, g_ada_b = [], []
    for l in range(depth):
        dm_l = dmod_all[:, l, :]
        (gb_l,) = _rowwise(f"ada_b_grad{l}", lambda v: ((), (jnp.sum(v, axis=0, keepdims=True),)), [dm_l], [], [],
                           [(1, N_ADA * d)], bm=8)
        g_ada_b.append(gb_l[0])
        dm_sh = lax.dynamic_slice(dm_l, (0, chip * nshard), (8, nshard))
        (gw_l,) = _matmul(f"ada_w_grad{l}", c16, jnp.pad(dm_sh, ((0, 8), (0, 0))).astype(BF16), ta=True)
        g_ada_w.append(gw_l)
    grads = dict(ada_w=jnp.stack(g_ada_w), ada_b=jnp.stack(g_ada_b))

    smalls = [loss_part.reshape(1), d_fnorm]
    for l in range(depth):
        sg = sgrads[l]
        smalls += [sg["ln_ffn1"], sg["ln_mix"], sg["ln_ffn2"], sg["a_log"], sg["dt_bias"], sg["dn_norm"],
                   sg["conv_w"].reshape(-1)]
    sizes = [t.shape[0] for t in smalls]
    tile = 8 * LANES
    flat = jnp.concatenate([jnp.pad(t, (0, (-t.shape[0]) % tile)).reshape(-1, LANES) for t in smalls], axis=0)
    tot = _sum_leading("small_sum", _allgather8("ag_small", flat))
    offs, acc = [], 0
    for n_ in sizes:
        offs.append(acc)
        acc += -(-n_ // tile) * 8
    take = lambda i: tot[offs[i]:offs[i] + -(-sizes[i] // tile) * 8].reshape(-1)[:sizes[i]]
    loss = take(0)[0]
    grads["final_norm"] = take(1)
    per = 7
    for key_i, key in enumerate(["ln_ffn1", "ln_mix", "ln_ffn2", "a_log", "dt_bias", "dn_norm"]):
        grads[key] = jnp.stack([take(2 + per * l + key_i) for l in range(depth)])
    conv_g = jnp.stack([take(2 + per * l + 6).reshape(DN_CONV, 3 * DN_WIDTH) for l in range(depth)])
    csh = conv_w.shape[2]
    grads["conv_w"] = lax.dynamic_slice(conv_g, (0, 0, chip * csh), (depth, DN_CONV, csh))

    deltas, new_m, new_v = {}, {}, {}
    for gi, (nms, ax) in enumerate(_GROUPS):
        for q, nm in enumerate(nms):
            wv, mv, vv = wts[nm], mom[nm], var[nm]
            per_layer = [reduced[l][gi][q] for l in range(depth)]
            if ax == 1 and wv.shape[2] % LANES and nm != "w_in":
                tr = lambda t: jnp.swapaxes(t, 1, 2)
                gt = jnp.stack([g.T for g in per_layer], axis=0)
                dl, mn, vn = _adamw("adamw_" + nm, tr(wv), gt, tr(mv), tr(vv))
                grads[nm], deltas[nm], new_m[nm], new_v[nm] = tr(gt), tr(dl), tr(mn), tr(vn)
            elif nm == "w_in" and wv.shape[2] % LANES:
                tr = lambda t: jnp.transpose(t, (2, 0, 1))
                back = lambda t: jnp.transpose(t, (1, 2, 0))
                gt = jnp.stack([g.T for g in per_layer], axis=1)
                dl, mn, vn = _adamw_leading("adamw_" + nm, tr(wv), gt, tr(mv), tr(vv))
                grads[nm], deltas[nm], new_m[nm], new_v[nm] = back(gt), back(dl), back(mn), back(vn)
            else:
                grads[nm], deltas[nm], new_m[nm], new_v[nm] = _adamw_stacked(
                    "adamw_" + nm, wv, mv, vv, [reduced[l][gi] for l in range(depth)], q)

    for name in names:
        if name in deltas:
            continue
        wv, gv, mv, vv = wts[name], grads[name], mom[name], var[name]
        if wv.ndim == 1:
            wv, gv, mv, vv = (t.reshape(-1, LANES) for t in (wv, gv, mv, vv))
        dl, mn, vn = _adamw("adamw_" + name, wv, gv, mv, vv)
        deltas[name], new_m[name], new_v[name] = (t.reshape(wts[name].shape) for t in (dl, mn, vn))
    return (loss, dx.reshape(1, s, d), *[grads[n_] for n_ in names], *[deltas[n_] for n_ in names],
            *[new_m[n_] for n_ in names], *[new_v[n_] for n_ in names])
```

```python
import functools

import jax
import jax.numpy as jnp
from jax import lax
from jax.experimental import pallas as pl
from jax.experimental.pallas import tpu as pltpu

F32 = jnp.float32
BF16 = jnp.bfloat16
MESH = pl.DeviceIdType.MESH

NORM_EPS = 1e-6
DN_HEADS, DN_DIM, DN_CHUNK, DN_CONV = 8, 128, 64, 4
DN_WIDTH = DN_HEADS * DN_DIM
DA_HEADS, DA_DIM, DA_BLOCK = 12, 64, 128
DA_WIDTH = DA_HEADS * DA_DIM
DA_PATTERNS = ((128, 1), (512, 4), (2048, 16))
ALIBI_MAX_EXP = 8.0
N_ADA = 9
LANES = 128
V7X_VMEM_BYTES = 64 << 20
ADAM_LR, ADAM_B1, ADAM_B2, ADAM_EPS, ADAM_WD, ADAM_STEP = 0.001, 0.9, 0.999, 1e-08, 0.01, 10
NEG = -1e30
HI = lax.Precision.HIGHEST
NN = (((1,), (0,)), ((), ()))
NT = (((1,), (1,)), ((), ()))
TN = (((0,), (0,)), ((), ()))


def _nbytes(shape, dtype):
    n = 1
    for s in shape:
        n *= s
    return n * jnp.dtype(dtype).itemsize


def _params(block_bytes, scratch_bytes=0):
    need = 2 * block_bytes + scratch_bytes
    lim = min(max(need + need // 4 + (4 << 20), 32 << 20), V7X_VMEM_BYTES - (6 << 20))
    return pltpu.CompilerParams(vmem_limit_bytes=int(lim))


def _pick(n, cands):
    for c in cands:
        if c <= n and n % c == 0:
            return c
    return n


def _sigmoid(x):
    return jax.nn.sigmoid(x)


def _silu(x):
    return x * jax.nn.sigmoid(x)


def _softplus(x):
    return jnp.maximum(x, 0.0) + jnp.log(1.0 + jnp.exp(-jnp.abs(x)))


def _rowwise(name, fn, rows, bcast, row_outs, red_outs=(), bm=512):
    rows = [r if isinstance(r, tuple) else (r, r.shape[1], 0) for r in rows]
    s = rows[0][0].shape[0]
    bm = _pick(s, (bm, 128, 64, 32, 16, 8))
    nr, nb, no, nd = len(rows), len(bcast), len(row_outs), len(red_outs)
    in_specs = [pl.BlockSpec((bm, w), functools.partial(lambda i, ci: (i, ci), ci=ci)) for (_, w, ci) in rows]
    in_specs += [pl.BlockSpec(b.shape, lambda i: (0, 0)) for b in bcast]
    out_shape = [jax.ShapeDtypeStruct((s, w), dt) for (w, dt) in row_outs]
    out_shape += [jax.ShapeDtypeStruct((r, w), F32) for (r, w) in red_outs]
    out_specs = [pl.BlockSpec((bm, w), lambda i: (i, 0)) for (w, _) in row_outs]
    out_specs += [pl.BlockSpec((r, w), lambda i: (0, 0)) for (r, w) in red_outs]

    def body(*refs):
        ins = [r[...] for r in refs[:nr + nb]]
        outs = refs[nr + nb:nr + nb + no]
        reds = refs[nr + nb + no:]
        ov, rv = fn(*ins)
        for o, v in zip(outs, ov):
            o[...] = v.astype(o.dtype)
        if nd:
            @pl.when(pl.program_id(0) == 0)
            def _():
                for r in reds:
                    r[...] = jnp.zeros(r.shape, F32)
            for r, v in zip(reds, rv):
                r[...] += v.astype(F32)

    blk = sum(_nbytes((bm, w), a.dtype) for (a, w, _) in rows) + sum(_nbytes(b.shape, b.dtype) for b in bcast)
    blk += sum(_nbytes((bm, w), dt) for (w, dt) in row_outs) + sum(_nbytes(r, F32) for r in red_outs)
    res = pl.pallas_call(
        body, name=name, grid=(s // bm,), in_specs=in_specs, out_specs=out_specs, out_shape=out_shape,
        compiler_params=_params(3 * blk),
    )(*[a for (a, _, _) in rows], *bcast)
    return res


def _matmul(name, a, b, *, ta=False, tb=False, outs=(F32,), epi=None, epi_rows=(), epi_bcast=(),
            bm=None, bn=None, bk=None):
    if ta:
        k, m = a.shape
    else:
        m, k = a.shape
    n = b.shape[0] if tb else b.shape[1]
    assert (b.shape[1] if tb else b.shape[0]) == k, (name, a.shape, b.shape)
    if bm is None:
        bm = _pick(m, (1024, 1408, 768, 512, 384, 256, 128)) if ta else _pick(m, (1024, 512, 256, 128, 64, 32, 16))
    if bk is None:
        bk = k if k <= 3072 else _pick(k, (2816, 2048, 1024, 512))
        if ta:
            bk = _pick(k, (1024, 512, 256, 128, 64, 32, 16))
    if bn is None:
        bn = _pick(n, (1024, 768, 512, 384, 256, 128) if bk <= 2048 else (512, 384, 256, 128))
    nk = k // bk
    dims = TN if ta else (NT if tb else NN)
    a_spec = pl.BlockSpec((bk, bm), lambda i, j, kk: (kk, i)) if ta else pl.BlockSpec((bm, bk), lambda i, j, kk: (i, kk))
    b_spec = pl.BlockSpec((bn, bk), lambda i, j, kk: (j, kk)) if tb else pl.BlockSpec((bk, bn), lambda i, j, kk: (kk, j))
    in_specs = [a_spec, b_spec]
    in_specs += [pl.BlockSpec((bm, bn), lambda i, j, kk: (i, j)) for _ in epi_rows]
    in_specs += [pl.BlockSpec((1, bn), lambda i, j, kk: (0, j)) for _ in epi_bcast]
    out_shape = [jax.ShapeDtypeStruct((m, n), dt) for dt in outs]
    out_specs = [pl.BlockSpec((bm, bn), lambda i, j, kk: (i, j)) for _ in outs]
    ner, neb, no = len(epi_rows), len(epi_bcast), len(outs)

    def body(*refs):
        a_ref, b_ref = refs[0], refs[1]
        extra = refs[2:2 + ner + neb]
        out_refs = refs[2 + ner + neb:2 + ner + neb + no]
        prod = lax.dot_general(a_ref[...], b_ref[...], dims, preferred_element_type=F32)

        def finish(acc):
            vals = epi(acc, *[r[...] for r in extra]) if epi is not None else (acc,)
            for o, v in zip(out_refs, vals):
                o[...] = v.astype(o.dtype)

        if nk == 1:
            finish(prod)
        else:
            acc_ref = refs[-1]
            kk = pl.program_id(2)

            @pl.when(kk == 0)
            def _():
                acc_ref[...] = prod

            @pl.when(kk > 0)
            def _():
                acc_ref[...] += prod

            @pl.when(kk == nk - 1)
            def _():
                finish(acc_ref[...])

    blk = _nbytes((bm, bk), a.dtype) + _nbytes((bk, bn), b.dtype)
    blk += sum(_nbytes((bm, bn), r.dtype) for r in epi_rows) + sum(_nbytes((bm, bn), dt) for dt in outs)
    scratch = [pltpu.VMEM((bm, bn), F32)] if nk > 1 else []
    res = pl.pallas_call(
        body, name=name, grid=(m // bm, n // bn, nk), in_specs=in_specs, out_specs=out_specs,
        out_shape=out_shape, scratch_shapes=scratch,
        compiler_params=_params(blk, 3 * _nbytes((bm, bn), F32)),
    )(a, b, *epi_rows, *epi_bcast)
    return res


def _mm_core(name, grid, nk, pairs, out_defs, acc_shape, epi=None, epi_ins=()):
    npair, nep, no = len(pairs), len(epi_ins), len(out_defs)

    def body(*refs):
        extra = refs[2 * npair:2 * npair + nep]
        out_refs = refs[2 * npair + nep:2 * npair + nep + no]
        prod = None
        for p in range(npair):
            d = lax.dot_general(refs[2 * p][...], refs[2 * p + 1][...], pairs[p][4], preferred_element_type=F32)
            prod = d if prod is None else prod + d

        def finish(acc):
            vals = epi(acc, *[r[...] for r in extra]) if epi is not None else (acc,)
            for o, v in zip(out_refs, vals):
                o[...] = v.astype(o.dtype)

        if nk == 1:
            finish(prod)
        else:
            acc_ref = refs[-1]
            kk = pl.program_id(2)

            @pl.when(kk == 0)
            def _():
                acc_ref[...] = prod

            @pl.when(kk > 0)
            def _():
                acc_ref[...] += prod

            @pl.when(kk == nk - 1)
            def _():
                finish(acc_ref[...])

    def blk_bytes(spec, dtype):
        return _nbytes([s for s in spec.block_shape if s is not None], dtype)

    blk = sum(blk_bytes(sa, a.dtype) + blk_bytes(sb, b.dtype) for (a, sa, b, sb, _) in pairs)
    blk += sum(blk_bytes(sp, arr.dtype) for (arr, sp) in epi_ins) + sum(blk_bytes(sp, dt) for (_, dt, sp) in out_defs)
    ins, in_specs = [], []
    for (a, sa, b, sb, _) in pairs:
        ins += [a, b]
        in_specs += [sa, sb]
    ins += [arr for (arr, _) in epi_ins]
    in_specs += [sp for (_, sp) in epi_ins]
    return pl.pallas_call(
        body, name=name, grid=grid, in_specs=in_specs, out_specs=[sp for (_, _, sp) in out_defs],
        out_shape=[jax.ShapeDtypeStruct(sh, dt) for (sh, dt, _) in out_defs],
        scratch_shapes=[pltpu.VMEM(acc_shape, F32)] if nk > 1 else [],
        compiler_params=_params(blk, 3 * _nbytes(acc_shape, F32)),
    )(*ins)


def _rms_mod(h, ln, sh, sc):
    n = h * lax.rsqrt(jnp.mean(h * h, axis=-1, keepdims=True) + NORM_EPS) * ln
    return n * (1.0 + sc) + sh


def _swiglu_act(g, u):
    return _silu(g.astype(F32)) * u.astype(F32)


def _dn_prep(yc, pba, alog, dtb):
    act = _silu(yc)
    parts = []
    for idx in range(2 * DN_HEADS):
        seg = act[:, idx * DN_DIM:(idx + 1) * DN_DIM]
        seg = seg * lax.rsqrt(jnp.sum(seg * seg, axis=-1, keepdims=True) + NORM_EPS)
        if idx < DN_HEADS:
            seg = seg * (DN_DIM ** -0.5)
        parts.append(seg)
    parts.append(act[:, 2 * DN_WIDTH:])
    qkvn = jnp.concatenate(parts, axis=1)
    lane = lax.broadcasted_iota(jnp.int32, pba.shape, 1)
    beta = _sigmoid(pba)
    g = -jnp.exp(alog) * _softplus(pba + dtb)
    gb = jnp.where(lane < DN_HEADS, beta, jnp.where(lane < 2 * DN_HEADS, g, 0.0))
    return qkvn, gb


def _dn_outnorm(o_a, z, dn):
    parts = []
    for h in range(DN_HEADS):
        seg = o_a[:, h * DN_DIM:(h + 1) * DN_DIM]
        seg = seg * lax.rsqrt(jnp.mean(seg * seg, axis=-1, keepdims=True) + NORM_EPS) * dn
        parts.append(seg)
    return jnp.concatenate(parts, axis=1) * _silu(z)


def _shift_down(x, halo8, s):
    r = pltpu.roll(x, s, axis=0)
    top = pltpu.roll(halo8, s, axis=0)
    i8 = lax.broadcasted_iota(jnp.int32, top.shape, 0)
    return jnp.concatenate([jnp.where(i8 < s, top, r[0:8]), r[8:]], axis=0)


def _shift_up(x, halo8, s):
    m = x.shape[0]
    r = pltpu.roll(x, m - s, axis=0)
    bot = pltpu.roll(halo8, 8 - s, axis=0)
    i8 = lax.broadcasted_iota(jnp.int32, bot.shape, 0)
    return jnp.concatenate([r[:m - 8], jnp.where(i8 >= 8 - s, bot, r[m - 8:])], axis=0)


def _conv_prep_fwd(name, pq, convw8, pba, alog, dtb, bm=256):
    s, w = pq.shape
    nblk = s // bm
    hb = bm // 16

    def body(x_ref, halo_ref, w_ref, pba_ref, alog_ref, dtb_ref, yc_ref, qkv_ref, gb_ref):
        i = pl.program_id(0)
        x = x_ref[...].astype(F32)
        halo = jnp.where(i > 0, halo_ref[...].astype(F32)[8:16], 0.0)
        cw = w_ref[...]
        y = x * cw[DN_CONV - 1:DN_CONV]
        for sft in range(1, DN_CONV):
            y = y + _shift_down(x, halo, sft) * cw[DN_CONV - 1 - sft:DN_CONV - sft]
        ycb = y.astype(BF16)
        yc_ref[...] = ycb
        qkvn, gb = _dn_prep(ycb.astype(F32), pba_ref[...], alog_ref[...], dtb_ref[...])
        qkv_ref[...] = qkvn.astype(BF16)
        gb_ref[...] = gb

    blk = 3 * _nbytes((bm, w), BF16) + 4 * _nbytes((bm, w), F32)
    return pl.pallas_call(
        body, name=name, grid=(nblk,),
        in_specs=[pl.BlockSpec((bm, w), lambda i: (i, 0)),
                  pl.BlockSpec((16, w), lambda i: (jnp.maximum(i * hb - 1, 0), 0)),
                  pl.BlockSpec(convw8.shape, lambda i: (0, 0)),
                  pl.BlockSpec((bm, LANES), lambda i: (i, 0)),
                  pl.BlockSpec((1, LANES), lambda i: (0, 0)),
                  pl.BlockSpec((1, LANES), lambda i: (0, 0))],
        out_specs=[pl.BlockSpec((bm, w), lambda i: (i, 0)), pl.BlockSpec((bm, w), lambda i: (i, 0)),
                   pl.BlockSpec((bm, LANES), lambda i: (i, 0))],
        out_shape=[jax.ShapeDtypeStruct((s, w), BF16), jax.ShapeDtypeStruct((s, w), BF16),
                   jax.ShapeDtypeStruct((s, LANES), F32)],
        compiler_params=_params(blk),
    )(pq, pq, convw8, pba, alog, dtb)


def _conv_bwd(name, dyc, pq, convw8, bm=256):
    s, w = pq.shape
    nblk = s // bm
    hb = bm // 16

    def body(dy_ref, dyn_ref, x_ref, xh_ref, w_ref, dx_ref, dw_ref):
        i = pl.program_id(0)
        dy = dy_ref[...].astype(F32)
        nxt = jnp.where(i < nblk - 1, dyn_ref[...].astype(F32)[0:8], 0.0)
        x = x_ref[...].astype(F32)
        halo = jnp.where(i > 0, xh_ref[...].astype(F32)[8:16], 0.0)
        cw = w_ref[...]
        dx = dy * cw[DN_CONV - 1:DN_CONV]
        for sft in range(1, DN_CONV):
            dx = dx + _shift_up(dy, nxt, sft) * cw[DN_CONV - 1 - sft:DN_CONV - sft]
        dx_ref[...] = dx.astype(dx_ref.dtype)
        r8 = lax.broadcasted_iota(jnp.int32, (8, w), 0)
        dw = jnp.zeros((8, w), F32)
        for j in range(DN_CONV):
            sft = DN_CONV - 1 - j
            xs = x if sft == 0 else _shift_down(x, halo, sft)
            dw = dw + jnp.where(r8 == j, jnp.sum(dy * xs, axis=0, keepdims=True), 0.0)

        @pl.when(i == 0)
        def _():
            dw_ref[...] = jnp.zeros((8, w), F32)
        dw_ref[...] += dw

    blk = 4 * _nbytes((bm, w), BF16) + 5 * _nbytes((bm, w), F32)
    return pl.pallas_call(
        body, name=name, grid=(nblk,),
        in_specs=[pl.BlockSpec((bm, w), lambda i: (i, 0)),
                  pl.BlockSpec((16, w), lambda i: (jnp.minimum((i + 1) * hb, s // 16 - 1), 0)),
                  pl.BlockSpec((bm, w), lambda i: (i, 0)),
                  pl.BlockSpec((16, w), lambda i: (jnp.maximum(i * hb - 1, 0), 0)),
                  pl.BlockSpec(convw8.shape, lambda i: (0, 0))],
        out_specs=[pl.BlockSpec((bm, w), lambda i: (i, 0)), pl.BlockSpec((8, w), lambda i: (0, 0))],
        out_shape=[jax.ShapeDtypeStruct((s, w), BF16), jax.ShapeDtypeStruct((8, w), F32)],
        compiler_params=_params(blk),
    )(dyc, dyc, pq, pq, convw8)


BNN = (((2,), (1,)), ((0,), (0,)))
BNT = (((2,), (2,)), ((0,), (0,)))
BTN = (((1,), (1,)), ((0,), (0,)))


def _raw_dot_1pass(a, b, dims):
    return lax.dot_general(a.astype(BF16), b.astype(BF16), dims, preferred_element_type=F32)


def _raw_dot_3pass(a, b, dims):
    ah = a.astype(BF16)
    al = (a - ah.astype(F32)).astype(BF16)
    bh = b.astype(BF16)
    bl = (b - bh.astype(F32)).astype(BF16)
    d = lambda x, y: lax.dot_general(x, y, dims, preferred_element_type=F32)
    return d(ah, bh) + (d(ah, bl) + d(al, bh))


def _with_same_precision_vjp(raw):
    @functools.partial(jax.custom_vjp, nondiff_argnums=(2,))
    def dot(a, b, dims):
        return raw(a, b, dims)

    def fwd(a, b, dims):
        return raw(a, b, dims), (a, b)

    def bwd(dims, res, ct):
        a, b = res
        if dims == BNN:
            return raw(ct, b, BNT), raw(a, ct, BTN)
        if dims == BNT:
            return raw(ct, b, BNN), raw(ct, a, BTN)
        assert dims == BTN
        return raw(b, ct, BNT), raw(a, ct, BNN)

    dot.defvjp(fwd, bwd)
    return dot


_dot_1pass_vjp = _with_same_precision_vjp(_raw_dot_1pass)
_dot_3pass_vjp = _with_same_precision_vjp(_raw_dot_3pass)


def _dot_bf16(a, b, dims=BNN):
    return _dot_1pass_vjp(a, b, dims)


def _dot_3pass(a, b, dims=BNN):
    return _dot_3pass_vjp(a, b, dims)


def _neumann_inverse(x):
    h, c, _ = x.shape
    eye = lax.broadcasted_iota(jnp.int32, (h, c, c), 1) == lax.broadcasted_iota(jnp.int32, (h, c, c), 2)
    t = jnp.where(eye, 1.0, 0.0) + x
    p = x
    for _ in range(5):
        p = _raw_dot_3pass(p, p, BNN)
        t = t + _raw_dot_3pass(t, p, BNN)
    return t


@jax.custom_vjp
def _known_inverse(x, t):
    return t


def _known_inverse_fwd(x, t):
    return t, t


def _known_inverse_bwd(t, ct):
    return _raw_dot_3pass(_raw_dot_3pass(t, ct, BTN), t, BNT), jnp.zeros_like(t)


_known_inverse.defvjp(_known_inverse_fwd, _known_inverse_bwd)


def _delta_chunk(q, k, v, gcol, bcol, state, t_known=None):
    h, c, _ = q.shape
    row = lax.broadcasted_iota(jnp.int32, (h, c, c), 1)
    col = lax.broadcasted_iota(jnp.int32, (h, c, c), 2)
    incl, strict, eye = row >= col, row > col, row == col
    g_b = jnp.broadcast_to(gcol, (h, c, c))
    gc_row = jnp.sum(jnp.where(row <= col, g_b, 0.0), axis=1, keepdims=True)
    g_r = jnp.sum(jnp.where(eye, g_b, 0.0), axis=1, keepdims=True)
    gc_col = jnp.sum(jnp.where(incl, jnp.broadcast_to(g_r, (h, c, c)), 0.0), axis=2, keepdims=True)
    decay = jnp.exp(jnp.where(incl, gc_col - gc_row, NEG))
    kb = k * bcol
    vb = v * bcol
    x = -jnp.where(strict, _dot_bf16(kb, k, BNT) * decay, 0.0)
    t = _neumann_inverse(x) if t_known is None else _known_inverse(x, t_known)
    eg = jnp.exp(gc_col)
    u = _dot_3pass(t, vb)
    w = _dot_3pass(t, kb * eg)
    qk = _dot_bf16(q, k, BNT) * decay
    v_new = u - _dot_bf16(w, state)
    o = _dot_bf16(q * eg, state) + _dot_bf16(qk, v_new)
    g_last = jnp.sum(g_r, axis=2, keepdims=True)
    new_state = state * jnp.exp(g_last) + _dot_bf16(k * jnp.exp(g_last - gc_col), v_new, BTN)
    return o, new_state, t


def _lane_col(blk, idx):
    lane = lax.broadcasted_iota(jnp.int32, blk.shape, 1)
    return jnp.sum(jnp.where(lane == idx, blk, 0.0), axis=1, keepdims=True)


def _dn_heads(ref, base):
    return jnp.stack([ref[:, base + h * DN_DIM:base + (h + 1) * DN_DIM] for h in range(DN_HEADS)], axis=0).astype(F32)


def _dn_cols(gbv, base):
    return jnp.stack([_lane_col(gbv, base + h) for h in range(DN_HEADS)], axis=0)


def _delta_fwd(name, qkvn, gb):
    s = qkvn.shape[0]
    n = s // DN_CHUNK
    c = DN_CHUNK

    def body(qkv_ref, gb_ref, o_ref, st_ref, t_ref, state):
        @pl.when(pl.program_id(0) == 0)
        def _():
            state[...] = jnp.zeros(state.shape, F32)

        gbv = gb_ref[...]
        st = state[...]
        st_ref[0] = st
        o, new, t = _delta_chunk(_dn_heads(qkv_ref, 0), _dn_heads(qkv_ref, DN_WIDTH), _dn_heads(qkv_ref, 2 * DN_WIDTH),
                                 _dn_cols(gbv, DN_HEADS), _dn_cols(gbv, 0), st)
        for h in range(DN_HEADS):
            o_ref[:, h * DN_DIM:(h + 1) * DN_DIM] = o[h]
        t_ref[0] = t
        state[...] = new

    blk = _nbytes((c, 3 * DN_WIDTH), BF16) + _nbytes((c, LANES), F32) + _nbytes((c, DN_WIDTH), F32)
    blk += _nbytes((DN_HEADS, DN_DIM, DN_DIM), F32) + _nbytes((DN_HEADS, c, c), F32)
    return pl.pallas_call(
        body, name=name, grid=(n,),
        in_specs=[pl.BlockSpec((c, 3 * DN_WIDTH), lambda i: (i, 0)), pl.BlockSpec((c, LANES), lambda i: (i, 0))],
        out_specs=[pl.BlockSpec((c, DN_WIDTH), lambda i: (i, 0)),
                   pl.BlockSpec((1, DN_HEADS, DN_DIM, DN_DIM), lambda i: (i, 0, 0, 0)),
                   pl.BlockSpec((1, DN_HEADS, c, c), lambda i: (i, 0, 0, 0))],
        out_shape=[jax.ShapeDtypeStruct((s, DN_WIDTH), F32),
                   jax.ShapeDtypeStruct((n, DN_HEADS, DN_DIM, DN_DIM), F32),
                   jax.ShapeDtypeStruct((n, DN_HEADS, c, c), F32)],
        scratch_shapes=[pltpu.VMEM((DN_HEADS, DN_DIM, DN_DIM), F32)],
        compiler_params=_params(blk, 8 << 20),
    )(qkvn, gb)


def _delta_bwd(name, qkvn, gb, states, tinv, d_o):
    s = qkvn.shape[0]
    n = s // DN_CHUNK
    c = DN_CHUNK

    def body(qkv_ref, gb_ref, st_ref, t_ref, do_ref, dqkv_ref, dgb_ref, dstate):
        @pl.when(pl.program_id(0) == 0)
        def _():
            dstate[...] = jnp.zeros(dstate.shape, F32)

        gbv = gb_ref[...]
        lane = lax.broadcasted_iota(jnp.int32, (c, LANES), 1)
        t_known = t_ref[0]
        chunk = lambda *args: _delta_chunk(*args, t_known=t_known)[:2]
        _, vjp = jax.vjp(chunk, _dn_heads(qkv_ref, 0), _dn_heads(qkv_ref, DN_WIDTH),
                         _dn_heads(qkv_ref, 2 * DN_WIDTH), _dn_cols(gbv, DN_HEADS), _dn_cols(gbv, 0), st_ref[0])
        dq, dk, dv, dg, db, dst = vjp((_dn_heads(do_ref, 0), dstate[...]))
        dgb = jnp.zeros((c, LANES), F32)
        for h in range(DN_HEADS):
            dqkv_ref[:, h * DN_DIM:(h + 1) * DN_DIM] = dq[h]
            dqkv_ref[:, DN_WIDTH + h * DN_DIM:DN_WIDTH + (h + 1) * DN_DIM] = dk[h]
            dqkv_ref[:, 2 * DN_WIDTH + h * DN_DIM:2 * DN_WIDTH + (h + 1) * DN_DIM] = dv[h]
            dgb = dgb + jnp.where(lane == h, db[h], 0.0) + jnp.where(lane == DN_HEADS + h, dg[h], 0.0)
        dstate[...] = dst
        dgb_ref[...] = dgb

    rev = lambda i: (n - 1 - i, 0)
    blk = _nbytes((c, 3 * DN_WIDTH), BF16) + 2 * _nbytes((c, LANES), F32) + _nbytes((c, DN_WIDTH), F32)
    blk += _nbytes((DN_HEADS, DN_DIM, DN_DIM), F32) + _nbytes((c, 3 * DN_WIDTH), F32)
    return pl.pallas_call(
        body, name=name, grid=(n,),
        in_specs=[pl.BlockSpec((c, 3 * DN_WIDTH), rev), pl.BlockSpec((c, LANES), rev),
                  pl.BlockSpec((1, DN_HEADS, DN_DIM, DN_DIM), lambda i: (n - 1 - i, 0, 0, 0)),
                  pl.BlockSpec((1, DN_HEADS, c, c), lambda i: (n - 1 - i, 0, 0, 0)),
                  pl.BlockSpec((c, DN_WIDTH), rev)],
        out_specs=[pl.BlockSpec((c, 3 * DN_WIDTH), rev), pl.BlockSpec((c, LANES), rev)],
        out_shape=[jax.ShapeDtypeStruct((s, 3 * DN_WIDTH), F32), jax.ShapeDtypeStruct((s, LANES), F32)],
        scratch_shapes=[pltpu.VMEM((DN_HEADS, DN_DIM, DN_DIM), F32)],
        compiler_params=_params(blk, 16 << 20),
    )(qkvn, gb, states, tinv, d_o)


def _da_scores(q2f, k2, sub, valid, distf, head):
    lane = lax.broadcasted_iota(jnp.int32, q2f.shape, 1)
    hmask = (lane < DA_DIM) if sub == 0 else (lane >= DA_DIM)
    qm = jnp.where(hmask, q2f, 0.0).astype(BF16)
    slope = 2.0 ** (-ALIBI_MAX_EXP * (head + 1) / DA_HEADS)
    sc = lax.dot_general(qm, k2, NT, preferred_element_type=F32) * (DA_DIM ** -0.5)
    return jnp.where(valid, sc - slope * distf, NEG), qm, hmask


def _da_mask(i, r):
    qi = lax.broadcasted_iota(jnp.int32, (DA_BLOCK, 2 * DA_BLOCK), 0)
    ki = lax.broadcasted_iota(jnp.int32, (DA_BLOCK, 2 * DA_BLOCK), 1)
    dist = qi + DA_BLOCK - ki
    valid = (dist >= 0) & (dist <= DA_BLOCK) & ((ki >= DA_BLOCK) | (i > 0))
    return valid, (dist * r).astype(F32)


def _da_fwd(name, pda, r):
    s = pda.shape[0]
    n = s // r
    nb = n // DA_BLOCK
    w = DA_WIDTH
    dav = pda.reshape(n, r * 3 * w)

    def body(q_ref, kc_ref, kp_ref, vc_ref, vp_ref, o_ref, lse_ref):
        i = pl.program_id(1)
        valid, distf = _da_mask(i, r)
        lane = lax.broadcasted_iota(jnp.int32, (DA_BLOCK, LANES), 1)
        lse = jnp.zeros((DA_BLOCK, LANES), F32)
        for hp in range(DA_HEADS // 2):
            sl = slice(hp * LANES, (hp + 1) * LANES)
            q2f = q_ref[:, sl].astype(F32)
            k2 = jnp.concatenate([kp_ref[:, sl], kc_ref[:, sl]], axis=0)
            v2 = jnp.concatenate([vp_ref[:, sl], vc_ref[:, sl]], axis=0)
            o2 = None
            for sub in range(2):
                head = 2 * hp + sub
                sc, _, hmask = _da_scores(q2f, k2, sub, valid, distf, head)
                mx = jnp.max(sc, axis=1, keepdims=True)
                p = jnp.exp(sc - mx)
                l = jnp.sum(p, axis=1, keepdims=True)
                pv = lax.dot_general(p.astype(BF16), v2, NN, preferred_element_type=F32) / l
                o2 = pv if sub == 0 else jnp.where(hmask, pv, o2)
                lse = jnp.where(lane == head, mx + jnp.log(l), lse)
            o_ref[:, sl] = o2.astype(o_ref.dtype)
        lse_ref[...] = lse

    prev = lambda col: (lambda p, i: (jnp.maximum(i - 1, 0), 3 * p + col))
    cur = lambda col: (lambda p, i: (i, 3 * p + col))
    blk = 5 * _nbytes((DA_BLOCK, w), BF16) + _nbytes((DA_BLOCK, w), F32) + _nbytes((DA_BLOCK, LANES), F32)
    o, lse = pl.pallas_call(
        body, name=name, grid=(r, nb),
        in_specs=[pl.BlockSpec((DA_BLOCK, w), cur(0)), pl.BlockSpec((DA_BLOCK, w), cur(1)),
                  pl.BlockSpec((DA_BLOCK, w), prev(1)), pl.BlockSpec((DA_BLOCK, w), cur(2)),
                  pl.BlockSpec((DA_BLOCK, w), prev(2))],
        out_specs=[pl.BlockSpec((DA_BLOCK, w), lambda p, i: (i, p)),
                   pl.BlockSpec((DA_BLOCK, LANES), lambda p, i: (i, p))],
        out_shape=[jax.ShapeDtypeStruct((n, r * w), BF16), jax.ShapeDtypeStruct((n, r * LANES), F32)],
        compiler_params=_params(blk, 8 << 20),
    )(dav, dav, dav, dav, dav)
    return o.reshape(s, w), lse.reshape(s, LANES)


def _da_bwd(name, pda, d_ob, lse_tot, delta, r):
    s = pda.shape[0]
    n = s // r
    nb = n // DA_BLOCK
    w = DA_WIDTH
    dav = pda.reshape(n, r * 3 * w)
    dov = d_ob.reshape(n, r * w)
    lv = lse_tot.reshape(n, r * LANES)
    dlv = delta.reshape(n, r * LANES)

    def body(q_ref, kc_ref, kp_ref, vc_ref, vp_ref, do_ref, l_ref, dl_ref, dq_ref, dk_ref, dv_ref, ck, cv):
        i = pl.program_id(1)

        @pl.when(i == 0)
        def _():
            ck[...] = jnp.zeros(ck.shape, F32)
            cv[...] = jnp.zeros(cv.shape, F32)

        @pl.when(i < nb)
        def _():
            valid, distf = _da_mask(i, r)
            lsev = l_ref[...]
            dlt = dl_ref[...]
            for hp in range(DA_HEADS // 2):
                sl = slice(hp * LANES, (hp + 1) * LANES)
                q2f = q_ref[:, sl].astype(F32)
                k2 = jnp.concatenate([kp_ref[:, sl], kc_ref[:, sl]], axis=0)
                v2 = jnp.concatenate([vp_ref[:, sl], vc_ref[:, sl]], axis=0)
                do2f = do_ref[:, sl].astype(F32)
                dq2 = jnp.zeros((DA_BLOCK, LANES), F32)
                dk2 = jnp.zeros((2 * DA_BLOCK, LANES), F32)
                dv2 = jnp.zeros((2 * DA_BLOCK, LANES), F32)
                for sub in range(2):
                    head = 2 * hp + sub
                    sc, qm, hmask = _da_scores(q2f, k2, sub, valid, distf, head)
                    p = jnp.exp(sc - _lane_col(lsev, head))
                    dom = jnp.where(hmask, do2f, 0.0).astype(BF16)
                    dp = lax.dot_general(dom, v2, NT, preferred_element_type=F32)
                    ds = (p * (dp - _lane_col(dlt, head)) * (DA_DIM ** -0.5)).astype(BF16)
                    dq2 = dq2 + jnp.where(hmask, lax.dot_general(ds, k2, NN, preferred_element_type=F32), 0.0)
                    dk2 = dk2 + lax.dot_general(ds, qm, TN, preferred_element_type=F32)
                    dv2 = dv2 + lax.dot_general(p.astype(BF16), dom, TN, preferred_element_type=F32)
                dq_ref[:, sl] = dq2.astype(dq_ref.dtype)
                dk_ref[:, sl] = (ck[:, sl] + dk2[:DA_BLOCK]).astype(dk_ref.dtype)
                dv_ref[:, sl] = (cv[:, sl] + dv2[:DA_BLOCK]).astype(dv_ref.dtype)
                ck[:, sl] = dk2[DA_BLOCK:]
                cv[:, sl] = dv2[DA_BLOCK:]

        @pl.when(i == nb)
        def _():
            dk_ref[...] = ck[...].astype(dk_ref.dtype)
            dv_ref[...] = cv[...].astype(dv_ref.dtype)

    qrow = lambda i: jnp.minimum(i, nb - 1)
    prev = lambda col: (lambda p, i: (jnp.maximum(qrow(i) - 1, 0), 3 * p + col))
    cur = lambda col: (lambda p, i: (qrow(i), 3 * p + col))
    same = lambda p, i: (qrow(i), p)
    late = lambda p, i: (jnp.maximum(i - 1, 0), p)
    blk = 6 * _nbytes((DA_BLOCK, w), BF16) + 2 * _nbytes((DA_BLOCK, LANES), F32) + 3 * _nbytes((DA_BLOCK, w), F32)
    dq, dk, dv = pl.pallas_call(
        body, name=name, grid=(r, nb + 1),
        in_specs=[pl.BlockSpec((DA_BLOCK, w), cur(0)), pl.BlockSpec((DA_BLOCK, w), cur(1)),
                  pl.BlockSpec((DA_BLOCK, w), prev(1)), pl.BlockSpec((DA_BLOCK, w), cur(2)),
                  pl.BlockSpec((DA_BLOCK, w), prev(2)), pl.BlockSpec((DA_BLOCK, w), same),
                  pl.BlockSpec((DA_BLOCK, LANES), same), pl.BlockSpec((DA_BLOCK, LANES), same)],
        out_specs=[pl.BlockSpec((DA_BLOCK, w), same), pl.BlockSpec((DA_BLOCK, w), late),
                   pl.BlockSpec((DA_BLOCK, w), late)],
        out_shape=[jax.ShapeDtypeStruct((n, r * w), BF16)] * 3,
        scratch_shapes=[pltpu.VMEM((DA_BLOCK, w), F32), pltpu.VMEM((DA_BLOCK, w), F32)],
        compiler_params=_params(blk, 12 << 20),
    )(dav, dav, dav, dav, dav, dov, lv, dlv)
    return dq.reshape(s, w), dk.reshape(s, w), dv.reshape(s, w)


def _head_expand():
    hrow = lax.broadcasted_iota(jnp.int32, (LANES, DA_WIDTH), 0)
    lcol = lax.broadcasted_iota(jnp.int32, (LANES, DA_WIDTH), 1)
    return jnp.where(lcol // DA_DIM == hrow, 1.0, 0.0).astype(F32)


def _ffn_up(name, a, ga, tg, tu):
    s, d = a.shape
    nsh, _, _, ffs = ga.shape
    bm = _pick(s, (1024, 512, 256, 128))

    def body(a_ref, wg_ref, wu_ref, g_ref, u_ref, f_ref):
        av = a_ref[...]
        g = lax.dot_general(av, wg_ref[...], NN, preferred_element_type=F32)
        u = lax.dot_general(av, wu_ref[...], NN, preferred_element_type=F32)
        g_ref[...] = g.astype(BF16)
        u_ref[...] = u.astype(BF16)
        f_ref[...] = (_silu(g) * u).astype(BF16)

    wspec = lambda t: pl.BlockSpec((None, None, d, ffs), lambda i, j: (j, t, 0, 0))
    ospec = pl.BlockSpec((None, bm, ffs), lambda i, j: (j, i, 0))
    blk = _nbytes((bm, d), BF16) + 2 * _nbytes((d, ffs), BF16) + 3 * _nbytes((bm, ffs), BF16)
    return pl.pallas_call(
        body, name=name, grid=(s // bm, nsh),
        in_specs=[pl.BlockSpec((bm, d), lambda i, j: (i, 0)), wspec(tg), wspec(tu)],
        out_specs=[ospec] * 3, out_shape=[jax.ShapeDtypeStruct((nsh, s, ffs), BF16)] * 3,
        compiler_params=_params(blk, 4 * _nbytes((bm, ffs), F32)),
    )(a, ga, ga)


def _ffn_fwd(tag, h_in, ln, sh, sc, gt, ga, tg, tu, gb, td, weight):
    s, d = h_in.shape
    nsh, _, ffs, _ = gb.shape
    (a,) = _rowwise(tag + "_norm", lambda h, l, s1, s2: ((_rms_mod(h, l, s1, s2),), ()), [h_in], [ln, sh, sc],
                    [(d, BF16)])
    g, u, f = _ffn_up(tag + "_up", a, ga, tg, tu)
    bm, bn = _pick(s, (1024, 512, 256, 128)), _pick(d, (1024, 512, 256, 128))
    io = pl.BlockSpec((bm, bn), lambda i, j, kk: (i, j))
    h_out, o = _mm_core(
        tag + "_down", (s // bm, d // bn, nsh), nsh,
        [(f, pl.BlockSpec((None, bm, ffs), lambda i, j, kk: (kk, i, 0)),
          gb, pl.BlockSpec((None, None, ffs, bn), lambda i, j, kk: (kk, td, 0, j)), NN)],
        [((s, d), F32, io), ((s, d), BF16, io)], (bm, bn),
        epi=lambda acc, h, gv: (h + weight * gv * acc, acc),
        epi_ins=[(h_in, io), (gt, pl.BlockSpec((1, bn), lambda i, j, kk: (0, j)))])
    return h_out, dict(a=a, g=g, u=u, f=f, o=o)


def _resid_bwd(tag, dh_out, o, gt, weight):
    d = dh_out.shape[1]

    def fn(dh, ov, g):
        return (weight * g * dh,), (jnp.sum(weight * dh * ov.astype(F32), axis=0, keepdims=True),)

    do, d_gt = _rowwise(tag + "_resid_bwd", fn, [dh_out, o], [gt], [(d, BF16)], [(1, d)])
    return do, d_gt


def _norm_bwd(tag, h_in, da, dh_out, ln, sh, sc):
    d = h_in.shape[1]

    def fn(h, dav, dh, l, s1, s2):
        _, vjp = jax.vjp(_rms_mod, h, l, s1, s2)
        gh, gl, gs1, gs2 = vjp(dav)
        return (dh + gh,), (gl, gs1, gs2)

    return _rowwise(tag + "_norm_bwd", fn, [h_in, da, dh_out], [ln, sh, sc], [(d, F32)], [(1, d)] * 3)


def _ffn_bwd(tag, h_in, dh_out, sv, ln, sh, sc, gt, ga, tg, tu, gb, td, weight):
    s, d = h_in.shape
    nsh, _, ffs, _ = gb.shape
    bm, bn = _pick(s, (1024, 512, 256, 128)), _pick(d, (1024, 512, 256, 128))
    bk = _pick(s, (1024, 512, 256, 128))
    do, d_gt = _resid_bwd(tag, dh_out, sv["o"], gt, weight)

    def act_bwd(df, g, u):
        _, vjp = jax.vjp(_swiglu_act, g, u)
        return vjp(df)

    hid = pl.BlockSpec((None, bm, ffs), lambda i, j, kk: (j, i, 0))
    dg, du = _mm_core(
        tag + "_down_dx", (s // bm, nsh, 1), 1,
        [(do, pl.BlockSpec((bm, d), lambda i, j, kk: (i, 0)),
          gb, pl.BlockSpec((None, None, ffs, d), lambda i, j, kk: (j, td, 0, 0)), NT)],
        [((nsh, s, ffs), BF16, hid)] * 2, (bm, ffs), epi=act_bwd, epi_ins=[(sv["g"], hid), (sv["u"], hid)])
    (d_wd,) = _mm_core(
        tag + "_down_dw", (nsh, d // bn, s // bk), s // bk,
        [(sv["f"], pl.BlockSpec((None, bk, ffs), lambda i, j, kk: (i, kk, 0)),
          do, pl.BlockSpec((bk, bn), lambda i, j, kk: (kk, j)), TN)],
        [((nsh, ffs, d), BF16, pl.BlockSpec((None, ffs, bn), lambda i, j, kk: (i, 0, j)))], (ffs, bn))
    kmaj = pl.BlockSpec((None, bm, ffs), lambda i, j, kk: (kk, i, 0))
    wsp = lambda t: pl.BlockSpec((None, None, bn, ffs), functools.partial(lambda i, j, kk, t: (kk, t, j, 0), t=t))
    (da,) = _mm_core(
        tag + "_up_dx", (s // bm, d // bn, nsh), nsh, [(dg, kmaj, ga, wsp(tg), NT), (du, kmaj, ga, wsp(tu), NT)],
        [((s, d), F32, pl.BlockSpec((bm, bn), lambda i, j, kk: (i, j)))], (bm, bn))
    dws = []
    for nm, dh in (("_wg_dw", dg), ("_wu_dw", du)):
        (dw,) = _mm_core(
            tag + nm, (1, nsh, s // bk), s // bk,
            [(sv["a"], pl.BlockSpec((bk, d), lambda i, j, kk: (kk, 0)),
              dh, pl.BlockSpec((None, bk, ffs), lambda i, j, kk: (j, kk, 0)), TN)],
            [((nsh, d, ffs), BF16, pl.BlockSpec((None, d, ffs), lambda i, j, kk: (j, 0, 0)))], (d, ffs))
        dws.append(dw)
    dh_in, d_ln, d_sh, d_sc = _norm_bwd(tag, h_in, da, dh_out, ln, sh, sc)
    return dh_in, dict(wg=dws[0], wu=dws[1], wd=d_wd), dict(ln=d_ln, sh=d_sh, sc=d_sc, gt=d_gt)


def _mixer_fwd(tag, h_in, ln, sh, sc, gt, w, sp):
    d = h_in.shape[1]
    (a,) = _rowwise(tag + "_norm", lambda h, l, s1, s2: ((_rms_mod(h, l, s1, s2),), ()), [h_in], [ln, sh, sc],
                    [(d, BF16)])
    (pq,) = _matmul(tag + "_pq", a, w["wq"], outs=(BF16,))
    (pz,) = _matmul(tag + "_pz", a, w["wz"], outs=(BF16,))
    (pba,) = _matmul(tag + "_pba", a, w["wba"])
    (pda,) = _matmul(tag + "_pda", a, w["wda"], outs=(BF16,))
    (pg,) = _matmul(tag + "_pg", a, w["wg"], outs=(BF16,))
    yc, qkvn, gb = _conv_prep_fwd(tag + "_conv", pq, sp["conv8"], pba, sp["alog"], sp["dtb"])
    o_a, states, tinv = _delta_fwd(tag + "_delta", qkvn, gb)
    (o_an,) = _rowwise(tag + "_dnorm", lambda o, z, dn: ((_dn_outnorm(o, z.astype(F32), dn),), ()), [o_a, pz],
                       [sp["dn"]], [(DN_WIDTH, BF16)])
    ops, lses = [], []
    for (_, r) in DA_PATTERNS:
        o_p, lse_p = _da_fwd(f"{tag}_da{r}", pda, r)
        ops.append(o_p)
        lses.append(lse_p)

    def merge(o1, o2, o3, l1, l2, l3):
        mx = jnp.maximum(jnp.maximum(l1, l2), l3)
        e1, e2, e3 = jnp.exp(l1 - mx), jnp.exp(l2 - mx), jnp.exp(l3 - mx)
        tot = e1 + e2 + e3
        ex = _head_expand()
        up = lambda wgt: lax.dot_general(wgt / tot, ex, NN, precision=HI, preferred_element_type=F32)
        return (up(e1) * o1 + up(e2) * o2 + up(e3) * o3, mx + jnp.log(tot)), ()

    o_b, lse_tot = _rowwise(tag + "_merge", merge, ops + lses, [], [(DA_WIDTH, BF16), (LANES, F32)])
    (y_a,) = _matmul(tag + "_wa", o_an, w["w_a"], outs=(BF16,))
    (y_b,) = _matmul(tag + "_wb", o_b, w["w_b"], outs=(BF16,))

    def gate(ga, gbv, ya, yb):
        return _sigmoid(ga.astype(F32)) * ya.astype(F32) + _sigmoid(gbv.astype(F32)) * yb.astype(F32)

    (merged,) = _rowwise(tag + "_gate", lambda *v: ((gate(*v),), ()), [(pg, d, 0), (pg, d, 1), y_a, y_b], [],
                         [(d, BF16)])
    h_out, m = _matmul(tag + "_wo", merged, w["w_o"], outs=(F32, BF16), epi_rows=[h_in], epi_bcast=[gt],
                       epi=lambda acc, h, g: (h + g * acc, acc))
    sv = dict(a=a, pq=pq, pz=pz, pba=pba, pda=pda, pg=pg, yc=yc, qkvn=qkvn, gb=gb, o_a=o_a, states=states, tinv=tinv,
              o_an=o_an, o_b=o_b, lse=lse_tot, y_a=y_a, y_b=y_b, merged=merged, m=m, gate=gate)
    return h_out, sv


def _mixer_bwd(tag, h_in, dh_out, sv, ln, sh, sc, gt, w, sp):
    d = h_in.shape[1]
    dm, d_gt = _resid_bwd(tag, dh_out, sv["m"], gt, 1.0)
    (d_merged,) = _matmul(tag + "_wo_dx", dm, w["w_o"], tb=True, outs=(BF16,))
    (d_wo,) = _matmul(tag + "_wo_dw", sv["merged"], dm, ta=True, outs=(BF16,))
    gate = sv["gate"]

    def gate_bwd(dmg, ga, gbv, ya, yb):
        _, vjp = jax.vjp(gate, ga.astype(F32), gbv.astype(F32), ya.astype(F32), yb.astype(F32))
        dga, dgb, dya, dyb = vjp(dmg.astype(F32))
        return (jnp.concatenate([dga, dgb], axis=1), dya, dyb), ()

    pg = sv["pg"]
    d_pg, d_ya, d_yb = _rowwise(tag + "_gate_bwd", gate_bwd, [d_merged, (pg, d, 0), (pg, d, 1), sv["y_a"], sv["y_b"]],
                                [], [(2 * d, BF16), (d, BF16), (d, BF16)])
    (d_oan,) = _matmul(tag + "_wa_dx", d_ya, w["w_a"], tb=True)
    (d_wa,) = _matmul(tag + "_wa_dw", sv["o_an"], d_ya, ta=True, outs=(BF16,))
    (d_ob,) = _matmul(tag + "_wb_dx", d_yb, w["w_b"], tb=True, outs=(BF16,))
    (d_wb,) = _matmul(tag + "_wb_dw", sv["o_b"], d_yb, ta=True, outs=(BF16,))

    def dnorm_bwd(doan, o, z, dn):
        _, vjp = jax.vjp(_dn_outnorm, o, z.astype(F32), dn)
        go, gz, gdn = vjp(doan)
        return (go, gz), (gdn,)

    d_oa, d_pz, d_dn = _rowwise(tag + "_dnorm_bwd", dnorm_bwd, [d_oan, sv["o_a"], sv["pz"]], [sp["dn"]],
                                [(DN_WIDTH, F32), (DN_WIDTH, BF16)], [(1, DN_DIM)])
    d_qkvn, d_gb = _delta_bwd(tag + "_delta_bwd", sv["qkvn"], sv["gb"], sv["states"], sv["tinv"], d_oa)

    def prep_bwd(dq, dgbv, yc, pba, alog, dtb):
        _, vjp = jax.vjp(_dn_prep, yc.astype(F32), pba, alog, dtb)
        gyc, gpba, galog, gdtb = vjp((dq, dgbv))
        return (gyc, gpba), (galog, gdtb)

    d_yc, d_pba, d_alog, d_dtb = _rowwise(tag + "_prep_bwd", prep_bwd, [d_qkvn, d_gb, sv["yc"], sv["pba"]],
                                          [sp["alog"], sp["dtb"]], [(3 * DN_WIDTH, BF16), (LANES, BF16)],
                                          [(1, LANES), (1, LANES)], bm=128)
    d_pq, d_conv = _conv_bwd(tag + "_conv_bwd", d_yc, sv["pq"], sp["conv8"])

    def delta_fn(dob, ob):
        prod = dob.astype(F32) * ob.astype(F32)
        return (lax.dot_general(prod, _head_expand(), NT, precision=HI, preferred_element_type=F32),), ()

    (delta,) = _rowwise(tag + "_da_delta", delta_fn, [d_ob, sv["o_b"]], [], [(LANES, F32)])
    grads = [_da_bwd(f"{tag}_da{r}_bwd", sv["pda"], d_ob, sv["lse"], delta, r) for (_, r) in DA_PATTERNS]

    def sum3(*parts):
        q1, k1, v1, q2, k2, v2, q3, k3, v3 = (p.astype(F32) for p in parts)
        return (jnp.concatenate([q1 + q2 + q3, k1 + k2 + k3, v1 + v2 + v3], axis=1),), ()

    (d_pda,) = _rowwise(tag + "_da_sum", sum3, [t for g in grads for t in g], [], [(3 * DA_WIDTH, BF16)])

    a = sv["a"]
    (da,) = _matmul(tag + "_pq_dx", d_pq, w["wq"], tb=True)
    add = lambda acc, prev: (acc + prev,)
    (da,) = _matmul(tag + "_pz_dx", d_pz, w["wz"], tb=True, epi_rows=[da], epi=add)
    (da,) = _matmul(tag + "_pba_dx", d_pba, w["wba"], tb=True, epi_rows=[da], epi=add)
    (da,) = _matmul(tag + "_pda_dx", d_pda, w["wda"], tb=True, epi_rows=[da], epi=add)
    (da,) = _matmul(tag + "_pg_dx", d_pg, w["wg"], tb=True, epi_rows=[da], epi=add)
    (d_wq,) = _matmul(tag + "_pq_dw", a, d_pq, ta=True, outs=(BF16,))
    (d_wz,) = _matmul(tag + "_pz_dw", a, d_pz, ta=True, outs=(BF16,))
    (d_wba,) = _matmul(tag + "_pba_dw", a, d_pba, ta=True, outs=(BF16,))
    (d_wda,) = _matmul(tag + "_pda_dw", a, d_pda, ta=True, outs=(BF16,))
    (d_wg,) = _matmul(tag + "_pg_dw", a, d_pg, ta=True, outs=(BF16,))
    dh_in, d_ln, d_sh, d_sc = _norm_bwd(tag, h_in, da, dh_out, ln, sh, sc)
    wgrads = dict(wq=d_wq, wz=d_wz, wba=d_wba, wda=d_wda, wg=d_wg, w_a=d_wa, w_b=d_wb, w_o=d_wo)
    small = dict(ln=d_ln, sh=d_sh, sc=d_sc, gt=d_gt, dn=d_dn, alog=d_alog, dtb=d_dtb, conv=d_conv)
    return dh_in, wgrads, small


def _loss_head(h, target, fnorm):
    d = h.shape[1]

    def fn(hv, tv, fw):
        def lossf(hh, ww):
            y = hh * lax.rsqrt(jnp.mean(hh * hh, axis=-1, keepdims=True) + NORM_EPS) * ww
            return 0.5 * jnp.sum(jnp.mean(jnp.square(y - tv), axis=-1))

        val, (dh, dw) = jax.value_and_grad(lossf, argnums=(0, 1))(hv, fw)
        return (dh,), (jnp.full((1, LANES), val, F32), dw)

    return _rowwise("loss_head", fn, [h, target], [fnorm], [(d, F32)], [(1, LANES), (1, d)])


def _row(v):
    return v.reshape(1, -1)


def _pad_lanes(v, offset):
    return jnp.pad(v.reshape(1, -1), ((0, 0), (offset, LANES - offset - v.shape[0])))


_UP_SLOTS = dict(ffn1_wg=0, ffn1_wu=1, ffn2_wg=2, ffn2_wu=3)
_DOWN_SLOTS = dict(ffn1_wd=0, ffn2_wd=1)


def _local_step(x2, target, mod, layer_weights, small, on_layer_grads):
    depth = mod.shape[0]
    d = x2.shape[1]
    h = x2
    saved = []
    mods = []
    up = lambda l, nm: _UP_SLOTS[nm]
    down = lambda l, nm: _DOWN_SLOTS[nm]
    for l in range(depth):
        m9 = [_row(mod[l, i * d:(i + 1) * d]) for i in range(N_ADA)]
        sp = dict(conv8=jnp.pad(small["conv_w"][l], ((0, 8 - DN_CONV), (0, 0))),
                  alog=_pad_lanes(small["a_log"][l], DN_HEADS), dtb=_pad_lanes(small["dt_bias"][l], DN_HEADS),
                  dn=_row(small["dn_norm"][l]))
        ga, gb, w = layer_weights(l, h)
        h0 = h
        h1, sv1 = _ffn_fwd(f"l{l}_ffn1", h0, _row(small["ln_ffn1"][l]), m9[0], m9[1], m9[2], ga, up(l, "ffn1_wg"),
                           up(l, "ffn1_wu"), gb, down(l, "ffn1_wd"), 0.5)
        h2, sv2 = _mixer_fwd(f"l{l}_mix", h1, _row(small["ln_mix"][l]), m9[3], m9[4], m9[5], w, sp)
        h3, sv3 = _ffn_fwd(f"l{l}_ffn2", h2, _row(small["ln_ffn2"][l]), m9[6], m9[7], m9[8], ga, up(l, "ffn2_wg"),
                           up(l, "ffn2_wu"), gb, down(l, "ffn2_wd"), 0.5)
        saved.append((h0, h1, h2, sv1, sv2, sv3, sp, ga, gb, w))
        mods.append(m9)
        h = h3
    dh, loss_part, d_fnorm = _loss_head(h, target, _row(small["final_norm"]))
    sgrads, dmods = [], []
    token = None
    for l in reversed(range(depth)):
        h0, h1, h2, sv1, sv2, sv3, sp, ga, gb, w = saved[l]
        m9 = mods[l] if token is None else [r + token for r in mods[l]]
        dh, g3, s3 = _ffn_bwd(f"l{l}_ffn2", h2, dh, sv3, _row(small["ln_ffn2"][l]), m9[6], m9[7], m9[8], ga,
                              up(l, "ffn2_wg"), up(l, "ffn2_wu"), gb, down(l, "ffn2_wd"), 0.5)
        dh, g2, s2 = _mixer_bwd(f"l{l}_mix", h1, dh, sv2, _row(small["ln_mix"][l]), m9[3], m9[4], m9[5], w, sp)
        dh, g1, s1 = _ffn_bwd(f"l{l}_ffn1", h0, dh, sv1, _row(small["ln_ffn1"][l]), m9[0], m9[1], m9[2], ga,
                              up(l, "ffn1_wg"), up(l, "ffn1_wu"), gb, down(l, "ffn1_wd"), 0.5)
        token = on_layer_grads(l, dict(ffn1_wg=g1["wg"], ffn1_wu=g1["wu"], ffn1_wd=g1["wd"], ffn2_wg=g3["wg"],
                                       ffn2_wu=g3["wu"], ffn2_wd=g3["wd"], **g2))
        dmods.append(jnp.concatenate([s1["sh"], s1["sc"], s1["gt"], s2["sh"], s2["sc"], s2["gt"],
                                      s3["sh"], s3["sc"], s3["gt"]], axis=1))
        sgrads.append(dict(ln_ffn1=s1["ln"][0], ln_mix=s2["ln"][0], ln_ffn2=s3["ln"][0],
                           a_log=s2["alog"][0, DN_HEADS:2 * DN_HEADS], dt_bias=s2["dtb"][0, DN_HEADS:2 * DN_HEADS],
                           dn_norm=s2["dn"][0], conv_w=s2["conv"][:DN_CONV]))
    sgrads.reverse()
    dmods.reverse()
    return loss_part[0, 0], dh, jnp.concatenate(dmods, axis=0), sgrads, d_fnorm[0]


def _flip(v, bit):
    return 1 - v if bit else v


def _allgather8(name, x):
    r, c = x.shape

    def body(x_ref, out_ref, send_sems, recv_sems, local_sem):
        mx, my, mc = lax.axis_index("x"), lax.axis_index("y"), lax.axis_index("c")
        me = 4 * mx + 2 * my + mc
        mine = pltpu.make_async_copy(x_ref, out_ref.at[me], local_sem)
        mine.start()
        sends = []
        for k in range(1, 8):
            peer = (_flip(mx, k & 4), _flip(my, k & 2), _flip(mc, k & 1))
            cp = pltpu.make_async_remote_copy(src_ref=x_ref, dst_ref=out_ref.at[me], send_sem=send_sems.at[k - 1],
                                              recv_sem=recv_sems.at[k - 1], device_id=peer, device_id_type=MESH)
            cp.start()
            sends.append(cp)
        for k in range(1, 8):
            peer = (_flip(mx, k & 4), _flip(my, k & 2), _flip(mc, k & 1))
            src = 4 * peer[0] + 2 * peer[1] + peer[2]
            pltpu.make_async_remote_copy(src_ref=x_ref, dst_ref=out_ref.at[src], send_sem=send_sems.at[k - 1],
                                         recv_sem=recv_sems.at[k - 1], device_id=peer, device_id_type=MESH).wait_recv()
        for cp in sends:
            cp.wait_send()
        mine.wait()

    return pl.pallas_call(
        body, name=name, out_shape=jax.ShapeDtypeStruct((8, r, c), x.dtype),
        in_specs=[pl.BlockSpec(memory_space=pltpu.VMEM)], out_specs=pl.BlockSpec(memory_space=pltpu.VMEM),
        scratch_shapes=[pltpu.SemaphoreType.DMA((7,)), pltpu.SemaphoreType.DMA((7,)), pltpu.SemaphoreType.DMA],
        compiler_params=_params(9 * _nbytes((r, c), x.dtype)),
    )(x)


def _chip_peers(mx, my):
    chips = [(1 - mx, my), (mx, 1 - my), (1 - mx, 1 - my)]
    return chips, [2 * cx + cy for (cx, cy) in chips]


_ANY = pl.BlockSpec(memory_space=pl.ANY)


def _row_half(mc, r):
    return pl.ds(pl.multiple_of(mc * (r // 2), 16), r // 2)


def _gather_groups(name, shards):
    ng = len(shards)

    def body(*refs):
        xs, outs = refs[:ng], refs[ng:2 * ng]
        send_sems, recv_sems = refs[2 * ng:]
        mx, my, mc = lax.axis_index("x"), lax.axis_index("y"), lax.axis_index("c")
        j = 2 * mx + my
        chips, idxs = _chip_peers(mx, my)
        sib = (mx, my, 1 - mc)

        def copy(k, src, dst, to):
            return pltpu.make_async_remote_copy(src_ref=src, dst_ref=dst, send_sem=send_sems.at[k],
                                                recv_sem=recv_sems.at[k], device_id=to, device_id_type=MESH)

        first, passed = [], []
        for g in range(ng):
            mine = _row_half(mc, shards[g].shape[1])
            for t, chip in enumerate(chips):
                cp = copy(6 * g + t, xs[g].at[:, mine], outs[g].at[j, :, mine], (*chip, mc))
                cp.start()
                first.append(cp)
        for g in range(ng):
            mine = _row_half(mc, shards[g].shape[1])
            for t, chip in enumerate(chips):
                landed = outs[g].at[idxs[t], :, mine]
                copy(6 * g + t, landed, landed, (*chip, mc)).wait_recv()
                fwd = copy(6 * g + 3 + t, landed, landed, sib)
                fwd.start()
                passed.append(fwd)
        for g in range(ng):
            theirs_half = _row_half(1 - mc, shards[g].shape[1])
            for t in range(3):
                theirs = outs[g].at[idxs[t], :, theirs_half]
                copy(6 * g + 3 + t, theirs, theirs, sib).wait_recv()
        for cp in first + passed:
            cp.wait_send()

    outs = pl.pallas_call(
        body, name=name, out_shape=[jax.ShapeDtypeStruct((4,) + x.shape, x.dtype) for x in shards],
        in_specs=[_ANY] * ng, out_specs=[_ANY] * ng,
        scratch_shapes=[pltpu.SemaphoreType.DMA((6 * ng,)), pltpu.SemaphoreType.DMA((6 * ng,))],
    )(*shards)
    return _place_own_slab(outs, shards)


def _place_own_slab(outs, shards):
    chip = 2 * lax.axis_index("x") + lax.axis_index("y")
    return [lax.dynamic_update_slice(o, x[None], (chip,) + (0,) * x.ndim) for o, x in zip(outs, shards)]


_HBM = pl.BlockSpec(memory_space=pltpu.HBM)
_SEM = pl.BlockSpec(memory_space=pltpu.SEMAPHORE)
_DATAFLOW = pltpu.SideEffectType.DATAFLOW_SIDE_EFFECTING


def _ici_gather_copies(src_refs, land_refs, send_sems, recv_sems, scatter=False):
    mx, my, mc = lax.axis_index("x"), lax.axis_index("y"), lax.axis_index("c")
    j = 2 * mx + my
    chips, idxs = _chip_peers(mx, my)
    sends, recvs = [], []
    for g, src in enumerate(src_refs):
        for t, chip in enumerate(chips):
            common = dict(send_sem=send_sems.at[3 * g + t], recv_sem=recv_sems.at[3 * g + t], device_id=(*chip, mc),
                          device_id_type=MESH)
            if scatter:
                out, to, frm = src.at[idxs[t]], land_refs[g].at[j], land_refs[g].at[idxs[t]]
            else:
                mine = _row_half(mc, src.shape[1])
                out, to, frm = src.at[:, mine], land_refs[g].at[j, :, mine], land_refs[g].at[idxs[t], :, mine]
            sends.append(pltpu.make_async_remote_copy(src_ref=out, dst_ref=to, **common))
            recvs.append(pltpu.make_async_remote_copy(src_ref=out, dst_ref=frm, **common))
    return sends, recvs


def _gather_start(name, shards, scatter=False):
    ng = len(shards)

    def body(*refs):
        srcs, lands = refs[:ng], refs[ng:2 * ng]
        send_sems, recv_sems = refs[2 * ng], refs[2 * ng + 1]
        token = refs[-1]
        sends, _ = _ici_gather_copies(srcs, lands, send_sems, recv_sems, scatter)
        for cp in sends:
            cp.start()
        token[...] = jnp.zeros(token.shape, token.dtype)

    land_shape = lambda x: x.shape if scatter else (4,) + x.shape
    srcs = [pltpu.with_memory_space_constraint(x, pltpu.HBM) for x in shards]
    lands = [pltpu.with_memory_space_constraint(lax.empty(land_shape(x), x.dtype), pltpu.HBM) for x in shards]
    res = pl.pallas_call(
        body, name=name,
        out_shape=(pltpu.SemaphoreType.DMA((3 * ng,)), pltpu.SemaphoreType.DMA((3 * ng,)),
                   *[pltpu.HBM(x.shape, x.dtype) for x in srcs], *[pltpu.HBM(x.shape, x.dtype) for x in lands],
                   jax.ShapeDtypeStruct((8, LANES), F32)),
        in_specs=[_HBM] * (2 * ng),
        out_specs=(_SEM, _SEM, *[_HBM] * (2 * ng), pl.BlockSpec(memory_space=pltpu.VMEM)),
        input_output_aliases={i: 2 + i for i in range(2 * ng)},
        compiler_params=pltpu.CompilerParams(has_side_effects=_DATAFLOW),
    )(*srcs, *lands)
    return dict(send_sems=res[0], recv_sems=res[1], srcs=list(res[2:2 + ng]), lands=list(res[2 + ng:2 + 2 * ng]),
                token=res[-1])


def _gather_wait(name, started, after, scatter=False):
    ng = len(started["srcs"])

    def body(*refs):
        srcs, lands = refs[:ng], refs[ng:2 * ng]
        send_sems, recv_sems = refs[2 * ng], refs[2 * ng + 1]
        sends, recvs = _ici_gather_copies(srcs, lands, send_sems, recv_sems, scatter)
        for cp in sends:
            cp.wait_send()
        for cp in recvs:
            cp.wait_recv()

    res = pl.pallas_call(
        body, name=name,
        out_shape=[pltpu.HBM(x.shape, x.dtype) for x in started["srcs"] + started["lands"]],
        in_specs=[_HBM] * (2 * ng) + [_SEM, _SEM, _ANY], out_specs=[_HBM] * (2 * ng),
        input_output_aliases={i: i for i in range(2 * ng)},
        compiler_params=pltpu.CompilerParams(has_side_effects=_DATAFLOW),
    )(*started["srcs"], *started["lands"], started["send_sems"], started["recv_sems"], after)
    return list(res[:ng]), list(res[ng:])


def _pair_forward_groups(name, lands, shards):
    ng = len(lands)

    def body(*refs):
        ins, outs = refs[:ng], refs[ng:2 * ng]
        send_sems, recv_sems = refs[2 * ng:]
        mx, my, mc = lax.axis_index("x"), lax.axis_index("y"), lax.axis_index("c")
        _, idxs = _chip_peers(mx, my)
        sib = (mx, my, 1 - mc)
        cps = []
        for g in range(ng):
            mine = _row_half(mc, lands[g].shape[2])
            for t in range(3):
                cp = pltpu.make_async_remote_copy(src_ref=ins[g].at[idxs[t], :, mine], dst_ref=outs[g].at[idxs[t], :, mine],
                                                  send_sem=send_sems.at[3 * g + t], recv_sem=recv_sems.at[3 * g + t],
                                                  device_id=sib, device_id_type=MESH)
                cp.start()
                cps.append(cp)
        for g in range(ng):
            theirs = _row_half(1 - mc, lands[g].shape[2])
            for t in range(3):
                pltpu.make_async_remote_copy(src_ref=ins[g].at[idxs[t], :, theirs], dst_ref=outs[g].at[idxs[t], :, theirs],
                                             send_sem=send_sems.at[3 * g + t], recv_sem=recv_sems.at[3 * g + t],
                                             device_id=sib, device_id_type=MESH).wait_recv()
        for cp in cps:
            cp.wait_send()

    outs = pl.pallas_call(
        body, name=name, out_shape=[jax.ShapeDtypeStruct(x.shape, x.dtype) for x in lands],
        in_specs=[_ANY] * ng, out_specs=[_ANY] * ng, input_output_aliases={i: i for i in range(ng)},
        scratch_shapes=[pltpu.SemaphoreType.DMA((3 * ng,)), pltpu.SemaphoreType.DMA((3 * ng,))],
    )(*lands)
    return _place_own_slab(outs, shards)


def _pair_swap_groups(name, gs):
    ng = len(gs)

    def body(*refs):
        xs, outs = refs[:ng], refs[ng:2 * ng]
        send_sems, recv_sems = refs[2 * ng:]
        mx, my, mc = lax.axis_index("x"), lax.axis_index("y"), lax.axis_index("c")
        cps = []
        for g in range(ng):
            cp = pltpu.make_async_remote_copy(src_ref=xs[g].at[:, :, _row_half(1 - mc, gs[g].shape[2])], dst_ref=outs[g],
                                              send_sem=send_sems.at[g], recv_sem=recv_sems.at[g],
                                              device_id=(mx, my, 1 - mc), device_id_type=MESH)
            cp.start()
            cps.append(cp)
        for cp in cps:
            cp.wait()

    return pl.pallas_call(
        body, name=name,
        out_shape=[jax.ShapeDtypeStruct(x.shape[:2] + (x.shape[2] // 2, x.shape[3]), x.dtype) for x in gs],
        in_specs=[_ANY] * ng, out_specs=[_ANY] * ng,
        scratch_shapes=[pltpu.SemaphoreType.DMA((ng,)), pltpu.SemaphoreType.DMA((ng,))],
    )(*gs)


def _chip_scatter_groups(name, ps):
    ng = len(ps)

    def body(*refs):
        xs, outs = refs[:ng], refs[ng:2 * ng]
        send_sems, recv_sems = refs[2 * ng:]
        mx, my, mc = lax.axis_index("x"), lax.axis_index("y"), lax.axis_index("c")
        j = 2 * mx + my
        chips, idxs = _chip_peers(mx, my)
        sends = []
        for g in range(ng):
            for t, chip in enumerate(chips):
                cp = pltpu.make_async_remote_copy(src_ref=xs[g].at[idxs[t]], dst_ref=outs[g].at[j],
                                                  send_sem=send_sems.at[3 * g + t], recv_sem=recv_sems.at[3 * g + t],
                                                  device_id=(*chip, mc), device_id_type=MESH)
                cp.start()
                sends.append(cp)
        for g in range(ng):
            for t, chip in enumerate(chips):
                pltpu.make_async_remote_copy(src_ref=xs[g].at[idxs[t]], dst_ref=outs[g].at[idxs[t]],
                                             send_sem=send_sems.at[3 * g + t], recv_sem=recv_sems.at[3 * g + t],
                                             device_id=(*chip, mc), device_id_type=MESH).wait_recv()
        for cp in sends:
            cp.wait_send()

    outs = pl.pallas_call(
        body, name=name, out_shape=[jax.ShapeDtypeStruct(x.shape, x.dtype) for x in ps],
        in_specs=[_ANY] * ng, out_specs=[_ANY] * ng,
        scratch_shapes=[pltpu.SemaphoreType.DMA((3 * ng,)), pltpu.SemaphoreType.DMA((3 * ng,))],
    )(*ps)
    return _place_own_part(outs, ps)


def _place_own_part(outs, ps):
    chip = 2 * lax.axis_index("x") + lax.axis_index("y")
    return [lax.dynamic_update_slice(o, lax.dynamic_index_in_dim(x, chip, 0, keepdims=True), (chip,) + (0,) * (x.ndim - 1))
            for o, x in zip(outs, ps)]


def _pair_merge_groups(name, fs):
    ng = len(fs)

    def body(*refs):
        xs, outs = refs[:ng], refs[ng:2 * ng]
        send_sems, recv_sems = refs[2 * ng:]
        mx, my, mc = lax.axis_index("x"), lax.axis_index("y"), lax.axis_index("c")
        cps = []
        for g in range(ng):
            mine = _row_half(mc, 2 * fs[g].shape[1])
            cp = pltpu.make_async_remote_copy(src_ref=xs[g], dst_ref=outs[g].at[:, mine], send_sem=send_sems.at[g],
                                              recv_sem=recv_sems.at[g], device_id=(mx, my, 1 - mc), device_id_type=MESH)
            cp.start()
            cps.append(cp)
        for g in range(ng):
            theirs = outs[g].at[:, _row_half(1 - mc, 2 * fs[g].shape[1])]
            pltpu.make_async_remote_copy(src_ref=xs[g], dst_ref=theirs, send_sem=send_sems.at[g],
                                         recv_sem=recv_sems.at[g], device_id=(mx, my, 1 - mc),
                                         device_id_type=MESH).wait_recv()
        for cp in cps:
            cp.wait_send()

    outs = pl.pallas_call(
        body, name=name,
        out_shape=[jax.ShapeDtypeStruct((x.shape[0], 2 * x.shape[1], x.shape[2]), x.dtype) for x in fs],
        in_specs=[_ANY] * ng, out_specs=[_ANY] * ng,
        scratch_shapes=[pltpu.SemaphoreType.DMA((ng,)), pltpu.SemaphoreType.DMA((ng,))],
    )(*fs)
    mc = lax.axis_index("c")
    return [lax.dynamic_update_slice(o, x, (0, mc * x.shape[1], 0)) for o, x in zip(outs, fs)]


def _block_rows(r, w, itemsize=4, budget=4 << 20):
    for c in (r, 2048, 1024, 512, 256, 128, 64, 32, 16):
        if c <= r and r % c == 0 and c * w * itemsize <= budget:
            return c
    return r


def _pair_sum(name, g, got, cidx):
    ns, t, r, w = g.shape
    rh = r // 2
    bm = _block_rows(rh, w)
    nb = rh // bm

    def body(c_ref, a_ref, b_ref, o_ref):
        o_ref[...] = (a_ref[...].astype(F32) + b_ref[...].astype(F32)).astype(o_ref.dtype)

    blk = (None, None, bm, w)
    return pl.pallas_call(
        body, name=name,
        grid_spec=pltpu.PrefetchScalarGridSpec(
            num_scalar_prefetch=1, grid=(ns, t, nb),
            in_specs=[pl.BlockSpec(blk, lambda s, tt, i, c: (s, tt, c[0] * nb + i, 0)),
                      pl.BlockSpec(blk, lambda s, tt, i, c: (s, tt, i, 0))],
            out_specs=pl.BlockSpec(blk, lambda s, tt, i, c: (s, tt, i, 0))),
        out_shape=jax.ShapeDtypeStruct((ns, t, rh, w), BF16),
        compiler_params=_params(3 * _nbytes((bm, w), F32)),
    )(cidx, g, got)


def _chip_sum(name, p):
    ns, th, r, w = p.shape
    bm = _block_rows(r, w, budget=2 << 20)

    def body(p_ref, o_ref):
        acc = p_ref[0].astype(F32)
        for s in range(1, ns):
            acc = acc + p_ref[s].astype(F32)
        o_ref[...] = acc

    return pl.pallas_call(
        body, name=name, grid=(th, r // bm),
        in_specs=[pl.BlockSpec((ns, None, bm, w), lambda tt, i: (0, tt, i, 0))],
        out_specs=pl.BlockSpec((None, bm, w), lambda tt, i: (tt, i, 0)),
        out_shape=jax.ShapeDtypeStruct((th, r, w), F32),
        compiler_params=_params(ns * _nbytes((bm, w), BF16) + 2 * _nbytes((bm, w), F32)),
    )(p)


def _sum_leading(name, x):
    n = x.shape[0]

    def body(p_ref, o_ref):
        acc = p_ref[0]
        for s in range(1, n):
            acc = acc + p_ref[s]
        o_ref[...] = acc

    return pl.pallas_call(body, name=name, out_shape=jax.ShapeDtypeStruct(x.shape[1:], F32),
                          compiler_params=_params(2 * _nbytes(x.shape, F32)))(x)


def _reduce_scatter_begin(tag, gs, overlap):
    cidx = lax.axis_index("c").astype(jnp.int32).reshape(1)
    got = _pair_swap_groups(tag + "_pair_swap", gs)
    pair = [_pair_sum(f"{tag}_pair_sum{i}", g, r_, cidx) for i, (g, r_) in enumerate(zip(gs, got))]
    if overlap:
        return _gather_start(tag + "_scatter_start", pair, scatter=True)
    return _chip_scatter_groups(tag + "_chip_scatter", pair)


def _reduce_scatter_end(tag, state, overlap, after):
    if overlap:
        srcs, lands = _gather_wait(tag + "_scatter_wait", state, after, scatter=True)
        state = _place_own_part(lands, srcs)
    fin = [_chip_sum(f"{tag}_chip_sum{i}", p) for i, p in enumerate(state)]
    return _pair_merge_groups(tag + "_pair_merge", fin)


_GROUPS = ((("ffn1_wg", "ffn1_wu", "ffn2_wg", "ffn2_wu"), 1), (("ffn1_wd", "ffn2_wd"), 0), (("w_a",), 0),
           (("w_o",), 0), (("w_in",), 1), (("w_b",), 1))


def _shard_major(g, ax):
    k, n = g.shape
    if ax == 0:
        return g.reshape(4, k // 4, n)
    return g.reshape(k, 4, n // 4).transpose(1, 0, 2)


def _in_cols(d):
    o1 = 3 * DN_WIDTH
    o2 = o1 + DN_WIDTH
    o3 = o2 + 2 * DN_HEADS
    o4 = o3 + 3 * DA_WIDTH
    return dict(wq=(0, o1), wz=(o1, o2), wba=(o2, o3), wda=(o3, o4), wg=(o4, o4 + 2 * d))


def _mixer_weights(w_in, w_a, w_b, w_o, d):
    w = {k: w_in[:, a:b] for k, (a, b) in _in_cols(d).items()}
    w["wba"] = jnp.pad(w["wba"], ((0, 0), (0, LANES - 2 * DN_HEADS)))
    w["w_a"], w["w_b"], w["w_o"] = w_a, w_b, w_o
    return w


def _w_in_grad(wg):
    return jnp.concatenate([wg["wq"], wg["wz"], wg["wba"][:, :2 * DN_HEADS], wg["wda"], wg["wg"]], axis=1)


def _adam_math(wv, gv, mv, vv):
    mn = ADAM_B1 * mv + (1.0 - ADAM_B1) * gv
    vn = ADAM_B2 * vv + (1.0 - ADAM_B2) * jnp.square(gv)
    m_hat = mn / (1.0 - ADAM_B1 ** ADAM_STEP)
    v_hat = vn / (1.0 - ADAM_B2 ** ADAM_STEP)
    delta = -ADAM_LR * (m_hat / (jnp.sqrt(v_hat) + ADAM_EPS) + ADAM_WD * wv)
    return delta, mn, vn


def _adamw(name, w, g, m, v):
    shape = w.shape
    cols = shape[-1]
    w2, g2, m2, v2 = (t.reshape(-1, cols) for t in (w, g, m, v))
    rows = w2.shape[0]
    bm = _pick(rows, (256, 128, 64, 32, 16, 8)) if rows >= 8 else rows
    delta, mn, vn = _rowwise(name, lambda *t: (_adam_math(*t), ()), [w2, g2, m2, v2], [], [(cols, F32)] * 3, bm=bm)
    return delta.reshape(shape), mn.reshape(shape), vn.reshape(shape)


def _adamw_leading(name, w, g, m, v):
    n = w.shape[0]
    padded_row = -(-w.shape[1] // 8) * 8 * w.shape[2] * 4
    bm = max(c for c in range(1, n + 1) if n % c == 0 and (c * padded_row <= (1 << 20) or c == 1))

    def body(w_ref, g_ref, m_ref, v_ref, d_ref, mo_ref, vo_ref):
        d_ref[...], mo_ref[...], vo_ref[...] = _adam_math(w_ref[...], g_ref[...], m_ref[...], v_ref[...])

    spec = pl.BlockSpec((bm,) + w.shape[1:], lambda i: (i, 0, 0))
    return pl.pallas_call(
        body, name=name, grid=(n // bm,), in_specs=[spec] * 4, out_specs=[spec] * 3,
        out_shape=[jax.ShapeDtypeStruct(w.shape, F32)] * 3, compiler_params=_params(7 * bm * padded_row),
    )(w, g, m, v)


def _adamw_stacked(name, w, m, v, gstacks, slot):
    depth, r, cdim = w.shape
    bm = _block_rows(r, cdim, budget=1 << 20)

    def body(w_ref, m_ref, v_ref, *rest):
        g_refs, (go_ref, d_ref, mo_ref, vo_ref) = rest[:depth], rest[depth:]
        layer = pl.program_id(0)
        gv = g_refs[0][...]
        for l in range(1, depth):
            gv = jnp.where(layer == l, g_refs[l][...], gv)
        go_ref[...] = gv
        d_ref[...], mo_ref[...], vo_ref[...] = _adam_math(w_ref[...], gv, m_ref[...], v_ref[...])

    nat = pl.BlockSpec((None, bm, cdim), lambda l, i: (l, i, 0))
    return pl.pallas_call(
        body, name=name, grid=(depth, r // bm),
        in_specs=[nat, nat, nat] + [pl.BlockSpec((None, bm, cdim), lambda l, i: (slot, i, 0))] * depth,
        out_specs=[nat] * 4, out_shape=[jax.ShapeDtypeStruct(w.shape, F32)] * 4,
        compiler_params=_params((7 + depth) * _nbytes((bm, cdim), F32)),
    )(w, m, v, *gstacks)


def kernel(x, c, ada_w, ada_b, ln_ffn1, ln_mix, ln_ffn2, ffn1_wg, ffn1_wu, ffn1_wd, w_in, conv_w, a_log, dt_bias, dn_norm, w_a, w_b, w_o, ffn2_wg, ffn2_wu, ffn2_wd, final_norm, loss_target, m_ada_w, m_ada_b, m_ln_ffn1, m_ln_mix, m_ln_ffn2, m_ffn1_wg, m_ffn1_wu, m_ffn1_wd, m_w_in, m_conv_w, m_a_log, m_dt_bias, m_dn_norm, m_w_a, m_w_b, m_w_o, m_ffn2_wg, m_ffn2_wu, m_ffn2_wd, m_final_norm, v_ada_w, v_ada_b, v_ln_ffn1, v_ln_mix, v_ln_ffn2, v_ffn1_wg, v_ffn1_wu, v_ffn1_wd, v_w_in, v_conv_w, v_a_log, v_dt_bias, v_dn_norm, v_w_a, v_w_b, v_w_o, v_ffn2_wg, v_ffn2_wu, v_ffn2_wd, v_final_norm):
    names = ["ada_w", "ada_b", "ln_ffn1", "ln_mix", "ln_ffn2", "ffn1_wg", "ffn1_wu", "ffn1_wd", "w_in", "conv_w",
             "a_log", "dt_bias", "dn_norm", "w_a", "w_b", "w_o", "ffn2_wg", "ffn2_wu", "ffn2_wd", "final_norm"]
    wts = dict(zip(names, (ada_w, ada_b, ln_ffn1, ln_mix, ln_ffn2, ffn1_wg, ffn1_wu, ffn1_wd, w_in, conv_w, a_log,
                           dt_bias, dn_norm, w_a, w_b, w_o, ffn2_wg, ffn2_wu, ffn2_wd, final_norm)))
    mom = dict(zip(names, (m_ada_w, m_ada_b, m_ln_ffn1, m_ln_mix, m_ln_ffn2, m_ffn1_wg, m_ffn1_wu, m_ffn1_wd, m_w_in,
                           m_conv_w, m_a_log, m_dt_bias, m_dn_norm, m_w_a, m_w_b, m_w_o, m_ffn2_wg, m_ffn2_wu,
                           m_ffn2_wd, m_final_norm)))
    var = dict(zip(names, (v_ada_w, v_ada_b, v_ln_ffn1, v_ln_mix, v_ln_ffn2, v_ffn1_wg, v_ffn1_wu, v_ffn1_wd, v_w_in,
                           v_conv_w, v_a_log, v_dt_bias, v_dn_norm, v_w_a, v_w_b, v_w_o, v_ffn2_wg, v_ffn2_wu,
                           v_ffn2_wd, v_final_norm)))
    _, s, d = x.shape
    depth = ada_w.shape[0]
    mx, my, mc = lax.axis_index("x"), lax.axis_index("y"), lax.axis_index("c")
    chip = 2 * mx + my
    me = 2 * chip + mc
    nshard = ada_w.shape[2]

    cact = _rowwise("c_silu", lambda cv: ((_silu(cv),), ()), [jnp.pad(c, ((0, 7), (0, 0)))], [], [(d, F32)], bm=8)[0]
    c_all = _allgather8("ag_c", cact)[:, 0, :]
    conv_all = _allgather8("ag_conv", jnp.pad(conv_w.reshape(depth * DN_CONV, -1), ((0, 8 - depth * DN_CONV), (0, 0))))
    conv_full = jnp.concatenate([conv_all[2 * j, :depth * DN_CONV] for j in range(4)], axis=1)
    conv_full = conv_full.reshape(depth, DN_CONV, 3 * DN_WIDTH)
    layer_shards = [[jnp.stack([wts[nm][l].astype(BF16) for nm in nms], axis=0) for nms, _ in _GROUPS]
                    for l in range(depth)]
    gathered0 = _gather_groups("ag_weights0", layer_shards[0])
    rows_of = lambda st: st[:, 0].reshape(-1, st.shape[-1])
    cols_of = lambda st: jnp.concatenate([st[j, 0] for j in range(4)], axis=1)

    def layer_weights(l, after):
        if l == 0:
            got = gathered0
        else:
            srcs, lands = _gather_wait(f"ag_weights{l}_wait", started[l], after)
            got = _pair_forward_groups(f"ag_weights{l}_pair", lands, srcs)
        ga, gb, g_wa, g_wo, g_win, g_wb = got
        return ga, gb, _mixer_weights(cols_of(g_win), rows_of(g_wa), cols_of(g_wb), rows_of(g_wo), d)

    c16 = jnp.pad(c_all, ((0, 8), (0, 0))).astype(BF16)
    parts = []
    for l in range(depth):
        bias = lax.dynamic_slice(ada_b[l], (chip * nshard,), (nshard,)).reshape(1, nshard)
        (mp,) = _matmul(f"ada_fwd{l}", c16, ada_w[l].astype(BF16), epi_bcast=[bias], epi=lambda acc, b: (acc + b,))
        parts.append(mp)
    mod_all = _allgather8("ag_mod", jnp.concatenate(parts, axis=0))
    mod_rows = jnp.concatenate([mod_all[2 * j] for j in range(4)], axis=1)
    mod = jnp.stack([lax.dynamic_index_in_dim(mod_rows, l * 16 + me, axis=0, keepdims=False) for l in range(depth)])

    gathered0, later, mod, conv_full = lax.optimization_barrier((gathered0, layer_shards[1:], mod, conv_full))
    started = {l: _gather_start(f"ag_weights{l}_start", later[l - 1]) for l in range(1, depth)}
    for st in started.values():
        mod = mod + st["token"][0, 0]
    small = dict(conv_w=conv_full, a_log=a_log, dt_bias=dt_bias, dn_norm=dn_norm, ln_ffn1=ln_ffn1, ln_mix=ln_mix,
                 ln_ffn2=ln_ffn2, final_norm=final_norm)
    ffn_names = _GROUPS[0][0] + _GROUPS[1][0]
    rs_state, first_layer = {}, {}

    def on_layer_grads(l, wg):
        wg["w_in"] = _w_in_grad(wg)
        gs = [jnp.stack([wg[nm] if nm in ffn_names else _shard_major(wg[nm], ax) for nm in nms], axis=1)
              for nms, ax in _GROUPS]
        if l == 0:
            first_layer["gs"] = gs
            return None
        rs_state[l] = _reduce_scatter_begin(f"rs{l}", gs, overlap=True)
        return rs_state[l]["token"][0, 0]

    loss_part, dx, dmod, sgrads, d_fnorm = _local_step(x[0], loss_target[0], mod, layer_weights, small,
                                                       on_layer_grads)

    dmod_all = _allgather8("ag_dmod", jnp.pad(dmod, ((0, 8 - depth), (0, 0))))
    smalls = [loss_part.reshape(1), d_fnorm]
    for l in range(depth):
        sg = sgrads[l]
        smalls += [sg["ln_ffn1"], sg["ln_mix"], sg["ln_ffn2"], sg["a_log"], sg["dt_bias"], sg["dn_norm"],
                   sg["conv_w"].reshape(-1)]
    sizes = [t.shape[0] for t in smalls]
    tile = 8 * LANES
    flat = jnp.concatenate([jnp.pad(t, (0, (-t.shape[0]) % tile)).reshape(-1, LANES) for t in smalls], axis=0)
    small_all = _allgather8("ag_small", flat)
    dmod_all, small_all, gs0 = lax.optimization_barrier((dmod_all, small_all, first_layer["gs"]))
    rs_state[0] = _reduce_scatter_begin("rs0", gs0, overlap=True)
    started0 = rs_state[0]["token"][0, 0]
    dmod_all = dmod_all + started0
    small_all = small_all + started0

    g_ada_w, g_ada_b = [], []
    for l in range(depth):
        dm_l = dmod_all[:, l, :]
        (gb_l,) = _rowwise(f"ada_b_grad{l}", lambda v: ((), (jnp.sum(v, axis=0, keepdims=True),)), [dm_l], [], [],
                           [(1, N_ADA * d)], bm=8)
        g_ada_b.append(gb_l[0])
        dm_sh = lax.dynamic_slice(dm_l, (0, chip * nshard), (8, nshard))
        (gw_l,) = _matmul(f"ada_w_grad{l}", c16, jnp.pad(dm_sh, ((0, 8), (0, 0))).astype(BF16), ta=True)
        g_ada_w.append(gw_l)
    grads = dict(ada_w=jnp.stack(g_ada_w), ada_b=jnp.stack(g_ada_b))

    tot = _sum_leading("small_sum", small_all)
    offs, acc = [], 0
    for n_ in sizes:
        offs.append(acc)
        acc += -(-n_ // tile) * 8
    take = lambda i: tot[offs[i]:offs[i] + -(-sizes[i] // tile) * 8].reshape(-1)[:sizes[i]]
    loss = take(0)[0]
    grads["final_norm"] = take(1)
    per = 7
    for key_i, key in enumerate(["ln_ffn1", "ln_mix", "ln_ffn2", "a_log", "dt_bias", "dn_norm"]):
        grads[key] = jnp.stack([take(2 + per * l + key_i) for l in range(depth)])
    conv_g = jnp.stack([take(2 + per * l + 6).reshape(DN_CONV, 3 * DN_WIDTH) for l in range(depth)])
    csh = conv_w.shape[2]
    grads["conv_w"] = lax.dynamic_slice(conv_g, (0, 0, chip * csh), (depth, DN_CONV, csh))

    deltas, new_m, new_v = {}, {}, {}
    big = {nm for nms, _ in _GROUPS for nm in nms}
    for name in names:
        if name in big:
            continue
        wv, gv, mv, vv = wts[name], grads[name], mom[name], var[name]
        if wv.ndim == 1:
            wv, gv, mv, vv = (t.reshape(-1, LANES) for t in (wv, gv, mv, vv))
        dl, mn, vn = _adamw("adamw_" + name, wv, gv, mv, vv)
        deltas[name], new_m[name], new_v[name] = (t.reshape(wts[name].shape) for t in (dl, mn, vn))

    reduced = {l: _reduce_scatter_end(f"rs{l}", rs_state[l], True, dx) for l in range(depth - 1, 0, -1)}
    reduced[0] = _reduce_scatter_end("rs0", rs_state[0], True, deltas["ada_w"])

    for gi, (nms, ax) in enumerate(_GROUPS):
        for q, nm in enumerate(nms):
            wv, mv, vv = wts[nm], mom[nm], var[nm]
            per_layer = [reduced[l][gi][q] for l in range(depth)]
            if ax == 1 and wv.shape[2] % LANES and nm != "w_in":
                tr = lambda t: jnp.swapaxes(t, 1, 2)
                gt = jnp.stack([g.T for g in per_layer], axis=0)
                dl, mn, vn = _adamw("adamw_" + nm, tr(wv), gt, tr(mv), tr(vv))
                grads[nm], deltas[nm], new_m[nm], new_v[nm] = tr(gt), tr(dl), tr(mn), tr(vn)
            elif nm == "w_in" and wv.shape[2] % LANES:
                tr = lambda t: jnp.transpose(t, (2, 0, 1))
                back = lambda t: jnp.transpose(t, (1, 2, 0))
                gt = jnp.stack([g.T for g in per_layer], axis=1)
                dl, mn, vn = _adamw_leading("adamw_" + nm, tr(wv), gt, tr(mv), tr(vv))
                grads[nm], deltas[nm], new_m[nm], new_v[nm] = back(gt), back(dl), back(mn), back(vn)
            else:
                grads[nm], deltas[nm], new_m[nm], new_v[nm] = _adamw_stacked(
                    "adamw_" + nm, wv, mv, vv, [reduced[l][gi] for l in range(depth)], q)

    return (loss, dx.reshape(1, s, d), *[grads[n_] for n_ in names], *[deltas[n_] for n_ in names],
            *[new_m[n_] for n_ in names], *[new_v[n_] for n_ in names])
```

```python
import functools

import jax
import jax.numpy as jnp
from jax import lax
from jax.experimental import pallas as pl
from jax.experimental.pallas import tpu as pltpu

F32 = jnp.float32
BF16 = jnp.bfloat16
MESH = pl.DeviceIdType.MESH

NORM_EPS = 1e-6
DN_HEADS, DN_DIM, DN_CHUNK, DN_CONV = 8, 128, 64, 4
DN_WIDTH = DN_HEADS * DN_DIM
DA_HEADS, DA_DIM, DA_BLOCK = 12, 64, 128
DA_WIDTH = DA_HEADS * DA_DIM
DA_PATTERNS = ((128, 1), (512, 4), (2048, 16))
ALIBI_MAX_EXP = 8.0
N_ADA = 9
LANES = 128
V7X_VMEM_BYTES = 64 << 20
ADAM_LR, ADAM_B1, ADAM_B2, ADAM_EPS, ADAM_WD, ADAM_STEP = 0.001, 0.9, 0.999, 1e-08, 0.01, 10
NEG = -1e30
HI = lax.Precision.HIGHEST
NN = (((1,), (0,)), ((), ()))
NT = (((1,), (1,)), ((), ()))
TN = (((0,), (0,)), ((), ()))


def _nbytes(shape, dtype):
    n = 1
    for s in shape:
        n *= s
    return n * jnp.dtype(dtype).itemsize


def _params(block_bytes, scratch_bytes=0):
    need = 2 * block_bytes + scratch_bytes
    lim = min(max(need + need // 4 + (4 << 20), 32 << 20), V7X_VMEM_BYTES - (6 << 20))
    return pltpu.CompilerParams(vmem_limit_bytes=int(lim))


def _pick(n, cands):
    for c in cands:
        if c <= n and n % c == 0:
            return c
    return n


def _sigmoid(x):
    return jax.nn.sigmoid(x)


def _silu(x):
    return x * jax.nn.sigmoid(x)


def _softplus(x):
    return jnp.maximum(x, 0.0) + jnp.log(1.0 + jnp.exp(-jnp.abs(x)))


def _rowwise(name, fn, rows, bcast, row_outs, red_outs=(), bm=512):
    rows = [r if isinstance(r, tuple) else (r, r.shape[1], 0) for r in rows]
    s = rows[0][0].shape[0]
    bm = _pick(s, (bm, 128, 64, 32, 16, 8))
    nr, nb, no, nd = len(rows), len(bcast), len(row_outs), len(red_outs)
    in_specs = [pl.BlockSpec((bm, w), functools.partial(lambda i, ci: (i, ci), ci=ci)) for (_, w, ci) in rows]
    in_specs += [pl.BlockSpec(b.shape, lambda i: (0, 0)) for b in bcast]
    out_shape = [jax.ShapeDtypeStruct((s, w), dt) for (w, dt) in row_outs]
    out_shape += [jax.ShapeDtypeStruct((r, w), F32) for (r, w) in red_outs]
    out_specs = [pl.BlockSpec((bm, w), lambda i: (i, 0)) for (w, _) in row_outs]
    out_specs += [pl.BlockSpec((r, w), lambda i: (0, 0)) for (r, w) in red_outs]

    def body(*refs):
        ins = [r[...] for r in refs[:nr + nb]]
        outs = refs[nr + nb:nr + nb + no]
        reds = refs[nr + nb + no:]
        ov, rv = fn(*ins)
        for o, v in zip(outs, ov):
            o[...] = v.astype(o.dtype)
        if nd:
            @pl.when(pl.program_id(0) == 0)
            def _():
                for r in reds:
                    r[...] = jnp.zeros(r.shape, F32)
            for r, v in zip(reds, rv):
                r[...] += v.astype(F32)

    blk = sum(_nbytes((bm, w), a.dtype) for (a, w, _) in rows) + sum(_nbytes(b.shape, b.dtype) for b in bcast)
    blk += sum(_nbytes((bm, w), dt) for (w, dt) in row_outs) + sum(_nbytes(r, F32) for r in red_outs)
    res = pl.pallas_call(
        body, name=name, grid=(s // bm,), in_specs=in_specs, out_specs=out_specs, out_shape=out_shape,
        compiler_params=_params(3 * blk),
    )(*[a for (a, _, _) in rows], *bcast)
    return res


def _matmul(name, a, b, *, ta=False, tb=False, outs=(F32,), epi=None, epi_rows=(), epi_bcast=(),
            bm=None, bn=None, bk=None):
    if ta:
        k, m = a.shape
    else:
        m, k = a.shape
    n = b.shape[0] if tb else b.shape[1]
    assert (b.shape[1] if tb else b.shape[0]) == k, (name, a.shape, b.shape)
    if bm is None:
        bm = _pick(m, (1024, 1408, 768, 512, 384, 256, 128)) if ta else _pick(m, (1024, 512, 256, 128, 64, 32, 16))
    if bk is None:
        bk = k if k <= 3072 else _pick(k, (2816, 2048, 1024, 512))
        if ta:
            bk = _pick(k, (1024, 512, 256, 128, 64, 32, 16))
    if bn is None:
        bn = _pick(n, (1024, 768, 512, 384, 256, 128) if bk <= 2048 else (512, 384, 256, 128))
    nk = k // bk
    dims = TN if ta else (NT if tb else NN)
    a_spec = pl.BlockSpec((bk, bm), lambda i, j, kk: (kk, i)) if ta else pl.BlockSpec((bm, bk), lambda i, j, kk: (i, kk))
    b_spec = pl.BlockSpec((bn, bk), lambda i, j, kk: (j, kk)) if tb else pl.BlockSpec((bk, bn), lambda i, j, kk: (kk, j))
    in_specs = [a_spec, b_spec]
    in_specs += [pl.BlockSpec((bm, bn), lambda i, j, kk: (i, j)) for _ in epi_rows]
    in_specs += [pl.BlockSpec((1, bn), lambda i, j, kk: (0, j)) for _ in epi_bcast]
    out_shape = [jax.ShapeDtypeStruct((m, n), dt) for dt in outs]
    out_specs = [pl.BlockSpec((bm, bn), lambda i, j, kk: (i, j)) for _ in outs]
    ner, neb, no = len(epi_rows), len(epi_bcast), len(outs)

    def body(*refs):
        a_ref, b_ref = refs[0], refs[1]
        extra = refs[2:2 + ner + neb]
        out_refs = refs[2 + ner + neb:2 + ner + neb + no]
        prod = lax.dot_general(a_ref[...], b_ref[...], dims, preferred_element_type=F32)

        def finish(acc):
            vals = epi(acc, *[r[...] for r in extra]) if epi is not None else (acc,)
            for o, v in zip(out_refs, vals):
                o[...] = v.astype(o.dtype)

        if nk == 1:
            finish(prod)
        else:
            acc_ref = refs[-1]
            kk = pl.program_id(2)

            @pl.when(kk == 0)
            def _():
                acc_ref[...] = prod

            @pl.when(kk > 0)
            def _():
                acc_ref[...] += prod

            @pl.when(kk == nk - 1)
            def _():
                finish(acc_ref[...])

    blk = _nbytes((bm, bk), a.dtype) + _nbytes((bk, bn), b.dtype)
    blk += sum(_nbytes((bm, bn), r.dtype) for r in epi_rows) + sum(_nbytes((bm, bn), dt) for dt in outs)
    scratch = [pltpu.VMEM((bm, bn), F32)] if nk > 1 else []
    res = pl.pallas_call(
        body, name=name, grid=(m // bm, n // bn, nk), in_specs=in_specs, out_specs=out_specs,
        out_shape=out_shape, scratch_shapes=scratch,
        compiler_params=_params(blk, 3 * _nbytes((bm, bn), F32)),
    )(a, b, *epi_rows, *epi_bcast)
    return res


def _mm_core(name, grid, nk, pairs, out_defs, acc_shape, epi=None, epi_ins=()):
    npair, nep, no = len(pairs), len(epi_ins), len(out_defs)

    def body(*refs):
        extra = refs[2 * npair:2 * npair + nep]
        out_refs = refs[2 * npair + nep:2 * npair + nep + no]
        prod = None
        for p in range(npair):
            d = lax.dot_general(refs[2 * p][...], refs[2 * p + 1][...], pairs[p][4], preferred_element_type=F32)
            prod = d if prod is None else prod + d

        def finish(acc):
            vals = epi(acc, *[r[...] for r in extra]) if epi is not None else (acc,)
            for o, v in zip(out_refs, vals):
                o[...] = v.astype(o.dtype)

        if nk == 1:
            finish(prod)
        else:
            acc_ref = refs[-1]
            kk = pl.program_id(2)

            @pl.when(kk == 0)
            def _():
                acc_ref[...] = prod

            @pl.when(kk > 0)
            def _():
                acc_ref[...] += prod

            @pl.when(kk == nk - 1)
            def _():
                finish(acc_ref[...])

    def blk_bytes(spec, dtype):
        return _nbytes([s for s in spec.block_shape if s is not None], dtype)

    blk = sum(blk_bytes(sa, a.dtype) + blk_bytes(sb, b.dtype) for (a, sa, b, sb, _) in pairs)
    blk += sum(blk_bytes(sp, arr.dtype) for (arr, sp) in epi_ins) + sum(blk_bytes(sp, dt) for (_, dt, sp) in out_defs)
    ins, in_specs = [], []
    for (a, sa, b, sb, _) in pairs:
        ins += [a, b]
        in_specs += [sa, sb]
    ins += [arr for (arr, _) in epi_ins]
    in_specs += [sp for (_, sp) in epi_ins]
    return pl.pallas_call(
        body, name=name, grid=grid, in_specs=in_specs, out_specs=[sp for (_, _, sp) in out_defs],
        out_shape=[jax.ShapeDtypeStruct(sh, dt) for (sh, dt, _) in out_defs],
        scratch_shapes=[pltpu.VMEM(acc_shape, F32)] if nk > 1 else [],
        compiler_params=_params(blk, 3 * _nbytes(acc_shape, F32)),
    )(*ins)


def _rms_mod(h, ln, sh, sc):
    n = h * lax.rsqrt(jnp.mean(h * h, axis=-1, keepdims=True) + NORM_EPS) * ln
    return n * (1.0 + sc) + sh


def _swiglu_act(g, u):
    return _silu(g.astype(F32)) * u.astype(F32)


def _dn_prep(yc, pba, alog, dtb):
    act = _silu(yc)
    parts = []
    for idx in range(2 * DN_HEADS):
        seg = act[:, idx * DN_DIM:(idx + 1) * DN_DIM]
        seg = seg * lax.rsqrt(jnp.sum(seg * seg, axis=-1, keepdims=True) + NORM_EPS)
        if idx < DN_HEADS:
            seg = seg * (DN_DIM ** -0.5)
        parts.append(seg)
    parts.append(act[:, 2 * DN_WIDTH:])
    qkvn = jnp.concatenate(parts, axis=1)
    lane = lax.broadcasted_iota(jnp.int32, pba.shape, 1)
    beta = _sigmoid(pba)
    g = -jnp.exp(alog) * _softplus(pba + dtb)
    gb = jnp.where(lane < DN_HEADS, beta, jnp.where(lane < 2 * DN_HEADS, g, 0.0))
    return qkvn, gb


def _dn_outnorm(o_a, z, dn):
    parts = []
    for h in range(DN_HEADS):
        seg = o_a[:, h * DN_DIM:(h + 1) * DN_DIM]
        seg = seg * lax.rsqrt(jnp.mean(seg * seg, axis=-1, keepdims=True) + NORM_EPS) * dn
        parts.append(seg)
    return jnp.concatenate(parts, axis=1) * _silu(z)


def _shift_down(x, halo8, s):
    r = pltpu.roll(x, s, axis=0)
    top = pltpu.roll(halo8, s, axis=0)
    i8 = lax.broadcasted_iota(jnp.int32, top.shape, 0)
    return jnp.concatenate([jnp.where(i8 < s, top, r[0:8]), r[8:]], axis=0)


def _shift_up(x, halo8, s):
    m = x.shape[0]
    r = pltpu.roll(x, m - s, axis=0)
    bot = pltpu.roll(halo8, 8 - s, axis=0)
    i8 = lax.broadcasted_iota(jnp.int32, bot.shape, 0)
    return jnp.concatenate([r[:m - 8], jnp.where(i8 >= 8 - s, bot, r[m - 8:])], axis=0)


def _conv_prep_fwd(name, pq, convw8, pba, alog, dtb, bm=256):
    s, w = pq.shape
    nblk = s // bm
    hb = bm // 16

    def body(x_ref, halo_ref, w_ref, pba_ref, alog_ref, dtb_ref, yc_ref, qkv_ref, gb_ref):
        i = pl.program_id(0)
        x = x_ref[...].astype(F32)
        halo = jnp.where(i > 0, halo_ref[...].astype(F32)[8:16], 0.0)
        cw = w_ref[...]
        y = x * cw[DN_CONV - 1:DN_CONV]
        for sft in range(1, DN_CONV):
            y = y + _shift_down(x, halo, sft) * cw[DN_CONV - 1 - sft:DN_CONV - sft]
        ycb = y.astype(BF16)
        yc_ref[...] = ycb
        qkvn, gb = _dn_prep(ycb.astype(F32), pba_ref[...], alog_ref[...], dtb_ref[...])
        qkv_ref[...] = qkvn.astype(BF16)
        gb_ref[...] = gb

    blk = 3 * _nbytes((bm, w), BF16) + 4 * _nbytes((bm, w), F32)
    return pl.pallas_call(
        body, name=name, grid=(nblk,),
        in_specs=[pl.BlockSpec((bm, w), lambda i: (i, 0)),
                  pl.BlockSpec((16, w), lambda i: (jnp.maximum(i * hb - 1, 0), 0)),
                  pl.BlockSpec(convw8.shape, lambda i: (0, 0)),
                  pl.BlockSpec((bm, LANES), lambda i: (i, 0)),
                  pl.BlockSpec((1, LANES), lambda i: (0, 0)),
                  pl.BlockSpec((1, LANES), lambda i: (0, 0))],
        out_specs=[pl.BlockSpec((bm, w), lambda i: (i, 0)), pl.BlockSpec((bm, w), lambda i: (i, 0)),
                   pl.BlockSpec((bm, LANES), lambda i: (i, 0))],
        out_shape=[jax.ShapeDtypeStruct((s, w), BF16), jax.ShapeDtypeStruct((s, w), BF16),
                   jax.ShapeDtypeStruct((s, LANES), F32)],
        compiler_params=_params(blk),
    )(pq, pq, convw8, pba, alog, dtb)


def _conv_bwd(name, dyc, pq, convw8, bm=256):
    s, w = pq.shape
    nblk = s // bm
    hb = bm // 16

    def body(dy_ref, dyn_ref, x_ref, xh_ref, w_ref, dx_ref, dw_ref):
        i = pl.program_id(0)
        dy = dy_ref[...].astype(F32)
        nxt = jnp.where(i < nblk - 1, dyn_ref[...].astype(F32)[0:8], 0.0)
        x = x_ref[...].astype(F32)
        halo = jnp.where(i > 0, xh_ref[...].astype(F32)[8:16], 0.0)
        cw = w_ref[...]
        dx = dy * cw[DN_CONV - 1:DN_CONV]
        for sft in range(1, DN_CONV):
            dx = dx + _shift_up(dy, nxt, sft) * cw[DN_CONV - 1 - sft:DN_CONV - sft]
        dx_ref[...] = dx.astype(dx_ref.dtype)
        r8 = lax.broadcasted_iota(jnp.int32, (8, w), 0)
        dw = jnp.zeros((8, w), F32)
        for j in range(DN_CONV):
            sft = DN_CONV - 1 - j
            xs = x if sft == 0 else _shift_down(x, halo, sft)
            dw = dw + jnp.where(r8 == j, jnp.sum(dy * xs, axis=0, keepdims=True), 0.0)

        @pl.when(i == 0)
        def _():
            dw_ref[...] = jnp.zeros((8, w), F32)
        dw_ref[...] += dw

    blk = 4 * _nbytes((bm, w), BF16) + 5 * _nbytes((bm, w), F32)
    return pl.pallas_call(
        body, name=name, grid=(nblk,),
        in_specs=[pl.BlockSpec((bm, w), lambda i: (i, 0)),
                  pl.BlockSpec((16, w), lambda i: (jnp.minimum((i + 1) * hb, s // 16 - 1), 0)),
                  pl.BlockSpec((bm, w), lambda i: (i, 0)),
                  pl.BlockSpec((16, w), lambda i: (jnp.maximum(i * hb - 1, 0), 0)),
                  pl.BlockSpec(convw8.shape, lambda i: (0, 0))],
        out_specs=[pl.BlockSpec((bm, w), lambda i: (i, 0)), pl.BlockSpec((8, w), lambda i: (0, 0))],
        out_shape=[jax.ShapeDtypeStruct((s, w), BF16), jax.ShapeDtypeStruct((8, w), F32)],
        compiler_params=_params(blk),
    )(dyc, dyc, pq, pq, convw8)


BNN = (((2,), (1,)), ((0,), (0,)))
BNT = (((2,), (2,)), ((0,), (0,)))
BTN = (((1,), (1,)), ((0,), (0,)))


def _raw_dot_1pass(a, b, dims):
    return lax.dot_general(a.astype(BF16), b.astype(BF16), dims, preferred_element_type=F32)


def _raw_dot_3pass(a, b, dims):
    ah = a.astype(BF16)
    al = (a - ah.astype(F32)).astype(BF16)
    bh = b.astype(BF16)
    bl = (b - bh.astype(F32)).astype(BF16)
    d = lambda x, y: lax.dot_general(x, y, dims, preferred_element_type=F32)
    return d(ah, bh) + (d(ah, bl) + d(al, bh))


def _with_same_precision_vjp(raw):
    @functools.partial(jax.custom_vjp, nondiff_argnums=(2,))
    def dot(a, b, dims):
        return raw(a, b, dims)

    def fwd(a, b, dims):
        return raw(a, b, dims), (a, b)

    def bwd(dims, res, ct):
        a, b = res
        if dims == BNN:
            return raw(ct, b, BNT), raw(a, ct, BTN)
        if dims == BNT:
            return raw(ct, b, BNN), raw(ct, a, BTN)
        assert dims == BTN
        return raw(b, ct, BNT), raw(a, ct, BNN)

    dot.defvjp(fwd, bwd)
    return dot


_dot_1pass_vjp = _with_same_precision_vjp(_raw_dot_1pass)
_dot_3pass_vjp = _with_same_precision_vjp(_raw_dot_3pass)


def _dot_bf16(a, b, dims=BNN):
    return _dot_1pass_vjp(a, b, dims)


def _dot_3pass(a, b, dims=BNN):
    return _dot_3pass_vjp(a, b, dims)


def _neumann_inverse(x):
    h, c, _ = x.shape
    eye = lax.broadcasted_iota(jnp.int32, (h, c, c), 1) == lax.broadcasted_iota(jnp.int32, (h, c, c), 2)
    t = jnp.where(eye, 1.0, 0.0) + x
    p = x
    for _ in range(5):
        p = _raw_dot_3pass(p, p, BNN)
        t = t + _raw_dot_3pass(t, p, BNN)
    return t


@jax.custom_vjp
def _known_inverse(x, t):
    return t


def _known_inverse_fwd(x, t):
    return t, t


def _known_inverse_bwd(t, ct):
    return _raw_dot_3pass(_raw_dot_3pass(t, ct, BTN), t, BNT), jnp.zeros_like(t)


_known_inverse.defvjp(_known_inverse_fwd, _known_inverse_bwd)


def _delta_chunk(q, k, v, gcol, bcol, state, t_known=None):
    h, c, _ = q.shape
    row = lax.broadcasted_iota(jnp.int32, (h, c, c), 1)
    col = lax.broadcasted_iota(jnp.int32, (h, c, c), 2)
    incl, strict, eye = row >= col, row > col, row == col
    g_b = jnp.broadcast_to(gcol, (h, c, c))
    gc_row = jnp.sum(jnp.where(row <= col, g_b, 0.0), axis=1, keepdims=True)
    g_r = jnp.sum(jnp.where(eye, g_b, 0.0), axis=1, keepdims=True)
    gc_col = jnp.sum(jnp.where(incl, jnp.broadcast_to(g_r, (h, c, c)), 0.0), axis=2, keepdims=True)
    decay = jnp.exp(jnp.where(incl, gc_col - gc_row, NEG))
    kb = k * bcol
    vb = v * bcol
    x = -jnp.where(strict, _dot_bf16(kb, k, BNT) * decay, 0.0)
    t = _neumann_inverse(x) if t_known is None else _known_inverse(x, t_known)
    eg = jnp.exp(gc_col)
    u = _dot_3pass(t, vb)
    w = _dot_3pass(t, kb * eg)
    qk = _dot_bf16(q, k, BNT) * decay
    v_new = u - _dot_bf16(w, state)
    o = _dot_bf16(q * eg, state) + _dot_bf16(qk, v_new)
    g_last = jnp.sum(g_r, axis=2, keepdims=True)
    new_state = state * jnp.exp(g_last) + _dot_bf16(k * jnp.exp(g_last - gc_col), v_new, BTN)
    return o, new_state, t


def _lane_col(blk, idx):
    lane = lax.broadcasted_iota(jnp.int32, blk.shape, 1)
    return jnp.sum(jnp.where(lane == idx, blk, 0.0), axis=1, keepdims=True)


def _dn_heads(ref, base):
    return jnp.stack([ref[:, base + h * DN_DIM:base + (h + 1) * DN_DIM] for h in range(DN_HEADS)], axis=0).astype(F32)


def _dn_cols(gbv, base):
    return jnp.stack([_lane_col(gbv, base + h) for h in range(DN_HEADS)], axis=0)


def _delta_fwd(name, qkvn, gb):
    s = qkvn.shape[0]
    n = s // DN_CHUNK
    c = DN_CHUNK

    def body(qkv_ref, gb_ref, o_ref, st_ref, t_ref, state):
        @pl.when(pl.program_id(0) == 0)
        def _():
            state[...] = jnp.zeros(state.shape, F32)

        gbv = gb_ref[...]
        st = state[...]
        st_ref[0] = st
        o, new, t = _delta_chunk(_dn_heads(qkv_ref, 0), _dn_heads(qkv_ref, DN_WIDTH), _dn_heads(qkv_ref, 2 * DN_WIDTH),
                                 _dn_cols(gbv, DN_HEADS), _dn_cols(gbv, 0), st)
        for h in range(DN_HEADS):
            o_ref[:, h * DN_DIM:(h + 1) * DN_DIM] = o[h]
        t_ref[0] = t
        state[...] = new

    blk = _nbytes((c, 3 * DN_WIDTH), BF16) + _nbytes((c, LANES), F32) + _nbytes((c, DN_WIDTH), F32)
    blk += _nbytes((DN_HEADS, DN_DIM, DN_DIM), F32) + _nbytes((DN_HEADS, c, c), F32)
    return pl.pallas_call(
        body, name=name, grid=(n,),
        in_specs=[pl.BlockSpec((c, 3 * DN_WIDTH), lambda i: (i, 0)), pl.BlockSpec((c, LANES), lambda i: (i, 0))],
        out_specs=[pl.BlockSpec((c, DN_WIDTH), lambda i: (i, 0)),
                   pl.BlockSpec((1, DN_HEADS, DN_DIM, DN_DIM), lambda i: (i, 0, 0, 0)),
                   pl.BlockSpec((1, DN_HEADS, c, c), lambda i: (i, 0, 0, 0))],
        out_shape=[jax.ShapeDtypeStruct((s, DN_WIDTH), F32),
                   jax.ShapeDtypeStruct((n, DN_HEADS, DN_DIM, DN_DIM), F32),
                   jax.ShapeDtypeStruct((n, DN_HEADS, c, c), F32)],
        scratch_shapes=[pltpu.VMEM((DN_HEADS, DN_DIM, DN_DIM), F32)],
        compiler_params=_params(blk, 8 << 20),
    )(qkvn, gb)


def _delta_bwd(name, qkvn, gb, states, tinv, d_o):
    s = qkvn.shape[0]
    n = s // DN_CHUNK
    c = DN_CHUNK

    def body(qkv_ref, gb_ref, st_ref, t_ref, do_ref, dqkv_ref, dgb_ref, dstate):
        @pl.when(pl.program_id(0) == 0)
        def _():
            dstate[...] = jnp.zeros(dstate.shape, F32)

        gbv = gb_ref[...]
        lane = lax.broadcasted_iota(jnp.int32, (c, LANES), 1)
        t_known = t_ref[0]
        chunk = lambda *args: _delta_chunk(*args, t_known=t_known)[:2]
        _, vjp = jax.vjp(chunk, _dn_heads(qkv_ref, 0), _dn_heads(qkv_ref, DN_WIDTH),
                         _dn_heads(qkv_ref, 2 * DN_WIDTH), _dn_cols(gbv, DN_HEADS), _dn_cols(gbv, 0), st_ref[0])
        dq, dk, dv, dg, db, dst = vjp((_dn_heads(do_ref, 0), dstate[...]))
        dgb = jnp.zeros((c, LANES), F32)
        for h in range(DN_HEADS):
            dqkv_ref[:, h * DN_DIM:(h + 1) * DN_DIM] = dq[h]
            dqkv_ref[:, DN_WIDTH + h * DN_DIM:DN_WIDTH + (h + 1) * DN_DIM] = dk[h]
            dqkv_ref[:, 2 * DN_WIDTH + h * DN_DIM:2 * DN_WIDTH + (h + 1) * DN_DIM] = dv[h]
            dgb = dgb + jnp.where(lane == h, db[h], 0.0) + jnp.where(lane == DN_HEADS + h, dg[h], 0.0)
        dstate[...] = dst
        dgb_ref[...] = dgb

    rev = lambda i: (n - 1 - i, 0)
    blk = _nbytes((c, 3 * DN_WIDTH), BF16) + 2 * _nbytes((c, LANES), F32) + _nbytes((c, DN_WIDTH), F32)
    blk += _nbytes((DN_HEADS, DN_DIM, DN_DIM), F32) + _nbytes((c, 3 * DN_WIDTH), F32)
    return pl.pallas_call(
        body, name=name, grid=(n,),
        in_specs=[pl.BlockSpec((c, 3 * DN_WIDTH), rev), pl.BlockSpec((c, LANES), rev),
                  pl.BlockSpec((1, DN_HEADS, DN_DIM, DN_DIM), lambda i: (n - 1 - i, 0, 0, 0)),
                  pl.BlockSpec((1, DN_HEADS, c, c), lambda i: (n - 1 - i, 0, 0, 0)),
                  pl.BlockSpec((c, DN_WIDTH), rev)],
        out_specs=[pl.BlockSpec((c, 3 * DN_WIDTH), rev), pl.BlockSpec((c, LANES), rev)],
        out_shape=[jax.ShapeDtypeStruct((s, 3 * DN_WIDTH), F32), jax.ShapeDtypeStruct((s, LANES), F32)],
        scratch_shapes=[pltpu.VMEM((DN_HEADS, DN_DIM, DN_DIM), F32)],
        compiler_params=_params(blk, 16 << 20),
    )(qkvn, gb, states, tinv, d_o)


def _da_scores(q2f, k2, sub, valid, distf, head):
    lane = lax.broadcasted_iota(jnp.int32, q2f.shape, 1)
    hmask = (lane < DA_DIM) if sub == 0 else (lane >= DA_DIM)
    qm = jnp.where(hmask, q2f, 0.0).astype(BF16)
    slope = 2.0 ** (-ALIBI_MAX_EXP * (head + 1) / DA_HEADS)
    sc = lax.dot_general(qm, k2, NT, preferred_element_type=F32) * (DA_DIM ** -0.5)
    return jnp.where(valid, sc - slope * distf, NEG), qm, hmask


def _da_mask(i, r):
    qi = lax.broadcasted_iota(jnp.int32, (DA_BLOCK, 2 * DA_BLOCK), 0)
    ki = lax.broadcasted_iota(jnp.int32, (DA_BLOCK, 2 * DA_BLOCK), 1)
    dist = qi + DA_BLOCK - ki
    valid = (dist >= 0) & (dist <= DA_BLOCK) & ((ki >= DA_BLOCK) | (i > 0))
    return valid, (dist * r).astype(F32)


def _da_fwd(name, pda, r):
    s = pda.shape[0]
    n = s // r
    nb = n // DA_BLOCK
    w = DA_WIDTH
    dav = pda.reshape(n, r * 3 * w)

    def body(q_ref, kc_ref, kp_ref, vc_ref, vp_ref, o_ref, lse_ref):
        i = pl.program_id(1)
        valid, distf = _da_mask(i, r)
        lane = lax.broadcasted_iota(jnp.int32, (DA_BLOCK, LANES), 1)
        lse = jnp.zeros((DA_BLOCK, LANES), F32)
        for hp in range(DA_HEADS // 2):
            sl = slice(hp * LANES, (hp + 1) * LANES)
            q2f = q_ref[:, sl].astype(F32)
            k2 = jnp.concatenate([kp_ref[:, sl], kc_ref[:, sl]], axis=0)
            v2 = jnp.concatenate([vp_ref[:, sl], vc_ref[:, sl]], axis=0)
            o2 = None
            for sub in range(2):
                head = 2 * hp + sub
                sc, _, hmask = _da_scores(q2f, k2, sub, valid, distf, head)
                mx = jnp.max(sc, axis=1, keepdims=True)
                p = jnp.exp(sc - mx)
                l = jnp.sum(p, axis=1, keepdims=True)
                pv = lax.dot_general(p.astype(BF16), v2, NN, preferred_element_type=F32) / l
                o2 = pv if sub == 0 else jnp.where(hmask, pv, o2)
                lse = jnp.where(lane == head, mx + jnp.log(l), lse)
            o_ref[:, sl] = o2.astype(o_ref.dtype)
        lse_ref[...] = lse

    prev = lambda col: (lambda p, i: (jnp.maximum(i - 1, 0), 3 * p + col))
    cur = lambda col: (lambda p, i: (i, 3 * p + col))
    blk = 5 * _nbytes((DA_BLOCK, w), BF16) + _nbytes((DA_BLOCK, w), F32) + _nbytes((DA_BLOCK, LANES), F32)
    o, lse = pl.pallas_call(
        body, name=name, grid=(r, nb),
        in_specs=[pl.BlockSpec((DA_BLOCK, w), cur(0)), pl.BlockSpec((DA_BLOCK, w), cur(1)),
                  pl.BlockSpec((DA_BLOCK, w), prev(1)), pl.BlockSpec((DA_BLOCK, w), cur(2)),
                  pl.BlockSpec((DA_BLOCK, w), prev(2))],
        out_specs=[pl.BlockSpec((DA_BLOCK, w), lambda p, i: (i, p)),
                   pl.BlockSpec((DA_BLOCK, LANES), lambda p, i: (i, p))],
        out_shape=[jax.ShapeDtypeStruct((n, r * w), BF16), jax.ShapeDtypeStruct((n, r * LANES), F32)],
        compiler_params=_params(blk, 8 << 20),
    )(dav, dav, dav, dav, dav)
    return o.reshape(s, w), lse.reshape(s, LANES)


def _da_bwd(name, pda, d_ob, lse_tot, delta, r):
    s = pda.shape[0]
    n = s // r
    nb = n // DA_BLOCK
    w = DA_WIDTH
    dav = pda.reshape(n, r * 3 * w)
    dov = d_ob.reshape(n, r * w)
    lv = lse_tot.reshape(n, r * LANES)
    dlv = delta.reshape(n, r * LANES)

    def body(q_ref, kc_ref, kp_ref, vc_ref, vp_ref, do_ref, l_ref, dl_ref, dq_ref, dk_ref, dv_ref, ck, cv):
        i = pl.program_id(1)

        @pl.when(i == 0)
        def _():
            ck[...] = jnp.zeros(ck.shape, F32)
            cv[...] = jnp.zeros(cv.shape, F32)

        @pl.when(i < nb)
        def _():
            valid, distf = _da_mask(i, r)
            lsev = l_ref[...]
            dlt = dl_ref[...]
            for hp in range(DA_HEADS // 2):
                sl = slice(hp * LANES, (hp + 1) * LANES)
                q2f = q_ref[:, sl].astype(F32)
                k2 = jnp.concatenate([kp_ref[:, sl], kc_ref[:, sl]], axis=0)
                v2 = jnp.concatenate([vp_ref[:, sl], vc_ref[:, sl]], axis=0)
                do2f = do_ref[:, sl].astype(F32)
                dq2 = jnp.zeros((DA_BLOCK, LANES), F32)
                dk2 = jnp.zeros((2 * DA_BLOCK, LANES), F32)
                dv2 = jnp.zeros((2 * DA_BLOCK, LANES), F32)
                for sub in range(2):
                    head = 2 * hp + sub
                    sc, qm, hmask = _da_scores(q2f, k2, sub, valid, distf, head)
                    p = jnp.exp(sc - _lane_col(lsev, head))
                    dom = jnp.where(hmask, do2f, 0.0).astype(BF16)
                    dp = lax.dot_general(dom, v2, NT, preferred_element_type=F32)
                    ds = (p * (dp - _lane_col(dlt, head)) * (DA_DIM ** -0.5)).astype(BF16)
                    dq2 = dq2 + jnp.where(hmask, lax.dot_general(ds, k2, NN, preferred_element_type=F32), 0.0)
                    dk2 = dk2 + lax.dot_general(ds, qm, TN, preferred_element_type=F32)
                    dv2 = dv2 + lax.dot_general(p.astype(BF16), dom, TN, preferred_element_type=F32)
                dq_ref[:, sl] = dq2.astype(dq_ref.dtype)
                dk_ref[:, sl] = (ck[:, sl] + dk2[:DA_BLOCK]).astype(dk_ref.dtype)
                dv_ref[:, sl] = (cv[:, sl] + dv2[:DA_BLOCK]).astype(dv_ref.dtype)
                ck[:, sl] = dk2[DA_BLOCK:]
                cv[:, sl] = dv2[DA_BLOCK:]

        @pl.when(i == nb)
        def _():
            dk_ref[...] = ck[...].astype(dk_ref.dtype)
            dv_ref[...] = cv[...].astype(dv_ref.dtype)

    qrow = lambda i: jnp.minimum(i, nb - 1)
    prev = lambda col: (lambda p, i: (jnp.maximum(qrow(i) - 1, 0), 3 * p + col))
    cur = lambda col: (lambda p, i: (qrow(i), 3 * p + col))
    same = lambda p, i: (qrow(i), p)
    late = lambda p, i: (jnp.maximum(i - 1, 0), p)
    blk = 6 * _nbytes((DA_BLOCK, w), BF16) + 2 * _nbytes((DA_BLOCK, LANES), F32) + 3 * _nbytes((DA_BLOCK, w), F32)
    dq, dk, dv = pl.pallas_call(
        body, name=name, grid=(r, nb + 1),
        in_specs=[pl.BlockSpec((DA_BLOCK, w), cur(0)), pl.BlockSpec((DA_BLOCK, w), cur(1)),
                  pl.BlockSpec((DA_BLOCK, w), prev(1)), pl.BlockSpec((DA_BLOCK, w), cur(2)),
                  pl.BlockSpec((DA_BLOCK, w), prev(2)), pl.BlockSpec((DA_BLOCK, w), same),
                  pl.BlockSpec((DA_BLOCK, LANES), same), pl.BlockSpec((DA_BLOCK, LANES), same)],
        out_specs=[pl.BlockSpec((DA_BLOCK, w), same), pl.BlockSpec((DA_BLOCK, w), late),
                   pl.BlockSpec((DA_BLOCK, w), late)],
        out_shape=[jax.ShapeDtypeStruct((n, r * w), BF16)] * 3,
        scratch_shapes=[pltpu.VMEM((DA_BLOCK, w), F32), pltpu.VMEM((DA_BLOCK, w), F32)],
        compiler_params=_params(blk, 12 << 20),
    )(dav, dav, dav, dav, dav, dov, lv, dlv)
    return dq.reshape(s, w), dk.reshape(s, w), dv.reshape(s, w)


def _head_expand():
    hrow = lax.broadcasted_iota(jnp.int32, (LANES, DA_WIDTH), 0)
    lcol = lax.broadcasted_iota(jnp.int32, (LANES, DA_WIDTH), 1)
    return jnp.where(lcol // DA_DIM == hrow, 1.0, 0.0).astype(F32)


def _ffn_up(name, a, ga, tg, tu):
    s, d = a.shape
    nsh, _, _, ffs = ga.shape
    bm = _pick(s, (1024, 512, 256, 128))

    def body(a_ref, wg_ref, wu_ref, g_ref, u_ref, f_ref):
        av = a_ref[...]
        g = lax.dot_general(av, wg_ref[...], NN, preferred_element_type=F32)
        u = lax.dot_general(av, wu_ref[...], NN, preferred_element_type=F32)
        g_ref[...] = g.astype(BF16)
        u_ref[...] = u.astype(BF16)
        f_ref[...] = (_silu(g) * u).astype(BF16)

    wspec = lambda t: pl.BlockSpec((None, None, d, ffs), lambda i, j: (j, t, 0, 0))
    ospec = pl.BlockSpec((None, bm, ffs), lambda i, j: (j, i, 0))
    blk = _nbytes((bm, d), BF16) + 2 * _nbytes((d, ffs), BF16) + 3 * _nbytes((bm, ffs), BF16)
    return pl.pallas_call(
        body, name=name, grid=(s // bm, nsh),
        in_specs=[pl.BlockSpec((bm, d), lambda i, j: (i, 0)), wspec(tg), wspec(tu)],
        out_specs=[ospec] * 3, out_shape=[jax.ShapeDtypeStruct((nsh, s, ffs), BF16)] * 3,
        compiler_params=_params(blk, 4 * _nbytes((bm, ffs), F32)),
    )(a, ga, ga)


def _ffn_fwd(tag, h_in, ln, sh, sc, gt, ga, tg, tu, gb, td, weight):
    s, d = h_in.shape
    nsh, _, ffs, _ = gb.shape
    (a,) = _rowwise(tag + "_norm", lambda h, l, s1, s2: ((_rms_mod(h, l, s1, s2),), ()), [h_in], [ln, sh, sc],
                    [(d, BF16)])
    g, u, f = _ffn_up(tag + "_up", a, ga, tg, tu)
    bm, bn = _pick(s, (1024, 512, 256, 128)), _pick(d, (1024, 512, 256, 128))
    io = pl.BlockSpec((bm, bn), lambda i, j, kk: (i, j))
    h_out, o = _mm_core(
        tag + "_down", (s // bm, d // bn, nsh), nsh,
        [(f, pl.BlockSpec((None, bm, ffs), lambda i, j, kk: (kk, i, 0)),
          gb, pl.BlockSpec((None, None, ffs, bn), lambda i, j, kk: (kk, td, 0, j)), NN)],
        [((s, d), F32, io), ((s, d), BF16, io)], (bm, bn),
        epi=lambda acc, h, gv: (h + weight * gv * acc, acc),
        epi_ins=[(h_in, io), (gt, pl.BlockSpec((1, bn), lambda i, j, kk: (0, j)))])
    return h_out, dict(a=a, g=g, u=u, f=f, o=o)


def _resid_bwd(tag, dh_out, o, gt, weight):
    d = dh_out.shape[1]

    def fn(dh, ov, g):
        return (weight * g * dh,), (jnp.sum(weight * dh * ov.astype(F32), axis=0, keepdims=True),)

    do, d_gt = _rowwise(tag + "_resid_bwd", fn, [dh_out, o], [gt], [(d, BF16)], [(1, d)])
    return do, d_gt


def _norm_bwd(tag, h_in, da, dh_out, ln, sh, sc):
    d = h_in.shape[1]

    def fn(h, dav, dh, l, s1, s2):
        _, vjp = jax.vjp(_rms_mod, h, l, s1, s2)
        gh, gl, gs1, gs2 = vjp(dav)
        return (dh + gh,), (gl, gs1, gs2)

    return _rowwise(tag + "_norm_bwd", fn, [h_in, da, dh_out], [ln, sh, sc], [(d, F32)], [(1, d)] * 3)


def _ffn_bwd(tag, h_in, dh_out, sv, ln, sh, sc, gt, ga, tg, tu, gb, td, weight):
    s, d = h_in.shape
    nsh, _, ffs, _ = gb.shape
    bm, bn = _pick(s, (1024, 512, 256, 128)), _pick(d, (1024, 512, 256, 128))
    bk = _pick(s, (1024, 512, 256, 128))
    do, d_gt = _resid_bwd(tag, dh_out, sv["o"], gt, weight)

    def act_bwd(df, g, u):
        _, vjp = jax.vjp(_swiglu_act, g, u)
        return vjp(df)

    hid = pl.BlockSpec((None, bm, ffs), lambda i, j, kk: (j, i, 0))
    dg, du = _mm_core(
        tag + "_down_dx", (s // bm, nsh, 1), 1,
        [(do, pl.BlockSpec((bm, d), lambda i, j, kk: (i, 0)),
          gb, pl.BlockSpec((None, None, ffs, d), lambda i, j, kk: (j, td, 0, 0)), NT)],
        [((nsh, s, ffs), BF16, hid)] * 2, (bm, ffs), epi=act_bwd, epi_ins=[(sv["g"], hid), (sv["u"], hid)])
    (d_wd,) = _mm_core(
        tag + "_down_dw", (nsh, d // bn, s // bk), s // bk,
        [(sv["f"], pl.BlockSpec((None, bk, ffs), lambda i, j, kk: (i, kk, 0)),
          do, pl.BlockSpec((bk, bn), lambda i, j, kk: (kk, j)), TN)],
        [((nsh, ffs, d), BF16, pl.BlockSpec((None, ffs, bn), lambda i, j, kk: (i, 0, j)))], (ffs, bn))
    kmaj = pl.BlockSpec((None, bm, ffs), lambda i, j, kk: (kk, i, 0))
    wsp = lambda t: pl.BlockSpec((None, None, bn, ffs), functools.partial(lambda i, j, kk, t: (kk, t, j, 0), t=t))
    (da,) = _mm_core(
        tag + "_up_dx", (s // bm, d // bn, nsh), nsh, [(dg, kmaj, ga, wsp(tg), NT), (du, kmaj, ga, wsp(tu), NT)],
        [((s, d), F32, pl.BlockSpec((bm, bn), lambda i, j, kk: (i, j)))], (bm, bn))
    dws = []
    for nm, dh in (("_wg_dw", dg), ("_wu_dw", du)):
        (dw,) = _mm_core(
            tag + nm, (1, nsh, s // bk), s // bk,
            [(sv["a"], pl.BlockSpec((bk, d), lambda i, j, kk: (kk, 0)),
              dh, pl.BlockSpec((None, bk, ffs), lambda i, j, kk: (j, kk, 0)), TN)],
            [((nsh, d, ffs), BF16, pl.BlockSpec((None, d, ffs), lambda i, j, kk: (j, 0, 0)))], (d, ffs))
        dws.append(dw)
    dh_in, d_ln, d_sh, d_sc = _norm_bwd(tag, h_in, da, dh_out, ln, sh, sc)
    return dh_in, dict(wg=dws[0], wu=dws[1], wd=d_wd), dict(ln=d_ln, sh=d_sh, sc=d_sc, gt=d_gt)


def _mixer_fwd(tag, h_in, ln, sh, sc, gt, w, sp):
    d = h_in.shape[1]
    (a,) = _rowwise(tag + "_norm", lambda h, l, s1, s2: ((_rms_mod(h, l, s1, s2),), ()), [h_in], [ln, sh, sc],
                    [(d, BF16)])
    (pq,) = _matmul(tag + "_pq", a, w["wq"], outs=(BF16,))
    (pz,) = _matmul(tag + "_pz", a, w["wz"], outs=(BF16,))
    (pba,) = _matmul(tag + "_pba", a, w["wba"])
    (pda,) = _matmul(tag + "_pda", a, w["wda"], outs=(BF16,))
    (pg,) = _matmul(tag + "_pg", a, w["wg"], outs=(BF16,))
    yc, qkvn, gb = _conv_prep_fwd(tag + "_conv", pq, sp["conv8"], pba, sp["alog"], sp["dtb"])
    o_a, states, tinv = _delta_fwd(tag + "_delta", qkvn, gb)
    (o_an,) = _rowwise(tag + "_dnorm", lambda o, z, dn: ((_dn_outnorm(o, z.astype(F32), dn),), ()), [o_a, pz],
                       [sp["dn"]], [(DN_WIDTH, BF16)])
    ops, lses = [], []
    for (_, r) in DA_PATTERNS:
        o_p, lse_p = _da_fwd(f"{tag}_da{r}", pda, r)
        ops.append(o_p)
        lses.append(lse_p)

    def merge(o1, o2, o3, l1, l2, l3):
        mx = jnp.maximum(jnp.maximum(l1, l2), l3)
        e1, e2, e3 = jnp.exp(l1 - mx), jnp.exp(l2 - mx), jnp.exp(l3 - mx)
        tot = e1 + e2 + e3
        ex = _head_expand()
        up = lambda wgt: lax.dot_general(wgt / tot, ex, NN, precision=HI, preferred_element_type=F32)
        return (up(e1) * o1 + up(e2) * o2 + up(e3) * o3, mx + jnp.log(tot)), ()

    o_b, lse_tot = _rowwise(tag + "_merge", merge, ops + lses, [], [(DA_WIDTH, BF16), (LANES, F32)])
    (y_a,) = _matmul(tag + "_wa", o_an, w["w_a"], outs=(BF16,))
    (y_b,) = _matmul(tag + "_wb", o_b, w["w_b"], outs=(BF16,))

    def gate(ga, gbv, ya, yb):
        return _sigmoid(ga.astype(F32)) * ya.astype(F32) + _sigmoid(gbv.astype(F32)) * yb.astype(F32)

    (merged,) = _rowwise(tag + "_gate", lambda *v: ((gate(*v),), ()), [(pg, d, 0), (pg, d, 1), y_a, y_b], [],
                         [(d, BF16)])
    h_out, m = _matmul(tag + "_wo", merged, w["w_o"], outs=(F32, BF16), epi_rows=[h_in], epi_bcast=[gt],
                       epi=lambda acc, h, g: (h + g * acc, acc))
    sv = dict(a=a, pq=pq, pz=pz, pba=pba, pda=pda, pg=pg, yc=yc, qkvn=qkvn, gb=gb, o_a=o_a, states=states, tinv=tinv,
              o_an=o_an, o_b=o_b, lse=lse_tot, y_a=y_a, y_b=y_b, merged=merged, m=m, gate=gate)
    return h_out, sv


def _mixer_bwd(tag, h_in, dh_out, sv, ln, sh, sc, gt, w, sp):
    d = h_in.shape[1]
    dm, d_gt = _resid_bwd(tag, dh_out, sv["m"], gt, 1.0)
    (d_merged,) = _matmul(tag + "_wo_dx", dm, w["w_o"], tb=True, outs=(BF16,))
    (d_wo,) = _matmul(tag + "_wo_dw", sv["merged"], dm, ta=True, outs=(BF16,))
    gate = sv["gate"]

    def gate_bwd(dmg, ga, gbv, ya, yb):
        _, vjp = jax.vjp(gate, ga.astype(F32), gbv.astype(F32), ya.astype(F32), yb.astype(F32))
        dga, dgb, dya, dyb = vjp(dmg.astype(F32))
        return (jnp.concatenate([dga, dgb], axis=1), dya, dyb), ()

    pg = sv["pg"]
    d_pg, d_ya, d_yb = _rowwise(tag + "_gate_bwd", gate_bwd, [d_merged, (pg, d, 0), (pg, d, 1), sv["y_a"], sv["y_b"]],
                                [], [(2 * d, BF16), (d, BF16), (d, BF16)])
    (d_oan,) = _matmul(tag + "_wa_dx", d_ya, w["w_a"], tb=True)
    (d_wa,) = _matmul(tag + "_wa_dw", sv["o_an"], d_ya, ta=True, outs=(BF16,))
    (d_ob,) = _matmul(tag + "_wb_dx", d_yb, w["w_b"], tb=True, outs=(BF16,))
    (d_wb,) = _matmul(tag + "_wb_dw", sv["o_b"], d_yb, ta=True, outs=(BF16,))

    def dnorm_bwd(doan, o, z, dn):
        _, vjp = jax.vjp(_dn_outnorm, o, z.astype(F32), dn)
        go, gz, gdn = vjp(doan)
        return (go, gz), (gdn,)

    d_oa, d_pz, d_dn = _rowwise(tag + "_dnorm_bwd", dnorm_bwd, [d_oan, sv["o_a"], sv["pz"]], [sp["dn"]],
                                [(DN_WIDTH, F32), (DN_WIDTH, BF16)], [(1, DN_DIM)])
    d_qkvn, d_gb = _delta_bwd(tag + "_delta_bwd", sv["qkvn"], sv["gb"], sv["states"], sv["tinv"], d_oa)

    def prep_bwd(dq, dgbv, yc, pba, alog, dtb):
        _, vjp = jax.vjp(_dn_prep, yc.astype(F32), pba, alog, dtb)
        gyc, gpba, galog, gdtb = vjp((dq, dgbv))
        return (gyc, gpba), (galog, gdtb)

    d_yc, d_pba, d_alog, d_dtb = _rowwise(tag + "_prep_bwd", prep_bwd, [d_qkvn, d_gb, sv["yc"], sv["pba"]],
                                          [sp["alog"], sp["dtb"]], [(3 * DN_WIDTH, BF16), (LANES, BF16)],
                                          [(1, LANES), (1, LANES)], bm=128)
    d_pq, d_conv = _conv_bwd(tag + "_conv_bwd", d_yc, sv["pq"], sp["conv8"])

    def delta_fn(dob, ob):
        prod = dob.astype(F32) * ob.astype(F32)
        return (lax.dot_general(prod, _head_expand(), NT, precision=HI, preferred_element_type=F32),), ()

    (delta,) = _rowwise(tag + "_da_delta", delta_fn, [d_ob, sv["o_b"]], [], [(LANES, F32)])
    grads = [_da_bwd(f"{tag}_da{r}_bwd", sv["pda"], d_ob, sv["lse"], delta, r) for (_, r) in DA_PATTERNS]

    def sum3(*parts):
        q1, k1, v1, q2, k2, v2, q3, k3, v3 = (p.astype(F32) for p in parts)
        return (jnp.concatenate([q1 + q2 + q3, k1 + k2 + k3, v1 + v2 + v3], axis=1),), ()

    (d_pda,) = _rowwise(tag + "_da_sum", sum3, [t for g in grads for t in g], [], [(3 * DA_WIDTH, BF16)])

    a = sv["a"]
    (da,) = _matmul(tag + "_pq_dx", d_pq, w["wq"], tb=True)
    add = lambda acc, prev: (acc + prev,)
    (da,) = _matmul(tag + "_pz_dx", d_pz, w["wz"], tb=True, epi_rows=[da], epi=add)
    (da,) = _matmul(tag + "_pba_dx", d_pba, w["wba"], tb=True, epi_rows=[da], epi=add)
    (da,) = _matmul(tag + "_pda_dx", d_pda, w["wda"], tb=True, epi_rows=[da], epi=add)
    (da,) = _matmul(tag + "_pg_dx", d_pg, w["wg"], tb=True, epi_rows=[da], epi=add)
    (d_wq,) = _matmul(tag + "_pq_dw", a, d_pq, ta=True, outs=(BF16,))
    (d_wz,) = _matmul(tag + "_pz_dw", a, d_pz, ta=True, outs=(BF16,))
    (d_wba,) = _matmul(tag + "_pba_dw", a, d_pba, ta=True, outs=(BF16,))
    (d_wda,) = _matmul(tag + "_pda_dw", a, d_pda, ta=True, outs=(BF16,))
    (d_wg,) = _matmul(tag + "_pg_dw", a, d_pg, ta=True, outs=(BF16,))
    dh_in, d_ln, d_sh, d_sc = _norm_bwd(tag, h_in, da, dh_out, ln, sh, sc)
    wgrads = dict(wq=d_wq, wz=d_wz, wba=d_wba, wda=d_wda, wg=d_wg, w_a=d_wa, w_b=d_wb, w_o=d_wo)
    small = dict(ln=d_ln, sh=d_sh, sc=d_sc, gt=d_gt, dn=d_dn, alog=d_alog, dtb=d_dtb, conv=d_conv)
    return dh_in, wgrads, small


def _loss_head(h, target, fnorm):
    d = h.shape[1]

    def fn(hv, tv, fw):
        def lossf(hh, ww):
            y = hh * lax.rsqrt(jnp.mean(hh * hh, axis=-1, keepdims=True) + NORM_EPS) * ww
            return 0.5 * jnp.sum(jnp.mean(jnp.square(y - tv), axis=-1))

        val, (dh, dw) = jax.value_and_grad(lossf, argnums=(0, 1))(hv, fw)
        return (dh,), (jnp.full((1, LANES), val, F32), dw)

    return _rowwise("loss_head", fn, [h, target], [fnorm], [(d, F32)], [(1, LANES), (1, d)])


def _row(v):
    return v.reshape(1, -1)


def _pad_lanes(v, offset):
    return jnp.pad(v.reshape(1, -1), ((0, 0), (offset, LANES - offset - v.shape[0])))


_UP_SLOTS = dict(ffn1_wg=0, ffn1_wu=1, ffn2_wg=2, ffn2_wu=3)
_DOWN_SLOTS = dict(ffn1_wd=0, ffn2_wd=1)


def _local_step(x2, target, mod, layer_weights, small, on_layer_grads):
    depth = mod.shape[0]
    d = x2.shape[1]
    h = x2
    saved = []
    mods = []
    up = lambda l, nm: _UP_SLOTS[nm]
    down = lambda l, nm: _DOWN_SLOTS[nm]
    for l in range(depth):
        m9 = [_row(mod[l, i * d:(i + 1) * d]) for i in range(N_ADA)]
        sp = dict(conv8=jnp.pad(small["conv_w"][l], ((0, 8 - DN_CONV), (0, 0))),
                  alog=_pad_lanes(small["a_log"][l], DN_HEADS), dtb=_pad_lanes(small["dt_bias"][l], DN_HEADS),
                  dn=_row(small["dn_norm"][l]))
        ga, gb, w = layer_weights(l, h)
        h0 = h
        h1, sv1 = _ffn_fwd(f"l{l}_ffn1", h0, _row(small["ln_ffn1"][l]), m9[0], m9[1], m9[2], ga, up(l, "ffn1_wg"),
                           up(l, "ffn1_wu"), gb, down(l, "ffn1_wd"), 0.5)
        h2, sv2 = _mixer_fwd(f"l{l}_mix", h1, _row(small["ln_mix"][l]), m9[3], m9[4], m9[5], w, sp)
        h3, sv3 = _ffn_fwd(f"l{l}_ffn2", h2, _row(small["ln_ffn2"][l]), m9[6], m9[7], m9[8], ga, up(l, "ffn2_wg"),
                           up(l, "ffn2_wu"), gb, down(l, "ffn2_wd"), 0.5)
        saved.append((h0, h1, h2, sv1, sv2, sv3, sp, ga, gb, w))
        mods.append(m9)
        h = h3
    dh, loss_part, d_fnorm = _loss_head(h, target, _row(small["final_norm"]))
    sgrads, dmods = [], []
    token = None
    for l in reversed(range(depth)):
        h0, h1, h2, sv1, sv2, sv3, sp, ga, gb, w = saved[l]
        m9 = mods[l] if token is None else [r + token for r in mods[l]]
        dh, g3, s3 = _ffn_bwd(f"l{l}_ffn2", h2, dh, sv3, _row(small["ln_ffn2"][l]), m9[6], m9[7], m9[8], ga,
                              up(l, "ffn2_wg"), up(l, "ffn2_wu"), gb, down(l, "ffn2_wd"), 0.5)
        dh, g2, s2 = _mixer_bwd(f"l{l}_mix", h1, dh, sv2, _row(small["ln_mix"][l]), m9[3], m9[4], m9[5], w, sp)
        dh, g1, s1 = _ffn_bwd(f"l{l}_ffn1", h0, dh, sv1, _row(small["ln_ffn1"][l]), m9[0], m9[1], m9[2], ga,
                              up(l, "ffn1_wg"), up(l, "ffn1_wu"), gb, down(l, "ffn1_wd"), 0.5)
        token = on_layer_grads(l, dict(ffn1_wg=g1["wg"], ffn1_wu=g1["wu"], ffn1_wd=g1["wd"], ffn2_wg=g3["wg"],
                                       ffn2_wu=g3["wu"], ffn2_wd=g3["wd"], **g2))
        dmods.append(jnp.concatenate([s1["sh"], s1["sc"], s1["gt"], s2["sh"], s2["sc"], s2["gt"],
                                      s3["sh"], s3["sc"], s3["gt"]], axis=1))
        sgrads.append(dict(ln_ffn1=s1["ln"][0], ln_mix=s2["ln"][0], ln_ffn2=s3["ln"][0],
                           a_log=s2["alog"][0, DN_HEADS:2 * DN_HEADS], dt_bias=s2["dtb"][0, DN_HEADS:2 * DN_HEADS],
                           dn_norm=s2["dn"][0], conv_w=s2["conv"][:DN_CONV]))
    sgrads.reverse()
    dmods.reverse()
    return loss_part[0, 0], dh, jnp.concatenate(dmods, axis=0), sgrads, d_fnorm[0]


def _flip(v, bit):
    return 1 - v if bit else v


def _allgather8(name, x):
    r, c = x.shape

    def body(x_ref, out_ref, send_sems, recv_sems, local_sem):
        mx, my, mc = lax.axis_index("x"), lax.axis_index("y"), lax.axis_index("c")
        me = 4 * mx + 2 * my + mc
        mine = pltpu.make_async_copy(x_ref, out_ref.at[me], local_sem)
        mine.start()
        sends = []
        for k in range(1, 8):
            peer = (_flip(mx, k & 4), _flip(my, k & 2), _flip(mc, k & 1))
            cp = pltpu.make_async_remote_copy(src_ref=x_ref, dst_ref=out_ref.at[me], send_sem=send_sems.at[k - 1],
                                              recv_sem=recv_sems.at[k - 1], device_id=peer, device_id_type=MESH)
            cp.start()
            sends.append(cp)
        for k in range(1, 8):
            peer = (_flip(mx, k & 4), _flip(my, k & 2), _flip(mc, k & 1))
            src = 4 * peer[0] + 2 * peer[1] + peer[2]
            pltpu.make_async_remote_copy(src_ref=x_ref, dst_ref=out_ref.at[src], send_sem=send_sems.at[k - 1],
                                         recv_sem=recv_sems.at[k - 1], device_id=peer, device_id_type=MESH).wait_recv()
        for cp in sends:
            cp.wait_send()
        mine.wait()

    return pl.pallas_call(
        body, name=name, out_shape=jax.ShapeDtypeStruct((8, r, c), x.dtype),
        in_specs=[pl.BlockSpec(memory_space=pltpu.VMEM)], out_specs=pl.BlockSpec(memory_space=pltpu.VMEM),
        scratch_shapes=[pltpu.SemaphoreType.DMA((7,)), pltpu.SemaphoreType.DMA((7,)), pltpu.SemaphoreType.DMA],
        compiler_params=_params(9 * _nbytes((r, c), x.dtype)),
    )(x)


def _chip_peers(mx, my):
    chips = [(1 - mx, my), (mx, 1 - my), (1 - mx, 1 - my)]
    return chips, [2 * cx + cy for (cx, cy) in chips]


_ANY = pl.BlockSpec(memory_space=pl.ANY)


def _row_half(mc, r):
    return pl.ds(pl.multiple_of(mc * (r // 2), 16), r // 2)


def _gather_groups(name, shards):
    ng = len(shards)

    def body(*refs):
        xs, outs = refs[:ng], refs[ng:2 * ng]
        send_sems, recv_sems = refs[2 * ng:]
        mx, my, mc = lax.axis_index("x"), lax.axis_index("y"), lax.axis_index("c")
        j = 2 * mx + my
        chips, idxs = _chip_peers(mx, my)
        sib = (mx, my, 1 - mc)

        def copy(k, src, dst, to):
            return pltpu.make_async_remote_copy(src_ref=src, dst_ref=dst, send_sem=send_sems.at[k],
                                                recv_sem=recv_sems.at[k], device_id=to, device_id_type=MESH)

        first, passed = [], []
        for g in range(ng):
            mine = _row_half(mc, shards[g].shape[1])
            for t, chip in enumerate(chips):
                cp = copy(6 * g + t, xs[g].at[:, mine], outs[g].at[j, :, mine], (*chip, mc))
                cp.start()
                first.append(cp)
        for g in range(ng):
            mine = _row_half(mc, shards[g].shape[1])
            for t, chip in enumerate(chips):
                landed = outs[g].at[idxs[t], :, mine]
                copy(6 * g + t, landed, landed, (*chip, mc)).wait_recv()
                fwd = copy(6 * g + 3 + t, landed, landed, sib)
                fwd.start()
                passed.append(fwd)
        for g in range(ng):
            theirs_half = _row_half(1 - mc, shards[g].shape[1])
            for t in range(3):
                theirs = outs[g].at[idxs[t], :, theirs_half]
                copy(6 * g + 3 + t, theirs, theirs, sib).wait_recv()
        for cp in first + passed:
            cp.wait_send()

    outs = pl.pallas_call(
        body, name=name, out_shape=[jax.ShapeDtypeStruct((4,) + x.shape, x.dtype) for x in shards],
        in_specs=[_ANY] * ng, out_specs=[_ANY] * ng,
        scratch_shapes=[pltpu.SemaphoreType.DMA((6 * ng,)), pltpu.SemaphoreType.DMA((6 * ng,))],
    )(*shards)
    return _place_own_slab(outs, shards)


def _place_own_slab(outs, shards):
    chip = 2 * lax.axis_index("x") + lax.axis_index("y")
    return [lax.dynamic_update_slice(o, x[None], (chip,) + (0,) * x.ndim) for o, x in zip(outs, shards)]


_HBM = pl.BlockSpec(memory_space=pltpu.HBM)
_SEM = pl.BlockSpec(memory_space=pltpu.SEMAPHORE)
_DATAFLOW = pltpu.SideEffectType.DATAFLOW_SIDE_EFFECTING


def _ici_gather_copies(src_refs, land_refs, send_sems, recv_sems, scatter=False):
    mx, my, mc = lax.axis_index("x"), lax.axis_index("y"), lax.axis_index("c")
    j = 2 * mx + my
    chips, idxs = _chip_peers(mx, my)
    sends, recvs = [], []
    for g, src in enumerate(src_refs):
        for t, chip in enumerate(chips):
            common = dict(send_sem=send_sems.at[3 * g + t], recv_sem=recv_sems.at[3 * g + t], device_id=(*chip, mc),
                          device_id_type=MESH)
            if scatter:
                out, to, frm = src.at[idxs[t]], land_refs[g].at[j], land_refs[g].at[idxs[t]]
            else:
                mine = _row_half(mc, src.shape[1])
                out, to, frm = src.at[:, mine], land_refs[g].at[j, :, mine], land_refs[g].at[idxs[t], :, mine]
            sends.append(pltpu.make_async_remote_copy(src_ref=out, dst_ref=to, **common))
            recvs.append(pltpu.make_async_remote_copy(src_ref=out, dst_ref=frm, **common))
    return sends, recvs


def _gather_start(name, shards, scatter=False):
    ng = len(shards)

    def body(*refs):
        srcs, lands = refs[:ng], refs[ng:2 * ng]
        send_sems, recv_sems = refs[2 * ng], refs[2 * ng + 1]
        token = refs[-1]
        sends, _ = _ici_gather_copies(srcs, lands, send_sems, recv_sems, scatter)
        for cp in sends:
            cp.start()
        token[...] = jnp.zeros(token.shape, token.dtype)

    land_shape = lambda x: x.shape if scatter else (4,) + x.shape
    srcs = [pltpu.with_memory_space_constraint(x, pltpu.HBM) for x in shards]
    lands = [pltpu.with_memory_space_constraint(lax.empty(land_shape(x), x.dtype), pltpu.HBM) for x in shards]
    res = pl.pallas_call(
        body, name=name,
        out_shape=(pltpu.SemaphoreType.DMA((3 * ng,)), pltpu.SemaphoreType.DMA((3 * ng,)),
                   *[pltpu.HBM(x.shape, x.dtype) for x in srcs], *[pltpu.HBM(x.shape, x.dtype) for x in lands],
                   jax.ShapeDtypeStruct((8, LANES), F32)),
        in_specs=[_HBM] * (2 * ng),
        out_specs=(_SEM, _SEM, *[_HBM] * (2 * ng), pl.BlockSpec(memory_space=pltpu.VMEM)),
        input_output_aliases={i: 2 + i for i in range(2 * ng)},
        compiler_params=pltpu.CompilerParams(has_side_effects=_DATAFLOW),
    )(*srcs, *lands)
    return dict(send_sems=res[0], recv_sems=res[1], srcs=list(res[2:2 + ng]), lands=list(res[2 + ng:2 + 2 * ng]),
                token=res[-1])


def _gather_wait(name, started, after, scatter=False):
    ng = len(started["srcs"])

    def body(*refs):
        srcs, lands = refs[:ng], refs[ng:2 * ng]
        send_sems, recv_sems = refs[2 * ng], refs[2 * ng + 1]
        sends, recvs = _ici_gather_copies(srcs, lands, send_sems, recv_sems, scatter)
        for cp in sends:
            cp.wait_send()
        for cp in recvs:
            cp.wait_recv()

    res = pl.pallas_call(
        body, name=name,
        out_shape=[pltpu.HBM(x.shape, x.dtype) for x in started["srcs"] + started["lands"]],
        in_specs=[_HBM] * (2 * ng) + [_SEM, _SEM, _ANY], out_specs=[_HBM] * (2 * ng),
        input_output_aliases={i: i for i in range(2 * ng)},
        compiler_params=pltpu.CompilerParams(has_side_effects=_DATAFLOW),
    )(*started["srcs"], *started["lands"], started["send_sems"], started["recv_sems"], after)
    return list(res[:ng]), list(res[ng:])


def _pair_forward_groups(name, lands, shards):
    ng = len(lands)

    def body(*refs):
        ins, outs = refs[:ng], refs[ng:2 * ng]
        send_sems, recv_sems = refs[2 * ng:]
        mx, my, mc = lax.axis_index("x"), lax.axis_index("y"), lax.axis_index("c")
        _, idxs = _chip_peers(mx, my)
        sib = (mx, my, 1 - mc)
        cps = []
        for g in range(ng):
            mine = _row_half(mc, lands[g].shape[2])
            for t in range(3):
                cp = pltpu.make_async_remote_copy(src_ref=ins[g].at[idxs[t], :, mine], dst_ref=outs[g].at[idxs[t], :, mine],
                                                  send_sem=send_sems.at[3 * g + t], recv_sem=recv_sems.at[3 * g + t],
                                                  device_id=sib, device_id_type=MESH)
                cp.start()
                cps.append(cp)
        for g in range(ng):
            theirs = _row_half(1 - mc, lands[g].shape[2])
            for t in range(3):
                pltpu.make_async_remote_copy(src_ref=ins[g].at[idxs[t], :, theirs], dst_ref=outs[g].at[idxs[t], :, theirs],
                                             send_sem=send_sems.at[3 * g + t], recv_sem=recv_sems.at[3 * g + t],
                                             device_id=sib, device_id_type=MESH).wait_recv()
        for cp in cps:
            cp.wait_send()

    outs = pl.pallas_call(
        body, name=name, out_shape=[jax.ShapeDtypeStruct(x.shape, x.dtype) for x in lands],
        in_specs=[_ANY] * ng, out_specs=[_ANY] * ng, input_output_aliases={i: i for i in range(ng)},
        scratch_shapes=[pltpu.SemaphoreType.DMA((3 * ng,)), pltpu.SemaphoreType.DMA((3 * ng,))],
    )(*lands)
    return _place_own_slab(outs, shards)


def _pair_swap_groups(name, gs):
    ng = len(gs)

    def body(*refs):
        xs, outs = refs[:ng], refs[ng:2 * ng]
        send_sems, recv_sems = refs[2 * ng:]
        mx, my, mc = lax.axis_index("x"), lax.axis_index("y"), lax.axis_index("c")
        cps = []
        for g in range(ng):
            cp = pltpu.make_async_remote_copy(src_ref=xs[g].at[:, :, _row_half(1 - mc, gs[g].shape[2])], dst_ref=outs[g],
                                              send_sem=send_sems.at[g], recv_sem=recv_sems.at[g],
                                              device_id=(mx, my, 1 - mc), device_id_type=MESH)
            cp.start()
            cps.append(cp)
        for cp in cps:
            cp.wait()

    return pl.pallas_call(
        body, name=name,
        out_shape=[jax.ShapeDtypeStruct(x.shape[:2] + (x.shape[2] // 2, x.shape[3]), x.dtype) for x in gs],
        in_specs=[_ANY] * ng, out_specs=[_ANY] * ng,
        scratch_shapes=[pltpu.SemaphoreType.DMA((ng,)), pltpu.SemaphoreType.DMA((ng,))],
    )(*gs)


def _chip_scatter_groups(name, ps):
    ng = len(ps)

    def body(*refs):
        xs, outs = refs[:ng], refs[ng:2 * ng]
        send_sems, recv_sems = refs[2 * ng:]
        mx, my, mc = lax.axis_index("x"), lax.axis_index("y"), lax.axis_index("c")
        j = 2 * mx + my
        chips, idxs = _chip_peers(mx, my)
        sends = []
        for g in range(ng):
            for t, chip in enumerate(chips):
                cp = pltpu.make_async_remote_copy(src_ref=xs[g].at[idxs[t]], dst_ref=outs[g].at[j],
                                                  send_sem=send_sems.at[3 * g + t], recv_sem=recv_sems.at[3 * g + t],
                                                  device_id=(*chip, mc), device_id_type=MESH)
                cp.start()
                sends.append(cp)
        for g in range(ng):
            for t, chip in enumerate(chips):
                pltpu.make_async_remote_copy(src_ref=xs[g].at[idxs[t]], dst_ref=outs[g].at[idxs[t]],
                                             send_sem=send_sems.at[3 * g + t], recv_sem=recv_sems.at[3 * g + t],
                                             device_id=(*chip, mc), device_id_type=MESH).wait_recv()
        for cp in sends:
            cp.wait_send()

    outs = pl.pallas_call(
        body, name=name, out_shape=[jax.ShapeDtypeStruct(x.shape, x.dtype) for x in ps],
        in_specs=[_ANY] * ng, out_specs=[_ANY] * ng,
        scratch_shapes=[pltpu.SemaphoreType.DMA((3 * ng,)), pltpu.SemaphoreType.DMA((3 * ng,))],
    )(*ps)
    return _place_own_part(outs, ps)


def _place_own_part(outs, ps):
    chip = 2 * lax.axis_index("x") + lax.axis_index("y")
    return [lax.dynamic_update_slice(o, lax.dynamic_index_in_dim(x, chip, 0, keepdims=True), (chip,) + (0,) * (x.ndim - 1))
            for o, x in zip(outs, ps)]


def _pair_merge_groups(name, fs):
    ng = len(fs)

    def body(*refs):
        xs, outs = refs[:ng], refs[ng:2 * ng]
        send_sems, recv_sems = refs[2 * ng:]
        mx, my, mc = lax.axis_index("x"), lax.axis_index("y"), lax.axis_index("c")
        cps = []
        for g in range(ng):
            mine = _row_half(mc, 2 * fs[g].shape[1])
            cp = pltpu.make_async_remote_copy(src_ref=xs[g], dst_ref=outs[g].at[:, mine], send_sem=send_sems.at[g],
                                              recv_sem=recv_sems.at[g], device_id=(mx, my, 1 - mc), device_id_type=MESH)
            cp.start()
            cps.append(cp)
        for g in range(ng):
            theirs = outs[g].at[:, _row_half(1 - mc, 2 * fs[g].shape[1])]
            pltpu.make_async_remote_copy(src_ref=xs[g], dst_ref=theirs, send_sem=send_sems.at[g],
                                         recv_sem=recv_sems.at[g], device_id=(mx, my, 1 - mc),
                                         device_id_type=MESH).wait_recv()
        for cp in cps:
            cp.wait_send()

    outs = pl.pallas_call(
        body, name=name,
        out_shape=[jax.ShapeDtypeStruct((x.shape[0], 2 * x.shape[1], x.shape[2]), x.dtype) for x in fs],
        in_specs=[_ANY] * ng, out_specs=[_ANY] * ng,
        scratch_shapes=[pltpu.SemaphoreType.DMA((ng,)), pltpu.SemaphoreType.DMA((ng,))],
    )(*fs)
    mc = lax.axis_index("c")
    return [lax.dynamic_update_slice(o, x, (0, mc * x.shape[1], 0)) for o, x in zip(outs, fs)]


def _block_rows(r, w, itemsize=4, budget=4 << 20):
    for c in (r, 2048, 1024, 512, 256, 128, 64, 32, 16):
        if c <= r and r % c == 0 and c * w * itemsize <= budget:
            return c
    return r


def _pair_sum(name, g, got, cidx):
    ns, t, r, w = g.shape
    rh = r // 2
    bm = _block_rows(rh, w)
    nb = rh // bm

    def body(c_ref, a_ref, b_ref, o_ref):
        o_ref[...] = (a_ref[...].astype(F32) + b_ref[...].astype(F32)).astype(o_ref.dtype)

    blk = (None, None, bm, w)
    return pl.pallas_call(
        body, name=name,
        grid_spec=pltpu.PrefetchScalarGridSpec(
            num_scalar_prefetch=1, grid=(ns, t, nb),
            in_specs=[pl.BlockSpec(blk, lambda s, tt, i, c: (s, tt, c[0] * nb + i, 0)),
                      pl.BlockSpec(blk, lambda s, tt, i, c: (s, tt, i, 0))],
            out_specs=pl.BlockSpec(blk, lambda s, tt, i, c: (s, tt, i, 0))),
        out_shape=jax.ShapeDtypeStruct((ns, t, rh, w), BF16),
        compiler_params=_params(3 * _nbytes((bm, w), F32)),
    )(cidx, g, got)


def _chip_sum(name, p):
    ns, th, r, w = p.shape
    bm = _block_rows(r, w, budget=2 << 20)

    def body(p_ref, o_ref):
        acc = p_ref[0].astype(F32)
        for s in range(1, ns):
            acc = acc + p_ref[s].astype(F32)
        o_ref[...] = acc

    return pl.pallas_call(
        body, name=name, grid=(th, r // bm),
        in_specs=[pl.BlockSpec((ns, None, bm, w), lambda tt, i: (0, tt, i, 0))],
        out_specs=pl.BlockSpec((None, bm, w), lambda tt, i: (tt, i, 0)),
        out_shape=jax.ShapeDtypeStruct((th, r, w), F32),
        compiler_params=_params(ns * _nbytes((bm, w), BF16) + 2 * _nbytes((bm, w), F32)),
    )(p)


def _sum_leading(name, x):
    n = x.shape[0]

    def body(p_ref, o_ref):
        acc = p_ref[0]
        for s in range(1, n):
            acc = acc + p_ref[s]
        o_ref[...] = acc

    return pl.pallas_call(body, name=name, out_shape=jax.ShapeDtypeStruct(x.shape[1:], F32),
                          compiler_params=_params(2 * _nbytes(x.shape, F32)))(x)


def _reduce_scatter_begin(tag, gs, overlap):
    cidx = lax.axis_index("c").astype(jnp.int32).reshape(1)
    got = _pair_swap_groups(tag + "_pair_swap", gs)
    pair = [_pair_sum(f"{tag}_pair_sum{i}", g, r_, cidx) for i, (g, r_) in enumerate(zip(gs, got))]
    if overlap:
        return _gather_start(tag + "_scatter_start", pair, scatter=True)
    return _chip_scatter_groups(tag + "_chip_scatter", pair)


def _reduce_scatter_end(tag, state, overlap, after):
    if overlap:
        srcs, lands = _gather_wait(tag + "_scatter_wait", state, after, scatter=True)
        state = _place_own_part(lands, srcs)
    fin = [_chip_sum(f"{tag}_chip_sum{i}", p) for i, p in enumerate(state)]
    return _pair_merge_groups(tag + "_pair_merge", fin)


_GROUPS = ((("ffn1_wg", "ffn1_wu", "ffn2_wg", "ffn2_wu"), 1), (("ffn1_wd", "ffn2_wd"), 0), (("w_a",), 0),
           (("w_o",), 0), (("w_in",), 1), (("w_b",), 1))


def _shard_major(g, ax):
    k, n = g.shape
    if ax == 0:
        return g.reshape(4, k // 4, n)
    return g.reshape(k, 4, n // 4).transpose(1, 0, 2)


def _in_cols(d):
    o1 = 3 * DN_WIDTH
    o2 = o1 + DN_WIDTH
    o3 = o2 + 2 * DN_HEADS
    o4 = o3 + 3 * DA_WIDTH
    return dict(wq=(0, o1), wz=(o1, o2), wba=(o2, o3), wda=(o3, o4), wg=(o4, o4 + 2 * d))


def _mixer_weights(w_in, w_a, w_b, w_o, d):
    w = {k: w_in[:, a:b] for k, (a, b) in _in_cols(d).items()}
    w["wba"] = jnp.pad(w["wba"], ((0, 0), (0, LANES - 2 * DN_HEADS)))
    w["w_a"], w["w_b"], w["w_o"] = w_a, w_b, w_o
    return w


def _w_in_grad(wg):
    return jnp.concatenate([wg["wq"], wg["wz"], wg["wba"][:, :2 * DN_HEADS], wg["wda"], wg["wg"]], axis=1)


def _adam_math(wv, gv, mv, vv):
    mn = ADAM_B1 * mv + (1.0 - ADAM_B1) * gv
    vn = ADAM_B2 * vv + (1.0 - ADAM_B2) * jnp.square(gv)
    m_hat = mn / (1.0 - ADAM_B1 ** ADAM_STEP)
    v_hat = vn / (1.0 - ADAM_B2 ** ADAM_STEP)
    delta = -ADAM_LR * (m_hat / (jnp.sqrt(v_hat) + ADAM_EPS) + ADAM_WD * wv)
    return delta, mn, vn


def _adamw(name, w, g, m, v):
    shape = w.shape
    cols = shape[-1]
    w2, g2, m2, v2 = (t.reshape(-1, cols) for t in (w, g, m, v))
    rows = w2.shape[0]
    bm = _pick(rows, (256, 128, 64, 32, 16, 8)) if rows >= 8 else rows
    delta, mn, vn = _rowwise(name, lambda *t: (_adam_math(*t), ()), [w2, g2, m2, v2], [], [(cols, F32)] * 3, bm=bm)
    return delta.reshape(shape), mn.reshape(shape), vn.reshape(shape)


def _adamw_leading(name, w, g, m, v):
    n = w.shape[0]
    padded_row = -(-w.shape[1] // 8) * 8 * w.shape[2] * 4
    bm = max(c for c in range(1, n + 1) if n % c == 0 and (c * padded_row <= (1 << 20) or c == 1))

    def body(w_ref, g_ref, m_ref, v_ref, d_ref, mo_ref, vo_ref):
        d_ref[...], mo_ref[...], vo_ref[...] = _adam_math(w_ref[...], g_ref[...], m_ref[...], v_ref[...])

    spec = pl.BlockSpec((bm,) + w.shape[1:], lambda i: (i, 0, 0))
    return pl.pallas_call(
        body, name=name, grid=(n // bm,), in_specs=[spec] * 4, out_specs=[spec] * 3,
        out_shape=[jax.ShapeDtypeStruct(w.shape, F32)] * 3, compiler_params=_params(7 * bm * padded_row),
    )(w, g, m, v)


def _adamw_layer(name, layer, w, m, v, g, prev):
    _, r, cdim = w.shape
    bm = _block_rows(r, cdim, budget=1 << 20)

    def body(w_ref, m_ref, v_ref, g_ref, *rest):
        go_ref, d_ref, mo_ref, vo_ref = rest[-4:]
        gv = g_ref[...]
        go_ref[...] = gv
        d_ref[...], mo_ref[...], vo_ref[...] = _adam_math(w_ref[...], gv, m_ref[...], v_ref[...])

    nat = pl.BlockSpec((None, bm, cdim), lambda i: (layer, i, 0))
    carried = list(prev) if prev is not None else []
    return pl.pallas_call(
        body, name=name, grid=(r // bm,),
        in_specs=[nat, nat, nat, pl.BlockSpec((bm, cdim), lambda i: (i, 0))] + [_ANY] * len(carried),
        out_specs=[nat] * 4, out_shape=[jax.ShapeDtypeStruct(w.shape, F32)] * 4,
        input_output_aliases={4 + k: k for k in range(len(carried))},
        compiler_params=_params(8 * _nbytes((bm, cdim), F32)),
    )(w, m, v, g, *carried)


def kernel(x, c, ada_w, ada_b, ln_ffn1, ln_mix, ln_ffn2, ffn1_wg, ffn1_wu, ffn1_wd, w_in, conv_w, a_log, dt_bias, dn_norm, w_a, w_b, w_o, ffn2_wg, ffn2_wu, ffn2_wd, final_norm, loss_target, m_ada_w, m_ada_b, m_ln_ffn1, m_ln_mix, m_ln_ffn2, m_ffn1_wg, m_ffn1_wu, m_ffn1_wd, m_w_in, m_conv_w, m_a_log, m_dt_bias, m_dn_norm, m_w_a, m_w_b, m_w_o, m_ffn2_wg, m_ffn2_wu, m_ffn2_wd, m_final_norm, v_ada_w, v_ada_b, v_ln_ffn1, v_ln_mix, v_ln_ffn2, v_ffn1_wg, v_ffn1_wu, v_ffn1_wd, v_w_in, v_conv_w, v_a_log, v_dt_bias, v_dn_norm, v_w_a, v_w_b, v_w_o, v_ffn2_wg, v_ffn2_wu, v_ffn2_wd, v_final_norm):
    names = ["ada_w", "ada_b", "ln_ffn1", "ln_mix", "ln_ffn2", "ffn1_wg", "ffn1_wu", "ffn1_wd", "w_in", "conv_w",
             "a_log", "dt_bias", "dn_norm", "w_a", "w_b", "w_o", "ffn2_wg", "ffn2_wu", "ffn2_wd", "final_norm"]
    wts = dict(zip(names, (ada_w, ada_b, ln_ffn1, ln_mix, ln_ffn2, ffn1_wg, ffn1_wu, ffn1_wd, w_in, conv_w, a_log,
                           dt_bias, dn_norm, w_a, w_b, w_o, ffn2_wg, ffn2_wu, ffn2_wd, final_norm)))
    mom = dict(zip(names, (m_ada_w, m_ada_b, m_ln_ffn1, m_ln_mix, m_ln_ffn2, m_ffn1_wg, m_ffn1_wu, m_ffn1_wd, m_w_in,
                           m_conv_w, m_a_log, m_dt_bias, m_dn_norm, m_w_a, m_w_b, m_w_o, m_ffn2_wg, m_ffn2_wu,
                           m_ffn2_wd, m_final_norm)))
    var = dict(zip(names, (v_ada_w, v_ada_b, v_ln_ffn1, v_ln_mix, v_ln_ffn2, v_ffn1_wg, v_ffn1_wu, v_ffn1_wd, v_w_in,
                           v_conv_w, v_a_log, v_dt_bias, v_dn_norm, v_w_a, v_w_b, v_w_o, v_ffn2_wg, v_ffn2_wu,
                           v_ffn2_wd, v_final_norm)))
    _, s, d = x.shape
    depth = ada_w.shape[0]
    mx, my, mc = lax.axis_index("x"), lax.axis_index("y"), lax.axis_index("c")
    chip = 2 * mx + my
    me = 2 * chip + mc
    nshard = ada_w.shape[2]

    cact = _rowwise("c_silu", lambda cv: ((_silu(cv),), ()), [jnp.pad(c, ((0, 7), (0, 0)))], [], [(d, F32)], bm=8)[0]
    c_all = _allgather8("ag_c", cact)[:, 0, :]
    conv_all = _allgather8("ag_conv", jnp.pad(conv_w.reshape(depth * DN_CONV, -1), ((0, 8 - depth * DN_CONV), (0, 0))))
    conv_full = jnp.concatenate([conv_all[2 * j, :depth * DN_CONV] for j in range(4)], axis=1)
    conv_full = conv_full.reshape(depth, DN_CONV, 3 * DN_WIDTH)
    layer_shards = [[jnp.stack([wts[nm][l].astype(BF16) for nm in nms], axis=0) for nms, _ in _GROUPS]
                    for l in range(depth)]
    gathered0 = _gather_groups("ag_weights0", layer_shards[0])
    rows_of = lambda st: st[:, 0].reshape(-1, st.shape[-1])
    cols_of = lambda st: jnp.concatenate([st[j, 0] for j in range(4)], axis=1)

    def layer_weights(l, after):
        if l == 0:
            got = gathered0
        else:
            srcs, lands = _gather_wait(f"ag_weights{l}_wait", started[l], after)
            got = _pair_forward_groups(f"ag_weights{l}_pair", lands, srcs)
        ga, gb, g_wa, g_wo, g_win, g_wb = got
        return ga, gb, _mixer_weights(cols_of(g_win), rows_of(g_wa), cols_of(g_wb), rows_of(g_wo), d)

    c16 = jnp.pad(c_all, ((0, 8), (0, 0))).astype(BF16)
    parts = []
    for l in range(depth):
        bias = lax.dynamic_slice(ada_b[l], (chip * nshard,), (nshard,)).reshape(1, nshard)
        (mp,) = _matmul(f"ada_fwd{l}", c16, ada_w[l].astype(BF16), epi_bcast=[bias], epi=lambda acc, b: (acc + b,))
        parts.append(mp)
    mod_all = _allgather8("ag_mod", jnp.concatenate(parts, axis=0))
    mod_rows = jnp.concatenate([mod_all[2 * j] for j in range(4)], axis=1)
    mod = jnp.stack([lax.dynamic_index_in_dim(mod_rows, l * 16 + me, axis=0, keepdims=False) for l in range(depth)])

    gathered0, later, mod, conv_full = lax.optimization_barrier((gathered0, layer_shards[1:], mod, conv_full))
    started = {l: _gather_start(f"ag_weights{l}_start", later[l - 1]) for l in range(1, depth)}
    for st in started.values():
        mod = mod + st["token"][0, 0]
    small = dict(conv_w=conv_full, a_log=a_log, dt_bias=dt_bias, dn_norm=dn_norm, ln_ffn1=ln_ffn1, ln_mix=ln_mix,
                 ln_ffn2=ln_ffn2, final_norm=final_norm)
    ffn_names = _GROUPS[0][0] + _GROUPS[1][0]
    rs_state, first_layer = {}, {}

    def on_layer_grads(l, wg):
        wg["w_in"] = _w_in_grad(wg)
        gs = [jnp.stack([wg[nm] if nm in ffn_names else _shard_major(wg[nm], ax) for nm in nms], axis=1)
              for nms, ax in _GROUPS]
        if l == 0:
            first_layer["gs"] = gs
            return None
        rs_state[l] = _reduce_scatter_begin(f"rs{l}", gs, overlap=True)
        return rs_state[l]["token"][0, 0]

    loss_part, dx, dmod, sgrads, d_fnorm = _local_step(x[0], loss_target[0], mod, layer_weights, small,
                                                       on_layer_grads)

    dmod_all = _allgather8("ag_dmod", jnp.pad(dmod, ((0, 8 - depth), (0, 0))))
    smalls = [loss_part.reshape(1), d_fnorm]
    for l in range(depth):
        sg = sgrads[l]
        smalls += [sg["ln_ffn1"], sg["ln_mix"], sg["ln_ffn2"], sg["a_log"], sg["dt_bias"], sg["dn_norm"],
                   sg["conv_w"].reshape(-1)]
    sizes = [t.shape[0] for t in smalls]
    tile = 8 * LANES
    flat = jnp.concatenate([jnp.pad(t, (0, (-t.shape[0]) % tile)).reshape(-1, LANES) for t in smalls], axis=0)
    small_all = _allgather8("ag_small", flat)
    dmod_all, small_all, gs0 = lax.optimization_barrier((dmod_all, small_all, first_layer["gs"]))
    rs_state[0] = _reduce_scatter_begin("rs0", gs0, overlap=True)
    started0 = rs_state[0]["token"][0, 0]
    dmod_all = dmod_all + started0
    small_all = small_all + started0

    g_ada_w, g_ada_b = [], []
    for l in range(depth):
        dm_l = dmod_all[:, l, :]
        (gb_l,) = _rowwise(f"ada_b_grad{l}", lambda v: ((), (jnp.sum(v, axis=0, keepdims=True),)), [dm_l], [], [],
                           [(1, N_ADA * d)], bm=8)
        g_ada_b.append(gb_l[0])
        dm_sh = lax.dynamic_slice(dm_l, (0, chip * nshard), (8, nshard))
        (gw_l,) = _matmul(f"ada_w_grad{l}", c16, jnp.pad(dm_sh, ((0, 8), (0, 0))).astype(BF16), ta=True)
        g_ada_w.append(gw_l)
    grads = dict(ada_w=jnp.stack(g_ada_w), ada_b=jnp.stack(g_ada_b))

    tot = _sum_leading("small_sum", small_all)
    offs, acc = [], 0
    for n_ in sizes:
        offs.append(acc)
        acc += -(-n_ // tile) * 8
    take = lambda i: tot[offs[i]:offs[i] + -(-sizes[i] // tile) * 8].reshape(-1)[:sizes[i]]
    loss = take(0)[0]
    grads["final_norm"] = take(1)
    per = 7
    for key_i, key in enumerate(["ln_ffn1", "ln_mix", "ln_ffn2", "a_log", "dt_bias", "dn_norm"]):
        grads[key] = jnp.stack([take(2 + per * l + key_i) for l in range(depth)])
    conv_g = jnp.stack([take(2 + per * l + 6).reshape(DN_CONV, 3 * DN_WIDTH) for l in range(depth)])
    csh = conv_w.shape[2]
    grads["conv_w"] = lax.dynamic_slice(conv_g, (0, 0, chip * csh), (depth, DN_CONV, csh))

    deltas, new_m, new_v = {}, {}, {}
    big = {nm for nms, _ in _GROUPS for nm in nms}
    for name in names:
        if name in big:
            continue
        wv, gv, mv, vv = wts[name], grads[name], mom[name], var[name]
        if wv.ndim == 1:
            wv, gv, mv, vv = (t.reshape(-1, LANES) for t in (wv, gv, mv, vv))
        dl, mn, vn = _adamw("adamw_" + name, wv, gv, mv, vv)
        deltas[name], new_m[name], new_v[name] = (t.reshape(wts[name].shape) for t in (dl, mn, vn))

    swap = lambda t: jnp.swapaxes(t, 1, 2)
    held_transposed = lambda nm, ax: ax == 1 and wts[nm].shape[2] % LANES != 0
    joint = [nm for nms, ax in _GROUPS for nm in nms if nm == "w_in" and held_transposed(nm, ax)]
    reduced, carried = {}, {}
    after = dmod_all
    for l in range(depth - 1, -1, -1):
        reduced[l] = _reduce_scatter_end(f"rs{l}", rs_state[l], True, after)
        for gi, (nms, ax) in enumerate(_GROUPS):
            for q, nm in enumerate(nms):
                if nm in joint:
                    continue
                g = reduced[l][gi][q]
                view = swap if held_transposed(nm, ax) else (lambda t: t)
                carried[nm] = _adamw_layer(f"adamw_{nm}_l{l}", l, view(wts[nm]), view(mom[nm]), view(var[nm]),
                                           g.T if held_transposed(nm, ax) else g, carried.get(nm))
                after = carried[nm][1]
    for nm, res in carried.items():
        ax = dict((n_, a_) for nms, a_ in _GROUPS for n_ in nms)[nm]
        view = swap if held_transposed(nm, ax) else (lambda t: t)
        grads[nm], deltas[nm], new_m[nm], new_v[nm] = (view(t) for t in res)
    for nm in joint:
        gi = [i for i, (nms, _) in enumerate(_GROUPS) if nm in nms][0]
        tr = lambda t: jnp.transpose(t, (2, 0, 1))
        back = lambda t: jnp.transpose(t, (1, 2, 0))
        gt = jnp.stack([reduced[l][gi][0].T for l in range(depth)], axis=1)
        dl, mn, vn = _adamw_leading("adamw_" + nm, tr(wts[nm]), gt, tr(mom[nm]), tr(var[nm]))
        grads[nm], deltas[nm], new_m[nm], new_v[nm] = back(gt), back(dl), back(mn), back(vn)

    return (loss, dx.reshape(1, s, d), *[grads[n_] for n_ in names], *[deltas[n_] for n_ in names],
            *[new_m[n_] for n_ in names], *[new_v[n_] for n_ in names])
```

```python
import functools

import jax
import jax.numpy as jnp
from jax import lax
from jax.experimental import pallas as pl
from jax.experimental.pallas import tpu as pltpu

F32 = jnp.float32
BF16 = jnp.bfloat16
MESH = pl.DeviceIdType.MESH

NORM_EPS = 1e-6
DN_HEADS, DN_DIM, DN_CHUNK, DN_CONV = 8, 128, 64, 4
DN_WIDTH = DN_HEADS * DN_DIM
DA_HEADS, DA_DIM, DA_BLOCK = 12, 64, 128
DA_WIDTH = DA_HEADS * DA_DIM
DA_PATTERNS = ((128, 1), (512, 4), (2048, 16))
ALIBI_MAX_EXP = 8.0
N_ADA = 9
LANES = 128
V7X_VMEM_BYTES = 64 << 20
ADAM_LR, ADAM_B1, ADAM_B2, ADAM_EPS, ADAM_WD, ADAM_STEP = 0.001, 0.9, 0.999, 1e-08, 0.01, 10
NEG = -1e30
HI = lax.Precision.HIGHEST
NN = (((1,), (0,)), ((), ()))
NT = (((1,), (1,)), ((), ()))
TN = (((0,), (0,)), ((), ()))


def _nbytes(shape, dtype):
    n = 1
    for s in shape:
        n *= s
    return n * jnp.dtype(dtype).itemsize


def _params(block_bytes, scratch_bytes=0):
    need = 2 * block_bytes + scratch_bytes
    lim = min(max(need + need // 4 + (4 << 20), 32 << 20), V7X_VMEM_BYTES - (6 << 20))
    return pltpu.CompilerParams(vmem_limit_bytes=int(lim))


def _pick(n, cands):
    for c in cands:
        if c <= n and n % c == 0:
            return c
    return n


def _sigmoid(x):
    return jax.nn.sigmoid(x)


def _silu(x):
    return x * jax.nn.sigmoid(x)


def _softplus(x):
    return jnp.maximum(x, 0.0) + jnp.log(1.0 + jnp.exp(-jnp.abs(x)))


def _rowwise(name, fn, rows, bcast, row_outs, red_outs=(), bm=1024):
    rows = [r if isinstance(r, tuple) else (r, r.shape[1], 0) for r in rows]
    s = rows[0][0].shape[0]
    bm = _pick(s, (bm, 128, 64, 32, 16, 8))
    nr, nb, no, nd = len(rows), len(bcast), len(row_outs), len(red_outs)
    in_specs = [pl.BlockSpec((bm, w), functools.partial(lambda i, ci: (i, ci), ci=ci)) for (_, w, ci) in rows]
    in_specs += [pl.BlockSpec(b.shape, lambda i: (0, 0)) for b in bcast]
    out_shape = [jax.ShapeDtypeStruct((s, w), dt) for (w, dt) in row_outs]
    out_shape += [jax.ShapeDtypeStruct((r, w), F32) for (r, w) in red_outs]
    out_specs = [pl.BlockSpec((bm, w), lambda i: (i, 0)) for (w, _) in row_outs]
    out_specs += [pl.BlockSpec((r, w), lambda i: (0, 0)) for (r, w) in red_outs]

    def body(*refs):
        ins = [r[...] for r in refs[:nr + nb]]
        outs = refs[nr + nb:nr + nb + no]
        reds = refs[nr + nb + no:]
        ov, rv = fn(*ins)
        for o, v in zip(outs, ov):
            o[...] = v.astype(o.dtype)
        if nd:
            @pl.when(pl.program_id(0) == 0)
            def _():
                for r in reds:
                    r[...] = jnp.zeros(r.shape, F32)
            for r, v in zip(reds, rv):
                r[...] += v.astype(F32)

    blk = sum(_nbytes((bm, w), a.dtype) for (a, w, _) in rows) + sum(_nbytes(b.shape, b.dtype) for b in bcast)
    blk += sum(_nbytes((bm, w), dt) for (w, dt) in row_outs) + sum(_nbytes(r, F32) for r in red_outs)
    res = pl.pallas_call(
        body, name=name, grid=(s // bm,), in_specs=in_specs, out_specs=out_specs, out_shape=out_shape,
        compiler_params=_params(3 * blk),
    )(*[a for (a, _, _) in rows], *bcast)
    return res


def _matmul(name, a, b, *, ta=False, tb=False, outs=(F32,), epi=None, epi_rows=(), epi_bcast=(),
            bm=None, bn=None, bk=None):
    if ta:
        k, m = a.shape
    else:
        m, k = a.shape
    n = b.shape[0] if tb else b.shape[1]
    assert (b.shape[1] if tb else b.shape[0]) == k, (name, a.shape, b.shape)
    if bm is None:
        bm = _pick(m, (1024, 1408, 768, 512, 384, 256, 128)) if ta else _pick(m, (1024, 512, 256, 128, 64, 32, 16))
    if bk is None:
        bk = k if k <= 3072 else _pick(k, (2816, 2048, 1024, 512))
        if ta:
            bk = _pick(k, (1024, 512, 256, 128, 64, 32, 16))
    if bn is None:
        bn = _pick(n, (1024, 768, 512, 384, 256, 128) if bk <= 2048 else (512, 384, 256, 128))
    nk = k // bk
    dims = TN if ta else (NT if tb else NN)
    a_spec = pl.BlockSpec((bk, bm), lambda i, j, kk: (kk, i)) if ta else pl.BlockSpec((bm, bk), lambda i, j, kk: (i, kk))
    b_spec = pl.BlockSpec((bn, bk), lambda i, j, kk: (j, kk)) if tb else pl.BlockSpec((bk, bn), lambda i, j, kk: (kk, j))
    in_specs = [a_spec, b_spec]
    in_specs += [pl.BlockSpec((bm, bn), lambda i, j, kk: (i, j)) for _ in epi_rows]
    in_specs += [pl.BlockSpec((1, bn), lambda i, j, kk: (0, j)) for _ in epi_bcast]
    out_shape = [jax.ShapeDtypeStruct((m, n), dt) for dt in outs]
    out_specs = [pl.BlockSpec((bm, bn), lambda i, j, kk: (i, j)) for _ in outs]
    ner, neb, no = len(epi_rows), len(epi_bcast), len(outs)

    def body(*refs):
        a_ref, b_ref = refs[0], refs[1]
        extra = refs[2:2 + ner + neb]
        out_refs = refs[2 + ner + neb:2 + ner + neb + no]
        prod = lax.dot_general(a_ref[...], b_ref[...], dims, preferred_element_type=F32)

        def finish(acc):
            vals = epi(acc, *[r[...] for r in extra]) if epi is not None else (acc,)
            for o, v in zip(out_refs, vals):
                o[...] = v.astype(o.dtype)

        if nk == 1:
            finish(prod)
        else:
            acc_ref = refs[-1]
            kk = pl.program_id(2)

            @pl.when(kk == 0)
            def _():
                acc_ref[...] = prod

            @pl.when(kk > 0)
            def _():
                acc_ref[...] += prod

            @pl.when(kk == nk - 1)
            def _():
                finish(acc_ref[...])

    blk = _nbytes((bm, bk), a.dtype) + _nbytes((bk, bn), b.dtype)
    blk += sum(_nbytes((bm, bn), r.dtype) for r in epi_rows) + sum(_nbytes((bm, bn), dt) for dt in outs)
    scratch = [pltpu.VMEM((bm, bn), F32)] if nk > 1 else []
    res = pl.pallas_call(
        body, name=name, grid=(m // bm, n // bn, nk), in_specs=in_specs, out_specs=out_specs,
        out_shape=out_shape, scratch_shapes=scratch,
        compiler_params=_params(blk, 3 * _nbytes((bm, bn), F32)),
    )(a, b, *epi_rows, *epi_bcast)
    return res


def _mm_core(name, grid, nk, pairs, out_defs, acc_shape, epi=None, epi_ins=()):
    npair, nep, no = len(pairs), len(epi_ins), len(out_defs)

    def body(*refs):
        extra = refs[2 * npair:2 * npair + nep]
        out_refs = refs[2 * npair + nep:2 * npair + nep + no]
        prod = None
        for p in range(npair):
            d = lax.dot_general(refs[2 * p][...], refs[2 * p + 1][...], pairs[p][4], preferred_element_type=F32)
            prod = d if prod is None else prod + d

        def finish(acc):
            vals = epi(acc, *[r[...] for r in extra]) if epi is not None else (acc,)
            for o, v in zip(out_refs, vals):
                o[...] = v.astype(o.dtype)

        if nk == 1:
            finish(prod)
        else:
            acc_ref = refs[-1]
            kk = pl.program_id(2)

            @pl.when(kk == 0)
            def _():
                acc_ref[...] = prod

            @pl.when(kk > 0)
            def _():
                acc_ref[...] += prod

            @pl.when(kk == nk - 1)
            def _():
                finish(acc_ref[...])

    def blk_bytes(spec, dtype):
        return _nbytes([s for s in spec.block_shape if s is not None], dtype)

    blk = sum(blk_bytes(sa, a.dtype) + blk_bytes(sb, b.dtype) for (a, sa, b, sb, _) in pairs)
    blk += sum(blk_bytes(sp, arr.dtype) for (arr, sp) in epi_ins) + sum(blk_bytes(sp, dt) for (_, dt, sp) in out_defs)
    ins, in_specs = [], []
    for (a, sa, b, sb, _) in pairs:
        ins += [a, b]
        in_specs += [sa, sb]
    ins += [arr for (arr, _) in epi_ins]
    in_specs += [sp for (_, sp) in epi_ins]
    return pl.pallas_call(
        body, name=name, grid=grid, in_specs=in_specs, out_specs=[sp for (_, _, sp) in out_defs],
        out_shape=[jax.ShapeDtypeStruct(sh, dt) for (sh, dt, _) in out_defs],
        scratch_shapes=[pltpu.VMEM(acc_shape, F32)] if nk > 1 else [],
        compiler_params=_params(blk, 3 * _nbytes(acc_shape, F32)),
    )(*ins)


def _rms_mod(h, ln, sh, sc):
    n = h * lax.rsqrt(jnp.mean(h * h, axis=-1, keepdims=True) + NORM_EPS) * ln
    return n * (1.0 + sc) + sh


def _swiglu_act(g, u):
    return _silu(g.astype(F32)) * u.astype(F32)


def _dn_prep(yc, pba, alog, dtb):
    act = _silu(yc)
    parts = []
    for idx in range(2 * DN_HEADS):
        seg = act[:, idx * DN_DIM:(idx + 1) * DN_DIM]
        seg = seg * lax.rsqrt(jnp.sum(seg * seg, axis=-1, keepdims=True) + NORM_EPS)
        if idx < DN_HEADS:
            seg = seg * (DN_DIM ** -0.5)
        parts.append(seg)
    parts.append(act[:, 2 * DN_WIDTH:])
    qkvn = jnp.concatenate(parts, axis=1)
    lane = lax.broadcasted_iota(jnp.int32, pba.shape, 1)
    beta = _sigmoid(pba)
    g = -jnp.exp(alog) * _softplus(pba + dtb)
    gb = jnp.where(lane < DN_HEADS, beta, jnp.where(lane < 2 * DN_HEADS, g, 0.0))
    return qkvn, gb


def _dn_outnorm(o_a, z, dn):
    parts = []
    for h in range(DN_HEADS):
        seg = o_a[:, h * DN_DIM:(h + 1) * DN_DIM]
        seg = seg * lax.rsqrt(jnp.mean(seg * seg, axis=-1, keepdims=True) + NORM_EPS) * dn
        parts.append(seg)
    return jnp.concatenate(parts, axis=1) * _silu(z)


def _shift_down(x, halo8, s):
    r = pltpu.roll(x, s, axis=0)
    top = pltpu.roll(halo8, s, axis=0)
    i8 = lax.broadcasted_iota(jnp.int32, top.shape, 0)
    return jnp.concatenate([jnp.where(i8 < s, top, r[0:8]), r[8:]], axis=0)


def _shift_up(x, halo8, s):
    m = x.shape[0]
    r = pltpu.roll(x, m - s, axis=0)
    bot = pltpu.roll(halo8, 8 - s, axis=0)
    i8 = lax.broadcasted_iota(jnp.int32, bot.shape, 0)
    return jnp.concatenate([r[:m - 8], jnp.where(i8 >= 8 - s, bot, r[m - 8:])], axis=0)


def _conv_prep_fwd(name, pq, convw8, pba, alog, dtb, bm=512):
    s, w = pq.shape
    nblk = s // bm
    hb = bm // 16

    def body(x_ref, halo_ref, w_ref, pba_ref, alog_ref, dtb_ref, yc_ref, qkv_ref, gb_ref):
        i = pl.program_id(0)
        x = x_ref[...].astype(F32)
        halo = jnp.where(i > 0, halo_ref[...].astype(F32)[8:16], 0.0)
        cw = w_ref[...]
        y = x * cw[DN_CONV - 1:DN_CONV]
        for sft in range(1, DN_CONV):
            y = y + _shift_down(x, halo, sft) * cw[DN_CONV - 1 - sft:DN_CONV - sft]
        ycb = y.astype(BF16)
        yc_ref[...] = ycb
        qkvn, gb = _dn_prep(ycb.astype(F32), pba_ref[...], alog_ref[...], dtb_ref[...])
        qkv_ref[...] = qkvn.astype(BF16)
        gb_ref[...] = gb

    blk = 3 * _nbytes((bm, w), BF16) + 4 * _nbytes((bm, w), F32)
    return pl.pallas_call(
        body, name=name, grid=(nblk,),
        in_specs=[pl.BlockSpec((bm, w), lambda i: (i, 0)),
                  pl.BlockSpec((16, w), lambda i: (jnp.maximum(i * hb - 1, 0), 0)),
                  pl.BlockSpec(convw8.shape, lambda i: (0, 0)),
                  pl.BlockSpec((bm, LANES), lambda i: (i, 0)),
                  pl.BlockSpec((1, LANES), lambda i: (0, 0)),
                  pl.BlockSpec((1, LANES), lambda i: (0, 0))],
        out_specs=[pl.BlockSpec((bm, w), lambda i: (i, 0)), pl.BlockSpec((bm, w), lambda i: (i, 0)),
                   pl.BlockSpec((bm, LANES), lambda i: (i, 0))],
        out_shape=[jax.ShapeDtypeStruct((s, w), BF16), jax.ShapeDtypeStruct((s, w), BF16),
                   jax.ShapeDtypeStruct((s, LANES), F32)],
        compiler_params=_params(blk),
    )(pq, pq, convw8, pba, alog, dtb)


def _conv_bwd(name, dyc, pq, convw8, bm=512):
    s, w = pq.shape
    nblk = s // bm
    hb = bm // 16

    def body(dy_ref, dyn_ref, x_ref, xh_ref, w_ref, dx_ref, dw_ref):
        i = pl.program_id(0)
        dy = dy_ref[...].astype(F32)
        nxt = jnp.where(i < nblk - 1, dyn_ref[...].astype(F32)[0:8], 0.0)
        x = x_ref[...].astype(F32)
        halo = jnp.where(i > 0, xh_ref[...].astype(F32)[8:16], 0.0)
        cw = w_ref[...]
        dx = dy * cw[DN_CONV - 1:DN_CONV]
        for sft in range(1, DN_CONV):
            dx = dx + _shift_up(dy, nxt, sft) * cw[DN_CONV - 1 - sft:DN_CONV - sft]
        dx_ref[...] = dx.astype(dx_ref.dtype)
        r8 = lax.broadcasted_iota(jnp.int32, (8, w), 0)
        dw = jnp.zeros((8, w), F32)
        for j in range(DN_CONV):
            sft = DN_CONV - 1 - j
            xs = x if sft == 0 else _shift_down(x, halo, sft)
            dw = dw + jnp.where(r8 == j, jnp.sum(dy * xs, axis=0, keepdims=True), 0.0)

        @pl.when(i == 0)
        def _():
            dw_ref[...] = jnp.zeros((8, w), F32)
        dw_ref[...] += dw

    blk = 4 * _nbytes((bm, w), BF16) + 5 * _nbytes((bm, w), F32)
    return pl.pallas_call(
        body, name=name, grid=(nblk,),
        in_specs=[pl.BlockSpec((bm, w), lambda i: (i, 0)),
                  pl.BlockSpec((16, w), lambda i: (jnp.minimum((i + 1) * hb, s // 16 - 1), 0)),
                  pl.BlockSpec((bm, w), lambda i: (i, 0)),
                  pl.BlockSpec((16, w), lambda i: (jnp.maximum(i * hb - 1, 0), 0)),
                  pl.BlockSpec(convw8.shape, lambda i: (0, 0))],
        out_specs=[pl.BlockSpec((bm, w), lambda i: (i, 0)), pl.BlockSpec((8, w), lambda i: (0, 0))],
        out_shape=[jax.ShapeDtypeStruct((s, w), BF16), jax.ShapeDtypeStruct((8, w), F32)],
        compiler_params=_params(blk),
    )(dyc, dyc, pq, pq, convw8)


BNN = (((2,), (1,)), ((0,), (0,)))
BNT = (((2,), (2,)), ((0,), (0,)))
BTN = (((1,), (1,)), ((0,), (0,)))


def _raw_dot_1pass(a, b, dims):
    return lax.dot_general(a.astype(BF16), b.astype(BF16), dims, preferred_element_type=F32)


def _raw_dot_3pass(a, b, dims):
    ah = a.astype(BF16)
    al = (a - ah.astype(F32)).astype(BF16)
    bh = b.astype(BF16)
    bl = (b - bh.astype(F32)).astype(BF16)
    d = lambda x, y: lax.dot_general(x, y, dims, preferred_element_type=F32)
    return d(ah, bh) + (d(ah, bl) + d(al, bh))


def _with_same_precision_vjp(raw):
    @functools.partial(jax.custom_vjp, nondiff_argnums=(2,))
    def dot(a, b, dims):
        return raw(a, b, dims)

    def fwd(a, b, dims):
        return raw(a, b, dims), (a, b)

    def bwd(dims, res, ct):
        a, b = res
        if dims == BNN:
            return raw(ct, b, BNT), raw(a, ct, BTN)
        if dims == BNT:
            return raw(ct, b, BNN), raw(ct, a, BTN)
        assert dims == BTN
        return raw(b, ct, BNT), raw(a, ct, BNN)

    dot.defvjp(fwd, bwd)
    return dot


_dot_1pass_vjp = _with_same_precision_vjp(_raw_dot_1pass)
_dot_3pass_vjp = _with_same_precision_vjp(_raw_dot_3pass)


def _dot_bf16(a, b, dims=BNN):
    return _dot_1pass_vjp(a, b, dims)


def _dot_3pass(a, b, dims=BNN):
    return _dot_3pass_vjp(a, b, dims)


def _neumann_inverse(x):
    h, c, _ = x.shape
    eye = lax.broadcasted_iota(jnp.int32, (h, c, c), 1) == lax.broadcasted_iota(jnp.int32, (h, c, c), 2)
    t = jnp.where(eye, 1.0, 0.0) + x
    p = x
    for _ in range(5):
        p = _raw_dot_3pass(p, p, BNN)
        t = t + _raw_dot_3pass(t, p, BNN)
    return t


@jax.custom_vjp
def _known_inverse(x, t):
    return t


def _known_inverse_fwd(x, t):
    return t, t


def _known_inverse_bwd(t, ct):
    return _raw_dot_3pass(_raw_dot_3pass(t, ct, BTN), t, BNT), jnp.zeros_like(t)


_known_inverse.defvjp(_known_inverse_fwd, _known_inverse_bwd)


def _delta_chunk(q, k, v, gcol, bcol, state, t_known=None):
    h, c, _ = q.shape
    row = lax.broadcasted_iota(jnp.int32, (h, c, c), 1)
    col = lax.broadcasted_iota(jnp.int32, (h, c, c), 2)
    incl, strict, eye = row >= col, row > col, row == col
    g_b = jnp.broadcast_to(gcol, (h, c, c))
    gc_row = jnp.sum(jnp.where(row <= col, g_b, 0.0), axis=1, keepdims=True)
    g_r = jnp.sum(jnp.where(eye, g_b, 0.0), axis=1, keepdims=True)
    gc_col = jnp.sum(jnp.where(incl, jnp.broadcast_to(g_r, (h, c, c)), 0.0), axis=2, keepdims=True)
    decay = jnp.exp(jnp.where(incl, gc_col - gc_row, NEG))
    kb = k * bcol
    vb = v * bcol
    x = -jnp.where(strict, _dot_bf16(kb, k, BNT) * decay, 0.0)
    t = _neumann_inverse(x) if t_known is None else _known_inverse(x, t_known)
    eg = jnp.exp(gc_col)
    u = _dot_3pass(t, vb)
    w = _dot_3pass(t, kb * eg)
    qk = _dot_bf16(q, k, BNT) * decay
    v_new = u - _dot_bf16(w, state)
    o = _dot_bf16(q * eg, state) + _dot_bf16(qk, v_new)
    g_last = jnp.sum(g_r, axis=2, keepdims=True)
    new_state = state * jnp.exp(g_last) + _dot_bf16(k * jnp.exp(g_last - gc_col), v_new, BTN)
    return o, new_state, t


def _lane_col(blk, idx):
    lane = lax.broadcasted_iota(jnp.int32, blk.shape, 1)
    return jnp.sum(jnp.where(lane == idx, blk, 0.0), axis=1, keepdims=True)


def _dn_heads(ref, base):
    return jnp.stack([ref[:, base + h * DN_DIM:base + (h + 1) * DN_DIM] for h in range(DN_HEADS)], axis=0).astype(F32)


def _dn_cols(gbv, base):
    return jnp.stack([_lane_col(gbv, base + h) for h in range(DN_HEADS)], axis=0)


def _delta_fwd(name, qkvn, gb):
    s = qkvn.shape[0]
    n = s // DN_CHUNK
    c = DN_CHUNK

    def body(qkv_ref, gb_ref, o_ref, st_ref, t_ref, state):
        @pl.when(pl.program_id(0) == 0)
        def _():
            state[...] = jnp.zeros(state.shape, F32)

        gbv = gb_ref[...]
        st = state[...]
        st_ref[0] = st
        o, new, t = _delta_chunk(_dn_heads(qkv_ref, 0), _dn_heads(qkv_ref, DN_WIDTH), _dn_heads(qkv_ref, 2 * DN_WIDTH),
                                 _dn_cols(gbv, DN_HEADS), _dn_cols(gbv, 0), st)
        for h in range(DN_HEADS):
            o_ref[:, h * DN_DIM:(h + 1) * DN_DIM] = o[h]
        t_ref[0] = t
        state[...] = new

    blk = _nbytes((c, 3 * DN_WIDTH), BF16) + _nbytes((c, LANES), F32) + _nbytes((c, DN_WIDTH), F32)
    blk += _nbytes((DN_HEADS, DN_DIM, DN_DIM), F32) + _nbytes((DN_HEADS, c, c), F32)
    return pl.pallas_call(
        body, name=name, grid=(n,),
        in_specs=[pl.BlockSpec((c, 3 * DN_WIDTH), lambda i: (i, 0)), pl.BlockSpec((c, LANES), lambda i: (i, 0))],
        out_specs=[pl.BlockSpec((c, DN_WIDTH), lambda i: (i, 0)),
                   pl.BlockSpec((1, DN_HEADS, DN_DIM, DN_DIM), lambda i: (i, 0, 0, 0)),
                   pl.BlockSpec((1, DN_HEADS, c, c), lambda i: (i, 0, 0, 0))],
        out_shape=[jax.ShapeDtypeStruct((s, DN_WIDTH), F32),
                   jax.ShapeDtypeStruct((n, DN_HEADS, DN_DIM, DN_DIM), F32),
                   jax.ShapeDtypeStruct((n, DN_HEADS, c, c), F32)],
        scratch_shapes=[pltpu.VMEM((DN_HEADS, DN_DIM, DN_DIM), F32)],
        compiler_params=_params(blk, 8 << 20),
    )(qkvn, gb)


def _delta_bwd(name, qkvn, gb, states, tinv, d_o):
    s = qkvn.shape[0]
    n = s // DN_CHUNK
    c = DN_CHUNK

    def body(qkv_ref, gb_ref, st_ref, t_ref, do_ref, dqkv_ref, dgb_ref, dstate):
        @pl.when(pl.program_id(0) == 0)
        def _():
            dstate[...] = jnp.zeros(dstate.shape, F32)

        gbv = gb_ref[...]
        lane = lax.broadcasted_iota(jnp.int32, (c, LANES), 1)
        t_known = t_ref[0]
        chunk = lambda *args: _delta_chunk(*args, t_known=t_known)[:2]
        _, vjp = jax.vjp(chunk, _dn_heads(qkv_ref, 0), _dn_heads(qkv_ref, DN_WIDTH),
                         _dn_heads(qkv_ref, 2 * DN_WIDTH), _dn_cols(gbv, DN_HEADS), _dn_cols(gbv, 0), st_ref[0])
        dq, dk, dv, dg, db, dst = vjp((_dn_heads(do_ref, 0), dstate[...]))
        dgb = jnp.zeros((c, LANES), F32)
        for h in range(DN_HEADS):
            dqkv_ref[:, h * DN_DIM:(h + 1) * DN_DIM] = dq[h]
            dqkv_ref[:, DN_WIDTH + h * DN_DIM:DN_WIDTH + (h + 1) * DN_DIM] = dk[h]
            dqkv_ref[:, 2 * DN_WIDTH + h * DN_DIM:2 * DN_WIDTH + (h + 1) * DN_DIM] = dv[h]
            dgb = dgb + jnp.where(lane == h, db[h], 0.0) + jnp.where(lane == DN_HEADS + h, dg[h], 0.0)
        dstate[...] = dst
        dgb_ref[...] = dgb

    rev = lambda i: (n - 1 - i, 0)
    blk = _nbytes((c, 3 * DN_WIDTH), BF16) + 2 * _nbytes((c, LANES), F32) + _nbytes((c, DN_WIDTH), F32)
    blk += _nbytes((DN_HEADS, DN_DIM, DN_DIM), F32) + _nbytes((c, 3 * DN_WIDTH), F32)
    return pl.pallas_call(
        body, name=name, grid=(n,),
        in_specs=[pl.BlockSpec((c, 3 * DN_WIDTH), rev), pl.BlockSpec((c, LANES), rev),
                  pl.BlockSpec((1, DN_HEADS, DN_DIM, DN_DIM), lambda i: (n - 1 - i, 0, 0, 0)),
                  pl.BlockSpec((1, DN_HEADS, c, c), lambda i: (n - 1 - i, 0, 0, 0)),
                  pl.BlockSpec((c, DN_WIDTH), rev)],
        out_specs=[pl.BlockSpec((c, 3 * DN_WIDTH), rev), pl.BlockSpec((c, LANES), rev)],
        out_shape=[jax.ShapeDtypeStruct((s, 3 * DN_WIDTH), F32), jax.ShapeDtypeStruct((s, LANES), F32)],
        scratch_shapes=[pltpu.VMEM((DN_HEADS, DN_DIM, DN_DIM), F32)],
        compiler_params=_params(blk, 16 << 20),
    )(qkvn, gb, states, tinv, d_o)


def _da_scores(q2f, k2, sub, valid, distf, head):
    lane = lax.broadcasted_iota(jnp.int32, q2f.shape, 1)
    hmask = (lane < DA_DIM) if sub == 0 else (lane >= DA_DIM)
    qm = jnp.where(hmask, q2f, 0.0).astype(BF16)
    slope = 2.0 ** (-ALIBI_MAX_EXP * (head + 1) / DA_HEADS)
    sc = lax.dot_general(qm, k2, NT, preferred_element_type=F32) * (DA_DIM ** -0.5)
    return jnp.where(valid, sc - slope * distf, NEG), qm, hmask


def _da_mask(i, r):
    qi = lax.broadcasted_iota(jnp.int32, (DA_BLOCK, 2 * DA_BLOCK), 0)
    ki = lax.broadcasted_iota(jnp.int32, (DA_BLOCK, 2 * DA_BLOCK), 1)
    dist = qi + DA_BLOCK - ki
    valid = (dist >= 0) & (dist <= DA_BLOCK) & ((ki >= DA_BLOCK) | (i > 0))
    return valid, (dist * r).astype(F32)


def _da_fwd(name, pda, r):
    s = pda.shape[0]
    n = s // r
    nb = n // DA_BLOCK
    w = DA_WIDTH
    dav = pda.reshape(n, r * 3 * w)

    def body(q_ref, kc_ref, kp_ref, vc_ref, vp_ref, o_ref, lse_ref):
        i = pl.program_id(1)
        valid, distf = _da_mask(i, r)
        lane = lax.broadcasted_iota(jnp.int32, (DA_BLOCK, LANES), 1)
        lse = jnp.zeros((DA_BLOCK, LANES), F32)
        for hp in range(DA_HEADS // 2):
            sl = slice(hp * LANES, (hp + 1) * LANES)
            q2f = q_ref[:, sl].astype(F32)
            k2 = jnp.concatenate([kp_ref[:, sl], kc_ref[:, sl]], axis=0)
            v2 = jnp.concatenate([vp_ref[:, sl], vc_ref[:, sl]], axis=0)
            o2 = None
            for sub in range(2):
                head = 2 * hp + sub
                sc, _, hmask = _da_scores(q2f, k2, sub, valid, distf, head)
                mx = jnp.max(sc, axis=1, keepdims=True)
                p = jnp.exp(sc - mx)
                l = jnp.sum(p, axis=1, keepdims=True)
                pv = lax.dot_general(p.astype(BF16), v2, NN, preferred_element_type=F32) / l
                o2 = pv if sub == 0 else jnp.where(hmask, pv, o2)
                lse = jnp.where(lane == head, mx + jnp.log(l), lse)
            o_ref[:, sl] = o2.astype(o_ref.dtype)
        lse_ref[...] = lse

    prev = lambda col: (lambda p, i: (jnp.maximum(i - 1, 0), 3 * p + col))
    cur = lambda col: (lambda p, i: (i, 3 * p + col))
    blk = 5 * _nbytes((DA_BLOCK, w), BF16) + _nbytes((DA_BLOCK, w), F32) + _nbytes((DA_BLOCK, LANES), F32)
    o, lse = pl.pallas_call(
        body, name=name, grid=(r, nb),
        in_specs=[pl.BlockSpec((DA_BLOCK, w), cur(0)), pl.BlockSpec((DA_BLOCK, w), cur(1)),
                  pl.BlockSpec((DA_BLOCK, w), prev(1)), pl.BlockSpec((DA_BLOCK, w), cur(2)),
                  pl.BlockSpec((DA_BLOCK, w), prev(2))],
        out_specs=[pl.BlockSpec((DA_BLOCK, w), lambda p, i: (i, p)),
                   pl.BlockSpec((DA_BLOCK, LANES), lambda p, i: (i, p))],
        out_shape=[jax.ShapeDtypeStruct((n, r * w), BF16), jax.ShapeDtypeStruct((n, r * LANES), F32)],
        compiler_params=_params(blk, 8 << 20),
    )(dav, dav, dav, dav, dav)
    return o.reshape(s, w), lse.reshape(s, LANES)


def _da_bwd(name, pda, d_ob, lse_tot, delta, r):
    s = pda.shape[0]
    n = s // r
    nb = n // DA_BLOCK
    w = DA_WIDTH
    dav = pda.reshape(n, r * 3 * w)
    dov = d_ob.reshape(n, r * w)
    lv = lse_tot.reshape(n, r * LANES)
    dlv = delta.reshape(n, r * LANES)

    def body(q_ref, kc_ref, kp_ref, vc_ref, vp_ref, do_ref, l_ref, dl_ref, dq_ref, dk_ref, dv_ref, ck, cv):
        i = pl.program_id(1)

        @pl.when(i == 0)
        def _():
            ck[...] = jnp.zeros(ck.shape, F32)
            cv[...] = jnp.zeros(cv.shape, F32)

        @pl.when(i < nb)
        def _():
            valid, distf = _da_mask(i, r)
            lsev = l_ref[...]
            dlt = dl_ref[...]
            for hp in range(DA_HEADS // 2):
                sl = slice(hp * LANES, (hp + 1) * LANES)
                q2f = q_ref[:, sl].astype(F32)
                k2 = jnp.concatenate([kp_ref[:, sl], kc_ref[:, sl]], axis=0)
                v2 = jnp.concatenate([vp_ref[:, sl], vc_ref[:, sl]], axis=0)
                do2f = do_ref[:, sl].astype(F32)
                dq2 = jnp.zeros((DA_BLOCK, LANES), F32)
                dk2 = jnp.zeros((2 * DA_BLOCK, LANES), F32)
                dv2 = jnp.zeros((2 * DA_BLOCK, LANES), F32)
                for sub in range(2):
                    head = 2 * hp + sub
                    sc, qm, hmask = _da_scores(q2f, k2, sub, valid, distf, head)
                    p = jnp.exp(sc - _lane_col(lsev, head))
                    dom = jnp.where(hmask, do2f, 0.0).astype(BF16)
                    dp = lax.dot_general(dom, v2, NT, preferred_element_type=F32)
                    ds = (p * (dp - _lane_col(dlt, head)) * (DA_DIM ** -0.5)).astype(BF16)
                    dq2 = dq2 + jnp.where(hmask, lax.dot_general(ds, k2, NN, preferred_element_type=F32), 0.0)
                    dk2 = dk2 + lax.dot_general(ds, qm, TN, preferred_element_type=F32)
                    dv2 = dv2 + lax.dot_general(p.astype(BF16), dom, TN, preferred_element_type=F32)
                dq_ref[:, sl] = dq2.astype(dq_ref.dtype)
                dk_ref[:, sl] = (ck[:, sl] + dk2[:DA_BLOCK]).astype(dk_ref.dtype)
                dv_ref[:, sl] = (cv[:, sl] + dv2[:DA_BLOCK]).astype(dv_ref.dtype)
                ck[:, sl] = dk2[DA_BLOCK:]
                cv[:, sl] = dv2[DA_BLOCK:]

        @pl.when(i == nb)
        def _():
            dk_ref[...] = ck[...].astype(dk_ref.dtype)
            dv_ref[...] = cv[...].astype(dv_ref.dtype)

    qrow = lambda i: jnp.minimum(i, nb - 1)
    prev = lambda col: (lambda p, i: (jnp.maximum(qrow(i) - 1, 0), 3 * p + col))
    cur = lambda col: (lambda p, i: (qrow(i), 3 * p + col))
    same = lambda p, i: (qrow(i), p)
    late = lambda p, i: (jnp.maximum(i - 1, 0), p)
    blk = 6 * _nbytes((DA_BLOCK, w), BF16) + 2 * _nbytes((DA_BLOCK, LANES), F32) + 3 * _nbytes((DA_BLOCK, w), F32)
    dq, dk, dv = pl.pallas_call(
        body, name=name, grid=(r, nb + 1),
        in_specs=[pl.BlockSpec((DA_BLOCK, w), cur(0)), pl.BlockSpec((DA_BLOCK, w), cur(1)),
                  pl.BlockSpec((DA_BLOCK, w), prev(1)), pl.BlockSpec((DA_BLOCK, w), cur(2)),
                  pl.BlockSpec((DA_BLOCK, w), prev(2)), pl.BlockSpec((DA_BLOCK, w), same),
                  pl.BlockSpec((DA_BLOCK, LANES), same), pl.BlockSpec((DA_BLOCK, LANES), same)],
        out_specs=[pl.BlockSpec((DA_BLOCK, w), same), pl.BlockSpec((DA_BLOCK, w), late),
                   pl.BlockSpec((DA_BLOCK, w), late)],
        out_shape=[jax.ShapeDtypeStruct((n, r * w), BF16)] * 3,
        scratch_shapes=[pltpu.VMEM((DA_BLOCK, w), F32), pltpu.VMEM((DA_BLOCK, w), F32)],
        compiler_params=_params(blk, 12 << 20),
    )(dav, dav, dav, dav, dav, dov, lv, dlv)
    return dq.reshape(s, w), dk.reshape(s, w), dv.reshape(s, w)


def _head_expand():
    hrow = lax.broadcasted_iota(jnp.int32, (LANES, DA_WIDTH), 0)
    lcol = lax.broadcasted_iota(jnp.int32, (LANES, DA_WIDTH), 1)
    return jnp.where(lcol // DA_DIM == hrow, 1.0, 0.0).astype(F32)


def _ffn_up(name, a, ga, tg, tu):
    s, d = a.shape
    nsh, _, _, ffs = ga.shape
    bm = _pick(s, (1024, 512, 256, 128))

    def body(a_ref, wg_ref, wu_ref, g_ref, u_ref, f_ref):
        av = a_ref[...]
        g = lax.dot_general(av, wg_ref[...], NN, preferred_element_type=F32)
        u = lax.dot_general(av, wu_ref[...], NN, preferred_element_type=F32)
        g_ref[...] = g.astype(BF16)
        u_ref[...] = u.astype(BF16)
        f_ref[...] = (_silu(g) * u).astype(BF16)

    wspec = lambda t: pl.BlockSpec((None, None, d, ffs), lambda i, j: (j, t, 0, 0))
    ospec = pl.BlockSpec((None, bm, ffs), lambda i, j: (j, i, 0))
    blk = _nbytes((bm, d), BF16) + 2 * _nbytes((d, ffs), BF16) + 3 * _nbytes((bm, ffs), BF16)
    return pl.pallas_call(
        body, name=name, grid=(s // bm, nsh),
        in_specs=[pl.BlockSpec((bm, d), lambda i, j: (i, 0)), wspec(tg), wspec(tu)],
        out_specs=[ospec] * 3, out_shape=[jax.ShapeDtypeStruct((nsh, s, ffs), BF16)] * 3,
        compiler_params=_params(blk, 4 * _nbytes((bm, ffs), F32)),
    )(a, ga, ga)


def _ffn_fwd(tag, h_in, ln, sh, sc, gt, ga, tg, tu, gb, td, weight):
    s, d = h_in.shape
    nsh, _, ffs, _ = gb.shape
    (a,) = _rowwise(tag + "_norm", lambda h, l, s1, s2: ((_rms_mod(h, l, s1, s2),), ()), [h_in], [ln, sh, sc],
                    [(d, BF16)])
    g, u, f = _ffn_up(tag + "_up", a, ga, tg, tu)
    bm, bn = _pick(s, (1024, 512, 256, 128)), _pick(d, (1024, 512, 256, 128))
    io = pl.BlockSpec((bm, bn), lambda i, j, kk: (i, j))
    h_out, o = _mm_core(
        tag + "_down", (s // bm, d // bn, nsh), nsh,
        [(f, pl.BlockSpec((None, bm, ffs), lambda i, j, kk: (kk, i, 0)),
          gb, pl.BlockSpec((None, None, ffs, bn), lambda i, j, kk: (kk, td, 0, j)), NN)],
        [((s, d), F32, io), ((s, d), BF16, io)], (bm, bn),
        epi=lambda acc, h, gv: (h + weight * gv * acc, acc),
        epi_ins=[(h_in, io), (gt, pl.BlockSpec((1, bn), lambda i, j, kk: (0, j)))])
    return h_out, dict(a=a, g=g, u=u, f=f, o=o)


def _resid_bwd(tag, dh_out, o, gt, weight):
    d = dh_out.shape[1]

    def fn(dh, ov, g):
        return (weight * g * dh,), (jnp.sum(weight * dh * ov.astype(F32), axis=0, keepdims=True),)

    do, d_gt = _rowwise(tag + "_resid_bwd", fn, [dh_out, o], [gt], [(d, BF16)], [(1, d)])
    return do, d_gt


def _norm_bwd(tag, h_in, da, dh_out, ln, sh, sc):
    d = h_in.shape[1]

    def fn(h, dav, dh, l, s1, s2):
        _, vjp = jax.vjp(_rms_mod, h, l, s1, s2)
        gh, gl, gs1, gs2 = vjp(dav)
        return (dh + gh,), (gl, gs1, gs2)

    return _rowwise(tag + "_norm_bwd", fn, [h_in, da, dh_out], [ln, sh, sc], [(d, F32)], [(1, d)] * 3)


def _ffn_bwd(tag, h_in, dh_out, sv, ln, sh, sc, gt, ga, tg, tu, gb, td, weight):
    s, d = h_in.shape
    nsh, _, ffs, _ = gb.shape
    bm, bn = _pick(s, (1024, 512, 256, 128)), _pick(d, (1024, 512, 256, 128))
    bk = _pick(s, (1024, 512, 256, 128))
    do, d_gt = _resid_bwd(tag, dh_out, sv["o"], gt, weight)

    def act_bwd(df, g, u):
        _, vjp = jax.vjp(_swiglu_act, g, u)
        return vjp(df)

    hid = pl.BlockSpec((None, bm, ffs), lambda i, j, kk: (j, i, 0))
    dg, du = _mm_core(
        tag + "_down_dx", (s // bm, nsh, 1), 1,
        [(do, pl.BlockSpec((bm, d), lambda i, j, kk: (i, 0)),
          gb, pl.BlockSpec((None, None, ffs, d), lambda i, j, kk: (j, td, 0, 0)), NT)],
        [((nsh, s, ffs), BF16, hid)] * 2, (bm, ffs), epi=act_bwd, epi_ins=[(sv["g"], hid), (sv["u"], hid)])
    (d_wd,) = _mm_core(
        tag + "_down_dw", (nsh, d // bn, s // bk), s // bk,
        [(sv["f"], pl.BlockSpec((None, bk, ffs), lambda i, j, kk: (i, kk, 0)),
          do, pl.BlockSpec((bk, bn), lambda i, j, kk: (kk, j)), TN)],
        [((nsh, ffs, d), BF16, pl.BlockSpec((None, ffs, bn), lambda i, j, kk: (i, 0, j)))], (ffs, bn))
    kmaj = pl.BlockSpec((None, bm, ffs), lambda i, j, kk: (kk, i, 0))
    wsp = lambda t: pl.BlockSpec((None, None, bn, ffs), functools.partial(lambda i, j, kk, t: (kk, t, j, 0), t=t))
    (da,) = _mm_core(
        tag + "_up_dx", (s // bm, d // bn, nsh), nsh, [(dg, kmaj, ga, wsp(tg), NT), (du, kmaj, ga, wsp(tu), NT)],
        [((s, d), F32, pl.BlockSpec((bm, bn), lambda i, j, kk: (i, j)))], (bm, bn))
    dws = []
    for nm, dh in (("_wg_dw", dg), ("_wu_dw", du)):
        (dw,) = _mm_core(
            tag + nm, (1, nsh, s // bk), s // bk,
            [(sv["a"], pl.BlockSpec((bk, d), lambda i, j, kk: (kk, 0)),
              dh, pl.BlockSpec((None, bk, ffs), lambda i, j, kk: (j, kk, 0)), TN)],
            [((nsh, d, ffs), BF16, pl.BlockSpec((None, d, ffs), lambda i, j, kk: (j, 0, 0)))], (d, ffs))
        dws.append(dw)
    dh_in, d_ln, d_sh, d_sc = _norm_bwd(tag, h_in, da, dh_out, ln, sh, sc)
    return dh_in, dict(wg=dws[0], wu=dws[1], wd=d_wd), dict(ln=d_ln, sh=d_sh, sc=d_sc, gt=d_gt)


def _mixer_fwd(tag, h_in, ln, sh, sc, gt, w, sp):
    d = h_in.shape[1]
    (a,) = _rowwise(tag + "_norm", lambda h, l, s1, s2: ((_rms_mod(h, l, s1, s2),), ()), [h_in], [ln, sh, sc],
                    [(d, BF16)])
    (pq,) = _matmul(tag + "_pq", a, w["wq"], outs=(BF16,))
    (pz,) = _matmul(tag + "_pz", a, w["wz"], outs=(BF16,))
    (pba,) = _matmul(tag + "_pba", a, w["wba"])
    (pda,) = _matmul(tag + "_pda", a, w["wda"], outs=(BF16,))
    (pg,) = _matmul(tag + "_pg", a, w["wg"], outs=(BF16,))
    yc, qkvn, gb = _conv_prep_fwd(tag + "_conv", pq, sp["conv8"], pba, sp["alog"], sp["dtb"])
    o_a, states, tinv = _delta_fwd(tag + "_delta", qkvn, gb)
    (o_an,) = _rowwise(tag + "_dnorm", lambda o, z, dn: ((_dn_outnorm(o, z.astype(F32), dn),), ()), [o_a, pz],
                       [sp["dn"]], [(DN_WIDTH, BF16)])
    ops, lses = [], []
    for (_, r) in DA_PATTERNS:
        o_p, lse_p = _da_fwd(f"{tag}_da{r}", pda, r)
        ops.append(o_p)
        lses.append(lse_p)

    def merge(o1, o2, o3, l1, l2, l3):
        mx = jnp.maximum(jnp.maximum(l1, l2), l3)
        e1, e2, e3 = jnp.exp(l1 - mx), jnp.exp(l2 - mx), jnp.exp(l3 - mx)
        tot = e1 + e2 + e3
        ex = _head_expand()
        up = lambda wgt: lax.dot_general(wgt / tot, ex, NN, precision=HI, preferred_element_type=F32)
        return (up(e1) * o1 + up(e2) * o2 + up(e3) * o3, mx + jnp.log(tot)), ()

    o_b, lse_tot = _rowwise(tag + "_merge", merge, ops + lses, [], [(DA_WIDTH, BF16), (LANES, F32)])
    (y_a,) = _matmul(tag + "_wa", o_an, w["w_a"], outs=(BF16,))
    (y_b,) = _matmul(tag + "_wb", o_b, w["w_b"], outs=(BF16,))

    def gate(ga, gbv, ya, yb):
        return _sigmoid(ga.astype(F32)) * ya.astype(F32) + _sigmoid(gbv.astype(F32)) * yb.astype(F32)

    (merged,) = _rowwise(tag + "_gate", lambda *v: ((gate(*v),), ()), [(pg, d, 0), (pg, d, 1), y_a, y_b], [],
                         [(d, BF16)])
    h_out, m = _matmul(tag + "_wo", merged, w["w_o"], outs=(F32, BF16), epi_rows=[h_in], epi_bcast=[gt],
                       epi=lambda acc, h, g: (h + g * acc, acc))
    sv = dict(a=a, pq=pq, pz=pz, pba=pba, pda=pda, pg=pg, yc=yc, qkvn=qkvn, gb=gb, o_a=o_a, states=states, tinv=tinv,
              o_an=o_an, o_b=o_b, lse=lse_tot, y_a=y_a, y_b=y_b, merged=merged, m=m, gate=gate)
    return h_out, sv


def _mixer_bwd(tag, h_in, dh_out, sv, ln, sh, sc, gt, w, sp):
    d = h_in.shape[1]
    dm, d_gt = _resid_bwd(tag, dh_out, sv["m"], gt, 1.0)
    (d_merged,) = _matmul(tag + "_wo_dx", dm, w["w_o"], tb=True, outs=(BF16,))
    (d_wo,) = _matmul(tag + "_wo_dw", sv["merged"], dm, ta=True, outs=(BF16,))
    gate = sv["gate"]

    def gate_bwd(dmg, ga, gbv, ya, yb):
        _, vjp = jax.vjp(gate, ga.astype(F32), gbv.astype(F32), ya.astype(F32), yb.astype(F32))
        dga, dgb, dya, dyb = vjp(dmg.astype(F32))
        return (jnp.concatenate([dga, dgb], axis=1), dya, dyb), ()

    pg = sv["pg"]
    d_pg, d_ya, d_yb = _rowwise(tag + "_gate_bwd", gate_bwd, [d_merged, (pg, d, 0), (pg, d, 1), sv["y_a"], sv["y_b"]],
                                [], [(2 * d, BF16), (d, BF16), (d, BF16)])
    (d_oan,) = _matmul(tag + "_wa_dx", d_ya, w["w_a"], tb=True)
    (d_wa,) = _matmul(tag + "_wa_dw", sv["o_an"], d_ya, ta=True, outs=(BF16,))
    (d_ob,) = _matmul(tag + "_wb_dx", d_yb, w["w_b"], tb=True, outs=(BF16,))
    (d_wb,) = _matmul(tag + "_wb_dw", sv["o_b"], d_yb, ta=True, outs=(BF16,))

    def dnorm_bwd(doan, o, z, dn):
        _, vjp = jax.vjp(_dn_outnorm, o, z.astype(F32), dn)
        go, gz, gdn = vjp(doan)
        return (go, gz), (gdn,)

    d_oa, d_pz, d_dn = _rowwise(tag + "_dnorm_bwd", dnorm_bwd, [d_oan, sv["o_a"], sv["pz"]], [sp["dn"]],
                                [(DN_WIDTH, F32), (DN_WIDTH, BF16)], [(1, DN_DIM)])
    d_qkvn, d_gb = _delta_bwd(tag + "_delta_bwd", sv["qkvn"], sv["gb"], sv["states"], sv["tinv"], d_oa)

    def prep_bwd(dq, dgbv, yc, pba, alog, dtb):
        _, vjp = jax.vjp(_dn_prep, yc.astype(F32), pba, alog, dtb)
        gyc, gpba, galog, gdtb = vjp((dq, dgbv))
        return (gyc, gpba), (galog, gdtb)

    d_yc, d_pba, d_alog, d_dtb = _rowwise(tag + "_prep_bwd", prep_bwd, [d_qkvn, d_gb, sv["yc"], sv["pba"]],
                                          [sp["alog"], sp["dtb"]], [(3 * DN_WIDTH, BF16), (LANES, BF16)],
                                          [(1, LANES), (1, LANES)], bm=128)
    d_pq, d_conv = _conv_bwd(tag + "_conv_bwd", d_yc, sv["pq"], sp["conv8"])

    def delta_fn(dob, ob):
        prod = dob.astype(F32) * ob.astype(F32)
        return (lax.dot_general(prod, _head_expand(), NT, precision=HI, preferred_element_type=F32),), ()

    (delta,) = _rowwise(tag + "_da_delta", delta_fn, [d_ob, sv["o_b"]], [], [(LANES, F32)])
    grads = [_da_bwd(f"{tag}_da{r}_bwd", sv["pda"], d_ob, sv["lse"], delta, r) for (_, r) in DA_PATTERNS]

    def sum3(*parts):
        q1, k1, v1, q2, k2, v2, q3, k3, v3 = (p.astype(F32) for p in parts)
        return (jnp.concatenate([q1 + q2 + q3, k1 + k2 + k3, v1 + v2 + v3], axis=1),), ()

    (d_pda,) = _rowwise(tag + "_da_sum", sum3, [t for g in grads for t in g], [], [(3 * DA_WIDTH, BF16)])

    a = sv["a"]
    (da,) = _matmul(tag + "_pq_dx", d_pq, w["wq"], tb=True)
    add = lambda acc, prev: (acc + prev,)
    (da,) = _matmul(tag + "_pz_dx", d_pz, w["wz"], tb=True, epi_rows=[da], epi=add)
    (da,) = _matmul(tag + "_pba_dx", d_pba, w["wba"], tb=True, epi_rows=[da], epi=add)
    (da,) = _matmul(tag + "_pda_dx", d_pda, w["wda"], tb=True, epi_rows=[da], epi=add)
    (da,) = _matmul(tag + "_pg_dx", d_pg, w["wg"], tb=True, epi_rows=[da], epi=add)
    (d_wq,) = _matmul(tag + "_pq_dw", a, d_pq, ta=True, outs=(BF16,))
    (d_wz,) = _matmul(tag + "_pz_dw", a, d_pz, ta=True, outs=(BF16,))
    (d_wba,) = _matmul(tag + "_pba_dw", a, d_pba, ta=True, outs=(BF16,))
    (d_wda,) = _matmul(tag + "_pda_dw", a, d_pda, ta=True, outs=(BF16,))
    (d_wg,) = _matmul(tag + "_pg_dw", a, d_pg, ta=True, outs=(BF16,))
    dh_in, d_ln, d_sh, d_sc = _norm_bwd(tag, h_in, da, dh_out, ln, sh, sc)
    wgrads = dict(wq=d_wq, wz=d_wz, wba=d_wba, wda=d_wda, wg=d_wg, w_a=d_wa, w_b=d_wb, w_o=d_wo)
    small = dict(ln=d_ln, sh=d_sh, sc=d_sc, gt=d_gt, dn=d_dn, alog=d_alog, dtb=d_dtb, conv=d_conv)
    return dh_in, wgrads, small


def _loss_head(h, target, fnorm):
    d = h.shape[1]

    def fn(hv, tv, fw):
        def lossf(hh, ww):
            y = hh * lax.rsqrt(jnp.mean(hh * hh, axis=-1, keepdims=True) + NORM_EPS) * ww
            return 0.5 * jnp.sum(jnp.mean(jnp.square(y - tv), axis=-1))

        val, (dh, dw) = jax.value_and_grad(lossf, argnums=(0, 1))(hv, fw)
        return (dh,), (jnp.full((1, LANES), val, F32), dw)

    return _rowwise("loss_head", fn, [h, target], [fnorm], [(d, F32)], [(1, LANES), (1, d)])


def _row(v):
    return v.reshape(1, -1)


def _pad_lanes(v, offset):
    return jnp.pad(v.reshape(1, -1), ((0, 0), (offset, LANES - offset - v.shape[0])))


_UP_SLOTS = dict(ffn1_wg=0, ffn1_wu=1, ffn2_wg=2, ffn2_wu=3)
_DOWN_SLOTS = dict(ffn1_wd=0, ffn2_wd=1)


def _local_step(x2, target, mod, layer_weights, small, on_layer_grads):
    depth = mod.shape[0]
    d = x2.shape[1]
    h = x2
    saved = []
    mods = []
    up = lambda l, nm: _UP_SLOTS[nm]
    down = lambda l, nm: _DOWN_SLOTS[nm]
    for l in range(depth):
        m9 = [_row(mod[l, i * d:(i + 1) * d]) for i in range(N_ADA)]
        sp = dict(conv8=jnp.pad(small["conv_w"][l], ((0, 8 - DN_CONV), (0, 0))),
                  alog=_pad_lanes(small["a_log"][l], DN_HEADS), dtb=_pad_lanes(small["dt_bias"][l], DN_HEADS),
                  dn=_row(small["dn_norm"][l]))
        ga, gb, w = layer_weights(l, h)
        h0 = h
        h1, sv1 = _ffn_fwd(f"l{l}_ffn1", h0, _row(small["ln_ffn1"][l]), m9[0], m9[1], m9[2], ga, up(l, "ffn1_wg"),
                           up(l, "ffn1_wu"), gb, down(l, "ffn1_wd"), 0.5)
        h2, sv2 = _mixer_fwd(f"l{l}_mix", h1, _row(small["ln_mix"][l]), m9[3], m9[4], m9[5], w, sp)
        h3, sv3 = _ffn_fwd(f"l{l}_ffn2", h2, _row(small["ln_ffn2"][l]), m9[6], m9[7], m9[8], ga, up(l, "ffn2_wg"),
                           up(l, "ffn2_wu"), gb, down(l, "ffn2_wd"), 0.5)
        saved.append((h0, h1, h2, sv1, sv2, sv3, sp, ga, gb, w))
        mods.append(m9)
        h = h3
    dh, loss_part, d_fnorm = _loss_head(h, target, _row(small["final_norm"]))
    sgrads, dmods = [], []
    token = None
    for l in reversed(range(depth)):
        h0, h1, h2, sv1, sv2, sv3, sp, ga, gb, w = saved[l]
        m9 = mods[l] if token is None else [r + token for r in mods[l]]
        dh, g3, s3 = _ffn_bwd(f"l{l}_ffn2", h2, dh, sv3, _row(small["ln_ffn2"][l]), m9[6], m9[7], m9[8], ga,
                              up(l, "ffn2_wg"), up(l, "ffn2_wu"), gb, down(l, "ffn2_wd"), 0.5)
        dh, g2, s2 = _mixer_bwd(f"l{l}_mix", h1, dh, sv2, _row(small["ln_mix"][l]), m9[3], m9[4], m9[5], w, sp)
        dh, g1, s1 = _ffn_bwd(f"l{l}_ffn1", h0, dh, sv1, _row(small["ln_ffn1"][l]), m9[0], m9[1], m9[2], ga,
                              up(l, "ffn1_wg"), up(l, "ffn1_wu"), gb, down(l, "ffn1_wd"), 0.5)
        token = on_layer_grads(l, dict(ffn1_wg=g1["wg"], ffn1_wu=g1["wu"], ffn1_wd=g1["wd"], ffn2_wg=g3["wg"],
                                       ffn2_wu=g3["wu"], ffn2_wd=g3["wd"], **g2))
        dmods.append(jnp.concatenate([s1["sh"], s1["sc"], s1["gt"], s2["sh"], s2["sc"], s2["gt"],
                                      s3["sh"], s3["sc"], s3["gt"]], axis=1))
        sgrads.append(dict(ln_ffn1=s1["ln"][0], ln_mix=s2["ln"][0], ln_ffn2=s3["ln"][0],
                           a_log=s2["alog"][0, DN_HEADS:2 * DN_HEADS], dt_bias=s2["dtb"][0, DN_HEADS:2 * DN_HEADS],
                           dn_norm=s2["dn"][0], conv_w=s2["conv"][:DN_CONV]))
    sgrads.reverse()
    dmods.reverse()
    return loss_part[0, 0], dh, jnp.concatenate(dmods, axis=0), sgrads, d_fnorm[0]


def _flip(v, bit):
    return 1 - v if bit else v


def _allgather8(name, x):
    r, c = x.shape

    def body(x_ref, out_ref, send_sems, recv_sems, local_sem):
        mx, my, mc = lax.axis_index("x"), lax.axis_index("y"), lax.axis_index("c")
        me = 4 * mx + 2 * my + mc
        mine = pltpu.make_async_copy(x_ref, out_ref.at[me], local_sem)
        mine.start()
        sends = []
        for k in range(1, 8):
            peer = (_flip(mx, k & 4), _flip(my, k & 2), _flip(mc, k & 1))
            cp = pltpu.make_async_remote_copy(src_ref=x_ref, dst_ref=out_ref.at[me], send_sem=send_sems.at[k - 1],
                                              recv_sem=recv_sems.at[k - 1], device_id=peer, device_id_type=MESH)
            cp.start()
            sends.append(cp)
        for k in range(1, 8):
            peer = (_flip(mx, k & 4), _flip(my, k & 2), _flip(mc, k & 1))
            src = 4 * peer[0] + 2 * peer[1] + peer[2]
            pltpu.make_async_remote_copy(src_ref=x_ref, dst_ref=out_ref.at[src], send_sem=send_sems.at[k - 1],
                                         recv_sem=recv_sems.at[k - 1], device_id=peer, device_id_type=MESH).wait_recv()
        for cp in sends:
            cp.wait_send()
        mine.wait()

    return pl.pallas_call(
        body, name=name, out_shape=jax.ShapeDtypeStruct((8, r, c), x.dtype),
        in_specs=[pl.BlockSpec(memory_space=pltpu.VMEM)], out_specs=pl.BlockSpec(memory_space=pltpu.VMEM),
        scratch_shapes=[pltpu.SemaphoreType.DMA((7,)), pltpu.SemaphoreType.DMA((7,)), pltpu.SemaphoreType.DMA],
        compiler_params=_params(9 * _nbytes((r, c), x.dtype)),
    )(x)


def _chip_peers(mx, my):
    chips = [(1 - mx, my), (mx, 1 - my), (1 - mx, 1 - my)]
    return chips, [2 * cx + cy for (cx, cy) in chips]


_ANY = pl.BlockSpec(memory_space=pl.ANY)


def _row_half(mc, r):
    return pl.ds(pl.multiple_of(mc * (r // 2), 16), r // 2)


def _gather_groups(name, shards):
    ng = len(shards)

    def body(*refs):
        xs, outs = refs[:ng], refs[ng:2 * ng]
        send_sems, recv_sems = refs[2 * ng:]
        mx, my, mc = lax.axis_index("x"), lax.axis_index("y"), lax.axis_index("c")
        j = 2 * mx + my
        chips, idxs = _chip_peers(mx, my)
        sib = (mx, my, 1 - mc)

        def copy(k, src, dst, to):
            return pltpu.make_async_remote_copy(src_ref=src, dst_ref=dst, send_sem=send_sems.at[k],
                                                recv_sem=recv_sems.at[k], device_id=to, device_id_type=MESH)

        first, passed = [], []
        for g in range(ng):
            mine = _row_half(mc, shards[g].shape[1])
            for t, chip in enumerate(chips):
                cp = copy(6 * g + t, xs[g].at[:, mine], outs[g].at[j, :, mine], (*chip, mc))
                cp.start()
                first.append(cp)
        for g in range(ng):
            mine = _row_half(mc, shards[g].shape[1])
            for t, chip in enumerate(chips):
                landed = outs[g].at[idxs[t], :, mine]
                copy(6 * g + t, landed, landed, (*chip, mc)).wait_recv()
                fwd = copy(6 * g + 3 + t, landed, landed, sib)
                fwd.start()
                passed.append(fwd)
        for g in range(ng):
            theirs_half = _row_half(1 - mc, shards[g].shape[1])
            for t in range(3):
                theirs = outs[g].at[idxs[t], :, theirs_half]
                copy(6 * g + 3 + t, theirs, theirs, sib).wait_recv()
        for cp in first + passed:
            cp.wait_send()

    outs = pl.pallas_call(
        body, name=name, out_shape=[jax.ShapeDtypeStruct((4,) + x.shape, x.dtype) for x in shards],
        in_specs=[_ANY] * ng, out_specs=[_ANY] * ng,
        scratch_shapes=[pltpu.SemaphoreType.DMA((6 * ng,)), pltpu.SemaphoreType.DMA((6 * ng,))],
    )(*shards)
    return _place_own_slab(outs, shards)


def _place_own_slab(outs, shards):
    chip = 2 * lax.axis_index("x") + lax.axis_index("y")
    return [lax.dynamic_update_slice(o, x[None], (chip,) + (0,) * x.ndim) for o, x in zip(outs, shards)]


_HBM = pl.BlockSpec(memory_space=pltpu.HBM)
_SEM = pl.BlockSpec(memory_space=pltpu.SEMAPHORE)
_DATAFLOW = pltpu.SideEffectType.DATAFLOW_SIDE_EFFECTING


def _ici_gather_copies(src_refs, land_refs, send_sems, recv_sems, scatter=False):
    mx, my, mc = lax.axis_index("x"), lax.axis_index("y"), lax.axis_index("c")
    j = 2 * mx + my
    chips, idxs = _chip_peers(mx, my)
    sends, recvs = [], []
    for g, src in enumerate(src_refs):
        for t, chip in enumerate(chips):
            common = dict(send_sem=send_sems.at[3 * g + t], recv_sem=recv_sems.at[3 * g + t], device_id=(*chip, mc),
                          device_id_type=MESH)
            if scatter:
                out, to, frm = src.at[idxs[t]], land_refs[g].at[j], land_refs[g].at[idxs[t]]
            else:
                mine = _row_half(mc, src.shape[1])
                out, to, frm = src.at[:, mine], land_refs[g].at[j, :, mine], land_refs[g].at[idxs[t], :, mine]
            sends.append(pltpu.make_async_remote_copy(src_ref=out, dst_ref=to, **common))
            recvs.append(pltpu.make_async_remote_copy(src_ref=out, dst_ref=frm, **common))
    return sends, recvs


def _gather_start(name, shards, scatter=False):
    ng = len(shards)

    def body(*refs):
        srcs, lands = refs[:ng], refs[ng:2 * ng]
        send_sems, recv_sems = refs[2 * ng], refs[2 * ng + 1]
        token = refs[-1]
        sends, _ = _ici_gather_copies(srcs, lands, send_sems, recv_sems, scatter)
        for cp in sends:
            cp.start()
        token[...] = jnp.zeros(token.shape, token.dtype)

    land_shape = lambda x: x.shape if scatter else (4,) + x.shape
    srcs = [pltpu.with_memory_space_constraint(x, pltpu.HBM) for x in shards]
    lands = [pltpu.with_memory_space_constraint(lax.empty(land_shape(x), x.dtype), pltpu.HBM) for x in shards]
    res = pl.pallas_call(
        body, name=name,
        out_shape=(pltpu.SemaphoreType.DMA((3 * ng,)), pltpu.SemaphoreType.DMA((3 * ng,)),
                   *[pltpu.HBM(x.shape, x.dtype) for x in srcs], *[pltpu.HBM(x.shape, x.dtype) for x in lands],
                   jax.ShapeDtypeStruct((8, LANES), F32)),
        in_specs=[_HBM] * (2 * ng),
        out_specs=(_SEM, _SEM, *[_HBM] * (2 * ng), pl.BlockSpec(memory_space=pltpu.VMEM)),
        input_output_aliases={i: 2 + i for i in range(2 * ng)},
        compiler_params=pltpu.CompilerParams(has_side_effects=_DATAFLOW),
    )(*srcs, *lands)
    return dict(send_sems=res[0], recv_sems=res[1], srcs=list(res[2:2 + ng]), lands=list(res[2 + ng:2 + 2 * ng]),
                token=res[-1])


def _gather_wait(name, started, after, scatter=False):
    ng = len(started["srcs"])

    def body(*refs):
        srcs, lands = refs[:ng], refs[ng:2 * ng]
        send_sems, recv_sems = refs[2 * ng], refs[2 * ng + 1]
        sends, recvs = _ici_gather_copies(srcs, lands, send_sems, recv_sems, scatter)
        for cp in sends:
            cp.wait_send()
        for cp in recvs:
            cp.wait_recv()

    res = pl.pallas_call(
        body, name=name,
        out_shape=[pltpu.HBM(x.shape, x.dtype) for x in started["srcs"] + started["lands"]],
        in_specs=[_HBM] * (2 * ng) + [_SEM, _SEM, _ANY], out_specs=[_HBM] * (2 * ng),
        input_output_aliases={i: i for i in range(2 * ng)},
        compiler_params=pltpu.CompilerParams(has_side_effects=_DATAFLOW),
    )(*started["srcs"], *started["lands"], started["send_sems"], started["recv_sems"], after)
    return list(res[:ng]), list(res[ng:])


def _pair_forward_groups(name, lands, shards):
    ng = len(lands)

    def body(*refs):
        ins, outs = refs[:ng], refs[ng:2 * ng]
        send_sems, recv_sems = refs[2 * ng:]
        mx, my, mc = lax.axis_index("x"), lax.axis_index("y"), lax.axis_index("c")
        _, idxs = _chip_peers(mx, my)
        sib = (mx, my, 1 - mc)
        cps = []
        for g in range(ng):
            mine = _row_half(mc, lands[g].shape[2])
            for t in range(3):
                cp = pltpu.make_async_remote_copy(src_ref=ins[g].at[idxs[t], :, mine], dst_ref=outs[g].at[idxs[t], :, mine],
                                                  send_sem=send_sems.at[3 * g + t], recv_sem=recv_sems.at[3 * g + t],
                                                  device_id=sib, device_id_type=MESH)
                cp.start()
                cps.append(cp)
        for g in range(ng):
            theirs = _row_half(1 - mc, lands[g].shape[2])
            for t in range(3):
                pltpu.make_async_remote_copy(src_ref=ins[g].at[idxs[t], :, theirs], dst_ref=outs[g].at[idxs[t], :, theirs],
                                             send_sem=send_sems.at[3 * g + t], recv_sem=recv_sems.at[3 * g + t],
                                             device_id=sib, device_id_type=MESH).wait_recv()
        for cp in cps:
            cp.wait_send()

    outs = pl.pallas_call(
        body, name=name, out_shape=[jax.ShapeDtypeStruct(x.shape, x.dtype) for x in lands],
        in_specs=[_ANY] * ng, out_specs=[_ANY] * ng, input_output_aliases={i: i for i in range(ng)},
        scratch_shapes=[pltpu.SemaphoreType.DMA((3 * ng,)), pltpu.SemaphoreType.DMA((3 * ng,))],
    )(*lands)
    return _place_own_slab(outs, shards)


def _pair_swap_groups(name, gs):
    ng = len(gs)

    def body(*refs):
        xs, outs = refs[:ng], refs[ng:2 * ng]
        send_sems, recv_sems = refs[2 * ng:]
        mx, my, mc = lax.axis_index("x"), lax.axis_index("y"), lax.axis_index("c")
        cps = []
        for g in range(ng):
            cp = pltpu.make_async_remote_copy(src_ref=xs[g].at[:, :, _row_half(1 - mc, gs[g].shape[2])], dst_ref=outs[g],
                                              send_sem=send_sems.at[g], recv_sem=recv_sems.at[g],
                                              device_id=(mx, my, 1 - mc), device_id_type=MESH)
            cp.start()
            cps.append(cp)
        for cp in cps:
            cp.wait()

    return pl.pallas_call(
        body, name=name,
        out_shape=[jax.ShapeDtypeStruct(x.shape[:2] + (x.shape[2] // 2, x.shape[3]), x.dtype) for x in gs],
        in_specs=[_ANY] * ng, out_specs=[_ANY] * ng,
        scratch_shapes=[pltpu.SemaphoreType.DMA((ng,)), pltpu.SemaphoreType.DMA((ng,))],
    )(*gs)


def _chip_scatter_groups(name, ps):
    ng = len(ps)

    def body(*refs):
        xs, outs = refs[:ng], refs[ng:2 * ng]
        send_sems, recv_sems = refs[2 * ng:]
        mx, my, mc = lax.axis_index("x"), lax.axis_index("y"), lax.axis_index("c")
        j = 2 * mx + my
        chips, idxs = _chip_peers(mx, my)
        sends = []
        for g in range(ng):
            for t, chip in enumerate(chips):
                cp = pltpu.make_async_remote_copy(src_ref=xs[g].at[idxs[t]], dst_ref=outs[g].at[j],
                                                  send_sem=send_sems.at[3 * g + t], recv_sem=recv_sems.at[3 * g + t],
                                                  device_id=(*chip, mc), device_id_type=MESH)
                cp.start()
                sends.append(cp)
        for g in range(ng):
            for t, chip in enumerate(chips):
                pltpu.make_async_remote_copy(src_ref=xs[g].at[idxs[t]], dst_ref=outs[g].at[idxs[t]],
                                             send_sem=send_sems.at[3 * g + t], recv_sem=recv_sems.at[3 * g + t],
                                             device_id=(*chip, mc), device_id_type=MESH).wait_recv()
        for cp in sends:
            cp.wait_send()

    outs = pl.pallas_call(
        body, name=name, out_shape=[jax.ShapeDtypeStruct(x.shape, x.dtype) for x in ps],
        in_specs=[_ANY] * ng, out_specs=[_ANY] * ng,
        scratch_shapes=[pltpu.SemaphoreType.DMA((3 * ng,)), pltpu.SemaphoreType.DMA((3 * ng,))],
    )(*ps)
    return _place_own_part(outs, ps)


def _place_own_part(outs, ps):
    chip = 2 * lax.axis_index("x") + lax.axis_index("y")
    return [lax.dynamic_update_slice(o, lax.dynamic_index_in_dim(x, chip, 0, keepdims=True), (chip,) + (0,) * (x.ndim - 1))
            for o, x in zip(outs, ps)]


def _pair_merge_groups(name, fs):
    ng = len(fs)

    def body(*refs):
        xs, outs = refs[:ng], refs[ng:2 * ng]
        send_sems, recv_sems = refs[2 * ng:]
        mx, my, mc = lax.axis_index("x"), lax.axis_index("y"), lax.axis_index("c")
        cps = []
        for g in range(ng):
            mine = _row_half(mc, 2 * fs[g].shape[1])
            cp = pltpu.make_async_remote_copy(src_ref=xs[g], dst_ref=outs[g].at[:, mine], send_sem=send_sems.at[g],
                                              recv_sem=recv_sems.at[g], device_id=(mx, my, 1 - mc), device_id_type=MESH)
            cp.start()
            cps.append(cp)
        for g in range(ng):
            theirs = outs[g].at[:, _row_half(1 - mc, 2 * fs[g].shape[1])]
            pltpu.make_async_remote_copy(src_ref=xs[g], dst_ref=theirs, send_sem=send_sems.at[g],
                                         recv_sem=recv_sems.at[g], device_id=(mx, my, 1 - mc),
                                         device_id_type=MESH).wait_recv()
        for cp in cps:
            cp.wait_send()

    outs = pl.pallas_call(
        body, name=name,
        out_shape=[jax.ShapeDtypeStruct((x.shape[0], 2 * x.shape[1], x.shape[2]), x.dtype) for x in fs],
        in_specs=[_ANY] * ng, out_specs=[_ANY] * ng,
        scratch_shapes=[pltpu.SemaphoreType.DMA((ng,)), pltpu.SemaphoreType.DMA((ng,))],
    )(*fs)
    mc = lax.axis_index("c")
    return [lax.dynamic_update_slice(o, x, (0, mc * x.shape[1], 0)) for o, x in zip(outs, fs)]


def _block_rows(r, w, itemsize=4, budget=4 << 20):
    for c in (r, 2048, 1024, 512, 256, 128, 64, 32, 16):
        if c <= r and r % c == 0 and c * w * itemsize <= budget:
            return c
    return r


def _pair_sum(name, g, got, cidx):
    ns, t, r, w = g.shape
    rh = r // 2
    bm = _block_rows(rh, w)
    nb = rh // bm

    def body(c_ref, a_ref, b_ref, o_ref):
        o_ref[...] = (a_ref[...].astype(F32) + b_ref[...].astype(F32)).astype(o_ref.dtype)

    blk = (None, None, bm, w)
    return pl.pallas_call(
        body, name=name,
        grid_spec=pltpu.PrefetchScalarGridSpec(
            num_scalar_prefetch=1, grid=(ns, t, nb),
            in_specs=[pl.BlockSpec(blk, lambda s, tt, i, c: (s, tt, c[0] * nb + i, 0)),
                      pl.BlockSpec(blk, lambda s, tt, i, c: (s, tt, i, 0))],
            out_specs=pl.BlockSpec(blk, lambda s, tt, i, c: (s, tt, i, 0))),
        out_shape=jax.ShapeDtypeStruct((ns, t, rh, w), BF16),
        compiler_params=_params(3 * _nbytes((bm, w), F32)),
    )(cidx, g, got)


def _chip_sum(name, p):
    ns, th, r, w = p.shape
    bm = _block_rows(r, w, budget=2 << 20)

    def body(p_ref, o_ref):
        acc = p_ref[0].astype(F32)
        for s in range(1, ns):
            acc = acc + p_ref[s].astype(F32)
        o_ref[...] = acc

    return pl.pallas_call(
        body, name=name, grid=(th, r // bm),
        in_specs=[pl.BlockSpec((ns, None, bm, w), lambda tt, i: (0, tt, i, 0))],
        out_specs=pl.BlockSpec((None, bm, w), lambda tt, i: (tt, i, 0)),
        out_shape=jax.ShapeDtypeStruct((th, r, w), F32),
        compiler_params=_params(ns * _nbytes((bm, w), BF16) + 2 * _nbytes((bm, w), F32)),
    )(p)


def _sum_leading(name, x):
    n = x.shape[0]

    def body(p_ref, o_ref):
        acc = p_ref[0]
        for s in range(1, n):
            acc = acc + p_ref[s]
        o_ref[...] = acc

    return pl.pallas_call(body, name=name, out_shape=jax.ShapeDtypeStruct(x.shape[1:], F32),
                          compiler_params=_params(2 * _nbytes(x.shape, F32)))(x)


def _reduce_scatter_begin(tag, gs, overlap):
    cidx = lax.axis_index("c").astype(jnp.int32).reshape(1)
    got = _pair_swap_groups(tag + "_pair_swap", gs)
    pair = [_pair_sum(f"{tag}_pair_sum{i}", g, r_, cidx) for i, (g, r_) in enumerate(zip(gs, got))]
    if overlap:
        return _gather_start(tag + "_scatter_start", pair, scatter=True)
    return _chip_scatter_groups(tag + "_chip_scatter", pair)


def _reduce_scatter_end(tag, state, overlap, after):
    if overlap:
        srcs, lands = _gather_wait(tag + "_scatter_wait", state, after, scatter=True)
        state = _place_own_part(lands, srcs)
    fin = [_chip_sum(f"{tag}_chip_sum{i}", p) for i, p in enumerate(state)]
    return _pair_merge_groups(tag + "_pair_merge", fin)


_GROUPS = ((("ffn1_wg", "ffn1_wu", "ffn2_wg", "ffn2_wu"), 1), (("ffn1_wd", "ffn2_wd"), 0), (("w_a",), 0),
           (("w_o",), 0), (("w_in",), 1), (("w_b",), 1))


def _shard_major(g, ax):
    k, n = g.shape
    if ax == 0:
        return g.reshape(4, k // 4, n)
    return g.reshape(k, 4, n // 4).transpose(1, 0, 2)


def _in_cols(d):
    o1 = 3 * DN_WIDTH
    o2 = o1 + DN_WIDTH
    o3 = o2 + 2 * DN_HEADS
    o4 = o3 + 3 * DA_WIDTH
    return dict(wq=(0, o1), wz=(o1, o2), wba=(o2, o3), wda=(o3, o4), wg=(o4, o4 + 2 * d))


def _mixer_weights(w_in, w_a, w_b, w_o, d):
    w = {k: w_in[:, a:b] for k, (a, b) in _in_cols(d).items()}
    w["wba"] = jnp.pad(w["wba"], ((0, 0), (0, LANES - 2 * DN_HEADS)))
    w["w_a"], w["w_b"], w["w_o"] = w_a, w_b, w_o
    return w


def _w_in_grad(wg):
    return jnp.concatenate([wg["wq"], wg["wz"], wg["wba"][:, :2 * DN_HEADS], wg["wda"], wg["wg"]], axis=1)


def _adam_math(wv, gv, mv, vv):
    mn = ADAM_B1 * mv + (1.0 - ADAM_B1) * gv
    vn = ADAM_B2 * vv + (1.0 - ADAM_B2) * jnp.square(gv)
    m_hat = mn / (1.0 - ADAM_B1 ** ADAM_STEP)
    v_hat = vn / (1.0 - ADAM_B2 ** ADAM_STEP)
    delta = -ADAM_LR * (m_hat / (jnp.sqrt(v_hat) + ADAM_EPS) + ADAM_WD * wv)
    return delta, mn, vn


def _adamw(name, w, g, m, v):
    shape = w.shape
    cols = shape[-1]
    w2, g2, m2, v2 = (t.reshape(-1, cols) for t in (w, g, m, v))
    rows = w2.shape[0]
    bm = _pick(rows, (256, 128, 64, 32, 16, 8)) if rows >= 8 else rows
    delta, mn, vn = _rowwise(name, lambda *t: (_adam_math(*t), ()), [w2, g2, m2, v2], [], [(cols, F32)] * 3, bm=bm)
    return delta.reshape(shape), mn.reshape(shape), vn.reshape(shape)


def _adamw_leading(name, w, g, m, v):
    n = w.shape[0]
    padded_row = -(-w.shape[1] // 8) * 8 * w.shape[2] * 4
    bm = max(c for c in range(1, n + 1) if n % c == 0 and (c * padded_row <= (1 << 20) or c == 1))

    def body(w_ref, g_ref, m_ref, v_ref, d_ref, mo_ref, vo_ref):
        d_ref[...], mo_ref[...], vo_ref[...] = _adam_math(w_ref[...], g_ref[...], m_ref[...], v_ref[...])

    spec = pl.BlockSpec((bm,) + w.shape[1:], lambda i: (i, 0, 0))
    return pl.pallas_call(
        body, name=name, grid=(n // bm,), in_specs=[spec] * 4, out_specs=[spec] * 3,
        out_shape=[jax.ShapeDtypeStruct(w.shape, F32)] * 3, compiler_params=_params(7 * bm * padded_row),
    )(w, g, m, v)


def _adamw_stacked(name, w, m, v, gstacks, slot):
    depth, r, cdim = w.shape
    bm = _block_rows(r, cdim, budget=1 << 20)

    def body(w_ref, m_ref, v_ref, *rest):
        g_refs, (go_ref, d_ref, mo_ref, vo_ref) = rest[:depth], rest[depth:]
        layer = pl.program_id(0)
        gv = g_refs[0][...]
        for l in range(1, depth):
            gv = jnp.where(layer == l, g_refs[l][...], gv)
        go_ref[...] = gv
        d_ref[...], mo_ref[...], vo_ref[...] = _adam_math(w_ref[...], gv, m_ref[...], v_ref[...])

    nat = pl.BlockSpec((None, bm, cdim), lambda l, i: (l, i, 0))
    return pl.pallas_call(
        body, name=name, grid=(depth, r // bm),
        in_specs=[nat, nat, nat] + [pl.BlockSpec((None, bm, cdim), lambda l, i: (slot, i, 0))] * depth,
        out_specs=[nat] * 4, out_shape=[jax.ShapeDtypeStruct(w.shape, F32)] * 4,
        compiler_params=_params((7 + depth) * _nbytes((bm, cdim), F32)),
    )(w, m, v, *gstacks)


def kernel(x, c, ada_w, ada_b, ln_ffn1, ln_mix, ln_ffn2, ffn1_wg, ffn1_wu, ffn1_wd, w_in, conv_w, a_log, dt_bias, dn_norm, w_a, w_b, w_o, ffn2_wg, ffn2_wu, ffn2_wd, final_norm, loss_target, m_ada_w, m_ada_b, m_ln_ffn1, m_ln_mix, m_ln_ffn2, m_ffn1_wg, m_ffn1_wu, m_ffn1_wd, m_w_in, m_conv_w, m_a_log, m_dt_bias, m_dn_norm, m_w_a, m_w_b, m_w_o, m_ffn2_wg, m_ffn2_wu, m_ffn2_wd, m_final_norm, v_ada_w, v_ada_b, v_ln_ffn1, v_ln_mix, v_ln_ffn2, v_ffn1_wg, v_ffn1_wu, v_ffn1_wd, v_w_in, v_conv_w, v_a_log, v_dt_bias, v_dn_norm, v_w_a, v_w_b, v_w_o, v_ffn2_wg, v_ffn2_wu, v_ffn2_wd, v_final_norm):
    names = ["ada_w", "ada_b", "ln_ffn1", "ln_mix", "ln_ffn2", "ffn1_wg", "ffn1_wu", "ffn1_wd", "w_in", "conv_w",
             "a_log", "dt_bias", "dn_norm", "w_a", "w_b", "w_o", "ffn2_wg", "ffn2_wu", "ffn2_wd", "final_norm"]
    wts = dict(zip(names, (ada_w, ada_b, ln_ffn1, ln_mix, ln_ffn2, ffn1_wg, ffn1_wu, ffn1_wd, w_in, conv_w, a_log,
                           dt_bias, dn_norm, w_a, w_b, w_o, ffn2_wg, ffn2_wu, ffn2_wd, final_norm)))
    mom = dict(zip(names, (m_ada_w, m_ada_b, m_ln_ffn1, m_ln_mix, m_ln_ffn2, m_ffn1_wg, m_ffn1_wu, m_ffn1_wd, m_w_in,
                           m_conv_w, m_a_log, m_dt_bias, m_dn_norm, m_w_a, m_w_b, m_w_o, m_ffn2_wg, m_ffn2_wu,
                           m_ffn2_wd, m_final_norm)))
    var = dict(zip(names, (v_ada_w, v_ada_b, v_ln_ffn1, v_ln_mix, v_ln_ffn2, v_ffn1_wg, v_ffn1_wu, v_ffn1_wd, v_w_in,
                           v_conv_w, v_a_log, v_dt_bias, v_dn_norm, v_w_a, v_w_b, v_w_o, v_ffn2_wg, v_ffn2_wu,
                           v_ffn2_wd, v_final_norm)))
    _, s, d = x.shape
    depth = ada_w.shape[0]
    mx, my, mc = lax.axis_index("x"), lax.axis_index("y"), lax.axis_index("c")
    chip = 2 * mx + my
    me = 2 * chip + mc
    nshard = ada_w.shape[2]

    cact = _rowwise("c_silu", lambda cv: ((_silu(cv),), ()), [jnp.pad(c, ((0, 7), (0, 0)))], [], [(d, F32)], bm=8)[0]
    c_all = _allgather8("ag_c", cact)[:, 0, :]
    conv_all = _allgather8("ag_conv", jnp.pad(conv_w.reshape(depth * DN_CONV, -1), ((0, 8 - depth * DN_CONV), (0, 0))))
    conv_full = jnp.concatenate([conv_all[2 * j, :depth * DN_CONV] for j in range(4)], axis=1)
    conv_full = conv_full.reshape(depth, DN_CONV, 3 * DN_WIDTH)
    layer_shards = [[jnp.stack([wts[nm][l].astype(BF16) for nm in nms], axis=0) for nms, _ in _GROUPS]
                    for l in range(depth)]
    gathered0 = _gather_groups("ag_weights0", layer_shards[0])
    rows_of = lambda st: st[:, 0].reshape(-1, st.shape[-1])
    cols_of = lambda st: jnp.concatenate([st[j, 0] for j in range(4)], axis=1)

    def layer_weights(l, after):
        if l == 0:
            got = gathered0
        else:
            srcs, lands = _gather_wait(f"ag_weights{l}_wait", started[l], after)
            got = _pair_forward_groups(f"ag_weights{l}_pair", lands, srcs)
        ga, gb, g_wa, g_wo, g_win, g_wb = got
        return ga, gb, _mixer_weights(cols_of(g_win), rows_of(g_wa), cols_of(g_wb), rows_of(g_wo), d)

    c16 = jnp.pad(c_all, ((0, 8), (0, 0))).astype(BF16)
    parts = []
    for l in range(depth):
        bias = lax.dynamic_slice(ada_b[l], (chip * nshard,), (nshard,)).reshape(1, nshard)
        (mp,) = _matmul(f"ada_fwd{l}", c16, ada_w[l].astype(BF16), epi_bcast=[bias], epi=lambda acc, b: (acc + b,))
        parts.append(mp)
    mod_all = _allgather8("ag_mod", jnp.concatenate(parts, axis=0))
    mod_rows = jnp.concatenate([mod_all[2 * j] for j in range(4)], axis=1)
    mod = jnp.stack([lax.dynamic_index_in_dim(mod_rows, l * 16 + me, axis=0, keepdims=False) for l in range(depth)])

    gathered0, later, mod, conv_full = lax.optimization_barrier((gathered0, layer_shards[1:], mod, conv_full))
    started = {l: _gather_start(f"ag_weights{l}_start", later[l - 1]) for l in range(1, depth)}
    for st in started.values():
        mod = mod + st["token"][0, 0]
    small = dict(conv_w=conv_full, a_log=a_log, dt_bias=dt_bias, dn_norm=dn_norm, ln_ffn1=ln_ffn1, ln_mix=ln_mix,
                 ln_ffn2=ln_ffn2, final_norm=final_norm)
    ffn_names = _GROUPS[0][0] + _GROUPS[1][0]
    rs_state, first_layer = {}, {}

    def on_layer_grads(l, wg):
        wg["w_in"] = _w_in_grad(wg)
        gs = [jnp.stack([wg[nm] if nm in ffn_names else _shard_major(wg[nm], ax) for nm in nms], axis=1)
              for nms, ax in _GROUPS]
        if l == 0:
            first_layer["gs"] = gs
            return None
        rs_state[l] = _reduce_scatter_begin(f"rs{l}", gs, overlap=True)
        return rs_state[l]["token"][0, 0]

    loss_part, dx, dmod, sgrads, d_fnorm = _local_step(x[0], loss_target[0], mod, layer_weights, small,
                                                       on_layer_grads)

    dmod_all = _allgather8("ag_dmod", jnp.pad(dmod, ((0, 8 - depth), (0, 0))))
    smalls = [loss_part.reshape(1), d_fnorm]
    for l in range(depth):
        sg = sgrads[l]
        smalls += [sg["ln_ffn1"], sg["ln_mix"], sg["ln_ffn2"], sg["a_log"], sg["dt_bias"], sg["dn_norm"],
                   sg["conv_w"].reshape(-1)]
    sizes = [t.shape[0] for t in smalls]
    tile = 8 * LANES
    flat = jnp.concatenate([jnp.pad(t, (0, (-t.shape[0]) % tile)).reshape(-1, LANES) for t in smalls], axis=0)
    small_all = _allgather8("ag_small", flat)
    dmod_all, small_all, gs0 = lax.optimization_barrier((dmod_all, small_all, first_layer["gs"]))
    rs_state[0] = _reduce_scatter_begin("rs0", gs0, overlap=True)
    started0 = rs_state[0]["token"][0, 0]
    dmod_all = dmod_all + started0
    small_all = small_all + started0

    g_ada_w, g_ada_b = [], []
    for l in range(depth):
        dm_l = dmod_all[:, l, :]
        (gb_l,) = _rowwise(f"ada_b_grad{l}", lambda v: ((), (jnp.sum(v, axis=0, keepdims=True),)), [dm_l], [], [],
                           [(1, N_ADA * d)], bm=8)
        g_ada_b.append(gb_l[0])
        dm_sh = lax.dynamic_slice(dm_l, (0, chip * nshard), (8, nshard))
        (gw_l,) = _matmul(f"ada_w_grad{l}", c16, jnp.pad(dm_sh, ((0, 8), (0, 0))).astype(BF16), ta=True)
        g_ada_w.append(gw_l)
    grads = dict(ada_w=jnp.stack(g_ada_w), ada_b=jnp.stack(g_ada_b))

    tot = _sum_leading("small_sum", small_all)
    offs, acc = [], 0
    for n_ in sizes:
        offs.append(acc)
        acc += -(-n_ // tile) * 8
    take = lambda i: tot[offs[i]:offs[i] + -(-sizes[i] // tile) * 8].reshape(-1)[:sizes[i]]
    loss = take(0)[0]
    grads["final_norm"] = take(1)
    per = 7
    for key_i, key in enumerate(["ln_ffn1", "ln_mix", "ln_ffn2", "a_log", "dt_bias", "dn_norm"]):
        grads[key] = jnp.stack([take(2 + per * l + key_i) for l in range(depth)])
    conv_g = jnp.stack([take(2 + per * l + 6).reshape(DN_CONV, 3 * DN_WIDTH) for l in range(depth)])
    csh = conv_w.shape[2]
    grads["conv_w"] = lax.dynamic_slice(conv_g, (0, 0, chip * csh), (depth, DN_CONV, csh))

    deltas, new_m, new_v = {}, {}, {}
    big = {nm for nms, _ in _GROUPS for nm in nms}
    for name in names:
        if name in big:
            continue
        wv, gv, mv, vv = wts[name], grads[name], mom[name], var[name]
        if wv.ndim == 1:
            wv, gv, mv, vv = (t.reshape(-1, LANES) for t in (wv, gv, mv, vv))
        dl, mn, vn = _adamw("adamw_" + name, wv, gv, mv, vv)
        deltas[name], new_m[name], new_v[name] = (t.reshape(wts[name].shape) for t in (dl, mn, vn))

    reduced = {l: _reduce_scatter_end(f"rs{l}", rs_state[l], True, dx) for l in range(depth - 1, 0, -1)}
    reduced[0] = _reduce_scatter_end("rs0", rs_state[0], True, deltas["ada_w"])

    for gi, (nms, ax) in enumerate(_GROUPS):
        for q, nm in enumerate(nms):
            wv, mv, vv = wts[nm], mom[nm], var[nm]
            per_layer = [reduced[l][gi][q] for l in range(depth)]
            if ax == 1 and wv.shape[2] % LANES and nm != "w_in":
                tr = lambda t: jnp.swapaxes(t, 1, 2)
                gt = jnp.stack([g.T for g in per_layer], axis=0)
                dl, mn, vn = _adamw("adamw_" + nm, tr(wv), gt, tr(mv), tr(vv))
                grads[nm], deltas[nm], new_m[nm], new_v[nm] = tr(gt), tr(dl), tr(mn), tr(vn)
            elif nm == "w_in" and wv.shape[2] % LANES:
                tr = lambda t: jnp.transpose(t, (2, 0, 1))
                back = lambda t: jnp.transpose(t, (1, 2, 0))
                gt = jnp.stack([g.T for g in per_layer], axis=1)
                dl, mn, vn = _adamw_leading("adamw_" + nm, tr(wv), gt, tr(mv), tr(vv))
                grads[nm], deltas[nm], new_m[nm], new_v[nm] = back(gt), back(dl), back(mn), back(vn)
            else:
                grads[nm], deltas[nm], new_m[nm], new_v[nm] = _adamw_stacked(
                    "adamw_" + nm, wv, mv, vv, [reduced[l][gi] for l in range(depth)], q)

    return (loss, dx.reshape(1, s, d), *[grads[n_] for n_ in names], *[deltas[n_] for n_ in names],
            *[new_m[n_] for n_ in names], *[new_v[n_] for n_ in names])
```

```python
import functools

import jax
import jax.numpy as jnp
from jax import lax
from jax.experimental import pallas as pl
from jax.experimental.pallas import tpu as pltpu

F32 = jnp.float32
BF16 = jnp.bfloat16
MESH = pl.DeviceIdType.MESH

NORM_EPS = 1e-6
DN_HEADS, DN_DIM, DN_CHUNK, DN_CONV = 8, 128, 64, 4
DN_WIDTH = DN_HEADS * DN_DIM
DA_HEADS, DA_DIM, DA_BLOCK = 12, 64, 128
DA_WIDTH = DA_HEADS * DA_DIM
DA_PATTERNS = ((128, 1), (512, 4), (2048, 16))
ALIBI_MAX_EXP = 8.0
N_ADA = 9
LANES = 128
V7X_VMEM_BYTES = 64 << 20
ADAM_LR, ADAM_B1, ADAM_B2, ADAM_EPS, ADAM_WD, ADAM_STEP = 0.001, 0.9, 0.999, 1e-08, 0.01, 10
NEG = -1e30
HI = lax.Precision.HIGHEST
NN = (((1,), (0,)), ((), ()))
NT = (((1,), (1,)), ((), ()))
TN = (((0,), (0,)), ((), ()))


def _nbytes(shape, dtype):
    n = 1
    for s in shape:
        n *= s
    return n * jnp.dtype(dtype).itemsize


def _params(block_bytes, scratch_bytes=0):
    need = 2 * block_bytes + scratch_bytes
    lim = min(max(need + need // 4 + (4 << 20), 32 << 20), V7X_VMEM_BYTES - (6 << 20))
    return pltpu.CompilerParams(vmem_limit_bytes=int(lim))


def _pick(n, cands):
    for c in cands:
        if c <= n and n % c == 0:
            return c
    return n


def _sigmoid(x):
    return jax.nn.sigmoid(x)


def _silu(x):
    return x * jax.nn.sigmoid(x)


def _softplus(x):
    return jnp.maximum(x, 0.0) + jnp.log(1.0 + jnp.exp(-jnp.abs(x)))


def _rowwise(name, fn, rows, bcast, row_outs, red_outs=(), bm=512):
    rows = [r if isinstance(r, tuple) else (r, r.shape[1], 0) for r in rows]
    s = rows[0][0].shape[0]
    bm = _pick(s, (bm, 128, 64, 32, 16, 8))
    nr, nb, no, nd = len(rows), len(bcast), len(row_outs), len(red_outs)
    in_specs = [pl.BlockSpec((bm, w), functools.partial(lambda i, ci: (i, ci), ci=ci)) for (_, w, ci) in rows]
    in_specs += [pl.BlockSpec(b.shape, lambda i: (0, 0)) for b in bcast]
    out_shape = [jax.ShapeDtypeStruct((s, w), dt) for (w, dt) in row_outs]
    out_shape += [jax.ShapeDtypeStruct((r, w), F32) for (r, w) in red_outs]
    out_specs = [pl.BlockSpec((bm, w), lambda i: (i, 0)) for (w, _) in row_outs]
    out_specs += [pl.BlockSpec((r, w), lambda i: (0, 0)) for (r, w) in red_outs]

    def body(*refs):
        ins = [r[...] for r in refs[:nr + nb]]
        outs = refs[nr + nb:nr + nb + no]
        reds = refs[nr + nb + no:]
        ov, rv = fn(*ins)
        for o, v in zip(outs, ov):
            o[...] = v.astype(o.dtype)
        if nd:
            @pl.when(pl.program_id(0) == 0)
            def _():
                for r in reds:
                    r[...] = jnp.zeros(r.shape, F32)
            for r, v in zip(reds, rv):
                r[...] += v.astype(F32)

    blk = sum(_nbytes((bm, w), a.dtype) for (a, w, _) in rows) + sum(_nbytes(b.shape, b.dtype) for b in bcast)
    blk += sum(_nbytes((bm, w), dt) for (w, dt) in row_outs) + sum(_nbytes(r, F32) for r in red_outs)
    res = pl.pallas_call(
        body, name=name, grid=(s // bm,), in_specs=in_specs, out_specs=out_specs, out_shape=out_shape,
        compiler_params=_params(3 * blk),
    )(*[a for (a, _, _) in rows], *bcast)
    return res


def _matmul(name, a, b, *, ta=False, tb=False, outs=(F32,), epi=None, epi_rows=(), epi_bcast=(),
            bm=None, bn=None, bk=None):
    if ta:
        k, m = a.shape
    else:
        m, k = a.shape
    n = b.shape[0] if tb else b.shape[1]
    assert (b.shape[1] if tb else b.shape[0]) == k, (name, a.shape, b.shape)
    if bm is None:
        bm = _pick(m, (1024, 1408, 768, 512, 384, 256, 128)) if ta else _pick(m, (1024, 512, 256, 128, 64, 32, 16))
    if bk is None:
        bk = k if k <= 3072 else _pick(k, (2816, 2048, 1024, 512))
        if ta:
            bk = _pick(k, (2048, 1024, 512, 256, 128, 64, 32, 16))
    if bn is None:
        bn = _pick(n, (1024, 768, 512, 384, 256, 128) if bk <= 2048 else (512, 384, 256, 128))
    nk = k // bk
    dims = TN if ta else (NT if tb else NN)
    a_spec = pl.BlockSpec((bk, bm), lambda i, j, kk: (kk, i)) if ta else pl.BlockSpec((bm, bk), lambda i, j, kk: (i, kk))
    b_spec = pl.BlockSpec((bn, bk), lambda i, j, kk: (j, kk)) if tb else pl.BlockSpec((bk, bn), lambda i, j, kk: (kk, j))
    in_specs = [a_spec, b_spec]
    in_specs += [pl.BlockSpec((bm, bn), lambda i, j, kk: (i, j)) for _ in epi_rows]
    in_specs += [pl.BlockSpec((1, bn), lambda i, j, kk: (0, j)) for _ in epi_bcast]
    out_shape = [jax.ShapeDtypeStruct((m, n), dt) for dt in outs]
    out_specs = [pl.BlockSpec((bm, bn), lambda i, j, kk: (i, j)) for _ in outs]
    ner, neb, no = len(epi_rows), len(epi_bcast), len(outs)

    def body(*refs):
        a_ref, b_ref = refs[0], refs[1]
        extra = refs[2:2 + ner + neb]
        out_refs = refs[2 + ner + neb:2 + ner + neb + no]
        prod = lax.dot_general(a_ref[...], b_ref[...], dims, preferred_element_type=F32)

        def finish(acc):
            vals = epi(acc, *[r[...] for r in extra]) if epi is not None else (acc,)
            for o, v in zip(out_refs, vals):
                o[...] = v.astype(o.dtype)

        if nk == 1:
            finish(prod)
        else:
            acc_ref = refs[-1]
            kk = pl.program_id(2)

            @pl.when(kk == 0)
            def _():
                acc_ref[...] = prod

            @pl.when(kk > 0)
            def _():
                acc_ref[...] += prod

            @pl.when(kk == nk - 1)
            def _():
                finish(acc_ref[...])

    blk = _nbytes((bm, bk), a.dtype) + _nbytes((bk, bn), b.dtype)
    blk += sum(_nbytes((bm, bn), r.dtype) for r in epi_rows) + sum(_nbytes((bm, bn), dt) for dt in outs)
    scratch = [pltpu.VMEM((bm, bn), F32)] if nk > 1 else []
    res = pl.pallas_call(
        body, name=name, grid=(m // bm, n // bn, nk), in_specs=in_specs, out_specs=out_specs,
        out_shape=out_shape, scratch_shapes=scratch,
        compiler_params=_params(blk, 3 * _nbytes((bm, bn), F32)),
    )(a, b, *epi_rows, *epi_bcast)
    return res


def _mm_core(name, grid, nk, pairs, out_defs, acc_shape, epi=None, epi_ins=()):
    npair, nep, no = len(pairs), len(epi_ins), len(out_defs)

    def body(*refs):
        extra = refs[2 * npair:2 * npair + nep]
        out_refs = refs[2 * npair + nep:2 * npair + nep + no]
        prod = None
        for p in range(npair):
            d = lax.dot_general(refs[2 * p][...], refs[2 * p + 1][...], pairs[p][4], preferred_element_type=F32)
            prod = d if prod is None else prod + d

        def finish(acc):
            vals = epi(acc, *[r[...] for r in extra]) if epi is not None else (acc,)
            for o, v in zip(out_refs, vals):
                o[...] = v.astype(o.dtype)

        if nk == 1:
            finish(prod)
        else:
            acc_ref = refs[-1]
            kk = pl.program_id(2)

            @pl.when(kk == 0)
            def _():
                acc_ref[...] = prod

            @pl.when(kk > 0)
            def _():
                acc_ref[...] += prod

            @pl.when(kk == nk - 1)
            def _():
                finish(acc_ref[...])

    def blk_bytes(spec, dtype):
        return _nbytes([s for s in spec.block_shape if s is not None], dtype)

    blk = sum(blk_bytes(sa, a.dtype) + blk_bytes(sb, b.dtype) for (a, sa, b, sb, _) in pairs)
    blk += sum(blk_bytes(sp, arr.dtype) for (arr, sp) in epi_ins) + sum(blk_bytes(sp, dt) for (_, dt, sp) in out_defs)
    ins, in_specs = [], []
    for (a, sa, b, sb, _) in pairs:
        ins += [a, b]
        in_specs += [sa, sb]
    ins += [arr for (arr, _) in epi_ins]
    in_specs += [sp for (_, sp) in epi_ins]
    return pl.pallas_call(
        body, name=name, grid=grid, in_specs=in_specs, out_specs=[sp for (_, _, sp) in out_defs],
        out_shape=[jax.ShapeDtypeStruct(sh, dt) for (sh, dt, _) in out_defs],
        scratch_shapes=[pltpu.VMEM(acc_shape, F32)] if nk > 1 else [],
        compiler_params=_params(blk, 3 * _nbytes(acc_shape, F32)),
    )(*ins)


def _rms_mod(h, ln, sh, sc):
    n = h * lax.rsqrt(jnp.mean(h * h, axis=-1, keepdims=True) + NORM_EPS) * ln
    return n * (1.0 + sc) + sh


def _swiglu_act(g, u):
    return _silu(g.astype(F32)) * u.astype(F32)


def _dn_prep(yc, pba, alog, dtb):
    act = _silu(yc)
    parts = []
    for idx in range(2 * DN_HEADS):
        seg = act[:, idx * DN_DIM:(idx + 1) * DN_DIM]
        seg = seg * lax.rsqrt(jnp.sum(seg * seg, axis=-1, keepdims=True) + NORM_EPS)
        if idx < DN_HEADS:
            seg = seg * (DN_DIM ** -0.5)
        parts.append(seg)
    parts.append(act[:, 2 * DN_WIDTH:])
    qkvn = jnp.concatenate(parts, axis=1)
    lane = lax.broadcasted_iota(jnp.int32, pba.shape, 1)
    beta = _sigmoid(pba)
    g = -jnp.exp(alog) * _softplus(pba + dtb)
    gb = jnp.where(lane < DN_HEADS, beta, jnp.where(lane < 2 * DN_HEADS, g, 0.0))
    return qkvn, gb


def _dn_outnorm(o_a, z, dn):
    parts = []
    for h in range(DN_HEADS):
        seg = o_a[:, h * DN_DIM:(h + 1) * DN_DIM]
        seg = seg * lax.rsqrt(jnp.mean(seg * seg, axis=-1, keepdims=True) + NORM_EPS) * dn
        parts.append(seg)
    return jnp.concatenate(parts, axis=1) * _silu(z)


def _shift_down(x, halo8, s):
    r = pltpu.roll(x, s, axis=0)
    top = pltpu.roll(halo8, s, axis=0)
    i8 = lax.broadcasted_iota(jnp.int32, top.shape, 0)
    return jnp.concatenate([jnp.where(i8 < s, top, r[0:8]), r[8:]], axis=0)


def _shift_up(x, halo8, s):
    m = x.shape[0]
    r = pltpu.roll(x, m - s, axis=0)
    bot = pltpu.roll(halo8, 8 - s, axis=0)
    i8 = lax.broadcasted_iota(jnp.int32, bot.shape, 0)
    return jnp.concatenate([r[:m - 8], jnp.where(i8 >= 8 - s, bot, r[m - 8:])], axis=0)


def _conv_prep_fwd(name, pq, convw8, pba, alog, dtb, bm=256):
    s, w = pq.shape
    nblk = s // bm
    hb = bm // 16

    def body(x_ref, halo_ref, w_ref, pba_ref, alog_ref, dtb_ref, yc_ref, qkv_ref, gb_ref):
        i = pl.program_id(0)
        x = x_ref[...].astype(F32)
        halo = jnp.where(i > 0, halo_ref[...].astype(F32)[8:16], 0.0)
        cw = w_ref[...]
        y = x * cw[DN_CONV - 1:DN_CONV]
        for sft in range(1, DN_CONV):
            y = y + _shift_down(x, halo, sft) * cw[DN_CONV - 1 - sft:DN_CONV - sft]
        ycb = y.astype(BF16)
        yc_ref[...] = ycb
        qkvn, gb = _dn_prep(ycb.astype(F32), pba_ref[...], alog_ref[...], dtb_ref[...])
        qkv_ref[...] = qkvn.astype(BF16)
        gb_ref[...] = gb

    blk = 3 * _nbytes((bm, w), BF16) + 4 * _nbytes((bm, w), F32)
    return pl.pallas_call(
        body, name=name, grid=(nblk,),
        in_specs=[pl.BlockSpec((bm, w), lambda i: (i, 0)),
                  pl.BlockSpec((16, w), lambda i: (jnp.maximum(i * hb - 1, 0), 0)),
                  pl.BlockSpec(convw8.shape, lambda i: (0, 0)),
                  pl.BlockSpec((bm, LANES), lambda i: (i, 0)),
                  pl.BlockSpec((1, LANES), lambda i: (0, 0)),
                  pl.BlockSpec((1, LANES), lambda i: (0, 0))],
        out_specs=[pl.BlockSpec((bm, w), lambda i: (i, 0)), pl.BlockSpec((bm, w), lambda i: (i, 0)),
                   pl.BlockSpec((bm, LANES), lambda i: (i, 0))],
        out_shape=[jax.ShapeDtypeStruct((s, w), BF16), jax.ShapeDtypeStruct((s, w), BF16),
                   jax.ShapeDtypeStruct((s, LANES), F32)],
        compiler_params=_params(blk),
    )(pq, pq, convw8, pba, alog, dtb)


def _conv_bwd(name, dyc, pq, convw8, bm=256):
    s, w = pq.shape
    nblk = s // bm
    hb = bm // 16

    def body(dy_ref, dyn_ref, x_ref, xh_ref, w_ref, dx_ref, dw_ref):
        i = pl.program_id(0)
        dy = dy_ref[...].astype(F32)
        nxt = jnp.where(i < nblk - 1, dyn_ref[...].astype(F32)[0:8], 0.0)
        x = x_ref[...].astype(F32)
        halo = jnp.where(i > 0, xh_ref[...].astype(F32)[8:16], 0.0)
        cw = w_ref[...]
        dx = dy * cw[DN_CONV - 1:DN_CONV]
        for sft in range(1, DN_CONV):
            dx = dx + _shift_up(dy, nxt, sft) * cw[DN_CONV - 1 - sft:DN_CONV - sft]
        dx_ref[...] = dx.astype(dx_ref.dtype)
        r8 = lax.broadcasted_iota(jnp.int32, (8, w), 0)
        dw = jnp.zeros((8, w), F32)
        for j in range(DN_CONV):
            sft = DN_CONV - 1 - j
            xs = x if sft == 0 else _shift_down(x, halo, sft)
            dw = dw + jnp.where(r8 == j, jnp.sum(dy * xs, axis=0, keepdims=True), 0.0)

        @pl.when(i == 0)
        def _():
            dw_ref[...] = jnp.zeros((8, w), F32)
        dw_ref[...] += dw

    blk = 4 * _nbytes((bm, w), BF16) + 5 * _nbytes((bm, w), F32)
    return pl.pallas_call(
        body, name=name, grid=(nblk,),
        in_specs=[pl.BlockSpec((bm, w), lambda i: (i, 0)),
                  pl.BlockSpec((16, w), lambda i: (jnp.minimum((i + 1) * hb, s // 16 - 1), 0)),
                  pl.BlockSpec((bm, w), lambda i: (i, 0)),
                  pl.BlockSpec((16, w), lambda i: (jnp.maximum(i * hb - 1, 0), 0)),
                  pl.BlockSpec(convw8.shape, lambda i: (0, 0))],
        out_specs=[pl.BlockSpec((bm, w), lambda i: (i, 0)), pl.BlockSpec((8, w), lambda i: (0, 0))],
        out_shape=[jax.ShapeDtypeStruct((s, w), BF16), jax.ShapeDtypeStruct((8, w), F32)],
        compiler_params=_params(blk),
    )(dyc, dyc, pq, pq, convw8)


BNN = (((2,), (1,)), ((0,), (0,)))
BNT = (((2,), (2,)), ((0,), (0,)))
BTN = (((1,), (1,)), ((0,), (0,)))


def _raw_dot_1pass(a, b, dims):
    return lax.dot_general(a.astype(BF16), b.astype(BF16), dims, preferred_element_type=F32)


def _raw_dot_3pass(a, b, dims):
    ah = a.astype(BF16)
    al = (a - ah.astype(F32)).astype(BF16)
    bh = b.astype(BF16)
    bl = (b - bh.astype(F32)).astype(BF16)
    d = lambda x, y: lax.dot_general(x, y, dims, preferred_element_type=F32)
    return d(ah, bh) + (d(ah, bl) + d(al, bh))


def _with_same_precision_vjp(raw):
    @functools.partial(jax.custom_vjp, nondiff_argnums=(2,))
    def dot(a, b, dims):
        return raw(a, b, dims)

    def fwd(a, b, dims):
        return raw(a, b, dims), (a, b)

    def bwd(dims, res, ct):
        a, b = res
        if dims == BNN:
            return raw(ct, b, BNT), raw(a, ct, BTN)
        if dims == BNT:
            return raw(ct, b, BNN), raw(ct, a, BTN)
        assert dims == BTN
        return raw(b, ct, BNT), raw(a, ct, BNN)

    dot.defvjp(fwd, bwd)
    return dot


_dot_1pass_vjp = _with_same_precision_vjp(_raw_dot_1pass)
_dot_3pass_vjp = _with_same_precision_vjp(_raw_dot_3pass)


def _dot_bf16(a, b, dims=BNN):
    return _dot_1pass_vjp(a, b, dims)


def _dot_3pass(a, b, dims=BNN):
    return _dot_3pass_vjp(a, b, dims)


def _neumann_inverse(x):
    h, c, _ = x.shape
    eye = lax.broadcasted_iota(jnp.int32, (h, c, c), 1) == lax.broadcasted_iota(jnp.int32, (h, c, c), 2)
    t = jnp.where(eye, 1.0, 0.0) + x
    p = x
    for _ in range(5):
        p = _raw_dot_3pass(p, p, BNN)
        t = t + _raw_dot_3pass(t, p, BNN)
    return t


@jax.custom_vjp
def _known_inverse(x, t):
    return t


def _known_inverse_fwd(x, t):
    return t, t


def _known_inverse_bwd(t, ct):
    return _raw_dot_3pass(_raw_dot_3pass(t, ct, BTN), t, BNT), jnp.zeros_like(t)


_known_inverse.defvjp(_known_inverse_fwd, _known_inverse_bwd)


def _delta_chunk(q, k, v, gcol, bcol, state, t_known=None):
    h, c, _ = q.shape
    row = lax.broadcasted_iota(jnp.int32, (h, c, c), 1)
    col = lax.broadcasted_iota(jnp.int32, (h, c, c), 2)
    incl, strict, eye = row >= col, row > col, row == col
    g_b = jnp.broadcast_to(gcol, (h, c, c))
    gc_row = jnp.sum(jnp.where(row <= col, g_b, 0.0), axis=1, keepdims=True)
    g_r = jnp.sum(jnp.where(eye, g_b, 0.0), axis=1, keepdims=True)
    gc_col = jnp.sum(jnp.where(incl, jnp.broadcast_to(g_r, (h, c, c)), 0.0), axis=2, keepdims=True)
    decay = jnp.exp(jnp.where(incl, gc_col - gc_row, NEG))
    kb = k * bcol
    vb = v * bcol
    x = -jnp.where(strict, _dot_bf16(kb, k, BNT) * decay, 0.0)
    t = _neumann_inverse(x) if t_known is None else _known_inverse(x, t_known)
    eg = jnp.exp(gc_col)
    u = _dot_3pass(t, vb)
    w = _dot_3pass(t, kb * eg)
    qk = _dot_bf16(q, k, BNT) * decay
    v_new = u - _dot_bf16(w, state)
    o = _dot_bf16(q * eg, state) + _dot_bf16(qk, v_new)
    g_last = jnp.sum(g_r, axis=2, keepdims=True)
    new_state = state * jnp.exp(g_last) + _dot_bf16(k * jnp.exp(g_last - gc_col), v_new, BTN)
    return o, new_state, t


def _lane_col(blk, idx):
    lane = lax.broadcasted_iota(jnp.int32, blk.shape, 1)
    return jnp.sum(jnp.where(lane == idx, blk, 0.0), axis=1, keepdims=True)


def _dn_heads(ref, base):
    return jnp.stack([ref[:, base + h * DN_DIM:base + (h + 1) * DN_DIM] for h in range(DN_HEADS)], axis=0).astype(F32)


def _dn_cols(gbv, base):
    return jnp.stack([_lane_col(gbv, base + h) for h in range(DN_HEADS)], axis=0)


def _delta_fwd(name, qkvn, gb):
    s = qkvn.shape[0]
    n = s // DN_CHUNK
    c = DN_CHUNK

    def body(qkv_ref, gb_ref, o_ref, st_ref, t_ref, state):
        @pl.when(pl.program_id(0) == 0)
        def _():
            state[...] = jnp.zeros(state.shape, F32)

        gbv = gb_ref[...]
        st = state[...]
        st_ref[0] = st
        o, new, t = _delta_chunk(_dn_heads(qkv_ref, 0), _dn_heads(qkv_ref, DN_WIDTH), _dn_heads(qkv_ref, 2 * DN_WIDTH),
                                 _dn_cols(gbv, DN_HEADS), _dn_cols(gbv, 0), st)
        for h in range(DN_HEADS):
            o_ref[:, h * DN_DIM:(h + 1) * DN_DIM] = o[h]
        t_ref[0] = t
        state[...] = new

    blk = _nbytes((c, 3 * DN_WIDTH), BF16) + _nbytes((c, LANES), F32) + _nbytes((c, DN_WIDTH), F32)
    blk += _nbytes((DN_HEADS, DN_DIM, DN_DIM), F32) + _nbytes((DN_HEADS, c, c), F32)
    return pl.pallas_call(
        body, name=name, grid=(n,),
        in_specs=[pl.BlockSpec((c, 3 * DN_WIDTH), lambda i: (i, 0)), pl.BlockSpec((c, LANES), lambda i: (i, 0))],
        out_specs=[pl.BlockSpec((c, DN_WIDTH), lambda i: (i, 0)),
                   pl.BlockSpec((1, DN_HEADS, DN_DIM, DN_DIM), lambda i: (i, 0, 0, 0)),
                   pl.BlockSpec((1, DN_HEADS, c, c), lambda i: (i, 0, 0, 0))],
        out_shape=[jax.ShapeDtypeStruct((s, DN_WIDTH), F32),
                   jax.ShapeDtypeStruct((n, DN_HEADS, DN_DIM, DN_DIM), F32),
                   jax.ShapeDtypeStruct((n, DN_HEADS, c, c), F32)],
        scratch_shapes=[pltpu.VMEM((DN_HEADS, DN_DIM, DN_DIM), F32)],
        compiler_params=_params(blk, 8 << 20),
    )(qkvn, gb)


def _delta_bwd(name, qkvn, gb, states, tinv, d_o):
    s = qkvn.shape[0]
    n = s // DN_CHUNK
    c = DN_CHUNK

    def body(qkv_ref, gb_ref, st_ref, t_ref, do_ref, dqkv_ref, dgb_ref, dstate):
        @pl.when(pl.program_id(0) == 0)
        def _():
            dstate[...] = jnp.zeros(dstate.shape, F32)

        gbv = gb_ref[...]
        lane = lax.broadcasted_iota(jnp.int32, (c, LANES), 1)
        t_known = t_ref[0]
        chunk = lambda *args: _delta_chunk(*args, t_known=t_known)[:2]
        _, vjp = jax.vjp(chunk, _dn_heads(qkv_ref, 0), _dn_heads(qkv_ref, DN_WIDTH),
                         _dn_heads(qkv_ref, 2 * DN_WIDTH), _dn_cols(gbv, DN_HEADS), _dn_cols(gbv, 0), st_ref[0])
        dq, dk, dv, dg, db, dst = vjp((_dn_heads(do_ref, 0), dstate[...]))
        dgb = jnp.zeros((c, LANES), F32)
        for h in range(DN_HEADS):
            dqkv_ref[:, h * DN_DIM:(h + 1) * DN_DIM] = dq[h]
            dqkv_ref[:, DN_WIDTH + h * DN_DIM:DN_WIDTH + (h + 1) * DN_DIM] = dk[h]
            dqkv_ref[:, 2 * DN_WIDTH + h * DN_DIM:2 * DN_WIDTH + (h + 1) * DN_DIM] = dv[h]
            dgb = dgb + jnp.where(lane == h, db[h], 0.0) + jnp.where(lane == DN_HEADS + h, dg[h], 0.0)
        dstate[...] = dst
        dgb_ref[...] = dgb

    rev = lambda i: (n - 1 - i, 0)
    blk = _nbytes((c, 3 * DN_WIDTH), BF16) + 2 * _nbytes((c, LANES), F32) + _nbytes((c, DN_WIDTH), F32)
    blk += _nbytes((DN_HEADS, DN_DIM, DN_DIM), F32) + _nbytes((c, 3 * DN_WIDTH), F32)
    return pl.pallas_call(
        body, name=name, grid=(n,),
        in_specs=[pl.BlockSpec((c, 3 * DN_WIDTH), rev), pl.BlockSpec((c, LANES), rev),
                  pl.BlockSpec((1, DN_HEADS, DN_DIM, DN_DIM), lambda i: (n - 1 - i, 0, 0, 0)),
                  pl.BlockSpec((1, DN_HEADS, c, c), lambda i: (n - 1 - i, 0, 0, 0)),
                  pl.BlockSpec((c, DN_WIDTH), rev)],
        out_specs=[pl.BlockSpec((c, 3 * DN_WIDTH), rev), pl.BlockSpec((c, LANES), rev)],
        out_shape=[jax.ShapeDtypeStruct((s, 3 * DN_WIDTH), F32), jax.ShapeDtypeStruct((s, LANES), F32)],
        scratch_shapes=[pltpu.VMEM((DN_HEADS, DN_DIM, DN_DIM), F32)],
        compiler_params=_params(blk, 16 << 20),
    )(qkvn, gb, states, tinv, d_o)


def _da_scores(q2f, k2, sub, valid, distf, head):
    lane = lax.broadcasted_iota(jnp.int32, q2f.shape, 1)
    hmask = (lane < DA_DIM) if sub == 0 else (lane >= DA_DIM)
    qm = jnp.where(hmask, q2f, 0.0).astype(BF16)
    slope = 2.0 ** (-ALIBI_MAX_EXP * (head + 1) / DA_HEADS)
    sc = lax.dot_general(qm, k2, NT, preferred_element_type=F32) * (DA_DIM ** -0.5)
    return jnp.where(valid, sc - slope * distf, NEG), qm, hmask


def _da_mask(i, r):
    qi = lax.broadcasted_iota(jnp.int32, (DA_BLOCK, 2 * DA_BLOCK), 0)
    ki = lax.broadcasted_iota(jnp.int32, (DA_BLOCK, 2 * DA_BLOCK), 1)
    dist = qi + DA_BLOCK - ki
    valid = (dist >= 0) & (dist <= DA_BLOCK) & ((ki >= DA_BLOCK) | (i > 0))
    return valid, (dist * r).astype(F32)


def _da_fwd(name, pda, r):
    s = pda.shape[0]
    n = s // r
    nb = n // DA_BLOCK
    w = DA_WIDTH
    dav = pda.reshape(n, r * 3 * w)

    def body(q_ref, kc_ref, kp_ref, vc_ref, vp_ref, o_ref, lse_ref):
        i = pl.program_id(1)
        valid, distf = _da_mask(i, r)
        lane = lax.broadcasted_iota(jnp.int32, (DA_BLOCK, LANES), 1)
        lse = jnp.zeros((DA_BLOCK, LANES), F32)
        for hp in range(DA_HEADS // 2):
            sl = slice(hp * LANES, (hp + 1) * LANES)
            q2f = q_ref[:, sl].astype(F32)
            k2 = jnp.concatenate([kp_ref[:, sl], kc_ref[:, sl]], axis=0)
            v2 = jnp.concatenate([vp_ref[:, sl], vc_ref[:, sl]], axis=0)
            o2 = None
            for sub in range(2):
                head = 2 * hp + sub
                sc, _, hmask = _da_scores(q2f, k2, sub, valid, distf, head)
                mx = jnp.max(sc, axis=1, keepdims=True)
                p = jnp.exp(sc - mx)
                l = jnp.sum(p, axis=1, keepdims=True)
                pv = lax.dot_general(p.astype(BF16), v2, NN, preferred_element_type=F32) / l
                o2 = pv if sub == 0 else jnp.where(hmask, pv, o2)
                lse = jnp.where(lane == head, mx + jnp.log(l), lse)
            o_ref[:, sl] = o2.astype(o_ref.dtype)
        lse_ref[...] = lse

    prev = lambda col: (lambda p, i: (jnp.maximum(i - 1, 0), 3 * p + col))
    cur = lambda col: (lambda p, i: (i, 3 * p + col))
    blk = 5 * _nbytes((DA_BLOCK, w), BF16) + _nbytes((DA_BLOCK, w), F32) + _nbytes((DA_BLOCK, LANES), F32)
    o, lse = pl.pallas_call(
        body, name=name, grid=(r, nb),
        in_specs=[pl.BlockSpec((DA_BLOCK, w), cur(0)), pl.BlockSpec((DA_BLOCK, w), cur(1)),
                  pl.BlockSpec((DA_BLOCK, w), prev(1)), pl.BlockSpec((DA_BLOCK, w), cur(2)),
                  pl.BlockSpec((DA_BLOCK, w), prev(2))],
        out_specs=[pl.BlockSpec((DA_BLOCK, w), lambda p, i: (i, p)),
                   pl.BlockSpec((DA_BLOCK, LANES), lambda p, i: (i, p))],
        out_shape=[jax.ShapeDtypeStruct((n, r * w), BF16), jax.ShapeDtypeStruct((n, r * LANES), F32)],
        compiler_params=_params(blk, 8 << 20),
    )(dav, dav, dav, dav, dav)
    return o.reshape(s, w), lse.reshape(s, LANES)


def _da_bwd(name, pda, d_ob, lse_tot, delta, r):
    s = pda.shape[0]
    n = s // r
    nb = n // DA_BLOCK
    w = DA_WIDTH
    dav = pda.reshape(n, r * 3 * w)
    dov = d_ob.reshape(n, r * w)
    lv = lse_tot.reshape(n, r * LANES)
    dlv = delta.reshape(n, r * LANES)

    def body(q_ref, kc_ref, kp_ref, vc_ref, vp_ref, do_ref, l_ref, dl_ref, dq_ref, dk_ref, dv_ref, ck, cv):
        i = pl.program_id(1)

        @pl.when(i == 0)
        def _():
            ck[...] = jnp.zeros(ck.shape, F32)
            cv[...] = jnp.zeros(cv.shape, F32)

        @pl.when(i < nb)
        def _():
            valid, distf = _da_mask(i, r)
            lsev = l_ref[...]
            dlt = dl_ref[...]
            for hp in range(DA_HEADS // 2):
                sl = slice(hp * LANES, (hp + 1) * LANES)
                q2f = q_ref[:, sl].astype(F32)
                k2 = jnp.concatenate([kp_ref[:, sl], kc_ref[:, sl]], axis=0)
                v2 = jnp.concatenate([vp_ref[:, sl], vc_ref[:, sl]], axis=0)
                do2f = do_ref[:, sl].astype(F32)
                dq2 = jnp.zeros((DA_BLOCK, LANES), F32)
                dk2 = jnp.zeros((2 * DA_BLOCK, LANES), F32)
                dv2 = jnp.zeros((2 * DA_BLOCK, LANES), F32)
                for sub in range(2):
                    head = 2 * hp + sub
                    sc, qm, hmask = _da_scores(q2f, k2, sub, valid, distf, head)
                    p = jnp.exp(sc - _lane_col(lsev, head))
                    dom = jnp.where(hmask, do2f, 0.0).astype(BF16)
                    dp = lax.dot_general(dom, v2, NT, preferred_element_type=F32)
                    ds = (p * (dp - _lane_col(dlt, head)) * (DA_DIM ** -0.5)).astype(BF16)
                    dq2 = dq2 + jnp.where(hmask, lax.dot_general(ds, k2, NN, preferred_element_type=F32), 0.0)
                    dk2 = dk2 + lax.dot_general(ds, qm, TN, preferred_element_type=F32)
                    dv2 = dv2 + lax.dot_general(p.astype(BF16), dom, TN, preferred_element_type=F32)
                dq_ref[:, sl] = dq2.astype(dq_ref.dtype)
                dk_ref[:, sl] = (ck[:, sl] + dk2[:DA_BLOCK]).astype(dk_ref.dtype)
                dv_ref[:, sl] = (cv[:, sl] + dv2[:DA_BLOCK]).astype(dv_ref.dtype)
                ck[:, sl] = dk2[DA_BLOCK:]
                cv[:, sl] = dv2[DA_BLOCK:]

        @pl.when(i == nb)
        def _():
            dk_ref[...] = ck[...].astype(dk_ref.dtype)
            dv_ref[...] = cv[...].astype(dv_ref.dtype)

    qrow = lambda i: jnp.minimum(i, nb - 1)
    prev = lambda col: (lambda p, i: (jnp.maximum(qrow(i) - 1, 0), 3 * p + col))
    cur = lambda col: (lambda p, i: (qrow(i), 3 * p + col))
    same = lambda p, i: (qrow(i), p)
    late = lambda p, i: (jnp.maximum(i - 1, 0), p)
    blk = 6 * _nbytes((DA_BLOCK, w), BF16) + 2 * _nbytes((DA_BLOCK, LANES), F32) + 3 * _nbytes((DA_BLOCK, w), F32)
    dq, dk, dv = pl.pallas_call(
        body, name=name, grid=(r, nb + 1),
        in_specs=[pl.BlockSpec((DA_BLOCK, w), cur(0)), pl.BlockSpec((DA_BLOCK, w), cur(1)),
                  pl.BlockSpec((DA_BLOCK, w), prev(1)), pl.BlockSpec((DA_BLOCK, w), cur(2)),
                  pl.BlockSpec((DA_BLOCK, w), prev(2)), pl.BlockSpec((DA_BLOCK, w), same),
                  pl.BlockSpec((DA_BLOCK, LANES), same), pl.BlockSpec((DA_BLOCK, LANES), same)],
        out_specs=[pl.BlockSpec((DA_BLOCK, w), same), pl.BlockSpec((DA_BLOCK, w), late),
                   pl.BlockSpec((DA_BLOCK, w), late)],
        out_shape=[jax.ShapeDtypeStruct((n, r * w), BF16)] * 3,
        scratch_shapes=[pltpu.VMEM((DA_BLOCK, w), F32), pltpu.VMEM((DA_BLOCK, w), F32)],
        compiler_params=_params(blk, 12 << 20),
    )(dav, dav, dav, dav, dav, dov, lv, dlv)
    return dq.reshape(s, w), dk.reshape(s, w), dv.reshape(s, w)


def _head_expand():
    hrow = lax.broadcasted_iota(jnp.int32, (LANES, DA_WIDTH), 0)
    lcol = lax.broadcasted_iota(jnp.int32, (LANES, DA_WIDTH), 1)
    return jnp.where(lcol // DA_DIM == hrow, 1.0, 0.0).astype(F32)


def _ffn_up(name, a, ga, tg, tu):
    s, d = a.shape
    nsh, _, _, ffs = ga.shape
    bm = _pick(s, (1024, 512, 256, 128))

    def body(a_ref, wg_ref, wu_ref, g_ref, u_ref, f_ref):
        av = a_ref[...]
        g = lax.dot_general(av, wg_ref[...], NN, preferred_element_type=F32)
        u = lax.dot_general(av, wu_ref[...], NN, preferred_element_type=F32)
        g_ref[...] = g.astype(BF16)
        u_ref[...] = u.astype(BF16)
        f_ref[...] = (_silu(g) * u).astype(BF16)

    wspec = lambda t: pl.BlockSpec((None, None, d, ffs), lambda i, j: (j, t, 0, 0))
    ospec = pl.BlockSpec((None, bm, ffs), lambda i, j: (j, i, 0))
    blk = _nbytes((bm, d), BF16) + 2 * _nbytes((d, ffs), BF16) + 3 * _nbytes((bm, ffs), BF16)
    return pl.pallas_call(
        body, name=name, grid=(s // bm, nsh),
        in_specs=[pl.BlockSpec((bm, d), lambda i, j: (i, 0)), wspec(tg), wspec(tu)],
        out_specs=[ospec] * 3, out_shape=[jax.ShapeDtypeStruct((nsh, s, ffs), BF16)] * 3,
        compiler_params=_params(blk, 4 * _nbytes((bm, ffs), F32)),
    )(a, ga, ga)


def _ffn_fwd(tag, h_in, ln, sh, sc, gt, ga, tg, tu, gb, td, weight):
    s, d = h_in.shape
    nsh, _, ffs, _ = gb.shape
    (a,) = _rowwise(tag + "_norm", lambda h, l, s1, s2: ((_rms_mod(h, l, s1, s2),), ()), [h_in], [ln, sh, sc],
                    [(d, BF16)])
    g, u, f = _ffn_up(tag + "_up", a, ga, tg, tu)
    bm, bn = _pick(s, (1024, 512, 256, 128)), _pick(d, (1024, 512, 256, 128))
    io = pl.BlockSpec((bm, bn), lambda i, j, kk: (i, j))
    h_out, o = _mm_core(
        tag + "_down", (s // bm, d // bn, nsh), nsh,
        [(f, pl.BlockSpec((None, bm, ffs), lambda i, j, kk: (kk, i, 0)),
          gb, pl.BlockSpec((None, None, ffs, bn), lambda i, j, kk: (kk, td, 0, j)), NN)],
        [((s, d), F32, io), ((s, d), BF16, io)], (bm, bn),
        epi=lambda acc, h, gv: (h + weight * gv * acc, acc),
        epi_ins=[(h_in, io), (gt, pl.BlockSpec((1, bn), lambda i, j, kk: (0, j)))])
    return h_out, dict(a=a, g=g, u=u, f=f, o=o)


def _resid_bwd(tag, dh_out, o, gt, weight):
    d = dh_out.shape[1]

    def fn(dh, ov, g):
        return (weight * g * dh,), (jnp.sum(weight * dh * ov.astype(F32), axis=0, keepdims=True),)

    do, d_gt = _rowwise(tag + "_resid_bwd", fn, [dh_out, o], [gt], [(d, BF16)], [(1, d)])
    return do, d_gt


def _norm_bwd(tag, h_in, da, dh_out, ln, sh, sc):
    d = h_in.shape[1]

    def fn(h, dav, dh, l, s1, s2):
        _, vjp = jax.vjp(_rms_mod, h, l, s1, s2)
        gh, gl, gs1, gs2 = vjp(dav)
        return (dh + gh,), (gl, gs1, gs2)

    return _rowwise(tag + "_norm_bwd", fn, [h_in, da, dh_out], [ln, sh, sc], [(d, F32)], [(1, d)] * 3)


def _ffn_bwd(tag, h_in, dh_out, sv, ln, sh, sc, gt, ga, tg, tu, gb, td, weight):
    s, d = h_in.shape
    nsh, _, ffs, _ = gb.shape
    bm, bn = _pick(s, (1024, 512, 256, 128)), _pick(d, (1024, 512, 256, 128))
    bk = _pick(s, (2048, 1024, 512, 256, 128))
    do, d_gt = _resid_bwd(tag, dh_out, sv["o"], gt, weight)

    def act_bwd(df, g, u):
        _, vjp = jax.vjp(_swiglu_act, g, u)
        return vjp(df)

    hid = pl.BlockSpec((None, bm, ffs), lambda i, j, kk: (j, i, 0))
    dg, du = _mm_core(
        tag + "_down_dx", (s // bm, nsh, 1), 1,
        [(do, pl.BlockSpec((bm, d), lambda i, j, kk: (i, 0)),
          gb, pl.BlockSpec((None, None, ffs, d), lambda i, j, kk: (j, td, 0, 0)), NT)],
        [((nsh, s, ffs), BF16, hid)] * 2, (bm, ffs), epi=act_bwd, epi_ins=[(sv["g"], hid), (sv["u"], hid)])
    (d_wd,) = _mm_core(
        tag + "_down_dw", (nsh, d // bn, s // bk), s // bk,
        [(sv["f"], pl.BlockSpec((None, bk, ffs), lambda i, j, kk: (i, kk, 0)),
          do, pl.BlockSpec((bk, bn), lambda i, j, kk: (kk, j)), TN)],
        [((nsh, ffs, d), BF16, pl.BlockSpec((None, ffs, bn), lambda i, j, kk: (i, 0, j)))], (ffs, bn))
    kmaj = pl.BlockSpec((None, bm, ffs), lambda i, j, kk: (kk, i, 0))
    wsp = lambda t: pl.BlockSpec((None, None, bn, ffs), functools.partial(lambda i, j, kk, t: (kk, t, j, 0), t=t))
    (da,) = _mm_core(
        tag + "_up_dx", (s // bm, d // bn, nsh), nsh, [(dg, kmaj, ga, wsp(tg), NT), (du, kmaj, ga, wsp(tu), NT)],
        [((s, d), F32, pl.BlockSpec((bm, bn), lambda i, j, kk: (i, j)))], (bm, bn))
    dws = []
    for nm, dh in (("_wg_dw", dg), ("_wu_dw", du)):
        (dw,) = _mm_core(
            tag + nm, (1, nsh, s // bk), s // bk,
            [(sv["a"], pl.BlockSpec((bk, d), lambda i, j, kk: (kk, 0)),
              dh, pl.BlockSpec((None, bk, ffs), lambda i, j, kk: (j, kk, 0)), TN)],
            [((nsh, d, ffs), BF16, pl.BlockSpec((None, d, ffs), lambda i, j, kk: (j, 0, 0)))], (d, ffs))
        dws.append(dw)
    dh_in, d_ln, d_sh, d_sc = _norm_bwd(tag, h_in, da, dh_out, ln, sh, sc)
    return dh_in, dict(wg=dws[0], wu=dws[1], wd=d_wd), dict(ln=d_ln, sh=d_sh, sc=d_sc, gt=d_gt)


def _mixer_fwd(tag, h_in, ln, sh, sc, gt, w, sp):
    d = h_in.shape[1]
    (a,) = _rowwise(tag + "_norm", lambda h, l, s1, s2: ((_rms_mod(h, l, s1, s2),), ()), [h_in], [ln, sh, sc],
                    [(d, BF16)])
    (pq,) = _matmul(tag + "_pq", a, w["wq"], outs=(BF16,))
    (pz,) = _matmul(tag + "_pz", a, w["wz"], outs=(BF16,))
    (pba,) = _matmul(tag + "_pba", a, w["wba"])
    (pda,) = _matmul(tag + "_pda", a, w["wda"], outs=(BF16,))
    (pg,) = _matmul(tag + "_pg", a, w["wg"], outs=(BF16,))
    yc, qkvn, gb = _conv_prep_fwd(tag + "_conv", pq, sp["conv8"], pba, sp["alog"], sp["dtb"])
    o_a, states, tinv = _delta_fwd(tag + "_delta", qkvn, gb)
    (o_an,) = _rowwise(tag + "_dnorm", lambda o, z, dn: ((_dn_outnorm(o, z.astype(F32), dn),), ()), [o_a, pz],
                       [sp["dn"]], [(DN_WIDTH, BF16)])
    ops, lses = [], []
    for (_, r) in DA_PATTERNS:
        o_p, lse_p = _da_fwd(f"{tag}_da{r}", pda, r)
        ops.append(o_p)
        lses.append(lse_p)

    def merge(o1, o2, o3, l1, l2, l3):
        mx = jnp.maximum(jnp.maximum(l1, l2), l3)
        e1, e2, e3 = jnp.exp(l1 - mx), jnp.exp(l2 - mx), jnp.exp(l3 - mx)
        tot = e1 + e2 + e3
        ex = _head_expand()
        up = lambda wgt: lax.dot_general(wgt / tot, ex, NN, precision=HI, preferred_element_type=F32)
        return (up(e1) * o1 + up(e2) * o2 + up(e3) * o3, mx + jnp.log(tot)), ()

    o_b, lse_tot = _rowwise(tag + "_merge", merge, ops + lses, [], [(DA_WIDTH, BF16), (LANES, F32)])
    (y_a,) = _matmul(tag + "_wa", o_an, w["w_a"], outs=(BF16,))
    (y_b,) = _matmul(tag + "_wb", o_b, w["w_b"], outs=(BF16,))

    def gate(ga, gbv, ya, yb):
        return _sigmoid(ga.astype(F32)) * ya.astype(F32) + _sigmoid(gbv.astype(F32)) * yb.astype(F32)

    (merged,) = _rowwise(tag + "_gate", lambda *v: ((gate(*v),), ()), [(pg, d, 0), (pg, d, 1), y_a, y_b], [],
                         [(d, BF16)])
    h_out, m = _matmul(tag + "_wo", merged, w["w_o"], outs=(F32, BF16), epi_rows=[h_in], epi_bcast=[gt],
                       epi=lambda acc, h, g: (h + g * acc, acc))
    sv = dict(a=a, pq=pq, pz=pz, pba=pba, pda=pda, pg=pg, yc=yc, qkvn=qkvn, gb=gb, o_a=o_a, states=states, tinv=tinv,
              o_an=o_an, o_b=o_b, lse=lse_tot, y_a=y_a, y_b=y_b, merged=merged, m=m, gate=gate)
    return h_out, sv


def _mixer_bwd(tag, h_in, dh_out, sv, ln, sh, sc, gt, w, sp):
    d = h_in.shape[1]
    dm, d_gt = _resid_bwd(tag, dh_out, sv["m"], gt, 1.0)
    (d_merged,) = _matmul(tag + "_wo_dx", dm, w["w_o"], tb=True, outs=(BF16,))
    (d_wo,) = _matmul(tag + "_wo_dw", sv["merged"], dm, ta=True, outs=(BF16,))
    gate = sv["gate"]

    def gate_bwd(dmg, ga, gbv, ya, yb):
        _, vjp = jax.vjp(gate, ga.astype(F32), gbv.astype(F32), ya.astype(F32), yb.astype(F32))
        dga, dgb, dya, dyb = vjp(dmg.astype(F32))
        return (jnp.concatenate([dga, dgb], axis=1), dya, dyb), ()

    pg = sv["pg"]
    d_pg, d_ya, d_yb = _rowwise(tag + "_gate_bwd", gate_bwd, [d_merged, (pg, d, 0), (pg, d, 1), sv["y_a"], sv["y_b"]],
                                [], [(2 * d, BF16), (d, BF16), (d, BF16)])
    (d_oan,) = _matmul(tag + "_wa_dx", d_ya, w["w_a"], tb=True)
    (d_wa,) = _matmul(tag + "_wa_dw", sv["o_an"], d_ya, ta=True, outs=(BF16,))
    (d_ob,) = _matmul(tag + "_wb_dx", d_yb, w["w_b"], tb=True, outs=(BF16,))
    (d_wb,) = _matmul(tag + "_wb_dw", sv["o_b"], d_yb, ta=True, outs=(BF16,))

    def dnorm_bwd(doan, o, z, dn):
        _, vjp = jax.vjp(_dn_outnorm, o, z.astype(F32), dn)
        go, gz, gdn = vjp(doan)
        return (go, gz), (gdn,)

    d_oa, d_pz, d_dn = _rowwise(tag + "_dnorm_bwd", dnorm_bwd, [d_oan, sv["o_a"], sv["pz"]], [sp["dn"]],
                                [(DN_WIDTH, F32), (DN_WIDTH, BF16)], [(1, DN_DIM)])
    d_qkvn, d_gb = _delta_bwd(tag + "_delta_bwd", sv["qkvn"], sv["gb"], sv["states"], sv["tinv"], d_oa)

    def prep_bwd(dq, dgbv, yc, pba, alog, dtb):
        _, vjp = jax.vjp(_dn_prep, yc.astype(F32), pba, alog, dtb)
        gyc, gpba, galog, gdtb = vjp((dq, dgbv))
        return (gyc, gpba), (galog, gdtb)

    d_yc, d_pba, d_alog, d_dtb = _rowwise(tag + "_prep_bwd", prep_bwd, [d_qkvn, d_gb, sv["yc"], sv["pba"]],
                                          [sp["alog"], sp["dtb"]], [(3 * DN_WIDTH, BF16), (LANES, BF16)],
                                          [(1, LANES), (1, LANES)], bm=128)
    d_pq, d_conv = _conv_bwd(tag + "_conv_bwd", d_yc, sv["pq"], sp["conv8"])

    def delta_fn(dob, ob):
        prod = dob.astype(F32) * ob.astype(F32)
        return (lax.dot_general(prod, _head_expand(), NT, precision=HI, preferred_element_type=F32),), ()

    (delta,) = _rowwise(tag + "_da_delta", delta_fn, [d_ob, sv["o_b"]], [], [(LANES, F32)])
    grads = [_da_bwd(f"{tag}_da{r}_bwd", sv["pda"], d_ob, sv["lse"], delta, r) for (_, r) in DA_PATTERNS]

    def sum3(*parts):
        q1, k1, v1, q2, k2, v2, q3, k3, v3 = (p.astype(F32) for p in parts)
        return (jnp.concatenate([q1 + q2 + q3, k1 + k2 + k3, v1 + v2 + v3], axis=1),), ()

    (d_pda,) = _rowwise(tag + "_da_sum", sum3, [t for g in grads for t in g], [], [(3 * DA_WIDTH, BF16)])

    a = sv["a"]
    (da,) = _matmul(tag + "_pq_dx", d_pq, w["wq"], tb=True)
    add = lambda acc, prev: (acc + prev,)
    (da,) = _matmul(tag + "_pz_dx", d_pz, w["wz"], tb=True, epi_rows=[da], epi=add)
    (da,) = _matmul(tag + "_pba_dx", d_pba, w["wba"], tb=True, epi_rows=[da], epi=add)
    (da,) = _matmul(tag + "_pda_dx", d_pda, w["wda"], tb=True, epi_rows=[da], epi=add)
    (da,) = _matmul(tag + "_pg_dx", d_pg, w["wg"], tb=True, epi_rows=[da], epi=add)
    (d_wq,) = _matmul(tag + "_pq_dw", a, d_pq, ta=True, outs=(BF16,))
    (d_wz,) = _matmul(tag + "_pz_dw", a, d_pz, ta=True, outs=(BF16,))
    (d_wba,) = _matmul(tag + "_pba_dw", a, d_pba, ta=True, outs=(BF16,))
    (d_wda,) = _matmul(tag + "_pda_dw", a, d_pda, ta=True, outs=(BF16,))
    (d_wg,) = _matmul(tag + "_pg_dw", a, d_pg, ta=True, outs=(BF16,))
    dh_in, d_ln, d_sh, d_sc = _norm_bwd(tag, h_in, da, dh_out, ln, sh, sc)
    wgrads = dict(wq=d_wq, wz=d_wz, wba=d_wba, wda=d_wda, wg=d_wg, w_a=d_wa, w_b=d_wb, w_o=d_wo)
    small = dict(ln=d_ln, sh=d_sh, sc=d_sc, gt=d_gt, dn=d_dn, alog=d_alog, dtb=d_dtb, conv=d_conv)
    return dh_in, wgrads, small


def _loss_head(h, target, fnorm):
    d = h.shape[1]

    def fn(hv, tv, fw):
        def lossf(hh, ww):
            y = hh * lax.rsqrt(jnp.mean(hh * hh, axis=-1, keepdims=True) + NORM_EPS) * ww
            return 0.5 * jnp.sum(jnp.mean(jnp.square(y - tv), axis=-1))

        val, (dh, dw) = jax.value_and_grad(lossf, argnums=(0, 1))(hv, fw)
        return (dh,), (jnp.full((1, LANES), val, F32), dw)

    return _rowwise("loss_head", fn, [h, target], [fnorm], [(d, F32)], [(1, LANES), (1, d)])


def _row(v):
    return v.reshape(1, -1)


def _pad_lanes(v, offset):
    return jnp.pad(v.reshape(1, -1), ((0, 0), (offset, LANES - offset - v.shape[0])))


_UP_SLOTS = dict(ffn1_wg=0, ffn1_wu=1, ffn2_wg=2, ffn2_wu=3)
_DOWN_SLOTS = dict(ffn1_wd=0, ffn2_wd=1)


def _local_step(x2, target, mod, layer_weights, small, on_layer_grads):
    depth = mod.shape[0]
    d = x2.shape[1]
    h = x2
    saved = []
    mods = []
    up = lambda l, nm: _UP_SLOTS[nm]
    down = lambda l, nm: _DOWN_SLOTS[nm]
    for l in range(depth):
        m9 = [_row(mod[l, i * d:(i + 1) * d]) for i in range(N_ADA)]
        sp = dict(conv8=jnp.pad(small["conv_w"][l], ((0, 8 - DN_CONV), (0, 0))),
                  alog=_pad_lanes(small["a_log"][l], DN_HEADS), dtb=_pad_lanes(small["dt_bias"][l], DN_HEADS),
                  dn=_row(small["dn_norm"][l]))
        ga, gb, w = layer_weights(l, h)
        h0 = h
        h1, sv1 = _ffn_fwd(f"l{l}_ffn1", h0, _row(small["ln_ffn1"][l]), m9[0], m9[1], m9[2], ga, up(l, "ffn1_wg"),
                           up(l, "ffn1_wu"), gb, down(l, "ffn1_wd"), 0.5)
        h2, sv2 = _mixer_fwd(f"l{l}_mix", h1, _row(small["ln_mix"][l]), m9[3], m9[4], m9[5], w, sp)
        h3, sv3 = _ffn_fwd(f"l{l}_ffn2", h2, _row(small["ln_ffn2"][l]), m9[6], m9[7], m9[8], ga, up(l, "ffn2_wg"),
                           up(l, "ffn2_wu"), gb, down(l, "ffn2_wd"), 0.5)
        saved.append((h0, h1, h2, sv1, sv2, sv3, sp, ga, gb, w))
        mods.append(m9)
        h = h3
    dh, loss_part, d_fnorm = _loss_head(h, target, _row(small["final_norm"]))
    sgrads, dmods = [], []
    token = None
    for l in reversed(range(depth)):
        h0, h1, h2, sv1, sv2, sv3, sp, ga, gb, w = saved[l]
        m9 = mods[l] if token is None else [r + token for r in mods[l]]
        dh, g3, s3 = _ffn_bwd(f"l{l}_ffn2", h2, dh, sv3, _row(small["ln_ffn2"][l]), m9[6], m9[7], m9[8], ga,
                              up(l, "ffn2_wg"), up(l, "ffn2_wu"), gb, down(l, "ffn2_wd"), 0.5)
        dh, g2, s2 = _mixer_bwd(f"l{l}_mix", h1, dh, sv2, _row(small["ln_mix"][l]), m9[3], m9[4], m9[5], w, sp)
        dh, g1, s1 = _ffn_bwd(f"l{l}_ffn1", h0, dh, sv1, _row(small["ln_ffn1"][l]), m9[0], m9[1], m9[2], ga,
                              up(l, "ffn1_wg"), up(l, "ffn1_wu"), gb, down(l, "ffn1_wd"), 0.5)
        token = on_layer_grads(l, dict(ffn1_wg=g1["wg"], ffn1_wu=g1["wu"], ffn1_wd=g1["wd"], ffn2_wg=g3["wg"],
                                       ffn2_wu=g3["wu"], ffn2_wd=g3["wd"], **g2))
        dmods.append(jnp.concatenate([s1["sh"], s1["sc"], s1["gt"], s2["sh"], s2["sc"], s2["gt"],
                                      s3["sh"], s3["sc"], s3["gt"]], axis=1))
        sgrads.append(dict(ln_ffn1=s1["ln"][0], ln_mix=s2["ln"][0], ln_ffn2=s3["ln"][0],
                           a_log=s2["alog"][0, DN_HEADS:2 * DN_HEADS], dt_bias=s2["dtb"][0, DN_HEADS:2 * DN_HEADS],
                           dn_norm=s2["dn"][0], conv_w=s2["conv"][:DN_CONV]))
    sgrads.reverse()
    dmods.reverse()
    return loss_part[0, 0], dh, jnp.concatenate(dmods, axis=0), sgrads, d_fnorm[0]


def _flip(v, bit):
    return 1 - v if bit else v


def _allgather8(name, x):
    r, c = x.shape

    def body(x_ref, out_ref, send_sems, recv_sems, local_sem):
        mx, my, mc = lax.axis_index("x"), lax.axis_index("y"), lax.axis_index("c")
        me = 4 * mx + 2 * my + mc
        mine = pltpu.make_async_copy(x_ref, out_ref.at[me], local_sem)
        mine.start()
        sends = []
        for k in range(1, 8):
            peer = (_flip(mx, k & 4), _flip(my, k & 2), _flip(mc, k & 1))
            cp = pltpu.make_async_remote_copy(src_ref=x_ref, dst_ref=out_ref.at[me], send_sem=send_sems.at[k - 1],
                                              recv_sem=recv_sems.at[k - 1], device_id=peer, device_id_type=MESH)
            cp.start()
            sends.append(cp)
        for k in range(1, 8):
            peer = (_flip(mx, k & 4), _flip(my, k & 2), _flip(mc, k & 1))
            src = 4 * peer[0] + 2 * peer[1] + peer[2]
            pltpu.make_async_remote_copy(src_ref=x_ref, dst_ref=out_ref.at[src], send_sem=send_sems.at[k - 1],
                                         recv_sem=recv_sems.at[k - 1], device_id=peer, device_id_type=MESH).wait_recv()
        for cp in sends:
            cp.wait_send()
        mine.wait()

    return pl.pallas_call(
        body, name=name, out_shape=jax.ShapeDtypeStruct((8, r, c), x.dtype),
        in_specs=[pl.BlockSpec(memory_space=pltpu.VMEM)], out_specs=pl.BlockSpec(memory_space=pltpu.VMEM),
        scratch_shapes=[pltpu.SemaphoreType.DMA((7,)), pltpu.SemaphoreType.DMA((7,)), pltpu.SemaphoreType.DMA],
        compiler_params=_params(9 * _nbytes((r, c), x.dtype)),
    )(x)


def _chip_peers(mx, my):
    chips = [(1 - mx, my), (mx, 1 - my), (1 - mx, 1 - my)]
    return chips, [2 * cx + cy for (cx, cy) in chips]


_ANY = pl.BlockSpec(memory_space=pl.ANY)


def _row_half(mc, r):
    return pl.ds(pl.multiple_of(mc * (r // 2), 16), r // 2)


def _gather_groups(name, shards):
    ng = len(shards)

    def body(*refs):
        xs, outs = refs[:ng], refs[ng:2 * ng]
        send_sems, recv_sems = refs[2 * ng:]
        mx, my, mc = lax.axis_index("x"), lax.axis_index("y"), lax.axis_index("c")
        j = 2 * mx + my
        chips, idxs = _chip_peers(mx, my)
        sib = (mx, my, 1 - mc)

        def copy(k, src, dst, to):
            return pltpu.make_async_remote_copy(src_ref=src, dst_ref=dst, send_sem=send_sems.at[k],
                                                recv_sem=recv_sems.at[k], device_id=to, device_id_type=MESH)

        first, passed = [], []
        for g in range(ng):
            mine = _row_half(mc, shards[g].shape[1])
            for t, chip in enumerate(chips):
                cp = copy(6 * g + t, xs[g].at[:, mine], outs[g].at[j, :, mine], (*chip, mc))
                cp.start()
                first.append(cp)
        for g in range(ng):
            mine = _row_half(mc, shards[g].shape[1])
            for t, chip in enumerate(chips):
                landed = outs[g].at[idxs[t], :, mine]
                copy(6 * g + t, landed, landed, (*chip, mc)).wait_recv()
                fwd = copy(6 * g + 3 + t, landed, landed, sib)
                fwd.start()
                passed.append(fwd)
        for g in range(ng):
            theirs_half = _row_half(1 - mc, shards[g].shape[1])
            for t in range(3):
                theirs = outs[g].at[idxs[t], :, theirs_half]
                copy(6 * g + 3 + t, theirs, theirs, sib).wait_recv()
        for cp in first + passed:
            cp.wait_send()

    outs = pl.pallas_call(
        body, name=name, out_shape=[jax.ShapeDtypeStruct((4,) + x.shape, x.dtype) for x in shards],
        in_specs=[_ANY] * ng, out_specs=[_ANY] * ng,
        scratch_shapes=[pltpu.SemaphoreType.DMA((6 * ng,)), pltpu.SemaphoreType.DMA((6 * ng,))],
    )(*shards)
    return _place_own_slab(outs, shards)


def _place_own_slab(outs, shards):
    chip = 2 * lax.axis_index("x") + lax.axis_index("y")
    return [lax.dynamic_update_slice(o, x[None], (chip,) + (0,) * x.ndim) for o, x in zip(outs, shards)]


_HBM = pl.BlockSpec(memory_space=pltpu.HBM)
_SEM = pl.BlockSpec(memory_space=pltpu.SEMAPHORE)
_DATAFLOW = pltpu.SideEffectType.DATAFLOW_SIDE_EFFECTING


def _ici_gather_copies(src_refs, land_refs, send_sems, recv_sems, scatter=False):
    mx, my, mc = lax.axis_index("x"), lax.axis_index("y"), lax.axis_index("c")
    j = 2 * mx + my
    chips, idxs = _chip_peers(mx, my)
    sends, recvs = [], []
    for g, src in enumerate(src_refs):
        for t, chip in enumerate(chips):
            common = dict(send_sem=send_sems.at[3 * g + t], recv_sem=recv_sems.at[3 * g + t], device_id=(*chip, mc),
                          device_id_type=MESH)
            if scatter:
                out, to, frm = src.at[idxs[t]], land_refs[g].at[j], land_refs[g].at[idxs[t]]
            else:
                mine = _row_half(mc, src.shape[1])
                out, to, frm = src.at[:, mine], land_refs[g].at[j, :, mine], land_refs[g].at[idxs[t], :, mine]
            sends.append(pltpu.make_async_remote_copy(src_ref=out, dst_ref=to, **common))
            recvs.append(pltpu.make_async_remote_copy(src_ref=out, dst_ref=frm, **common))
    return sends, recvs


def _gather_start(name, shards, scatter=False):
    ng = len(shards)

    def body(*refs):
        srcs, lands = refs[:ng], refs[ng:2 * ng]
        send_sems, recv_sems = refs[2 * ng], refs[2 * ng + 1]
        token = refs[-1]
        sends, _ = _ici_gather_copies(srcs, lands, send_sems, recv_sems, scatter)
        for cp in sends:
            cp.start()
        token[...] = jnp.zeros(token.shape, token.dtype)

    land_shape = lambda x: x.shape if scatter else (4,) + x.shape
    srcs = [pltpu.with_memory_space_constraint(x, pltpu.HBM) for x in shards]
    lands = [pltpu.with_memory_space_constraint(lax.empty(land_shape(x), x.dtype), pltpu.HBM) for x in shards]
    res = pl.pallas_call(
        body, name=name,
        out_shape=(pltpu.SemaphoreType.DMA((3 * ng,)), pltpu.SemaphoreType.DMA((3 * ng,)),
                   *[pltpu.HBM(x.shape, x.dtype) for x in srcs], *[pltpu.HBM(x.shape, x.dtype) for x in lands],
                   jax.ShapeDtypeStruct((8, LANES), F32)),
        in_specs=[_HBM] * (2 * ng),
        out_specs=(_SEM, _SEM, *[_HBM] * (2 * ng), pl.BlockSpec(memory_space=pltpu.VMEM)),
        input_output_aliases={i: 2 + i for i in range(2 * ng)},
        compiler_params=pltpu.CompilerParams(has_side_effects=_DATAFLOW),
    )(*srcs, *lands)
    return dict(send_sems=res[0], recv_sems=res[1], srcs=list(res[2:2 + ng]), lands=list(res[2 + ng:2 + 2 * ng]),
                token=res[-1])


def _gather_wait(name, started, after, scatter=False):
    ng = len(started["srcs"])

    def body(*refs):
        srcs, lands = refs[:ng], refs[ng:2 * ng]
        send_sems, recv_sems = refs[2 * ng], refs[2 * ng + 1]
        sends, recvs = _ici_gather_copies(srcs, lands, send_sems, recv_sems, scatter)
        for cp in sends:
            cp.wait_send()
        for cp in recvs:
            cp.wait_recv()

    res = pl.pallas_call(
        body, name=name,
        out_shape=[pltpu.HBM(x.shape, x.dtype) for x in started["srcs"] + started["lands"]],
        in_specs=[_HBM] * (2 * ng) + [_SEM, _SEM, _ANY], out_specs=[_HBM] * (2 * ng),
        input_output_aliases={i: i for i in range(2 * ng)},
        compiler_params=pltpu.CompilerParams(has_side_effects=_DATAFLOW),
    )(*started["srcs"], *started["lands"], started["send_sems"], started["recv_sems"], after)
    return list(res[:ng]), list(res[ng:])


def _pair_forward_groups(name, lands, shards):
    ng = len(lands)

    def body(*refs):
        ins, outs = refs[:ng], refs[ng:2 * ng]
        send_sems, recv_sems = refs[2 * ng:]
        mx, my, mc = lax.axis_index("x"), lax.axis_index("y"), lax.axis_index("c")
        _, idxs = _chip_peers(mx, my)
        sib = (mx, my, 1 - mc)
        cps = []
        for g in range(ng):
            mine = _row_half(mc, lands[g].shape[2])
            for t in range(3):
                cp = pltpu.make_async_remote_copy(src_ref=ins[g].at[idxs[t], :, mine], dst_ref=outs[g].at[idxs[t], :, mine],
                                                  send_sem=send_sems.at[3 * g + t], recv_sem=recv_sems.at[3 * g + t],
                                                  device_id=sib, device_id_type=MESH)
                cp.start()
                cps.append(cp)
        for g in range(ng):
            theirs = _row_half(1 - mc, lands[g].shape[2])
            for t in range(3):
                pltpu.make_async_remote_copy(src_ref=ins[g].at[idxs[t], :, theirs], dst_ref=outs[g].at[idxs[t], :, theirs],
                                             send_sem=send_sems.at[3 * g + t], recv_sem=recv_sems.at[3 * g + t],
                                             device_id=sib, device_id_type=MESH).wait_recv()
        for cp in cps:
            cp.wait_send()

    outs = pl.pallas_call(
        body, name=name, out_shape=[jax.ShapeDtypeStruct(x.shape, x.dtype) for x in lands],
        in_specs=[_ANY] * ng, out_specs=[_ANY] * ng, input_output_aliases={i: i for i in range(ng)},
        scratch_shapes=[pltpu.SemaphoreType.DMA((3 * ng,)), pltpu.SemaphoreType.DMA((3 * ng,))],
    )(*lands)
    return _place_own_slab(outs, shards)


def _pair_swap_groups(name, gs):
    ng = len(gs)

    def body(*refs):
        xs, outs = refs[:ng], refs[ng:2 * ng]
        send_sems, recv_sems = refs[2 * ng:]
        mx, my, mc = lax.axis_index("x"), lax.axis_index("y"), lax.axis_index("c")
        cps = []
        for g in range(ng):
            cp = pltpu.make_async_remote_copy(src_ref=xs[g].at[:, :, _row_half(1 - mc, gs[g].shape[2])], dst_ref=outs[g],
                                              send_sem=send_sems.at[g], recv_sem=recv_sems.at[g],
                                              device_id=(mx, my, 1 - mc), device_id_type=MESH)
            cp.start()
            cps.append(cp)
        for cp in cps:
            cp.wait()

    return pl.pallas_call(
        body, name=name,
        out_shape=[jax.ShapeDtypeStruct(x.shape[:2] + (x.shape[2] // 2, x.shape[3]), x.dtype) for x in gs],
        in_specs=[_ANY] * ng, out_specs=[_ANY] * ng,
        scratch_shapes=[pltpu.SemaphoreType.DMA((ng,)), pltpu.SemaphoreType.DMA((ng,))],
    )(*gs)


def _chip_scatter_groups(name, ps):
    ng = len(ps)

    def body(*refs):
        xs, outs = refs[:ng], refs[ng:2 * ng]
        send_sems, recv_sems = refs[2 * ng:]
        mx, my, mc = lax.axis_index("x"), lax.axis_index("y"), lax.axis_index("c")
        j = 2 * mx + my
        chips, idxs = _chip_peers(mx, my)
        sends = []
        for g in range(ng):
            for t, chip in enumerate(chips):
                cp = pltpu.make_async_remote_copy(src_ref=xs[g].at[idxs[t]], dst_ref=outs[g].at[j],
                                                  send_sem=send_sems.at[3 * g + t], recv_sem=recv_sems.at[3 * g + t],
                                                  device_id=(*chip, mc), device_id_type=MESH)
                cp.start()
                sends.append(cp)
        for g in range(ng):
            for t, chip in enumerate(chips):
                pltpu.make_async_remote_copy(src_ref=xs[g].at[idxs[t]], dst_ref=outs[g].at[idxs[t]],
                                             send_sem=send_sems.at[3 * g + t], recv_sem=recv_sems.at[3 * g + t],
                                             device_id=(*chip, mc), device_id_type=MESH).wait_recv()
        for cp in sends:
            cp.wait_send()

    outs = pl.pallas_call(
        body, name=name, out_shape=[jax.ShapeDtypeStruct(x.shape, x.dtype) for x in ps],
        in_specs=[_ANY] * ng, out_specs=[_ANY] * ng,
        scratch_shapes=[pltpu.SemaphoreType.DMA((3 * ng,)), pltpu.SemaphoreType.DMA((3 * ng,))],
    )(*ps)
    return _place_own_part(outs, ps)


def _place_own_part(outs, ps):
    chip = 2 * lax.axis_index("x") + lax.axis_index("y")
    return [lax.dynamic_update_slice(o, lax.dynamic_index_in_dim(x, chip, 0, keepdims=True), (chip,) + (0,) * (x.ndim - 1))
            for o, x in zip(outs, ps)]


def _pair_merge_groups(name, fs):
    ng = len(fs)

    def body(*refs):
        xs, outs = refs[:ng], refs[ng:2 * ng]
        send_sems, recv_sems = refs[2 * ng:]
        mx, my, mc = lax.axis_index("x"), lax.axis_index("y"), lax.axis_index("c")
        cps = []
        for g in range(ng):
            mine = _row_half(mc, 2 * fs[g].shape[1])
            cp = pltpu.make_async_remote_copy(src_ref=xs[g], dst_ref=outs[g].at[:, mine], send_sem=send_sems.at[g],
                                              recv_sem=recv_sems.at[g], device_id=(mx, my, 1 - mc), device_id_type=MESH)
            cp.start()
            cps.append(cp)
        for g in range(ng):
            theirs = outs[g].at[:, _row_half(1 - mc, 2 * fs[g].shape[1])]
            pltpu.make_async_remote_copy(src_ref=xs[g], dst_ref=theirs, send_sem=send_sems.at[g],
                                         recv_sem=recv_sems.at[g], device_id=(mx, my, 1 - mc),
                                         device_id_type=MESH).wait_recv()
        for cp in cps:
            cp.wait_send()

    outs = pl.pallas_call(
        body, name=name,
        out_shape=[jax.ShapeDtypeStruct((x.shape[0], 2 * x.shape[1], x.shape[2]), x.dtype) for x in fs],
        in_specs=[_ANY] * ng, out_specs=[_ANY] * ng,
        scratch_shapes=[pltpu.SemaphoreType.DMA((ng,)), pltpu.SemaphoreType.DMA((ng,))],
    )(*fs)
    mc = lax.axis_index("c")
    return [lax.dynamic_update_slice(o, x, (0, mc * x.shape[1], 0)) for o, x in zip(outs, fs)]


def _block_rows(r, w, itemsize=4, budget=4 << 20):
    for c in (r, 2048, 1024, 512, 256, 128, 64, 32, 16):
        if c <= r and r % c == 0 and c * w * itemsize <= budget:
            return c
    return r


def _pair_sum(name, g, got, cidx):
    ns, t, r, w = g.shape
    rh = r // 2
    bm = _block_rows(rh, w)
    nb = rh // bm

    def body(c_ref, a_ref, b_ref, o_ref):
        o_ref[...] = (a_ref[...].astype(F32) + b_ref[...].astype(F32)).astype(o_ref.dtype)

    blk = (None, None, bm, w)
    return pl.pallas_call(
        body, name=name,
        grid_spec=pltpu.PrefetchScalarGridSpec(
            num_scalar_prefetch=1, grid=(ns, t, nb),
            in_specs=[pl.BlockSpec(blk, lambda s, tt, i, c: (s, tt, c[0] * nb + i, 0)),
                      pl.BlockSpec(blk, lambda s, tt, i, c: (s, tt, i, 0))],
            out_specs=pl.BlockSpec(blk, lambda s, tt, i, c: (s, tt, i, 0))),
        out_shape=jax.ShapeDtypeStruct((ns, t, rh, w), BF16),
        compiler_params=_params(3 * _nbytes((bm, w), F32)),
    )(cidx, g, got)


def _chip_sum(name, p):
    ns, th, r, w = p.shape
    bm = _block_rows(r, w, budget=2 << 20)

    def body(p_ref, o_ref):
        acc = p_ref[0].astype(F32)
        for s in range(1, ns):
            acc = acc + p_ref[s].astype(F32)
        o_ref[...] = acc

    return pl.pallas_call(
        body, name=name, grid=(th, r // bm),
        in_specs=[pl.BlockSpec((ns, None, bm, w), lambda tt, i: (0, tt, i, 0))],
        out_specs=pl.BlockSpec((None, bm, w), lambda tt, i: (tt, i, 0)),
        out_shape=jax.ShapeDtypeStruct((th, r, w), F32),
        compiler_params=_params(ns * _nbytes((bm, w), BF16) + 2 * _nbytes((bm, w), F32)),
    )(p)


def _sum_leading(name, x):
    n = x.shape[0]

    def body(p_ref, o_ref):
        acc = p_ref[0]
        for s in range(1, n):
            acc = acc + p_ref[s]
        o_ref[...] = acc

    return pl.pallas_call(body, name=name, out_shape=jax.ShapeDtypeStruct(x.shape[1:], F32),
                          compiler_params=_params(2 * _nbytes(x.shape, F32)))(x)


def _reduce_scatter_begin(tag, gs, overlap):
    cidx = lax.axis_index("c").astype(jnp.int32).reshape(1)
    got = _pair_swap_groups(tag + "_pair_swap", gs)
    pair = [_pair_sum(f"{tag}_pair_sum{i}", g, r_, cidx) for i, (g, r_) in enumerate(zip(gs, got))]
    if overlap:
        return _gather_start(tag + "_scatter_start", pair, scatter=True)
    return _chip_scatter_groups(tag + "_chip_scatter", pair)


def _reduce_scatter_end(tag, state, overlap, after):
    if overlap:
        srcs, lands = _gather_wait(tag + "_scatter_wait", state, after, scatter=True)
        state = _place_own_part(lands, srcs)
    fin = [_chip_sum(f"{tag}_chip_sum{i}", p) for i, p in enumerate(state)]
    return _pair_merge_groups(tag + "_pair_merge", fin)


_GROUPS = ((("ffn1_wg", "ffn1_wu", "ffn2_wg", "ffn2_wu"), 1), (("ffn1_wd", "ffn2_wd"), 0), (("w_a",), 0),
           (("w_o",), 0), (("w_in",), 1), (("w_b",), 1))


def _shard_major(g, ax):
    k, n = g.shape
    if ax == 0:
        return g.reshape(4, k // 4, n)
    return g.reshape(k, 4, n // 4).transpose(1, 0, 2)


def _in_cols(d):
    o1 = 3 * DN_WIDTH
    o2 = o1 + DN_WIDTH
    o3 = o2 + 2 * DN_HEADS
    o4 = o3 + 3 * DA_WIDTH
    return dict(wq=(0, o1), wz=(o1, o2), wba=(o2, o3), wda=(o3, o4), wg=(o4, o4 + 2 * d))


def _mixer_weights(w_in, w_a, w_b, w_o, d):
    w = {k: w_in[:, a:b] for k, (a, b) in _in_cols(d).items()}
    w["wba"] = jnp.pad(w["wba"], ((0, 0), (0, LANES - 2 * DN_HEADS)))
    w["w_a"], w["w_b"], w["w_o"] = w_a, w_b, w_o
    return w


def _w_in_grad(wg):
    return jnp.concatenate([wg["wq"], wg["wz"], wg["wba"][:, :2 * DN_HEADS], wg["wda"], wg["wg"]], axis=1)


def _adam_math(wv, gv, mv, vv):
    mn = ADAM_B1 * mv + (1.0 - ADAM_B1) * gv
    vn = ADAM_B2 * vv + (1.0 - ADAM_B2) * jnp.square(gv)
    m_hat = mn / (1.0 - ADAM_B1 ** ADAM_STEP)
    v_hat = vn / (1.0 - ADAM_B2 ** ADAM_STEP)
    delta = -ADAM_LR * (m_hat / (jnp.sqrt(v_hat) + ADAM_EPS) + ADAM_WD * wv)
    return delta, mn, vn


def _adamw(name, w, g, m, v):
    shape = w.shape
    cols = shape[-1]
    w2, g2, m2, v2 = (t.reshape(-1, cols) for t in (w, g, m, v))
    rows = w2.shape[0]
    bm = _pick(rows, (256, 128, 64, 32, 16, 8)) if rows >= 8 else rows
    delta, mn, vn = _rowwise(name, lambda *t: (_adam_math(*t), ()), [w2, g2, m2, v2], [], [(cols, F32)] * 3, bm=bm)
    return delta.reshape(shape), mn.reshape(shape), vn.reshape(shape)


def _adamw_leading(name, w, g, m, v):
    n = w.shape[0]
    padded_row = -(-w.shape[1] // 8) * 8 * w.shape[2] * 4
    bm = max(c for c in range(1, n + 1) if n % c == 0 and (c * padded_row <= (1 << 20) or c == 1))

    def body(w_ref, g_ref, m_ref, v_ref, d_ref, mo_ref, vo_ref):
        d_ref[...], mo_ref[...], vo_ref[...] = _adam_math(w_ref[...], g_ref[...], m_ref[...], v_ref[...])

    spec = pl.BlockSpec((bm,) + w.shape[1:], lambda i: (i, 0, 0))
    return pl.pallas_call(
        body, name=name, grid=(n // bm,), in_specs=[spec] * 4, out_specs=[spec] * 3,
        out_shape=[jax.ShapeDtypeStruct(w.shape, F32)] * 3, compiler_params=_params(7 * bm * padded_row),
    )(w, g, m, v)


def _adamw_stacked(name, w, m, v, gstacks, slot):
    depth, r, cdim = w.shape
    bm = _block_rows(r, cdim, budget=1 << 20)

    def body(w_ref, m_ref, v_ref, *rest):
        g_refs, (go_ref, d_ref, mo_ref, vo_ref) = rest[:depth], rest[depth:]
        layer = pl.program_id(0)
        gv = g_refs[0][...]
        for l in range(1, depth):
            gv = jnp.where(layer == l, g_refs[l][...], gv)
        go_ref[...] = gv
        d_ref[...], mo_ref[...], vo_ref[...] = _adam_math(w_ref[...], gv, m_ref[...], v_ref[...])

    nat = pl.BlockSpec((None, bm, cdim), lambda l, i: (l, i, 0))
    return pl.pallas_call(
        body, name=name, grid=(depth, r // bm),
        in_specs=[nat, nat, nat] + [pl.BlockSpec((None, bm, cdim), lambda l, i: (slot, i, 0))] * depth,
        out_specs=[nat] * 4, out_shape=[jax.ShapeDtypeStruct(w.shape, F32)] * 4,
        compiler_params=_params((7 + depth) * _nbytes((bm, cdim), F32)),
    )(w, m, v, *gstacks)


def kernel(x, c, ada_w, ada_b, ln_ffn1, ln_mix, ln_ffn2, ffn1_wg, ffn1_wu, ffn1_wd, w_in, conv_w, a_log, dt_bias, dn_norm, w_a, w_b, w_o, ffn2_wg, ffn2_wu, ffn2_wd, final_norm, loss_target, m_ada_w, m_ada_b, m_ln_ffn1, m_ln_mix, m_ln_ffn2, m_ffn1_wg, m_ffn1_wu, m_ffn1_wd, m_w_in, m_conv_w, m_a_log, m_dt_bias, m_dn_norm, m_w_a, m_w_b, m_w_o, m_ffn2_wg, m_ffn2_wu, m_ffn2_wd, m_final_norm, v_ada_w, v_ada_b, v_ln_ffn1, v_ln_mix, v_ln_ffn2, v_ffn1_wg, v_ffn1_wu, v_ffn1_wd, v_w_in, v_conv_w, v_a_log, v_dt_bias, v_dn_norm, v_w_a, v_w_b, v_w_o, v_ffn2_wg, v_ffn2_wu, v_ffn2_wd, v_final_norm):
    names = ["ada_w", "ada_b", "ln_ffn1", "ln_mix", "ln_ffn2", "ffn1_wg", "ffn1_wu", "ffn1_wd", "w_in", "conv_w",
             "a_log", "dt_bias", "dn_norm", "w_a", "w_b", "w_o", "ffn2_wg", "ffn2_wu", "ffn2_wd", "final_norm"]
    wts = dict(zip(names, (ada_w, ada_b, ln_ffn1, ln_mix, ln_ffn2, ffn1_wg, ffn1_wu, ffn1_wd, w_in, conv_w, a_log,
                           dt_bias, dn_norm, w_a, w_b, w_o, ffn2_wg, ffn2_wu, ffn2_wd, final_norm)))
    mom = dict(zip(names, (m_ada_w, m_ada_b, m_ln_ffn1, m_ln_mix, m_ln_ffn2, m_ffn1_wg, m_ffn1_wu, m_ffn1_wd, m_w_in,
                           m_conv_w, m_a_log, m_dt_bias, m_dn_norm, m_w_a, m_w_b, m_w_o, m_ffn2_wg, m_ffn2_wu,
                           m_ffn2_wd, m_final_norm)))
    var = dict(zip(names, (v_ada_w, v_ada_b, v_ln_ffn1, v_ln_mix, v_ln_ffn2, v_ffn1_wg, v_ffn1_wu, v_ffn1_wd, v_w_in,
                           v_conv_w, v_a_log, v_dt_bias, v_dn_norm, v_w_a, v_w_b, v_w_o, v_ffn2_wg, v_ffn2_wu,
                           v_ffn2_wd, v_final_norm)))
    _, s, d = x.shape
    depth = ada_w.shape[0]
    mx, my, mc = lax.axis_index("x"), lax.axis_index("y"), lax.axis_index("c")
    chip = 2 * mx + my
    me = 2 * chip + mc
    nshard = ada_w.shape[2]

    cact = _rowwise("c_silu", lambda cv: ((_silu(cv),), ()), [jnp.pad(c, ((0, 7), (0, 0)))], [], [(d, F32)], bm=8)[0]
    c_all = _allgather8("ag_c", cact)[:, 0, :]
    conv_all = _allgather8("ag_conv", jnp.pad(conv_w.reshape(depth * DN_CONV, -1), ((0, 8 - depth * DN_CONV), (0, 0))))
    conv_full = jnp.concatenate([conv_all[2 * j, :depth * DN_CONV] for j in range(4)], axis=1)
    conv_full = conv_full.reshape(depth, DN_CONV, 3 * DN_WIDTH)
    layer_shards = [[jnp.stack([wts[nm][l].astype(BF16) for nm in nms], axis=0) for nms, _ in _GROUPS]
                    for l in range(depth)]
    gathered0 = _gather_groups("ag_weights0", layer_shards[0])
    rows_of = lambda st: st[:, 0].reshape(-1, st.shape[-1])
    cols_of = lambda st: jnp.concatenate([st[j, 0] for j in range(4)], axis=1)

    def layer_weights(l, after):
        if l == 0:
            got = gathered0
        else:
            srcs, lands = _gather_wait(f"ag_weights{l}_wait", started[l], after)
            got = _pair_forward_groups(f"ag_weights{l}_pair", lands, srcs)
        ga, gb, g_wa, g_wo, g_win, g_wb = got
        return ga, gb, _mixer_weights(cols_of(g_win), rows_of(g_wa), cols_of(g_wb), rows_of(g_wo), d)

    c16 = jnp.pad(c_all, ((0, 8), (0, 0))).astype(BF16)
    parts = []
    for l in range(depth):
        bias = lax.dynamic_slice(ada_b[l], (chip * nshard,), (nshard,)).reshape(1, nshard)
        (mp,) = _matmul(f"ada_fwd{l}", c16, ada_w[l].astype(BF16), epi_bcast=[bias], epi=lambda acc, b: (acc + b,))
        parts.append(mp)
    mod_all = _allgather8("ag_mod", jnp.concatenate(parts, axis=0))
    mod_rows = jnp.concatenate([mod_all[2 * j] for j in range(4)], axis=1)
    mod = jnp.stack([lax.dynamic_index_in_dim(mod_rows, l * 16 + me, axis=0, keepdims=False) for l in range(depth)])

    gathered0, later, mod, conv_full = lax.optimization_barrier((gathered0, layer_shards[1:], mod, conv_full))
    started = {l: _gather_start(f"ag_weights{l}_start", later[l - 1]) for l in range(1, depth)}
    for st in started.values():
        mod = mod + st["token"][0, 0]
    small = dict(conv_w=conv_full, a_log=a_log, dt_bias=dt_bias, dn_norm=dn_norm, ln_ffn1=ln_ffn1, ln_mix=ln_mix,
                 ln_ffn2=ln_ffn2, final_norm=final_norm)
    ffn_names = _GROUPS[0][0] + _GROUPS[1][0]
    rs_state, first_layer = {}, {}

    def on_layer_grads(l, wg):
        wg["w_in"] = _w_in_grad(wg)
        gs = [jnp.stack([wg[nm] if nm in ffn_names else _shard_major(wg[nm], ax) for nm in nms], axis=1)
              for nms, ax in _GROUPS]
        if l == 0:
            first_layer["gs"] = gs
            return None
        rs_state[l] = _reduce_scatter_begin(f"rs{l}", gs, overlap=True)
        return rs_state[l]["token"][0, 0]

    loss_part, dx, dmod, sgrads, d_fnorm = _local_step(x[0], loss_target[0], mod, layer_weights, small,
                                                       on_layer_grads)

    dmod_all = _allgather8("ag_dmod", jnp.pad(dmod, ((0, 8 - depth), (0, 0))))
    smalls = [loss_part.reshape(1), d_fnorm]
    for l in range(depth):
        sg = sgrads[l]
        smalls += [sg["ln_ffn1"], sg["ln_mix"], sg["ln_ffn2"], sg["a_log"], sg["dt_bias"], sg["dn_norm"],
                   sg["conv_w"].reshape(-1)]
    sizes = [t.shape[0] for t in smalls]
    tile = 8 * LANES
    flat = jnp.concatenate([jnp.pad(t, (0, (-t.shape[0]) % tile)).reshape(-1, LANES) for t in smalls], axis=0)
    small_all = _allgather8("ag_small", flat)
    dmod_all, small_all, gs0 = lax.optimization_barrier((dmod_all, small_all, first_layer["gs"]))
    rs_state[0] = _reduce_scatter_begin("rs0", gs0, overlap=True)
    started0 = rs_state[0]["token"][0, 0]
    dmod_all = dmod_all + started0
    small_all = small_all + started0

    g_ada_w, g_ada_b = [], []
    for l in range(depth):
        dm_l = dmod_all[:, l, :]
        (gb_l,) = _rowwise(f"ada_b_grad{l}", lambda v: ((), (jnp.sum(v, axis=0, keepdims=True),)), [dm_l], [], [],
                           [(1, N_ADA * d)], bm=8)
        g_ada_b.append(gb_l[0])
        dm_sh = lax.dynamic_slice(dm_l, (0, chip * nshard), (8, nshard))
        (gw_l,) = _matmul(f"ada_w_grad{l}", c16, jnp.pad(dm_sh, ((0, 8), (0, 0))).astype(BF16), ta=True)
        g_ada_w.append(gw_l)
    grads = dict(ada_w=jnp.stack(g_ada_w), ada_b=jnp.stack(g_ada_b))

    tot = _sum_leading("small_sum", small_all)
    offs, acc = [], 0
    for n_ in sizes:
        offs.append(acc)
        acc += -(-n_ // tile) * 8
    take = lambda i: tot[offs[i]:offs[i] + -(-sizes[i] // tile) * 8].reshape(-1)[:sizes[i]]
    loss = take(0)[0]
    grads["final_norm"] = take(1)
    per = 7
    for key_i, key in enumerate(["ln_ffn1", "ln_mix", "ln_ffn2", "a_log", "dt_bias", "dn_norm"]):
        grads[key] = jnp.stack([take(2 + per * l + key_i) for l in range(depth)])
    conv_g = jnp.stack([take(2 + per * l + 6).reshape(DN_CONV, 3 * DN_WIDTH) for l in range(depth)])
    csh = conv_w.shape[2]
    grads["conv_w"] = lax.dynamic_slice(conv_g, (0, 0, chip * csh), (depth, DN_CONV, csh))

    deltas, new_m, new_v = {}, {}, {}
    big = {nm for nms, _ in _GROUPS for nm in nms}
    for name in names:
        if name in big:
            continue
        wv, gv, mv, vv = wts[name], grads[name], mom[name], var[name]
        if wv.ndim == 1:
            wv, gv, mv, vv = (t.reshape(-1, LANES) for t in (wv, gv, mv, vv))
        dl, mn, vn = _adamw("adamw_" + name, wv, gv, mv, vv)
        deltas[name], new_m[name], new_v[name] = (t.reshape(wts[name].shape) for t in (dl, mn, vn))

    reduced = {l: _reduce_scatter_end(f"rs{l}", rs_state[l], True, dx) for l in range(depth - 1, 0, -1)}
    reduced[0] = _reduce_scatter_end("rs0", rs_state[0], True, deltas["ada_w"])

    for gi, (nms, ax) in enumerate(_GROUPS):
        for q, nm in enumerate(nms):
            wv, mv, vv = wts[nm], mom[nm], var[nm]
            per_layer = [reduced[l][gi][q] for l in range(depth)]
            if ax == 1 and wv.shape[2] % LANES and nm != "w_in":
                tr = lambda t: jnp.swapaxes(t, 1, 2)
                gt = jnp.stack([g.T for g in per_layer], axis=0)
                dl, mn, vn = _adamw("adamw_" + nm, tr(wv), gt, tr(mv), tr(vv))
                grads[nm], deltas[nm], new_m[nm], new_v[nm] = tr(gt), tr(dl), tr(mn), tr(vn)
            elif nm == "w_in" and wv.shape[2] % LANES:
                tr = lambda t: jnp.transpose(t, (2, 0, 1))
                back = lambda t: jnp.transpose(t, (1, 2, 0))
                gt = jnp.stack([g.T for g in per_layer], axis=1)
                dl, mn, vn = _adamw_leading("adamw_" + nm, tr(wv), gt, tr(mv), tr(vv))
                grads[nm], deltas[nm], new_m[nm], new_v[nm] = back(gt), back(dl), back(mn), back(vn)
            else:
                grads[nm], deltas[nm], new_m[nm], new_v[nm] = _adamw_stacked(
                    "adamw_" + nm, wv, mv, vv, [reduced[l][gi] for l in range(depth)], q)

    return (loss, dx.reshape(1, s, d), *[grads[n_] for n_ in names], *[deltas[n_] for n_ in names],
            *[new_m[n_] for n_ in names], *[new_v[n_] for n_ in names])
```

```python
import functools

import jax
import jax.numpy as jnp
from jax import lax
from jax.experimental import pallas as pl
from jax.experimental.pallas import tpu as pltpu

F32 = jnp.float32
BF16 = jnp.bfloat16
MESH = pl.DeviceIdType.MESH

NORM_EPS = 1e-6
DN_HEADS, DN_DIM, DN_CHUNK, DN_CONV = 8, 128, 64, 4
DN_WIDTH = DN_HEADS * DN_DIM
DA_HEADS, DA_DIM, DA_BLOCK = 12, 64, 128
DA_WIDTH = DA_HEADS * DA_DIM
DA_PATTERNS = ((128, 1), (512, 4), (2048, 16))
ALIBI_MAX_EXP = 8.0
N_ADA = 9
LANES = 128
V7X_VMEM_BYTES = 64 << 20
ADAM_LR, ADAM_B1, ADAM_B2, ADAM_EPS, ADAM_WD, ADAM_STEP = 0.001, 0.9, 0.999, 1e-08, 0.01, 10
NEG = -1e30
HI = lax.Precision.HIGHEST
NN = (((1,), (0,)), ((), ()))
NT = (((1,), (1,)), ((), ()))
TN = (((0,), (0,)), ((), ()))


def _nbytes(shape, dtype):
    n = 1
    for s in shape:
        n *= s
    return n * jnp.dtype(dtype).itemsize


def _params(block_bytes, scratch_bytes=0):
    need = 2 * block_bytes + scratch_bytes
    lim = min(max(need + need // 4 + (4 << 20), 32 << 20), V7X_VMEM_BYTES - (6 << 20))
    return pltpu.CompilerParams(vmem_limit_bytes=int(lim))


def _pick(n, cands):
    for c in cands:
        if c <= n and n % c == 0:
            return c
    return n


def _sigmoid(x):
    return jax.nn.sigmoid(x)


def _silu(x):
    return x * jax.nn.sigmoid(x)


def _softplus(x):
    return jnp.maximum(x, 0.0) + jnp.log(1.0 + jnp.exp(-jnp.abs(x)))


def _rowwise(name, fn, rows, bcast, row_outs, red_outs=(), bm=512):
    rows = [r if isinstance(r, tuple) else (r, r.shape[1], 0) for r in rows]
    s = rows[0][0].shape[0]
    bm = _pick(s, (bm, 128, 64, 32, 16, 8))
    nr, nb, no, nd = len(rows), len(bcast), len(row_outs), len(red_outs)
    in_specs = [pl.BlockSpec((bm, w), functools.partial(lambda i, ci: (i, ci), ci=ci)) for (_, w, ci) in rows]
    in_specs += [pl.BlockSpec(b.shape, lambda i: (0, 0)) for b in bcast]
    out_shape = [jax.ShapeDtypeStruct((s, w), dt) for (w, dt) in row_outs]
    out_shape += [jax.ShapeDtypeStruct((r, w), F32) for (r, w) in red_outs]
    out_specs = [pl.BlockSpec((bm, w), lambda i: (i, 0)) for (w, _) in row_outs]
    out_specs += [pl.BlockSpec((r, w), lambda i: (0, 0)) for (r, w) in red_outs]

    def body(*refs):
        ins = [r[...] for r in refs[:nr + nb]]
        outs = refs[nr + nb:nr + nb + no]
        reds = refs[nr + nb + no:]
        ov, rv = fn(*ins)
        for o, v in zip(outs, ov):
            o[...] = v.astype(o.dtype)
        if nd:
            @pl.when(pl.program_id(0) == 0)
            def _():
                for r in reds:
                    r[...] = jnp.zeros(r.shape, F32)
            for r, v in zip(reds, rv):
                r[...] += v.astype(F32)

    blk = sum(_nbytes((bm, w), a.dtype) for (a, w, _) in rows) + sum(_nbytes(b.shape, b.dtype) for b in bcast)
    blk += sum(_nbytes((bm, w), dt) for (w, dt) in row_outs) + sum(_nbytes(r, F32) for r in red_outs)
    res = pl.pallas_call(
        body, name=name, grid=(s // bm,), in_specs=in_specs, out_specs=out_specs, out_shape=out_shape,
        compiler_params=_params(3 * blk),
    )(*[a for (a, _, _) in rows], *bcast)
    return res


def _matmul(name, a, b, *, ta=False, tb=False, outs=(F32,), epi=None, epi_rows=(), epi_bcast=(),
            bm=None, bn=None, bk=None):
    if ta:
        k, m = a.shape
    else:
        m, k = a.shape
    n = b.shape[0] if tb else b.shape[1]
    assert (b.shape[1] if tb else b.shape[0]) == k, (name, a.shape, b.shape)
    if bm is None:
        bm = _pick(m, (1024, 1408, 768, 512, 384, 256, 128)) if ta else _pick(m, (1024, 512, 256, 128, 64, 32, 16))
    if bk is None:
        bk = k if k <= 3072 else _pick(k, (2816, 2048, 1024, 512))
        if ta:
            bk = _pick(k, (4096, 2048, 1024, 512, 256, 128, 64, 32, 16))
    if bn is None:
        bn = _pick(n, (1024, 768, 512, 384, 256, 128) if bk <= 2048 else (512, 384, 256, 128))
    nk = k // bk
    dims = TN if ta else (NT if tb else NN)
    a_spec = pl.BlockSpec((bk, bm), lambda i, j, kk: (kk, i)) if ta else pl.BlockSpec((bm, bk), lambda i, j, kk: (i, kk))
    b_spec = pl.BlockSpec((bn, bk), lambda i, j, kk: (j, kk)) if tb else pl.BlockSpec((bk, bn), lambda i, j, kk: (kk, j))
    in_specs = [a_spec, b_spec]
    in_specs += [pl.BlockSpec((bm, bn), lambda i, j, kk: (i, j)) for _ in epi_rows]
    in_specs += [pl.BlockSpec((1, bn), lambda i, j, kk: (0, j)) for _ in epi_bcast]
    out_shape = [jax.ShapeDtypeStruct((m, n), dt) for dt in outs]
    out_specs = [pl.BlockSpec((bm, bn), lambda i, j, kk: (i, j)) for _ in outs]
    ner, neb, no = len(epi_rows), len(epi_bcast), len(outs)

    def body(*refs):
        a_ref, b_ref = refs[0], refs[1]
        extra = refs[2:2 + ner + neb]
        out_refs = refs[2 + ner + neb:2 + ner + neb + no]
        prod = lax.dot_general(a_ref[...], b_ref[...], dims, preferred_element_type=F32)

        def finish(acc):
            vals = epi(acc, *[r[...] for r in extra]) if epi is not None else (acc,)
            for o, v in zip(out_refs, vals):
                o[...] = v.astype(o.dtype)

        if nk == 1:
            finish(prod)
        else:
            acc_ref = refs[-1]
            kk = pl.program_id(2)

            @pl.when(kk == 0)
            def _():
                acc_ref[...] = prod

            @pl.when(kk > 0)
            def _():
                acc_ref[...] += prod

            @pl.when(kk == nk - 1)
            def _():
                finish(acc_ref[...])

    blk = _nbytes((bm, bk), a.dtype) + _nbytes((bk, bn), b.dtype)
    blk += sum(_nbytes((bm, bn), r.dtype) for r in epi_rows) + sum(_nbytes((bm, bn), dt) for dt in outs)
    scratch = [pltpu.VMEM((bm, bn), F32)] if nk > 1 else []
    res = pl.pallas_call(
        body, name=name, grid=(m // bm, n // bn, nk), in_specs=in_specs, out_specs=out_specs,
        out_shape=out_shape, scratch_shapes=scratch,
        compiler_params=_params(blk, 3 * _nbytes((bm, bn), F32)),
    )(a, b, *epi_rows, *epi_bcast)
    return res


def _mm_core(name, grid, nk, pairs, out_defs, acc_shape, epi=None, epi_ins=()):
    npair, nep, no = len(pairs), len(epi_ins), len(out_defs)

    def body(*refs):
        extra = refs[2 * npair:2 * npair + nep]
        out_refs = refs[2 * npair + nep:2 * npair + nep + no]
        prod = None
        for p in range(npair):
            d = lax.dot_general(refs[2 * p][...], refs[2 * p + 1][...], pairs[p][4], preferred_element_type=F32)
            prod = d if prod is None else prod + d

        def finish(acc):
            vals = epi(acc, *[r[...] for r in extra]) if epi is not None else (acc,)
            for o, v in zip(out_refs, vals):
                o[...] = v.astype(o.dtype)

        if nk == 1:
            finish(prod)
        else:
            acc_ref = refs[-1]
            kk = pl.program_id(2)

            @pl.when(kk == 0)
            def _():
                acc_ref[...] = prod

            @pl.when(kk > 0)
            def _():
                acc_ref[...] += prod

            @pl.when(kk == nk - 1)
            def _():
                finish(acc_ref[...])

    def blk_bytes(spec, dtype):
        return _nbytes([s for s in spec.block_shape if s is not None], dtype)

    blk = sum(blk_bytes(sa, a.dtype) + blk_bytes(sb, b.dtype) for (a, sa, b, sb, _) in pairs)
    blk += sum(blk_bytes(sp, arr.dtype) for (arr, sp) in epi_ins) + sum(blk_bytes(sp, dt) for (_, dt, sp) in out_defs)
    ins, in_specs = [], []
    for (a, sa, b, sb, _) in pairs:
        ins += [a, b]
        in_specs += [sa, sb]
    ins += [arr for (arr, _) in epi_ins]
    in_specs += [sp for (_, sp) in epi_ins]
    return pl.pallas_call(
        body, name=name, grid=grid, in_specs=in_specs, out_specs=[sp for (_, _, sp) in out_defs],
        out_shape=[jax.ShapeDtypeStruct(sh, dt) for (sh, dt, _) in out_defs],
        scratch_shapes=[pltpu.VMEM(acc_shape, F32)] if nk > 1 else [],
        compiler_params=_params(blk, 3 * _nbytes(acc_shape, F32)),
    )(*ins)


def _rms_mod(h, ln, sh, sc):
    n = h * lax.rsqrt(jnp.mean(h * h, axis=-1, keepdims=True) + NORM_EPS) * ln
    return n * (1.0 + sc) + sh


def _swiglu_act(g, u):
    return _silu(g.astype(F32)) * u.astype(F32)


def _dn_prep(yc, pba, alog, dtb):
    act = _silu(yc)
    parts = []
    for idx in range(2 * DN_HEADS):
        seg = act[:, idx * DN_DIM:(idx + 1) * DN_DIM]
        seg = seg * lax.rsqrt(jnp.sum(seg * seg, axis=-1, keepdims=True) + NORM_EPS)
        if idx < DN_HEADS:
            seg = seg * (DN_DIM ** -0.5)
        parts.append(seg)
    parts.append(act[:, 2 * DN_WIDTH:])
    qkvn = jnp.concatenate(parts, axis=1)
    lane = lax.broadcasted_iota(jnp.int32, pba.shape, 1)
    beta = _sigmoid(pba)
    g = -jnp.exp(alog) * _softplus(pba + dtb)
    gb = jnp.where(lane < DN_HEADS, beta, jnp.where(lane < 2 * DN_HEADS, g, 0.0))
    return qkvn, gb


def _dn_outnorm(o_a, z, dn):
    parts = []
    for h in range(DN_HEADS):
        seg = o_a[:, h * DN_DIM:(h + 1) * DN_DIM]
        seg = seg * lax.rsqrt(jnp.mean(seg * seg, axis=-1, keepdims=True) + NORM_EPS) * dn
        parts.append(seg)
    return jnp.concatenate(parts, axis=1) * _silu(z)


def _shift_down(x, halo8, s):
    r = pltpu.roll(x, s, axis=0)
    top = pltpu.roll(halo8, s, axis=0)
    i8 = lax.broadcasted_iota(jnp.int32, top.shape, 0)
    return jnp.concatenate([jnp.where(i8 < s, top, r[0:8]), r[8:]], axis=0)


def _shift_up(x, halo8, s):
    m = x.shape[0]
    r = pltpu.roll(x, m - s, axis=0)
    bot = pltpu.roll(halo8, 8 - s, axis=0)
    i8 = lax.broadcasted_iota(jnp.int32, bot.shape, 0)
    return jnp.concatenate([r[:m - 8], jnp.where(i8 >= 8 - s, bot, r[m - 8:])], axis=0)


def _conv_prep_fwd(name, pq, convw8, pba, alog, dtb, bm=256):
    s, w = pq.shape
    nblk = s // bm
    hb = bm // 16

    def body(x_ref, halo_ref, w_ref, pba_ref, alog_ref, dtb_ref, yc_ref, qkv_ref, gb_ref):
        i = pl.program_id(0)
        x = x_ref[...].astype(F32)
        halo = jnp.where(i > 0, halo_ref[...].astype(F32)[8:16], 0.0)
        cw = w_ref[...]
        y = x * cw[DN_CONV - 1:DN_CONV]
        for sft in range(1, DN_CONV):
            y = y + _shift_down(x, halo, sft) * cw[DN_CONV - 1 - sft:DN_CONV - sft]
        ycb = y.astype(BF16)
        yc_ref[...] = ycb
        qkvn, gb = _dn_prep(ycb.astype(F32), pba_ref[...], alog_ref[...], dtb_ref[...])
        qkv_ref[...] = qkvn.astype(BF16)
        gb_ref[...] = gb

    blk = 3 * _nbytes((bm, w), BF16) + 4 * _nbytes((bm, w), F32)
    return pl.pallas_call(
        body, name=name, grid=(nblk,),
        in_specs=[pl.BlockSpec((bm, w), lambda i: (i, 0)),
                  pl.BlockSpec((16, w), lambda i: (jnp.maximum(i * hb - 1, 0), 0)),
                  pl.BlockSpec(convw8.shape, lambda i: (0, 0)),
                  pl.BlockSpec((bm, LANES), lambda i: (i, 0)),
                  pl.BlockSpec((1, LANES), lambda i: (0, 0)),
                  pl.BlockSpec((1, LANES), lambda i: (0, 0))],
        out_specs=[pl.BlockSpec((bm, w), lambda i: (i, 0)), pl.BlockSpec((bm, w), lambda i: (i, 0)),
                   pl.BlockSpec((bm, LANES), lambda i: (i, 0))],
        out_shape=[jax.ShapeDtypeStruct((s, w), BF16), jax.ShapeDtypeStruct((s, w), BF16),
                   jax.ShapeDtypeStruct((s, LANES), F32)],
        compiler_params=_params(blk),
    )(pq, pq, convw8, pba, alog, dtb)


def _conv_bwd(name, dyc, pq, convw8, bm=256):
    s, w = pq.shape
    nblk = s // bm
    hb = bm // 16

    def body(dy_ref, dyn_ref, x_ref, xh_ref, w_ref, dx_ref, dw_ref):
        i = pl.program_id(0)
        dy = dy_ref[...].astype(F32)
        nxt = jnp.where(i < nblk - 1, dyn_ref[...].astype(F32)[0:8], 0.0)
        x = x_ref[...].astype(F32)
        halo = jnp.where(i > 0, xh_ref[...].astype(F32)[8:16], 0.0)
        cw = w_ref[...]
        dx = dy * cw[DN_CONV - 1:DN_CONV]
        for sft in range(1, DN_CONV):
            dx = dx + _shift_up(dy, nxt, sft) * cw[DN_CONV - 1 - sft:DN_CONV - sft]
        dx_ref[...] = dx.astype(dx_ref.dtype)
        r8 = lax.broadcasted_iota(jnp.int32, (8, w), 0)
        dw = jnp.zeros((8, w), F32)
        for j in range(DN_CONV):
            sft = DN_CONV - 1 - j
            xs = x if sft == 0 else _shift_down(x, halo, sft)
            dw = dw + jnp.where(r8 == j, jnp.sum(dy * xs, axis=0, keepdims=True), 0.0)

        @pl.when(i == 0)
        def _():
            dw_ref[...] = jnp.zeros((8, w), F32)
        dw_ref[...] += dw

    blk = 4 * _nbytes((bm, w), BF16) + 5 * _nbytes((bm, w), F32)
    return pl.pallas_call(
        body, name=name, grid=(nblk,),
        in_specs=[pl.BlockSpec((bm, w), lambda i: (i, 0)),
                  pl.BlockSpec((16, w), lambda i: (jnp.minimum((i + 1) * hb, s // 16 - 1), 0)),
                  pl.BlockSpec((bm, w), lambda i: (i, 0)),
                  pl.BlockSpec((16, w), lambda i: (jnp.maximum(i * hb - 1, 0), 0)),
                  pl.BlockSpec(convw8.shape, lambda i: (0, 0))],
        out_specs=[pl.BlockSpec((bm, w), lambda i: (i, 0)), pl.BlockSpec((8, w), lambda i: (0, 0))],
        out_shape=[jax.ShapeDtypeStruct((s, w), BF16), jax.ShapeDtypeStruct((8, w), F32)],
        compiler_params=_params(blk),
    )(dyc, dyc, pq, pq, convw8)


BNN = (((2,), (1,)), ((0,), (0,)))
BNT = (((2,), (2,)), ((0,), (0,)))
BTN = (((1,), (1,)), ((0,), (0,)))


def _raw_dot_1pass(a, b, dims):
    return lax.dot_general(a.astype(BF16), b.astype(BF16), dims, preferred_element_type=F32)


def _raw_dot_3pass(a, b, dims):
    ah = a.astype(BF16)
    al = (a - ah.astype(F32)).astype(BF16)
    bh = b.astype(BF16)
    bl = (b - bh.astype(F32)).astype(BF16)
    d = lambda x, y: lax.dot_general(x, y, dims, preferred_element_type=F32)
    return d(ah, bh) + (d(ah, bl) + d(al, bh))


def _with_same_precision_vjp(raw):
    @functools.partial(jax.custom_vjp, nondiff_argnums=(2,))
    def dot(a, b, dims):
        return raw(a, b, dims)

    def fwd(a, b, dims):
        return raw(a, b, dims), (a, b)

    def bwd(dims, res, ct):
        a, b = res
        if dims == BNN:
            return raw(ct, b, BNT), raw(a, ct, BTN)
        if dims == BNT:
            return raw(ct, b, BNN), raw(ct, a, BTN)
        assert dims == BTN
        return raw(b, ct, BNT), raw(a, ct, BNN)

    dot.defvjp(fwd, bwd)
    return dot


_dot_1pass_vjp = _with_same_precision_vjp(_raw_dot_1pass)
_dot_3pass_vjp = _with_same_precision_vjp(_raw_dot_3pass)


def _dot_bf16(a, b, dims=BNN):
    return _dot_1pass_vjp(a, b, dims)


def _dot_3pass(a, b, dims=BNN):
    return _dot_3pass_vjp(a, b, dims)


def _neumann_inverse(x):
    h, c, _ = x.shape
    eye = lax.broadcasted_iota(jnp.int32, (h, c, c), 1) == lax.broadcasted_iota(jnp.int32, (h, c, c), 2)
    t = jnp.where(eye, 1.0, 0.0) + x
    p = x
    for _ in range(5):
        p = _raw_dot_3pass(p, p, BNN)
        t = t + _raw_dot_3pass(t, p, BNN)
    return t


@jax.custom_vjp
def _known_inverse(x, t):
    return t


def _known_inverse_fwd(x, t):
    return t, t


def _known_inverse_bwd(t, ct):
    return _raw_dot_3pass(_raw_dot_3pass(t, ct, BTN), t, BNT), jnp.zeros_like(t)


_known_inverse.defvjp(_known_inverse_fwd, _known_inverse_bwd)


def _delta_chunk(q, k, v, gcol, bcol, state, t_known=None):
    h, c, _ = q.shape
    row = lax.broadcasted_iota(jnp.int32, (h, c, c), 1)
    col = lax.broadcasted_iota(jnp.int32, (h, c, c), 2)
    incl, strict, eye = row >= col, row > col, row == col
    g_b = jnp.broadcast_to(gcol, (h, c, c))
    gc_row = jnp.sum(jnp.where(row <= col, g_b, 0.0), axis=1, keepdims=True)
    g_r = jnp.sum(jnp.where(eye, g_b, 0.0), axis=1, keepdims=True)
    gc_col = jnp.sum(jnp.where(incl, jnp.broadcast_to(g_r, (h, c, c)), 0.0), axis=2, keepdims=True)
    decay = jnp.exp(jnp.where(incl, gc_col - gc_row, NEG))
    kb = k * bcol
    vb = v * bcol
    x = -jnp.where(strict, _dot_bf16(kb, k, BNT) * decay, 0.0)
    t = _neumann_inverse(x) if t_known is None else _known_inverse(x, t_known)
    eg = jnp.exp(gc_col)
    u = _dot_3pass(t, vb)
    w = _dot_3pass(t, kb * eg)
    qk = _dot_bf16(q, k, BNT) * decay
    v_new = u - _dot_bf16(w, state)
    o = _dot_bf16(q * eg, state) + _dot_bf16(qk, v_new)
    g_last = jnp.sum(g_r, axis=2, keepdims=True)
    new_state = state * jnp.exp(g_last) + _dot_bf16(k * jnp.exp(g_last - gc_col), v_new, BTN)
    return o, new_state, t


def _lane_col(blk, idx):
    lane = lax.broadcasted_iota(jnp.int32, blk.shape, 1)
    return jnp.sum(jnp.where(lane == idx, blk, 0.0), axis=1, keepdims=True)


def _dn_heads(ref, base):
    return jnp.stack([ref[:, base + h * DN_DIM:base + (h + 1) * DN_DIM] for h in range(DN_HEADS)], axis=0).astype(F32)


def _dn_cols(gbv, base):
    return jnp.stack([_lane_col(gbv, base + h) for h in range(DN_HEADS)], axis=0)


def _delta_fwd(name, qkvn, gb):
    s = qkvn.shape[0]
    n = s // DN_CHUNK
    c = DN_CHUNK

    def body(qkv_ref, gb_ref, o_ref, st_ref, t_ref, state):
        @pl.when(pl.program_id(0) == 0)
        def _():
            state[...] = jnp.zeros(state.shape, F32)

        gbv = gb_ref[...]
        st = state[...]
        st_ref[0] = st
        o, new, t = _delta_chunk(_dn_heads(qkv_ref, 0), _dn_heads(qkv_ref, DN_WIDTH), _dn_heads(qkv_ref, 2 * DN_WIDTH),
                                 _dn_cols(gbv, DN_HEADS), _dn_cols(gbv, 0), st)
        for h in range(DN_HEADS):
            o_ref[:, h * DN_DIM:(h + 1) * DN_DIM] = o[h]
        t_ref[0] = t
        state[...] = new

    blk = _nbytes((c, 3 * DN_WIDTH), BF16) + _nbytes((c, LANES), F32) + _nbytes((c, DN_WIDTH), F32)
    blk += _nbytes((DN_HEADS, DN_DIM, DN_DIM), F32) + _nbytes((DN_HEADS, c, c), F32)
    return pl.pallas_call(
        body, name=name, grid=(n,),
        in_specs=[pl.BlockSpec((c, 3 * DN_WIDTH), lambda i: (i, 0)), pl.BlockSpec((c, LANES), lambda i: (i, 0))],
        out_specs=[pl.BlockSpec((c, DN_WIDTH), lambda i: (i, 0)),
                   pl.BlockSpec((1, DN_HEADS, DN_DIM, DN_DIM), lambda i: (i, 0, 0, 0)),
                   pl.BlockSpec((1, DN_HEADS, c, c), lambda i: (i, 0, 0, 0))],
        out_shape=[jax.ShapeDtypeStruct((s, DN_WIDTH), F32),
                   jax.ShapeDtypeStruct((n, DN_HEADS, DN_DIM, DN_DIM), F32),
                   jax.ShapeDtypeStruct((n, DN_HEADS, c, c), F32)],
        scratch_shapes=[pltpu.VMEM((DN_HEADS, DN_DIM, DN_DIM), F32)],
        compiler_params=_params(blk, 8 << 20),
    )(qkvn, gb)


def _delta_bwd(name, qkvn, gb, states, tinv, d_o):
    s = qkvn.shape[0]
    n = s // DN_CHUNK
    c = DN_CHUNK

    def body(qkv_ref, gb_ref, st_ref, t_ref, do_ref, dqkv_ref, dgb_ref, dstate):
        @pl.when(pl.program_id(0) == 0)
        def _():
            dstate[...] = jnp.zeros(dstate.shape, F32)

        gbv = gb_ref[...]
        lane = lax.broadcasted_iota(jnp.int32, (c, LANES), 1)
        t_known = t_ref[0]
        chunk = lambda *args: _delta_chunk(*args, t_known=t_known)[:2]
        _, vjp = jax.vjp(chunk, _dn_heads(qkv_ref, 0), _dn_heads(qkv_ref, DN_WIDTH),
                         _dn_heads(qkv_ref, 2 * DN_WIDTH), _dn_cols(gbv, DN_HEADS), _dn_cols(gbv, 0), st_ref[0])
        dq, dk, dv, dg, db, dst = vjp((_dn_heads(do_ref, 0), dstate[...]))
        dgb = jnp.zeros((c, LANES), F32)
        for h in range(DN_HEADS):
            dqkv_ref[:, h * DN_DIM:(h + 1) * DN_DIM] = dq[h]
            dqkv_ref[:, DN_WIDTH + h * DN_DIM:DN_WIDTH + (h + 1) * DN_DIM] = dk[h]
            dqkv_ref[:, 2 * DN_WIDTH + h * DN_DIM:2 * DN_WIDTH + (h + 1) * DN_DIM] = dv[h]
            dgb = dgb + jnp.where(lane == h, db[h], 0.0) + jnp.where(lane == DN_HEADS + h, dg[h], 0.0)
        dstate[...] = dst
        dgb_ref[...] = dgb

    rev = lambda i: (n - 1 - i, 0)
    blk = _nbytes((c, 3 * DN_WIDTH), BF16) + 2 * _nbytes((c, LANES), F32) + _nbytes((c, DN_WIDTH), F32)
    blk += _nbytes((DN_HEADS, DN_DIM, DN_DIM), F32) + _nbytes((c, 3 * DN_WIDTH), F32)
    return pl.pallas_call(
        body, name=name, grid=(n,),
        in_specs=[pl.BlockSpec((c, 3 * DN_WIDTH), rev), pl.BlockSpec((c, LANES), rev),
                  pl.BlockSpec((1, DN_HEADS, DN_DIM, DN_DIM), lambda i: (n - 1 - i, 0, 0, 0)),
                  pl.BlockSpec((1, DN_HEADS, c, c), lambda i: (n - 1 - i, 0, 0, 0)),
                  pl.BlockSpec((c, DN_WIDTH), rev)],
        out_specs=[pl.BlockSpec((c, 3 * DN_WIDTH), rev), pl.BlockSpec((c, LANES), rev)],
        out_shape=[jax.ShapeDtypeStruct((s, 3 * DN_WIDTH), F32), jax.ShapeDtypeStruct((s, LANES), F32)],
        scratch_shapes=[pltpu.VMEM((DN_HEADS, DN_DIM, DN_DIM), F32)],
        compiler_params=_params(blk, 16 << 20),
    )(qkvn, gb, states, tinv, d_o)


def _da_scores(q2f, k2, sub, valid, distf, head):
    lane = lax.broadcasted_iota(jnp.int32, q2f.shape, 1)
    hmask = (lane < DA_DIM) if sub == 0 else (lane >= DA_DIM)
    qm = jnp.where(hmask, q2f, 0.0).astype(BF16)
    slope = 2.0 ** (-ALIBI_MAX_EXP * (head + 1) / DA_HEADS)
    sc = lax.dot_general(qm, k2, NT, preferred_element_type=F32) * (DA_DIM ** -0.5)
    return jnp.where(valid, sc - slope * distf, NEG), qm, hmask


def _da_mask(i, r):
    qi = lax.broadcasted_iota(jnp.int32, (DA_BLOCK, 2 * DA_BLOCK), 0)
    ki = lax.broadcasted_iota(jnp.int32, (DA_BLOCK, 2 * DA_BLOCK), 1)
    dist = qi + DA_BLOCK - ki
    valid = (dist >= 0) & (dist <= DA_BLOCK) & ((ki >= DA_BLOCK) | (i > 0))
    return valid, (dist * r).astype(F32)


def _da_fwd(name, pda, r):
    s = pda.shape[0]
    n = s // r
    nb = n // DA_BLOCK
    w = DA_WIDTH
    dav = pda.reshape(n, r * 3 * w)

    def body(q_ref, kc_ref, kp_ref, vc_ref, vp_ref, o_ref, lse_ref):
        i = pl.program_id(1)
        valid, distf = _da_mask(i, r)
        lane = lax.broadcasted_iota(jnp.int32, (DA_BLOCK, LANES), 1)
        lse = jnp.zeros((DA_BLOCK, LANES), F32)
        for hp in range(DA_HEADS // 2):
            sl = slice(hp * LANES, (hp + 1) * LANES)
            q2f = q_ref[:, sl].astype(F32)
            k2 = jnp.concatenate([kp_ref[:, sl], kc_ref[:, sl]], axis=0)
            v2 = jnp.concatenate([vp_ref[:, sl], vc_ref[:, sl]], axis=0)
            o2 = None
            for sub in range(2):
                head = 2 * hp + sub
                sc, _, hmask = _da_scores(q2f, k2, sub, valid, distf, head)
                mx = jnp.max(sc, axis=1, keepdims=True)
                p = jnp.exp(sc - mx)
                l = jnp.sum(p, axis=1, keepdims=True)
                pv = lax.dot_general(p.astype(BF16), v2, NN, preferred_element_type=F32) / l
                o2 = pv if sub == 0 else jnp.where(hmask, pv, o2)
                lse = jnp.where(lane == head, mx + jnp.log(l), lse)
            o_ref[:, sl] = o2.astype(o_ref.dtype)
        lse_ref[...] = lse

    prev = lambda col: (lambda p, i: (jnp.maximum(i - 1, 0), 3 * p + col))
    cur = lambda col: (lambda p, i: (i, 3 * p + col))
    blk = 5 * _nbytes((DA_BLOCK, w), BF16) + _nbytes((DA_BLOCK, w), F32) + _nbytes((DA_BLOCK, LANES), F32)
    o, lse = pl.pallas_call(
        body, name=name, grid=(r, nb),
        in_specs=[pl.BlockSpec((DA_BLOCK, w), cur(0)), pl.BlockSpec((DA_BLOCK, w), cur(1)),
                  pl.BlockSpec((DA_BLOCK, w), prev(1)), pl.BlockSpec((DA_BLOCK, w), cur(2)),
                  pl.BlockSpec((DA_BLOCK, w), prev(2))],
        out_specs=[pl.BlockSpec((DA_BLOCK, w), lambda p, i: (i, p)),
                   pl.BlockSpec((DA_BLOCK, LANES), lambda p, i: (i, p))],
        out_shape=[jax.ShapeDtypeStruct((n, r * w), BF16), jax.ShapeDtypeStruct((n, r * LANES), F32)],
        compiler_params=_params(blk, 8 << 20),
    )(dav, dav, dav, dav, dav)
    return o.reshape(s, w), lse.reshape(s, LANES)


def _da_bwd(name, pda, d_ob, lse_tot, delta, r):
    s = pda.shape[0]
    n = s // r
    nb = n // DA_BLOCK
    w = DA_WIDTH
    dav = pda.reshape(n, r * 3 * w)
    dov = d_ob.reshape(n, r * w)
    lv = lse_tot.reshape(n, r * LANES)
    dlv = delta.reshape(n, r * LANES)

    def body(q_ref, kc_ref, kp_ref, vc_ref, vp_ref, do_ref, l_ref, dl_ref, dq_ref, dk_ref, dv_ref, ck, cv):
        i = pl.program_id(1)

        @pl.when(i == 0)
        def _():
            ck[...] = jnp.zeros(ck.shape, F32)
            cv[...] = jnp.zeros(cv.shape, F32)

        @pl.when(i < nb)
        def _():
            valid, distf = _da_mask(i, r)
            lsev = l_ref[...]
            dlt = dl_ref[...]
            for hp in range(DA_HEADS // 2):
                sl = slice(hp * LANES, (hp + 1) * LANES)
                q2f = q_ref[:, sl].astype(F32)
                k2 = jnp.concatenate([kp_ref[:, sl], kc_ref[:, sl]], axis=0)
                v2 = jnp.concatenate([vp_ref[:, sl], vc_ref[:, sl]], axis=0)
                do2f = do_ref[:, sl].astype(F32)
                dq2 = jnp.zeros((DA_BLOCK, LANES), F32)
                dk2 = jnp.zeros((2 * DA_BLOCK, LANES), F32)
                dv2 = jnp.zeros((2 * DA_BLOCK, LANES), F32)
                for sub in range(2):
                    head = 2 * hp + sub
                    sc, qm, hmask = _da_scores(q2f, k2, sub, valid, distf, head)
                    p = jnp.exp(sc - _lane_col(lsev, head))
                    dom = jnp.where(hmask, do2f, 0.0).astype(BF16)
                    dp = lax.dot_general(dom, v2, NT, preferred_element_type=F32)
                    ds = (p * (dp - _lane_col(dlt, head)) * (DA_DIM ** -0.5)).astype(BF16)
                    dq2 = dq2 + jnp.where(hmask, lax.dot_general(ds, k2, NN, preferred_element_type=F32), 0.0)
                    dk2 = dk2 + lax.dot_general(ds, qm, TN, preferred_element_type=F32)
                    dv2 = dv2 + lax.dot_general(p.astype(BF16), dom, TN, preferred_element_type=F32)
                dq_ref[:, sl] = dq2.astype(dq_ref.dtype)
                dk_ref[:, sl] = (ck[:, sl] + dk2[:DA_BLOCK]).astype(dk_ref.dtype)
                dv_ref[:, sl] = (cv[:, sl] + dv2[:DA_BLOCK]).astype(dv_ref.dtype)
                ck[:, sl] = dk2[DA_BLOCK:]
                cv[:, sl] = dv2[DA_BLOCK:]

        @pl.when(i == nb)
        def _():
            dk_ref[...] = ck[...].astype(dk_ref.dtype)
            dv_ref[...] = cv[...].astype(dv_ref.dtype)

    qrow = lambda i: jnp.minimum(i, nb - 1)
    prev = lambda col: (lambda p, i: (jnp.maximum(qrow(i) - 1, 0), 3 * p + col))
    cur = lambda col: (lambda p, i: (qrow(i), 3 * p + col))
    same = lambda p, i: (qrow(i), p)
    late = lambda p, i: (jnp.maximum(i - 1, 0), p)
    blk = 6 * _nbytes((DA_BLOCK, w), BF16) + 2 * _nbytes((DA_BLOCK, LANES), F32) + 3 * _nbytes((DA_BLOCK, w), F32)
    dq, dk, dv = pl.pallas_call(
        body, name=name, grid=(r, nb + 1),
        in_specs=[pl.BlockSpec((DA_BLOCK, w), cur(0)), pl.BlockSpec((DA_BLOCK, w), cur(1)),
                  pl.BlockSpec((DA_BLOCK, w), prev(1)), pl.BlockSpec((DA_BLOCK, w), cur(2)),
                  pl.BlockSpec((DA_BLOCK, w), prev(2)), pl.BlockSpec((DA_BLOCK, w), same),
                  pl.BlockSpec((DA_BLOCK, LANES), same), pl.BlockSpec((DA_BLOCK, LANES), same)],
        out_specs=[pl.BlockSpec((DA_BLOCK, w), same), pl.BlockSpec((DA_BLOCK, w), late),
                   pl.BlockSpec((DA_BLOCK, w), late)],
        out_shape=[jax.ShapeDtypeStruct((n, r * w), BF16)] * 3,
        scratch_shapes=[pltpu.VMEM((DA_BLOCK, w), F32), pltpu.VMEM((DA_BLOCK, w), F32)],
        compiler_params=_params(blk, 12 << 20),
    )(dav, dav, dav, dav, dav, dov, lv, dlv)
    return dq.reshape(s, w), dk.reshape(s, w), dv.reshape(s, w)


def _head_expand():
    hrow = lax.broadcasted_iota(jnp.int32, (LANES, DA_WIDTH), 0)
    lcol = lax.broadcasted_iota(jnp.int32, (LANES, DA_WIDTH), 1)
    return jnp.where(lcol // DA_DIM == hrow, 1.0, 0.0).astype(F32)


def _ffn_up(name, a, ga, tg, tu):
    s, d = a.shape
    nsh, _, _, ffs = ga.shape
    bm = _pick(s, (1024, 512, 256, 128))

    def body(a_ref, wg_ref, wu_ref, g_ref, u_ref, f_ref):
        av = a_ref[...]
        g = lax.dot_general(av, wg_ref[...], NN, preferred_element_type=F32)
        u = lax.dot_general(av, wu_ref[...], NN, preferred_element_type=F32)
        g_ref[...] = g.astype(BF16)
        u_ref[...] = u.astype(BF16)
        f_ref[...] = (_silu(g) * u).astype(BF16)

    wspec = lambda t: pl.BlockSpec((None, None, d, ffs), lambda i, j: (j, t, 0, 0))
    ospec = pl.BlockSpec((None, bm, ffs), lambda i, j: (j, i, 0))
    blk = _nbytes((bm, d), BF16) + 2 * _nbytes((d, ffs), BF16) + 3 * _nbytes((bm, ffs), BF16)
    return pl.pallas_call(
        body, name=name, grid=(s // bm, nsh),
        in_specs=[pl.BlockSpec((bm, d), lambda i, j: (i, 0)), wspec(tg), wspec(tu)],
        out_specs=[ospec] * 3, out_shape=[jax.ShapeDtypeStruct((nsh, s, ffs), BF16)] * 3,
        compiler_params=_params(blk, 4 * _nbytes((bm, ffs), F32)),
    )(a, ga, ga)


def _ffn_fwd(tag, h_in, ln, sh, sc, gt, ga, tg, tu, gb, td, weight):
    s, d = h_in.shape
    nsh, _, ffs, _ = gb.shape
    (a,) = _rowwise(tag + "_norm", lambda h, l, s1, s2: ((_rms_mod(h, l, s1, s2),), ()), [h_in], [ln, sh, sc],
                    [(d, BF16)])
    g, u, f = _ffn_up(tag + "_up", a, ga, tg, tu)
    bm, bn = _pick(s, (1024, 512, 256, 128)), _pick(d, (1024, 512, 256, 128))
    io = pl.BlockSpec((bm, bn), lambda i, j, kk: (i, j))
    h_out, o = _mm_core(
        tag + "_down", (s // bm, d // bn, nsh), nsh,
        [(f, pl.BlockSpec((None, bm, ffs), lambda i, j, kk: (kk, i, 0)),
          gb, pl.BlockSpec((None, None, ffs, bn), lambda i, j, kk: (kk, td, 0, j)), NN)],
        [((s, d), F32, io), ((s, d), BF16, io)], (bm, bn),
        epi=lambda acc, h, gv: (h + weight * gv * acc, acc),
        epi_ins=[(h_in, io), (gt, pl.BlockSpec((1, bn), lambda i, j, kk: (0, j)))])
    return h_out, dict(a=a, g=g, u=u, f=f, o=o)


def _resid_bwd(tag, dh_out, o, gt, weight):
    d = dh_out.shape[1]

    def fn(dh, ov, g):
        return (weight * g * dh,), (jnp.sum(weight * dh * ov.astype(F32), axis=0, keepdims=True),)

    do, d_gt = _rowwise(tag + "_resid_bwd", fn, [dh_out, o], [gt], [(d, BF16)], [(1, d)])
    return do, d_gt


def _norm_bwd(tag, h_in, da, dh_out, ln, sh, sc):
    d = h_in.shape[1]

    def fn(h, dav, dh, l, s1, s2):
        _, vjp = jax.vjp(_rms_mod, h, l, s1, s2)
        gh, gl, gs1, gs2 = vjp(dav)
        return (dh + gh,), (gl, gs1, gs2)

    return _rowwise(tag + "_norm_bwd", fn, [h_in, da, dh_out], [ln, sh, sc], [(d, F32)], [(1, d)] * 3)


def _ffn_bwd(tag, h_in, dh_out, sv, ln, sh, sc, gt, ga, tg, tu, gb, td, weight):
    s, d = h_in.shape
    nsh, _, ffs, _ = gb.shape
    bm, bn = _pick(s, (1024, 512, 256, 128)), _pick(d, (1024, 512, 256, 128))
    bk = _pick(s, (4096, 2048, 1024, 512, 256, 128))
    do, d_gt = _resid_bwd(tag, dh_out, sv["o"], gt, weight)

    def act_bwd(df, g, u):
        _, vjp = jax.vjp(_swiglu_act, g, u)
        return vjp(df)

    hid = pl.BlockSpec((None, bm, ffs), lambda i, j, kk: (j, i, 0))
    dg, du = _mm_core(
        tag + "_down_dx", (s // bm, nsh, 1), 1,
        [(do, pl.BlockSpec((bm, d), lambda i, j, kk: (i, 0)),
          gb, pl.BlockSpec((None, None, ffs, d), lambda i, j, kk: (j, td, 0, 0)), NT)],
        [((nsh, s, ffs), BF16, hid)] * 2, (bm, ffs), epi=act_bwd, epi_ins=[(sv["g"], hid), (sv["u"], hid)])
    (d_wd,) = _mm_core(
        tag + "_down_dw", (nsh, d // bn, s // bk), s // bk,
        [(sv["f"], pl.BlockSpec((None, bk, ffs), lambda i, j, kk: (i, kk, 0)),
          do, pl.BlockSpec((bk, bn), lambda i, j, kk: (kk, j)), TN)],
        [((nsh, ffs, d), BF16, pl.BlockSpec((None, ffs, bn), lambda i, j, kk: (i, 0, j)))], (ffs, bn))
    kmaj = pl.BlockSpec((None, bm, ffs), lambda i, j, kk: (kk, i, 0))
    wsp = lambda t: pl.BlockSpec((None, None, bn, ffs), functools.partial(lambda i, j, kk, t: (kk, t, j, 0), t=t))
    (da,) = _mm_core(
        tag + "_up_dx", (s // bm, d // bn, nsh), nsh, [(dg, kmaj, ga, wsp(tg), NT), (du, kmaj, ga, wsp(tu), NT)],
        [((s, d), F32, pl.BlockSpec((bm, bn), lambda i, j, kk: (i, j)))], (bm, bn))
    dws = []
    for nm, dh in (("_wg_dw", dg), ("_wu_dw", du)):
        (dw,) = _mm_core(
            tag + nm, (1, nsh, s // bk), s // bk,
            [(sv["a"], pl.BlockSpec((bk, d), lambda i, j, kk: (kk, 0)),
              dh, pl.BlockSpec((None, bk, ffs), lambda i, j, kk: (j, kk, 0)), TN)],
            [((nsh, d, ffs), BF16, pl.BlockSpec((None, d, ffs), lambda i, j, kk: (j, 0, 0)))], (d, ffs))
        dws.append(dw)
    dh_in, d_ln, d_sh, d_sc = _norm_bwd(tag, h_in, da, dh_out, ln, sh, sc)
    return dh_in, dict(wg=dws[0], wu=dws[1], wd=d_wd), dict(ln=d_ln, sh=d_sh, sc=d_sc, gt=d_gt)


def _mixer_fwd(tag, h_in, ln, sh, sc, gt, w, sp):
    d = h_in.shape[1]
    (a,) = _rowwise(tag + "_norm", lambda h, l, s1, s2: ((_rms_mod(h, l, s1, s2),), ()), [h_in], [ln, sh, sc],
                    [(d, BF16)])
    (pq,) = _matmul(tag + "_pq", a, w["wq"], outs=(BF16,))
    (pz,) = _matmul(tag + "_pz", a, w["wz"], outs=(BF16,))
    (pba,) = _matmul(tag + "_pba", a, w["wba"])
    (pda,) = _matmul(tag + "_pda", a, w["wda"], outs=(BF16,))
    (pg,) = _matmul(tag + "_pg", a, w["wg"], outs=(BF16,))
    yc, qkvn, gb = _conv_prep_fwd(tag + "_conv", pq, sp["conv8"], pba, sp["alog"], sp["dtb"])
    o_a, states, tinv = _delta_fwd(tag + "_delta", qkvn, gb)
    (o_an,) = _rowwise(tag + "_dnorm", lambda o, z, dn: ((_dn_outnorm(o, z.astype(F32), dn),), ()), [o_a, pz],
                       [sp["dn"]], [(DN_WIDTH, BF16)])
    ops, lses = [], []
    for (_, r) in DA_PATTERNS:
        o_p, lse_p = _da_fwd(f"{tag}_da{r}", pda, r)
        ops.append(o_p)
        lses.append(lse_p)

    def merge(o1, o2, o3, l1, l2, l3):
        mx = jnp.maximum(jnp.maximum(l1, l2), l3)
        e1, e2, e3 = jnp.exp(l1 - mx), jnp.exp(l2 - mx), jnp.exp(l3 - mx)
        tot = e1 + e2 + e3
        ex = _head_expand()
        up = lambda wgt: lax.dot_general(wgt / tot, ex, NN, precision=HI, preferred_element_type=F32)
        return (up(e1) * o1 + up(e2) * o2 + up(e3) * o3, mx + jnp.log(tot)), ()

    o_b, lse_tot = _rowwise(tag + "_merge", merge, ops + lses, [], [(DA_WIDTH, BF16), (LANES, F32)])
    (y_a,) = _matmul(tag + "_wa", o_an, w["w_a"], outs=(BF16,))
    (y_b,) = _matmul(tag + "_wb", o_b, w["w_b"], outs=(BF16,))

    def gate(ga, gbv, ya, yb):
        return _sigmoid(ga.astype(F32)) * ya.astype(F32) + _sigmoid(gbv.astype(F32)) * yb.astype(F32)

    (merged,) = _rowwise(tag + "_gate", lambda *v: ((gate(*v),), ()), [(pg, d, 0), (pg, d, 1), y_a, y_b], [],
                         [(d, BF16)])
    h_out, m = _matmul(tag + "_wo", merged, w["w_o"], outs=(F32, BF16), epi_rows=[h_in], epi_bcast=[gt],
                       epi=lambda acc, h, g: (h + g * acc, acc))
    sv = dict(a=a, pq=pq, pz=pz, pba=pba, pda=pda, pg=pg, yc=yc, qkvn=qkvn, gb=gb, o_a=o_a, states=states, tinv=tinv,
              o_an=o_an, o_b=o_b, lse=lse_tot, y_a=y_a, y_b=y_b, merged=merged, m=m, gate=gate)
    return h_out, sv


def _mixer_bwd(tag, h_in, dh_out, sv, ln, sh, sc, gt, w, sp):
    d = h_in.shape[1]
    dm, d_gt = _resid_bwd(tag, dh_out, sv["m"], gt, 1.0)
    (d_merged,) = _matmul(tag + "_wo_dx", dm, w["w_o"], tb=True, outs=(BF16,))
    (d_wo,) = _matmul(tag + "_wo_dw", sv["merged"], dm, ta=True, outs=(BF16,))
    gate = sv["gate"]

    def gate_bwd(dmg, ga, gbv, ya, yb):
        _, vjp = jax.vjp(gate, ga.astype(F32), gbv.astype(F32), ya.astype(F32), yb.astype(F32))
        dga, dgb, dya, dyb = vjp(dmg.astype(F32))
        return (jnp.concatenate([dga, dgb], axis=1), dya, dyb), ()

    pg = sv["pg"]
    d_pg, d_ya, d_yb = _rowwise(tag + "_gate_bwd", gate_bwd, [d_merged, (pg, d, 0), (pg, d, 1), sv["y_a"], sv["y_b"]],
                                [], [(2 * d, BF16), (d, BF16), (d, BF16)])
    (d_oan,) = _matmul(tag + "_wa_dx", d_ya, w["w_a"], tb=True)
    (d_wa,) = _matmul(tag + "_wa_dw", sv["o_an"], d_ya, ta=True, outs=(BF16,))
    (d_ob,) = _matmul(tag + "_wb_dx", d_yb, w["w_b"], tb=True, outs=(BF16,))
    (d_wb,) = _matmul(tag + "_wb_dw", sv["o_b"], d_yb, ta=True, outs=(BF16,))

    def dnorm_bwd(doan, o, z, dn):
        _, vjp = jax.vjp(_dn_outnorm, o, z.astype(F32), dn)
        go, gz, gdn = vjp(doan)
        return (go, gz), (gdn,)

    d_oa, d_pz, d_dn = _rowwise(tag + "_dnorm_bwd", dnorm_bwd, [d_oan, sv["o_a"], sv["pz"]], [sp["dn"]],
                                [(DN_WIDTH, F32), (DN_WIDTH, BF16)], [(1, DN_DIM)])
    d_qkvn, d_gb = _delta_bwd(tag + "_delta_bwd", sv["qkvn"], sv["gb"], sv["states"], sv["tinv"], d_oa)

    def prep_bwd(dq, dgbv, yc, pba, alog, dtb):
        _, vjp = jax.vjp(_dn_prep, yc.astype(F32), pba, alog, dtb)
        gyc, gpba, galog, gdtb = vjp((dq, dgbv))
        return (gyc, gpba), (galog, gdtb)

    d_yc, d_pba, d_alog, d_dtb = _rowwise(tag + "_prep_bwd", prep_bwd, [d_qkvn, d_gb, sv["yc"], sv["pba"]],
                                          [sp["alog"], sp["dtb"]], [(3 * DN_WIDTH, BF16), (LANES, BF16)],
                                          [(1, LANES), (1, LANES)], bm=128)
    d_pq, d_conv = _conv_bwd(tag + "_conv_bwd", d_yc, sv["pq"], sp["conv8"])

    def delta_fn(dob, ob):
        prod = dob.astype(F32) * ob.astype(F32)
        return (lax.dot_general(prod, _head_expand(), NT, precision=HI, preferred_element_type=F32),), ()

    (delta,) = _rowwise(tag + "_da_delta", delta_fn, [d_ob, sv["o_b"]], [], [(LANES, F32)])
    grads = [_da_bwd(f"{tag}_da{r}_bwd", sv["pda"], d_ob, sv["lse"], delta, r) for (_, r) in DA_PATTERNS]

    def sum3(*parts):
        q1, k1, v1, q2, k2, v2, q3, k3, v3 = (p.astype(F32) for p in parts)
        return (jnp.concatenate([q1 + q2 + q3, k1 + k2 + k3, v1 + v2 + v3], axis=1),), ()

    (d_pda,) = _rowwise(tag + "_da_sum", sum3, [t for g in grads for t in g], [], [(3 * DA_WIDTH, BF16)])

    a = sv["a"]
    (da,) = _matmul(tag + "_pq_dx", d_pq, w["wq"], tb=True)
    add = lambda acc, prev: (acc + prev,)
    (da,) = _matmul(tag + "_pz_dx", d_pz, w["wz"], tb=True, epi_rows=[da], epi=add)
    (da,) = _matmul(tag + "_pba_dx", d_pba, w["wba"], tb=True, epi_rows=[da], epi=add)
    (da,) = _matmul(tag + "_pda_dx", d_pda, w["wda"], tb=True, epi_rows=[da], epi=add)
    (da,) = _matmul(tag + "_pg_dx", d_pg, w["wg"], tb=True, epi_rows=[da], epi=add)
    (d_wq,) = _matmul(tag + "_pq_dw", a, d_pq, ta=True, outs=(BF16,))
    (d_wz,) = _matmul(tag + "_pz_dw", a, d_pz, ta=True, outs=(BF16,))
    (d_wba,) = _matmul(tag + "_pba_dw", a, d_pba, ta=True, outs=(BF16,))
    (d_wda,) = _matmul(tag + "_pda_dw", a, d_pda, ta=True, outs=(BF16,))
    (d_wg,) = _matmul(tag + "_pg_dw", a, d_pg, ta=True, outs=(BF16,))
    dh_in, d_ln, d_sh, d_sc = _norm_bwd(tag, h_in, da, dh_out, ln, sh, sc)
    wgrads = dict(wq=d_wq, wz=d_wz, wba=d_wba, wda=d_wda, wg=d_wg, w_a=d_wa, w_b=d_wb, w_o=d_wo)
    small = dict(ln=d_ln, sh=d_sh, sc=d_sc, gt=d_gt, dn=d_dn, alog=d_alog, dtb=d_dtb, conv=d_conv)
    return dh_in, wgrads, small


def _loss_head(h, target, fnorm):
    d = h.shape[1]

    def fn(hv, tv, fw):
        def lossf(hh, ww):
            y = hh * lax.rsqrt(jnp.mean(hh * hh, axis=-1, keepdims=True) + NORM_EPS) * ww
            return 0.5 * jnp.sum(jnp.mean(jnp.square(y - tv), axis=-1))

        val, (dh, dw) = jax.value_and_grad(lossf, argnums=(0, 1))(hv, fw)
        return (dh,), (jnp.full((1, LANES), val, F32), dw)

    return _rowwise("loss_head", fn, [h, target], [fnorm], [(d, F32)], [(1, LANES), (1, d)])


def _row(v):
    return v.reshape(1, -1)


def _pad_lanes(v, offset):
    return jnp.pad(v.reshape(1, -1), ((0, 0), (offset, LANES - offset - v.shape[0])))


_UP_SLOTS = dict(ffn1_wg=0, ffn1_wu=1, ffn2_wg=2, ffn2_wu=3)
_DOWN_SLOTS = dict(ffn1_wd=0, ffn2_wd=1)


def _local_step(x2, target, mod, layer_weights, small, on_layer_grads):
    depth = mod.shape[0]
    d = x2.shape[1]
    h = x2
    saved = []
    mods = []
    up = lambda l, nm: _UP_SLOTS[nm]
    down = lambda l, nm: _DOWN_SLOTS[nm]
    for l in range(depth):
        m9 = [_row(mod[l, i * d:(i + 1) * d]) for i in range(N_ADA)]
        sp = dict(conv8=jnp.pad(small["conv_w"][l], ((0, 8 - DN_CONV), (0, 0))),
                  alog=_pad_lanes(small["a_log"][l], DN_HEADS), dtb=_pad_lanes(small["dt_bias"][l], DN_HEADS),
                  dn=_row(small["dn_norm"][l]))
        ga, gb, w = layer_weights(l, h)
        h0 = h
        h1, sv1 = _ffn_fwd(f"l{l}_ffn1", h0, _row(small["ln_ffn1"][l]), m9[0], m9[1], m9[2], ga, up(l, "ffn1_wg"),
                           up(l, "ffn1_wu"), gb, down(l, "ffn1_wd"), 0.5)
        h2, sv2 = _mixer_fwd(f"l{l}_mix", h1, _row(small["ln_mix"][l]), m9[3], m9[4], m9[5], w, sp)
        h3, sv3 = _ffn_fwd(f"l{l}_ffn2", h2, _row(small["ln_ffn2"][l]), m9[6], m9[7], m9[8], ga, up(l, "ffn2_wg"),
                           up(l, "ffn2_wu"), gb, down(l, "ffn2_wd"), 0.5)
        saved.append((h0, h1, h2, sv1, sv2, sv3, sp, ga, gb, w))
        mods.append(m9)
        h = h3
    dh, loss_part, d_fnorm = _loss_head(h, target, _row(small["final_norm"]))
    sgrads, dmods = [], []
    token = None
    for l in reversed(range(depth)):
        h0, h1, h2, sv1, sv2, sv3, sp, ga, gb, w = saved[l]
        m9 = mods[l] if token is None else [r + token for r in mods[l]]
        dh, g3, s3 = _ffn_bwd(f"l{l}_ffn2", h2, dh, sv3, _row(small["ln_ffn2"][l]), m9[6], m9[7], m9[8], ga,
                              up(l, "ffn2_wg"), up(l, "ffn2_wu"), gb, down(l, "ffn2_wd"), 0.5)
        dh, g2, s2 = _mixer_bwd(f"l{l}_mix", h1, dh, sv2, _row(small["ln_mix"][l]), m9[3], m9[4], m9[5], w, sp)
        dh, g1, s1 = _ffn_bwd(f"l{l}_ffn1", h0, dh, sv1, _row(small["ln_ffn1"][l]), m9[0], m9[1], m9[2], ga,
                              up(l, "ffn1_wg"), up(l, "ffn1_wu"), gb, down(l, "ffn1_wd"), 0.5)
        token = on_layer_grads(l, dict(ffn1_wg=g1["wg"], ffn1_wu=g1["wu"], ffn1_wd=g1["wd"], ffn2_wg=g3["wg"],
                                       ffn2_wu=g3["wu"], ffn2_wd=g3["wd"], **g2))
        dmods.append(jnp.concatenate([s1["sh"], s1["sc"], s1["gt"], s2["sh"], s2["sc"], s2["gt"],
                                      s3["sh"], s3["sc"], s3["gt"]], axis=1))
        sgrads.append(dict(ln_ffn1=s1["ln"][0], ln_mix=s2["ln"][0], ln_ffn2=s3["ln"][0],
                           a_log=s2["alog"][0, DN_HEADS:2 * DN_HEADS], dt_bias=s2["dtb"][0, DN_HEADS:2 * DN_HEADS],
                           dn_norm=s2["dn"][0], conv_w=s2["conv"][:DN_CONV]))
    sgrads.reverse()
    dmods.reverse()
    return loss_part[0, 0], dh, jnp.concatenate(dmods, axis=0), sgrads, d_fnorm[0]


def _flip(v, bit):
    return 1 - v if bit else v


def _allgather8(name, x):
    r, c = x.shape

    def body(x_ref, out_ref, send_sems, recv_sems, local_sem):
        mx, my, mc = lax.axis_index("x"), lax.axis_index("y"), lax.axis_index("c")
        me = 4 * mx + 2 * my + mc
        mine = pltpu.make_async_copy(x_ref, out_ref.at[me], local_sem)
        mine.start()
        sends = []
        for k in range(1, 8):
            peer = (_flip(mx, k & 4), _flip(my, k & 2), _flip(mc, k & 1))
            cp = pltpu.make_async_remote_copy(src_ref=x_ref, dst_ref=out_ref.at[me], send_sem=send_sems.at[k - 1],
                                              recv_sem=recv_sems.at[k - 1], device_id=peer, device_id_type=MESH)
            cp.start()
            sends.append(cp)
        for k in range(1, 8):
            peer = (_flip(mx, k & 4), _flip(my, k & 2), _flip(mc, k & 1))
            src = 4 * peer[0] + 2 * peer[1] + peer[2]
            pltpu.make_async_remote_copy(src_ref=x_ref, dst_ref=out_ref.at[src], send_sem=send_sems.at[k - 1],
                                         recv_sem=recv_sems.at[k - 1], device_id=peer, device_id_type=MESH).wait_recv()
        for cp in sends:
            cp.wait_send()
        mine.wait()

    return pl.pallas_call(
        body, name=name, out_shape=jax.ShapeDtypeStruct((8, r, c), x.dtype),
        in_specs=[pl.BlockSpec(memory_space=pltpu.VMEM)], out_specs=pl.BlockSpec(memory_space=pltpu.VMEM),
        scratch_shapes=[pltpu.SemaphoreType.DMA((7,)), pltpu.SemaphoreType.DMA((7,)), pltpu.SemaphoreType.DMA],
        compiler_params=_params(9 * _nbytes((r, c), x.dtype)),
    )(x)


def _chip_peers(mx, my):
    chips = [(1 - mx, my), (mx, 1 - my), (1 - mx, 1 - my)]
    return chips, [2 * cx + cy for (cx, cy) in chips]


_ANY = pl.BlockSpec(memory_space=pl.ANY)


def _row_half(mc, r):
    return pl.ds(pl.multiple_of(mc * (r // 2), 16), r // 2)


def _gather_groups(name, shards):
    ng = len(shards)

    def body(*refs):
        xs, outs = refs[:ng], refs[ng:2 * ng]
        send_sems, recv_sems = refs[2 * ng:]
        mx, my, mc = lax.axis_index("x"), lax.axis_index("y"), lax.axis_index("c")
        j = 2 * mx + my
        chips, idxs = _chip_peers(mx, my)
        sib = (mx, my, 1 - mc)

        def copy(k, src, dst, to):
            return pltpu.make_async_remote_copy(src_ref=src, dst_ref=dst, send_sem=send_sems.at[k],
                                                recv_sem=recv_sems.at[k], device_id=to, device_id_type=MESH)

        first, passed = [], []
        for g in range(ng):
            mine = _row_half(mc, shards[g].shape[1])
            for t, chip in enumerate(chips):
                cp = copy(6 * g + t, xs[g].at[:, mine], outs[g].at[j, :, mine], (*chip, mc))
                cp.start()
                first.append(cp)
        for g in range(ng):
            mine = _row_half(mc, shards[g].shape[1])
            for t, chip in enumerate(chips):
                landed = outs[g].at[idxs[t], :, mine]
                copy(6 * g + t, landed, landed, (*chip, mc)).wait_recv()
                fwd = copy(6 * g + 3 + t, landed, landed, sib)
                fwd.start()
                passed.append(fwd)
        for g in range(ng):
            theirs_half = _row_half(1 - mc, shards[g].shape[1])
            for t in range(3):
                theirs = outs[g].at[idxs[t], :, theirs_half]
                copy(6 * g + 3 + t, theirs, theirs, sib).wait_recv()
        for cp in first + passed:
            cp.wait_send()

    outs = pl.pallas_call(
        body, name=name, out_shape=[jax.ShapeDtypeStruct((4,) + x.shape, x.dtype) for x in shards],
        in_specs=[_ANY] * ng, out_specs=[_ANY] * ng,
        scratch_shapes=[pltpu.SemaphoreType.DMA((6 * ng,)), pltpu.SemaphoreType.DMA((6 * ng,))],
    )(*shards)
    return _place_own_slab(outs, shards)


def _place_own_slab(outs, shards):
    chip = 2 * lax.axis_index("x") + lax.axis_index("y")
    return [lax.dynamic_update_slice(o, x[None], (chip,) + (0,) * x.ndim) for o, x in zip(outs, shards)]


_HBM = pl.BlockSpec(memory_space=pltpu.HBM)
_SEM = pl.BlockSpec(memory_space=pltpu.SEMAPHORE)
_DATAFLOW = pltpu.SideEffectType.DATAFLOW_SIDE_EFFECTING


def _ici_gather_copies(src_refs, land_refs, send_sems, recv_sems, scatter=False):
    mx, my, mc = lax.axis_index("x"), lax.axis_index("y"), lax.axis_index("c")
    j = 2 * mx + my
    chips, idxs = _chip_peers(mx, my)
    sends, recvs = [], []
    for g, src in enumerate(src_refs):
        for t, chip in enumerate(chips):
            common = dict(send_sem=send_sems.at[3 * g + t], recv_sem=recv_sems.at[3 * g + t], device_id=(*chip, mc),
                          device_id_type=MESH)
            if scatter:
                out, to, frm = src.at[idxs[t]], land_refs[g].at[j], land_refs[g].at[idxs[t]]
            else:
                mine = _row_half(mc, src.shape[1])
                out, to, frm = src.at[:, mine], land_refs[g].at[j, :, mine], land_refs[g].at[idxs[t], :, mine]
            sends.append(pltpu.make_async_remote_copy(src_ref=out, dst_ref=to, **common))
            recvs.append(pltpu.make_async_remote_copy(src_ref=out, dst_ref=frm, **common))
    return sends, recvs


def _gather_start(name, shards, scatter=False):
    ng = len(shards)

    def body(*refs):
        srcs, lands = refs[:ng], refs[ng:2 * ng]
        send_sems, recv_sems = refs[2 * ng], refs[2 * ng + 1]
        token = refs[-1]
        sends, _ = _ici_gather_copies(srcs, lands, send_sems, recv_sems, scatter)
        for cp in sends:
            cp.start()
        token[...] = jnp.zeros(token.shape, token.dtype)

    land_shape = lambda x: x.shape if scatter else (4,) + x.shape
    srcs = [pltpu.with_memory_space_constraint(x, pltpu.HBM) for x in shards]
    lands = [pltpu.with_memory_space_constraint(lax.empty(land_shape(x), x.dtype), pltpu.HBM) for x in shards]
    res = pl.pallas_call(
        body, name=name,
        out_shape=(pltpu.SemaphoreType.DMA((3 * ng,)), pltpu.SemaphoreType.DMA((3 * ng,)),
                   *[pltpu.HBM(x.shape, x.dtype) for x in srcs], *[pltpu.HBM(x.shape, x.dtype) for x in lands],
                   jax.ShapeDtypeStruct((8, LANES), F32)),
        in_specs=[_HBM] * (2 * ng),
        out_specs=(_SEM, _SEM, *[_HBM] * (2 * ng), pl.BlockSpec(memory_space=pltpu.VMEM)),
        input_output_aliases={i: 2 + i for i in range(2 * ng)},
        compiler_params=pltpu.CompilerParams(has_side_effects=_DATAFLOW),
    )(*srcs, *lands)
    return dict(send_sems=res[0], recv_sems=res[1], srcs=list(res[2:2 + ng]), lands=list(res[2 + ng:2 + 2 * ng]),
                token=res[-1])


def _gather_wait(name, started, after, scatter=False):
    ng = len(started["srcs"])

    def body(*refs):
        srcs, lands = refs[:ng], refs[ng:2 * ng]
        send_sems, recv_sems = refs[2 * ng], refs[2 * ng + 1]
        sends, recvs = _ici_gather_copies(srcs, lands, send_sems, recv_sems, scatter)
        for cp in sends:
            cp.wait_send()
        for cp in recvs:
            cp.wait_recv()

    res = pl.pallas_call(
        body, name=name,
        out_shape=[pltpu.HBM(x.shape, x.dtype) for x in started["srcs"] + started["lands"]],
        in_specs=[_HBM] * (2 * ng) + [_SEM, _SEM, _ANY], out_specs=[_HBM] * (2 * ng),
        input_output_aliases={i: i for i in range(2 * ng)},
        compiler_params=pltpu.CompilerParams(has_side_effects=_DATAFLOW),
    )(*started["srcs"], *started["lands"], started["send_sems"], started["recv_sems"], after)
    return list(res[:ng]), list(res[ng:])


def _pair_forward_groups(name, lands, shards):
    ng = len(lands)

    def body(*refs):
        ins, outs = refs[:ng], refs[ng:2 * ng]
        send_sems, recv_sems = refs[2 * ng:]
        mx, my, mc = lax.axis_index("x"), lax.axis_index("y"), lax.axis_index("c")
        _, idxs = _chip_peers(mx, my)
        sib = (mx, my, 1 - mc)
        cps = []
        for g in range(ng):
            mine = _row_half(mc, lands[g].shape[2])
            for t in range(3):
                cp = pltpu.make_async_remote_copy(src_ref=ins[g].at[idxs[t], :, mine], dst_ref=outs[g].at[idxs[t], :, mine],
                                                  send_sem=send_sems.at[3 * g + t], recv_sem=recv_sems.at[3 * g + t],
                                                  device_id=sib, device_id_type=MESH)
                cp.start()
                cps.append(cp)
        for g in range(ng):
            theirs = _row_half(1 - mc, lands[g].shape[2])
            for t in range(3):
                pltpu.make_async_remote_copy(src_ref=ins[g].at[idxs[t], :, theirs], dst_ref=outs[g].at[idxs[t], :, theirs],
                                             send_sem=send_sems.at[3 * g + t], recv_sem=recv_sems.at[3 * g + t],
                                             device_id=sib, device_id_type=MESH).wait_recv()
        for cp in cps:
            cp.wait_send()

    outs = pl.pallas_call(
        body, name=name, out_shape=[jax.ShapeDtypeStruct(x.shape, x.dtype) for x in lands],
        in_specs=[_ANY] * ng, out_specs=[_ANY] * ng, input_output_aliases={i: i for i in range(ng)},
        scratch_shapes=[pltpu.SemaphoreType.DMA((3 * ng,)), pltpu.SemaphoreType.DMA((3 * ng,))],
    )(*lands)
    return _place_own_slab(outs, shards)


def _pair_swap_groups(name, gs):
    ng = len(gs)

    def body(*refs):
        xs, outs = refs[:ng], refs[ng:2 * ng]
        send_sems, recv_sems = refs[2 * ng:]
        mx, my, mc = lax.axis_index("x"), lax.axis_index("y"), lax.axis_index("c")
        cps = []
        for g in range(ng):
            cp = pltpu.make_async_remote_copy(src_ref=xs[g].at[:, :, _row_half(1 - mc, gs[g].shape[2])], dst_ref=outs[g],
                                              send_sem=send_sems.at[g], recv_sem=recv_sems.at[g],
                                              device_id=(mx, my, 1 - mc), device_id_type=MESH)
            cp.start()
            cps.append(cp)
        for cp in cps:
            cp.wait()

    return pl.pallas_call(
        body, name=name,
        out_shape=[jax.ShapeDtypeStruct(x.shape[:2] + (x.shape[2] // 2, x.shape[3]), x.dtype) for x in gs],
        in_specs=[_ANY] * ng, out_specs=[_ANY] * ng,
        scratch_shapes=[pltpu.SemaphoreType.DMA((ng,)), pltpu.SemaphoreType.DMA((ng,))],
    )(*gs)


def _chip_scatter_groups(name, ps):
    ng = len(ps)

    def body(*refs):
        xs, outs = refs[:ng], refs[ng:2 * ng]
        send_sems, recv_sems = refs[2 * ng:]
        mx, my, mc = lax.axis_index("x"), lax.axis_index("y"), lax.axis_index("c")
        j = 2 * mx + my
        chips, idxs = _chip_peers(mx, my)
        sends = []
        for g in range(ng):
            for t, chip in enumerate(chips):
                cp = pltpu.make_async_remote_copy(src_ref=xs[g].at[idxs[t]], dst_ref=outs[g].at[j],
                                                  send_sem=send_sems.at[3 * g + t], recv_sem=recv_sems.at[3 * g + t],
                                                  device_id=(*chip, mc), device_id_type=MESH)
                cp.start()
                sends.append(cp)
        for g in range(ng):
            for t, chip in enumerate(chips):
                pltpu.make_async_remote_copy(src_ref=xs[g].at[idxs[t]], dst_ref=outs[g].at[idxs[t]],
                                             send_sem=send_sems.at[3 * g + t], recv_sem=recv_sems.at[3 * g + t],
                                             device_id=(*chip, mc), device_id_type=MESH).wait_recv()
        for cp in sends:
            cp.wait_send()

    outs = pl.pallas_call(
        body, name=name, out_shape=[jax.ShapeDtypeStruct(x.shape, x.dtype) for x in ps],
        in_specs=[_ANY] * ng, out_specs=[_ANY] * ng,
        scratch_shapes=[pltpu.SemaphoreType.DMA((3 * ng,)), pltpu.SemaphoreType.DMA((3 * ng,))],
    )(*ps)
    return _place_own_part(outs, ps)


def _place_own_part(outs, ps):
    chip = 2 * lax.axis_index("x") + lax.axis_index("y")
    return [lax.dynamic_update_slice(o, lax.dynamic_index_in_dim(x, chip, 0, keepdims=True), (chip,) + (0,) * (x.ndim - 1))
            for o, x in zip(outs, ps)]


def _pair_merge_groups(name, fs):
    ng = len(fs)

    def body(*refs):
        xs, outs = refs[:ng], refs[ng:2 * ng]
        send_sems, recv_sems = refs[2 * ng:]
        mx, my, mc = lax.axis_index("x"), lax.axis_index("y"), lax.axis_index("c")
        cps = []
        for g in range(ng):
            mine = _row_half(mc, 2 * fs[g].shape[1])
            cp = pltpu.make_async_remote_copy(src_ref=xs[g], dst_ref=outs[g].at[:, mine], send_sem=send_sems.at[g],
                                              recv_sem=recv_sems.at[g], device_id=(mx, my, 1 - mc), device_id_type=MESH)
            cp.start()
            cps.append(cp)
        for g in range(ng):
            theirs = outs[g].at[:, _row_half(1 - mc, 2 * fs[g].shape[1])]
            pltpu.make_async_remote_copy(src_ref=xs[g], dst_ref=theirs, send_sem=send_sems.at[g],
                                         recv_sem=recv_sems.at[g], device_id=(mx, my, 1 - mc),
                                         device_id_type=MESH).wait_recv()
        for cp in cps:
            cp.wait_send()

    outs = pl.pallas_call(
        body, name=name,
        out_shape=[jax.ShapeDtypeStruct((x.shape[0], 2 * x.shape[1], x.shape[2]), x.dtype) for x in fs],
        in_specs=[_ANY] * ng, out_specs=[_ANY] * ng,
        scratch_shapes=[pltpu.SemaphoreType.DMA((ng,)), pltpu.SemaphoreType.DMA((ng,))],
    )(*fs)
    mc = lax.axis_index("c")
    return [lax.dynamic_update_slice(o, x, (0, mc * x.shape[1], 0)) for o, x in zip(outs, fs)]


def _block_rows(r, w, itemsize=4, budget=4 << 20):
    for c in (r, 2048, 1024, 512, 256, 128, 64, 32, 16):
        if c <= r and r % c == 0 and c * w * itemsize <= budget:
            return c
    return r


def _pair_sum(name, g, got, cidx):
    ns, t, r, w = g.shape
    rh = r // 2
    bm = _block_rows(rh, w)
    nb = rh // bm

    def body(c_ref, a_ref, b_ref, o_ref):
        o_ref[...] = (a_ref[...].astype(F32) + b_ref[...].astype(F32)).astype(o_ref.dtype)

    blk = (None, None, bm, w)
    return pl.pallas_call(
        body, name=name,
        grid_spec=pltpu.PrefetchScalarGridSpec(
            num_scalar_prefetch=1, grid=(ns, t, nb),
            in_specs=[pl.BlockSpec(blk, lambda s, tt, i, c: (s, tt, c[0] * nb + i, 0)),
                      pl.BlockSpec(blk, lambda s, tt, i, c: (s, tt, i, 0))],
            out_specs=pl.BlockSpec(blk, lambda s, tt, i, c: (s, tt, i, 0))),
        out_shape=jax.ShapeDtypeStruct((ns, t, rh, w), BF16),
        compiler_params=_params(3 * _nbytes((bm, w), F32)),
    )(cidx, g, got)


def _chip_sum(name, p):
    ns, th, r, w = p.shape
    bm = _block_rows(r, w, budget=2 << 20)

    def body(p_ref, o_ref):
        acc = p_ref[0].astype(F32)
        for s in range(1, ns):
            acc = acc + p_ref[s].astype(F32)
        o_ref[...] = acc

    return pl.pallas_call(
        body, name=name, grid=(th, r // bm),
        in_specs=[pl.BlockSpec((ns, None, bm, w), lambda tt, i: (0, tt, i, 0))],
        out_specs=pl.BlockSpec((None, bm, w), lambda tt, i: (tt, i, 0)),
        out_shape=jax.ShapeDtypeStruct((th, r, w), F32),
        compiler_params=_params(ns * _nbytes((bm, w), BF16) + 2 * _nbytes((bm, w), F32)),
    )(p)


def _sum_leading(name, x):
    n = x.shape[0]

    def body(p_ref, o_ref):
        acc = p_ref[0]
        for s in range(1, n):
            acc = acc + p_ref[s]
        o_ref[...] = acc

    return pl.pallas_call(body, name=name, out_shape=jax.ShapeDtypeStruct(x.shape[1:], F32),
                          compiler_params=_params(2 * _nbytes(x.shape, F32)))(x)


def _reduce_scatter_begin(tag, gs, overlap):
    cidx = lax.axis_index("c").astype(jnp.int32).reshape(1)
    got = _pair_swap_groups(tag + "_pair_swap", gs)
    pair = [_pair_sum(f"{tag}_pair_sum{i}", g, r_, cidx) for i, (g, r_) in enumerate(zip(gs, got))]
    if overlap:
        return _gather_start(tag + "_scatter_start", pair, scatter=True)
    return _chip_scatter_groups(tag + "_chip_scatter", pair)


def _reduce_scatter_end(tag, state, overlap, after):
    if overlap:
        srcs, lands = _gather_wait(tag + "_scatter_wait", state, after, scatter=True)
        state = _place_own_part(lands, srcs)
    fin = [_chip_sum(f"{tag}_chip_sum{i}", p) for i, p in enumerate(state)]
    return _pair_merge_groups(tag + "_pair_merge", fin)


_GROUPS = ((("ffn1_wg", "ffn1_wu", "ffn2_wg", "ffn2_wu"), 1), (("ffn1_wd", "ffn2_wd"), 0), (("w_a",), 0),
           (("w_o",), 0), (("w_in",), 1), (("w_b",), 1))


def _shard_major(g, ax):
    k, n = g.shape
    if ax == 0:
        return g.reshape(4, k // 4, n)
    return g.reshape(k, 4, n // 4).transpose(1, 0, 2)


def _in_cols(d):
    o1 = 3 * DN_WIDTH
    o2 = o1 + DN_WIDTH
    o3 = o2 + 2 * DN_HEADS
    o4 = o3 + 3 * DA_WIDTH
    return dict(wq=(0, o1), wz=(o1, o2), wba=(o2, o3), wda=(o3, o4), wg=(o4, o4 + 2 * d))


def _mixer_weights(w_in, w_a, w_b, w_o, d):
    w = {k: w_in[:, a:b] for k, (a, b) in _in_cols(d).items()}
    w["wba"] = jnp.pad(w["wba"], ((0, 0), (0, LANES - 2 * DN_HEADS)))
    w["w_a"], w["w_b"], w["w_o"] = w_a, w_b, w_o
    return w


def _w_in_grad(wg):
    return jnp.concatenate([wg["wq"], wg["wz"], wg["wba"][:, :2 * DN_HEADS], wg["wda"], wg["wg"]], axis=1)


def _adam_math(wv, gv, mv, vv):
    mn = ADAM_B1 * mv + (1.0 - ADAM_B1) * gv
    vn = ADAM_B2 * vv + (1.0 - ADAM_B2) * jnp.square(gv)
    m_hat = mn / (1.0 - ADAM_B1 ** ADAM_STEP)
    v_hat = vn / (1.0 - ADAM_B2 ** ADAM_STEP)
    delta = -ADAM_LR * (m_hat / (jnp.sqrt(v_hat) + ADAM_EPS) + ADAM_WD * wv)
    return delta, mn, vn


def _adamw(name, w, g, m, v):
    shape = w.shape
    cols = shape[-1]
    w2, g2, m2, v2 = (t.reshape(-1, cols) for t in (w, g, m, v))
    rows = w2.shape[0]
    bm = _pick(rows, (256, 128, 64, 32, 16, 8)) if rows >= 8 else rows
    delta, mn, vn = _rowwise(name, lambda *t: (_adam_math(*t), ()), [w2, g2, m2, v2], [], [(cols, F32)] * 3, bm=bm)
    return delta.reshape(shape), mn.reshape(shape), vn.reshape(shape)


def _adamw_leading(name, w, g, m, v):
    n = w.shape[0]
    padded_row = -(-w.shape[1] // 8) * 8 * w.shape[2] * 4
    bm = max(c for c in range(1, n + 1) if n % c == 0 and (c * padded_row <= (1 << 20) or c == 1))

    def body(w_ref, g_ref, m_ref, v_ref, d_ref, mo_ref, vo_ref):
        d_ref[...], mo_ref[...], vo_ref[...] = _adam_math(w_ref[...], g_ref[...], m_ref[...], v_ref[...])

    spec = pl.BlockSpec((bm,) + w.shape[1:], lambda i: (i, 0, 0))
    return pl.pallas_call(
        body, name=name, grid=(n // bm,), in_specs=[spec] * 4, out_specs=[spec] * 3,
        out_shape=[jax.ShapeDtypeStruct(w.shape, F32)] * 3, compiler_params=_params(7 * bm * padded_row),
    )(w, g, m, v)


def _adamw_stacked(name, w, m, v, gstacks, slot):
    depth, r, cdim = w.shape
    bm = _block_rows(r, cdim, budget=1 << 20)

    def body(w_ref, m_ref, v_ref, *rest):
        g_refs, (go_ref, d_ref, mo_ref, vo_ref) = rest[:depth], rest[depth:]
        layer = pl.program_id(0)
        gv = g_refs[0][...]
        for l in range(1, depth):
            gv = jnp.where(layer == l, g_refs[l][...], gv)
        go_ref[...] = gv
        d_ref[...], mo_ref[...], vo_ref[...] = _adam_math(w_ref[...], gv, m_ref[...], v_ref[...])

    nat = pl.BlockSpec((None, bm, cdim), lambda l, i: (l, i, 0))
    return pl.pallas_call(
        body, name=name, grid=(depth, r // bm),
        in_specs=[nat, nat, nat] + [pl.BlockSpec((None, bm, cdim), lambda l, i: (slot, i, 0))] * depth,
        out_specs=[nat] * 4, out_shape=[jax.ShapeDtypeStruct(w.shape, F32)] * 4,
        compiler_params=_params((7 + depth) * _nbytes((bm, cdim), F32)),
    )(w, m, v, *gstacks)


def kernel(x, c, ada_w, ada_b, ln_ffn1, ln_mix, ln_ffn2, ffn1_wg, ffn1_wu, ffn1_wd, w_in, conv_w, a_log, dt_bias, dn_norm, w_a, w_b, w_o, ffn2_wg, ffn2_wu, ffn2_wd, final_norm, loss_target, m_ada_w, m_ada_b, m_ln_ffn1, m_ln_mix, m_ln_ffn2, m_ffn1_wg, m_ffn1_wu, m_ffn1_wd, m_w_in, m_conv_w, m_a_log, m_dt_bias, m_dn_norm, m_w_a, m_w_b, m_w_o, m_ffn2_wg, m_ffn2_wu, m_ffn2_wd, m_final_norm, v_ada_w, v_ada_b, v_ln_ffn1, v_ln_mix, v_ln_ffn2, v_ffn1_wg, v_ffn1_wu, v_ffn1_wd, v_w_in, v_conv_w, v_a_log, v_dt_bias, v_dn_norm, v_w_a, v_w_b, v_w_o, v_ffn2_wg, v_ffn2_wu, v_ffn2_wd, v_final_norm):
    names = ["ada_w", "ada_b", "ln_ffn1", "ln_mix", "ln_ffn2", "ffn1_wg", "ffn1_wu", "ffn1_wd", "w_in", "conv_w",
             "a_log", "dt_bias", "dn_norm", "w_a", "w_b", "w_o", "ffn2_wg", "ffn2_wu", "ffn2_wd", "final_norm"]
    wts = dict(zip(names, (ada_w, ada_b, ln_ffn1, ln_mix, ln_ffn2, ffn1_wg, ffn1_wu, ffn1_wd, w_in, conv_w, a_log,
                           dt_bias, dn_norm, w_a, w_b, w_o, ffn2_wg, ffn2_wu, ffn2_wd, final_norm)))
    mom = dict(zip(names, (m_ada_w, m_ada_b, m_ln_ffn1, m_ln_mix, m_ln_ffn2, m_ffn1_wg, m_ffn1_wu, m_ffn1_wd, m_w_in,
                           m_conv_w, m_a_log, m_dt_bias, m_dn_norm, m_w_a, m_w_b, m_w_o, m_ffn2_wg, m_ffn2_wu,
                           m_ffn2_wd, m_final_norm)))
    var = dict(zip(names, (v_ada_w, v_ada_b, v_ln_ffn1, v_ln_mix, v_ln_ffn2, v_ffn1_wg, v_ffn1_wu, v_ffn1_wd, v_w_in,
                           v_conv_w, v_a_log, v_dt_bias, v_dn_norm, v_w_a, v_w_b, v_w_o, v_ffn2_wg, v_ffn2_wu,
                           v_ffn2_wd, v_final_norm)))
    _, s, d = x.shape
    depth = ada_w.shape[0]
    mx, my, mc = lax.axis_index("x"), lax.axis_index("y"), lax.axis_index("c")
    chip = 2 * mx + my
    me = 2 * chip + mc
    nshard = ada_w.shape[2]

    cact = _rowwise("c_silu", lambda cv: ((_silu(cv),), ()), [jnp.pad(c, ((0, 7), (0, 0)))], [], [(d, F32)], bm=8)[0]
    c_all = _allgather8("ag_c", cact)[:, 0, :]
    conv_all = _allgather8("ag_conv", jnp.pad(conv_w.reshape(depth * DN_CONV, -1), ((0, 8 - depth * DN_CONV), (0, 0))))
    conv_full = jnp.concatenate([conv_all[2 * j, :depth * DN_CONV] for j in range(4)], axis=1)
    conv_full = conv_full.reshape(depth, DN_CONV, 3 * DN_WIDTH)
    layer_shards = [[jnp.stack([wts[nm][l].astype(BF16) for nm in nms], axis=0) for nms, _ in _GROUPS]
                    for l in range(depth)]
    gathered0 = _gather_groups("ag_weights0", layer_shards[0])
    rows_of = lambda st: st[:, 0].reshape(-1, st.shape[-1])
    cols_of = lambda st: jnp.concatenate([st[j, 0] for j in range(4)], axis=1)

    def layer_weights(l, after):
        if l == 0:
            got = gathered0
        else:
            srcs, lands = _gather_wait(f"ag_weights{l}_wait", started[l], after)
            got = _pair_forward_groups(f"ag_weights{l}_pair", lands, srcs)
        ga, gb, g_wa, g_wo, g_win, g_wb = got
        return ga, gb, _mixer_weights(cols_of(g_win), rows_of(g_wa), cols_of(g_wb), rows_of(g_wo), d)

    c16 = jnp.pad(c_all, ((0, 8), (0, 0))).astype(BF16)
    parts = []
    for l in range(depth):
        bias = lax.dynamic_slice(ada_b[l], (chip * nshard,), (nshard,)).reshape(1, nshard)
        (mp,) = _matmul(f"ada_fwd{l}", c16, ada_w[l].astype(BF16), epi_bcast=[bias], epi=lambda acc, b: (acc + b,))
        parts.append(mp)
    mod_all = _allgather8("ag_mod", jnp.concatenate(parts, axis=0))
    mod_rows = jnp.concatenate([mod_all[2 * j] for j in range(4)], axis=1)
    mod = jnp.stack([lax.dynamic_index_in_dim(mod_rows, l * 16 + me, axis=0, keepdims=False) for l in range(depth)])

    gathered0, later, mod, conv_full = lax.optimization_barrier((gathered0, layer_shards[1:], mod, conv_full))
    started = {l: _gather_start(f"ag_weights{l}_start", later[l - 1]) for l in range(1, depth)}
    for st in started.values():
        mod = mod + st["token"][0, 0]
    small = dict(conv_w=conv_full, a_log=a_log, dt_bias=dt_bias, dn_norm=dn_norm, ln_ffn1=ln_ffn1, ln_mix=ln_mix,
                 ln_ffn2=ln_ffn2, final_norm=final_norm)
    ffn_names = _GROUPS[0][0] + _GROUPS[1][0]
    rs_state, first_layer = {}, {}

    def on_layer_grads(l, wg):
        wg["w_in"] = _w_in_grad(wg)
        gs = [jnp.stack([wg[nm] if nm in ffn_names else _shard_major(wg[nm], ax) for nm in nms], axis=1)
              for nms, ax in _GROUPS]
        if l == 0:
            first_layer["gs"] = gs
            return None
        rs_state[l] = _reduce_scatter_begin(f"rs{l}", gs, overlap=True)
        return rs_state[l]["token"][0, 0]

    loss_part, dx, dmod, sgrads, d_fnorm = _local_step(x[0], loss_target[0], mod, layer_weights, small,
                                                       on_layer_grads)

    dmod_all = _allgather8("ag_dmod", jnp.pad(dmod, ((0, 8 - depth), (0, 0))))
    smalls = [loss_part.reshape(1), d_fnorm]
    for l in range(depth):
        sg = sgrads[l]
        smalls += [sg["ln_ffn1"], sg["ln_mix"], sg["ln_ffn2"], sg["a_log"], sg["dt_bias"], sg["dn_norm"],
                   sg["conv_w"].reshape(-1)]
    sizes = [t.shape[0] for t in smalls]
    tile = 8 * LANES
    flat = jnp.concatenate([jnp.pad(t, (0, (-t.shape[0]) % tile)).reshape(-1, LANES) for t in smalls], axis=0)
    small_all = _allgather8("ag_small", flat)
    dmod_all, small_all, gs0 = lax.optimization_barrier((dmod_all, small_all, first_layer["gs"]))
    rs_state[0] = _reduce_scatter_begin("rs0", gs0, overlap=True)
    started0 = rs_state[0]["token"][0, 0]
    dmod_all = dmod_all + started0
    small_all = small_all + started0

    g_ada_w, g_ada_b = [], []
    for l in range(depth):
        dm_l = dmod_all[:, l, :]
        (gb_l,) = _rowwise(f"ada_b_grad{l}", lambda v: ((), (jnp.sum(v, axis=0, keepdims=True),)), [dm_l], [], [],
                           [(1, N_ADA * d)], bm=8)
        g_ada_b.append(gb_l[0])
        dm_sh = lax.dynamic_slice(dm_l, (0, chip * nshard), (8, nshard))
        (gw_l,) = _matmul(f"ada_w_grad{l}", c16, jnp.pad(dm_sh, ((0, 8), (0, 0))).astype(BF16), ta=True)
        g_ada_w.append(gw_l)
    grads = dict(ada_w=jnp.stack(g_ada_w), ada_b=jnp.stack(g_ada_b))

    tot = _sum_leading("small_sum", small_all)
    offs, acc = [], 0
    for n_ in sizes:
        offs.append(acc)
        acc += -(-n_ // tile) * 8
    take = lambda i: tot[offs[i]:offs[i] + -(-sizes[i] // tile) * 8].reshape(-1)[:sizes[i]]
    loss = take(0)[0]
    grads["final_norm"] = take(1)
    per = 7
    for key_i, key in enumerate(["ln_ffn1", "ln_mix", "ln_ffn2", "a_log", "dt_bias", "dn_norm"]):
        grads[key] = jnp.stack([take(2 + per * l + key_i) for l in range(depth)])
    conv_g = jnp.stack([take(2 + per * l + 6).reshape(DN_CONV, 3 * DN_WIDTH) for l in range(depth)])
    csh = conv_w.shape[2]
    grads["conv_w"] = lax.dynamic_slice(conv_g, (0, 0, chip * csh), (depth, DN_CONV, csh))

    deltas, new_m, new_v = {}, {}, {}
    big = {nm for nms, _ in _GROUPS for nm in nms}
    for name in names:
        if name in big:
            continue
        wv, gv, mv, vv = wts[name], grads[name], mom[name], var[name]
        if wv.ndim == 1:
            wv, gv, mv, vv = (t.reshape(-1, LANES) for t in (wv, gv, mv, vv))
        dl, mn, vn = _adamw("adamw_" + name, wv, gv, mv, vv)
        deltas[name], new_m[name], new_v[name] = (t.reshape(wts[name].shape) for t in (dl, mn, vn))

    reduced = {l: _reduce_scatter_end(f"rs{l}", rs_state[l], True, dx) for l in range(depth - 1, 0, -1)}
    reduced[0] = _reduce_scatter_end("rs0", rs_state[0], True, deltas["ada_w"])

    for gi, (nms, ax) in enumerate(_GROUPS):
        for q, nm in enumerate(nms):
            wv, mv, vv = wts[nm], mom[nm], var[nm]
            per_layer = [reduced[l][gi][q] for l in range(depth)]
            if ax == 1 and wv.shape[2] % LANES and nm != "w_in":
                tr = lambda t: jnp.swapaxes(t, 1, 2)
                gt = jnp.stack([g.T for g in per_layer], axis=0)
                dl, mn, vn = _adamw("adamw_" + nm, tr(wv), gt, tr(mv), tr(vv))
                grads[nm], deltas[nm], new_m[nm], new_v[nm] = tr(gt), tr(dl), tr(mn), tr(vn)
            elif nm == "w_in" and wv.shape[2] % LANES:
                tr = lambda t: jnp.transpose(t, (2, 0, 1))
                back = lambda t: jnp.transpose(t, (1, 2, 0))
                gt = jnp.stack([g.T for g in per_layer], axis=1)
                dl, mn, vn = _adamw_leading("adamw_" + nm, tr(wv), gt, tr(mv), tr(vv))
                grads[nm], deltas[nm], new_m[nm], new_v[nm] = back(gt), back(dl), back(mn), back(vn)
            else:
                grads[nm], deltas[nm], new_m[nm], new_v[nm] = _adamw_stacked(
                    "adamw_" + nm, wv, mv, vv, [reduced[l][gi] for l in range(depth)], q)

    return (loss, dx.reshape(1, s, d), *[grads[n_] for n_ in names], *[deltas[n_] for n_ in names],
            *[new_m[n_] for n_ in names], *[new_v[n_] for n_ in names])
```

```python
import functools

import jax
import jax.numpy as jnp
from jax import lax
from jax.experimental import pallas as pl
from jax.experimental.pallas import tpu as pltpu

F32 = jnp.float32
BF16 = jnp.bfloat16
MESH = pl.DeviceIdType.MESH

NORM_EPS = 1e-6
DN_HEADS, DN_DIM, DN_CHUNK, DN_CONV = 8, 128, 64, 4
DN_WIDTH = DN_HEADS * DN_DIM
DA_HEADS, DA_DIM, DA_BLOCK = 12, 64, 128
DA_WIDTH = DA_HEADS * DA_DIM
DA_PATTERNS = ((128, 1), (512, 4), (2048, 16))
ALIBI_MAX_EXP = 8.0
N_ADA = 9
LANES = 128
V7X_VMEM_BYTES = 64 << 20
ADAM_LR, ADAM_B1, ADAM_B2, ADAM_EPS, ADAM_WD, ADAM_STEP = 0.001, 0.9, 0.999, 1e-08, 0.01, 10
NEG = -1e30
HI = lax.Precision.HIGHEST
NN = (((1,), (0,)), ((), ()))
NT = (((1,), (1,)), ((), ()))
TN = (((0,), (0,)), ((), ()))


def _nbytes(shape, dtype):
    n = 1
    for s in shape:
        n *= s
    return n * jnp.dtype(dtype).itemsize


def _params(block_bytes, scratch_bytes=0):
    need = 2 * block_bytes + scratch_bytes
    lim = min(max(need + need // 4 + (4 << 20), 32 << 20), V7X_VMEM_BYTES - (6 << 20))
    return pltpu.CompilerParams(vmem_limit_bytes=int(lim))


def _pick(n, cands):
    for c in cands:
        if c <= n and n % c == 0:
            return c
    return n


def _sigmoid(x):
    return jax.nn.sigmoid(x)


def _silu(x):
    return x * jax.nn.sigmoid(x)


def _softplus(x):
    return jnp.maximum(x, 0.0) + jnp.log(1.0 + jnp.exp(-jnp.abs(x)))


def _rowwise(name, fn, rows, bcast, row_outs, red_outs=(), bm=512):
    rows = [r if isinstance(r, tuple) else (r, r.shape[1], 0) for r in rows]
    s = rows[0][0].shape[0]
    bm = _pick(s, (bm, 128, 64, 32, 16, 8))
    nr, nb, no, nd = len(rows), len(bcast), len(row_outs), len(red_outs)
    in_specs = [pl.BlockSpec((bm, w), functools.partial(lambda i, ci: (i, ci), ci=ci)) for (_, w, ci) in rows]
    in_specs += [pl.BlockSpec(b.shape, lambda i: (0, 0)) for b in bcast]
    out_shape = [jax.ShapeDtypeStruct((s, w), dt) for (w, dt) in row_outs]
    out_shape += [jax.ShapeDtypeStruct((r, w), F32) for (r, w) in red_outs]
    out_specs = [pl.BlockSpec((bm, w), lambda i: (i, 0)) for (w, _) in row_outs]
    out_specs += [pl.BlockSpec((r, w), lambda i: (0, 0)) for (r, w) in red_outs]

    def body(*refs):
        ins = [r[...] for r in refs[:nr + nb]]
        outs = refs[nr + nb:nr + nb + no]
        reds = refs[nr + nb + no:]
        ov, rv = fn(*ins)
        for o, v in zip(outs, ov):
            o[...] = v.astype(o.dtype)
        if nd:
            @pl.when(pl.program_id(0) == 0)
            def _():
                for r in reds:
                    r[...] = jnp.zeros(r.shape, F32)
            for r, v in zip(reds, rv):
                r[...] += v.astype(F32)

    blk = sum(_nbytes((bm, w), a.dtype) for (a, w, _) in rows) + sum(_nbytes(b.shape, b.dtype) for b in bcast)
    blk += sum(_nbytes((bm, w), dt) for (w, dt) in row_outs) + sum(_nbytes(r, F32) for r in red_outs)
    res = pl.pallas_call(
        body, name=name, grid=(s // bm,), in_specs=in_specs, out_specs=out_specs, out_shape=out_shape,
        compiler_params=_params(3 * blk),
    )(*[a for (a, _, _) in rows], *bcast)
    return res


def _matmul(name, a, b, *, ta=False, tb=False, outs=(F32,), epi=None, epi_rows=(), epi_bcast=(),
            bm=None, bn=None, bk=None):
    if ta:
        k, m = a.shape
    else:
        m, k = a.shape
    n = b.shape[0] if tb else b.shape[1]
    assert (b.shape[1] if tb else b.shape[0]) == k, (name, a.shape, b.shape)
    if bm is None:
        bm = _pick(m, (1024, 1408, 768, 512, 384, 256, 128)) if ta else _pick(m, (1024, 512, 256, 128, 64, 32, 16))
    if bk is None:
        bk = k if k <= 3072 else _pick(k, (2816, 2048, 1024, 512))
        if ta:
            bk = _pick(k, (4096, 2048, 1024, 512, 256, 128, 64, 32, 16))
    if bn is None:
        bn = _pick(n, (1024, 768, 512, 384, 256, 128) if bk <= 2048 else (512, 384, 256, 128))
    nk = k // bk
    dims = TN if ta else (NT if tb else NN)
    a_spec = pl.BlockSpec((bk, bm), lambda i, j, kk: (kk, i)) if ta else pl.BlockSpec((bm, bk), lambda i, j, kk: (i, kk))
    b_spec = pl.BlockSpec((bn, bk), lambda i, j, kk: (j, kk)) if tb else pl.BlockSpec((bk, bn), lambda i, j, kk: (kk, j))
    in_specs = [a_spec, b_spec]
    in_specs += [pl.BlockSpec((bm, bn), lambda i, j, kk: (i, j)) for _ in epi_rows]
    in_specs += [pl.BlockSpec((1, bn), lambda i, j, kk: (0, j)) for _ in epi_bcast]
    out_shape = [jax.ShapeDtypeStruct((m, n), dt) for dt in outs]
    out_specs = [pl.BlockSpec((bm, bn), lambda i, j, kk: (i, j)) for _ in outs]
    ner, neb, no = len(epi_rows), len(epi_bcast), len(outs)

    def body(*refs):
        a_ref, b_ref = refs[0], refs[1]
        extra = refs[2:2 + ner + neb]
        out_refs = refs[2 + ner + neb:2 + ner + neb + no]
        prod = lax.dot_general(a_ref[...], b_ref[...], dims, preferred_element_type=F32)

        def finish(acc):
            vals = epi(acc, *[r[...] for r in extra]) if epi is not None else (acc,)
            for o, v in zip(out_refs, vals):
                o[...] = v.astype(o.dtype)

        if nk == 1:
            finish(prod)
        else:
            acc_ref = refs[-1]
            kk = pl.program_id(2)

            @pl.when(kk == 0)
            def _():
                acc_ref[...] = prod

            @pl.when(kk > 0)
            def _():
                acc_ref[...] += prod

            @pl.when(kk == nk - 1)
            def _():
                finish(acc_ref[...])

    blk = _nbytes((bm, bk), a.dtype) + _nbytes((bk, bn), b.dtype)
    blk += sum(_nbytes((bm, bn), r.dtype) for r in epi_rows) + sum(_nbytes((bm, bn), dt) for dt in outs)
    scratch = [pltpu.VMEM((bm, bn), F32)] if nk > 1 else []
    res = pl.pallas_call(
        body, name=name, grid=(m // bm, n // bn, nk), in_specs=in_specs, out_specs=out_specs,
        out_shape=out_shape, scratch_shapes=scratch,
        compiler_params=_params(blk, 3 * _nbytes((bm, bn), F32)),
    )(a, b, *epi_rows, *epi_bcast)
    return res


def _mm_core(name, grid, nk, pairs, out_defs, acc_shape, epi=None, epi_ins=()):
    npair, nep, no = len(pairs), len(epi_ins), len(out_defs)

    def body(*refs):
        extra = refs[2 * npair:2 * npair + nep]
        out_refs = refs[2 * npair + nep:2 * npair + nep + no]
        prod = None
        for p in range(npair):
            d = lax.dot_general(refs[2 * p][...], refs[2 * p + 1][...], pairs[p][4], preferred_element_type=F32)
            prod = d if prod is None else prod + d

        def finish(acc):
            vals = epi(acc, *[r[...] for r in extra]) if epi is not None else (acc,)
            for o, v in zip(out_refs, vals):
                o[...] = v.astype(o.dtype)

        if nk == 1:
            finish(prod)
        else:
            acc_ref = refs[-1]
            kk = pl.program_id(2)

            @pl.when(kk == 0)
            def _():
                acc_ref[...] = prod

            @pl.when(kk > 0)
            def _():
                acc_ref[...] += prod

            @pl.when(kk == nk - 1)
            def _():
                finish(acc_ref[...])

    def blk_bytes(spec, dtype):
        return _nbytes([s for s in spec.block_shape if s is not None], dtype)

    blk = sum(blk_bytes(sa, a.dtype) + blk_bytes(sb, b.dtype) for (a, sa, b, sb, _) in pairs)
    blk += sum(blk_bytes(sp, arr.dtype) for (arr, sp) in epi_ins) + sum(blk_bytes(sp, dt) for (_, dt, sp) in out_defs)
    ins, in_specs = [], []
    for (a, sa, b, sb, _) in pairs:
        ins += [a, b]
        in_specs += [sa, sb]
    ins += [arr for (arr, _) in epi_ins]
    in_specs += [sp for (_, sp) in epi_ins]
    return pl.pallas_call(
        body, name=name, grid=grid, in_specs=in_specs, out_specs=[sp for (_, _, sp) in out_defs],
        out_shape=[jax.ShapeDtypeStruct(sh, dt) for (sh, dt, _) in out_defs],
        scratch_shapes=[pltpu.VMEM(acc_shape, F32)] if nk > 1 else [],
        compiler_params=_params(blk, 3 * _nbytes(acc_shape, F32)),
    )(*ins)


def _rms_mod(h, ln, sh, sc):
    n = h * lax.rsqrt(jnp.mean(h * h, axis=-1, keepdims=True) + NORM_EPS) * ln
    return n * (1.0 + sc) + sh


def _swiglu_act(g, u):
    return _silu(g.astype(F32)) * u.astype(F32)


def _dn_prep(yc, pba, alog, dtb):
    act = _silu(yc)
    parts = []
    for idx in range(2 * DN_HEADS):
        seg = act[:, idx * DN_DIM:(idx + 1) * DN_DIM]
        seg = seg * lax.rsqrt(jnp.sum(seg * seg, axis=-1, keepdims=True) + NORM_EPS)
        if idx < DN_HEADS:
            seg = seg * (DN_DIM ** -0.5)
        parts.append(seg)
    parts.append(act[:, 2 * DN_WIDTH:])
    qkvn = jnp.concatenate(parts, axis=1)
    lane = lax.broadcasted_iota(jnp.int32, pba.shape, 1)
    beta = _sigmoid(pba)
    g = -jnp.exp(alog) * _softplus(pba + dtb)
    gb = jnp.where(lane < DN_HEADS, beta, jnp.where(lane < 2 * DN_HEADS, g, 0.0))
    return qkvn, gb


def _dn_outnorm(o_a, z, dn):
    parts = []
    for h in range(DN_HEADS):
        seg = o_a[:, h * DN_DIM:(h + 1) * DN_DIM]
        seg = seg * lax.rsqrt(jnp.mean(seg * seg, axis=-1, keepdims=True) + NORM_EPS) * dn
        parts.append(seg)
    return jnp.concatenate(parts, axis=1) * _silu(z)


def _shift_down(x, halo8, s):
    r = pltpu.roll(x, s, axis=0)
    top = pltpu.roll(halo8, s, axis=0)
    i8 = lax.broadcasted_iota(jnp.int32, top.shape, 0)
    return jnp.concatenate([jnp.where(i8 < s, top, r[0:8]), r[8:]], axis=0)


def _shift_up(x, halo8, s):
    m = x.shape[0]
    r = pltpu.roll(x, m - s, axis=0)
    bot = pltpu.roll(halo8, 8 - s, axis=0)
    i8 = lax.broadcasted_iota(jnp.int32, bot.shape, 0)
    return jnp.concatenate([r[:m - 8], jnp.where(i8 >= 8 - s, bot, r[m - 8:])], axis=0)


def _conv_prep_fwd(name, pq, convw8, pba, alog, dtb, bm=256):
    s, w = pq.shape
    nblk = s // bm
    hb = bm // 16

    def body(x_ref, halo_ref, w_ref, pba_ref, alog_ref, dtb_ref, yc_ref, qkv_ref, gb_ref):
        i = pl.program_id(0)
        x = x_ref[...].astype(F32)
        halo = jnp.where(i > 0, halo_ref[...].astype(F32)[8:16], 0.0)
        cw = w_ref[...]
        y = x * cw[DN_CONV - 1:DN_CONV]
        for sft in range(1, DN_CONV):
            y = y + _shift_down(x, halo, sft) * cw[DN_CONV - 1 - sft:DN_CONV - sft]
        ycb = y.astype(BF16)
        yc_ref[...] = ycb
        qkvn, gb = _dn_prep(ycb.astype(F32), pba_ref[...], alog_ref[...], dtb_ref[...])
        qkv_ref[...] = qkvn.astype(BF16)
        gb_ref[...] = gb

    blk = 3 * _nbytes((bm, w), BF16) + 4 * _nbytes((bm, w), F32)
    return pl.pallas_call(
        body, name=name, grid=(nblk,),
        in_specs=[pl.BlockSpec((bm, w), lambda i: (i, 0)),
                  pl.BlockSpec((16, w), lambda i: (jnp.maximum(i * hb - 1, 0), 0)),
                  pl.BlockSpec(convw8.shape, lambda i: (0, 0)),
                  pl.BlockSpec((bm, LANES), lambda i: (i, 0)),
                  pl.BlockSpec((1, LANES), lambda i: (0, 0)),
                  pl.BlockSpec((1, LANES), lambda i: (0, 0))],
        out_specs=[pl.BlockSpec((bm, w), lambda i: (i, 0)), pl.BlockSpec((bm, w), lambda i: (i, 0)),
                   pl.BlockSpec((bm, LANES), lambda i: (i, 0))],
        out_shape=[jax.ShapeDtypeStruct((s, w), BF16), jax.ShapeDtypeStruct((s, w), BF16),
                   jax.ShapeDtypeStruct((s, LANES), F32)],
        compiler_params=_params(blk),
    )(pq, pq, convw8, pba, alog, dtb)


def _conv_bwd(name, dyc, pq, convw8, bm=256):
    s, w = pq.shape
    nblk = s // bm
    hb = bm // 16

    def body(dy_ref, dyn_ref, x_ref, xh_ref, w_ref, dx_ref, dw_ref):
        i = pl.program_id(0)
        dy = dy_ref[...].astype(F32)
        nxt = jnp.where(i < nblk - 1, dyn_ref[...].astype(F32)[0:8], 0.0)
        x = x_ref[...].astype(F32)
        halo = jnp.where(i > 0, xh_ref[...].astype(F32)[8:16], 0.0)
        cw = w_ref[...]
        dx = dy * cw[DN_CONV - 1:DN_CONV]
        for sft in range(1, DN_CONV):
            dx = dx + _shift_up(dy, nxt, sft) * cw[DN_CONV - 1 - sft:DN_CONV - sft]
        dx_ref[...] = dx.astype(dx_ref.dtype)
        r8 = lax.broadcasted_iota(jnp.int32, (8, w), 0)
        dw = jnp.zeros((8, w), F32)
        for j in range(DN_CONV):
            sft = DN_CONV - 1 - j
            xs = x if sft == 0 else _shift_down(x, halo, sft)
            dw = dw + jnp.where(r8 == j, jnp.sum(dy * xs, axis=0, keepdims=True), 0.0)

        @pl.when(i == 0)
        def _():
            dw_ref[...] = jnp.zeros((8, w), F32)
        dw_ref[...] += dw

    blk = 4 * _nbytes((bm, w), BF16) + 5 * _nbytes((bm, w), F32)
    return pl.pallas_call(
        body, name=name, grid=(nblk,),
        in_specs=[pl.BlockSpec((bm, w), lambda i: (i, 0)),
                  pl.BlockSpec((16, w), lambda i: (jnp.minimum((i + 1) * hb, s // 16 - 1), 0)),
                  pl.BlockSpec((bm, w), lambda i: (i, 0)),
                  pl.BlockSpec((16, w), lambda i: (jnp.maximum(i * hb - 1, 0), 0)),
                  pl.BlockSpec(convw8.shape, lambda i: (0, 0))],
        out_specs=[pl.BlockSpec((bm, w), lambda i: (i, 0)), pl.BlockSpec((8, w), lambda i: (0, 0))],
        out_shape=[jax.ShapeDtypeStruct((s, w), BF16), jax.ShapeDtypeStruct((8, w), F32)],
        compiler_params=_params(blk),
    )(dyc, dyc, pq, pq, convw8)


BNN = (((2,), (1,)), ((0,), (0,)))
BNT = (((2,), (2,)), ((0,), (0,)))
BTN = (((1,), (1,)), ((0,), (0,)))


def _raw_dot_1pass(a, b, dims):
    return lax.dot_general(a.astype(BF16), b.astype(BF16), dims, preferred_element_type=F32)


def _raw_dot_3pass(a, b, dims):
    ah = a.astype(BF16)
    al = (a - ah.astype(F32)).astype(BF16)
    bh = b.astype(BF16)
    bl = (b - bh.astype(F32)).astype(BF16)
    d = lambda x, y: lax.dot_general(x, y, dims, preferred_element_type=F32)
    return d(ah, bh) + (d(ah, bl) + d(al, bh))


def _with_same_precision_vjp(raw):
    @functools.partial(jax.custom_vjp, nondiff_argnums=(2,))
    def dot(a, b, dims):
        return raw(a, b, dims)

    def fwd(a, b, dims):
        return raw(a, b, dims), (a, b)

    def bwd(dims, res, ct):
        a, b = res
        if dims == BNN:
            return raw(ct, b, BNT), raw(a, ct, BTN)
        if dims == BNT:
            return raw(ct, b, BNN), raw(ct, a, BTN)
        assert dims == BTN
        return raw(b, ct, BNT), raw(a, ct, BNN)

    dot.defvjp(fwd, bwd)
    return dot


_dot_1pass_vjp = _with_same_precision_vjp(_raw_dot_1pass)
_dot_3pass_vjp = _with_same_precision_vjp(_raw_dot_3pass)


def _dot_bf16(a, b, dims=BNN):
    return _dot_1pass_vjp(a, b, dims)


def _dot_3pass(a, b, dims=BNN):
    return _dot_3pass_vjp(a, b, dims)


def _neumann_inverse(x):
    h, c, _ = x.shape
    eye = lax.broadcasted_iota(jnp.int32, (h, c, c), 1) == lax.broadcasted_iota(jnp.int32, (h, c, c), 2)
    t = jnp.where(eye, 1.0, 0.0) + x
    p = x
    for _ in range(5):
        p = _raw_dot_3pass(p, p, BNN)
        t = t + _raw_dot_3pass(t, p, BNN)
    return t


@jax.custom_vjp
def _known_inverse(x, t):
    return t


def _known_inverse_fwd(x, t):
    return t, t


def _known_inverse_bwd(t, ct):
    return _raw_dot_3pass(_raw_dot_3pass(t, ct, BTN), t, BNT), jnp.zeros_like(t)


_known_inverse.defvjp(_known_inverse_fwd, _known_inverse_bwd)


def _delta_chunk(q, k, v, gcol, bcol, state, t_known=None):
    h, c, _ = q.shape
    row = lax.broadcasted_iota(jnp.int32, (h, c, c), 1)
    col = lax.broadcasted_iota(jnp.int32, (h, c, c), 2)
    incl, strict, eye = row >= col, row > col, row == col
    g_b = jnp.broadcast_to(gcol, (h, c, c))
    gc_row = jnp.sum(jnp.where(row <= col, g_b, 0.0), axis=1, keepdims=True)
    g_r = jnp.sum(jnp.where(eye, g_b, 0.0), axis=1, keepdims=True)
    gc_col = jnp.sum(jnp.where(incl, jnp.broadcast_to(g_r, (h, c, c)), 0.0), axis=2, keepdims=True)
    decay = jnp.exp(jnp.where(incl, gc_col - gc_row, NEG))
    kb = k * bcol
    vb = v * bcol
    x = -jnp.where(strict, _dot_bf16(kb, k, BNT) * decay, 0.0)
    t = _neumann_inverse(x) if t_known is None else _known_inverse(x, t_known)
    eg = jnp.exp(gc_col)
    u = _dot_3pass(t, vb)
    w = _dot_3pass(t, kb * eg)
    qk = _dot_bf16(q, k, BNT) * decay
    v_new = u - _dot_bf16(w, state)
    o = _dot_bf16(q * eg, state) + _dot_bf16(qk, v_new)
    g_last = jnp.sum(g_r, axis=2, keepdims=True)
    new_state = state * jnp.exp(g_last) + _dot_bf16(k * jnp.exp(g_last - gc_col), v_new, BTN)
    return o, new_state, t


def _lane_col(blk, idx):
    lane = lax.broadcasted_iota(jnp.int32, blk.shape, 1)
    return jnp.sum(jnp.where(lane == idx, blk, 0.0), axis=1, keepdims=True)


def _dn_heads(ref, base):
    return jnp.stack([ref[:, base + h * DN_DIM:base + (h + 1) * DN_DIM] for h in range(DN_HEADS)], axis=0).astype(F32)


def _dn_cols(gbv, base):
    return jnp.stack([_lane_col(gbv, base + h) for h in range(DN_HEADS)], axis=0)


def _delta_fwd(name, qkvn, gb):
    s = qkvn.shape[0]
    n = s // DN_CHUNK
    c = DN_CHUNK

    def body(qkv_ref, gb_ref, o_ref, st_ref, t_ref, state):
        @pl.when(pl.program_id(0) == 0)
        def _():
            state[...] = jnp.zeros(state.shape, F32)

        gbv = gb_ref[...]
        st = state[...]
        st_ref[0] = st
        o, new, t = _delta_chunk(_dn_heads(qkv_ref, 0), _dn_heads(qkv_ref, DN_WIDTH), _dn_heads(qkv_ref, 2 * DN_WIDTH),
                                 _dn_cols(gbv, DN_HEADS), _dn_cols(gbv, 0), st)
        for h in range(DN_HEADS):
            o_ref[:, h * DN_DIM:(h + 1) * DN_DIM] = o[h]
        t_ref[0] = t
        state[...] = new

    blk = _nbytes((c, 3 * DN_WIDTH), BF16) + _nbytes((c, LANES), F32) + _nbytes((c, DN_WIDTH), F32)
    blk += _nbytes((DN_HEADS, DN_DIM, DN_DIM), F32) + _nbytes((DN_HEADS, c, c), F32)
    return pl.pallas_call(
        body, name=name, grid=(n,),
        in_specs=[pl.BlockSpec((c, 3 * DN_WIDTH), lambda i: (i, 0)), pl.BlockSpec((c, LANES), lambda i: (i, 0))],
        out_specs=[pl.BlockSpec((c, DN_WIDTH), lambda i: (i, 0)),
                   pl.BlockSpec((1, DN_HEADS, DN_DIM, DN_DIM), lambda i: (i, 0, 0, 0)),
                   pl.BlockSpec((1, DN_HEADS, c, c), lambda i: (i, 0, 0, 0))],
        out_shape=[jax.ShapeDtypeStruct((s, DN_WIDTH), F32),
                   jax.ShapeDtypeStruct((n, DN_HEADS, DN_DIM, DN_DIM), F32),
                   jax.ShapeDtypeStruct((n, DN_HEADS, c, c), F32)],
        scratch_shapes=[pltpu.VMEM((DN_HEADS, DN_DIM, DN_DIM), F32)],
        compiler_params=_params(blk, 8 << 20),
    )(qkvn, gb)


def _delta_bwd(name, qkvn, gb, states, tinv, d_o):
    s = qkvn.shape[0]
    n = s // DN_CHUNK
    c = DN_CHUNK

    def body(qkv_ref, gb_ref, st_ref, t_ref, do_ref, dqkv_ref, dgb_ref, dstate):
        @pl.when(pl.program_id(0) == 0)
        def _():
            dstate[...] = jnp.zeros(dstate.shape, F32)

        gbv = gb_ref[...]
        lane = lax.broadcasted_iota(jnp.int32, (c, LANES), 1)
        t_known = t_ref[0]
        chunk = lambda *args: _delta_chunk(*args, t_known=t_known)[:2]
        _, vjp = jax.vjp(chunk, _dn_heads(qkv_ref, 0), _dn_heads(qkv_ref, DN_WIDTH),
                         _dn_heads(qkv_ref, 2 * DN_WIDTH), _dn_cols(gbv, DN_HEADS), _dn_cols(gbv, 0), st_ref[0])
        dq, dk, dv, dg, db, dst = vjp((_dn_heads(do_ref, 0), dstate[...]))
        dgb = jnp.zeros((c, LANES), F32)
        for h in range(DN_HEADS):
            dqkv_ref[:, h * DN_DIM:(h + 1) * DN_DIM] = dq[h]
            dqkv_ref[:, DN_WIDTH + h * DN_DIM:DN_WIDTH + (h + 1) * DN_DIM] = dk[h]
            dqkv_ref[:, 2 * DN_WIDTH + h * DN_DIM:2 * DN_WIDTH + (h + 1) * DN_DIM] = dv[h]
            dgb = dgb + jnp.where(lane == h, db[h], 0.0) + jnp.where(lane == DN_HEADS + h, dg[h], 0.0)
        dstate[...] = dst
        dgb_ref[...] = dgb

    rev = lambda i: (n - 1 - i, 0)
    blk = _nbytes((c, 3 * DN_WIDTH), BF16) + 2 * _nbytes((c, LANES), F32) + _nbytes((c, DN_WIDTH), F32)
    blk += _nbytes((DN_HEADS, DN_DIM, DN_DIM), F32) + _nbytes((c, 3 * DN_WIDTH), F32)
    return pl.pallas_call(
        body, name=name, grid=(n,),
        in_specs=[pl.BlockSpec((c, 3 * DN_WIDTH), rev), pl.BlockSpec((c, LANES), rev),
                  pl.BlockSpec((1, DN_HEADS, DN_DIM, DN_DIM), lambda i: (n - 1 - i, 0, 0, 0)),
                  pl.BlockSpec((1, DN_HEADS, c, c), lambda i: (n - 1 - i, 0, 0, 0)),
                  pl.BlockSpec((c, DN_WIDTH), rev)],
        out_specs=[pl.BlockSpec((c, 3 * DN_WIDTH), rev), pl.BlockSpec((c, LANES), rev)],
        out_shape=[jax.ShapeDtypeStruct((s, 3 * DN_WIDTH), F32), jax.ShapeDtypeStruct((s, LANES), F32)],
        scratch_shapes=[pltpu.VMEM((DN_HEADS, DN_DIM, DN_DIM), F32)],
        compiler_params=_params(blk, 16 << 20),
    )(qkvn, gb, states, tinv, d_o)


def _da_scores(q2f, k2, sub, valid, distf, head):
    lane = lax.broadcasted_iota(jnp.int32, q2f.shape, 1)
    hmask = (lane < DA_DIM) if sub == 0 else (lane >= DA_DIM)
    qm = jnp.where(hmask, q2f, 0.0).astype(BF16)
    slope = 2.0 ** (-ALIBI_MAX_EXP * (head + 1) / DA_HEADS)
    sc = lax.dot_general(qm, k2, NT, preferred_element_type=F32) * (DA_DIM ** -0.5)
    return jnp.where(valid, sc - slope * distf, NEG), qm, hmask


def _da_mask(i, r):
    qi = lax.broadcasted_iota(jnp.int32, (DA_BLOCK, 2 * DA_BLOCK), 0)
    ki = lax.broadcasted_iota(jnp.int32, (DA_BLOCK, 2 * DA_BLOCK), 1)
    dist = qi + DA_BLOCK - ki
    valid = (dist >= 0) & (dist <= DA_BLOCK) & ((ki >= DA_BLOCK) | (i > 0))
    return valid, (dist * r).astype(F32)


def _da_fwd(name, pda, r):
    s = pda.shape[0]
    n = s // r
    nb = n // DA_BLOCK
    w = DA_WIDTH
    dav = pda.reshape(n, r * 3 * w)

    def body(q_ref, kc_ref, kp_ref, vc_ref, vp_ref, o_ref, lse_ref):
        i = pl.program_id(1)
        valid, distf = _da_mask(i, r)
        lane = lax.broadcasted_iota(jnp.int32, (DA_BLOCK, LANES), 1)
        lse = jnp.zeros((DA_BLOCK, LANES), F32)
        for hp in range(DA_HEADS // 2):
            sl = slice(hp * LANES, (hp + 1) * LANES)
            q2f = q_ref[:, sl].astype(F32)
            k2 = jnp.concatenate([kp_ref[:, sl], kc_ref[:, sl]], axis=0)
            v2 = jnp.concatenate([vp_ref[:, sl], vc_ref[:, sl]], axis=0)
            o2 = None
            for sub in range(2):
                head = 2 * hp + sub
                sc, _, hmask = _da_scores(q2f, k2, sub, valid, distf, head)
                mx = jnp.max(sc, axis=1, keepdims=True)
                p = jnp.exp(sc - mx)
                l = jnp.sum(p, axis=1, keepdims=True)
                pv = lax.dot_general(p.astype(BF16), v2, NN, preferred_element_type=F32) / l
                o2 = pv if sub == 0 else jnp.where(hmask, pv, o2)
                lse = jnp.where(lane == head, mx + jnp.log(l), lse)
            o_ref[:, sl] = o2.astype(o_ref.dtype)
        lse_ref[...] = lse

    prev = lambda col: (lambda p, i: (jnp.maximum(i - 1, 0), 3 * p + col))
    cur = lambda col: (lambda p, i: (i, 3 * p + col))
    blk = 5 * _nbytes((DA_BLOCK, w), BF16) + _nbytes((DA_BLOCK, w), F32) + _nbytes((DA_BLOCK, LANES), F32)
    o, lse = pl.pallas_call(
        body, name=name, grid=(r, nb),
        in_specs=[pl.BlockSpec((DA_BLOCK, w), cur(0)), pl.BlockSpec((DA_BLOCK, w), cur(1)),
                  pl.BlockSpec((DA_BLOCK, w), prev(1)), pl.BlockSpec((DA_BLOCK, w), cur(2)),
                  pl.BlockSpec((DA_BLOCK, w), prev(2))],
        out_specs=[pl.BlockSpec((DA_BLOCK, w), lambda p, i: (i, p)),
                   pl.BlockSpec((DA_BLOCK, LANES), lambda p, i: (i, p))],
        out_shape=[jax.ShapeDtypeStruct((n, r * w), BF16), jax.ShapeDtypeStruct((n, r * LANES), F32)],
        compiler_params=_params(blk, 8 << 20),
    )(dav, dav, dav, dav, dav)
    return o.reshape(s, w), lse.reshape(s, LANES)


def _da_bwd(name, pda, d_ob, lse_tot, delta, r):
    s = pda.shape[0]
    n = s // r
    nb = n // DA_BLOCK
    w = DA_WIDTH
    dav = pda.reshape(n, r * 3 * w)
    dov = d_ob.reshape(n, r * w)
    lv = lse_tot.reshape(n, r * LANES)
    dlv = delta.reshape(n, r * LANES)

    def body(q_ref, kc_ref, kp_ref, vc_ref, vp_ref, do_ref, l_ref, dl_ref, dq_ref, dk_ref, dv_ref, ck, cv):
        i = pl.program_id(1)

        @pl.when(i == 0)
        def _():
            ck[...] = jnp.zeros(ck.shape, F32)
            cv[...] = jnp.zeros(cv.shape, F32)

        @pl.when(i < nb)
        def _():
            valid, distf = _da_mask(i, r)
            lsev = l_ref[...]
            dlt = dl_ref[...]
            for hp in range(DA_HEADS // 2):
                sl = slice(hp * LANES, (hp + 1) * LANES)
                q2f = q_ref[:, sl].astype(F32)
                k2 = jnp.concatenate([kp_ref[:, sl], kc_ref[:, sl]], axis=0)
                v2 = jnp.concatenate([vp_ref[:, sl], vc_ref[:, sl]], axis=0)
                do2f = do_ref[:, sl].astype(F32)
                dq2 = jnp.zeros((DA_BLOCK, LANES), F32)
                dk2 = jnp.zeros((2 * DA_BLOCK, LANES), F32)
                dv2 = jnp.zeros((2 * DA_BLOCK, LANES), F32)
                for sub in range(2):
                    head = 2 * hp + sub
                    sc, qm, hmask = _da_scores(q2f, k2, sub, valid, distf, head)
                    p = jnp.exp(sc - _lane_col(lsev, head))
                    dom = jnp.where(hmask, do2f, 0.0).astype(BF16)
                    dp = lax.dot_general(dom, v2, NT, preferred_element_type=F32)
                    ds = (p * (dp - _lane_col(dlt, head)) * (DA_DIM ** -0.5)).astype(BF16)
                    dq2 = dq2 + jnp.where(hmask, lax.dot_general(ds, k2, NN, preferred_element_type=F32), 0.0)
                    dk2 = dk2 + lax.dot_general(ds, qm, TN, preferred_element_type=F32)
                    dv2 = dv2 + lax.dot_general(p.astype(BF16), dom, TN, preferred_element_type=F32)
                dq_ref[:, sl] = dq2.astype(dq_ref.dtype)
                dk_ref[:, sl] = (ck[:, sl] + dk2[:DA_BLOCK]).astype(dk_ref.dtype)
                dv_ref[:, sl] = (cv[:, sl] + dv2[:DA_BLOCK]).astype(dv_ref.dtype)
                ck[:, sl] = dk2[DA_BLOCK:]
                cv[:, sl] = dv2[DA_BLOCK:]

        @pl.when(i == nb)
        def _():
            dk_ref[...] = ck[...].astype(dk_ref.dtype)
            dv_ref[...] = cv[...].astype(dv_ref.dtype)

    qrow = lambda i: jnp.minimum(i, nb - 1)
    prev = lambda col: (lambda p, i: (jnp.maximum(qrow(i) - 1, 0), 3 * p + col))
    cur = lambda col: (lambda p, i: (qrow(i), 3 * p + col))
    same = lambda p, i: (qrow(i), p)
    late = lambda p, i: (jnp.maximum(i - 1, 0), p)
    blk = 6 * _nbytes((DA_BLOCK, w), BF16) + 2 * _nbytes((DA_BLOCK, LANES), F32) + 3 * _nbytes((DA_BLOCK, w), F32)
    dq, dk, dv = pl.pallas_call(
        body, name=name, grid=(r, nb + 1),
        in_specs=[pl.BlockSpec((DA_BLOCK, w), cur(0)), pl.BlockSpec((DA_BLOCK, w), cur(1)),
                  pl.BlockSpec((DA_BLOCK, w), prev(1)), pl.BlockSpec((DA_BLOCK, w), cur(2)),
                  pl.BlockSpec((DA_BLOCK, w), prev(2)), pl.BlockSpec((DA_BLOCK, w), same),
                  pl.BlockSpec((DA_BLOCK, LANES), same), pl.BlockSpec((DA_BLOCK, LANES), same)],
        out_specs=[pl.BlockSpec((DA_BLOCK, w), same), pl.BlockSpec((DA_BLOCK, w), late),
                   pl.BlockSpec((DA_BLOCK, w), late)],
        out_shape=[jax.ShapeDtypeStruct((n, r * w), BF16)] * 3,
        scratch_shapes=[pltpu.VMEM((DA_BLOCK, w), F32), pltpu.VMEM((DA_BLOCK, w), F32)],
        compiler_params=_params(blk, 12 << 20),
    )(dav, dav, dav, dav, dav, dov, lv, dlv)
    return dq.reshape(s, w), dk.reshape(s, w), dv.reshape(s, w)


def _head_expand():
    hrow = lax.broadcasted_iota(jnp.int32, (LANES, DA_WIDTH), 0)
    lcol = lax.broadcasted_iota(jnp.int32, (LANES, DA_WIDTH), 1)
    return jnp.where(lcol // DA_DIM == hrow, 1.0, 0.0).astype(F32)


def _ffn_up(name, h, ln, sh, sc, ga, tg, tu):
    s, d = h.shape
    nsh, _, _, ffs = ga.shape
    bm = _pick(s, (1024, 512, 256, 128))

    def body(h_ref, ln_ref, sh_ref, sc_ref, wg_ref, wu_ref, a_ref, g_ref, u_ref, f_ref, a_vmem):
        @pl.when(pl.program_id(1) == 0)
        def _():
            av = _rms_mod(h_ref[...], ln_ref[...], sh_ref[...], sc_ref[...]).astype(BF16)
            a_vmem[...] = av
            a_ref[...] = av

        av = a_vmem[...]
        g = lax.dot_general(av, wg_ref[...], NN, preferred_element_type=F32)
        u = lax.dot_general(av, wu_ref[...], NN, preferred_element_type=F32)
        g_ref[...] = g.astype(BF16)
        u_ref[...] = u.astype(BF16)
        f_ref[...] = (_silu(g) * u).astype(BF16)

    wspec = lambda t: pl.BlockSpec((None, None, d, ffs), lambda i, j: (j, t, 0, 0))
    ospec = pl.BlockSpec((None, bm, ffs), lambda i, j: (j, i, 0))
    rows = pl.BlockSpec((bm, d), lambda i, j: (i, 0))
    vec = pl.BlockSpec((1, d), lambda i, j: (0, 0))
    blk = _nbytes((bm, d), F32) + _nbytes((bm, d), BF16) + 2 * _nbytes((d, ffs), BF16) + 3 * _nbytes((bm, ffs), BF16)
    return pl.pallas_call(
        body, name=name, grid=(s // bm, nsh),
        in_specs=[rows, vec, vec, vec, wspec(tg), wspec(tu)],
        out_specs=[rows] + [ospec] * 3,
        out_shape=[jax.ShapeDtypeStruct((s, d), BF16)] + [jax.ShapeDtypeStruct((nsh, s, ffs), BF16)] * 3,
        scratch_shapes=[pltpu.VMEM((bm, d), BF16)],
        compiler_params=_params(blk, 4 * _nbytes((bm, ffs), F32) + 3 * _nbytes((bm, d), F32)),
    )(h, ln, sh, sc, ga, ga)


def _ffn_fwd(tag, h_in, ln, sh, sc, gt, ga, tg, tu, gb, td, weight):
    s, d = h_in.shape
    nsh, _, ffs, _ = gb.shape
    a, g, u, f = _ffn_up(tag + "_up", h_in, ln, sh, sc, ga, tg, tu)
    bm, bn = _pick(s, (1024, 512, 256, 128)), _pick(d, (1024, 512, 256, 128))
    io = pl.BlockSpec((bm, bn), lambda i, j, kk: (i, j))
    h_out, o = _mm_core(
        tag + "_down", (s // bm, d // bn, nsh), nsh,
        [(f, pl.BlockSpec((None, bm, ffs), lambda i, j, kk: (kk, i, 0)),
          gb, pl.BlockSpec((None, None, ffs, bn), lambda i, j, kk: (kk, td, 0, j)), NN)],
        [((s, d), F32, io), ((s, d), BF16, io)], (bm, bn),
        epi=lambda acc, h, gv: (h + weight * gv * acc, acc),
        epi_ins=[(h_in, io), (gt, pl.BlockSpec((1, bn), lambda i, j, kk: (0, j)))])
    return h_out, dict(a=a, g=g, u=u, f=f, o=o)


def _resid_bwd(tag, dh_out, o, gt, weight):
    d = dh_out.shape[1]

    def fn(dh, ov, g):
        return (weight * g * dh,), (jnp.sum(weight * dh * ov.astype(F32), axis=0, keepdims=True),)

    do, d_gt = _rowwise(tag + "_resid_bwd", fn, [dh_out, o], [gt], [(d, BF16)], [(1, d)])
    return do, d_gt


def _norm_bwd(tag, h_in, da, dh_out, ln, sh, sc):
    d = h_in.shape[1]

    def fn(h, dav, dh, l, s1, s2):
        _, vjp = jax.vjp(_rms_mod, h, l, s1, s2)
        gh, gl, gs1, gs2 = vjp(dav)
        return (dh + gh,), (gl, gs1, gs2)

    return _rowwise(tag + "_norm_bwd", fn, [h_in, da, dh_out], [ln, sh, sc], [(d, F32)], [(1, d)] * 3)


def _ffn_bwd(tag, h_in, dh_out, sv, ln, sh, sc, gt, ga, tg, tu, gb, td, weight):
    s, d = h_in.shape
    nsh, _, ffs, _ = gb.shape
    bm, bn = _pick(s, (1024, 512, 256, 128)), _pick(d, (1024, 512, 256, 128))
    bk = _pick(s, (4096, 2048, 1024, 512, 256, 128))
    do, d_gt = _resid_bwd(tag, dh_out, sv["o"], gt, weight)

    def act_bwd(df, g, u):
        _, vjp = jax.vjp(_swiglu_act, g, u)
        return vjp(df)

    hid = pl.BlockSpec((None, bm, ffs), lambda i, j, kk: (j, i, 0))
    dg, du = _mm_core(
        tag + "_down_dx", (s // bm, nsh, 1), 1,
        [(do, pl.BlockSpec((bm, d), lambda i, j, kk: (i, 0)),
          gb, pl.BlockSpec((None, None, ffs, d), lambda i, j, kk: (j, td, 0, 0)), NT)],
        [((nsh, s, ffs), BF16, hid)] * 2, (bm, ffs), epi=act_bwd, epi_ins=[(sv["g"], hid), (sv["u"], hid)])
    (d_wd,) = _mm_core(
        tag + "_down_dw", (nsh, d // bn, s // bk), s // bk,
        [(sv["f"], pl.BlockSpec((None, bk, ffs), lambda i, j, kk: (i, kk, 0)),
          do, pl.BlockSpec((bk, bn), lambda i, j, kk: (kk, j)), TN)],
        [((nsh, ffs, d), BF16, pl.BlockSpec((None, ffs, bn), lambda i, j, kk: (i, 0, j)))], (ffs, bn))
    kmaj = pl.BlockSpec((None, bm, ffs), lambda i, j, kk: (kk, i, 0))
    wsp = lambda t: pl.BlockSpec((None, None, bn, ffs), functools.partial(lambda i, j, kk, t: (kk, t, j, 0), t=t))
    (da,) = _mm_core(
        tag + "_up_dx", (s // bm, d // bn, nsh), nsh, [(dg, kmaj, ga, wsp(tg), NT), (du, kmaj, ga, wsp(tu), NT)],
        [((s, d), F32, pl.BlockSpec((bm, bn), lambda i, j, kk: (i, j)))], (bm, bn))
    dws = []
    for nm, dh in (("_wg_dw", dg), ("_wu_dw", du)):
        (dw,) = _mm_core(
            tag + nm, (1, nsh, s // bk), s // bk,
            [(sv["a"], pl.BlockSpec((bk, d), lambda i, j, kk: (kk, 0)),
              dh, pl.BlockSpec((None, bk, ffs), lambda i, j, kk: (j, kk, 0)), TN)],
            [((nsh, d, ffs), BF16, pl.BlockSpec((None, d, ffs), lambda i, j, kk: (j, 0, 0)))], (d, ffs))
        dws.append(dw)
    dh_in, d_ln, d_sh, d_sc = _norm_bwd(tag, h_in, da, dh_out, ln, sh, sc)
    return dh_in, dict(wg=dws[0], wu=dws[1], wd=d_wd), dict(ln=d_ln, sh=d_sh, sc=d_sc, gt=d_gt)


def _mixer_fwd(tag, h_in, ln, sh, sc, gt, w, sp):
    d = h_in.shape[1]
    (a,) = _rowwise(tag + "_norm", lambda h, l, s1, s2: ((_rms_mod(h, l, s1, s2),), ()), [h_in], [ln, sh, sc],
                    [(d, BF16)])
    (pq,) = _matmul(tag + "_pq", a, w["wq"], outs=(BF16,))
    (pz,) = _matmul(tag + "_pz", a, w["wz"], outs=(BF16,))
    (pba,) = _matmul(tag + "_pba", a, w["wba"])
    (pda,) = _matmul(tag + "_pda", a, w["wda"], outs=(BF16,))
    (pg,) = _matmul(tag + "_pg", a, w["wg"], outs=(BF16,))
    yc, qkvn, gb = _conv_prep_fwd(tag + "_conv", pq, sp["conv8"], pba, sp["alog"], sp["dtb"])
    o_a, states, tinv = _delta_fwd(tag + "_delta", qkvn, gb)
    (o_an,) = _rowwise(tag + "_dnorm", lambda o, z, dn: ((_dn_outnorm(o, z.astype(F32), dn),), ()), [o_a, pz],
                       [sp["dn"]], [(DN_WIDTH, BF16)])
    ops, lses = [], []
    for (_, r) in DA_PATTERNS:
        o_p, lse_p = _da_fwd(f"{tag}_da{r}", pda, r)
        ops.append(o_p)
        lses.append(lse_p)

    def merge(o1, o2, o3, l1, l2, l3):
        mx = jnp.maximum(jnp.maximum(l1, l2), l3)
        e1, e2, e3 = jnp.exp(l1 - mx), jnp.exp(l2 - mx), jnp.exp(l3 - mx)
        tot = e1 + e2 + e3
        ex = _head_expand()
        up = lambda wgt: lax.dot_general(wgt / tot, ex, NN, precision=HI, preferred_element_type=F32)
        return (up(e1) * o1 + up(e2) * o2 + up(e3) * o3, mx + jnp.log(tot)), ()

    o_b, lse_tot = _rowwise(tag + "_merge", merge, ops + lses, [], [(DA_WIDTH, BF16), (LANES, F32)])
    (y_a,) = _matmul(tag + "_wa", o_an, w["w_a"], outs=(BF16,))
    (y_b,) = _matmul(tag + "_wb", o_b, w["w_b"], outs=(BF16,))

    def gate(ga, gbv, ya, yb):
        return _sigmoid(ga.astype(F32)) * ya.astype(F32) + _sigmoid(gbv.astype(F32)) * yb.astype(F32)

    (merged,) = _rowwise(tag + "_gate", lambda *v: ((gate(*v),), ()), [(pg, d, 0), (pg, d, 1), y_a, y_b], [],
                         [(d, BF16)])
    h_out, m = _matmul(tag + "_wo", merged, w["w_o"], outs=(F32, BF16), epi_rows=[h_in], epi_bcast=[gt],
                       epi=lambda acc, h, g: (h + g * acc, acc))
    sv = dict(a=a, pq=pq, pz=pz, pba=pba, pda=pda, pg=pg, yc=yc, qkvn=qkvn, gb=gb, o_a=o_a, states=states, tinv=tinv,
              o_an=o_an, o_b=o_b, lse=lse_tot, y_a=y_a, y_b=y_b, merged=merged, m=m, gate=gate)
    return h_out, sv


def _mixer_bwd(tag, h_in, dh_out, sv, ln, sh, sc, gt, w, sp):
    d = h_in.shape[1]
    dm, d_gt = _resid_bwd(tag, dh_out, sv["m"], gt, 1.0)
    (d_merged,) = _matmul(tag + "_wo_dx", dm, w["w_o"], tb=True, outs=(BF16,))
    (d_wo,) = _matmul(tag + "_wo_dw", sv["merged"], dm, ta=True, outs=(BF16,))
    gate = sv["gate"]

    def gate_bwd(dmg, ga, gbv, ya, yb):
        _, vjp = jax.vjp(gate, ga.astype(F32), gbv.astype(F32), ya.astype(F32), yb.astype(F32))
        dga, dgb, dya, dyb = vjp(dmg.astype(F32))
        return (jnp.concatenate([dga, dgb], axis=1), dya, dyb), ()

    pg = sv["pg"]
    d_pg, d_ya, d_yb = _rowwise(tag + "_gate_bwd", gate_bwd, [d_merged, (pg, d, 0), (pg, d, 1), sv["y_a"], sv["y_b"]],
                                [], [(2 * d, BF16), (d, BF16), (d, BF16)])
    (d_oan,) = _matmul(tag + "_wa_dx", d_ya, w["w_a"], tb=True)
    (d_wa,) = _matmul(tag + "_wa_dw", sv["o_an"], d_ya, ta=True, outs=(BF16,))
    (d_ob,) = _matmul(tag + "_wb_dx", d_yb, w["w_b"], tb=True, outs=(BF16,))
    (d_wb,) = _matmul(tag + "_wb_dw", sv["o_b"], d_yb, ta=True, outs=(BF16,))

    def dnorm_bwd(doan, o, z, dn):
        _, vjp = jax.vjp(_dn_outnorm, o, z.astype(F32), dn)
        go, gz, gdn = vjp(doan)
        return (go, gz), (gdn,)

    d_oa, d_pz, d_dn = _rowwise(tag + "_dnorm_bwd", dnorm_bwd, [d_oan, sv["o_a"], sv["pz"]], [sp["dn"]],
                                [(DN_WIDTH, F32), (DN_WIDTH, BF16)], [(1, DN_DIM)])
    d_qkvn, d_gb = _delta_bwd(tag + "_delta_bwd", sv["qkvn"], sv["gb"], sv["states"], sv["tinv"], d_oa)

    def prep_bwd(dq, dgbv, yc, pba, alog, dtb):
        _, vjp = jax.vjp(_dn_prep, yc.astype(F32), pba, alog, dtb)
        gyc, gpba, galog, gdtb = vjp((dq, dgbv))
        return (gyc, gpba), (galog, gdtb)

    d_yc, d_pba, d_alog, d_dtb = _rowwise(tag + "_prep_bwd", prep_bwd, [d_qkvn, d_gb, sv["yc"], sv["pba"]],
                                          [sp["alog"], sp["dtb"]], [(3 * DN_WIDTH, BF16), (LANES, BF16)],
                                          [(1, LANES), (1, LANES)], bm=128)
    d_pq, d_conv = _conv_bwd(tag + "_conv_bwd", d_yc, sv["pq"], sp["conv8"])

    def delta_fn(dob, ob):
        prod = dob.astype(F32) * ob.astype(F32)
        return (lax.dot_general(prod, _head_expand(), NT, precision=HI, preferred_element_type=F32),), ()

    (delta,) = _rowwise(tag + "_da_delta", delta_fn, [d_ob, sv["o_b"]], [], [(LANES, F32)])
    grads = [_da_bwd(f"{tag}_da{r}_bwd", sv["pda"], d_ob, sv["lse"], delta, r) for (_, r) in DA_PATTERNS]

    def sum3(*parts):
        q1, k1, v1, q2, k2, v2, q3, k3, v3 = (p.astype(F32) for p in parts)
        return (jnp.concatenate([q1 + q2 + q3, k1 + k2 + k3, v1 + v2 + v3], axis=1),), ()

    (d_pda,) = _rowwise(tag + "_da_sum", sum3, [t for g in grads for t in g], [], [(3 * DA_WIDTH, BF16)])

    a = sv["a"]
    (da,) = _matmul(tag + "_pq_dx", d_pq, w["wq"], tb=True)
    add = lambda acc, prev: (acc + prev,)
    (da,) = _matmul(tag + "_pz_dx", d_pz, w["wz"], tb=True, epi_rows=[da], epi=add)
    (da,) = _matmul(tag + "_pba_dx", d_pba, w["wba"], tb=True, epi_rows=[da], epi=add)
    (da,) = _matmul(tag + "_pda_dx", d_pda, w["wda"], tb=True, epi_rows=[da], epi=add)
    (da,) = _matmul(tag + "_pg_dx", d_pg, w["wg"], tb=True, epi_rows=[da], epi=add)
    (d_wq,) = _matmul(tag + "_pq_dw", a, d_pq, ta=True, outs=(BF16,))
    (d_wz,) = _matmul(tag + "_pz_dw", a, d_pz, ta=True, outs=(BF16,))
    (d_wba,) = _matmul(tag + "_pba_dw", a, d_pba, ta=True, outs=(BF16,))
    (d_wda,) = _matmul(tag + "_pda_dw", a, d_pda, ta=True, outs=(BF16,))
    (d_wg,) = _matmul(tag + "_pg_dw", a, d_pg, ta=True, outs=(BF16,))
    dh_in, d_ln, d_sh, d_sc = _norm_bwd(tag, h_in, da, dh_out, ln, sh, sc)
    wgrads = dict(wq=d_wq, wz=d_wz, wba=d_wba, wda=d_wda, wg=d_wg, w_a=d_wa, w_b=d_wb, w_o=d_wo)
    small = dict(ln=d_ln, sh=d_sh, sc=d_sc, gt=d_gt, dn=d_dn, alog=d_alog, dtb=d_dtb, conv=d_conv)
    return dh_in, wgrads, small


def _loss_head(h, target, fnorm):
    d = h.shape[1]

    def fn(hv, tv, fw):
        def lossf(hh, ww):
            y = hh * lax.rsqrt(jnp.mean(hh * hh, axis=-1, keepdims=True) + NORM_EPS) * ww
            return 0.5 * jnp.sum(jnp.mean(jnp.square(y - tv), axis=-1))

        val, (dh, dw) = jax.value_and_grad(lossf, argnums=(0, 1))(hv, fw)
        return (dh,), (jnp.full((1, LANES), val, F32), dw)

    return _rowwise("loss_head", fn, [h, target], [fnorm], [(d, F32)], [(1, LANES), (1, d)])


def _row(v):
    return v.reshape(1, -1)


def _pad_lanes(v, offset):
    return jnp.pad(v.reshape(1, -1), ((0, 0), (offset, LANES - offset - v.shape[0])))


_UP_SLOTS = dict(ffn1_wg=0, ffn1_wu=1, ffn2_wg=2, ffn2_wu=3)
_DOWN_SLOTS = dict(ffn1_wd=0, ffn2_wd=1)


def _local_step(x2, target, mod, layer_weights, small, on_layer_grads):
    depth = mod.shape[0]
    d = x2.shape[1]
    h = x2
    saved = []
    mods = []
    up = lambda l, nm: _UP_SLOTS[nm]
    down = lambda l, nm: _DOWN_SLOTS[nm]
    for l in range(depth):
        m9 = [_row(mod[l, i * d:(i + 1) * d]) for i in range(N_ADA)]
        sp = dict(conv8=jnp.pad(small["conv_w"][l], ((0, 8 - DN_CONV), (0, 0))),
                  alog=_pad_lanes(small["a_log"][l], DN_HEADS), dtb=_pad_lanes(small["dt_bias"][l], DN_HEADS),
                  dn=_row(small["dn_norm"][l]))
        ga, gb, w = layer_weights(l, h)
        h0 = h
        h1, sv1 = _ffn_fwd(f"l{l}_ffn1", h0, _row(small["ln_ffn1"][l]), m9[0], m9[1], m9[2], ga, up(l, "ffn1_wg"),
                           up(l, "ffn1_wu"), gb, down(l, "ffn1_wd"), 0.5)
        h2, sv2 = _mixer_fwd(f"l{l}_mix", h1, _row(small["ln_mix"][l]), m9[3], m9[4], m9[5], w, sp)
        h3, sv3 = _ffn_fwd(f"l{l}_ffn2", h2, _row(small["ln_ffn2"][l]), m9[6], m9[7], m9[8], ga, up(l, "ffn2_wg"),
                           up(l, "ffn2_wu"), gb, down(l, "ffn2_wd"), 0.5)
        saved.append((h0, h1, h2, sv1, sv2, sv3, sp, ga, gb, w))
        mods.append(m9)
        h = h3
    dh, loss_part, d_fnorm = _loss_head(h, target, _row(small["final_norm"]))
    sgrads, dmods = [], []
    token = None
    for l in reversed(range(depth)):
        h0, h1, h2, sv1, sv2, sv3, sp, ga, gb, w = saved[l]
        m9 = mods[l] if token is None else [r + token for r in mods[l]]
        dh, g3, s3 = _ffn_bwd(f"l{l}_ffn2", h2, dh, sv3, _row(small["ln_ffn2"][l]), m9[6], m9[7], m9[8], ga,
                              up(l, "ffn2_wg"), up(l, "ffn2_wu"), gb, down(l, "ffn2_wd"), 0.5)
        dh, g2, s2 = _mixer_bwd(f"l{l}_mix", h1, dh, sv2, _row(small["ln_mix"][l]), m9[3], m9[4], m9[5], w, sp)
        dh, g1, s1 = _ffn_bwd(f"l{l}_ffn1", h0, dh, sv1, _row(small["ln_ffn1"][l]), m9[0], m9[1], m9[2], ga,
                              up(l, "ffn1_wg"), up(l, "ffn1_wu"), gb, down(l, "ffn1_wd"), 0.5)
        token = on_layer_grads(l, dict(ffn1_wg=g1["wg"], ffn1_wu=g1["wu"], ffn1_wd=g1["wd"], ffn2_wg=g3["wg"],
                                       ffn2_wu=g3["wu"], ffn2_wd=g3["wd"], **g2))
        dmods.append(jnp.concatenate([s1["sh"], s1["sc"], s1["gt"], s2["sh"], s2["sc"], s2["gt"],
                                      s3["sh"], s3["sc"], s3["gt"]], axis=1))
        sgrads.append(dict(ln_ffn1=s1["ln"][0], ln_mix=s2["ln"][0], ln_ffn2=s3["ln"][0],
                           a_log=s2["alog"][0, DN_HEADS:2 * DN_HEADS], dt_bias=s2["dtb"][0, DN_HEADS:2 * DN_HEADS],
                           dn_norm=s2["dn"][0], conv_w=s2["conv"][:DN_CONV]))
    sgrads.reverse()
    dmods.reverse()
    return loss_part[0, 0], dh, jnp.concatenate(dmods, axis=0), sgrads, d_fnorm[0]


def _flip(v, bit):
    return 1 - v if bit else v


def _allgather8(name, x):
    r, c = x.shape

    def body(x_ref, out_ref, send_sems, recv_sems, local_sem):
        mx, my, mc = lax.axis_index("x"), lax.axis_index("y"), lax.axis_index("c")
        me = 4 * mx + 2 * my + mc
        mine = pltpu.make_async_copy(x_ref, out_ref.at[me], local_sem)
        mine.start()
        sends = []
        for k in range(1, 8):
            peer = (_flip(mx, k & 4), _flip(my, k & 2), _flip(mc, k & 1))
            cp = pltpu.make_async_remote_copy(src_ref=x_ref, dst_ref=out_ref.at[me], send_sem=send_sems.at[k - 1],
                                              recv_sem=recv_sems.at[k - 1], device_id=peer, device_id_type=MESH)
            cp.start()
            sends.append(cp)
        for k in range(1, 8):
            peer = (_flip(mx, k & 4), _flip(my, k & 2), _flip(mc, k & 1))
            src = 4 * peer[0] + 2 * peer[1] + peer[2]
            pltpu.make_async_remote_copy(src_ref=x_ref, dst_ref=out_ref.at[src], send_sem=send_sems.at[k - 1],
                                         recv_sem=recv_sems.at[k - 1], device_id=peer, device_id_type=MESH).wait_recv()
        for cp in sends:
            cp.wait_send()
        mine.wait()

    return pl.pallas_call(
        body, name=name, out_shape=jax.ShapeDtypeStruct((8, r, c), x.dtype),
        in_specs=[pl.BlockSpec(memory_space=pltpu.VMEM)], out_specs=pl.BlockSpec(memory_space=pltpu.VMEM),
        scratch_shapes=[pltpu.SemaphoreType.DMA((7,)), pltpu.SemaphoreType.DMA((7,)), pltpu.SemaphoreType.DMA],
        compiler_params=_params(9 * _nbytes((r, c), x.dtype)),
    )(x)


def _chip_peers(mx, my):
    chips = [(1 - mx, my), (mx, 1 - my), (1 - mx, 1 - my)]
    return chips, [2 * cx + cy for (cx, cy) in chips]


_ANY = pl.BlockSpec(memory_space=pl.ANY)


def _row_half(mc, r):
    return pl.ds(pl.multiple_of(mc * (r // 2), 16), r // 2)


def _gather_groups(name, shards):
    ng = len(shards)

    def body(*refs):
        xs, outs = refs[:ng], refs[ng:2 * ng]
        send_sems, recv_sems = refs[2 * ng:]
        mx, my, mc = lax.axis_index("x"), lax.axis_index("y"), lax.axis_index("c")
        j = 2 * mx + my
        chips, idxs = _chip_peers(mx, my)
        sib = (mx, my, 1 - mc)

        def copy(k, src, dst, to):
            return pltpu.make_async_remote_copy(src_ref=src, dst_ref=dst, send_sem=send_sems.at[k],
                                                recv_sem=recv_sems.at[k], device_id=to, device_id_type=MESH)

        first, passed = [], []
        for g in range(ng):
            mine = _row_half(mc, shards[g].shape[1])
            for t, chip in enumerate(chips):
                cp = copy(6 * g + t, xs[g].at[:, mine], outs[g].at[j, :, mine], (*chip, mc))
                cp.start()
                first.append(cp)
        for g in range(ng):
            mine = _row_half(mc, shards[g].shape[1])
            for t, chip in enumerate(chips):
                landed = outs[g].at[idxs[t], :, mine]
                copy(6 * g + t, landed, landed, (*chip, mc)).wait_recv()
                fwd = copy(6 * g + 3 + t, landed, landed, sib)
                fwd.start()
                passed.append(fwd)
        for g in range(ng):
            theirs_half = _row_half(1 - mc, shards[g].shape[1])
            for t in range(3):
                theirs = outs[g].at[idxs[t], :, theirs_half]
                copy(6 * g + 3 + t, theirs, theirs, sib).wait_recv()
        for cp in first + passed:
            cp.wait_send()

    outs = pl.pallas_call(
        body, name=name, out_shape=[jax.ShapeDtypeStruct((4,) + x.shape, x.dtype) for x in shards],
        in_specs=[_ANY] * ng, out_specs=[_ANY] * ng,
        scratch_shapes=[pltpu.SemaphoreType.DMA((6 * ng,)), pltpu.SemaphoreType.DMA((6 * ng,))],
    )(*shards)
    return _place_own_slab(outs, shards)


def _place_own_slab(outs, shards):
    chip = 2 * lax.axis_index("x") + lax.axis_index("y")
    return [lax.dynamic_update_slice(o, x[None], (chip,) + (0,) * x.ndim) for o, x in zip(outs, shards)]


_HBM = pl.BlockSpec(memory_space=pltpu.HBM)
_SEM = pl.BlockSpec(memory_space=pltpu.SEMAPHORE)
_DATAFLOW = pltpu.SideEffectType.DATAFLOW_SIDE_EFFECTING


def _ici_gather_copies(src_refs, land_refs, send_sems, recv_sems, scatter=False):
    mx, my, mc = lax.axis_index("x"), lax.axis_index("y"), lax.axis_index("c")
    j = 2 * mx + my
    chips, idxs = _chip_peers(mx, my)
    sends, recvs = [], []
    for g, src in enumerate(src_refs):
        for t, chip in enumerate(chips):
            common = dict(send_sem=send_sems.at[3 * g + t], recv_sem=recv_sems.at[3 * g + t], device_id=(*chip, mc),
                          device_id_type=MESH)
            if scatter:
                out, to, frm = src.at[idxs[t]], land_refs[g].at[j], land_refs[g].at[idxs[t]]
            else:
                mine = _row_half(mc, src.shape[1])
                out, to, frm = src.at[:, mine], land_refs[g].at[j, :, mine], land_refs[g].at[idxs[t], :, mine]
            sends.append(pltpu.make_async_remote_copy(src_ref=out, dst_ref=to, **common))
            recvs.append(pltpu.make_async_remote_copy(src_ref=out, dst_ref=frm, **common))
    return sends, recvs


def _gather_start(name, shards, scatter=False):
    ng = len(shards)

    def body(*refs):
        srcs, lands = refs[:ng], refs[ng:2 * ng]
        send_sems, recv_sems = refs[2 * ng], refs[2 * ng + 1]
        token = refs[-1]
        sends, _ = _ici_gather_copies(srcs, lands, send_sems, recv_sems, scatter)
        for cp in sends:
            cp.start()
        token[...] = jnp.zeros(token.shape, token.dtype)

    land_shape = lambda x: x.shape if scatter else (4,) + x.shape
    srcs = [pltpu.with_memory_space_constraint(x, pltpu.HBM) for x in shards]
    lands = [pltpu.with_memory_space_constraint(lax.empty(land_shape(x), x.dtype), pltpu.HBM) for x in shards]
    res = pl.pallas_call(
        body, name=name,
        out_shape=(pltpu.SemaphoreType.DMA((3 * ng,)), pltpu.SemaphoreType.DMA((3 * ng,)),
                   *[pltpu.HBM(x.shape, x.dtype) for x in srcs], *[pltpu.HBM(x.shape, x.dtype) for x in lands],
                   jax.ShapeDtypeStruct((8, LANES), F32)),
        in_specs=[_HBM] * (2 * ng),
        out_specs=(_SEM, _SEM, *[_HBM] * (2 * ng), pl.BlockSpec(memory_space=pltpu.VMEM)),
        input_output_aliases={i: 2 + i for i in range(2 * ng)},
        compiler_params=pltpu.CompilerParams(has_side_effects=_DATAFLOW),
    )(*srcs, *lands)
    return dict(send_sems=res[0], recv_sems=res[1], srcs=list(res[2:2 + ng]), lands=list(res[2 + ng:2 + 2 * ng]),
                token=res[-1])


def _gather_wait(name, started, after, scatter=False):
    ng = len(started["srcs"])

    def body(*refs):
        srcs, lands = refs[:ng], refs[ng:2 * ng]
        send_sems, recv_sems = refs[2 * ng], refs[2 * ng + 1]
        sends, recvs = _ici_gather_copies(srcs, lands, send_sems, recv_sems, scatter)
        for cp in sends:
            cp.wait_send()
        for cp in recvs:
            cp.wait_recv()

    res = pl.pallas_call(
        body, name=name,
        out_shape=[pltpu.HBM(x.shape, x.dtype) for x in started["srcs"] + started["lands"]],
        in_specs=[_HBM] * (2 * ng) + [_SEM, _SEM, _ANY], out_specs=[_HBM] * (2 * ng),
        input_output_aliases={i: i for i in range(2 * ng)},
        compiler_params=pltpu.CompilerParams(has_side_effects=_DATAFLOW),
    )(*started["srcs"], *started["lands"], started["send_sems"], started["recv_sems"], after)
    return list(res[:ng]), list(res[ng:])


def _pair_forward_groups(name, lands, shards):
    ng = len(lands)

    def body(*refs):
        ins, outs = refs[:ng], refs[ng:2 * ng]
        send_sems, recv_sems = refs[2 * ng:]
        mx, my, mc = lax.axis_index("x"), lax.axis_index("y"), lax.axis_index("c")
        _, idxs = _chip_peers(mx, my)
        sib = (mx, my, 1 - mc)
        cps = []
        for g in range(ng):
            mine = _row_half(mc, lands[g].shape[2])
            for t in range(3):
                cp = pltpu.make_async_remote_copy(src_ref=ins[g].at[idxs[t], :, mine], dst_ref=outs[g].at[idxs[t], :, mine],
                                                  send_sem=send_sems.at[3 * g + t], recv_sem=recv_sems.at[3 * g + t],
                                                  device_id=sib, device_id_type=MESH)
                cp.start()
                cps.append(cp)
        for g in range(ng):
            theirs = _row_half(1 - mc, lands[g].shape[2])
            for t in range(3):
                pltpu.make_async_remote_copy(src_ref=ins[g].at[idxs[t], :, theirs], dst_ref=outs[g].at[idxs[t], :, theirs],
                                             send_sem=send_sems.at[3 * g + t], recv_sem=recv_sems.at[3 * g + t],
                                             device_id=sib, device_id_type=MESH).wait_recv()
        for cp in cps:
            cp.wait_send()

    outs = pl.pallas_call(
        body, name=name, out_shape=[jax.ShapeDtypeStruct(x.shape, x.dtype) for x in lands],
        in_specs=[_ANY] * ng, out_specs=[_ANY] * ng, input_output_aliases={i: i for i in range(ng)},
        scratch_shapes=[pltpu.SemaphoreType.DMA((3 * ng,)), pltpu.SemaphoreType.DMA((3 * ng,))],
    )(*lands)
    return _place_own_slab(outs, shards)


def _pair_swap_groups(name, gs):
    ng = len(gs)

    def body(*refs):
        xs, outs = refs[:ng], refs[ng:2 * ng]
        send_sems, recv_sems = refs[2 * ng:]
        mx, my, mc = lax.axis_index("x"), lax.axis_index("y"), lax.axis_index("c")
        cps = []
        for g in range(ng):
            cp = pltpu.make_async_remote_copy(src_ref=xs[g].at[:, :, _row_half(1 - mc, gs[g].shape[2])], dst_ref=outs[g],
                                              send_sem=send_sems.at[g], recv_sem=recv_sems.at[g],
                                              device_id=(mx, my, 1 - mc), device_id_type=MESH)
            cp.start()
            cps.append(cp)
        for cp in cps:
            cp.wait()

    return pl.pallas_call(
        body, name=name,
        out_shape=[jax.ShapeDtypeStruct(x.shape[:2] + (x.shape[2] // 2, x.shape[3]), x.dtype) for x in gs],
        in_specs=[_ANY] * ng, out_specs=[_ANY] * ng,
        scratch_shapes=[pltpu.SemaphoreType.DMA((ng,)), pltpu.SemaphoreType.DMA((ng,))],
    )(*gs)


def _chip_scatter_groups(name, ps):
    ng = len(ps)

    def body(*refs):
        xs, outs = refs[:ng], refs[ng:2 * ng]
        send_sems, recv_sems = refs[2 * ng:]
        mx, my, mc = lax.axis_index("x"), lax.axis_index("y"), lax.axis_index("c")
        j = 2 * mx + my
        chips, idxs = _chip_peers(mx, my)
        sends = []
        for g in range(ng):
            for t, chip in enumerate(chips):
                cp = pltpu.make_async_remote_copy(src_ref=xs[g].at[idxs[t]], dst_ref=outs[g].at[j],
                                                  send_sem=send_sems.at[3 * g + t], recv_sem=recv_sems.at[3 * g + t],
                                                  device_id=(*chip, mc), device_id_type=MESH)
                cp.start()
                sends.append(cp)
        for g in range(ng):
            for t, chip in enumerate(chips):
                pltpu.make_async_remote_copy(src_ref=xs[g].at[idxs[t]], dst_ref=outs[g].at[idxs[t]],
                                             send_sem=send_sems.at[3 * g + t], recv_sem=recv_sems.at[3 * g + t],
                                             device_id=(*chip, mc), device_id_type=MESH).wait_recv()
        for cp in sends:
            cp.wait_send()

    outs = pl.pallas_call(
        body, name=name, out_shape=[jax.ShapeDtypeStruct(x.shape, x.dtype) for x in ps],
        in_specs=[_ANY] * ng, out_specs=[_ANY] * ng,
        scratch_shapes=[pltpu.SemaphoreType.DMA((3 * ng,)), pltpu.SemaphoreType.DMA((3 * ng,))],
    )(*ps)
    return _place_own_part(outs, ps)


def _place_own_part(outs, ps):
    chip = 2 * lax.axis_index("x") + lax.axis_index("y")
    return [lax.dynamic_update_slice(o, lax.dynamic_index_in_dim(x, chip, 0, keepdims=True), (chip,) + (0,) * (x.ndim - 1))
            for o, x in zip(outs, ps)]


def _pair_merge_groups(name, fs):
    ng = len(fs)

    def body(*refs):
        xs, outs = refs[:ng], refs[ng:2 * ng]
        send_sems, recv_sems = refs[2 * ng:]
        mx, my, mc = lax.axis_index("x"), lax.axis_index("y"), lax.axis_index("c")
        cps = []
        for g in range(ng):
            mine = _row_half(mc, 2 * fs[g].shape[1])
            cp = pltpu.make_async_remote_copy(src_ref=xs[g], dst_ref=outs[g].at[:, mine], send_sem=send_sems.at[g],
                                              recv_sem=recv_sems.at[g], device_id=(mx, my, 1 - mc), device_id_type=MESH)
            cp.start()
            cps.append(cp)
        for g in range(ng):
            theirs = outs[g].at[:, _row_half(1 - mc, 2 * fs[g].shape[1])]
            pltpu.make_async_remote_copy(src_ref=xs[g], dst_ref=theirs, send_sem=send_sems.at[g],
                                         recv_sem=recv_sems.at[g], device_id=(mx, my, 1 - mc),
                                         device_id_type=MESH).wait_recv()
        for cp in cps:
            cp.wait_send()

    outs = pl.pallas_call(
        body, name=name,
        out_shape=[jax.ShapeDtypeStruct((x.shape[0], 2 * x.shape[1], x.shape[2]), x.dtype) for x in fs],
        in_specs=[_ANY] * ng, out_specs=[_ANY] * ng,
        scratch_shapes=[pltpu.SemaphoreType.DMA((ng,)), pltpu.SemaphoreType.DMA((ng,))],
    )(*fs)
    mc = lax.axis_index("c")
    return [lax.dynamic_update_slice(o, x, (0, mc * x.shape[1], 0)) for o, x in zip(outs, fs)]


def _block_rows(r, w, itemsize=4, budget=4 << 20):
    for c in (r, 2048, 1024, 512, 256, 128, 64, 32, 16):
        if c <= r and r % c == 0 and c * w * itemsize <= budget:
            return c
    return r


def _pair_sum(name, g, got, cidx):
    ns, t, r, w = g.shape
    rh = r // 2
    bm = _block_rows(rh, w)
    nb = rh // bm

    def body(c_ref, a_ref, b_ref, o_ref):
        o_ref[...] = (a_ref[...].astype(F32) + b_ref[...].astype(F32)).astype(o_ref.dtype)

    blk = (None, None, bm, w)
    return pl.pallas_call(
        body, name=name,
        grid_spec=pltpu.PrefetchScalarGridSpec(
            num_scalar_prefetch=1, grid=(ns, t, nb),
            in_specs=[pl.BlockSpec(blk, lambda s, tt, i, c: (s, tt, c[0] * nb + i, 0)),
                      pl.BlockSpec(blk, lambda s, tt, i, c: (s, tt, i, 0))],
            out_specs=pl.BlockSpec(blk, lambda s, tt, i, c: (s, tt, i, 0))),
        out_shape=jax.ShapeDtypeStruct((ns, t, rh, w), BF16),
        compiler_params=_params(3 * _nbytes((bm, w), F32)),
    )(cidx, g, got)


def _chip_sum(name, p):
    ns, th, r, w = p.shape
    bm = _block_rows(r, w, budget=2 << 20)

    def body(p_ref, o_ref):
        acc = p_ref[0].astype(F32)
        for s in range(1, ns):
            acc = acc + p_ref[s].astype(F32)
        o_ref[...] = acc

    return pl.pallas_call(
        body, name=name, grid=(th, r // bm),
        in_specs=[pl.BlockSpec((ns, None, bm, w), lambda tt, i: (0, tt, i, 0))],
        out_specs=pl.BlockSpec((None, bm, w), lambda tt, i: (tt, i, 0)),
        out_shape=jax.ShapeDtypeStruct((th, r, w), F32),
        compiler_params=_params(ns * _nbytes((bm, w), BF16) + 2 * _nbytes((bm, w), F32)),
    )(p)


def _sum_leading(name, x):
    n = x.shape[0]

    def body(p_ref, o_ref):
        acc = p_ref[0]
        for s in range(1, n):
            acc = acc + p_ref[s]
        o_ref[...] = acc

    return pl.pallas_call(body, name=name, out_shape=jax.ShapeDtypeStruct(x.shape[1:], F32),
                          compiler_params=_params(2 * _nbytes(x.shape, F32)))(x)


def _reduce_scatter_begin(tag, gs, overlap):
    cidx = lax.axis_index("c").astype(jnp.int32).reshape(1)
    got = _pair_swap_groups(tag + "_pair_swap", gs)
    pair = [_pair_sum(f"{tag}_pair_sum{i}", g, r_, cidx) for i, (g, r_) in enumerate(zip(gs, got))]
    if overlap:
        return _gather_start(tag + "_scatter_start", pair, scatter=True)
    return _chip_scatter_groups(tag + "_chip_scatter", pair)


def _reduce_scatter_end(tag, state, overlap, after):
    if overlap:
        srcs, lands = _gather_wait(tag + "_scatter_wait", state, after, scatter=True)
        state = _place_own_part(lands, srcs)
    fin = [_chip_sum(f"{tag}_chip_sum{i}", p) for i, p in enumerate(state)]
    return _pair_merge_groups(tag + "_pair_merge", fin)


_GROUPS = ((("ffn1_wg", "ffn1_wu", "ffn2_wg", "ffn2_wu"), 1), (("ffn1_wd", "ffn2_wd"), 0), (("w_a",), 0),
           (("w_o",), 0), (("w_in",), 1), (("w_b",), 1))


def _shard_major(g, ax):
    k, n = g.shape
    if ax == 0:
        return g.reshape(4, k // 4, n)
    return g.reshape(k, 4, n // 4).transpose(1, 0, 2)


def _in_cols(d):
    o1 = 3 * DN_WIDTH
    o2 = o1 + DN_WIDTH
    o3 = o2 + 2 * DN_HEADS
    o4 = o3 + 3 * DA_WIDTH
    return dict(wq=(0, o1), wz=(o1, o2), wba=(o2, o3), wda=(o3, o4), wg=(o4, o4 + 2 * d))


def _mixer_weights(w_in, w_a, w_b, w_o, d):
    w = {k: w_in[:, a:b] for k, (a, b) in _in_cols(d).items()}
    w["wba"] = jnp.pad(w["wba"], ((0, 0), (0, LANES - 2 * DN_HEADS)))
    w["w_a"], w["w_b"], w["w_o"] = w_a, w_b, w_o
    return w


def _w_in_grad(wg):
    return jnp.concatenate([wg["wq"], wg["wz"], wg["wba"][:, :2 * DN_HEADS], wg["wda"], wg["wg"]], axis=1)


def _adam_math(wv, gv, mv, vv):
    mn = ADAM_B1 * mv + (1.0 - ADAM_B1) * gv
    vn = ADAM_B2 * vv + (1.0 - ADAM_B2) * jnp.square(gv)
    m_hat = mn / (1.0 - ADAM_B1 ** ADAM_STEP)
    v_hat = vn / (1.0 - ADAM_B2 ** ADAM_STEP)
    delta = -ADAM_LR * (m_hat / (jnp.sqrt(v_hat) + ADAM_EPS) + ADAM_WD * wv)
    return delta, mn, vn


def _adamw(name, w, g, m, v):
    shape = w.shape
    cols = shape[-1]
    w2, g2, m2, v2 = (t.reshape(-1, cols) for t in (w, g, m, v))
    rows = w2.shape[0]
    bm = _pick(rows, (256, 128, 64, 32, 16, 8)) if rows >= 8 else rows
    delta, mn, vn = _rowwise(name, lambda *t: (_adam_math(*t), ()), [w2, g2, m2, v2], [], [(cols, F32)] * 3, bm=bm)
    return delta.reshape(shape), mn.reshape(shape), vn.reshape(shape)


def _adamw_leading(name, w, g, m, v):
    n = w.shape[0]
    padded_row = -(-w.shape[1] // 8) * 8 * w.shape[2] * 4
    bm = max(c for c in range(1, n + 1) if n % c == 0 and (c * padded_row <= (1 << 20) or c == 1))

    def body(w_ref, g_ref, m_ref, v_ref, d_ref, mo_ref, vo_ref):
        d_ref[...], mo_ref[...], vo_ref[...] = _adam_math(w_ref[...], g_ref[...], m_ref[...], v_ref[...])

    spec = pl.BlockSpec((bm,) + w.shape[1:], lambda i: (i, 0, 0))
    return pl.pallas_call(
        body, name=name, grid=(n // bm,), in_specs=[spec] * 4, out_specs=[spec] * 3,
        out_shape=[jax.ShapeDtypeStruct(w.shape, F32)] * 3, compiler_params=_params(7 * bm * padded_row),
    )(w, g, m, v)


def _adamw_stacked(name, w, m, v, gstacks, slot):
    depth, r, cdim = w.shape
    bm = _block_rows(r, cdim, budget=1 << 20)

    def body(w_ref, m_ref, v_ref, *rest):
        g_refs, (go_ref, d_ref, mo_ref, vo_ref) = rest[:depth], rest[depth:]
        layer = pl.program_id(0)
        gv = g_refs[0][...]
        for l in range(1, depth):
            gv = jnp.where(layer == l, g_refs[l][...], gv)
        go_ref[...] = gv
        d_ref[...], mo_ref[...], vo_ref[...] = _adam_math(w_ref[...], gv, m_ref[...], v_ref[...])

    nat = pl.BlockSpec((None, bm, cdim), lambda l, i: (l, i, 0))
    return pl.pallas_call(
        body, name=name, grid=(depth, r // bm),
        in_specs=[nat, nat, nat] + [pl.BlockSpec((None, bm, cdim), lambda l, i: (slot, i, 0))] * depth,
        out_specs=[nat] * 4, out_shape=[jax.ShapeDtypeStruct(w.shape, F32)] * 4,
        compiler_params=_params((7 + depth) * _nbytes((bm, cdim), F32)),
    )(w, m, v, *gstacks)


def kernel(x, c, ada_w, ada_b, ln_ffn1, ln_mix, ln_ffn2, ffn1_wg, ffn1_wu, ffn1_wd, w_in, conv_w, a_log, dt_bias, dn_norm, w_a, w_b, w_o, ffn2_wg, ffn2_wu, ffn2_wd, final_norm, loss_target, m_ada_w, m_ada_b, m_ln_ffn1, m_ln_mix, m_ln_ffn2, m_ffn1_wg, m_ffn1_wu, m_ffn1_wd, m_w_in, m_conv_w, m_a_log, m_dt_bias, m_dn_norm, m_w_a, m_w_b, m_w_o, m_ffn2_wg, m_ffn2_wu, m_ffn2_wd, m_final_norm, v_ada_w, v_ada_b, v_ln_ffn1, v_ln_mix, v_ln_ffn2, v_ffn1_wg, v_ffn1_wu, v_ffn1_wd, v_w_in, v_conv_w, v_a_log, v_dt_bias, v_dn_norm, v_w_a, v_w_b, v_w_o, v_ffn2_wg, v_ffn2_wu, v_ffn2_wd, v_final_norm):
    names = ["ada_w", "ada_b", "ln_ffn1", "ln_mix", "ln_ffn2", "ffn1_wg", "ffn1_wu", "ffn1_wd", "w_in", "conv_w",
             "a_log", "dt_bias", "dn_norm", "w_a", "w_b", "w_o", "ffn2_wg", "ffn2_wu", "ffn2_wd", "final_norm"]
    wts = dict(zip(names, (ada_w, ada_b, ln_ffn1, ln_mix, ln_ffn2, ffn1_wg, ffn1_wu, ffn1_wd, w_in, conv_w, a_log,
                           dt_bias, dn_norm, w_a, w_b, w_o, ffn2_wg, ffn2_wu, ffn2_wd, final_norm)))
    mom = dict(zip(names, (m_ada_w, m_ada_b, m_ln_ffn1, m_ln_mix, m_ln_ffn2, m_ffn1_wg, m_ffn1_wu, m_ffn1_wd, m_w_in,
                           m_conv_w, m_a_log, m_dt_bias, m_dn_norm, m_w_a, m_w_b, m_w_o, m_ffn2_wg, m_ffn2_wu,
                           m_ffn2_wd, m_final_norm)))
    var = dict(zip(names, (v_ada_w, v_ada_b, v_ln_ffn1, v_ln_mix, v_ln_ffn2, v_ffn1_wg, v_ffn1_wu, v_ffn1_wd, v_w_in,
                           v_conv_w, v_a_log, v_dt_bias, v_dn_norm, v_w_a, v_w_b, v_w_o, v_ffn2_wg, v_ffn2_wu,
                           v_ffn2_wd, v_final_norm)))
    _, s, d = x.shape
    depth = ada_w.shape[0]
    mx, my, mc = lax.axis_index("x"), lax.axis_index("y"), lax.axis_index("c")
    chip = 2 * mx + my
    me = 2 * chip + mc
    nshard = ada_w.shape[2]

    cact = _rowwise("c_silu", lambda cv: ((_silu(cv),), ()), [jnp.pad(c, ((0, 7), (0, 0)))], [], [(d, F32)], bm=8)[0]
    c_all = _allgather8("ag_c", cact)[:, 0, :]
    conv_all = _allgather8("ag_conv", jnp.pad(conv_w.reshape(depth * DN_CONV, -1), ((0, 8 - depth * DN_CONV), (0, 0))))
    conv_full = jnp.concatenate([conv_all[2 * j, :depth * DN_CONV] for j in range(4)], axis=1)
    conv_full = conv_full.reshape(depth, DN_CONV, 3 * DN_WIDTH)
    layer_shards = [[jnp.stack([wts[nm][l].astype(BF16) for nm in nms], axis=0) for nms, _ in _GROUPS]
                    for l in range(depth)]
    gathered0 = _gather_groups("ag_weights0", layer_shards[0])
    rows_of = lambda st: st[:, 0].reshape(-1, st.shape[-1])
    cols_of = lambda st: jnp.concatenate([st[j, 0] for j in range(4)], axis=1)

    def layer_weights(l, after):
        if l == 0:
            got = gathered0
        else:
            srcs, lands = _gather_wait(f"ag_weights{l}_wait", started[l], after)
            got = _pair_forward_groups(f"ag_weights{l}_pair", lands, srcs)
        ga, gb, g_wa, g_wo, g_win, g_wb = got
        return ga, gb, _mixer_weights(cols_of(g_win), rows_of(g_wa), cols_of(g_wb), rows_of(g_wo), d)

    c16 = jnp.pad(c_all, ((0, 8), (0, 0))).astype(BF16)
    parts = []
    for l in range(depth):
        bias = lax.dynamic_slice(ada_b[l], (chip * nshard,), (nshard,)).reshape(1, nshard)
        (mp,) = _matmul(f"ada_fwd{l}", c16, ada_w[l].astype(BF16), epi_bcast=[bias], epi=lambda acc, b: (acc + b,))
        parts.append(mp)
    mod_all = _allgather8("ag_mod", jnp.concatenate(parts, axis=0))
    mod_rows = jnp.concatenate([mod_all[2 * j] for j in range(4)], axis=1)
    mod = jnp.stack([lax.dynamic_index_in_dim(mod_rows, l * 16 + me, axis=0, keepdims=False) for l in range(depth)])

    gathered0, later, mod, conv_full = lax.optimization_barrier((gathered0, layer_shards[1:], mod, conv_full))
    started = {l: _gather_start(f"ag_weights{l}_start", later[l - 1]) for l in range(1, depth)}
    for st in started.values():
        mod = mod + st["token"][0, 0]
    small = dict(conv_w=conv_full, a_log=a_log, dt_bias=dt_bias, dn_norm=dn_norm, ln_ffn1=ln_ffn1, ln_mix=ln_mix,
                 ln_ffn2=ln_ffn2, final_norm=final_norm)
    ffn_names = _GROUPS[0][0] + _GROUPS[1][0]
    rs_state, first_layer = {}, {}

    def on_layer_grads(l, wg):
        wg["w_in"] = _w_in_grad(wg)
        gs = [jnp.stack([wg[nm] if nm in ffn_names else _shard_major(wg[nm], ax) for nm in nms], axis=1)
              for nms, ax in _GROUPS]
        if l == 0:
            first_layer["gs"] = gs
            return None
        rs_state[l] = _reduce_scatter_begin(f"rs{l}", gs, overlap=True)
        return rs_state[l]["token"][0, 0]

    loss_part, dx, dmod, sgrads, d_fnorm = _local_step(x[0], loss_target[0], mod, layer_weights, small,
                                                       on_layer_grads)

    dmod_all = _allgather8("ag_dmod", jnp.pad(dmod, ((0, 8 - depth), (0, 0))))
    smalls = [loss_part.reshape(1), d_fnorm]
    for l in range(depth):
        sg = sgrads[l]
        smalls += [sg["ln_ffn1"], sg["ln_mix"], sg["ln_ffn2"], sg["a_log"], sg["dt_bias"], sg["dn_norm"],
                   sg["conv_w"].reshape(-1)]
    sizes = [t.shape[0] for t in smalls]
    tile = 8 * LANES
    flat = jnp.concatenate([jnp.pad(t, (0, (-t.shape[0]) % tile)).reshape(-1, LANES) for t in smalls], axis=0)
    small_all = _allgather8("ag_small", flat)
    dmod_all, small_all, gs0 = lax.optimization_barrier((dmod_all, small_all, first_layer["gs"]))
    rs_state[0] = _reduce_scatter_begin("rs0", gs0, overlap=True)
    started0 = rs_state[0]["token"][0, 0]
    dmod_all = dmod_all + started0
    small_all = small_all + started0

    g_ada_w, g_ada_b = [], []
    for l in range(depth):
        dm_l = dmod_all[:, l, :]
        (gb_l,) = _rowwise(f"ada_b_grad{l}", lambda v: ((), (jnp.sum(v, axis=0, keepdims=True),)), [dm_l], [], [],
                           [(1, N_ADA * d)], bm=8)
        g_ada_b.append(gb_l[0])
        dm_sh = lax.dynamic_slice(dm_l, (0, chip * nshard), (8, nshard))
        (gw_l,) = _matmul(f"ada_w_grad{l}", c16, jnp.pad(dm_sh, ((0, 8), (0, 0))).astype(BF16), ta=True)
        g_ada_w.append(gw_l)
    grads = dict(ada_w=jnp.stack(g_ada_w), ada_b=jnp.stack(g_ada_b))

    tot = _sum_leading("small_sum", small_all)
    offs, acc = [], 0
    for n_ in sizes:
        offs.append(acc)
        acc += -(-n_ // tile) * 8
    take = lambda i: tot[offs[i]:offs[i] + -(-sizes[i] // tile) * 8].reshape(-1)[:sizes[i]]
    loss = take(0)[0]
    grads["final_norm"] = take(1)
    per = 7
    for key_i, key in enumerate(["ln_ffn1", "ln_mix", "ln_ffn2", "a_log", "dt_bias", "dn_norm"]):
        grads[key] = jnp.stack([take(2 + per * l + key_i) for l in range(depth)])
    conv_g = jnp.stack([take(2 + per * l + 6).reshape(DN_CONV, 3 * DN_WIDTH) for l in range(depth)])
    csh = conv_w.shape[2]
    grads["conv_w"] = lax.dynamic_slice(conv_g, (0, 0, chip * csh), (depth, DN_CONV, csh))

    deltas, new_m, new_v = {}, {}, {}
    big = {nm for nms, _ in _GROUPS for nm in nms}
    for name in names:
        if name in big:
            continue
        wv, gv, mv, vv = wts[name], grads[name], mom[name], var[name]
        if wv.ndim == 1:
            wv, gv, mv, vv = (t.reshape(-1, LANES) for t in (wv, gv, mv, vv))
        dl, mn, vn = _adamw("adamw_" + name, wv, gv, mv, vv)
        deltas[name], new_m[name], new_v[name] = (t.reshape(wts[name].shape) for t in (dl, mn, vn))

    reduced = {l: _reduce_scatter_end(f"rs{l}", rs_state[l], True, dx) for l in range(depth - 1, 0, -1)}
    reduced[0] = _reduce_scatter_end("rs0", rs_state[0], True, deltas["ada_w"])

    for gi, (nms, ax) in enumerate(_GROUPS):
        for q, nm in enumerate(nms):
            wv, mv, vv = wts[nm], mom[nm], var[nm]
            per_layer = [reduced[l][gi][q] for l in range(depth)]
            if ax == 1 and wv.shape[2] % LANES and nm != "w_in":
                tr = lambda t: jnp.swapaxes(t, 1, 2)
                gt = jnp.stack([g.T for g in per_layer], axis=0)
                dl, mn, vn = _adamw("adamw_" + nm, tr(wv), gt, tr(mv), tr(vv))
                grads[nm], deltas[nm], new_m[nm], new_v[nm] = tr(gt), tr(dl), tr(mn), tr(vn)
            elif nm == "w_in" and wv.shape[2] % LANES:
                tr = lambda t: jnp.transpose(t, (2, 0, 1))
                back = lambda t: jnp.transpose(t, (1, 2, 0))
                gt = jnp.stack([g.T for g in per_layer], axis=1)
                dl, mn, vn = _adamw_leading("adamw_" + nm, tr(wv), gt, tr(mv), tr(vv))
                grads[nm], deltas[nm], new_m[nm], new_v[nm] = back(gt), back(dl), back(mn), back(vn)
            else:
                grads[nm], deltas[nm], new_m[nm], new_v[nm] = _adamw_stacked(
                    "adamw_" + nm, wv, mv, vv, [reduced[l][gi] for l in range(depth)], q)

    return (loss, dx.reshape(1, s, d), *[grads[n_] for n_ in names], *[deltas[n_] for n_ in names],
            *[new_m[n_] for n_ in names], *[new_v[n_] for n_ in names])
```
